```python
import jax, jax.numpy as jnp
from jax import lax
import numpy as np

D_MODEL = 1024
BATCH = 8
SEQ = 4096
DEPTH = 2

N_EVEN = (DEPTH + 1) // 2
N_ODD = DEPTH // 2

LRU_WIDTH = D_MODEL // 2
LRU_HEADS = 8
LRU_HEAD_DIM = LRU_WIDTH // LRU_HEADS
LRU_CONV = 4
LRU_C = 8.0
LRU_MIN_RAD = 0.9
LRU_MAX_RAD = 0.999
SC_WIDTH = D_MODEL // 2
SC_CONV = 3
EVEN_IN = 2 * LRU_WIDTH + 3 * SC_WIDTH

SGU_WIDTH = D_MODEL // 2
SGU_HEADS = 8
SGU_HEAD_DIM = SGU_WIDTH // SGU_HEADS
CHUNK = 128
FOX_HEADS = 8
FOX_HEAD_DIM = 64
FOX_WIDTH = FOX_HEADS * FOX_HEAD_DIM
Q_BLOCK = 128
ODD_IN = 2 * SGU_WIDTH + 3 * FOX_WIDTH + FOX_HEADS

D_FF = 2816
FFN_CONV = 3
EPS = 1e-6

kernel_name = "hybrid_rglru_shortconv_sgu_fox_block"


def rmsnorm(x, g):
    xf = x.astype(jnp.float32)
    y = xf * lax.rsqrt(jnp.mean(xf * xf, axis=-1, keepdims=True) + EPS)
    return (y * g.astype(jnp.float32)).astype(x.dtype)


def causal_depthwise_conv(x, w, b):
    k_width, ch = w.shape
    out = lax.conv_general_dilated(
        x, w[:, None, :].astype(x.dtype), window_strides=(1,),
        padding=[(k_width - 1, 0)], dimension_numbers=('NWC', 'WIO', 'NWC'),
        feature_group_count=ch)
    return out + b.astype(x.dtype)


def rg_lru(x, w_a, b_a, w_x, b_x, lam):
    bsz, s, w = x.shape
    xh = x.reshape(bsz, s, LRU_HEADS, LRU_HEAD_DIM)
    r = jax.nn.sigmoid((jnp.einsum('bshi,hij->bshj', xh, w_a).reshape(bsz, s, w) + b_a).astype(jnp.float32))
    i = jax.nn.sigmoid((jnp.einsum('bshi,hij->bshj', xh, w_x).reshape(bsz, s, w) + b_x).astype(jnp.float32))
    log_a = -LRU_C * r * jax.nn.softplus(-lam.astype(jnp.float32))
    a = jnp.exp(log_a)
    u = jnp.sqrt(-jnp.expm1(2.0 * log_a)) * (i * x.astype(jnp.float32))

    def combine(left, right):
        a_l, h_l = left
        a_r, h_r = right
        return a_l * a_r, a_r * h_l + h_r

    _, h = lax.associative_scan(combine, (a, u), axis=1)
    return h.astype(x.dtype)


def even_mixer(x, w_in, lru_conv_w, lru_conv_b, lru_wa, lru_ba, lru_wx, lru_bx, lru_lambda,
               sconv_w, sconv_b, w_out):
    p = x @ w_in
    W, C = LRU_WIDTH, SC_WIDTH
    xa, ga, c_pre, b_post, vb = jnp.split(p, [W, 2 * W, 2 * W + C, 2 * W + 2 * C], axis=-1)
    xa = causal_depthwise_conv(xa, lru_conv_w, lru_conv_b)
    ya = rg_lru(xa, lru_wa, lru_ba, lru_wx, lru_bx, lru_lambda) * jax.nn.gelu(ga)
    yb = b_post * causal_depthwise_conv(c_pre * vb, sconv_w, sconv_b)
    return jnp.concatenate([ya, yb], axis=-1) @ w_out


def chunked_spatial_gating(u, g, g_norm, w_s, b_s):
    bsz, s, _ = u.shape
    n_chunks = s // CHUNK
    gv = rmsnorm(g.reshape(bsz, s, SGU_HEADS, SGU_HEAD_DIM), g_norm.reshape(SGU_HEADS, SGU_HEAD_DIM))
    gv = gv.reshape(bsz, n_chunks, CHUNK, SGU_HEADS, SGU_HEAD_DIM)
    w_causal = jnp.tril(w_s)
    mixed = jnp.einsum('gts,bnsgc->bntgc', w_causal, gv) + b_s.T[:, :, None]
    return u * mixed.reshape(bsz, s, SGU_WIDTH)


def forgetting_attention(q, k, v, f_logit, b_f):
    bsz, s, _ = q.shape
    nb = s // Q_BLOCK

    def heads(t):
        return t.reshape(bsz, s, FOX_HEADS, FOX_HEAD_DIM).transpose(0, 2, 1, 3)

    q, k, v = heads(q), heads(k), heads(v)
    log_f = jax.nn.log_sigmoid(f_logit.astype(jnp.float32) + b_f.astype(jnp.float32))
    c = jnp.cumsum(log_f, axis=1).transpose(0, 2, 1)
    q_blocks = q.reshape(bsz, FOX_HEADS, nb, Q_BLOCK, FOX_HEAD_DIM).transpose(2, 0, 1, 3, 4)
    c_blocks = c.reshape(bsz, FOX_HEADS, nb, Q_BLOCK).transpose(2, 0, 1, 3)
    starts = jnp.arange(nb) * Q_BLOCK
    key_pos = jnp.arange(s)
    scale = FOX_HEAD_DIM ** -0.5

    def block(args):
        qb, cb, start = args
        logits = (jnp.einsum('bhqd,bhkd->bhqk', qb, k).astype(jnp.float32) * scale
                  + cb[..., None] - c[:, :, None, :])
        q_pos = start + jnp.arange(Q_BLOCK)
        logits = jnp.where(key_pos[None, :] <= q_pos[:, None], logits, -jnp.inf)
        p = jax.nn.softmax(logits, axis=-1)
        return jnp.einsum('bhqk,bhkd->bhqd', p.astype(v.dtype), v)

    out = lax.map(block, (q_blocks, c_blocks, starts))
    return out.transpose(1, 0, 3, 2, 4).reshape(bsz, s, FOX_WIDTH)


def odd_mixer(x, w_in, sgu_norm, sgu_w, sgu_b, fox_bf, w_out):
    p = x @ w_in
    Z, F = 2 * SGU_WIDTH, FOX_WIDTH
    z, q, k, v, f = jnp.split(p, [Z, Z + F, Z + 2 * F, Z + 3 * F], axis=-1)
    z = jax.nn.gelu(z)
    u, g = jnp.split(z, 2, axis=-1)
    yc = chunked_spatial_gating(u, g, sgu_norm, sgu_w, sgu_b)
    yd = forgetting_attention(q, k, v, f, fox_bf)
    return jnp.concatenate([yc, yd], axis=-1) @ w_out


def conv_glu_ffn(x, w_up, conv_w, conv_b, w_down):
    h = causal_depthwise_conv(x @ w_up, conv_w, conv_b)
    gate, val = jnp.split(h, 2, axis=-1)
    return (jax.nn.silu(gate) * val) @ w_down


def _fwd_setup_inputs(seed: int = 0) -> dict:
    key = jax.random.key(seed)
    ks = jax.random.split(key, 32)
    f32 = jnp.float32

    def dense(k, shape, fan_in):
        return jax.random.normal(k, shape, f32) * fan_in ** -0.5

    def gain(k, shape):
        return 1.0 + 0.05 * jax.random.normal(k, shape, f32)

    def bias(k, shape):
        return 0.1 * jax.random.normal(k, shape, f32)

    a_c = jax.random.uniform(ks[9], (N_EVEN, LRU_WIDTH), f32, LRU_MIN_RAD, LRU_MAX_RAD)
    s_base = a_c ** (1.0 / LRU_C)
    lru_lambda = jnp.log(s_base) - jnp.log1p(-s_base)

    return {
        "x": jax.random.normal(ks[0], (BATCH, SEQ, D_MODEL), f32),
        "mix0_norm": gain(ks[1], (N_EVEN, D_MODEL)),
        "mix0_w_in": dense(ks[2], (N_EVEN, D_MODEL, EVEN_IN), D_MODEL),
        "lru_conv_w": dense(ks[3], (N_EVEN, LRU_CONV, LRU_WIDTH), LRU_CONV),
        "lru_conv_b": bias(ks[4], (N_EVEN, LRU_WIDTH)),
        "lru_wa": dense(ks[5], (N_EVEN, LRU_HEADS, LRU_HEAD_DIM, LRU_HEAD_DIM), LRU_HEAD_DIM),
        "lru_ba": bias(ks[6], (N_EVEN, LRU_WIDTH)),
        "lru_wx": dense(ks[7], (N_EVEN, LRU_HEADS, LRU_HEAD_DIM, LRU_HEAD_DIM), LRU_HEAD_DIM),
        "lru_bx": bias(ks[8], (N_EVEN, LRU_WIDTH)),
        "lru_lambda": lru_lambda,
        "sconv_w": dense(ks[10], (N_EVEN, SC_CONV, SC_WIDTH), SC_CONV),
        "sconv_b": bias(ks[11], (N_EVEN, SC_WIDTH)),
        "mix0_w_out": dense(ks[12], (N_EVEN, LRU_WIDTH + SC_WIDTH, D_MODEL), LRU_WIDTH + SC_WIDTH),
        "mix1_norm": gain(ks[13], (N_ODD, D_MODEL)),
        "mix1_w_in": dense(ks[14], (N_ODD, D_MODEL, ODD_IN), D_MODEL),
        "sgu_norm": gain(ks[15], (N_ODD, SGU_WIDTH)),
        "sgu_w": dense(ks[16], (N_ODD, SGU_HEADS, CHUNK, CHUNK), CHUNK),
        "sgu_b": 1.0 + bias(ks[17], (N_ODD, SGU_HEADS, CHUNK)),
        "fox_bf": bias(ks[18], (N_ODD, FOX_HEADS)),
        "mix1_w_out": dense(ks[19], (N_ODD, SGU_WIDTH + FOX_WIDTH, D_MODEL), SGU_WIDTH + FOX_WIDTH),
        "ffn_norm": gain(ks[20], (DEPTH, D_MODEL)),
        "ffn_up": dense(ks[21], (DEPTH, D_MODEL, 2 * D_FF), D_MODEL),
        "ffn_conv_w": dense(ks[22], (DEPTH, FFN_CONV, 2 * D_FF), FFN_CONV),
        "ffn_conv_b": bias(ks[23], (DEPTH, 2 * D_FF)),
        "ffn_down": dense(ks[24], (DEPTH, D_FF, D_MODEL), D_FF),
        "final_norm": gain(ks[25], (D_MODEL,)),
    }


def _fwd_reference(x, mix0_norm, mix0_w_in, lru_conv_w, lru_conv_b, lru_wa, lru_ba, lru_wx, lru_bx,
              lru_lambda, sconv_w, sconv_b, mix0_w_out, mix1_norm, mix1_w_in, sgu_norm, sgu_w,
              sgu_b, fox_bf, mix1_w_out, ffn_norm, ffn_up, ffn_conv_w, ffn_conv_b, ffn_down,
              final_norm):
    h = x
    for layer in range(DEPTH):
        i = layer // 2
        if layer % 2 == 0:
            h = h + even_mixer(rmsnorm(h, mix0_norm[i]), mix0_w_in[i], lru_conv_w[i], lru_conv_b[i],
                               lru_wa[i], lru_ba[i], lru_wx[i], lru_bx[i], lru_lambda[i],
                               sconv_w[i], sconv_b[i], mix0_w_out[i])
        else:
            h = h + odd_mixer(rmsnorm(h, mix1_norm[i]), mix1_w_in[i], sgu_norm[i], sgu_w[i],
                              sgu_b[i], fox_bf[i], mix1_w_out[i])
        h = h + conv_glu_ffn(rmsnorm(h, ffn_norm[layer]), ffn_up[layer], ffn_conv_w[layer],
                             ffn_conv_b[layer], ffn_down[layer])
    return rmsnorm(h, final_norm)


import jax as _jax
import jax.numpy as _jnp

TWIN_FORMAT = 'train_step'
FWD_PARAMS = ['x', 'mix0_norm', 'mix0_w_in', 'lru_conv_w', 'lru_conv_b', 'lru_wa', 'lru_ba', 'lru_wx', 'lru_bx', 'lru_lambda', 'sconv_w', 'sconv_b', 'mix0_w_out', 'mix1_norm', 'mix1_w_in', 'sgu_norm', 'sgu_w', 'sgu_b', 'fox_bf', 'mix1_w_out', 'ffn_norm', 'ffn_up', 'ffn_conv_w', 'ffn_conv_b', 'ffn_down', 'final_norm']
TWIN_WEIGHTS = ['mix0_norm', 'mix0_w_in', 'lru_conv_w', 'lru_conv_b', 'lru_wa', 'lru_ba', 'lru_wx', 'lru_bx', 'lru_lambda', 'sconv_w', 'sconv_b', 'mix0_w_out', 'mix1_norm', 'mix1_w_in', 'sgu_norm', 'sgu_w', 'sgu_b', 'fox_bf', 'mix1_w_out', 'ffn_norm', 'ffn_up', 'ffn_conv_w', 'ffn_conv_b', 'ffn_down', 'final_norm']
TWIN_DIFF_INPUT = 'x'
TWIN_INPUTS = ['x', 'mix0_norm', 'mix0_w_in', 'lru_conv_w', 'lru_conv_b', 'lru_wa', 'lru_ba', 'lru_wx', 'lru_bx', 'lru_lambda', 'sconv_w', 'sconv_b', 'mix0_w_out', 'mix1_norm', 'mix1_w_in', 'sgu_norm', 'sgu_w', 'sgu_b', 'fox_bf', 'mix1_w_out', 'ffn_norm', 'ffn_up', 'ffn_conv_w', 'ffn_conv_b', 'ffn_down', 'final_norm', 'loss_target', 'm_mix0_norm', 'm_mix0_w_in', 'm_lru_conv_w', 'm_lru_conv_b', 'm_lru_wa', 'm_lru_ba', 'm_lru_wx', 'm_lru_bx', 'm_lru_lambda', 'm_sconv_w', 'm_sconv_b', 'm_mix0_w_out', 'm_mix1_norm', 'm_mix1_w_in', 'm_sgu_norm', 'm_sgu_w', 'm_sgu_b', 'm_fox_bf', 'm_mix1_w_out', 'm_ffn_norm', 'm_ffn_up', 'm_ffn_conv_w', 'm_ffn_conv_b', 'm_ffn_down', 'm_final_norm', 'v_mix0_norm', 'v_mix0_w_in', 'v_lru_conv_w', 'v_lru_conv_b', 'v_lru_wa', 'v_lru_ba', 'v_lru_wx', 'v_lru_bx', 'v_lru_lambda', 'v_sconv_w', 'v_sconv_b', 'v_mix0_w_out', 'v_mix1_norm', 'v_mix1_w_in', 'v_sgu_norm', 'v_sgu_w', 'v_sgu_b', 'v_fox_bf', 'v_mix1_w_out', 'v_ffn_norm', 'v_ffn_up', 'v_ffn_conv_w', 'v_ffn_conv_b', 'v_ffn_down', 'v_final_norm']
TWIN_OUTPUTS = ['loss', 'grad_x', 'grad_mix0_norm', 'grad_mix0_w_in', 'grad_lru_conv_w', 'grad_lru_conv_b', 'grad_lru_wa', 'grad_lru_ba', 'grad_lru_wx', 'grad_lru_bx', 'grad_lru_lambda', 'grad_sconv_w', 'grad_sconv_b', 'grad_mix0_w_out', 'grad_mix1_norm', 'grad_mix1_w_in', 'grad_sgu_norm', 'grad_sgu_w', 'grad_sgu_b', 'grad_fox_bf', 'grad_mix1_w_out', 'grad_ffn_norm', 'grad_ffn_up', 'grad_ffn_conv_w', 'grad_ffn_conv_b', 'grad_ffn_down', 'grad_final_norm', 'delta_mix0_norm', 'delta_mix0_w_in', 'delta_lru_conv_w', 'delta_lru_conv_b', 'delta_lru_wa', 'delta_lru_ba', 'delta_lru_wx', 'delta_lru_bx', 'delta_lru_lambda', 'delta_sconv_w', 'delta_sconv_b', 'delta_mix0_w_out', 'delta_mix1_norm', 'delta_mix1_w_in', 'delta_sgu_norm', 'delta_sgu_w', 'delta_sgu_b', 'delta_fox_bf', 'delta_mix1_w_out', 'delta_ffn_norm', 'delta_ffn_up', 'delta_ffn_conv_w', 'delta_ffn_conv_b', 'delta_ffn_down', 'delta_final_norm', 'new_m_mix0_norm', 'new_m_mix0_w_in', 'new_m_lru_conv_w', 'new_m_lru_conv_b', 'new_m_lru_wa', 'new_m_lru_ba', 'new_m_lru_wx', 'new_m_lru_bx', 'new_m_lru_lambda', 'new_m_sconv_w', 'new_m_sconv_b', 'new_m_mix0_w_out', 'new_m_mix1_norm', 'new_m_mix1_w_in', 'new_m_sgu_norm', 'new_m_sgu_w', 'new_m_sgu_b', 'new_m_fox_bf', 'new_m_mix1_w_out', 'new_m_ffn_norm', 'new_m_ffn_up', 'new_m_ffn_conv_w', 'new_m_ffn_conv_b', 'new_m_ffn_down', 'new_m_final_norm', 'new_v_mix0_norm', 'new_v_mix0_w_in', 'new_v_lru_conv_w', 'new_v_lru_conv_b', 'new_v_lru_wa', 'new_v_lru_ba', 'new_v_lru_wx', 'new_v_lru_bx', 'new_v_lru_lambda', 'new_v_sconv_w', 'new_v_sconv_b', 'new_v_mix0_w_out', 'new_v_mix1_norm', 'new_v_mix1_w_in', 'new_v_sgu_norm', 'new_v_sgu_w', 'new_v_sgu_b', 'new_v_fox_bf', 'new_v_mix1_w_out', 'new_v_ffn_norm', 'new_v_ffn_up', 'new_v_ffn_conv_w', 'new_v_ffn_conv_b', 'new_v_ffn_down', 'new_v_final_norm']
TWIN_LEAF_KINDS = {'loss': 'loss', 'grad_x': 'grad_x', 'grad_mix0_norm': 'grad_w', 'grad_mix0_w_in': 'grad_w', 'grad_lru_conv_w': 'grad_w', 'grad_lru_conv_b': 'grad_w', 'grad_lru_wa': 'grad_w', 'grad_lru_ba': 'grad_w', 'grad_lru_wx': 'grad_w', 'grad_lru_bx': 'grad_w', 'grad_lru_lambda': 'grad_w', 'grad_sconv_w': 'grad_w', 'grad_sconv_b': 'grad_w', 'grad_mix0_w_out': 'grad_w', 'grad_mix1_norm': 'grad_w', 'grad_mix1_w_in': 'grad_w', 'grad_sgu_norm': 'grad_w', 'grad_sgu_w': 'grad_w', 'grad_sgu_b': 'grad_w', 'grad_fox_bf': 'grad_w', 'grad_mix1_w_out': 'grad_w', 'grad_ffn_norm': 'grad_w', 'grad_ffn_up': 'grad_w', 'grad_ffn_conv_w': 'grad_w', 'grad_ffn_conv_b': 'grad_w', 'grad_ffn_down': 'grad_w', 'grad_final_norm': 'grad_w', 'delta_mix0_norm': 'delta_w', 'delta_mix0_w_in': 'delta_w', 'delta_lru_conv_w': 'delta_w', 'delta_lru_conv_b': 'delta_w', 'delta_lru_wa': 'delta_w', 'delta_lru_ba': 'delta_w', 'delta_lru_wx': 'delta_w', 'delta_lru_bx': 'delta_w', 'delta_lru_lambda': 'delta_w', 'delta_sconv_w': 'delta_w', 'delta_sconv_b': 'delta_w', 'delta_mix0_w_out': 'delta_w', 'delta_mix1_norm': 'delta_w', 'delta_mix1_w_in': 'delta_w', 'delta_sgu_norm': 'delta_w', 'delta_sgu_w': 'delta_w', 'delta_sgu_b': 'delta_w', 'delta_fox_bf': 'delta_w', 'delta_mix1_w_out': 'delta_w', 'delta_ffn_norm': 'delta_w', 'delta_ffn_up': 'delta_w', 'delta_ffn_conv_w': 'delta_w', 'delta_ffn_conv_b': 'delta_w', 'delta_ffn_down': 'delta_w', 'delta_final_norm': 'delta_w', 'new_m_mix0_norm': 'new_m', 'new_m_mix0_w_in': 'new_m', 'new_m_lru_conv_w': 'new_m', 'new_m_lru_conv_b': 'new_m', 'new_m_lru_wa': 'new_m', 'new_m_lru_ba': 'new_m', 'new_m_lru_wx': 'new_m', 'new_m_lru_bx': 'new_m', 'new_m_lru_lambda': 'new_m', 'new_m_sconv_w': 'new_m', 'new_m_sconv_b': 'new_m', 'new_m_mix0_w_out': 'new_m', 'new_m_mix1_norm': 'new_m', 'new_m_mix1_w_in': 'new_m', 'new_m_sgu_norm': 'new_m', 'new_m_sgu_w': 'new_m', 'new_m_sgu_b': 'new_m', 'new_m_fox_bf': 'new_m', 'new_m_mix1_w_out': 'new_m', 'new_m_ffn_norm': 'new_m', 'new_m_ffn_up': 'new_m', 'new_m_ffn_conv_w': 'new_m', 'new_m_ffn_conv_b': 'new_m', 'new_m_ffn_down': 'new_m', 'new_m_final_norm': 'new_m', 'new_v_mix0_norm': 'new_v', 'new_v_mix0_w_in': 'new_v', 'new_v_lru_conv_w': 'new_v', 'new_v_lru_conv_b': 'new_v', 'new_v_lru_wa': 'new_v', 'new_v_lru_ba': 'new_v', 'new_v_lru_wx': 'new_v', 'new_v_lru_bx': 'new_v', 'new_v_lru_lambda': 'new_v', 'new_v_sconv_w': 'new_v', 'new_v_sconv_b': 'new_v', 'new_v_mix0_w_out': 'new_v', 'new_v_mix1_norm': 'new_v', 'new_v_mix1_w_in': 'new_v', 'new_v_sgu_norm': 'new_v', 'new_v_sgu_w': 'new_v', 'new_v_sgu_b': 'new_v', 'new_v_fox_bf': 'new_v', 'new_v_mix1_w_out': 'new_v', 'new_v_ffn_norm': 'new_v', 'new_v_ffn_up': 'new_v', 'new_v_ffn_conv_w': 'new_v', 'new_v_ffn_conv_b': 'new_v', 'new_v_ffn_down': 'new_v', 'new_v_final_norm': 'new_v'}


def _forward(args):
    return _fwd_reference(*[args[k] for k in FWD_PARAMS])


def _output_shape():
    def fwd():
        inp = _fwd_setup_inputs(0)
        return _fwd_reference(*[inp[k] for k in FWD_PARAMS])
    out = _jax.eval_shape(fwd)
    return out.shape, out.dtype

N_MICROBATCH = 1
ADAM_LR = 0.001
ADAM_B1 = 0.9
ADAM_B2 = 0.999
ADAM_EPS = 1e-08
ADAM_WD = 0.01
ADAM_STEP = 10
PER_EXAMPLE_BATCH_AXIS = {'x': 0, 'loss_target': 0}
SHARED_INPUTS = []
_WEIGHT_DTYPES = {'mix0_norm': _jnp.float32, 'mix0_w_in': _jnp.float32, 'lru_conv_w': _jnp.float32, 'lru_conv_b': _jnp.float32, 'lru_wa': _jnp.float32, 'lru_ba': _jnp.float32, 'lru_wx': _jnp.float32, 'lru_bx': _jnp.float32, 'lru_lambda': _jnp.float32, 'sconv_w': _jnp.float32, 'sconv_b': _jnp.float32, 'mix0_w_out': _jnp.float32, 'mix1_norm': _jnp.float32, 'mix1_w_in': _jnp.float32, 'sgu_norm': _jnp.float32, 'sgu_w': _jnp.float32, 'sgu_b': _jnp.float32, 'fox_bf': _jnp.float32, 'mix1_w_out': _jnp.float32, 'ffn_norm': _jnp.float32, 'ffn_up': _jnp.float32, 'ffn_conv_w': _jnp.float32, 'ffn_conv_b': _jnp.float32, 'ffn_down': _jnp.float32, 'final_norm': _jnp.float32}
MOMENT_SCALE = {'mix0_norm': 2.246583e-01, 'mix0_w_in': 1.415134e-01, 'lru_conv_w': 8.657236e-02, 'lru_conv_b': 9.030784e-01, 'lru_wa': 3.723135e-02, 'lru_ba': 2.815719e-02, 'lru_wx': 6.638862e-02, 'lru_bx': 4.168680e-02, 'lru_lambda': 5.598442e-02, 'sconv_w': 1.742183e-01, 'sconv_b': 1.861032e-01, 'mix0_w_out': 1.436564e-01, 'mix1_norm': 1.158185e-01, 'mix1_w_in': 6.927648e-02, 'sgu_norm': 5.644502e-02, 'sgu_w': 3.924770e-02, 'sgu_b': 5.593094e-02, 'fox_bf': 2.382408e-01, 'mix1_w_out': 9.811698e-02, 'ffn_norm': 1.125927e-01, 'ffn_up': 4.760511e-02, 'ffn_conv_w': 4.776057e-02, 'ffn_conv_b': 5.009166e-02, 'ffn_down': 7.842721e-02, 'final_norm': 3.205713e+01}


def _to_microbatches(a, axis):
    t = _jnp.moveaxis(a, axis, 0)
    t = t.reshape((N_MICROBATCH, t.shape[0] // N_MICROBATCH) + t.shape[1:])
    return _jnp.moveaxis(t, 1, axis + 1)


def setup_inputs(seed: int = 0) -> dict:
    inp = _fwd_setup_inputs(seed)
    key = _jax.random.fold_in(_jax.random.key(seed), 7919)
    shape, _ = _output_shape()
    out = dict(inp)
    out["loss_target"] = _jax.random.normal(_jax.random.fold_in(key, 0), shape, _jnp.float32)
    for i, name in enumerate(TWIN_WEIGHTS):
        w = inp[name].astype(_jnp.float32)
        if MOMENT_SCALE is None:
            s = _jnp.sqrt(_jnp.mean(_jnp.square(w)) + 1e-30)
        else:
            s = MOMENT_SCALE[name]
        km, kv = _jax.random.split(_jax.random.fold_in(key, i + 1))
        out[name] = w
        out["m_" + name] = s * _jax.random.normal(km, w.shape, _jnp.float32)
        out["v_" + name] = (s * s) * _jax.random.uniform(kv, w.shape, _jnp.float32, 0.5, 1.5)
    if N_MICROBATCH > 1:
        for name, axis in PER_EXAMPLE_BATCH_AXIS.items():
            out[name] = _to_microbatches(out[name], axis)
    return {'x': out['x'], 'mix0_norm': out['mix0_norm'], 'mix0_w_in': out['mix0_w_in'], 'lru_conv_w': out['lru_conv_w'], 'lru_conv_b': out['lru_conv_b'], 'lru_wa': out['lru_wa'], 'lru_ba': out['lru_ba'], 'lru_wx': out['lru_wx'], 'lru_bx': out['lru_bx'], 'lru_lambda': out['lru_lambda'], 'sconv_w': out['sconv_w'], 'sconv_b': out['sconv_b'], 'mix0_w_out': out['mix0_w_out'], 'mix1_norm': out['mix1_norm'], 'mix1_w_in': out['mix1_w_in'], 'sgu_norm': out['sgu_norm'], 'sgu_w': out['sgu_w'], 'sgu_b': out['sgu_b'], 'fox_bf': out['fox_bf'], 'mix1_w_out': out['mix1_w_out'], 'ffn_norm': out['ffn_norm'], 'ffn_up': out['ffn_up'], 'ffn_conv_w': out['ffn_conv_w'], 'ffn_conv_b': out['ffn_conv_b'], 'ffn_down': out['ffn_down'], 'final_norm': out['final_norm'], 'loss_target': out['loss_target'], 'm_mix0_norm': out['m_mix0_norm'], 'm_mix0_w_in': out['m_mix0_w_in'], 'm_lru_conv_w': out['m_lru_conv_w'], 'm_lru_conv_b': out['m_lru_conv_b'], 'm_lru_wa': out['m_lru_wa'], 'm_lru_ba': out['m_lru_ba'], 'm_lru_wx': out['m_lru_wx'], 'm_lru_bx': out['m_lru_bx'], 'm_lru_lambda': out['m_lru_lambda'], 'm_sconv_w': out['m_sconv_w'], 'm_sconv_b': out['m_sconv_b'], 'm_mix0_w_out': out['m_mix0_w_out'], 'm_mix1_norm': out['m_mix1_norm'], 'm_mix1_w_in': out['m_mix1_w_in'], 'm_sgu_norm': out['m_sgu_norm'], 'm_sgu_w': out['m_sgu_w'], 'm_sgu_b': out['m_sgu_b'], 'm_fox_bf': out['m_fox_bf'], 'm_mix1_w_out': out['m_mix1_w_out'], 'm_ffn_norm': out['m_ffn_norm'], 'm_ffn_up': out['m_ffn_up'], 'm_ffn_conv_w': out['m_ffn_conv_w'], 'm_ffn_conv_b': out['m_ffn_conv_b'], 'm_ffn_down': out['m_ffn_down'], 'm_final_norm': out['m_final_norm'], 'v_mix0_norm': out['v_mix0_norm'], 'v_mix0_w_in': out['v_mix0_w_in'], 'v_lru_conv_w': out['v_lru_conv_w'], 'v_lru_conv_b': out['v_lru_conv_b'], 'v_lru_wa': out['v_lru_wa'], 'v_lru_ba': out['v_lru_ba'], 'v_lru_wx': out['v_lru_wx'], 'v_lru_bx': out['v_lru_bx'], 'v_lru_lambda': out['v_lru_lambda'], 'v_sconv_w': out['v_sconv_w'], 'v_sconv_b': out['v_sconv_b'], 'v_mix0_w_out': out['v_mix0_w_out'], 'v_mix1_norm': out['v_mix1_norm'], 'v_mix1_w_in': out['v_mix1_w_in'], 'v_sgu_norm': out['v_sgu_norm'], 'v_sgu_w': out['v_sgu_w'], 'v_sgu_b': out['v_sgu_b'], 'v_fox_bf': out['v_fox_bf'], 'v_mix1_w_out': out['v_mix1_w_out'], 'v_ffn_norm': out['v_ffn_norm'], 'v_ffn_up': out['v_ffn_up'], 'v_ffn_conv_w': out['v_ffn_conv_w'], 'v_ffn_conv_b': out['v_ffn_conv_b'], 'v_ffn_down': out['v_ffn_down'], 'v_final_norm': out['v_final_norm']}


def _loss(weights, diff, rest, loss_target):
    with _jax.named_scope("forward"):
        args = {**rest, TWIN_DIFF_INPUT: diff, **{k: w.astype(_WEIGHT_DTYPES[k]) for k, w in weights.items()}}
        y = _forward(args)
    with _jax.named_scope("loss_head"):
        err = _jnp.square(y.astype(_jnp.float32) - loss_target)
        return 0.5 * _jnp.sum(_jnp.mean(err, axis=-1)) if err.ndim else 0.5 * err


def _adamw(w, g, m, v):
    m = ADAM_B1 * m + (1.0 - ADAM_B1) * g
    v = ADAM_B2 * v + (1.0 - ADAM_B2) * _jnp.square(g)
    m_hat = m / (1.0 - ADAM_B1 ** ADAM_STEP)
    v_hat = v / (1.0 - ADAM_B2 ** ADAM_STEP)
    delta = -ADAM_LR * (m_hat / (_jnp.sqrt(v_hat) + ADAM_EPS) + ADAM_WD * w)
    return delta, m, v


def reference(x, mix0_norm, mix0_w_in, lru_conv_w, lru_conv_b, lru_wa, lru_ba, lru_wx, lru_bx, lru_lambda, sconv_w, sconv_b, mix0_w_out, mix1_norm, mix1_w_in, sgu_norm, sgu_w, sgu_b, fox_bf, mix1_w_out, ffn_norm, ffn_up, ffn_conv_w, ffn_conv_b, ffn_down, final_norm, loss_target, m_mix0_norm, m_mix0_w_in, m_lru_conv_w, m_lru_conv_b, m_lru_wa, m_lru_ba, m_lru_wx, m_lru_bx, m_lru_lambda, m_sconv_w, m_sconv_b, m_mix0_w_out, m_mix1_norm, m_mix1_w_in, m_sgu_norm, m_sgu_w, m_sgu_b, m_fox_bf, m_mix1_w_out, m_ffn_norm, m_ffn_up, m_ffn_conv_w, m_ffn_conv_b, m_ffn_down, m_final_norm, v_mix0_norm, v_mix0_w_in, v_lru_conv_w, v_lru_conv_b, v_lru_wa, v_lru_ba, v_lru_wx, v_lru_bx, v_lru_lambda, v_sconv_w, v_sconv_b, v_mix0_w_out, v_mix1_norm, v_mix1_w_in, v_sgu_norm, v_sgu_w, v_sgu_b, v_fox_bf, v_mix1_w_out, v_ffn_norm, v_ffn_up, v_ffn_conv_w, v_ffn_conv_b, v_ffn_down, v_final_norm):
    given = dict(x=x, mix0_norm=mix0_norm, mix0_w_in=mix0_w_in, lru_conv_w=lru_conv_w, lru_conv_b=lru_conv_b, lru_wa=lru_wa, lru_ba=lru_ba, lru_wx=lru_wx, lru_bx=lru_bx, lru_lambda=lru_lambda, sconv_w=sconv_w, sconv_b=sconv_b, mix0_w_out=mix0_w_out, mix1_norm=mix1_norm, mix1_w_in=mix1_w_in, sgu_norm=sgu_norm, sgu_w=sgu_w, sgu_b=sgu_b, fox_bf=fox_bf, mix1_w_out=mix1_w_out, ffn_norm=ffn_norm, ffn_up=ffn_up, ffn_conv_w=ffn_conv_w, ffn_conv_b=ffn_conv_b, ffn_down=ffn_down, final_norm=final_norm, loss_target=loss_target, m_mix0_norm=m_mix0_norm, m_mix0_w_in=m_mix0_w_in, m_lru_conv_w=m_lru_conv_w, m_lru_conv_b=m_lru_conv_b, m_lru_wa=m_lru_wa, m_lru_ba=m_lru_ba, m_lru_wx=m_lru_wx, m_lru_bx=m_lru_bx, m_lru_lambda=m_lru_lambda, m_sconv_w=m_sconv_w, m_sconv_b=m_sconv_b, m_mix0_w_out=m_mix0_w_out, m_mix1_norm=m_mix1_norm, m_mix1_w_in=m_mix1_w_in, m_sgu_norm=m_sgu_norm, m_sgu_w=m_sgu_w, m_sgu_b=m_sgu_b, m_fox_bf=m_fox_bf, m_mix1_w_out=m_mix1_w_out, m_ffn_norm=m_ffn_norm, m_ffn_up=m_ffn_up, m_ffn_conv_w=m_ffn_conv_w, m_ffn_conv_b=m_ffn_conv_b, m_ffn_down=m_ffn_down, m_final_norm=m_final_norm, v_mix0_norm=v_mix0_norm, v_mix0_w_in=v_mix0_w_in, v_lru_conv_w=v_lru_conv_w, v_lru_conv_b=v_lru_conv_b, v_lru_wa=v_lru_wa, v_lru_ba=v_lru_ba, v_lru_wx=v_lru_wx, v_lru_bx=v_lru_bx, v_lru_lambda=v_lru_lambda, v_sconv_w=v_sconv_w, v_sconv_b=v_sconv_b, v_mix0_w_out=v_mix0_w_out, v_mix1_norm=v_mix1_norm, v_mix1_w_in=v_mix1_w_in, v_sgu_norm=v_sgu_norm, v_sgu_w=v_sgu_w, v_sgu_b=v_sgu_b, v_fox_bf=v_fox_bf, v_mix1_w_out=v_mix1_w_out, v_ffn_norm=v_ffn_norm, v_ffn_up=v_ffn_up, v_ffn_conv_w=v_ffn_conv_w, v_ffn_conv_b=v_ffn_conv_b, v_ffn_down=v_ffn_down, v_final_norm=v_final_norm)
    weights = {n: given[n] for n in TWIN_WEIGHTS}
    shared = {n: given[n] for n in SHARED_INPUTS}
    per_example = {n: given[n] for n in ['x']}
    grad_fn = _jax.value_and_grad(_loss, argnums=(0, 1))

    def one_microbatch(ex, loss_target):
        ex = dict(ex)
        diff = ex.pop(TWIN_DIFF_INPUT)
        return grad_fn(weights, diff, {**shared, **ex}, loss_target)

    if N_MICROBATCH == 1:
        loss, (grad_w, grad_x) = one_microbatch(per_example, given["loss_target"])
    else:
        def body(carry, xs):
            loss_sum, grad_sum = carry
            l_k, (gw_k, gx_k) = one_microbatch(xs[0], xs[1])
            with _jax.named_scope("update"):
                return (loss_sum + l_k, _jax.tree.map(_jnp.add, grad_sum, gw_k)), gx_k

        init = (_jnp.zeros((), _jnp.float32), _jax.tree.map(_jnp.zeros_like, weights))
        (loss, grad_w), grad_x = _jax.lax.scan(body, init, (per_example, given["loss_target"]))
    with _jax.named_scope("update"):
        delta_w, new_m, new_v = {}, {}, {}
        for n in TWIN_WEIGHTS:
            delta_w[n], new_m[n], new_v[n] = _adamw(weights[n], grad_w[n], given["m_" + n], given["v_" + n])
    return (loss, grad_x, *[grad_w[n] for n in TWIN_WEIGHTS], *[delta_w[n] for n in TWIN_WEIGHTS],
            *[new_m[n] for n in TWIN_WEIGHTS], *[new_v[n] for n in TWIN_WEIGHTS])
```

```python
import functools

import jax
import jax.numpy as jnp
from jax import lax
from jax.experimental import pallas as pl
from jax.experimental.pallas import tpu as pltpu

F32 = jnp.float32
BF16 = jnp.bfloat16
MESH = pl.DeviceIdType.MESH

D_MODEL = 1024
LANES = 128
SUBLANES = 8
N_CHIPS = 4
EPS = 1e-6
LRU_C = 8.0
D_FF = 2816
FFN_CB = 256
CHUNK = 128
NEG = -1e30

ADAM_LR = 0.001
ADAM_B1 = 0.9
ADAM_B2 = 0.999
ADAM_EPS = 1e-08
ADAM_WD = 0.01
ADAM_STEP = 10
ADAM_C1 = 1.0 - ADAM_B1 ** ADAM_STEP
ADAM_C2 = 1.0 - ADAM_B2 ** ADAM_STEP

_GELU_C = 0.7978845608028654
_GELU_A = 0.044715


def _sigmoid(x):
    return 1.0 / (1.0 + jnp.exp(-x))


def _log1p_pos(e):
    w = 1.0 + e
    return jnp.where(w == 1.0, e, jnp.log(w) * (e / (w - 1.0)))


def _softplus(x):
    return jnp.maximum(x, 0.0) + _log1p_pos(jnp.exp(-jnp.abs(x)))


def _gelu(x):
    t = jnp.tanh(_GELU_C * (x + _GELU_A * (x * x * x)))
    return 0.5 * x * (1.0 + t), t


def _gelu_grad(x, t):
    return 0.5 * (1.0 + t) + 0.5 * x * (1.0 - t * t) * (_GELU_C * (1.0 + 3.0 * _GELU_A * x * x))


def _rows(shape):
    return lax.broadcasted_iota(jnp.int32, shape, 0)


def _lanes(shape):
    return lax.broadcasted_iota(jnp.int32, shape, 1)


def _shift_down(x, halo8, j):
    if j == 0:
        return x
    r = pltpu.roll(x, j, 0)
    hr = pltpu.roll(halo8, j, 0)
    top = jnp.where(_rows(hr.shape) < j, hr, r[:SUBLANES])
    return jnp.concatenate([top, r[SUBLANES:]], axis=0)


def _shift_up(x, next8, j):
    if j == 0:
        return x
    n = x.shape[0]
    r = pltpu.roll(x, n - j, 0)
    nr = pltpu.roll(next8, SUBLANES - j, 0)
    bot = jnp.where(_rows(nr.shape) >= SUBLANES - j, nr, r[n - SUBLANES:])
    return jnp.concatenate([r[:n - SUBLANES], bot], axis=0)


def _scan_fwd(a, u):
    n = a.shape[0]
    row = _rows(a.shape)
    h = u
    k = 1
    while k < n:
        keep = row >= k
        h_sh = jnp.where(keep, pltpu.roll(h, k, 0), 0.0)
        a_sh = jnp.where(keep, pltpu.roll(a, k, 0), 1.0)
        h = a * h_sh + h
        a = a * a_sh
        k *= 2
    return h, a


def _scan_rev(b, d):
    n = b.shape[0]
    row = _rows(b.shape)
    g = d
    k = 1
    while k < n:
        keep = row < n - k
        g_sh = jnp.where(keep, pltpu.roll(g, n - k, 0), 0.0)
        b_sh = jnp.where(keep, pltpu.roll(b, n - k, 0), 1.0)
        g = b * g_sh + g
        b = b * b_sh
        k *= 2
    return g, b


def _cumsum_fwd(x):
    n = x.shape[0]
    row = _rows(x.shape)
    k = 1
    while k < n:
        x = x + jnp.where(row >= k, pltpu.roll(x, k, 0), 0.0)
        k *= 2
    return x


def _cumsum_rev(x):
    n = x.shape[0]
    row = _rows(x.shape)
    k = 1
    while k < n:
        x = x + jnp.where(row < n - k, pltpu.roll(x, n - k, 0), 0.0)
        k *= 2
    return x


def _dot(a, b):
    return lax.dot_general(a, b, (((1,), (0,)), ((), ())), preferred_element_type=F32)


def _dot_nt(a, b):
    return lax.dot_general(a, b, (((1,), (1,)), ((), ())), preferred_element_type=F32)


def _dot_tn(a, b):
    return lax.dot_general(a, b, (((0,), (0,)), ((), ())), preferred_element_type=F32)


def _dot_split(x, m_bf16):
    hi = x.astype(BF16)
    lo = (x - hi.astype(F32)).astype(BF16)
    return _dot(hi, m_bf16) + _dot(lo, m_bf16)


def _tile_rows(ts, s):
    return min(ts, s)


def _mm(a_list, w, *, trans_w=False, res=None, out_dtype=F32, ts=512, nb=None, name):
    s = a_list[0].shape[0]
    ks = [a.shape[1] for a in a_list]
    k = sum(ks)
    n = w.shape[0] if trans_w else w.shape[1]
    ts = _tile_rows(ts, s)
    nb = n if nb is None else nb
    na = len(a_list)
    has_res = res is not None

    def body(*refs):
        a_refs = refs[:na]
        w_ref = refs[na]
        o_ref = refs[-1]
        parts = [r[...].astype(BF16) for r in a_refs]
        a = parts[0] if na == 1 else jnp.concatenate(parts, axis=1)
        acc = _dot_nt(a, w_ref[...]) if trans_w else _dot(a, w_ref[...])
        if has_res:
            acc = acc + refs[na + 1][...]
        o_ref[...] = acc.astype(out_dtype)

    in_specs = [pl.BlockSpec((ts, kk), lambda j, i: (i, 0)) for kk in ks]
    if trans_w:
        in_specs.append(pl.BlockSpec((nb, k), lambda j, i: (j, 0)))
    else:
        in_specs.append(pl.BlockSpec((k, nb), lambda j, i: (0, j)))
    args = list(a_list) + [w]
    if has_res:
        in_specs.append(pl.BlockSpec((ts, nb), lambda j, i: (i, j)))
        args.append(res)
    return pl.pallas_call(
        body, name=name, grid=(n // nb, s // ts), in_specs=in_specs,
        out_specs=pl.BlockSpec((ts, nb), lambda j, i: (i, j)),
        out_shape=jax.ShapeDtypeStruct((s, n), out_dtype),
    )(*args)


def _mm_tn(a_list, b, *, ts=512, nb=None, name):
    s = b.shape[0]
    ks = [a.shape[1] for a in a_list]
    k = sum(ks)
    n = b.shape[1]
    ts = _tile_rows(ts, s)
    nb = n if nb is None else nb
    na = len(a_list)

    def body(*refs):
        a_refs = refs[:na]
        b_ref = refs[na]
        o_ref = refs[-1]
        i = pl.program_id(1)
        parts = [r[...].astype(BF16) for r in a_refs]
        a = parts[0] if na == 1 else jnp.concatenate(parts, axis=1)
        upd = _dot_tn(a, b_ref[...].astype(BF16))

        @pl.when(i == 0)
        def _():
            o_ref[...] = upd

        @pl.when(i > 0)
        def _():
            o_ref[...] += upd

    in_specs = [pl.BlockSpec((ts, kk), lambda j, i: (i, 0)) for kk in ks]
    in_specs.append(pl.BlockSpec((ts, nb), lambda j, i: (i, j)))
    return pl.pallas_call(
        body, name=name, grid=(n // nb, s // ts), in_specs=in_specs,
        out_specs=pl.BlockSpec((k, nb), lambda j, i: (0, j)),
        out_shape=jax.ShapeDtypeStruct((k, n), F32),
    )(*a_list, b)


def _norm_fwd(h, g, *, ts=512, name):
    s, d = h.shape
    ts = _tile_rows(ts, s)

    def body(h_ref, g_ref, n_ref):
        x = h_ref[...]
        r = lax.rsqrt(jnp.mean(x * x, axis=-1, keepdims=True) + EPS)
        n_ref[...] = ((x * r) * g_ref[...]).astype(BF16)

    return pl.pallas_call(
        body, name=name, grid=(s // ts,),
        in_specs=[pl.BlockSpec((ts, d), lambda i: (i, 0)), pl.BlockSpec((1, d), lambda i: (0, 0))],
        out_specs=pl.BlockSpec((ts, d), lambda i: (i, 0)),
        out_shape=jax.ShapeDtypeStruct((s, d), BF16),
    )(h, g)


def _norm_bwd(dn, h, g, dres, *, ts=512, name):
    s, d = h.shape
    ts = _tile_rows(ts, s)

    def body(dn_ref, h_ref, g_ref, dres_ref, dh_ref, dg_ref):
        i = pl.program_id(0)
        x = h_ref[...]
        dnv = dn_ref[...]
        r = lax.rsqrt(jnp.mean(x * x, axis=-1, keepdims=True) + EPS)
        xhat = x * r
        part = jnp.sum(dnv * xhat, axis=0, keepdims=True)

        @pl.when(i == 0)
        def _():
            dg_ref[...] = part

        @pl.when(i > 0)
        def _():
            dg_ref[...] += part

        dxh = dnv * g_ref[...]
        dh_ref[...] = dres_ref[...] + r * (dxh - xhat * jnp.mean(dxh * xhat, axis=-1, keepdims=True))

    tile = pl.BlockSpec((ts, d), lambda i: (i, 0))
    vec = pl.BlockSpec((1, d), lambda i: (0, 0))
    return pl.pallas_call(
        body, name=name, grid=(s // ts,), in_specs=[tile, tile, vec, tile],
        out_specs=(tile, vec),
        out_shape=(jax.ShapeDtypeStruct((s, d), F32), jax.ShapeDtypeStruct((1, d), F32)),
    )(dn, h, g, dres)


def _final(h, g, target, *, ts=512, name):
    s, d = h.shape
    ts = _tile_rows(ts, s)
    nt = s // ts

    def body(h_ref, g_ref, t_ref, dh_ref, loss_ref, dg_ref, acc_ref):
        i = pl.program_id(0)
        x = h_ref[...]
        r = lax.rsqrt(jnp.mean(x * x, axis=-1, keepdims=True) + EPS)
        xhat = x * r
        gv = g_ref[...]
        err = xhat * gv - t_ref[...]
        sq = jnp.sum(err * err, axis=0, keepdims=True)
        dy = err * (1.0 / d)
        part = jnp.sum(dy * xhat, axis=0, keepdims=True)

        @pl.when(i == 0)
        def _():
            acc_ref[...] = sq
            dg_ref[...] = part

        @pl.when(i > 0)
        def _():
            acc_ref[...] += sq
            dg_ref[...] += part

        dxh = dy * gv
        dh_ref[...] = r * (dxh - xhat * jnp.mean(dxh * xhat, axis=-1, keepdims=True))

        @pl.when(i == nt - 1)
        def _():
            tot = jnp.sum(acc_ref[...], axis=1, keepdims=True) * (0.5 / d)
            loss_ref[...] = jnp.broadcast_to(tot, (1, LANES))

    tile = pl.BlockSpec((ts, d), lambda i: (i, 0))
    vec = pl.BlockSpec((1, d), lambda i: (0, 0))
    return pl.pallas_call(
        body, name=name, grid=(nt,), in_specs=[tile, vec, tile],
        out_specs=(tile, pl.BlockSpec((1, LANES), lambda i: (0, 0)), vec),
        out_shape=(jax.ShapeDtypeStruct((s, d), F32), jax.ShapeDtypeStruct((1, LANES), F32),
                   jax.ShapeDtypeStruct((1, d), F32)),
        scratch_shapes=[pltpu.VMEM((1, d), F32)],
    )(h, g, target)


def _halo_map(ts, width_blocks):
    per = ts // SUBLANES

    def index(j, i):
        return (jnp.maximum(i * per - 1, 0), width_blocks(j))

    return index


def _even_gates(xc, wa, ba, wx, bx, sp):
    xb = xc.astype(BF16)
    r = _sigmoid(_dot(xb, wa) + ba)
    ig = _sigmoid(_dot(xb, wx) + bx)
    la = (-LRU_C) * r * sp
    a = jnp.exp(la)
    a2 = a * a
    m = jnp.sqrt(-jnp.tanh(la) * (1.0 + a2))
    return r, ig, la, a, a2, m


def _even_core_fwd(p, w4, b4, wa, ba, wx, bx, lam, w3, b3, *, ts=512, name):
    s = p.shape[0]
    ts = _tile_rows(ts, s)
    nt = s // ts
    nblk = 4

    def body(p_ref, ph_ref, w4_ref, b4_ref, wa_ref, ba_ref, wx_ref, bx_ref, lam_ref, w3_ref, b3_ref,
             ya_ref, yb_ref, hl_ref, hcar_ref):
        i = pl.program_id(1)
        first = (i > 0).astype(F32)
        xa = p_ref[:, 0:LANES]
        ga = p_ref[:, LANES:2 * LANES]
        cp = p_ref[:, 2 * LANES:3 * LANES]
        bp = p_ref[:, 3 * LANES:4 * LANES]
        vb = p_ref[:, 4 * LANES:5 * LANES]
        xa_h = ph_ref[:, 0:LANES] * first
        s_h = ph_ref[:, 2 * LANES:3 * LANES] * ph_ref[:, 4 * LANES:5 * LANES] * first

        xc = b4_ref[...] + w4_ref[3:4, :] * xa
        for k in range(3):
            xc = xc + w4_ref[k:k + 1, :] * _shift_down(xa, xa_h, 3 - k)
        sp = _softplus(-lam_ref[...])
        _, ig, _, a, _, m = _even_gates(xc, wa_ref[0], ba_ref[...], wx_ref[0], bx_ref[...], sp)
        u = m * (ig * xc)
        hs, acum = _scan_fwd(a, u)

        @pl.when(i == 0)
        def _():
            hcar_ref[...] = jnp.zeros_like(hcar_ref)

        hs = hs + acum * hcar_ref[0:1, :]
        hl_ref[...] = hs
        hcar_ref[0:1, :] = hl_ref[ts - 1:ts, :]
        ge, _ = _gelu(ga)
        ya_ref[...] = (hs * ge).astype(BF16)

        sv = cp * vb
        sc = b3_ref[...] + w3_ref[2:3, :] * sv
        for k in range(2):
            sc = sc + w3_ref[k:k + 1, :] * _shift_down(sv, s_h, 2 - k)
        yb_ref[...] = (bp * sc).astype(BF16)

    blk = pl.BlockSpec((ts, 5 * LANES), lambda j, i: (i, j))
    halo = pl.BlockSpec((SUBLANES, 5 * LANES), _halo_map(ts, lambda j: j))
    vec = pl.BlockSpec((1, LANES), lambda j, i: (0, j))
    out = pl.BlockSpec((ts, LANES), lambda j, i: (i, j))
    return pl.pallas_call(
        body, name=name, grid=(nblk, nt),
        in_specs=[blk, halo,
                  pl.BlockSpec((4, LANES), lambda j, i: (0, j)), vec,
                  pl.BlockSpec((1, LANES, LANES), lambda j, i: (j, 0, 0)), vec,
                  pl.BlockSpec((1, LANES, LANES), lambda j, i: (j, 0, 0)), vec, vec,
                  pl.BlockSpec((3, LANES), lambda j, i: (0, j)), vec],
        out_specs=(out, out, out),
        out_shape=(jax.ShapeDtypeStruct((s, 4 * LANES), BF16), jax.ShapeDtypeStruct((s, 4 * LANES), BF16),
                   jax.ShapeDtypeStruct((s, 4 * LANES), F32)),
        scratch_shapes=[pltpu.VMEM((SUBLANES, LANES), F32)],
    )(p, p, w4, b4, wa, ba, wx, bx, lam, w3, b3)


def _even_core_bwd(dy, p, hl, w4, b4, wa, wat, ba, wx, wxt, bx, lam, w3, b3, *, ts=256, name):
    s = p.shape[0]
    ts = _tile_rows(ts, s)
    nt = s // ts
    nblk = 4
    per = ts // SUBLANES

    def body(dya_ref, dyb_ref, p_ref, ph_ref, hl_ref, hh_ref,
             w4_ref, b4_ref, wa_ref, wat_ref, ba_ref, wx_ref, wxt_ref, bx_ref, lam_ref, w3_ref, b3_ref,
             dp_ref, dw4_ref, db4_ref, dwa_ref, dba_ref, dwx_ref, dbx_ref, dlam_ref, dw3_ref, db3_ref,
             dxc_nx, dsc_nx, cg_ref):
        i = pl.program_id(1)
        ti = nt - 1 - i
        first = (ti > 0).astype(F32)
        xa = p_ref[:, 0:LANES]
        ga = p_ref[:, LANES:2 * LANES]
        cp = p_ref[:, 2 * LANES:3 * LANES]
        bp = p_ref[:, 3 * LANES:4 * LANES]
        vb = p_ref[:, 4 * LANES:5 * LANES]
        xa_h = ph_ref[:, 0:LANES] * first
        s_h = ph_ref[:, 2 * LANES:3 * LANES] * ph_ref[:, 4 * LANES:5 * LANES] * first
        h_h = hh_ref[...] * first

        @pl.when(i == 0)
        def _():
            dxc_nx[...] = jnp.zeros_like(dxc_nx)
            dsc_nx[...] = jnp.zeros_like(dsc_nx)
            cg_ref[...] = jnp.zeros_like(cg_ref)
            for ref in (dw4_ref, db4_ref, dwa_ref, dba_ref, dwx_ref, dbx_ref, dlam_ref, dw3_ref, db3_ref):
                ref[...] = jnp.zeros_like(ref)

        xa_sh = [_shift_down(xa, xa_h, 3 - k) for k in range(3)] + [xa]
        xc = b4_ref[...]
        for k in range(4):
            xc = xc + w4_ref[k:k + 1, :] * xa_sh[k]
        lamv = lam_ref[...]
        sp = _softplus(-lamv)
        r, ig, _, a, a2, m = _even_gates(xc, wa_ref[0], ba_ref[...], wx_ref[0], bx_ref[...], sp)
        sv = cp * vb
        sv_sh = [_shift_down(sv, s_h, 2 - k) for k in range(2)] + [sv]
        sc = b3_ref[...]
        for k in range(3):
            sc = sc + w3_ref[k:k + 1, :] * sv_sh[k]
        hs = hl_ref[...]
        h_prev = _shift_down(hs, h_h, 1)

        dya = dya_ref[...]
        dyb = dyb_ref[...]
        ge, gt = _gelu(ga)
        dga = dya * hs * _gelu_grad(ga, gt)
        dh = dya * ge

        ones8 = jnp.ones((SUBLANES, LANES), F32)
        b = _shift_up(a, ones8, 1)
        g, bcum = _scan_rev(b, dh)
        g = g + bcum * cg_ref[0:1, :]
        ag = a * g
        cg_ref[...] = ag[:SUBLANES]

        da = g * h_prev
        xi = ig * xc
        dm = g * xi
        dig = g * m * xc
        dxc = g * m * ig
        dla = da * a - dm * (a2 / m)
        dr = dla * ((-LRU_C) * sp)
        dlam_ref[...] += jnp.sum(dla * r, axis=0, keepdims=True) * (LRU_C * _sigmoid(-lamv))
        dra = dr * r * (1.0 - r)
        dia = dig * ig * (1.0 - ig)
        drab = dra.astype(BF16)
        diab = dia.astype(BF16)
        xcb = xc.astype(BF16)
        dxc = dxc + _dot(drab, wat_ref[0]) + _dot(diab, wxt_ref[0])
        dwa_ref[0] += _dot_tn(xcb, drab)
        dwx_ref[0] += _dot_tn(xcb, diab)
        dba_ref[...] += jnp.sum(dra, axis=0, keepdims=True)
        dbx_ref[...] += jnp.sum(dia, axis=0, keepdims=True)

        nx = dxc_nx[...]
        dxa = w4_ref[3:4, :] * dxc
        for k in range(3):
            dxa = dxa + w4_ref[k:k + 1, :] * _shift_up(dxc, nx, 3 - k)
        for k in range(4):
            dw4_ref[k:k + 1, :] += jnp.sum(dxc * xa_sh[k], axis=0, keepdims=True)
        db4_ref[...] += jnp.sum(dxc, axis=0, keepdims=True)
        dxc_nx[...] = dxc[:SUBLANES]

        dbp = dyb * sc
        dsc = dyb * bp
        nsc = dsc_nx[...]
        ds = w3_ref[2:3, :] * dsc
        for k in range(2):
            ds = ds + w3_ref[k:k + 1, :] * _shift_up(dsc, nsc, 2 - k)
        for k in range(3):
            dw3_ref[k:k + 1, :] += jnp.sum(dsc * sv_sh[k], axis=0, keepdims=True)
        db3_ref[...] += jnp.sum(dsc, axis=0, keepdims=True)
        dsc_nx[...] = dsc[:SUBLANES]

        dp_ref[:, 0:LANES] = dxa.astype(BF16)
        dp_ref[:, LANES:2 * LANES] = dga.astype(BF16)
        dp_ref[:, 2 * LANES:3 * LANES] = (ds * vb).astype(BF16)
        dp_ref[:, 3 * LANES:4 * LANES] = dbp.astype(BF16)
        dp_ref[:, 4 * LANES:5 * LANES] = (ds * cp).astype(BF16)

    def rev(j, i):
        return (nt - 1 - i, j)

    def rev_halo(col):
        def index(j, i):
            return (jnp.maximum((nt - 1 - i) * per - 1, 0), col(j))
        return index

    blk = pl.BlockSpec((ts, 5 * LANES), rev)
    one = pl.BlockSpec((ts, LANES), rev)
    vec = pl.BlockSpec((1, LANES), lambda j, i: (0, j))
    mat = pl.BlockSpec((1, LANES, LANES), lambda j, i: (j, 0, 0))
    w4s = pl.BlockSpec((4, LANES), lambda j, i: (0, j))
    w3s = pl.BlockSpec((3, LANES), lambda j, i: (0, j))
    f = jax.ShapeDtypeStruct
    return pl.pallas_call(
        body, name=name, grid=(nblk, nt),
        in_specs=[one, pl.BlockSpec((ts, LANES), lambda j, i: (nt - 1 - i, 4 + j)),
                  blk, pl.BlockSpec((SUBLANES, 5 * LANES), rev_halo(lambda j: j)),
                  one, pl.BlockSpec((SUBLANES, LANES), rev_halo(lambda j: j)),
                  w4s, vec, mat, mat, vec, mat, mat, vec, vec, w3s, vec],
        out_specs=(blk, w4s, vec, mat, vec, mat, vec, vec, w3s, vec),
        out_shape=(f((s, 20 * LANES), BF16), f((4, 4 * LANES), F32), f((1, 4 * LANES), F32),
                   f((4, LANES, LANES), F32), f((1, 4 * LANES), F32),
                   f((4, LANES, LANES), F32), f((1, 4 * LANES), F32), f((1, 4 * LANES), F32),
                   f((3, 4 * LANES), F32), f((1, 4 * LANES), F32)),
        scratch_shapes=[pltpu.VMEM((SUBLANES, LANES), F32), pltpu.VMEM((SUBLANES, LANES), F32),
                        pltpu.VMEM((SUBLANES, LANES), F32)],
    )(dy, dy, p, p, hl, hl, w4, b4, wa, wat, ba, wx, wxt, bx, lam, w3, b3)


def _ffn_conv(u, u_h, w_ref, b_ref):
    u_sh = [_shift_down(u, u_h, 2 - k) for k in range(2)] + [u]
    hc = b_ref[...]
    for k in range(3):
        hc = hc + w_ref[k:k + 1, :] * u_sh[k]
    return hc, u_sh


def _ffn_core_fwd(up, w, b, *, ts=512, name):
    s = up.shape[0]
    ts = _tile_rows(ts, s)
    nt = s // ts
    nblk = D_FF // FFN_CB
    cw = 2 * FFN_CB
    per = ts // 16

    def body(u_ref, uh_ref, w_ref, b_ref, act_ref):
        i = pl.program_id(1)
        first = (i > 0).astype(F32)
        u = u_ref[...].astype(F32)
        u_h = uh_ref[...].astype(F32)[SUBLANES:] * first
        hc, _ = _ffn_conv(u, u_h, w_ref, b_ref)
        gate = hc[:, :FFN_CB]
        val = hc[:, FFN_CB:]
        act_ref[...] = (gate * _sigmoid(gate) * val).astype(BF16)

    return pl.pallas_call(
        body, name=name, grid=(nblk, nt),
        in_specs=[pl.BlockSpec((ts, cw), lambda j, i: (i, j)),
                  pl.BlockSpec((16, cw), lambda j, i: (jnp.maximum(i * per - 1, 0), j)),
                  pl.BlockSpec((3, cw), lambda j, i: (0, j)), pl.BlockSpec((1, cw), lambda j, i: (0, j))],
        out_specs=pl.BlockSpec((ts, FFN_CB), lambda j, i: (i, j)),
        out_shape=jax.ShapeDtypeStruct((s, D_FF), BF16),
    )(up, up, w, b)


def _ffn_core_bwd(dact, up, w, b, *, ts=512, name):
    s = up.shape[0]
    ts = _tile_rows(ts, s)
    nt = s // ts
    nblk = D_FF // FFN_CB
    cw = 2 * FFN_CB
    per = ts // 16

    def body(da_ref, u_ref, uh_ref, w_ref, b_ref, dup_ref, dw_ref, db_ref, nx_ref):
        i = pl.program_id(1)
        ti = nt - 1 - i
        first = (ti > 0).astype(F32)
        u = u_ref[...].astype(F32)
        u_h = uh_ref[...].astype(F32)[SUBLANES:] * first
        hc, u_sh = _ffn_conv(u, u_h, w_ref, b_ref)
        gate = hc[:, :FFN_CB]
        val = hc[:, FFN_CB:]
        da = da_ref[...].astype(F32)
        sg = _sigmoid(gate)
        dgate = da * val * (sg * (1.0 + gate * (1.0 - sg)))
        dval = da * (gate * sg)
        dhc = jnp.concatenate([dgate, dval], axis=1)

        @pl.when(i == 0)
        def _():
            nx_ref[...] = jnp.zeros_like(nx_ref)
            dw_ref[...] = jnp.zeros_like(dw_ref)
            db_ref[...] = jnp.zeros_like(db_ref)

        nx = nx_ref[...]
        dup = w_ref[2:3, :] * dhc
        for k in range(2):
            dup = dup + w_ref[k:k + 1, :] * _shift_up(dhc, nx, 2 - k)
        dup_ref[...] = dup.astype(BF16)
        for k in range(3):
            dw_ref[k:k + 1, :] += jnp.sum(dhc * u_sh[k], axis=0, keepdims=True)
        db_ref[...] += jnp.sum(dhc, axis=0, keepdims=True)
        nx_ref[...] = dhc[:SUBLANES]

    return pl.pallas_call(
        body, name=name, grid=(nblk, nt),
        in_specs=[pl.BlockSpec((ts, FFN_CB), lambda j, i: (nt - 1 - i, j)),
                  pl.BlockSpec((ts, cw), lambda j, i: (nt - 1 - i, j)),
                  pl.BlockSpec((16, cw), lambda j, i: (jnp.maximum((nt - 1 - i) * per - 1, 0), j)),
                  pl.BlockSpec((3, cw), lambda j, i: (0, j)), pl.BlockSpec((1, cw), lambda j, i: (0, j))],
        out_specs=(pl.BlockSpec((ts, cw), lambda j, i: (nt - 1 - i, j)),
                   pl.BlockSpec((3, cw), lambda j, i: (0, j)), pl.BlockSpec((1, cw), lambda j, i: (0, j))),
        out_shape=(jax.ShapeDtypeStruct((s, 2 * D_FF), BF16), jax.ShapeDtypeStruct((3, 2 * D_FF), F32),
                   jax.ShapeDtypeStruct((1, 2 * D_FF), F32)),
        scratch_shapes=[pltpu.VMEM((SUBLANES, cw), F32)],
    )(dact, up, up, w, b)


def _sgu_forward_block(zu, zg, gn, w_ref, bias, seg):
    u, tu = _gelu(zu)
    g, tg = _gelu(zg)
    ms = _dot_split(g * g, seg)
    rs = lax.rsqrt(ms + EPS)
    ghat = g * rs
    gv = ghat * gn
    gvb = gv.astype(BF16)
    lane = _lanes((CHUNK, LANES))
    chunks = []
    for c in range(zu.shape[0] // CHUNK):
        gc = gvb[c * CHUNK:(c + 1) * CHUNK]
        mix = jnp.where(lane < 64, _dot(w_ref[0], gc), _dot(w_ref[1], gc)) + bias
        chunks.append(mix)
    mixed = chunks[0] if len(chunks) == 1 else jnp.concatenate(chunks, axis=0)
    return u, tu, g, tg, rs, ghat, gvb, mixed


def _sgu_fwd(p1, gn, w, bias, seg, *, ts=512, name):
    s = p1.shape[0]
    ts = _tile_rows(ts, s)

    def body(zu_ref, zg_ref, gn_ref, w_ref, bias_ref, seg_ref, yc_ref):
        u, _, _, _, _, _, _, mixed = _sgu_forward_block(
            zu_ref[...], zg_ref[...], gn_ref[...], w_ref, bias_ref[...], seg_ref[...])
        yc_ref[...] = (u * mixed).astype(BF16)

    return pl.pallas_call(
        body, name=name, grid=(4, s // ts),
        in_specs=[pl.BlockSpec((ts, LANES), lambda j, i: (i, j)),
                  pl.BlockSpec((ts, LANES), lambda j, i: (i, 4 + j)),
                  pl.BlockSpec((1, LANES), lambda j, i: (0, j)),
                  pl.BlockSpec((2, CHUNK, CHUNK), lambda j, i: (j, 0, 0)),
                  pl.BlockSpec((CHUNK, LANES), lambda j, i: (0, j)),
                  pl.BlockSpec((LANES, LANES), lambda j, i: (0, 0))],
        out_specs=pl.BlockSpec((ts, LANES), lambda j, i: (i, j)),
        out_shape=jax.ShapeDtypeStruct((s, 4 * LANES), BF16),
    )(p1, p1, gn, w, bias, seg)


def _sgu_bwd(p1, dy, gn, w, wt, bias, seg, tril, *, ts=512, name):
    s = p1.shape[0]
    ts = _tile_rows(ts, s)
    nt = s // ts

    def body(zu_ref, zg_ref, dy_ref, gn_ref, w_ref, wt_ref, bias_ref, seg_ref, tril_ref,
             dzu_ref, dzg_ref, dw_ref, dbias_ref, dgn_ref):
        i = pl.program_id(1)
        zu = zu_ref[...]
        zg = zg_ref[...]
        gn_v = gn_ref[...]
        segv = seg_ref[...]
        u, tu, g, tg, rs, ghat, gvb, mixed = _sgu_forward_block(zu, zg, gn_v, w_ref, bias_ref[...], segv)
        dyv = dy_ref[...]
        du = dyv * mixed
        dmx = dyv * u

        @pl.when(i == 0)
        def _():
            dw_ref[...] = jnp.zeros_like(dw_ref)
            dbias_ref[...] = jnp.zeros_like(dbias_ref)
            dgn_ref[...] = jnp.zeros_like(dgn_ref)

        lane = _lanes((CHUNK, LANES))
        dgv_chunks = []
        dbias = jnp.zeros((CHUNK, LANES), F32)
        for c in range(ts // CHUNK):
            dmc = dmx[c * CHUNK:(c + 1) * CHUNK]
            gc = gvb[c * CHUNK:(c + 1) * CHUNK]
            dm_a = jnp.where(lane < 64, dmc, 0.0).astype(BF16)
            dm_b = jnp.where(lane >= 64, dmc, 0.0).astype(BF16)
            dw_ref[0] += _dot_nt(dm_a, gc)
            dw_ref[1] += _dot_nt(dm_b, gc)
            dgv_chunks.append(_dot(wt_ref[0], dm_a) + _dot(wt_ref[1], dm_b))
            dbias = dbias + dmc
        dbias_ref[...] += dbias
        dgv = dgv_chunks[0] if len(dgv_chunks) == 1 else jnp.concatenate(dgv_chunks, axis=0)
        dgn_ref[...] += jnp.sum(dgv * ghat, axis=0, keepdims=True)
        dgh = dgv * gn_v
        dg = rs * (dgh - ghat * _dot_split(dgh * ghat, segv))
        dzu_ref[...] = (du * _gelu_grad(zu, tu)).astype(BF16)
        dzg_ref[...] = (dg * _gelu_grad(zg, tg)).astype(BF16)

        @pl.when(i == nt - 1)
        def _():
            dw_ref[0] = dw_ref[0] * tril_ref[...]
            dw_ref[1] = dw_ref[1] * tril_ref[...]

    f = jax.ShapeDtypeStruct
    colj = pl.BlockSpec((ts, LANES), lambda j, i: (i, j))
    wsp = pl.BlockSpec((2, CHUNK, CHUNK), lambda j, i: (j, 0, 0))
    sq = pl.BlockSpec((LANES, LANES), lambda j, i: (0, 0))
    return pl.pallas_call(
        body, name=name, grid=(4, nt),
        in_specs=[colj, pl.BlockSpec((ts, LANES), lambda j, i: (i, 4 + j)), colj,
                  pl.BlockSpec((1, LANES), lambda j, i: (0, j)), wsp, wsp,
                  pl.BlockSpec((CHUNK, LANES), lambda j, i: (0, j)), sq, sq],
        out_specs=(colj, colj, wsp, pl.BlockSpec((CHUNK, LANES), lambda j, i: (0, j)),
                   pl.BlockSpec((1, LANES), lambda j, i: (0, j))),
        out_shape=(f((s, 4 * LANES), BF16), f((s, 4 * LANES), BF16), f((8, CHUNK, CHUNK), F32),
                   f((CHUNK, 4 * LANES), F32), f((1, 4 * LANES), F32)),
    )(p1, p1, dy, gn, w, wt, bias, seg, tril)


F_COL = 20


def _fcum_fwd(p1, bf, *, ts=512, name):
    s = p1.shape[0]
    ts = _tile_rows(ts, s)

    def body(f_ref, bf_ref, c_ref, car_ref):
        i = pl.program_id(0)
        z = f_ref[...] + bf_ref[...]
        logf = jnp.minimum(z, 0.0) - _log1p_pos(jnp.exp(-jnp.abs(z)))

        @pl.when(i == 0)
        def _():
            car_ref[...] = jnp.zeros_like(car_ref)

        c_ref[...] = _cumsum_fwd(logf) + car_ref[0:1, :]
        car_ref[0:1, :] = c_ref[ts - 1:ts, :]

    return pl.pallas_call(
        body, name=name, grid=(s // ts,),
        in_specs=[pl.BlockSpec((ts, LANES), lambda i: (i, F_COL)), pl.BlockSpec((1, LANES), lambda i: (0, 0))],
        out_specs=pl.BlockSpec((ts, LANES), lambda i: (i, 0)),
        out_shape=jax.ShapeDtypeStruct((s, LANES), F32),
        scratch_shapes=[pltpu.VMEM((SUBLANES, LANES), F32)],
    )(p1, bf)


def _fcum_bwd(dcs, p1, bf, *, ts=512, name):
    s = p1.shape[0]
    ts = _tile_rows(ts, s)
    nt = s // ts

    def body(dc_ref, f_ref, bf_ref, df_ref, dbf_ref, car_ref):
        i = pl.program_id(0)

        @pl.when(i == 0)
        def _():
            car_ref[...] = jnp.zeros_like(car_ref)
            dbf_ref[...] = jnp.zeros_like(dbf_ref)

        dlog = _cumsum_rev(dc_ref[...]) + car_ref[0:1, :]
        car_ref[...] = dlog[:SUBLANES]
        z = f_ref[...] + bf_ref[...]
        df = dlog * _sigmoid(-z)
        df_ref[...] = df.astype(BF16)
        dbf_ref[...] += jnp.sum(df, axis=0, keepdims=True)

    return pl.pallas_call(
        body, name=name, grid=(nt,),
        in_specs=[pl.BlockSpec((ts, LANES), lambda i: (nt - 1 - i, 0)),
                  pl.BlockSpec((ts, LANES), lambda i: (nt - 1 - i, F_COL)),
                  pl.BlockSpec((1, LANES), lambda i: (0, 0))],
        out_specs=(pl.BlockSpec((ts, LANES), lambda i: (nt - 1 - i, 0)), pl.BlockSpec((1, LANES), lambda i: (0, 0))),
        out_shape=(jax.ShapeDtypeStruct((s, LANES), BF16), jax.ShapeDtypeStruct((1, LANES), F32)),
        scratch_shapes=[pltpu.VMEM((SUBLANES, LANES), F32)],
    )(dcs, p1, bf)


def _fox_scores(qm, kb, cq, ck, qi, kj, tq, tk):
    sc = _dot_nt(qm, kb) + jnp.tile(cq, (1, tk // LANES)) - ck
    row = _rows((tq, tk)) + qi * tq
    col = _lanes((tq, tk)) + kj * tk
    return jnp.where(col <= row, sc, NEG)


def _fox_fwd(p1, cq, ck, *, tq=512, name):
    s = p1.shape[0]
    tq = _tile_rows(tq, s)
    tk = tq
    nq = s // tq

    def body(q_ref, k_ref, v_ref, cq_ref, ck_ref, o_ref, lse_ref, m_ref, l_ref, acc_ref):
        qi = pl.program_id(1)
        kj = pl.program_id(2)

        @pl.when(kj == 0)
        def _():
            m_ref[...] = jnp.full_like(m_ref, NEG)
            l_ref[...] = jnp.zeros_like(l_ref)
            acc_ref[...] = jnp.zeros_like(acc_ref)

        @pl.when(kj <= qi)
        def _():
            q = q_ref[...] * 0.125
            kb = k_ref[...].astype(BF16)
            vb = v_ref[...].astype(BF16)
            lane = _lanes((tq, LANES))
            outs = []
            for hh in range(2):
                sel = (lane < 64) if hh == 0 else (lane >= 64)
                qm = jnp.where(sel, q, 0.0).astype(BF16)
                sc = _fox_scores(qm, kb, cq_ref[:, hh * LANES:(hh + 1) * LANES], ck_ref[hh], qi, kj, tq, tk)
                m_prev = m_ref[hh]
                m_new = jnp.maximum(m_prev, jnp.max(sc, axis=1, keepdims=True))
                pm = jnp.exp(sc - jnp.tile(m_new, (1, tk // LANES)))
                alpha = jnp.exp(m_prev - m_new)
                l_ref[hh] = alpha * l_ref[hh] + jnp.sum(pm, axis=1, keepdims=True)
                m_ref[hh] = m_new
                outs.append(acc_ref[...] * alpha + _dot(pm.astype(BF16), vb))
            acc_ref[...] = jnp.where(lane < 64, outs[0], outs[1])

        @pl.when(kj == qi)
        def _():
            lane = _lanes((tq, LANES))
            l_sel = jnp.where(lane < 64, l_ref[0], l_ref[1])
            o_ref[...] = (acc_ref[...] / l_sel).astype(BF16)
            lse_ref[:, 0:LANES] = m_ref[0] + jnp.log(l_ref[0])
            lse_ref[:, LANES:2 * LANES] = m_ref[1] + jnp.log(l_ref[1])

    def kmap(col0):
        return lambda j, qi, kj: (jnp.minimum(kj, qi), col0 + j)

    return pl.pallas_call(
        body, name=name, grid=(4, nq, nq),
        in_specs=[pl.BlockSpec((tq, LANES), lambda j, qi, kj: (qi, 8 + j)),
                  pl.BlockSpec((tk, LANES), kmap(12)), pl.BlockSpec((tk, LANES), kmap(16)),
                  pl.BlockSpec((tq, 2 * LANES), lambda j, qi, kj: (qi, j)),
                  pl.BlockSpec((2, 1, tk), lambda j, qi, kj: (j, 0, jnp.minimum(kj, qi)))],
        out_specs=(pl.BlockSpec((tq, LANES), lambda j, qi, kj: (qi, j)),
                   pl.BlockSpec((tq, 2 * LANES), lambda j, qi, kj: (qi, j))),
        out_shape=(jax.ShapeDtypeStruct((s, 4 * LANES), BF16), jax.ShapeDtypeStruct((s, 8 * LANES), F32)),
        scratch_shapes=[pltpu.VMEM((2, tq, LANES), F32), pltpu.VMEM((2, tq, LANES), F32),
                        pltpu.VMEM((tq, LANES), F32)],
    )(p1, p1, p1, cq, ck)


def _fox_delta(dy, o, sel, *, ts=512, name):
    s = o.shape[0]
    ts = _tile_rows(ts, s)

    def body(do_ref, o_ref, sel_ref, d_ref):
        prod = do_ref[...] * o_ref[...].astype(F32)
        d_ref[:, 0:LANES] = _dot_split(prod, sel_ref[0])
        d_ref[:, LANES:2 * LANES] = _dot_split(prod, sel_ref[1])

    return pl.pallas_call(
        body, name=name, grid=(4, s // ts),
        in_specs=[pl.BlockSpec((ts, LANES), lambda j, i: (i, 4 + j)),
                  pl.BlockSpec((ts, LANES), lambda j, i: (i, j)),
                  pl.BlockSpec((2, LANES, LANES), lambda j, i: (0, 0, 0))],
        out_specs=pl.BlockSpec((ts, 2 * LANES), lambda j, i: (i, j)),
        out_shape=jax.ShapeDtypeStruct((s, 8 * LANES), F32),
    )(dy, o, sel)


def _fox_dkv(p1, dy, lse, delta, cq, ck, *, tq=512, name):
    s = p1.shape[0]
    tq = _tile_rows(tq, s)
    tk = tq
    nq = s // tq

    def body(q_ref, k_ref, v_ref, do_ref, lse_ref, dl_ref, cq_ref, ck_ref,
             dk_ref, dv_ref, dck_ref, dka_ref, dva_ref, dca_ref):
        kj = pl.program_id(1)
        qi = pl.program_id(2)

        @pl.when(qi == 0)
        def _():
            dka_ref[...] = jnp.zeros_like(dka_ref)
            dva_ref[...] = jnp.zeros_like(dva_ref)
            dca_ref[...] = jnp.zeros_like(dca_ref)

        @pl.when(qi >= kj)
        def _():
            q = q_ref[...] * 0.125
            kb = k_ref[...].astype(BF16)
            vb = v_ref[...].astype(BF16)
            do = do_ref[...]
            lane = _lanes((tq, LANES))
            for hh in range(2):
                sel = (lane < 64) if hh == 0 else (lane >= 64)
                qm = jnp.where(sel, q, 0.0).astype(BF16)
                dom = jnp.where(sel, do, 0.0).astype(BF16)
                sc = _fox_scores(qm, kb, cq_ref[:, hh * LANES:(hh + 1) * LANES], ck_ref[hh], qi, kj, tq, tk)
                pm = jnp.exp(sc - jnp.tile(lse_ref[:, hh * LANES:(hh + 1) * LANES], (1, tk // LANES)))
                dva_ref[...] += _dot_tn(pm.astype(BF16), dom)
                dp = _dot_nt(dom, vb)
                ds = pm * (dp - jnp.tile(dl_ref[:, hh * LANES:(hh + 1) * LANES], (1, tk // LANES)))
                dka_ref[...] += _dot_tn(ds.astype(BF16), qm)
                dca_ref[hh] -= jnp.sum(ds, axis=0, keepdims=True)

        @pl.when(qi == nq - 1)
        def _():
            dk_ref[...] = dka_ref[...].astype(BF16)
            dv_ref[...] = dva_ref[...].astype(BF16)
            dck_ref[...] = dca_ref[...]

    def qmap(col0):
        return lambda j, kj, qi: (jnp.maximum(qi, kj), col0 + j)

    pair = pl.BlockSpec((tq, 2 * LANES), qmap(0))
    return pl.pallas_call(
        body, name=name, grid=(4, nq, nq),
        in_specs=[pl.BlockSpec((tq, LANES), qmap(8)),
                  pl.BlockSpec((tk, LANES), lambda j, kj, qi: (kj, 12 + j)),
                  pl.BlockSpec((tk, LANES), lambda j, kj, qi: (kj, 16 + j)),
                  pl.BlockSpec((tq, LANES), qmap(4)), pair, pair, pair,
                  pl.BlockSpec((2, 1, tk), lambda j, kj, qi: (j, 0, kj))],
        out_specs=(pl.BlockSpec((tk, LANES), lambda j, kj, qi: (kj, j)),
                   pl.BlockSpec((tk, LANES), lambda j, kj, qi: (kj, j)),
                   pl.BlockSpec((2, 1, tk), lambda j, kj, qi: (j, 0, kj))),
        out_shape=(jax.ShapeDtypeStruct((s, 4 * LANES), BF16), jax.ShapeDtypeStruct((s, 4 * LANES), BF16),
                   jax.ShapeDtypeStruct((8, 1, s), F32)),
        scratch_shapes=[pltpu.VMEM((tk, LANES), F32), pltpu.VMEM((tk, LANES), F32), pltpu.VMEM((2, 1, tk), F32)],
    )(p1, p1, p1, dy, lse, delta, cq, ck)


def _fox_dq(p1, dy, lse, delta, cq, ck, *, tq=512, name):
    s = p1.shape[0]
    tq = _tile_rows(tq, s)
    tk = tq
    nq = s // tq

    def body(q_ref, k_ref, v_ref, do_ref, lse_ref, dl_ref, cq_ref, ck_ref, dq_ref, dcq_ref, dqa_ref, dca_ref):
        qi = pl.program_id(1)
        kj = pl.program_id(2)

        @pl.when(kj == 0)
        def _():
            dqa_ref[...] = jnp.zeros_like(dqa_ref)
            dca_ref[...] = jnp.zeros_like(dca_ref)

        @pl.when(kj <= qi)
        def _():
            q = q_ref[...] * 0.125
            kf = k_ref[...]
            kb = kf.astype(BF16)
            vb = v_ref[...].astype(BF16)
            do = do_ref[...]
            lane = _lanes((tq, LANES))
            klane = _lanes((tk, LANES))
            for hh in range(2):
                sel = (lane < 64) if hh == 0 else (lane >= 64)
                ksel = (klane < 64) if hh == 0 else (klane >= 64)
                qm = jnp.where(sel, q, 0.0).astype(BF16)
                dom = jnp.where(sel, do, 0.0).astype(BF16)
                km = jnp.where(ksel, kf, 0.0).astype(BF16)
                sc = _fox_scores(qm, kb, cq_ref[:, hh * LANES:(hh + 1) * LANES], ck_ref[hh], qi, kj, tq, tk)
                pm = jnp.exp(sc - jnp.tile(lse_ref[:, hh * LANES:(hh + 1) * LANES], (1, tk // LANES)))
                dp = _dot_nt(dom, vb)
                ds = pm * (dp - jnp.tile(dl_ref[:, hh * LANES:(hh + 1) * LANES], (1, tk // LANES)))
                dqa_ref[...] += _dot(ds.astype(BF16), km)
                dca_ref[hh] += jnp.sum(ds, axis=1, keepdims=True)

        @pl.when(kj == qi)
        def _():
            dq_ref[...] = (dqa_ref[...] * 0.125).astype(BF16)
            dcq_ref[:, 0:LANES] = dca_ref[0]
            dcq_ref[:, LANES:2 * LANES] = dca_ref[1]

    def kmap(col0):
        return lambda j, qi, kj: (jnp.minimum(kj, qi), col0 + j)

    pair = pl.BlockSpec((tq, 2 * LANES), lambda j, qi, kj: (qi, j))
    return pl.pallas_call(
        body, name=name, grid=(4, nq, nq),
        in_specs=[pl.BlockSpec((tq, LANES), lambda j, qi, kj: (qi, 8 + j)),
                  pl.BlockSpec((tk, LANES), kmap(12)), pl.BlockSpec((tk, LANES), kmap(16)),
                  pl.BlockSpec((tq, LANES), lambda j, qi, kj: (qi, 4 + j)), pair, pair, pair,
                  pl.BlockSpec((2, 1, tk), lambda j, qi, kj: (j, 0, jnp.minimum(kj, qi)))],
        out_specs=(pl.BlockSpec((tq, LANES), lambda j, qi, kj: (qi, j)), pair),
        out_shape=(jax.ShapeDtypeStruct((s, 4 * LANES), BF16), jax.ShapeDtypeStruct((s, 8 * LANES), F32)),
        scratch_shapes=[pltpu.VMEM((tq, LANES), F32), pltpu.VMEM((2, tq, LANES), F32)],
    )(p1, p1, p1, dy, lse, delta, cq, ck)


def _row_block(r, cap=256):
    best = None
    for rb in range(SUBLANES, min(r, cap) + 1, SUBLANES):
        if r % rb == 0:
            best = rb
    return r if best is None else best


def _adamw(w, g, m, v, *, name):
    r, c = w.shape
    rb = _row_block(r)

    def body(w_ref, g_ref, m_ref, v_ref, d_ref, nm_ref, nv_ref):
        gv = g_ref[...]
        mn = ADAM_B1 * m_ref[...] + (1.0 - ADAM_B1) * gv
        vn = ADAM_B2 * v_ref[...] + (1.0 - ADAM_B2) * (gv * gv)
        m_hat = mn / ADAM_C1
        v_hat = vn / ADAM_C2
        d_ref[...] = (-ADAM_LR) * (m_hat / (jnp.sqrt(v_hat) + ADAM_EPS) + ADAM_WD * w_ref[...])
        nm_ref[...] = mn
        nv_ref[...] = vn

    blk = pl.BlockSpec((rb, c), lambda i: (i, 0))
    shp = jax.ShapeDtypeStruct((r, c), F32)
    return pl.pallas_call(
        body, name=name, grid=(r // rb,), in_specs=[blk] * 4, out_specs=(blk,) * 3, out_shape=(shp,) * 3,
    )(w, g, m, v)


def _pair_sum(g, ra, core, *, name):
    _, _, rh, c = g.shape
    rb = _row_block(rh)

    def body(core_ref, g_ref, ra_ref, h_ref):
        h_ref[...] = g_ref[...] + ra_ref[...]

    return pl.pallas_call(
        body, name=name,
        grid_spec=pltpu.PrefetchScalarGridSpec(
            num_scalar_prefetch=1, grid=(N_CHIPS, rh // rb),
            in_specs=[pl.BlockSpec((None, None, rb, c), lambda k, i, core_ref: (k, core_ref[0], i, 0)),
                      pl.BlockSpec((None, rb, c), lambda k, i, core_ref: (k, i, 0))],
            out_specs=pl.BlockSpec((None, rb, c), lambda k, i, core_ref: (k, i, 0))),
        out_shape=jax.ShapeDtypeStruct((N_CHIPS, rh, c), F32),
    )(core, g, ra)


def _chip_sum(h, rb3, chip, *, name):
    _, rh, c = h.shape
    rb = _row_block(rh)

    def body(chip_ref, h_ref, r0_ref, r1_ref, r2_ref, t_ref):
        t_ref[...] = ((h_ref[...] + r0_ref[...]) + r1_ref[...]) + r2_ref[...]

    def slot(n):
        return pl.BlockSpec((None, rb, c), lambda i, chip_ref: (n, i, 0))

    return pl.pallas_call(
        body, name=name,
        grid_spec=pltpu.PrefetchScalarGridSpec(
            num_scalar_prefetch=1, grid=(rh // rb,),
            in_specs=[pl.BlockSpec((None, rb, c), lambda i, chip_ref: (chip_ref[0], i, 0)),
                      slot(0), slot(1), slot(2)],
            out_specs=pl.BlockSpec((rb, c), lambda i, chip_ref: (i, 0))),
        out_shape=jax.ShapeDtypeStruct((rh, c), F32),
    )(chip, h, rb3, rb3, rb3)


ANY = pl.BlockSpec(memory_space=pl.ANY)


def _mesh_pos():
    return lax.axis_index("x"), lax.axis_index("y"), lax.axis_index("c")


def _other_chips(x, y):
    return [(1 - x, y), (x, 1 - y), (1 - x, 1 - y)]


def _remote(src, dst, ssem, rsem, dev):
    return pltpu.make_async_remote_copy(src_ref=src, dst_ref=dst, send_sem=ssem, recv_sem=rsem,
                                        device_id=dev, device_id_type=MESH)


def _all_gather(shards, *, name):
    n = len(shards)

    def body(*refs):
        ins, outs = refs[:n], refs[n:2 * n]
        ssem, rsem, lsem = refs[2 * n:]
        x, y, c = _mesh_pos()
        me = 2 * x + y
        sib = (x, y, 1 - c)
        chips = _other_chips(x, y)
        local = []
        for a in range(n):
            cp = pltpu.make_async_copy(ins[a], outs[a].at[me], lsem.at[a])
            cp.start()
            local.append(cp)
        sends = []
        for a in range(n):
            for j, (cx, cy) in enumerate(chips):
                cp = _remote(ins[a].at[c], outs[a].at[me, c], ssem.at[a, j], rsem.at[a, j], (cx, cy, c))
                cp.start()
                sends.append(cp)
        for a in range(n):
            for j, (cx, cy) in enumerate(chips):
                slab = outs[a].at[2 * cx + cy, c]
                _remote(slab, slab, ssem.at[a, j], rsem.at[a, j], (cx, cy, c)).wait_recv()
                cp = _remote(slab, slab, ssem.at[a, 3 + j], rsem.at[a, 3 + j], sib)
                cp.start()
                sends.append(cp)
        for a in range(n):
            for j, (cx, cy) in enumerate(chips):
                slab = outs[a].at[2 * cx + cy, 1 - c]
                _remote(slab, slab, ssem.at[a, 3 + j], rsem.at[a, 3 + j], sib).wait_recv()
        for cp in sends:
            cp.wait_send()
        for cp in local:
            cp.wait()

    return pl.pallas_call(
        body, name=name, in_specs=[ANY] * n, out_specs=[ANY] * n,
        out_shape=[jax.ShapeDtypeStruct((N_CHIPS,) + a.shape, a.dtype) for a in shards],
        scratch_shapes=[pltpu.SemaphoreType.DMA((n, 6)), pltpu.SemaphoreType.DMA((n, 6)),
                        pltpu.SemaphoreType.DMA((n,))],
    )(*shards)


def _send_other_half(grads, *, name):
    n = len(grads)

    def body(*refs):
        ins, outs = refs[:n], refs[n:2 * n]
        ssem, rsem = refs[2 * n:]
        x, y, c = _mesh_pos()
        sib = (x, y, 1 - c)
        sends = []
        for a in range(n):
            for k in range(N_CHIPS):
                cp = _remote(ins[a].at[k, 1 - c], outs[a].at[k], ssem.at[a, k], rsem.at[a, k], sib)
                cp.start()
                sends.append(cp)
        for cp in sends:
            cp.wait()

    return pl.pallas_call(
        body, name=name, in_specs=[ANY] * n, out_specs=[ANY] * n,
        out_shape=[jax.ShapeDtypeStruct((N_CHIPS,) + g.shape[2:], g.dtype) for g in grads],
        scratch_shapes=[pltpu.SemaphoreType.DMA((n, N_CHIPS)), pltpu.SemaphoreType.DMA((n, N_CHIPS))],
    )(*grads)


def _send_to_owner(sums, *, name):
    n = len(sums)

    def body(*refs):
        ins, outs = refs[:n], refs[n:2 * n]
        ssem, rsem = refs[2 * n:]
        x, y, c = _mesh_pos()
        chips = _other_chips(x, y)
        sends = []
        for a in range(n):
            for j, (cx, cy) in enumerate(chips):
                cp = _remote(ins[a].at[2 * cx + cy], outs[a].at[j], ssem.at[a, j], rsem.at[a, j], (cx, cy, c))
                cp.start()
                sends.append(cp)
        for cp in sends:
            cp.wait()

    return pl.pallas_call(
        body, name=name, in_specs=[ANY] * n, out_specs=[ANY] * n,
        out_shape=[jax.ShapeDtypeStruct((3,) + h.shape[1:], h.dtype) for h in sums],
        scratch_shapes=[pltpu.SemaphoreType.DMA((n, 3)), pltpu.SemaphoreType.DMA((n, 3))],
    )(*sums)


def _swap_halves(halves, *, name):
    n = len(halves)

    def body(*refs):
        ins, outs = refs[:n], refs[n:2 * n]
        ssem, rsem, lsem = refs[2 * n:]
        x, y, c = _mesh_pos()
        sib = (x, y, 1 - c)
        cps = []
        for a in range(n):
            loc = pltpu.make_async_copy(ins[a], outs[a].at[c], lsem.at[a])
            loc.start()
            cp = _remote(ins[a], outs[a].at[c], ssem.at[a], rsem.at[a], sib)
            cp.start()
            cps.append((loc, cp))
        for a, (loc, cp) in enumerate(cps):
            cp.wait_send()
            _remote(ins[a], outs[a].at[1 - c], ssem.at[a], rsem.at[a], sib).wait_recv()
            loc.wait()

    return pl.pallas_call(
        body, name=name, in_specs=[ANY] * n, out_specs=[ANY] * n,
        out_shape=[jax.ShapeDtypeStruct((2,) + t.shape, t.dtype) for t in halves],
        scratch_shapes=[pltpu.SemaphoreType.DMA((n,)), pltpu.SemaphoreType.DMA((n,)),
                        pltpu.SemaphoreType.DMA((n,))],
    )(*halves)


def _all_reduce_small(buf, *, name):
    r = buf.shape[0]
    rh = r // 2

    def body(in_ref, out_ref, x1_ref, x2_ref, ssem, rsem):
        x, y, c = _mesh_pos()
        me = 2 * x + y
        sib = (x, y, 1 - c)
        chips = _other_chips(x, y)
        cp = _remote(in_ref, x1_ref, ssem.at[0], rsem.at[0], sib)
        cp.start()
        cp.wait()
        off = pl.multiple_of(c * rh, SUBLANES)
        x2_ref[me] = in_ref[pl.ds(off, rh), :] + x1_ref[pl.ds(off, rh), :]
        sends = []
        for j, (cx, cy) in enumerate(chips):
            s = _remote(x2_ref.at[me], x2_ref.at[me], ssem.at[1 + j], rsem.at[1 + j], (cx, cy, c))
            s.start()
            sends.append(s)
        for j, (cx, cy) in enumerate(chips):
            slot = x2_ref.at[2 * cx + cy]
            _remote(slot, slot, ssem.at[1 + j], rsem.at[1 + j], (cx, cy, c)).wait_recv()
        out_ref[pl.ds(off, rh), :] = ((x2_ref[0] + x2_ref[1]) + x2_ref[2]) + x2_ref[3]
        for s in sends:
            s.wait_send()
        mine = out_ref.at[pl.ds(off, rh), :]
        s3 = _remote(mine, mine, ssem.at[4], rsem.at[4], sib)
        s3.start()
        off2 = pl.multiple_of((1 - c) * rh, SUBLANES)
        theirs = out_ref.at[pl.ds(off2, rh), :]
        _remote(theirs, theirs, ssem.at[4], rsem.at[4], sib).wait_recv()
        s3.wait_send()

    vm = pl.BlockSpec(memory_space=pltpu.VMEM)
    return pl.pallas_call(
        body, name=name, in_specs=[vm], out_specs=vm,
        out_shape=jax.ShapeDtypeStruct((r, LANES), F32),
        scratch_shapes=[pltpu.VMEM((r, LANES), F32), pltpu.VMEM((N_CHIPS, rh, LANES), F32),
                        pltpu.SemaphoreType.DMA((5,)), pltpu.SemaphoreType.DMA((5,))],
    )(buf)


PACK_ALIGN = 2 * SUBLANES * LANES


def _pack(arrays, rows_multiple=2 * SUBLANES):
    parts, offs, off = [], [], 0
    for a in arrays:
        flat = a.reshape(-1).astype(F32)
        padded = -(-flat.shape[0] // PACK_ALIGN) * PACK_ALIGN
        parts.append(jnp.pad(flat, (0, padded - flat.shape[0])))
        offs.append(off)
        off += padded
    buf = jnp.concatenate(parts).reshape(-1, LANES)
    return buf, offs


def _unpack(buf, offs, shapes):
    flat = buf.reshape(-1)
    out = []
    for off, shp in zip(offs, shapes):
        size = 1
        for d in shp:
            size *= d
        out.append(flat[off:off + size].reshape(shp))
    return out


def _cols_from_shards(g4):
    _, k, ns = g4.shape
    return jnp.transpose(g4, (1, 0, 2)).reshape(k, N_CHIPS * ns)


def _cols_to_shards(w):
    k, n = w.shape
    return jnp.transpose(w.reshape(k, N_CHIPS, n // N_CHIPS), (1, 0, 2))


def _block_cols(w, parts, blocks):
    lead = w.shape[:-1]
    width = w.shape[-1] // (parts * blocks)
    w = w.reshape(lead + (parts, blocks, width))
    w = jnp.swapaxes(w, -3, -2)
    return w.reshape(lead + (parts * blocks * width,))


def _unblock_cols(w, parts, blocks):
    lead = w.shape[:-1]
    width = w.shape[-1] // (parts * blocks)
    w = w.reshape(lead + (blocks, parts, width))
    w = jnp.swapaxes(w, -3, -2)
    return w.reshape(lead + (parts * blocks * width,))


def _pair_blockdiag(w8):
    w = w8.reshape(4, 2, 64, 64)
    z = jnp.zeros((4, 64, 64), w8.dtype)
    top = jnp.concatenate([w[:, 0], z], axis=2)
    bot = jnp.concatenate([z, w[:, 1]], axis=2)
    return jnp.concatenate([top, bot], axis=1)


def _pair_diag_blocks(w4):
    a = w4[:, :64, :64]
    b = w4[:, 64:, 64:]
    return jnp.stack([a, b], axis=1).reshape(8, 64, 64)


def _local_step(x, target, wts):
    s = x.shape[0]
    g = {}

    win0 = wts["w_in0"]
    wout0 = wts["w_out0"]
    win1 = wts["w_in1"]
    wout1 = wts["w_out1"]
    wup = wts["w_up"]
    wdown = wts["w_down"]
    w4, b4, w3, b3 = wts["w4"], wts["b4"], wts["w3"], wts["b3"]
    wa, wx = wts["wa"], wts["wx"]
    wat, wxt = jnp.swapaxes(wa, 1, 2), jnp.swapaxes(wx, 1, 2)
    ba, bx, lam = wts["ba"], wts["bx"], wts["lam"]
    fcw, fcb = wts["ffn_cw"], wts["ffn_cb"]
    sgu_w, sgu_wt = wts["sgu_w"], wts["sgu_wt"]
    sgu_bias, sgu_gn = wts["sgu_bias"], wts["sgu_gn"]
    bf = wts["bf"]

    lane = jnp.arange(LANES)
    seg = jnp.where((lane[:, None] // 64) == (lane[None, :] // 64), 1.0 / 64.0, 0.0).astype(BF16)
    sel = jnp.stack([jnp.broadcast_to((lane[:, None] < 64), (LANES, LANES)),
                     jnp.broadcast_to((lane[:, None] >= 64), (LANES, LANES))]).astype(BF16)
    tril = (lane[:, None] >= lane[None, :]).astype(F32)

    n0 = _norm_fwd(x, wts["g_mix0"], name="norm_mix0")
    p0 = _mm([n0], win0, nb=640, name="mm_in0")
    ya, yb, hl = _even_core_fwd(p0, w4, b4, wa, ba, wx, bx, lam, w3, b3, name="even_fwd")
    h1 = _mm([ya, yb], wout0, res=x, name="mm_out0")

    def ffn_fwd(h, layer):
        n = _norm_fwd(h, wts["g_ffn"][layer], name=f"norm_ffn{layer}")
        up = _mm([n], wup[layer], out_dtype=BF16, nb=1408, name=f"mm_up{layer}")
        act = _ffn_core_fwd(up, fcw[layer], fcb[layer], name=f"ffn_fwd{layer}")
        hn = _mm([act], wdown[layer], res=h, name=f"mm_down{layer}")
        return n, up, act, hn

    n1, up0, act0, h2 = ffn_fwd(h1, 0)

    n2 = _norm_fwd(h2, wts["g_mix1"], name="norm_mix1")
    p1 = _mm([n2], win1, nb=896, name="mm_in1")
    yc = _sgu_fwd(p1, sgu_gn, sgu_w, sgu_bias, seg, name="sgu_fwd")
    cum = _fcum_fwd(p1, bf, name="fcum_fwd")
    c8 = cum[:, :8]
    cq = jnp.broadcast_to(c8[:, :, None], (s, 8, LANES)).reshape(s, 8 * LANES)
    ck = jnp.transpose(c8).reshape(8, 1, s)
    yd, lse = _fox_fwd(p1, cq, ck, name="fox_fwd")
    h3 = _mm([yc, yd], wout1, res=h2, name="mm_out1")

    n3, up1, act1, h4 = ffn_fwd(h3, 1)
    dh4, loss, g["final_norm"] = _final(h4, wts["g_final"], target, name="final")

    def ffn_bwd(dh, h, n, up, act, layer):
        dact = _mm([dh], wdown[layer], trans_w=True, out_dtype=BF16, nb=1408, name=f"mm_dact{layer}")
        dwd = _mm_tn([act], dh, nb=512, name=f"mm_dwdown{layer}")
        dup, dcw, dcb = _ffn_core_bwd(dact, up, fcw[layer], fcb[layer], name=f"ffn_bwd{layer}")
        dn = _mm([dup], wup[layer], trans_w=True, nb=512, name=f"mm_dn_ffn{layer}")
        dwu = _mm_tn([n], dup, nb=1408, name=f"mm_dwup{layer}")
        dhn, dg = _norm_bwd(dn, h, wts["g_ffn"][layer], dh, name=f"norm_bwd_ffn{layer}")
        return dhn, dwd, dwu, dcw, dcb, dg

    dh3, g["w_down1"], g["w_up1"], g["ffn_cw1"], g["ffn_cb1"], g["g_ffn1"] = ffn_bwd(dh4, h3, n3, up1, act1, 1)

    dy1 = _mm([dh3], wout1, trans_w=True, name="mm_dy1")
    g["w_out1"] = _mm_tn([yc, yd], dh3, nb=512, name="mm_dwout1")
    dzu, dzg, g["sgu_w"], g["sgu_bias"], g["sgu_gn"] = _sgu_bwd(
        p1, dy1, sgu_gn, sgu_w, sgu_wt, sgu_bias, seg, tril, name="sgu_bwd")
    delta = _fox_delta(dy1, yd, sel, name="fox_delta")
    dk, dv, dck = _fox_dkv(p1, dy1, lse, delta, cq, ck, name="fox_dkv")
    dq, dcq = _fox_dq(p1, dy1, lse, delta, cq, ck, name="fox_dq")
    dcs = jnp.pad(jnp.transpose(dck.reshape(8, s)) + dcq.reshape(s, 8, LANES)[:, :, 0], ((0, 0), (0, LANES - 8)))
    df, g["bf"] = _fcum_bwd(dcs, p1, bf, name="fcum_bwd")
    dp1 = jnp.concatenate([dzu, dzg, dq, dk, dv, df], axis=1)
    dn2 = _mm([dp1], win1, trans_w=True, name="mm_dn_mix1")
    g["w_in1"] = _mm_tn([n2], dp1, nb=896, name="mm_dwin1")
    dh2, g["g_mix1"] = _norm_bwd(dn2, h2, wts["g_mix1"], dh3, name="norm_bwd_mix1")

    dh1, g["w_down0"], g["w_up0"], g["ffn_cw0"], g["ffn_cb0"], g["g_ffn0"] = ffn_bwd(dh2, h1, n1, up0, act0, 0)

    dy0 = _mm([dh1], wout0, trans_w=True, name="mm_dy0")
    g["w_out0"] = _mm_tn([ya, yb], dh1, nb=512, name="mm_dwout0")
    (dp0, g["w4"], g["b4"], g["wa"], g["ba"], g["wx"], g["bx"], g["lam"], g["w3"], g["b3"]) = _even_core_bwd(
        dy0, p0, hl, w4, b4, wa, wat, ba, wx, wxt, bx, lam, w3, b3, name="even_bwd")
    dn0 = _mm([dp0], win0, trans_w=True, name="mm_dn_mix0")
    g["w_in0"] = _mm_tn([n0], dp0, nb=640, name="mm_dwin0")
    grad_x, g["g_mix0"] = _norm_bwd(dn0, x, wts["g_mix0"], dh1, name="norm_bwd_mix0")
    return loss, grad_x, g


def _prepare_weights(nat):
    lane = jnp.arange(LANES)
    tril = (lane[:, None] >= lane[None, :]).astype(F32)
    sgu_tril = nat["sgu_w"][0] * tril
    w_in1 = nat["mix1_w_in"]
    nblk = D_FF // FFN_CB
    return {
        "w_in0": _block_cols(nat["mix0_w_in"], 5, 4),
        "w_out0": nat["mix0_w_out"],
        "w_in1": jnp.pad(w_in1, ((0, 0), (0, 21 * LANES - w_in1.shape[1]))),
        "w_out1": nat["mix1_w_out"],
        "w_up": [_block_cols(nat["ffn_up"][l], 2, nblk) for l in range(2)],
        "w_down": [nat["ffn_down"][l] for l in range(2)],
        "w4": nat["lru_conv_w"], "b4": nat["lru_conv_b"], "w3": nat["sconv_w"], "b3": nat["sconv_b"],
        "wa": _pair_blockdiag(nat["lru_wa"][0]).astype(BF16), "wx": _pair_blockdiag(nat["lru_wx"][0]).astype(BF16),
        "ba": nat["lru_ba"], "bx": nat["lru_bx"], "lam": nat["lru_lambda"],
        "ffn_cw": [_block_cols(nat["ffn_conv_w"][l], 2, nblk) for l in range(2)],
        "ffn_cb": [_block_cols(nat["ffn_conv_b"][l:l + 1], 2, nblk) for l in range(2)],
        "sgu_w": sgu_tril.astype(BF16), "sgu_wt": jnp.swapaxes(sgu_tril, 1, 2).astype(BF16),
        "sgu_bias": jnp.repeat(jnp.transpose(nat["sgu_b"][0]), 64, axis=1), "sgu_gn": nat["sgu_norm"],
        "bf": jnp.pad(nat["fox_bf"], ((0, 0), (0, LANES - 8))),
        "g_mix0": nat["mix0_norm"], "g_mix1": nat["mix1_norm"],
        "g_ffn": [nat["ffn_norm"][0:1], nat["ffn_norm"][1:2]], "g_final": nat["final_norm"].reshape(1, D_MODEL),
    }


def _natural_grads(g):
    nblk = D_FF // FFN_CB
    small = {
        "mix0_norm": g["g_mix0"], "lru_conv_b": g["b4"],
        "lru_wa": _pair_diag_blocks(g["wa"])[None], "lru_ba": g["ba"],
        "lru_wx": _pair_diag_blocks(g["wx"])[None], "lru_bx": g["bx"],
        "lru_lambda": g["lam"], "sconv_b": g["b3"],
        "sgu_w": g["sgu_w"][None],
        "sgu_b": jnp.transpose(g["sgu_bias"].reshape(CHUNK, 8, 64).sum(axis=2))[None],
        "fox_bf": g["bf"][:, :8],
        "ffn_norm": jnp.concatenate([g["g_ffn0"], g["g_ffn1"]], axis=0),
        "ffn_conv_b": jnp.concatenate([_unblock_cols(g["ffn_cb0"], 2, nblk),
                                       _unblock_cols(g["ffn_cb1"], 2, nblk)], axis=0),
        "final_norm": g["final_norm"].reshape(D_MODEL),
        "lru_conv_w": g["w4"][None], "sconv_w": g["w3"][None],
        "ffn_conv_w": jnp.stack([_unblock_cols(g["ffn_cw0"], 2, nblk), _unblock_cols(g["ffn_cw1"], 2, nblk)]),
        "mix1_norm": g["g_mix1"], "sgu_norm": g["sgu_gn"],
    }
    big = {
        "mix0_w_in": _unblock_cols(g["w_in0"], 5, 4), "mix0_w_out": g["w_out0"],
        "mix1_w_in": g["w_in1"][:, :2568], "mix1_w_out": g["w_out1"],
        "ffn_up0": _unblock_cols(g["w_up0"], 2, nblk), "ffn_up1": _unblock_cols(g["w_up1"], 2, nblk),
        "ffn_down0": g["w_down0"], "ffn_down1": g["w_down1"],
    }
    return small, big


COL_SHARDED = ("mix0_w_in", "mix1_w_in", "ffn_up0", "ffn_up1")
SMALL_SHARDED = ("lru_conv_w", "sconv_w", "ffn_conv_w", "mix1_norm", "sgu_norm")
SMALL_REPLICATED = ("mix0_norm", "lru_conv_b", "lru_wa", "lru_ba", "lru_wx", "lru_bx", "lru_lambda", "sconv_b",
                    "sgu_w", "sgu_b", "fox_bf", "ffn_norm", "ffn_conv_b", "final_norm")
BIG = ("mix0_w_in", "mix0_w_out", "mix1_w_in", "mix1_w_out", "ffn_up0", "ffn_up1", "ffn_down0", "ffn_down1")
WEIGHT_ORDER = ("mix0_norm", "mix0_w_in", "lru_conv_w", "lru_conv_b", "lru_wa", "lru_ba", "lru_wx", "lru_bx",
                "lru_lambda", "sconv_w", "sconv_b", "mix0_w_out", "mix1_norm", "mix1_w_in", "sgu_norm", "sgu_w",
                "sgu_b", "fox_bf", "mix1_w_out", "ffn_norm", "ffn_up", "ffn_conv_w", "ffn_conv_b", "ffn_down",
                "final_norm")


def _halves(a):
    r = a.shape[0]
    return a.reshape((2, r // 2) + a.shape[1:])


def _train_step(x, target, w, m, v):
    x2 = x[0]
    t2 = target[0]
    chip = 2 * lax.axis_index("x") + lax.axis_index("y")
    core = lax.axis_index("c")

    big_shards = {
        "mix0_w_in": w["mix0_w_in"][0], "mix0_w_out": w["mix0_w_out"][0],
        "mix1_w_in": w["mix1_w_in"][0], "mix1_w_out": w["mix1_w_out"][0],
        "ffn_up0": w["ffn_up"][0], "ffn_up1": w["ffn_up"][1],
        "ffn_down0": w["ffn_down"][0], "ffn_down1": w["ffn_down"][1],
    }
    small_shards = [w[k] for k in SMALL_SHARDED]
    small_buf, small_offs = _pack(small_shards)
    gathered = _all_gather([_halves(big_shards[k].astype(BF16)) for k in BIG] + [_halves(small_buf)],
                           name="gather_weights")
    full = {}
    for k, arr in zip(BIG, gathered[:-1]):
        full[k] = arr.reshape((N_CHIPS, arr.shape[1] * arr.shape[2], arr.shape[3]))
    small_all = gathered[-1].reshape(N_CHIPS, -1, LANES)
    per_chip = [_unpack(small_all[k], small_offs, [a.shape for a in small_shards]) for k in range(N_CHIPS)]
    lru_conv_w = jnp.concatenate([per_chip[k][0] for k in range(N_CHIPS)], axis=-1)[0]
    sconv_w = jnp.concatenate([per_chip[k][1] for k in range(N_CHIPS)], axis=-1)[0]
    ffn_conv_w = jnp.concatenate([per_chip[k][2] for k in range(N_CHIPS)], axis=-1)
    mix1_norm = jnp.concatenate([per_chip[k][3] for k in range(N_CHIPS)], axis=-1)
    sgu_norm = jnp.concatenate([per_chip[k][4] for k in range(N_CHIPS)], axis=-1)

    nat = {
        "mix0_w_in": _cols_from_shards(full["mix0_w_in"]), "mix0_w_out": full["mix0_w_out"].reshape(-1, D_MODEL),
        "mix1_w_in": _cols_from_shards(full["mix1_w_in"]), "mix1_w_out": full["mix1_w_out"].reshape(-1, D_MODEL),
        "ffn_up": [_cols_from_shards(full[f"ffn_up{l}"]) for l in range(2)],
        "ffn_down": [full[f"ffn_down{l}"].reshape(-1, D_MODEL) for l in range(2)],
        "lru_conv_w": lru_conv_w, "sconv_w": sconv_w, "ffn_conv_w": ffn_conv_w, "mix1_norm": mix1_norm,
        "sgu_norm": sgu_norm,
    }
    for k in SMALL_REPLICATED:
        nat[k] = w[k]
    wts = _prepare_weights(nat)
    loss, grad_x, g = _local_step(x2, t2, wts)

    grads_small, grads_big = _natural_grads(g)
    big_stacked = {k: (_cols_to_shards(a) if k in COL_SHARDED else a.reshape(N_CHIPS, -1, D_MODEL))
                   for k, a in grads_big.items()}

    small_names = SMALL_REPLICATED + SMALL_SHARDED
    small_list = [grads_small[k] for k in small_names] + [loss[:, :1]]
    sbuf, soffs = _pack(small_list)
    sred = _all_reduce_small(sbuf, name="reduce_small")
    small_red = _unpack(sred, soffs, [a.shape for a in small_list])
    loss_total = small_red[-1][0, 0]
    gsum = dict(zip(small_names, small_red[:-1]))
    for k in SMALL_SHARDED:
        width = w[k].shape[-1]
        gsum[k] = lax.dynamic_slice_in_dim(gsum[k], chip * width, width, axis=gsum[k].ndim - 1)

    stacked = [big_stacked[k] for k in BIG]
    g4 = [a.reshape((N_CHIPS, 2, a.shape[1] // 2) + a.shape[2:]) for a in stacked]
    from_sib = _send_other_half(g4, name="rs_pair")
    core_arr = core.reshape(1).astype(jnp.int32)
    chip_arr = chip.reshape(1).astype(jnp.int32)
    pair = [_pair_sum(a, b, core_arr, name=f"rs_pair_sum_{k}") for k, a, b in zip(BIG, g4, from_sib)]
    from_chips = _send_to_owner(pair, name="rs_chips")
    mine = [_chip_sum(a, b, chip_arr, name=f"rs_chip_sum_{k}") for k, a, b in zip(BIG, pair, from_chips)]
    both = _swap_halves(mine, name="rs_swap")
    gbig = {k: a.reshape((a.shape[0] * a.shape[1],) + a.shape[2:]) for k, a in zip(BIG, both)}

    out_g, out_d, out_m, out_v = {}, {}, {}, {}
    small_w = [w[k] for k in small_names]
    pg, offs = _pack([gsum[k] for k in small_names])
    pw, _ = _pack(small_w)
    pm, _ = _pack([m[k] for k in small_names])
    pv, _ = _pack([v[k] for k in small_names])
    sd, sm, sv = _adamw(pw, pg, pm, pv, name="adamw_small")
    shapes = [a.shape for a in small_w]
    for k, dd, mm, vv in zip(small_names, _unpack(sd, offs, shapes), _unpack(sm, offs, shapes),
                             _unpack(sv, offs, shapes)):
        out_g[k], out_d[k], out_m[k], out_v[k] = gsum[k].reshape(w[k].shape), dd, mm, vv

    def big_adam(name, wk, mk, vk, gk):
        shp = wk.shape
        w2, m2, v2 = (a.reshape(gk.shape) for a in (wk, mk, vk))
        d, nm, nv = _adamw(w2, gk, m2, v2, name=f"adamw_{name}")
        return gk.reshape(shp), d.reshape(shp), nm.reshape(shp), nv.reshape(shp)

    for k in ("mix0_w_in", "mix0_w_out", "mix1_w_in", "mix1_w_out"):
        out_g[k], out_d[k], out_m[k], out_v[k] = big_adam(k, w[k][0], m[k][0], v[k][0], gbig[k])
        out_g[k], out_d[k], out_m[k], out_v[k] = (a[None] for a in (out_g[k], out_d[k], out_m[k], out_v[k]))
    for k in ("ffn_up", "ffn_down"):
        res = [big_adam(f"{k}{l}", w[k][l], m[k][l], v[k][l], gbig[f"{k}{l}"]) for l in range(2)]
        out_g[k], out_d[k], out_m[k], out_v[k] = (jnp.stack([res[0][i], res[1][i]]) for i in range(4))

    outs = [loss_total, grad_x[None]]
    for d in (out_g, out_d, out_m, out_v):
        outs.extend(d[k] for k in WEIGHT_ORDER)
    return tuple(outs)


def kernel(x, mix0_norm, mix0_w_in, lru_conv_w, lru_conv_b, lru_wa, lru_ba, lru_wx, lru_bx, lru_lambda, sconv_w, sconv_b, mix0_w_out, mix1_norm, mix1_w_in, sgu_norm, sgu_w, sgu_b, fox_bf, mix1_w_out, ffn_norm, ffn_up, ffn_conv_w, ffn_conv_b, ffn_down, final_norm, loss_target, m_mix0_norm, m_mix0_w_in, m_lru_conv_w, m_lru_conv_b, m_lru_wa, m_lru_ba, m_lru_wx, m_lru_bx, m_lru_lambda, m_sconv_w, m_sconv_b, m_mix0_w_out, m_mix1_norm, m_mix1_w_in, m_sgu_norm, m_sgu_w, m_sgu_b, m_fox_bf, m_mix1_w_out, m_ffn_norm, m_ffn_up, m_ffn_conv_w, m_ffn_conv_b, m_ffn_down, m_final_norm, v_mix0_norm, v_mix0_w_in, v_lru_conv_w, v_lru_conv_b, v_lru_wa, v_lru_ba, v_lru_wx, v_lru_bx, v_lru_lambda, v_sconv_w, v_sconv_b, v_mix0_w_out, v_mix1_norm, v_mix1_w_in, v_sgu_norm, v_sgu_w, v_sgu_b, v_fox_bf, v_mix1_w_out, v_ffn_norm, v_ffn_up, v_ffn_conv_w, v_ffn_conv_b, v_ffn_down, v_final_norm):
    w = dict(zip(WEIGHT_ORDER, (mix0_norm, mix0_w_in, lru_conv_w, lru_conv_b, lru_wa, lru_ba, lru_wx, lru_bx, lru_lambda, sconv_w, sconv_b, mix0_w_out, mix1_norm, mix1_w_in, sgu_norm, sgu_w, sgu_b, fox_bf, mix1_w_out, ffn_norm, ffn_up, ffn_conv_w, ffn_conv_b, ffn_down, final_norm)))
    m = dict(zip(WEIGHT_ORDER, (m_mix0_norm, m_mix0_w_in, m_lru_conv_w, m_lru_conv_b, m_lru_wa, m_lru_ba, m_lru_wx, m_lru_bx, m_lru_lambda, m_sconv_w, m_sconv_b, m_mix0_w_out, m_mix1_norm, m_mix1_w_in, m_sgu_norm, m_sgu_w, m_sgu_b, m_fox_bf, m_mix1_w_out, m_ffn_norm, m_ffn_up, m_ffn_conv_w, m_ffn_conv_b, m_ffn_down, m_final_norm)))
    v = dict(zip(WEIGHT_ORDER, (v_mix0_norm, v_mix0_w_in, v_lru_conv_w, v_lru_conv_b, v_lru_wa, v_lru_ba, v_lru_wx, v_lru_bx, v_lru_lambda, v_sconv_w, v_sconv_b, v_mix0_w_out, v_mix1_norm, v_mix1_w_in, v_sgu_norm, v_sgu_w, v_sgu_b, v_fox_bf, v_mix1_w_out, v_ffn_norm, v_ffn_up, v_ffn_conv_w, v_ffn_conv_b, v_ffn_down, v_final_norm)))
    return _train_step(x, loss_target, w, m, v)
```

```python
import functools

import jax
import jax.numpy as jnp
from jax import lax
from jax.experimental import pallas as pl
from jax.experimental.pallas import tpu as pltpu

F32 = jnp.float32
BF16 = jnp.bfloat16
MESH = pl.DeviceIdType.MESH

D_MODEL = 1024
LANES = 128
SUBLANES = 8
N_CHIPS = 4
EPS = 1e-6
LRU_C = 8.0
D_FF = 2816
FFN_CB = 256
CHUNK = 128
NEG = -1e30

ADAM_LR = 0.001
ADAM_B1 = 0.9
ADAM_B2 = 0.999
ADAM_EPS = 1e-08
ADAM_WD = 0.01
ADAM_STEP = 10
ADAM_C1 = 1.0 - ADAM_B1 ** ADAM_STEP
ADAM_C2 = 1.0 - ADAM_B2 ** ADAM_STEP

_GELU_C = 0.7978845608028654
_GELU_A = 0.044715


def _sigmoid(x):
    return 1.0 / (1.0 + jnp.exp(-x))


def _log1p_pos(e):
    w = 1.0 + e
    return jnp.where(w == 1.0, e, jnp.log(w) * (e / (w - 1.0)))


def _softplus(x):
    return jnp.maximum(x, 0.0) + _log1p_pos(jnp.exp(-jnp.abs(x)))


def _gelu(x):
    t = jnp.tanh(_GELU_C * (x + _GELU_A * (x * x * x)))
    return 0.5 * x * (1.0 + t), t


def _gelu_grad(x, t):
    return 0.5 * (1.0 + t) + 0.5 * x * (1.0 - t * t) * (_GELU_C * (1.0 + 3.0 * _GELU_A * x * x))


def _rows(shape):
    return lax.broadcasted_iota(jnp.int32, shape, 0)


def _lanes(shape):
    return lax.broadcasted_iota(jnp.int32, shape, 1)


def _shift_down(x, halo8, j):
    if j == 0:
        return x
    r = pltpu.roll(x, j, 0)
    hr = pltpu.roll(halo8, j, 0)
    top = jnp.where(_rows(hr.shape) < j, hr, r[:SUBLANES])
    return jnp.concatenate([top, r[SUBLANES:]], axis=0)


def _shift_up(x, next8, j):
    if j == 0:
        return x
    n = x.shape[0]
    r = pltpu.roll(x, n - j, 0)
    nr = pltpu.roll(next8, SUBLANES - j, 0)
    bot = jnp.where(_rows(nr.shape) >= SUBLANES - j, nr, r[n - SUBLANES:])
    return jnp.concatenate([r[:n - SUBLANES], bot], axis=0)


def _scan_fwd(a, u):
    n = a.shape[0]
    row = _rows(a.shape)
    h = u
    k = 1
    while k < n:
        keep = row >= k
        h_sh = jnp.where(keep, pltpu.roll(h, k, 0), 0.0)
        a_sh = jnp.where(keep, pltpu.roll(a, k, 0), 1.0)
        h = a * h_sh + h
        a = a * a_sh
        k *= 2
    return h, a


def _scan_rev(b, d):
    n = b.shape[0]
    row = _rows(b.shape)
    g = d
    k = 1
    while k < n:
        keep = row < n - k
        g_sh = jnp.where(keep, pltpu.roll(g, n - k, 0), 0.0)
        b_sh = jnp.where(keep, pltpu.roll(b, n - k, 0), 1.0)
        g = b * g_sh + g
        b = b * b_sh
        k *= 2
    return g, b


def _cumsum_fwd(x):
    n = x.shape[0]
    row = _rows(x.shape)
    k = 1
    while k < n:
        x = x + jnp.where(row >= k, pltpu.roll(x, k, 0), 0.0)
        k *= 2
    return x


def _cumsum_rev(x):
    n = x.shape[0]
    row = _rows(x.shape)
    k = 1
    while k < n:
        x = x + jnp.where(row < n - k, pltpu.roll(x, n - k, 0), 0.0)
        k *= 2
    return x


def _dot(a, b):
    return lax.dot_general(a, b, (((1,), (0,)), ((), ())), preferred_element_type=F32)


def _dot_nt(a, b):
    return lax.dot_general(a, b, (((1,), (1,)), ((), ())), preferred_element_type=F32)


def _dot_tn(a, b):
    return lax.dot_general(a, b, (((0,), (0,)), ((), ())), preferred_element_type=F32)


def _dot_split(x, m_bf16):
    hi = x.astype(BF16)
    lo = (x - hi.astype(F32)).astype(BF16)
    return _dot(hi, m_bf16) + _dot(lo, m_bf16)


def _tile_rows(ts, s):
    return min(ts, s)


def _mm(a_list, w, *, trans_w=False, res=None, out_dtype=F32, ts=512, nb=None, name):
    s = a_list[0].shape[0]
    ks = [a.shape[1] for a in a_list]
    k = sum(ks)
    n = w.shape[0] if trans_w else w.shape[1]
    ts = _tile_rows(ts, s)
    nb = n if nb is None else nb
    na = len(a_list)
    has_res = res is not None

    def body(*refs):
        a_refs = refs[:na]
        w_ref = refs[na]
        o_ref = refs[-1]
        parts = [r[...].astype(BF16) for r in a_refs]
        a = parts[0] if na == 1 else jnp.concatenate(parts, axis=1)
        acc = _dot_nt(a, w_ref[...]) if trans_w else _dot(a, w_ref[...])
        if has_res:
            acc = acc + refs[na + 1][...]
        o_ref[...] = acc.astype(out_dtype)

    in_specs = [pl.BlockSpec((ts, kk), lambda j, i: (i, 0)) for kk in ks]
    if trans_w:
        in_specs.append(pl.BlockSpec((nb, k), lambda j, i: (j, 0)))
    else:
        in_specs.append(pl.BlockSpec((k, nb), lambda j, i: (0, j)))
    args = list(a_list) + [w]
    if has_res:
        in_specs.append(pl.BlockSpec((ts, nb), lambda j, i: (i, j)))
        args.append(res)
    return pl.pallas_call(
        body, name=name, grid=(n // nb, s // ts), in_specs=in_specs,
        out_specs=pl.BlockSpec((ts, nb), lambda j, i: (i, j)),
        out_shape=jax.ShapeDtypeStruct((s, n), out_dtype),
    )(*args)


def _mm_tn(a_list, b_list, *, ts=512, nb=None, name):
    s = b_list[0].shape[0]
    ks = [a.shape[1] for a in a_list]
    k = sum(ks)
    width = b_list[0].shape[1]
    n = width * len(b_list)
    ts = _tile_rows(ts, s)
    nb = width if nb is None else nb
    per = width // nb
    na = len(a_list)
    nparts = len(b_list)

    def body(*refs):
        a_refs = refs[:na]
        b_refs = refs[na:na + nparts]
        o_ref = refs[-1]
        j = pl.program_id(0)
        i = pl.program_id(1)
        parts = [r[...].astype(BF16) for r in a_refs]
        a = parts[0] if na == 1 else jnp.concatenate(parts, axis=1)

        def accumulate(b_ref):
            upd = _dot_tn(a, b_ref[...].astype(BF16))

            @pl.when(i == 0)
            def _():
                o_ref[...] = upd

            @pl.when(i > 0)
            def _():
                o_ref[...] += upd

        if nparts == 1:
            accumulate(b_refs[0])
        else:
            for part, b_ref in enumerate(b_refs):
                pl.when(j // per == part)(functools.partial(accumulate, b_ref))

    in_specs = [pl.BlockSpec((ts, kk), lambda j, i: (i, 0)) for kk in ks]
    for part in range(nparts):
        in_specs.append(pl.BlockSpec(
            (ts, nb), lambda j, i, part=part: (i, jnp.clip(j - part * per, 0, per - 1))))
    return pl.pallas_call(
        body, name=name, grid=(n // nb, s // ts), in_specs=in_specs,
        out_specs=pl.BlockSpec((k, nb), lambda j, i: (0, j)),
        out_shape=jax.ShapeDtypeStruct((k, n), F32),
    )(*a_list, *b_list)


def _norm_fwd(h, g, *, ts=512, name):
    s, d = h.shape
    ts = _tile_rows(ts, s)

    def body(h_ref, g_ref, n_ref):
        x = h_ref[...]
        r = lax.rsqrt(jnp.mean(x * x, axis=-1, keepdims=True) + EPS)
        n_ref[...] = ((x * r) * g_ref[...]).astype(BF16)

    return pl.pallas_call(
        body, name=name, grid=(s // ts,),
        in_specs=[pl.BlockSpec((ts, d), lambda i: (i, 0)), pl.BlockSpec((1, d), lambda i: (0, 0))],
        out_specs=pl.BlockSpec((ts, d), lambda i: (i, 0)),
        out_shape=jax.ShapeDtypeStruct((s, d), BF16),
    )(h, g)


def _norm_bwd(dn, h, g, dres, *, ts=512, name):
    s, d = h.shape
    ts = _tile_rows(ts, s)

    def body(dn_ref, h_ref, g_ref, dres_ref, dh_ref, dg_ref):
        i = pl.program_id(0)
        x = h_ref[...]
        dnv = dn_ref[...]
        r = lax.rsqrt(jnp.mean(x * x, axis=-1, keepdims=True) + EPS)
        xhat = x * r
        part = jnp.sum(dnv * xhat, axis=0, keepdims=True)

        @pl.when(i == 0)
        def _():
            dg_ref[...] = part

        @pl.when(i > 0)
        def _():
            dg_ref[...] += part

        dxh = dnv * g_ref[...]
        dh_ref[...] = dres_ref[...] + r * (dxh - xhat * jnp.mean(dxh * xhat, axis=-1, keepdims=True))

    tile = pl.BlockSpec((ts, d), lambda i: (i, 0))
    vec = pl.BlockSpec((1, d), lambda i: (0, 0))
    return pl.pallas_call(
        body, name=name, grid=(s // ts,), in_specs=[tile, tile, vec, tile],
        out_specs=(tile, vec),
        out_shape=(jax.ShapeDtypeStruct((s, d), F32), jax.ShapeDtypeStruct((1, d), F32)),
    )(dn, h, g, dres)


def _final(h, g, target, *, ts=512, name):
    s, d = h.shape
    ts = _tile_rows(ts, s)
    nt = s // ts

    def body(h_ref, g_ref, t_ref, dh_ref, loss_ref, dg_ref, acc_ref):
        i = pl.program_id(0)
        x = h_ref[...]
        r = lax.rsqrt(jnp.mean(x * x, axis=-1, keepdims=True) + EPS)
        xhat = x * r
        gv = g_ref[...]
        err = xhat * gv - t_ref[...]
        sq = jnp.sum(err * err, axis=0, keepdims=True)
        dy = err * (1.0 / d)
        part = jnp.sum(dy * xhat, axis=0, keepdims=True)

        @pl.when(i == 0)
        def _():
            acc_ref[...] = sq
            dg_ref[...] = part

        @pl.when(i > 0)
        def _():
            acc_ref[...] += sq
            dg_ref[...] += part

        dxh = dy * gv
        dh_ref[...] = r * (dxh - xhat * jnp.mean(dxh * xhat, axis=-1, keepdims=True))

        @pl.when(i == nt - 1)
        def _():
            tot = jnp.sum(acc_ref[...], axis=1, keepdims=True) * (0.5 / d)
            loss_ref[...] = jnp.broadcast_to(tot, (1, LANES))

    tile = pl.BlockSpec((ts, d), lambda i: (i, 0))
    vec = pl.BlockSpec((1, d), lambda i: (0, 0))
    return pl.pallas_call(
        body, name=name, grid=(nt,), in_specs=[tile, vec, tile],
        out_specs=(tile, pl.BlockSpec((1, LANES), lambda i: (0, 0)), vec),
        out_shape=(jax.ShapeDtypeStruct((s, d), F32), jax.ShapeDtypeStruct((1, LANES), F32),
                   jax.ShapeDtypeStruct((1, d), F32)),
        scratch_shapes=[pltpu.VMEM((1, d), F32)],
    )(h, g, target)


def _halo_map(ts, width_blocks):
    per = ts // SUBLANES

    def index(j, i):
        return (jnp.maximum(i * per - 1, 0), width_blocks(j))

    return index


def _even_gates(xc, wa, ba, wx, bx, sp):
    xb = xc.astype(BF16)
    r = _sigmoid(_dot(xb, wa) + ba)
    ig = _sigmoid(_dot(xb, wx) + bx)
    la = (-LRU_C) * r * sp
    a = jnp.exp(la)
    a2 = a * a
    m = jnp.sqrt(-jnp.tanh(la) * (1.0 + a2))
    return r, ig, la, a, a2, m


def _even_core_fwd(p, w4, b4, wa, ba, wx, bx, lam, w3, b3, *, ts=512, name):
    s = p.shape[0]
    ts = _tile_rows(ts, s)
    nt = s // ts
    nblk = 4

    def body(p_ref, ph_ref, w4_ref, b4_ref, wa_ref, ba_ref, wx_ref, bx_ref, lam_ref, w3_ref, b3_ref,
             ya_ref, yb_ref, hl_ref, hcar_ref):
        i = pl.program_id(1)
        first = (i > 0).astype(F32)
        xa = p_ref[:, 0:LANES]
        ga = p_ref[:, LANES:2 * LANES]
        cp = p_ref[:, 2 * LANES:3 * LANES]
        bp = p_ref[:, 3 * LANES:4 * LANES]
        vb = p_ref[:, 4 * LANES:5 * LANES]
        xa_h = ph_ref[:, 0:LANES] * first
        s_h = ph_ref[:, 2 * LANES:3 * LANES] * ph_ref[:, 4 * LANES:5 * LANES] * first

        xc = b4_ref[...] + w4_ref[3:4, :] * xa
        for k in range(3):
            xc = xc + w4_ref[k:k + 1, :] * _shift_down(xa, xa_h, 3 - k)
        sp = _softplus(-lam_ref[...])
        _, ig, _, a, _, m = _even_gates(xc, wa_ref[0], ba_ref[...], wx_ref[0], bx_ref[...], sp)
        u = m * (ig * xc)
        hs, acum = _scan_fwd(a, u)

        @pl.when(i == 0)
        def _():
            hcar_ref[...] = jnp.zeros_like(hcar_ref)

        hs = hs + acum * hcar_ref[0:1, :]
        hl_ref[...] = hs
        hcar_ref[0:1, :] = hl_ref[ts - 1:ts, :]
        ge, _ = _gelu(ga)
        ya_ref[...] = (hs * ge).astype(BF16)

        sv = cp * vb
        sc = b3_ref[...] + w3_ref[2:3, :] * sv
        for k in range(2):
            sc = sc + w3_ref[k:k + 1, :] * _shift_down(sv, s_h, 2 - k)
        yb_ref[...] = (bp * sc).astype(BF16)

    blk = pl.BlockSpec((ts, 5 * LANES), lambda j, i: (i, j))
    halo = pl.BlockSpec((SUBLANES, 5 * LANES), _halo_map(ts, lambda j: j))
    vec = pl.BlockSpec((1, LANES), lambda j, i: (0, j))
    out = pl.BlockSpec((ts, LANES), lambda j, i: (i, j))
    return pl.pallas_call(
        body, name=name, grid=(nblk, nt),
        in_specs=[blk, halo,
                  pl.BlockSpec((4, LANES), lambda j, i: (0, j)), vec,
                  pl.BlockSpec((1, LANES, LANES), lambda j, i: (j, 0, 0)), vec,
                  pl.BlockSpec((1, LANES, LANES), lambda j, i: (j, 0, 0)), vec, vec,
                  pl.BlockSpec((3, LANES), lambda j, i: (0, j)), vec],
        out_specs=(out, out, out),
        out_shape=(jax.ShapeDtypeStruct((s, 4 * LANES), BF16), jax.ShapeDtypeStruct((s, 4 * LANES), BF16),
                   jax.ShapeDtypeStruct((s, 4 * LANES), F32)),
        scratch_shapes=[pltpu.VMEM((SUBLANES, LANES), F32)],
    )(p, p, w4, b4, wa, ba, wx, bx, lam, w3, b3)


def _even_core_bwd(dy, p, hl, w4, b4, wa, wat, ba, wx, wxt, bx, lam, w3, b3, *, ts=256, name):
    s = p.shape[0]
    ts = _tile_rows(ts, s)
    nt = s // ts
    nblk = 4
    per = ts // SUBLANES

    def body(dya_ref, dyb_ref, p_ref, ph_ref, hl_ref, hh_ref,
             w4_ref, b4_ref, wa_ref, wat_ref, ba_ref, wx_ref, wxt_ref, bx_ref, lam_ref, w3_ref, b3_ref,
             dp_ref, dw4_ref, db4_ref, dwa_ref, dba_ref, dwx_ref, dbx_ref, dlam_ref, dw3_ref, db3_ref,
             dxc_nx, dsc_nx, cg_ref):
        i = pl.program_id(1)
        ti = nt - 1 - i
        first = (ti > 0).astype(F32)
        xa = p_ref[:, 0:LANES]
        ga = p_ref[:, LANES:2 * LANES]
        cp = p_ref[:, 2 * LANES:3 * LANES]
        bp = p_ref[:, 3 * LANES:4 * LANES]
        vb = p_ref[:, 4 * LANES:5 * LANES]
        xa_h = ph_ref[:, 0:LANES] * first
        s_h = ph_ref[:, 2 * LANES:3 * LANES] * ph_ref[:, 4 * LANES:5 * LANES] * first
        h_h = hh_ref[...] * first

        @pl.when(i == 0)
        def _():
            dxc_nx[...] = jnp.zeros_like(dxc_nx)
            dsc_nx[...] = jnp.zeros_like(dsc_nx)
            cg_ref[...] = jnp.zeros_like(cg_ref)
            for ref in (dw4_ref, db4_ref, dwa_ref, dba_ref, dwx_ref, dbx_ref, dlam_ref, dw3_ref, db3_ref):
                ref[...] = jnp.zeros_like(ref)

        xa_sh = [_shift_down(xa, xa_h, 3 - k) for k in range(3)] + [xa]
        xc = b4_ref[...]
        for k in range(4):
            xc = xc + w4_ref[k:k + 1, :] * xa_sh[k]
        lamv = lam_ref[...]
        sp = _softplus(-lamv)
        r, ig, _, a, a2, m = _even_gates(xc, wa_ref[0], ba_ref[...], wx_ref[0], bx_ref[...], sp)
        sv = cp * vb
        sv_sh = [_shift_down(sv, s_h, 2 - k) for k in range(2)] + [sv]
        sc = b3_ref[...]
        for k in range(3):
            sc = sc + w3_ref[k:k + 1, :] * sv_sh[k]
        hs = hl_ref[...]
        h_prev = _shift_down(hs, h_h, 1)

        dya = dya_ref[...]
        dyb = dyb_ref[...]
        ge, gt = _gelu(ga)
        dga = dya * hs * _gelu_grad(ga, gt)
        dh = dya * ge

        ones8 = jnp.ones((SUBLANES, LANES), F32)
        b = _shift_up(a, ones8, 1)
        g, bcum = _scan_rev(b, dh)
        g = g + bcum * cg_ref[0:1, :]
        ag = a * g
        cg_ref[...] = ag[:SUBLANES]

        da = g * h_prev
        xi = ig * xc
        dm = g * xi
        dig = g * m * xc
        dxc = g * m * ig
        dla = da * a - dm * (a2 / m)
        dr = dla * ((-LRU_C) * sp)
        dlam_ref[...] += jnp.sum(dla * r, axis=0, keepdims=True) * (LRU_C * _sigmoid(-lamv))
        dra = dr * r * (1.0 - r)
        dia = dig * ig * (1.0 - ig)
        drab = dra.astype(BF16)
        diab = dia.astype(BF16)
        xcb = xc.astype(BF16)
        dxc = dxc + _dot(drab, wat_ref[0]) + _dot(diab, wxt_ref[0])
        dwa_ref[0] += _dot_tn(xcb, drab)
        dwx_ref[0] += _dot_tn(xcb, diab)
        dba_ref[...] += jnp.sum(dra, axis=0, keepdims=True)
        dbx_ref[...] += jnp.sum(dia, axis=0, keepdims=True)

        nx = dxc_nx[...]
        dxa = w4_ref[3:4, :] * dxc
        for k in range(3):
            dxa = dxa + w4_ref[k:k + 1, :] * _shift_up(dxc, nx, 3 - k)
        for k in range(4):
            dw4_ref[k:k + 1, :] += jnp.sum(dxc * xa_sh[k], axis=0, keepdims=True)
        db4_ref[...] += jnp.sum(dxc, axis=0, keepdims=True)
        dxc_nx[...] = dxc[:SUBLANES]

        dbp = dyb * sc
        dsc = dyb * bp
        nsc = dsc_nx[...]
        ds = w3_ref[2:3, :] * dsc
        for k in range(2):
            ds = ds + w3_ref[k:k + 1, :] * _shift_up(dsc, nsc, 2 - k)
        for k in range(3):
            dw3_ref[k:k + 1, :] += jnp.sum(dsc * sv_sh[k], axis=0, keepdims=True)
        db3_ref[...] += jnp.sum(dsc, axis=0, keepdims=True)
        dsc_nx[...] = dsc[:SUBLANES]

        dp_ref[:, 0:LANES] = dxa.astype(BF16)
        dp_ref[:, LANES:2 * LANES] = dga.astype(BF16)
        dp_ref[:, 2 * LANES:3 * LANES] = (ds * vb).astype(BF16)
        dp_ref[:, 3 * LANES:4 * LANES] = dbp.astype(BF16)
        dp_ref[:, 4 * LANES:5 * LANES] = (ds * cp).astype(BF16)

    def rev(j, i):
        return (nt - 1 - i, j)

    def rev_halo(col):
        def index(j, i):
            return (jnp.maximum((nt - 1 - i) * per - 1, 0), col(j))
        return index

    blk = pl.BlockSpec((ts, 5 * LANES), rev)
    one = pl.BlockSpec((ts, LANES), rev)
    vec = pl.BlockSpec((1, LANES), lambda j, i: (0, j))
    mat = pl.BlockSpec((1, LANES, LANES), lambda j, i: (j, 0, 0))
    w4s = pl.BlockSpec((4, LANES), lambda j, i: (0, j))
    w3s = pl.BlockSpec((3, LANES), lambda j, i: (0, j))
    f = jax.ShapeDtypeStruct
    return pl.pallas_call(
        body, name=name, grid=(nblk, nt),
        in_specs=[one, pl.BlockSpec((ts, LANES), lambda j, i: (nt - 1 - i, 4 + j)),
                  blk, pl.BlockSpec((SUBLANES, 5 * LANES), rev_halo(lambda j: j)),
                  one, pl.BlockSpec((SUBLANES, LANES), rev_halo(lambda j: j)),
                  w4s, vec, mat, mat, vec, mat, mat, vec, vec, w3s, vec],
        out_specs=(blk, w4s, vec, mat, vec, mat, vec, vec, w3s, vec),
        out_shape=(f((s, 20 * LANES), BF16), f((4, 4 * LANES), F32), f((1, 4 * LANES), F32),
                   f((4, LANES, LANES), F32), f((1, 4 * LANES), F32),
                   f((4, LANES, LANES), F32), f((1, 4 * LANES), F32), f((1, 4 * LANES), F32),
                   f((3, 4 * LANES), F32), f((1, 4 * LANES), F32)),
        scratch_shapes=[pltpu.VMEM((SUBLANES, LANES), F32), pltpu.VMEM((SUBLANES, LANES), F32),
                        pltpu.VMEM((SUBLANES, LANES), F32)],
    )(dy, dy, p, p, hl, hl, w4, b4, wa, wat, ba, wx, wxt, bx, lam, w3, b3)


def _ffn_conv(u_ref, uh_ref, w_ref, b_ref, first):
    u = u_ref[...].astype(F32)
    u_h = uh_ref[...].astype(F32)[SUBLANES:] * first
    u_sh = [_shift_down(u, u_h, 2 - k) for k in range(2)] + [u]
    hc = b_ref[...]
    for k in range(3):
        hc = hc + w_ref[k:k + 1, :] * u_sh[k]
    return hc, u_sh


def _ffn_specs(ts, row, halo_row):
    nblk = D_FF // FFN_CB
    specs = []
    for off in (0, nblk):
        specs.append(pl.BlockSpec((ts, FFN_CB), lambda j, i, off=off: (row(i), off + j)))
        specs.append(pl.BlockSpec((16, FFN_CB), lambda j, i, off=off: (halo_row(i), off + j)))
        specs.append(pl.BlockSpec((3, FFN_CB), lambda j, i, off=off: (0, off + j)))
        specs.append(pl.BlockSpec((1, FFN_CB), lambda j, i, off=off: (0, off + j)))
    return specs


def _ffn_core_fwd(up, w, b, *, ts=512, name):
    s = up.shape[0]
    ts = _tile_rows(ts, s)
    nt = s // ts
    nblk = D_FF // FFN_CB
    per = ts // 16

    def body(g_ref, gh_ref, wg_ref, bg_ref, v_ref, vh_ref, wv_ref, bv_ref, act_ref):
        first = (pl.program_id(1) > 0).astype(F32)
        gate, _ = _ffn_conv(g_ref, gh_ref, wg_ref, bg_ref, first)
        val, _ = _ffn_conv(v_ref, vh_ref, wv_ref, bv_ref, first)
        act_ref[...] = (gate * _sigmoid(gate) * val).astype(BF16)

    return pl.pallas_call(
        body, name=name, grid=(nblk, nt),
        in_specs=_ffn_specs(ts, lambda i: i, lambda i: jnp.maximum(i * per - 1, 0)),
        out_specs=pl.BlockSpec((ts, FFN_CB), lambda j, i: (i, j)),
        out_shape=jax.ShapeDtypeStruct((s, D_FF), BF16),
    )(up, up, w, b, up, up, w, b)


def _ffn_core_bwd(dact, up, w, b, *, ts=512, name):
    s = up.shape[0]
    ts = _tile_rows(ts, s)
    nt = s // ts
    nblk = D_FF // FFN_CB
    per = ts // 16

    def conv_bwd(dhc, u_sh, w_ref, nx_ref, du_ref, dw_ref, db_ref):
        nx = nx_ref[...]
        du = w_ref[2:3, :] * dhc
        for k in range(2):
            du = du + w_ref[k:k + 1, :] * _shift_up(dhc, nx, 2 - k)
        du_ref[...] = du.astype(BF16)
        for k in range(3):
            dw_ref[k:k + 1, :] += jnp.sum(dhc * u_sh[k], axis=0, keepdims=True)
        db_ref[...] += jnp.sum(dhc, axis=0, keepdims=True)
        nx_ref[...] = dhc[:SUBLANES]

    def body(da_ref, g_ref, gh_ref, wg_ref, bg_ref, v_ref, vh_ref, wv_ref, bv_ref,
             dg_ref, dv_ref, dwg_ref, dwv_ref, dbg_ref, dbv_ref, nxg_ref, nxv_ref):
        i = pl.program_id(1)
        first = (nt - 1 - i > 0).astype(F32)
        gate, g_sh = _ffn_conv(g_ref, gh_ref, wg_ref, bg_ref, first)
        val, v_sh = _ffn_conv(v_ref, vh_ref, wv_ref, bv_ref, first)
        da = da_ref[...].astype(F32)
        sg = _sigmoid(gate)
        dgate = da * val * (sg * (1.0 + gate * (1.0 - sg)))
        dval = da * (gate * sg)

        @pl.when(i == 0)
        def _():
            for ref in (nxg_ref, nxv_ref, dwg_ref, dwv_ref, dbg_ref, dbv_ref):
                ref[...] = jnp.zeros_like(ref)

        conv_bwd(dgate, g_sh, wg_ref, nxg_ref, dg_ref, dwg_ref, dbg_ref)
        conv_bwd(dval, v_sh, wv_ref, nxv_ref, dv_ref, dwv_ref, dbv_ref)

    def rev(i):
        return nt - 1 - i

    tile = pl.BlockSpec((ts, FFN_CB), lambda j, i: (rev(i), j))
    w_out = pl.BlockSpec((3, FFN_CB), lambda j, i: (0, j))
    b_out = pl.BlockSpec((1, FFN_CB), lambda j, i: (0, j))
    f = jax.ShapeDtypeStruct
    return pl.pallas_call(
        body, name=name, grid=(nblk, nt),
        in_specs=[tile] + _ffn_specs(ts, rev, lambda i: jnp.maximum(rev(i) * per - 1, 0)),
        out_specs=(tile, tile, w_out, w_out, b_out, b_out),
        out_shape=(f((s, D_FF), BF16), f((s, D_FF), BF16), f((3, D_FF), F32), f((3, D_FF), F32),
                   f((1, D_FF), F32), f((1, D_FF), F32)),
        scratch_shapes=[pltpu.VMEM((SUBLANES, FFN_CB), F32), pltpu.VMEM((SUBLANES, FFN_CB), F32)],
    )(dact, up, up, w, b, up, up, w, b)


def _sgu_forward_block(zu, zg, gn, w_ref, bias, seg):
    u, tu = _gelu(zu)
    g, tg = _gelu(zg)
    ms = _dot_split(g * g, seg)
    rs = lax.rsqrt(ms + EPS)
    ghat = g * rs
    gv = ghat * gn
    gvb = gv.astype(BF16)
    lane = _lanes((CHUNK, LANES))
    chunks = []
    for c in range(zu.shape[0] // CHUNK):
        gc = gvb[c * CHUNK:(c + 1) * CHUNK]
        mix = jnp.where(lane < 64, _dot(w_ref[0], gc), _dot(w_ref[1], gc)) + bias
        chunks.append(mix)
    mixed = chunks[0] if len(chunks) == 1 else jnp.concatenate(chunks, axis=0)
    return u, tu, g, tg, rs, ghat, gvb, mixed


def _sgu_fwd(p1, gn, w, bias, seg, *, ts=512, name):
    s = p1.shape[0]
    ts = _tile_rows(ts, s)

    def body(zu_ref, zg_ref, gn_ref, w_ref, bias_ref, seg_ref, yc_ref):
        u, _, _, _, _, _, _, mixed = _sgu_forward_block(
            zu_ref[...], zg_ref[...], gn_ref[...], w_ref, bias_ref[...], seg_ref[...])
        yc_ref[...] = (u * mixed).astype(BF16)

    return pl.pallas_call(
        body, name=name, grid=(4, s // ts),
        in_specs=[pl.BlockSpec((ts, LANES), lambda j, i: (i, j)),
                  pl.BlockSpec((ts, LANES), lambda j, i: (i, 4 + j)),
                  pl.BlockSpec((1, LANES), lambda j, i: (0, j)),
                  pl.BlockSpec((2, CHUNK, CHUNK), lambda j, i: (j, 0, 0)),
                  pl.BlockSpec((CHUNK, LANES), lambda j, i: (0, j)),
                  pl.BlockSpec((LANES, LANES), lambda j, i: (0, 0))],
        out_specs=pl.BlockSpec((ts, LANES), lambda j, i: (i, j)),
        out_shape=jax.ShapeDtypeStruct((s, 4 * LANES), BF16),
    )(p1, p1, gn, w, bias, seg)


def _sgu_bwd(p1, dy, gn, w, wt, bias, seg, tril, *, ts=512, name):
    s = p1.shape[0]
    ts = _tile_rows(ts, s)
    nt = s // ts

    def body(zu_ref, zg_ref, dy_ref, gn_ref, w_ref, wt_ref, bias_ref, seg_ref, tril_ref,
             dzu_ref, dzg_ref, dw_ref, dbias_ref, dgn_ref):
        i = pl.program_id(1)
        zu = zu_ref[...]
        zg = zg_ref[...]
        gn_v = gn_ref[...]
        segv = seg_ref[...]
        u, tu, g, tg, rs, ghat, gvb, mixed = _sgu_forward_block(zu, zg, gn_v, w_ref, bias_ref[...], segv)
        dyv = dy_ref[...]
        du = dyv * mixed
        dmx = dyv * u

        @pl.when(i == 0)
        def _():
            dw_ref[...] = jnp.zeros_like(dw_ref)
            dbias_ref[...] = jnp.zeros_like(dbias_ref)
            dgn_ref[...] = jnp.zeros_like(dgn_ref)

        lane = _lanes((CHUNK, LANES))
        dgv_chunks = []
        dbias = jnp.zeros((CHUNK, LANES), F32)
        for c in range(ts // CHUNK):
            dmc = dmx[c * CHUNK:(c + 1) * CHUNK]
            gc = gvb[c * CHUNK:(c + 1) * CHUNK]
            dm_a = jnp.where(lane < 64, dmc, 0.0).astype(BF16)
            dm_b = jnp.where(lane >= 64, dmc, 0.0).astype(BF16)
            dw_ref[0] += _dot_nt(dm_a, gc)
            dw_ref[1] += _dot_nt(dm_b, gc)
            dgv_chunks.append(_dot(wt_ref[0], dm_a) + _dot(wt_ref[1], dm_b))
            dbias = dbias + dmc
        dbias_ref[...] += dbias
        dgv = dgv_chunks[0] if len(dgv_chunks) == 1 else jnp.concatenate(dgv_chunks, axis=0)
        dgn_ref[...] += jnp.sum(dgv * ghat, axis=0, keepdims=True)
        dgh = dgv * gn_v
        dg = rs * (dgh - ghat * _dot_split(dgh * ghat, segv))
        dzu_ref[...] = (du * _gelu_grad(zu, tu)).astype(BF16)
        dzg_ref[...] = (dg * _gelu_grad(zg, tg)).astype(BF16)

        @pl.when(i == nt - 1)
        def _():
            dw_ref[0] = dw_ref[0] * tril_ref[...]
            dw_ref[1] = dw_ref[1] * tril_ref[...]

    f = jax.ShapeDtypeStruct
    colj = pl.BlockSpec((ts, LANES), lambda j, i: (i, j))
    wsp = pl.BlockSpec((2, CHUNK, CHUNK), lambda j, i: (j, 0, 0))
    sq = pl.BlockSpec((LANES, LANES), lambda j, i: (0, 0))
    return pl.pallas_call(
        body, name=name, grid=(4, nt),
        in_specs=[colj, pl.BlockSpec((ts, LANES), lambda j, i: (i, 4 + j)), colj,
                  pl.BlockSpec((1, LANES), lambda j, i: (0, j)), wsp, wsp,
                  pl.BlockSpec((CHUNK, LANES), lambda j, i: (0, j)), sq, sq],
        out_specs=(colj, colj, wsp, pl.BlockSpec((CHUNK, LANES), lambda j, i: (0, j)),
                   pl.BlockSpec((1, LANES), lambda j, i: (0, j))),
        out_shape=(f((s, 4 * LANES), BF16), f((s, 4 * LANES), BF16), f((8, CHUNK, CHUNK), F32),
                   f((CHUNK, 4 * LANES), F32), f((1, 4 * LANES), F32)),
    )(p1, p1, dy, gn, w, wt, bias, seg, tril)


F_COL = 20


def _fcum_fwd(p1, bf, *, ts=512, name):
    s = p1.shape[0]
    ts = _tile_rows(ts, s)

    def body(f_ref, bf_ref, c_ref, car_ref):
        i = pl.program_id(0)
        z = f_ref[...] + bf_ref[...]
        logf = jnp.minimum(z, 0.0) - _log1p_pos(jnp.exp(-jnp.abs(z)))

        @pl.when(i == 0)
        def _():
            car_ref[...] = jnp.zeros_like(car_ref)

        c_ref[...] = _cumsum_fwd(logf) + car_ref[0:1, :]
        car_ref[0:1, :] = c_ref[ts - 1:ts, :]

    return pl.pallas_call(
        body, name=name, grid=(s // ts,),
        in_specs=[pl.BlockSpec((ts, LANES), lambda i: (i, F_COL)), pl.BlockSpec((1, LANES), lambda i: (0, 0))],
        out_specs=pl.BlockSpec((ts, LANES), lambda i: (i, 0)),
        out_shape=jax.ShapeDtypeStruct((s, LANES), F32),
        scratch_shapes=[pltpu.VMEM((SUBLANES, LANES), F32)],
    )(p1, bf)


def _fcum_bwd(dcs, p1, bf, *, ts=512, name):
    s = p1.shape[0]
    ts = _tile_rows(ts, s)
    nt = s // ts

    def body(dc_ref, f_ref, bf_ref, df_ref, dbf_ref, car_ref):
        i = pl.program_id(0)

        @pl.when(i == 0)
        def _():
            car_ref[...] = jnp.zeros_like(car_ref)
            dbf_ref[...] = jnp.zeros_like(dbf_ref)

        dlog = _cumsum_rev(dc_ref[...]) + car_ref[0:1, :]
        car_ref[...] = dlog[:SUBLANES]
        z = f_ref[...] + bf_ref[...]
        df = dlog * _sigmoid(-z)
        df_ref[...] = df.astype(BF16)
        dbf_ref[...] += jnp.sum(df, axis=0, keepdims=True)

    return pl.pallas_call(
        body, name=name, grid=(nt,),
        in_specs=[pl.BlockSpec((ts, LANES), lambda i: (nt - 1 - i, 0)),
                  pl.BlockSpec((ts, LANES), lambda i: (nt - 1 - i, F_COL)),
                  pl.BlockSpec((1, LANES), lambda i: (0, 0))],
        out_specs=(pl.BlockSpec((ts, LANES), lambda i: (nt - 1 - i, 0)), pl.BlockSpec((1, LANES), lambda i: (0, 0))),
        out_shape=(jax.ShapeDtypeStruct((s, LANES), BF16), jax.ShapeDtypeStruct((1, LANES), F32)),
        scratch_shapes=[pltpu.VMEM((SUBLANES, LANES), F32)],
    )(dcs, p1, bf)


def _fox_scores(qm, kb, cq, ck, diagonal, tq, tk):
    sc = _dot_nt(qm, kb) + jnp.tile(cq, (1, tk // LANES)) - ck
    if diagonal:
        sc = jnp.where(_lanes((tq, tk)) <= _rows((tq, tk)), sc, NEG)
    return sc


def _fox_fwd(p1, cq, ck, *, tq=512, name):
    s = p1.shape[0]
    tq = _tile_rows(tq, s)
    tk = tq
    nq = s // tq

    def body(q_ref, k_ref, v_ref, cq_ref, ck_ref, o_ref, lse_ref, m_ref, l_ref, acc_ref):
        qi = pl.program_id(1)
        kj = pl.program_id(2)

        @pl.when(kj == 0)
        def _():
            m_ref[...] = jnp.full_like(m_ref, NEG)
            l_ref[...] = jnp.zeros_like(l_ref)
            acc_ref[...] = jnp.zeros_like(acc_ref)

        def step(diagonal):
            q = q_ref[...] * 0.125
            kb = k_ref[...].astype(BF16)
            vb = v_ref[...].astype(BF16)
            lane = _lanes((tq, LANES))
            outs = []
            for hh in range(2):
                sel = (lane < 64) if hh == 0 else (lane >= 64)
                qm = jnp.where(sel, q, 0.0).astype(BF16)
                sc = _fox_scores(qm, kb, cq_ref[:, hh * LANES:(hh + 1) * LANES], ck_ref[hh], diagonal, tq, tk)
                m_prev = m_ref[hh]
                m_new = jnp.maximum(m_prev, jnp.max(sc, axis=1, keepdims=True))
                pm = jnp.exp(sc - jnp.tile(m_new, (1, tk // LANES)))
                alpha = jnp.exp(m_prev - m_new)
                l_ref[hh] = alpha * l_ref[hh] + jnp.sum(pm, axis=1, keepdims=True)
                m_ref[hh] = m_new
                outs.append(acc_ref[...] * alpha + _dot(pm.astype(BF16), vb))
            acc_ref[...] = jnp.where(lane < 64, outs[0], outs[1])

        pl.when(kj < qi)(functools.partial(step, False))
        pl.when(kj == qi)(functools.partial(step, True))

        @pl.when(kj == qi)
        def _():
            lane = _lanes((tq, LANES))
            l_sel = jnp.where(lane < 64, l_ref[0], l_ref[1])
            o_ref[...] = (acc_ref[...] / l_sel).astype(BF16)
            lse_ref[:, 0:LANES] = m_ref[0] + jnp.log(l_ref[0])
            lse_ref[:, LANES:2 * LANES] = m_ref[1] + jnp.log(l_ref[1])

    def kmap(col0):
        return lambda j, qi, kj: (jnp.minimum(kj, qi), col0 + j)

    return pl.pallas_call(
        body, name=name, grid=(4, nq, nq),
        in_specs=[pl.BlockSpec((tq, LANES), lambda j, qi, kj: (qi, 8 + j)),
                  pl.BlockSpec((tk, LANES), kmap(12)), pl.BlockSpec((tk, LANES), kmap(16)),
                  pl.BlockSpec((tq, 2 * LANES), lambda j, qi, kj: (qi, j)),
                  pl.BlockSpec((2, 1, tk), lambda j, qi, kj: (j, 0, jnp.minimum(kj, qi)))],
        out_specs=(pl.BlockSpec((tq, LANES), lambda j, qi, kj: (qi, j)),
                   pl.BlockSpec((tq, 2 * LANES), lambda j, qi, kj: (qi, j))),
        out_shape=(jax.ShapeDtypeStruct((s, 4 * LANES), BF16), jax.ShapeDtypeStruct((s, 8 * LANES), F32)),
        scratch_shapes=[pltpu.VMEM((2, tq, LANES), F32), pltpu.VMEM((2, tq, LANES), F32),
                        pltpu.VMEM((tq, LANES), F32)],
    )(p1, p1, p1, cq, ck)


def _fox_delta(dy, o, sel, *, ts=512, name):
    s = o.shape[0]
    ts = _tile_rows(ts, s)

    def body(do_ref, o_ref, sel_ref, d_ref):
        prod = do_ref[...] * o_ref[...].astype(F32)
        d_ref[:, 0:LANES] = _dot_split(prod, sel_ref[0])
        d_ref[:, LANES:2 * LANES] = _dot_split(prod, sel_ref[1])

    return pl.pallas_call(
        body, name=name, grid=(4, s // ts),
        in_specs=[pl.BlockSpec((ts, LANES), lambda j, i: (i, 4 + j)),
                  pl.BlockSpec((ts, LANES), lambda j, i: (i, j)),
                  pl.BlockSpec((2, LANES, LANES), lambda j, i: (0, 0, 0))],
        out_specs=pl.BlockSpec((ts, 2 * LANES), lambda j, i: (i, j)),
        out_shape=jax.ShapeDtypeStruct((s, 8 * LANES), F32),
    )(dy, o, sel)


def _fox_bwd(p1, dy, lse, delta, cq, ck, *, tq=512, name):
    s = p1.shape[0]
    tq = _tile_rows(tq, s)
    tk = tq
    nq = s // tq

    def body(q_ref, k_ref, v_ref, do_ref, lse_ref, dl_ref, cq_ref, ck_ref,
             dq_ref, dk_ref, dv_ref, dck_ref, dcq_ref, dka_ref, dva_ref, dca_ref, dqa_ref, dra_ref):
        kj = pl.program_id(1)
        qi = pl.program_id(2)

        @pl.when((kj == 0) & (qi == 0))
        def _():
            dqa_ref[...] = jnp.zeros_like(dqa_ref)
            dra_ref[...] = jnp.zeros_like(dra_ref)

        @pl.when(qi == 0)
        def _():
            dka_ref[...] = jnp.zeros_like(dka_ref)
            dva_ref[...] = jnp.zeros_like(dva_ref)
            dca_ref[...] = jnp.zeros_like(dca_ref)

        def step(diagonal):
            q = q_ref[...] * 0.125
            kf = k_ref[...]
            kb = kf.astype(BF16)
            vb = v_ref[...].astype(BF16)
            do = do_ref[...]
            lane = _lanes((tq, LANES))
            klane = _lanes((tk, LANES))
            rows = pl.ds(pl.multiple_of(qi * tq, tq), tq)
            for hh in range(2):
                sel = (lane < 64) if hh == 0 else (lane >= 64)
                ksel = (klane < 64) if hh == 0 else (klane >= 64)
                qm = jnp.where(sel, q, 0.0).astype(BF16)
                dom = jnp.where(sel, do, 0.0).astype(BF16)
                km = jnp.where(ksel, kf, 0.0).astype(BF16)
                sc = _fox_scores(qm, kb, cq_ref[:, hh * LANES:(hh + 1) * LANES], ck_ref[hh], diagonal, tq, tk)
                pm = jnp.exp(sc - jnp.tile(lse_ref[:, hh * LANES:(hh + 1) * LANES], (1, tk // LANES)))
                dva_ref[...] += _dot_tn(pm.astype(BF16), dom)
                dp = _dot_nt(dom, vb)
                ds = pm * (dp - jnp.tile(dl_ref[:, hh * LANES:(hh + 1) * LANES], (1, tk // LANES)))
                dsb = ds.astype(BF16)
                dka_ref[...] += _dot_tn(dsb, qm)
                dca_ref[hh] -= jnp.sum(ds, axis=0, keepdims=True)
                dqa_ref[rows, :] += _dot(dsb, km)
                dra_ref[hh, rows, :] += jnp.sum(ds, axis=1, keepdims=True)

        pl.when(qi > kj)(functools.partial(step, False))
        pl.when(qi == kj)(functools.partial(step, True))

        @pl.when(qi == nq - 1)
        def _():
            dk_ref[...] = dka_ref[...].astype(BF16)
            dv_ref[...] = dva_ref[...].astype(BF16)
            dck_ref[...] = dca_ref[...]

        @pl.when((kj == nq - 1) & (qi == nq - 1))
        def _():
            dq_ref[...] = (dqa_ref[...] * 0.125).astype(BF16)
            dcq_ref[:, 0:LANES] = dra_ref[0]
            dcq_ref[:, LANES:2 * LANES] = dra_ref[1]

    def qmap(col0):
        return lambda j, kj, qi: (jnp.maximum(qi, kj), col0 + j)

    pair = pl.BlockSpec((tq, 2 * LANES), qmap(0))
    kblk = pl.BlockSpec((tk, LANES), lambda j, kj, qi: (kj, j))
    f = jax.ShapeDtypeStruct
    return pl.pallas_call(
        body, name=name, grid=(4, nq, nq),
        in_specs=[pl.BlockSpec((tq, LANES), qmap(8)),
                  pl.BlockSpec((tk, LANES), lambda j, kj, qi: (kj, 12 + j)),
                  pl.BlockSpec((tk, LANES), lambda j, kj, qi: (kj, 16 + j)),
                  pl.BlockSpec((tq, LANES), qmap(4)), pair, pair, pair,
                  pl.BlockSpec((2, 1, tk), lambda j, kj, qi: (j, 0, kj))],
        out_specs=(pl.BlockSpec((s, LANES), lambda j, kj, qi: (0, j)), kblk, kblk,
                   pl.BlockSpec((2, 1, tk), lambda j, kj, qi: (j, 0, kj)),
                   pl.BlockSpec((s, 2 * LANES), lambda j, kj, qi: (0, j))),
        out_shape=(f((s, 4 * LANES), BF16), f((s, 4 * LANES), BF16), f((s, 4 * LANES), BF16),
                   f((8, 1, s), F32), f((s, 8 * LANES), F32)),
        scratch_shapes=[pltpu.VMEM((tk, LANES), F32), pltpu.VMEM((tk, LANES), F32), pltpu.VMEM((2, 1, tk), F32),
                        pltpu.VMEM((s, LANES), F32), pltpu.VMEM((2, s, LANES), F32)],
    )(p1, p1, p1, dy, lse, delta, cq, ck)


def _row_block(r, cap=256):
    best = None
    for rb in range(2 * SUBLANES, min(r, cap) + 1, 2 * SUBLANES):
        if r % rb == 0:
            best = rb
    return r if best is None else best


def _adamw(w, g, m, v, *, name):
    r, c = w.shape
    rb = _row_block(r)

    def body(w_ref, g_ref, m_ref, v_ref, d_ref, nm_ref, nv_ref):
        gv = g_ref[...]
        mn = ADAM_B1 * m_ref[...] + (1.0 - ADAM_B1) * gv
        vn = ADAM_B2 * v_ref[...] + (1.0 - ADAM_B2) * (gv * gv)
        m_hat = mn / ADAM_C1
        v_hat = vn / ADAM_C2
        d_ref[...] = (-ADAM_LR) * (m_hat / (jnp.sqrt(v_hat) + ADAM_EPS) + ADAM_WD * w_ref[...])
        nm_ref[...] = mn
        nv_ref[...] = vn

    blk = pl.BlockSpec((rb, c), lambda i: (i, 0))
    shp = jax.ShapeDtypeStruct((r, c), F32)
    return pl.pallas_call(
        body, name=name, grid=(r // rb,), in_specs=[blk] * 4, out_specs=(blk,) * 3, out_shape=(shp,) * 3,
    )(w, g, m, v)


def _pair_sum(g, col, ra, core, *, name):
    _, rh, c = ra.shape
    rb = _row_block(rh)

    def body(core_ref, g_ref, ra_ref, h_ref, h16_ref):
        tot = g_ref[...] + ra_ref[...]
        h_ref[...] = tot
        h16_ref[...] = tot.astype(BF16)

    if col:
        g_spec = pl.BlockSpec((None, rb, c), lambda k, i, core_ref: (core_ref[0], i, k))
    else:
        g_spec = pl.BlockSpec((None, None, rb, c), lambda k, i, core_ref: (k, core_ref[0], i, 0))
    slot = pl.BlockSpec((None, rb, c), lambda k, i, core_ref: (k, i, 0))
    return pl.pallas_call(
        body, name=name,
        grid_spec=pltpu.PrefetchScalarGridSpec(
            num_scalar_prefetch=1, grid=(N_CHIPS, rh // rb), in_specs=[g_spec, slot], out_specs=(slot, slot)),
        out_shape=(jax.ShapeDtypeStruct((N_CHIPS, rh, c), F32), jax.ShapeDtypeStruct((N_CHIPS, rh, c), BF16)),
    )(core, g, ra)


def _first_sum(h, r1, keep, *, name):
    _, rh, c = h.shape
    rb = _row_block(rh)

    def body(keep_ref, h_ref, r_ref, s_ref, s16_ref):
        tot = h_ref[...] + r_ref[...].astype(F32)
        s_ref[...] = tot
        s16_ref[...] = tot.astype(BF16)

    slot = pl.BlockSpec((None, rb, c), lambda t, i, keep_ref: (t, i, 0))
    return pl.pallas_call(
        body, name=name,
        grid_spec=pltpu.PrefetchScalarGridSpec(
            num_scalar_prefetch=1, grid=(2, rh // rb),
            in_specs=[pl.BlockSpec((None, rb, c), lambda t, i, keep_ref: (keep_ref[t], i, 0)), slot],
            out_specs=(slot, slot)),
        out_shape=(jax.ShapeDtypeStruct((2, rh, c), F32), jax.ShapeDtypeStruct((2, rh, c), BF16)),
    )(keep, h, r1)


def _second_sum(s1, r2, mine, *, name):
    _, rh, c = s1.shape
    rb = _row_block(rh)

    def body(mine_ref, s_ref, r_ref, t_ref):
        t_ref[...] = s_ref[...] + r_ref[...].astype(F32)

    flat = pl.BlockSpec((rb, c), lambda i, mine_ref: (i, 0))
    return pl.pallas_call(
        body, name=name,
        grid_spec=pltpu.PrefetchScalarGridSpec(
            num_scalar_prefetch=1, grid=(rh // rb,),
            in_specs=[pl.BlockSpec((None, rb, c), lambda i, mine_ref: (mine_ref[0], i, 0)), flat],
            out_specs=flat),
        out_shape=jax.ShapeDtypeStruct((rh, c), F32),
    )(mine, s1, r2)


ANY = pl.BlockSpec(memory_space=pl.ANY)


def _mesh_pos():
    return lax.axis_index("x"), lax.axis_index("y"), lax.axis_index("c")


def _other_chips(x, y):
    return [(1 - x, y), (x, 1 - y), (1 - x, 1 - y)]


def _remote(src, dst, ssem, rsem, dev):
    return pltpu.make_async_remote_copy(src_ref=src, dst_ref=dst, send_sem=ssem, recv_sem=rsem,
                                        device_id=dev, device_id_type=MESH)


def _flip(a, b):
    return a + b - 2 * a * b


def _slab(ref, col, width, k, h):
    if not col:
        return ref.at[k, h]
    start = k * width if isinstance(k, int) else pl.multiple_of(k * width, LANES)
    return ref.at[h, :, pl.ds(start, width)]


def _all_gather(shards, cols, *, name):
    n = len(shards)

    def body(*refs):
        ins, outs = refs[:n], refs[n:2 * n]
        ssem, rsem, lsem = refs[2 * n:]
        x, y, c = _mesh_pos()
        me = 2 * x + y
        sib = (x, y, 1 - c)
        n1 = (_flip(x, 1 - c), _flip(y, c))
        n2 = (_flip(x, c), _flip(y, 1 - c))
        k1 = 2 * n1[0] + n1[1]
        k2 = 2 * n2[0] + n2[1]
        kd = 2 * (1 - x) + (1 - y)

        def slab(a, k, h):
            return _slab(outs[a], cols[a], ins[a].shape[2], k, h)

        def copy(a, j, src, dst, dev):
            return _remote(src, dst, ssem.at[a, j], rsem.at[a, j], dev)

        local, sends = [], []
        for a in range(n):
            for h in range(2):
                cp = pltpu.make_async_copy(ins[a].at[h], slab(a, me, h), lsem.at[a, h])
                cp.start()
                local.append(cp)
        for a in range(n):
            for j, nb in ((0, n1), (1, n2)):
                cp = copy(a, j, ins[a].at[c], slab(a, me, c), nb + (c,))
                cp.start()
                sends.append(cp)
        arrivals = ((0, k1, n1, 3), (1, k2, n2, 4), (2, kd, n2, 5))
        for j, k, nb, fwd in arrivals:
            for a in range(n):
                got = slab(a, k, c)
                copy(a, j, got, got, nb + (c,)).wait_recv()
                if j == 0:
                    cp = copy(a, 2, got, got, n2 + (c,))
                    cp.start()
                    sends.append(cp)
                cp = copy(a, fwd, got, got, sib)
                cp.start()
                sends.append(cp)
        for fwd, k in ((3, k2), (4, k1), (5, kd)):
            for a in range(n):
                got = slab(a, k, 1 - c)
                copy(a, fwd, got, got, sib).wait_recv()
        for cp in sends:
            cp.wait_send()
        for cp in local:
            cp.wait()

    def out_shape(a, col):
        _, rh, c = a.shape
        return jax.ShapeDtypeStruct((2, rh, N_CHIPS * c) if col else (N_CHIPS, 2, rh, c), a.dtype)

    return pl.pallas_call(
        body, name=name, in_specs=[ANY] * n, out_specs=[ANY] * n,
        out_shape=[out_shape(a, col) for a, col in zip(shards, cols)],
        scratch_shapes=[pltpu.SemaphoreType.DMA((n, 6)), pltpu.SemaphoreType.DMA((n, 6)),
                        pltpu.SemaphoreType.DMA((n, 2))],
    )(*shards)


def _send_other_half(grads, cols, *, name):
    n = len(grads)

    def shard_shape(g, col):
        if col:
            return (g.shape[1], g.shape[2] // N_CHIPS)
        return g.shape[2:]

    shapes = [shard_shape(g, col) for g, col in zip(grads, cols)]

    def body(*refs):
        ins, outs = refs[:n], refs[n:2 * n]
        ssem, rsem = refs[2 * n:]
        x, y, c = _mesh_pos()
        sib = (x, y, 1 - c)
        sends = []
        for a in range(n):
            for k in range(N_CHIPS):
                src = _slab(ins[a], cols[a], shapes[a][1], k, 1 - c)
                cp = _remote(src, outs[a].at[k], ssem.at[a, k], rsem.at[a, k], sib)
                cp.start()
                sends.append(cp)
        for cp in sends:
            cp.wait()

    return pl.pallas_call(
        body, name=name, in_specs=[ANY] * n, out_specs=[ANY] * n,
        out_shape=[jax.ShapeDtypeStruct((N_CHIPS,) + shp, g.dtype) for g, shp in zip(grads, shapes)],
        scratch_shapes=[pltpu.SemaphoreType.DMA((n, N_CHIPS)), pltpu.SemaphoreType.DMA((n, N_CHIPS))],
    )(*grads)


def _send_first(sums, *, name):
    n = len(sums)

    def body(*refs):
        ins, outs = refs[:n], refs[n:2 * n]
        ssem, rsem = refs[2 * n:]
        x, y, c = _mesh_pos()
        nb = (_flip(x, c), _flip(y, 1 - c), c)
        sends = []
        for a in range(n):
            for t in range(2):
                k = 2 * (c * (1 - x) + (1 - c) * t) + (c * t + (1 - c) * (1 - y))
                cp = _remote(ins[a].at[k], outs[a].at[t], ssem.at[a, t], rsem.at[a, t], nb)
                cp.start()
                sends.append(cp)
        for cp in sends:
            cp.wait()

    return pl.pallas_call(
        body, name=name, in_specs=[ANY] * n, out_specs=[ANY] * n,
        out_shape=[jax.ShapeDtypeStruct((2,) + h.shape[1:], h.dtype) for h in sums],
        scratch_shapes=[pltpu.SemaphoreType.DMA((n, 2)), pltpu.SemaphoreType.DMA((n, 2))],
    )(*sums)


def _send_second(sums, *, name):
    n = len(sums)

    def body(*refs):
        ins, outs = refs[:n], refs[n:2 * n]
        ssem, rsem = refs[2 * n:]
        x, y, c = _mesh_pos()
        nb = (_flip(x, 1 - c), _flip(y, c), c)
        other = 1 - (c * y + (1 - c) * x)
        sends = []
        for a in range(n):
            cp = _remote(ins[a].at[other], outs[a], ssem.at[a], rsem.at[a], nb)
            cp.start()
            sends.append(cp)
        for cp in sends:
            cp.wait()

    return pl.pallas_call(
        body, name=name, in_specs=[ANY] * n, out_specs=[ANY] * n,
        out_shape=[jax.ShapeDtypeStruct(s.shape[1:], s.dtype) for s in sums],
        scratch_shapes=[pltpu.SemaphoreType.DMA((n,)), pltpu.SemaphoreType.DMA((n,))],
    )(*sums)


def _swap_halves(halves, *, name):
    n = len(halves)

    def body(*refs):
        ins, outs = refs[:n], refs[n:2 * n]
        ssem, rsem, lsem = refs[2 * n:]
        x, y, c = _mesh_pos()
        sib = (x, y, 1 - c)
        cps = []
        for a in range(n):
            loc = pltpu.make_async_copy(ins[a], outs[a].at[c], lsem.at[a])
            loc.start()
            cp = _remote(ins[a], outs[a].at[c], ssem.at[a], rsem.at[a], sib)
            cp.start()
            cps.append((loc, cp))
        for a, (loc, cp) in enumerate(cps):
            cp.wait_send()
            _remote(ins[a], outs[a].at[1 - c], ssem.at[a], rsem.at[a], sib).wait_recv()
            loc.wait()

    return pl.pallas_call(
        body, name=name, in_specs=[ANY] * n, out_specs=[ANY] * n,
        out_shape=[jax.ShapeDtypeStruct((2,) + t.shape, t.dtype) for t in halves],
        scratch_shapes=[pltpu.SemaphoreType.DMA((n,)), pltpu.SemaphoreType.DMA((n,)),
                        pltpu.SemaphoreType.DMA((n,))],
    )(*halves)


def _all_reduce_small(buf, *, name):
    r = buf.shape[0]
    rh = r // 2

    def body(in_ref, out_ref, x1_ref, x2_ref, ssem, rsem):
        x, y, c = _mesh_pos()
        me = 2 * x + y
        sib = (x, y, 1 - c)
        chips = _other_chips(x, y)
        cp = _remote(in_ref, x1_ref, ssem.at[0], rsem.at[0], sib)
        cp.start()
        cp.wait()
        off = pl.multiple_of(c * rh, SUBLANES)
        x2_ref[me] = in_ref[pl.ds(off, rh), :] + x1_ref[pl.ds(off, rh), :]
        sends = []
        for j, (cx, cy) in enumerate(chips):
            s = _remote(x2_ref.at[me], x2_ref.at[me], ssem.at[1 + j], rsem.at[1 + j], (cx, cy, c))
            s.start()
            sends.append(s)
        for j, (cx, cy) in enumerate(chips):
            slot = x2_ref.at[2 * cx + cy]
            _remote(slot, slot, ssem.at[1 + j], rsem.at[1 + j], (cx, cy, c)).wait_recv()
        out_ref[pl.ds(off, rh), :] = ((x2_ref[0] + x2_ref[1]) + x2_ref[2]) + x2_ref[3]
        for s in sends:
            s.wait_send()
        mine = out_ref.at[pl.ds(off, rh), :]
        s3 = _remote(mine, mine, ssem.at[4], rsem.at[4], sib)
        s3.start()
        off2 = pl.multiple_of((1 - c) * rh, SUBLANES)
        theirs = out_ref.at[pl.ds(off2, rh), :]
        _remote(theirs, theirs, ssem.at[4], rsem.at[4], sib).wait_recv()
        s3.wait_send()

    vm = pl.BlockSpec(memory_space=pltpu.VMEM)
    return pl.pallas_call(
        body, name=name, in_specs=[vm], out_specs=vm,
        out_shape=jax.ShapeDtypeStruct((r, LANES), F32),
        scratch_shapes=[pltpu.VMEM((r, LANES), F32), pltpu.VMEM((N_CHIPS, rh, LANES), F32),
                        pltpu.SemaphoreType.DMA((5,)), pltpu.SemaphoreType.DMA((5,))],
    )(buf)


PACK_ALIGN = 2 * SUBLANES * LANES


def _pack(arrays, rows_multiple=2 * SUBLANES):
    parts, offs, off = [], [], 0
    for a in arrays:
        flat = a.reshape(-1).astype(F32)
        padded = -(-flat.shape[0] // PACK_ALIGN) * PACK_ALIGN
        parts.append(jnp.pad(flat, (0, padded - flat.shape[0])))
        offs.append(off)
        off += padded
    buf = jnp.concatenate(parts).reshape(-1, LANES)
    return buf, offs


def _unpack(buf, offs, shapes):
    flat = buf.reshape(-1)
    out = []
    for off, shp in zip(offs, shapes):
        size = 1
        for d in shp:
            size *= d
        out.append(flat[off:off + size].reshape(shp))
    return out


def _cols_from_shards(g4):
    _, k, ns = g4.shape
    return jnp.transpose(g4, (1, 0, 2)).reshape(k, N_CHIPS * ns)


def _cols_to_shards(w):
    k, n = w.shape
    return jnp.transpose(w.reshape(k, N_CHIPS, n // N_CHIPS), (1, 0, 2))


def _block_cols(w, parts, blocks):
    lead = w.shape[:-1]
    width = w.shape[-1] // (parts * blocks)
    w = w.reshape(lead + (parts, blocks, width))
    w = jnp.swapaxes(w, -3, -2)
    return w.reshape(lead + (parts * blocks * width,))


def _unblock_cols(w, parts, blocks):
    lead = w.shape[:-1]
    width = w.shape[-1] // (parts * blocks)
    w = w.reshape(lead + (blocks, parts, width))
    w = jnp.swapaxes(w, -3, -2)
    return w.reshape(lead + (parts * blocks * width,))


def _pair_blockdiag(w8):
    w = w8.reshape(4, 2, 64, 64)
    z = jnp.zeros((4, 64, 64), w8.dtype)
    top = jnp.concatenate([w[:, 0], z], axis=2)
    bot = jnp.concatenate([z, w[:, 1]], axis=2)
    return jnp.concatenate([top, bot], axis=1)


def _pair_diag_blocks(w4):
    a = w4[:, :64, :64]
    b = w4[:, 64:, 64:]
    return jnp.stack([a, b], axis=1).reshape(8, 64, 64)


def _local_step(x, target, wts):
    s = x.shape[0]
    g = {}

    win0 = wts["w_in0"]
    wout0 = wts["w_out0"]
    win1 = wts["w_in1"]
    wout1 = wts["w_out1"]
    wup = wts["w_up"]
    wdown = wts["w_down"]
    w4, b4, w3, b3 = wts["w4"], wts["b4"], wts["w3"], wts["b3"]
    wa, wx = wts["wa"], wts["wx"]
    wat, wxt = jnp.swapaxes(wa, 1, 2), jnp.swapaxes(wx, 1, 2)
    ba, bx, lam = wts["ba"], wts["bx"], wts["lam"]
    fcw, fcb = wts["ffn_cw"], wts["ffn_cb"]
    sgu_w, sgu_wt = wts["sgu_w"], wts["sgu_wt"]
    sgu_bias, sgu_gn = wts["sgu_bias"], wts["sgu_gn"]
    bf = wts["bf"]

    lane = jnp.arange(LANES)
    seg = jnp.where((lane[:, None] // 64) == (lane[None, :] // 64), 1.0 / 64.0, 0.0).astype(BF16)
    sel = jnp.stack([jnp.broadcast_to((lane[:, None] < 64), (LANES, LANES)),
                     jnp.broadcast_to((lane[:, None] >= 64), (LANES, LANES))]).astype(BF16)
    tril = (lane[:, None] >= lane[None, :]).astype(F32)

    n0 = _norm_fwd(x, wts["g_mix0"], name="norm_mix0")
    p0 = _mm([n0], win0, nb=640, name="mm_in0")
    ya, yb, hl = _even_core_fwd(p0, w4, b4, wa, ba, wx, bx, lam, w3, b3, name="even_fwd")
    h1 = _mm([ya, yb], wout0, res=x, name="mm_out0")

    def ffn_fwd(h, layer):
        n = _norm_fwd(h, wts["g_ffn"][layer], name=f"norm_ffn{layer}")
        up = _mm([n], wup[layer], out_dtype=BF16, nb=1408, name=f"mm_up{layer}")
        act = _ffn_core_fwd(up, fcw[layer], fcb[layer], name=f"ffn_fwd{layer}")
        hn = _mm([act], wdown[layer], res=h, name=f"mm_down{layer}")
        return n, up, act, hn

    n1, up0, act0, h2 = ffn_fwd(h1, 0)

    n2 = _norm_fwd(h2, wts["g_mix1"], name="norm_mix1")
    p1 = _mm([n2], win1, nb=896, name="mm_in1")
    yc = _sgu_fwd(p1, sgu_gn, sgu_w, sgu_bias, seg, name="sgu_fwd")
    cum = _fcum_fwd(p1, bf, name="fcum_fwd")
    c8 = cum[:, :8]
    cq = jnp.broadcast_to(c8[:, :, None], (s, 8, LANES)).reshape(s, 8 * LANES)
    ck = jnp.transpose(c8).reshape(8, 1, s)
    yd, lse = _fox_fwd(p1, cq, ck, name="fox_fwd")
    h3 = _mm([yc, yd], wout1, res=h2, name="mm_out1")

    n3, up1, act1, h4 = ffn_fwd(h3, 1)
    dh4, loss, g["final_norm"] = _final(h4, wts["g_final"], target, name="final")

    def ffn_bwd(dh, h, n, up, act, layer):
        dact = _mm([dh], wdown[layer], trans_w=True, out_dtype=BF16, nb=1408, name=f"mm_dact{layer}")
        dwd = _mm_tn([act], [dh], nb=512, name=f"mm_dwdown{layer}")
        dgate, dval, dcwg, dcwv, dcbg, dcbv = _ffn_core_bwd(dact, up, fcw[layer], fcb[layer], name=f"ffn_bwd{layer}")
        dn = _mm([dgate, dval], wup[layer], trans_w=True, nb=512, name=f"mm_dn_ffn{layer}")
        dwu = _mm_tn([n], [dgate, dval], nb=1408, name=f"mm_dwup{layer}")
        dhn, dg = _norm_bwd(dn, h, wts["g_ffn"][layer], dh, name=f"norm_bwd_ffn{layer}")
        dcw = jnp.concatenate([dcwg, dcwv], axis=1)
        dcb = jnp.concatenate([dcbg, dcbv], axis=1)
        return dhn, dwd, dwu, dcw, dcb, dg

    dh3, g["w_down1"], g["w_up1"], g["ffn_cw1"], g["ffn_cb1"], g["g_ffn1"] = ffn_bwd(dh4, h3, n3, up1, act1, 1)

    dy1 = _mm([dh3], wout1, trans_w=True, name="mm_dy1")
    g["w_out1"] = _mm_tn([yc, yd], [dh3], nb=512, name="mm_dwout1")
    dzu, dzg, g["sgu_w"], g["sgu_bias"], g["sgu_gn"] = _sgu_bwd(
        p1, dy1, sgu_gn, sgu_w, sgu_wt, sgu_bias, seg, tril, name="sgu_bwd")
    delta = _fox_delta(dy1, yd, sel, name="fox_delta")
    dq, dk, dv, dck, dcq = _fox_bwd(p1, dy1, lse, delta, cq, ck, name="fox_bwd")
    dcs = jnp.pad(jnp.transpose(dck.reshape(8, s)) + dcq.reshape(s, 8, LANES)[:, :, 0], ((0, 0), (0, LANES - 8)))
    df, g["bf"] = _fcum_bwd(dcs, p1, bf, name="fcum_bwd")
    dp1 = jnp.concatenate([dzu, dzg, dq, dk, dv, df], axis=1)
    dn2 = _mm([dp1], win1, trans_w=True, name="mm_dn_mix1")
    g["w_in1"] = _mm_tn([n2], [dp1], nb=896, name="mm_dwin1")
    dh2, g["g_mix1"] = _norm_bwd(dn2, h2, wts["g_mix1"], dh3, name="norm_bwd_mix1")

    dh1, g["w_down0"], g["w_up0"], g["ffn_cw0"], g["ffn_cb0"], g["g_ffn0"] = ffn_bwd(dh2, h1, n1, up0, act0, 0)

    dy0 = _mm([dh1], wout0, trans_w=True, name="mm_dy0")
    g["w_out0"] = _mm_tn([ya, yb], [dh1], nb=512, name="mm_dwout0")
    (dp0, g["w4"], g["b4"], g["wa"], g["ba"], g["wx"], g["bx"], g["lam"], g["w3"], g["b3"]) = _even_core_bwd(
        dy0, p0, hl, w4, b4, wa, wat, ba, wx, wxt, bx, lam, w3, b3, name="even_bwd")
    dn0 = _mm([dp0], win0, trans_w=True, name="mm_dn_mix0")
    g["w_in0"] = _mm_tn([n0], [dp0], nb=640, name="mm_dwin0")
    grad_x, g["g_mix0"] = _norm_bwd(dn0, x, wts["g_mix0"], dh1, name="norm_bwd_mix0")
    return loss, grad_x, g


def _prepare_weights(nat):
    lane = jnp.arange(LANES)
    tril = (lane[:, None] >= lane[None, :]).astype(F32)
    sgu_tril = nat["sgu_w"][0] * tril
    w_in1 = nat["mix1_w_in"]
    nblk = D_FF // FFN_CB
    return {
        "w_in0": _block_cols(nat["mix0_w_in"], 5, 4),
        "w_out0": nat["mix0_w_out"],
        "w_in1": jnp.pad(w_in1, ((0, 0), (0, 21 * LANES - w_in1.shape[1]))),
        "w_out1": nat["mix1_w_out"],
        "w_up": [nat["ffn_up"][l] for l in range(2)],
        "w_down": [nat["ffn_down"][l] for l in range(2)],
        "w4": nat["lru_conv_w"], "b4": nat["lru_conv_b"], "w3": nat["sconv_w"], "b3": nat["sconv_b"],
        "wa": _pair_blockdiag(nat["lru_wa"][0]).astype(BF16), "wx": _pair_blockdiag(nat["lru_wx"][0]).astype(BF16),
        "ba": nat["lru_ba"], "bx": nat["lru_bx"], "lam": nat["lru_lambda"],
        "ffn_cw": [nat["ffn_conv_w"][l] for l in range(2)],
        "ffn_cb": [nat["ffn_conv_b"][l:l + 1] for l in range(2)],
        "sgu_w": sgu_tril.astype(BF16), "sgu_wt": jnp.swapaxes(sgu_tril, 1, 2).astype(BF16),
        "sgu_bias": jnp.repeat(jnp.transpose(nat["sgu_b"][0]), 64, axis=1), "sgu_gn": nat["sgu_norm"],
        "bf": jnp.pad(nat["fox_bf"], ((0, 0), (0, LANES - 8))),
        "g_mix0": nat["mix0_norm"], "g_mix1": nat["mix1_norm"],
        "g_ffn": [nat["ffn_norm"][0:1], nat["ffn_norm"][1:2]], "g_final": nat["final_norm"].reshape(1, D_MODEL),
    }


def _natural_grads(g):
    nblk = D_FF // FFN_CB
    small = {
        "mix0_norm": g["g_mix0"], "lru_conv_b": g["b4"],
        "lru_wa": _pair_diag_blocks(g["wa"])[None], "lru_ba": g["ba"],
        "lru_wx": _pair_diag_blocks(g["wx"])[None], "lru_bx": g["bx"],
        "lru_lambda": g["lam"], "sconv_b": g["b3"],
        "sgu_w": g["sgu_w"][None],
        "sgu_b": jnp.transpose(g["sgu_bias"].reshape(CHUNK, 8, 64).sum(axis=2))[None],
        "fox_bf": g["bf"][:, :8],
        "ffn_norm": jnp.concatenate([g["g_ffn0"], g["g_ffn1"]], axis=0),
        "ffn_conv_b": jnp.concatenate([g["ffn_cb0"], g["ffn_cb1"]], axis=0),
        "final_norm": g["final_norm"].reshape(D_MODEL),
        "lru_conv_w": g["w4"][None], "sconv_w": g["w3"][None],
        "ffn_conv_w": jnp.stack([g["ffn_cw0"], g["ffn_cw1"]]),
        "mix1_norm": g["g_mix1"], "sgu_norm": g["sgu_gn"],
    }
    big = {
        "mix0_w_in": _unblock_cols(g["w_in0"], 5, 4), "mix0_w_out": g["w_out0"],
        "mix1_w_in": g["w_in1"][:, :2568], "mix1_w_out": g["w_out1"],
        "ffn_up0": g["w_up0"], "ffn_up1": g["w_up1"],
        "ffn_down0": g["w_down0"], "ffn_down1": g["w_down1"],
    }
    return small, big


COL_SHARDED = ("mix0_w_in", "mix1_w_in", "ffn_up0", "ffn_up1")
COL_ALIGNED = ("mix0_w_in", "ffn_up0", "ffn_up1")
SMALL_SHARDED = ("lru_conv_w", "sconv_w", "ffn_conv_w", "mix1_norm", "sgu_norm")
SMALL_REPLICATED = ("mix0_norm", "lru_conv_b", "lru_wa", "lru_ba", "lru_wx", "lru_bx", "lru_lambda", "sconv_b",
                    "sgu_w", "sgu_b", "fox_bf", "ffn_norm", "ffn_conv_b", "final_norm")
BIG = ("mix0_w_in", "mix0_w_out", "mix1_w_in", "mix1_w_out", "ffn_up0", "ffn_up1", "ffn_down0", "ffn_down1")
WEIGHT_ORDER = ("mix0_norm", "mix0_w_in", "lru_conv_w", "lru_conv_b", "lru_wa", "lru_ba", "lru_wx", "lru_bx",
                "lru_lambda", "sconv_w", "sconv_b", "mix0_w_out", "mix1_norm", "mix1_w_in", "sgu_norm", "sgu_w",
                "sgu_b", "fox_bf", "mix1_w_out", "ffn_norm", "ffn_up", "ffn_conv_w", "ffn_conv_b", "ffn_down",
                "final_norm")


def _halves(a):
    r = a.shape[0]
    return a.reshape((2, r // 2) + a.shape[1:])


def _train_step(x, target, w, m, v):
    x2 = x[0]
    t2 = target[0]
    chip = 2 * lax.axis_index("x") + lax.axis_index("y")
    core = lax.axis_index("c")

    big_shards = {
        "mix0_w_in": w["mix0_w_in"][0], "mix0_w_out": w["mix0_w_out"][0],
        "mix1_w_in": w["mix1_w_in"][0], "mix1_w_out": w["mix1_w_out"][0],
        "ffn_up0": w["ffn_up"][0], "ffn_up1": w["ffn_up"][1],
        "ffn_down0": w["ffn_down"][0], "ffn_down1": w["ffn_down"][1],
    }
    small_shards = [w[k] for k in SMALL_SHARDED]
    small_buf, small_offs = _pack(small_shards)
    cols = [k in COL_ALIGNED for k in BIG]
    gathered = _all_gather([_halves(big_shards[k].astype(BF16)) for k in BIG] + [_halves(small_buf)],
                           cols + [False], name="gather_weights")
    full = {}
    for k, arr in zip(BIG, gathered[:-1]):
        if k in COL_ALIGNED:
            full[k] = arr.reshape(arr.shape[0] * arr.shape[1], arr.shape[2])
        elif k in COL_SHARDED:
            full[k] = _cols_from_shards(arr.reshape((N_CHIPS, arr.shape[1] * arr.shape[2], arr.shape[3])))
        else:
            full[k] = arr.reshape(-1, arr.shape[3])
    small_all = gathered[-1].reshape(N_CHIPS, -1, LANES)
    per_chip = [_unpack(small_all[k], small_offs, [a.shape for a in small_shards]) for k in range(N_CHIPS)]
    lru_conv_w = jnp.concatenate([per_chip[k][0] for k in range(N_CHIPS)], axis=-1)[0]
    sconv_w = jnp.concatenate([per_chip[k][1] for k in range(N_CHIPS)], axis=-1)[0]
    ffn_conv_w = jnp.concatenate([per_chip[k][2] for k in range(N_CHIPS)], axis=-1)
    mix1_norm = jnp.concatenate([per_chip[k][3] for k in range(N_CHIPS)], axis=-1)
    sgu_norm = jnp.concatenate([per_chip[k][4] for k in range(N_CHIPS)], axis=-1)

    nat = {
        "mix0_w_in": full["mix0_w_in"], "mix0_w_out": full["mix0_w_out"],
        "mix1_w_in": full["mix1_w_in"], "mix1_w_out": full["mix1_w_out"],
        "ffn_up": [full["ffn_up0"], full["ffn_up1"]], "ffn_down": [full["ffn_down0"], full["ffn_down1"]],
        "lru_conv_w": lru_conv_w, "sconv_w": sconv_w, "ffn_conv_w": ffn_conv_w, "mix1_norm": mix1_norm,
        "sgu_norm": sgu_norm,
    }
    for k in SMALL_REPLICATED:
        nat[k] = w[k]
    wts = _prepare_weights(nat)
    loss, grad_x, g = _local_step(x2, t2, wts)

    grads_small, grads_big = _natural_grads(g)

    small_names = SMALL_REPLICATED + SMALL_SHARDED
    small_list = [grads_small[k] for k in small_names] + [loss[:, :1]]
    sbuf, soffs = _pack(small_list)
    sred = _all_reduce_small(sbuf, name="reduce_small")
    small_red = _unpack(sred, soffs, [a.shape for a in small_list])
    loss_total = small_red[-1][0, 0]
    gsum = dict(zip(small_names, small_red[:-1]))
    for k in SMALL_SHARDED:
        width = w[k].shape[-1]
        gsum[k] = lax.dynamic_slice_in_dim(gsum[k], chip * width, width, axis=gsum[k].ndim - 1)

    def grad_view(k):
        a = grads_big[k]
        if k in COL_ALIGNED:
            return a.reshape(2, a.shape[0] // 2, a.shape[1])
        if k in COL_SHARDED:
            a = _cols_to_shards(a)
            return a.reshape(N_CHIPS, 2, a.shape[1] // 2, a.shape[2])
        rows = a.shape[0] // (2 * N_CHIPS)
        return a.reshape(N_CHIPS, 2, rows, a.shape[1])

    gviews = [grad_view(k) for k in BIG]
    xi, yi = lax.axis_index("x"), lax.axis_index("y")
    core_arr = core.reshape(1).astype(jnp.int32)
    keep_arr = jnp.stack([core * (2 * xi + t) + (1 - core) * (2 * t + yi) for t in range(2)]).astype(jnp.int32)
    mine_arr = (core * yi + (1 - core) * xi).reshape(1).astype(jnp.int32)
    from_sib = _send_other_half(gviews, cols, name="rs_pair")
    pair = [_pair_sum(a, col, b, core_arr, name=f"rs_pair_sum_{k}")
            for k, a, col, b in zip(BIG, gviews, cols, from_sib)]
    first_in = _send_first([p16 for _, p16 in pair], name="rs_first")
    first = [_first_sum(p32, r1, keep_arr, name=f"rs_first_sum_{k}") for k, (p32, _), r1 in zip(BIG, pair, first_in)]
    second_in = _send_second([s16 for _, s16 in first], name="rs_second")
    mine = [_second_sum(s32, r2, mine_arr, name=f"rs_second_sum_{k}")
            for k, (s32, _), r2 in zip(BIG, first, second_in)]
    both = _swap_halves(mine, name="rs_swap")
    gbig = {k: a.reshape((a.shape[0] * a.shape[1],) + a.shape[2:]) for k, a in zip(BIG, both)}

    out_g, out_d, out_m, out_v = {}, {}, {}, {}
    small_w = [w[k] for k in small_names]
    pg, offs = _pack([gsum[k] for k in small_names])
    pw, _ = _pack(small_w)
    pm, _ = _pack([m[k] for k in small_names])
    pv, _ = _pack([v[k] for k in small_names])
    sd, sm, sv = _adamw(pw, pg, pm, pv, name="adamw_small")
    shapes = [a.shape for a in small_w]
    for k, dd, mm, vv in zip(small_names, _unpack(sd, offs, shapes), _unpack(sm, offs, shapes),
                             _unpack(sv, offs, shapes)):
        out_g[k], out_d[k], out_m[k], out_v[k] = gsum[k].reshape(w[k].shape), dd, mm, vv

    def big_adam(name, wk, mk, vk, gk):
        shp = wk.shape
        w2, m2, v2 = (a.reshape(gk.shape) for a in (wk, mk, vk))
        d, nm, nv = _adamw(w2, gk, m2, v2, name=f"adamw_{name}")
        return gk.reshape(shp), d.reshape(shp), nm.reshape(shp), nv.reshape(shp)

    for k in ("mix0_w_in", "mix0_w_out", "mix1_w_in", "mix1_w_out"):
        out_g[k], out_d[k], out_m[k], out_v[k] = big_adam(k, w[k][0], m[k][0], v[k][0], gbig[k])
        out_g[k], out_d[k], out_m[k], out_v[k] = (a[None] for a in (out_g[k], out_d[k], out_m[k], out_v[k]))
    for k in ("ffn_up", "ffn_down"):
        res = [big_adam(f"{k}{l}", w[k][l], m[k][l], v[k][l], gbig[f"{k}{l}"]) for l in range(2)]
        out_g[k], out_d[k], out_m[k], out_v[k] = (jnp.stack([res[0][i], res[1][i]]) for i in range(4))

    outs = [loss_total, grad_x[None]]
    for d in (out_g, out_d, out_m, out_v):
        outs.extend(d[k] for k in WEIGHT_ORDER)
    return tuple(outs)


def kernel(x, mix0_norm, mix0_w_in, lru_conv_w, lru_conv_b, lru_wa, lru_ba, lru_wx, lru_bx, lru_lambda, sconv_w, sconv_b, mix0_w_out, mix1_norm, mix1_w_in, sgu_norm, sgu_w, sgu_b, fox_bf, mix1_w_out, ffn_norm, ffn_up, ffn_conv_w, ffn_conv_b, ffn_down, final_norm, loss_target, m_mix0_norm, m_mix0_w_in, m_lru_conv_w, m_lru_conv_b, m_lru_wa, m_lru_ba, m_lru_wx, m_lru_bx, m_lru_lambda, m_sconv_w, m_sconv_b, m_mix0_w_out, m_mix1_norm, m_mix1_w_in, m_sgu_norm, m_sgu_w, m_sgu_b, m_fox_bf, m_mix1_w_out, m_ffn_norm, m_ffn_up, m_ffn_conv_w, m_ffn_conv_b, m_ffn_down, m_final_norm, v_mix0_norm, v_mix0_w_in, v_lru_conv_w, v_lru_conv_b, v_lru_wa, v_lru_ba, v_lru_wx, v_lru_bx, v_lru_lambda, v_sconv_w, v_sconv_b, v_mix0_w_out, v_mix1_norm, v_mix1_w_in, v_sgu_norm, v_sgu_w, v_sgu_b, v_fox_bf, v_mix1_w_out, v_ffn_norm, v_ffn_up, v_ffn_conv_w, v_ffn_conv_b, v_ffn_down, v_final_norm):
    w = dict(zip(WEIGHT_ORDER, (mix0_norm, mix0_w_in, lru_conv_w, lru_conv_b, lru_wa, lru_ba, lru_wx, lru_bx, lru_lambda, sconv_w, sconv_b, mix0_w_out, mix1_norm, mix1_w_in, sgu_norm, sgu_w, sgu_b, fox_bf, mix1_w_out, ffn_norm, ffn_up, ffn_conv_w, ffn_conv_b, ffn_down, final_norm)))
    m = dict(zip(WEIGHT_ORDER, (m_mix0_norm, m_mix0_w_in, m_lru_conv_w, m_lru_conv_b, m_lru_wa, m_lru_ba, m_lru_wx, m_lru_bx, m_lru_lambda, m_sconv_w, m_sconv_b, m_mix0_w_out, m_mix1_norm, m_mix1_w_in, m_sgu_norm, m_sgu_w, m_sgu_b, m_fox_bf, m_mix1_w_out, m_ffn_norm, m_ffn_up, m_ffn_conv_w, m_ffn_conv_b, m_ffn_down, m_final_norm)))
    v = dict(zip(WEIGHT_ORDER, (v_mix0_norm, v_mix0_w_in, v_lru_conv_w, v_lru_conv_b, v_lru_wa, v_lru_ba, v_lru_wx, v_lru_bx, v_lru_lambda, v_sconv_w, v_sconv_b, v_mix0_w_out, v_mix1_norm, v_mix1_w_in, v_sgu_norm, v_sgu_w, v_sgu_b, v_fox_bf, v_mix1_w_out, v_ffn_norm, v_ffn_up, v_ffn_conv_w, v_ffn_conv_b, v_ffn_down, v_final_norm)))
    return _train_step(x, loss_target, w, m, v)
```

```python
import functools

import jax
import jax.numpy as jnp
from jax import lax
from jax.experimental import pallas as pl
from jax.experimental.pallas import tpu as pltpu

F32 = jnp.float32
BF16 = jnp.bfloat16
MESH = pl.DeviceIdType.MESH

D_MODEL = 1024
LANES = 128
SUBLANES = 8
N_CHIPS = 4
EPS = 1e-6
LRU_C = 8.0
D_FF = 2816
FFN_CB = 256
CHUNK = 128
NEG = -1e30

ADAM_LR = 0.001
ADAM_B1 = 0.9
ADAM_B2 = 0.999
ADAM_EPS = 1e-08
ADAM_WD = 0.01
ADAM_STEP = 10
ADAM_C1 = 1.0 - ADAM_B1 ** ADAM_STEP
ADAM_C2 = 1.0 - ADAM_B2 ** ADAM_STEP

_GELU_C = 0.7978845608028654
_GELU_A = 0.044715


def _sigmoid(x):
    return 1.0 / (1.0 + jnp.exp(-x))


def _log1p_pos(e):
    w = 1.0 + e
    return jnp.where(w == 1.0, e, jnp.log(w) * (e / (w - 1.0)))


def _softplus(x):
    return jnp.maximum(x, 0.0) + _log1p_pos(jnp.exp(-jnp.abs(x)))


def _gelu(x):
    t = jnp.tanh(_GELU_C * (x + _GELU_A * (x * x * x)))
    return 0.5 * x * (1.0 + t), t


def _gelu_grad(x, t):
    return 0.5 * (1.0 + t) + 0.5 * x * (1.0 - t * t) * (_GELU_C * (1.0 + 3.0 * _GELU_A * x * x))


def _rows(shape):
    return lax.broadcasted_iota(jnp.int32, shape, 0)


def _lanes(shape):
    return lax.broadcasted_iota(jnp.int32, shape, 1)


def _shift_down(x, halo8, j):
    if j == 0:
        return x
    r = pltpu.roll(x, j, 0)
    hr = pltpu.roll(halo8, j, 0)
    top = jnp.where(_rows(hr.shape) < j, hr, r[:SUBLANES])
    return jnp.concatenate([top, r[SUBLANES:]], axis=0)


def _shift_up(x, next8, j):
    if j == 0:
        return x
    n = x.shape[0]
    r = pltpu.roll(x, n - j, 0)
    nr = pltpu.roll(next8, SUBLANES - j, 0)
    bot = jnp.where(_rows(nr.shape) >= SUBLANES - j, nr, r[n - SUBLANES:])
    return jnp.concatenate([r[:n - SUBLANES], bot], axis=0)


def _scan_fwd(a, u):
    n = a.shape[0]
    row = _rows(a.shape)
    h = u
    k = 1
    while k < n:
        keep = row >= k
        h_sh = jnp.where(keep, pltpu.roll(h, k, 0), 0.0)
        a_sh = jnp.where(keep, pltpu.roll(a, k, 0), 1.0)
        h = a * h_sh + h
        a = a * a_sh
        k *= 2
    return h, a


def _scan_rev(b, d):
    n = b.shape[0]
    row = _rows(b.shape)
    g = d
    k = 1
    while k < n:
        keep = row < n - k
        g_sh = jnp.where(keep, pltpu.roll(g, n - k, 0), 0.0)
        b_sh = jnp.where(keep, pltpu.roll(b, n - k, 0), 1.0)
        g = b * g_sh + g
        b = b * b_sh
        k *= 2
    return g, b


def _cumsum_fwd(x):
    n = x.shape[0]
    row = _rows(x.shape)
    k = 1
    while k < n:
        x = x + jnp.where(row >= k, pltpu.roll(x, k, 0), 0.0)
        k *= 2
    return x


def _cumsum_rev(x):
    n = x.shape[0]
    row = _rows(x.shape)
    k = 1
    while k < n:
        x = x + jnp.where(row < n - k, pltpu.roll(x, n - k, 0), 0.0)
        k *= 2
    return x


def _dot(a, b):
    return lax.dot_general(a, b, (((1,), (0,)), ((), ())), preferred_element_type=F32)


def _dot_nt(a, b):
    return lax.dot_general(a, b, (((1,), (1,)), ((), ())), preferred_element_type=F32)


def _dot_tn(a, b):
    return lax.dot_general(a, b, (((0,), (0,)), ((), ())), preferred_element_type=F32)


def _dot_split(x, m_bf16):
    hi = x.astype(BF16)
    lo = (x - hi.astype(F32)).astype(BF16)
    return _dot(hi, m_bf16) + _dot(lo, m_bf16)


def _tile_rows(ts, s):
    return min(ts, s)


def _mm(a_list, w, *, trans_w=False, res=None, out_dtype=F32, ts=512, nb=None, name):
    s = a_list[0].shape[0]
    ks = [a.shape[1] for a in a_list]
    k = sum(ks)
    n = w.shape[0] if trans_w else w.shape[1]
    ts = _tile_rows(ts, s)
    nb = n if nb is None else nb
    na = len(a_list)
    has_res = res is not None

    def body(*refs):
        a_refs = refs[:na]
        w_ref = refs[na]
        o_ref = refs[-1]
        parts = [r[...].astype(BF16) for r in a_refs]
        a = parts[0] if na == 1 else jnp.concatenate(parts, axis=1)
        acc = _dot_nt(a, w_ref[...]) if trans_w else _dot(a, w_ref[...])
        if has_res:
            acc = acc + refs[na + 1][...]
        o_ref[...] = acc.astype(out_dtype)

    in_specs = [pl.BlockSpec((ts, kk), lambda j, i: (i, 0)) for kk in ks]
    if trans_w:
        in_specs.append(pl.BlockSpec((nb, k), lambda j, i: (j, 0)))
    else:
        in_specs.append(pl.BlockSpec((k, nb), lambda j, i: (0, j)))
    args = list(a_list) + [w]
    if has_res:
        in_specs.append(pl.BlockSpec((ts, nb), lambda j, i: (i, j)))
        args.append(res)
    return pl.pallas_call(
        body, name=name, grid=(n // nb, s // ts), in_specs=in_specs,
        out_specs=pl.BlockSpec((ts, nb), lambda j, i: (i, j)),
        out_shape=jax.ShapeDtypeStruct((s, n), out_dtype),
    )(*args)


def _mm_tn(a_list, b_list, *, ts=512, nb=None, name):
    s = b_list[0].shape[0]
    ks = [a.shape[1] for a in a_list]
    k = sum(ks)
    width = b_list[0].shape[1]
    n = width * len(b_list)
    ts = _tile_rows(ts, s)
    nb = width if nb is None else nb
    per = width // nb
    na = len(a_list)
    nparts = len(b_list)

    def body(*refs):
        a_refs = refs[:na]
        b_refs = refs[na:na + nparts]
        o_ref = refs[-1]
        j = pl.program_id(0)
        i = pl.program_id(1)
        parts = [r[...].astype(BF16) for r in a_refs]
        a = parts[0] if na == 1 else jnp.concatenate(parts, axis=1)

        def accumulate(b_ref):
            upd = _dot_tn(a, b_ref[...].astype(BF16))

            @pl.when(i == 0)
            def _():
                o_ref[...] = upd

            @pl.when(i > 0)
            def _():
                o_ref[...] += upd

        if nparts == 1:
            accumulate(b_refs[0])
        else:
            for part, b_ref in enumerate(b_refs):
                pl.when(j // per == part)(functools.partial(accumulate, b_ref))

    in_specs = [pl.BlockSpec((ts, kk), lambda j, i: (i, 0)) for kk in ks]
    for part in range(nparts):
        in_specs.append(pl.BlockSpec(
            (ts, nb), lambda j, i, part=part: (i, jnp.clip(j - part * per, 0, per - 1))))
    return pl.pallas_call(
        body, name=name, grid=(n // nb, s // ts), in_specs=in_specs,
        out_specs=pl.BlockSpec((k, nb), lambda j, i: (0, j)),
        out_shape=jax.ShapeDtypeStruct((k, n), F32),
    )(*a_list, *b_list)


def _norm_fwd(h, g, *, ts=512, name):
    s, d = h.shape
    ts = _tile_rows(ts, s)

    def body(h_ref, g_ref, n_ref):
        x = h_ref[...]
        r = lax.rsqrt(jnp.mean(x * x, axis=-1, keepdims=True) + EPS)
        n_ref[...] = ((x * r) * g_ref[...]).astype(BF16)

    return pl.pallas_call(
        body, name=name, grid=(s // ts,),
        in_specs=[pl.BlockSpec((ts, d), lambda i: (i, 0)), pl.BlockSpec((1, d), lambda i: (0, 0))],
        out_specs=pl.BlockSpec((ts, d), lambda i: (i, 0)),
        out_shape=jax.ShapeDtypeStruct((s, d), BF16),
    )(h, g)


def _norm_bwd(dn, h, g, dres, *, ts=512, name):
    s, d = h.shape
    ts = _tile_rows(ts, s)

    def body(dn_ref, h_ref, g_ref, dres_ref, dh_ref, dg_ref):
        i = pl.program_id(0)
        x = h_ref[...]
        dnv = dn_ref[...]
        r = lax.rsqrt(jnp.mean(x * x, axis=-1, keepdims=True) + EPS)
        xhat = x * r
        part = jnp.sum(dnv * xhat, axis=0, keepdims=True)

        @pl.when(i == 0)
        def _():
            dg_ref[...] = part

        @pl.when(i > 0)
        def _():
            dg_ref[...] += part

        dxh = dnv * g_ref[...]
        dh_ref[...] = dres_ref[...] + r * (dxh - xhat * jnp.mean(dxh * xhat, axis=-1, keepdims=True))

    tile = pl.BlockSpec((ts, d), lambda i: (i, 0))
    vec = pl.BlockSpec((1, d), lambda i: (0, 0))
    return pl.pallas_call(
        body, name=name, grid=(s // ts,), in_specs=[tile, tile, vec, tile],
        out_specs=(tile, vec),
        out_shape=(jax.ShapeDtypeStruct((s, d), F32), jax.ShapeDtypeStruct((1, d), F32)),
    )(dn, h, g, dres)


def _final(h, g, target, *, ts=512, name):
    s, d = h.shape
    ts = _tile_rows(ts, s)
    nt = s // ts

    def body(h_ref, g_ref, t_ref, dh_ref, loss_ref, dg_ref, acc_ref):
        i = pl.program_id(0)
        x = h_ref[...]
        r = lax.rsqrt(jnp.mean(x * x, axis=-1, keepdims=True) + EPS)
        xhat = x * r
        gv = g_ref[...]
        err = xhat * gv - t_ref[...]
        sq = jnp.sum(err * err, axis=0, keepdims=True)
        dy = err * (1.0 / d)
        part = jnp.sum(dy * xhat, axis=0, keepdims=True)

        @pl.when(i == 0)
        def _():
            acc_ref[...] = sq
            dg_ref[...] = part

        @pl.when(i > 0)
        def _():
            acc_ref[...] += sq
            dg_ref[...] += part

        dxh = dy * gv
        dh_ref[...] = r * (dxh - xhat * jnp.mean(dxh * xhat, axis=-1, keepdims=True))

        @pl.when(i == nt - 1)
        def _():
            tot = jnp.sum(acc_ref[...], axis=1, keepdims=True) * (0.5 / d)
            loss_ref[...] = jnp.broadcast_to(tot, (1, LANES))

    tile = pl.BlockSpec((ts, d), lambda i: (i, 0))
    vec = pl.BlockSpec((1, d), lambda i: (0, 0))
    return pl.pallas_call(
        body, name=name, grid=(nt,), in_specs=[tile, vec, tile],
        out_specs=(tile, pl.BlockSpec((1, LANES), lambda i: (0, 0)), vec),
        out_shape=(jax.ShapeDtypeStruct((s, d), F32), jax.ShapeDtypeStruct((1, LANES), F32),
                   jax.ShapeDtypeStruct((1, d), F32)),
        scratch_shapes=[pltpu.VMEM((1, d), F32)],
    )(h, g, target)


def _halo_map(ts, width_blocks):
    per = ts // SUBLANES

    def index(j, i):
        return (jnp.maximum(i * per - 1, 0), width_blocks(j))

    return index


def _even_gates(xc, wa, ba, wx, bx, sp):
    xb = xc.astype(BF16)
    r = _sigmoid(_dot(xb, wa) + ba)
    ig = _sigmoid(_dot(xb, wx) + bx)
    la = (-LRU_C) * r * sp
    a = jnp.exp(la)
    a2 = a * a
    m = jnp.sqrt(-jnp.tanh(la) * (1.0 + a2))
    return r, ig, la, a, a2, m


def _even_core_fwd(p, w4, b4, wa, ba, wx, bx, lam, w3, b3, *, ts=512, name):
    s = p.shape[0]
    ts = _tile_rows(ts, s)
    nt = s // ts
    nblk = 4

    def body(p_ref, ph_ref, w4_ref, b4_ref, wa_ref, ba_ref, wx_ref, bx_ref, lam_ref, w3_ref, b3_ref,
             ya_ref, yb_ref, hl_ref, hcar_ref):
        i = pl.program_id(1)
        first = (i > 0).astype(F32)
        xa = p_ref[:, 0:LANES]
        ga = p_ref[:, LANES:2 * LANES]
        cp = p_ref[:, 2 * LANES:3 * LANES]
        bp = p_ref[:, 3 * LANES:4 * LANES]
        vb = p_ref[:, 4 * LANES:5 * LANES]
        xa_h = ph_ref[:, 0:LANES] * first
        s_h = ph_ref[:, 2 * LANES:3 * LANES] * ph_ref[:, 4 * LANES:5 * LANES] * first

        xc = b4_ref[...] + w4_ref[3:4, :] * xa
        for k in range(3):
            xc = xc + w4_ref[k:k + 1, :] * _shift_down(xa, xa_h, 3 - k)
        sp = _softplus(-lam_ref[...])
        _, ig, _, a, _, m = _even_gates(xc, wa_ref[0], ba_ref[...], wx_ref[0], bx_ref[...], sp)
        u = m * (ig * xc)
        hs, acum = _scan_fwd(a, u)

        @pl.when(i == 0)
        def _():
            hcar_ref[...] = jnp.zeros_like(hcar_ref)

        hs = hs + acum * hcar_ref[0:1, :]
        hl_ref[...] = hs
        hcar_ref[0:1, :] = hl_ref[ts - 1:ts, :]
        ge, _ = _gelu(ga)
        ya_ref[...] = (hs * ge).astype(BF16)

        sv = cp * vb
        sc = b3_ref[...] + w3_ref[2:3, :] * sv
        for k in range(2):
            sc = sc + w3_ref[k:k + 1, :] * _shift_down(sv, s_h, 2 - k)
        yb_ref[...] = (bp * sc).astype(BF16)

    blk = pl.BlockSpec((ts, 5 * LANES), lambda j, i: (i, j))
    halo = pl.BlockSpec((SUBLANES, 5 * LANES), _halo_map(ts, lambda j: j))
    vec = pl.BlockSpec((1, LANES), lambda j, i: (0, j))
    out = pl.BlockSpec((ts, LANES), lambda j, i: (i, j))
    return pl.pallas_call(
        body, name=name, grid=(nblk, nt),
        in_specs=[blk, halo,
                  pl.BlockSpec((4, LANES), lambda j, i: (0, j)), vec,
                  pl.BlockSpec((1, LANES, LANES), lambda j, i: (j, 0, 0)), vec,
                  pl.BlockSpec((1, LANES, LANES), lambda j, i: (j, 0, 0)), vec, vec,
                  pl.BlockSpec((3, LANES), lambda j, i: (0, j)), vec],
        out_specs=(out, out, out),
        out_shape=(jax.ShapeDtypeStruct((s, 4 * LANES), BF16), jax.ShapeDtypeStruct((s, 4 * LANES), BF16),
                   jax.ShapeDtypeStruct((s, 4 * LANES), F32)),
        scratch_shapes=[pltpu.VMEM((SUBLANES, LANES), F32)],
    )(p, p, w4, b4, wa, ba, wx, bx, lam, w3, b3)


def _even_core_bwd(dy, p, hl, w4, b4, wa, wat, ba, wx, wxt, bx, lam, w3, b3, *, ts=256, name):
    s = p.shape[0]
    ts = _tile_rows(ts, s)
    nt = s // ts
    nblk = 4
    per = ts // SUBLANES

    def body(dya_ref, dyb_ref, p_ref, ph_ref, hl_ref, hh_ref,
             w4_ref, b4_ref, wa_ref, wat_ref, ba_ref, wx_ref, wxt_ref, bx_ref, lam_ref, w3_ref, b3_ref,
             dp_ref, dw4_ref, db4_ref, dwa_ref, dba_ref, dwx_ref, dbx_ref, dlam_ref, dw3_ref, db3_ref,
             dxc_nx, dsc_nx, cg_ref):
        i = pl.program_id(1)
        ti = nt - 1 - i
        first = (ti > 0).astype(F32)
        xa = p_ref[:, 0:LANES]
        ga = p_ref[:, LANES:2 * LANES]
        cp = p_ref[:, 2 * LANES:3 * LANES]
        bp = p_ref[:, 3 * LANES:4 * LANES]
        vb = p_ref[:, 4 * LANES:5 * LANES]
        xa_h = ph_ref[:, 0:LANES] * first
        s_h = ph_ref[:, 2 * LANES:3 * LANES] * ph_ref[:, 4 * LANES:5 * LANES] * first
        h_h = hh_ref[...] * first

        @pl.when(i == 0)
        def _():
            dxc_nx[...] = jnp.zeros_like(dxc_nx)
            dsc_nx[...] = jnp.zeros_like(dsc_nx)
            cg_ref[...] = jnp.zeros_like(cg_ref)
            for ref in (dw4_ref, db4_ref, dwa_ref, dba_ref, dwx_ref, dbx_ref, dlam_ref, dw3_ref, db3_ref):
                ref[...] = jnp.zeros_like(ref)

        xa_sh = [_shift_down(xa, xa_h, 3 - k) for k in range(3)] + [xa]
        xc = b4_ref[...]
        for k in range(4):
            xc = xc + w4_ref[k:k + 1, :] * xa_sh[k]
        lamv = lam_ref[...]
        sp = _softplus(-lamv)
        r, ig, _, a, a2, m = _even_gates(xc, wa_ref[0], ba_ref[...], wx_ref[0], bx_ref[...], sp)
        sv = cp * vb
        sv_sh = [_shift_down(sv, s_h, 2 - k) for k in range(2)] + [sv]
        sc = b3_ref[...]
        for k in range(3):
            sc = sc + w3_ref[k:k + 1, :] * sv_sh[k]
        hs = hl_ref[...]
        h_prev = _shift_down(hs, h_h, 1)

        dya = dya_ref[...]
        dyb = dyb_ref[...]
        ge, gt = _gelu(ga)
        dga = dya * hs * _gelu_grad(ga, gt)
        dh = dya * ge

        ones8 = jnp.ones((SUBLANES, LANES), F32)
        b = _shift_up(a, ones8, 1)
        g, bcum = _scan_rev(b, dh)
        g = g + bcum * cg_ref[0:1, :]
        ag = a * g
        cg_ref[...] = ag[:SUBLANES]

        da = g * h_prev
        xi = ig * xc
        dm = g * xi
        dig = g * m * xc
        dxc = g * m * ig
        dla = da * a - dm * (a2 / m)
        dr = dla * ((-LRU_C) * sp)
        dlam_ref[...] += jnp.sum(dla * r, axis=0, keepdims=True) * (LRU_C * _sigmoid(-lamv))
        dra = dr * r * (1.0 - r)
        dia = dig * ig * (1.0 - ig)
        drab = dra.astype(BF16)
        diab = dia.astype(BF16)
        xcb = xc.astype(BF16)
        dxc = dxc + _dot(drab, wat_ref[0]) + _dot(diab, wxt_ref[0])
        dwa_ref[0] += _dot_tn(xcb, drab)
        dwx_ref[0] += _dot_tn(xcb, diab)
        dba_ref[...] += jnp.sum(dra, axis=0, keepdims=True)
        dbx_ref[...] += jnp.sum(dia, axis=0, keepdims=True)

        nx = dxc_nx[...]
        dxa = w4_ref[3:4, :] * dxc
        for k in range(3):
            dxa = dxa + w4_ref[k:k + 1, :] * _shift_up(dxc, nx, 3 - k)
        for k in range(4):
            dw4_ref[k:k + 1, :] += jnp.sum(dxc * xa_sh[k], axis=0, keepdims=True)
        db4_ref[...] += jnp.sum(dxc, axis=0, keepdims=True)
        dxc_nx[...] = dxc[:SUBLANES]

        dbp = dyb * sc
        dsc = dyb * bp
        nsc = dsc_nx[...]
        ds = w3_ref[2:3, :] * dsc
        for k in range(2):
            ds = ds + w3_ref[k:k + 1, :] * _shift_up(dsc, nsc, 2 - k)
        for k in range(3):
            dw3_ref[k:k + 1, :] += jnp.sum(dsc * sv_sh[k], axis=0, keepdims=True)
        db3_ref[...] += jnp.sum(dsc, axis=0, keepdims=True)
        dsc_nx[...] = dsc[:SUBLANES]

        dp_ref[:, 0:LANES] = dxa.astype(BF16)
        dp_ref[:, LANES:2 * LANES] = dga.astype(BF16)
        dp_ref[:, 2 * LANES:3 * LANES] = (ds * vb).astype(BF16)
        dp_ref[:, 3 * LANES:4 * LANES] = dbp.astype(BF16)
        dp_ref[:, 4 * LANES:5 * LANES] = (ds * cp).astype(BF16)

    def rev(j, i):
        return (nt - 1 - i, j)

    def rev_halo(col):
        def index(j, i):
            return (jnp.maximum((nt - 1 - i) * per - 1, 0), col(j))
        return index

    blk = pl.BlockSpec((ts, 5 * LANES), rev)
    one = pl.BlockSpec((ts, LANES), rev)
    vec = pl.BlockSpec((1, LANES), lambda j, i: (0, j))
    mat = pl.BlockSpec((1, LANES, LANES), lambda j, i: (j, 0, 0))
    w4s = pl.BlockSpec((4, LANES), lambda j, i: (0, j))
    w3s = pl.BlockSpec((3, LANES), lambda j, i: (0, j))
    f = jax.ShapeDtypeStruct
    return pl.pallas_call(
        body, name=name, grid=(nblk, nt),
        in_specs=[one, pl.BlockSpec((ts, LANES), lambda j, i: (nt - 1 - i, 4 + j)),
                  blk, pl.BlockSpec((SUBLANES, 5 * LANES), rev_halo(lambda j: j)),
                  one, pl.BlockSpec((SUBLANES, LANES), rev_halo(lambda j: j)),
                  w4s, vec, mat, mat, vec, mat, mat, vec, vec, w3s, vec],
        out_specs=(blk, w4s, vec, mat, vec, mat, vec, vec, w3s, vec),
        out_shape=(f((s, 20 * LANES), BF16), f((4, 4 * LANES), F32), f((1, 4 * LANES), F32),
                   f((4, LANES, LANES), F32), f((1, 4 * LANES), F32),
                   f((4, LANES, LANES), F32), f((1, 4 * LANES), F32), f((1, 4 * LANES), F32),
                   f((3, 4 * LANES), F32), f((1, 4 * LANES), F32)),
        scratch_shapes=[pltpu.VMEM((SUBLANES, LANES), F32), pltpu.VMEM((SUBLANES, LANES), F32),
                        pltpu.VMEM((SUBLANES, LANES), F32)],
    )(dy, dy, p, p, hl, hl, w4, b4, wa, wat, ba, wx, wxt, bx, lam, w3, b3)


def _ffn_conv(u_ref, uh_ref, w_ref, b_ref, first):
    u = u_ref[...].astype(F32)
    u_h = uh_ref[...].astype(F32)[SUBLANES:] * first
    u_sh = [_shift_down(u, u_h, 2 - k) for k in range(2)] + [u]
    hc = b_ref[...]
    for k in range(3):
        hc = hc + w_ref[k:k + 1, :] * u_sh[k]
    return hc, u_sh


def _ffn_specs(ts, row, halo_row):
    nblk = D_FF // FFN_CB
    specs = []
    for off in (0, nblk):
        specs.append(pl.BlockSpec((ts, FFN_CB), lambda j, i, off=off: (row(i), off + j)))
        specs.append(pl.BlockSpec((16, FFN_CB), lambda j, i, off=off: (halo_row(i), off + j)))
        specs.append(pl.BlockSpec((3, FFN_CB), lambda j, i, off=off: (0, off + j)))
        specs.append(pl.BlockSpec((1, FFN_CB), lambda j, i, off=off: (0, off + j)))
    return specs


def _ffn_core_fwd(up, w, b, *, ts=512, name):
    s = up.shape[0]
    ts = _tile_rows(ts, s)
    nt = s // ts
    nblk = D_FF // FFN_CB
    per = ts // 16

    def body(g_ref, gh_ref, wg_ref, bg_ref, v_ref, vh_ref, wv_ref, bv_ref, act_ref):
        first = (pl.program_id(1) > 0).astype(F32)
        gate, _ = _ffn_conv(g_ref, gh_ref, wg_ref, bg_ref, first)
        val, _ = _ffn_conv(v_ref, vh_ref, wv_ref, bv_ref, first)
        act_ref[...] = (gate * _sigmoid(gate) * val).astype(BF16)

    return pl.pallas_call(
        body, name=name, grid=(nblk, nt),
        in_specs=_ffn_specs(ts, lambda i: i, lambda i: jnp.maximum(i * per - 1, 0)),
        out_specs=pl.BlockSpec((ts, FFN_CB), lambda j, i: (i, j)),
        out_shape=jax.ShapeDtypeStruct((s, D_FF), BF16),
    )(up, up, w, b, up, up, w, b)


def _ffn_core_bwd(dact, up, w, b, *, ts=512, name):
    s = up.shape[0]
    ts = _tile_rows(ts, s)
    nt = s // ts
    nblk = D_FF // FFN_CB
    per = ts // 16

    def conv_bwd(dhc, u_sh, w_ref, nx_ref, du_ref, dw_ref, db_ref):
        nx = nx_ref[...]
        du = w_ref[2:3, :] * dhc
        for k in range(2):
            du = du + w_ref[k:k + 1, :] * _shift_up(dhc, nx, 2 - k)
        du_ref[...] = du.astype(BF16)
        for k in range(3):
            dw_ref[k:k + 1, :] += jnp.sum(dhc * u_sh[k], axis=0, keepdims=True)
        db_ref[...] += jnp.sum(dhc, axis=0, keepdims=True)
        nx_ref[...] = dhc[:SUBLANES]

    def body(da_ref, g_ref, gh_ref, wg_ref, bg_ref, v_ref, vh_ref, wv_ref, bv_ref,
             dg_ref, dv_ref, dwg_ref, dwv_ref, dbg_ref, dbv_ref, nxg_ref, nxv_ref):
        i = pl.program_id(1)
        first = (nt - 1 - i > 0).astype(F32)
        gate, g_sh = _ffn_conv(g_ref, gh_ref, wg_ref, bg_ref, first)
        val, v_sh = _ffn_conv(v_ref, vh_ref, wv_ref, bv_ref, first)
        da = da_ref[...].astype(F32)
        sg = _sigmoid(gate)
        dgate = da * val * (sg * (1.0 + gate * (1.0 - sg)))
        dval = da * (gate * sg)

        @pl.when(i == 0)
        def _():
            for ref in (nxg_ref, nxv_ref, dwg_ref, dwv_ref, dbg_ref, dbv_ref):
                ref[...] = jnp.zeros_like(ref)

        conv_bwd(dgate, g_sh, wg_ref, nxg_ref, dg_ref, dwg_ref, dbg_ref)
        conv_bwd(dval, v_sh, wv_ref, nxv_ref, dv_ref, dwv_ref, dbv_ref)

    def rev(i):
        return nt - 1 - i

    tile = pl.BlockSpec((ts, FFN_CB), lambda j, i: (rev(i), j))
    w_out = pl.BlockSpec((3, FFN_CB), lambda j, i: (0, j))
    b_out = pl.BlockSpec((1, FFN_CB), lambda j, i: (0, j))
    f = jax.ShapeDtypeStruct
    return pl.pallas_call(
        body, name=name, grid=(nblk, nt),
        in_specs=[tile] + _ffn_specs(ts, rev, lambda i: jnp.maximum(rev(i) * per - 1, 0)),
        out_specs=(tile, tile, w_out, w_out, b_out, b_out),
        out_shape=(f((s, D_FF), BF16), f((s, D_FF), BF16), f((3, D_FF), F32), f((3, D_FF), F32),
                   f((1, D_FF), F32), f((1, D_FF), F32)),
        scratch_shapes=[pltpu.VMEM((SUBLANES, FFN_CB), F32), pltpu.VMEM((SUBLANES, FFN_CB), F32)],
    )(dact, up, up, w, b, up, up, w, b)


def _sgu_forward_block(zu, zg, gn, w_ref, bias, seg):
    u, tu = _gelu(zu)
    g, tg = _gelu(zg)
    ms = _dot_split(g * g, seg)
    rs = lax.rsqrt(ms + EPS)
    ghat = g * rs
    gv = ghat * gn
    gvb = gv.astype(BF16)
    lane = _lanes((CHUNK, LANES))
    chunks = []
    for c in range(zu.shape[0] // CHUNK):
        gc = gvb[c * CHUNK:(c + 1) * CHUNK]
        mix = jnp.where(lane < 64, _dot(w_ref[0], gc), _dot(w_ref[1], gc)) + bias
        chunks.append(mix)
    mixed = chunks[0] if len(chunks) == 1 else jnp.concatenate(chunks, axis=0)
    return u, tu, g, tg, rs, ghat, gvb, mixed


def _sgu_fwd(p1, gn, w, bias, seg, *, ts=512, name):
    s = p1.shape[0]
    ts = _tile_rows(ts, s)

    def body(zu_ref, zg_ref, gn_ref, w_ref, bias_ref, seg_ref, yc_ref):
        u, _, _, _, _, _, _, mixed = _sgu_forward_block(
            zu_ref[...], zg_ref[...], gn_ref[...], w_ref, bias_ref[...], seg_ref[...])
        yc_ref[...] = (u * mixed).astype(BF16)

    return pl.pallas_call(
        body, name=name, grid=(4, s // ts),
        in_specs=[pl.BlockSpec((ts, LANES), lambda j, i: (i, j)),
                  pl.BlockSpec((ts, LANES), lambda j, i: (i, 4 + j)),
                  pl.BlockSpec((1, LANES), lambda j, i: (0, j)),
                  pl.BlockSpec((2, CHUNK, CHUNK), lambda j, i: (j, 0, 0)),
                  pl.BlockSpec((CHUNK, LANES), lambda j, i: (0, j)),
                  pl.BlockSpec((LANES, LANES), lambda j, i: (0, 0))],
        out_specs=pl.BlockSpec((ts, LANES), lambda j, i: (i, j)),
        out_shape=jax.ShapeDtypeStruct((s, 4 * LANES), BF16),
    )(p1, p1, gn, w, bias, seg)


def _sgu_bwd(p1, dy, gn, w, wt, bias, seg, tril, *, ts=512, name):
    s = p1.shape[0]
    ts = _tile_rows(ts, s)
    nt = s // ts

    def body(zu_ref, zg_ref, dy_ref, gn_ref, w_ref, wt_ref, bias_ref, seg_ref, tril_ref,
             dzu_ref, dzg_ref, dw_ref, dbias_ref, dgn_ref):
        i = pl.program_id(1)
        zu = zu_ref[...]
        zg = zg_ref[...]
        gn_v = gn_ref[...]
        segv = seg_ref[...]
        u, tu, g, tg, rs, ghat, gvb, mixed = _sgu_forward_block(zu, zg, gn_v, w_ref, bias_ref[...], segv)
        dyv = dy_ref[...]
        du = dyv * mixed
        dmx = dyv * u

        @pl.when(i == 0)
        def _():
            dw_ref[...] = jnp.zeros_like(dw_ref)
            dbias_ref[...] = jnp.zeros_like(dbias_ref)
            dgn_ref[...] = jnp.zeros_like(dgn_ref)

        lane = _lanes((CHUNK, LANES))
        dgv_chunks = []
        dbias = jnp.zeros((CHUNK, LANES), F32)
        for c in range(ts // CHUNK):
            dmc = dmx[c * CHUNK:(c + 1) * CHUNK]
            gc = gvb[c * CHUNK:(c + 1) * CHUNK]
            dm_a = jnp.where(lane < 64, dmc, 0.0).astype(BF16)
            dm_b = jnp.where(lane >= 64, dmc, 0.0).astype(BF16)
            dw_ref[0] += _dot_nt(dm_a, gc)
            dw_ref[1] += _dot_nt(dm_b, gc)
            dgv_chunks.append(_dot(wt_ref[0], dm_a) + _dot(wt_ref[1], dm_b))
            dbias = dbias + dmc
        dbias_ref[...] += dbias
        dgv = dgv_chunks[0] if len(dgv_chunks) == 1 else jnp.concatenate(dgv_chunks, axis=0)
        dgn_ref[...] += jnp.sum(dgv * ghat, axis=0, keepdims=True)
        dgh = dgv * gn_v
        dg = rs * (dgh - ghat * _dot_split(dgh * ghat, segv))
        dzu_ref[...] = (du * _gelu_grad(zu, tu)).astype(BF16)
        dzg_ref[...] = (dg * _gelu_grad(zg, tg)).astype(BF16)

        @pl.when(i == nt - 1)
        def _():
            dw_ref[0] = dw_ref[0] * tril_ref[...]
            dw_ref[1] = dw_ref[1] * tril_ref[...]

    f = jax.ShapeDtypeStruct
    colj = pl.BlockSpec((ts, LANES), lambda j, i: (i, j))
    wsp = pl.BlockSpec((2, CHUNK, CHUNK), lambda j, i: (j, 0, 0))
    sq = pl.BlockSpec((LANES, LANES), lambda j, i: (0, 0))
    return pl.pallas_call(
        body, name=name, grid=(4, nt),
        in_specs=[colj, pl.BlockSpec((ts, LANES), lambda j, i: (i, 4 + j)), colj,
                  pl.BlockSpec((1, LANES), lambda j, i: (0, j)), wsp, wsp,
                  pl.BlockSpec((CHUNK, LANES), lambda j, i: (0, j)), sq, sq],
        out_specs=(colj, colj, wsp, pl.BlockSpec((CHUNK, LANES), lambda j, i: (0, j)),
                   pl.BlockSpec((1, LANES), lambda j, i: (0, j))),
        out_shape=(f((s, 4 * LANES), BF16), f((s, 4 * LANES), BF16), f((8, CHUNK, CHUNK), F32),
                   f((CHUNK, 4 * LANES), F32), f((1, 4 * LANES), F32)),
    )(p1, p1, dy, gn, w, wt, bias, seg, tril)


F_COL = 20


def _fcum_fwd(p1, bf, *, ts=512, name):
    s = p1.shape[0]
    ts = _tile_rows(ts, s)

    def body(f_ref, bf_ref, c_ref, car_ref):
        i = pl.program_id(0)
        z = f_ref[...] + bf_ref[...]
        logf = jnp.minimum(z, 0.0) - _log1p_pos(jnp.exp(-jnp.abs(z)))

        @pl.when(i == 0)
        def _():
            car_ref[...] = jnp.zeros_like(car_ref)

        c_ref[...] = _cumsum_fwd(logf) + car_ref[0:1, :]
        car_ref[0:1, :] = c_ref[ts - 1:ts, :]

    return pl.pallas_call(
        body, name=name, grid=(s // ts,),
        in_specs=[pl.BlockSpec((ts, LANES), lambda i: (i, F_COL)), pl.BlockSpec((1, LANES), lambda i: (0, 0))],
        out_specs=pl.BlockSpec((ts, LANES), lambda i: (i, 0)),
        out_shape=jax.ShapeDtypeStruct((s, LANES), F32),
        scratch_shapes=[pltpu.VMEM((SUBLANES, LANES), F32)],
    )(p1, bf)


def _fcum_bwd(dcs, p1, bf, *, ts=512, name):
    s = p1.shape[0]
    ts = _tile_rows(ts, s)
    nt = s // ts

    def body(dc_ref, f_ref, bf_ref, df_ref, dbf_ref, car_ref):
        i = pl.program_id(0)

        @pl.when(i == 0)
        def _():
            car_ref[...] = jnp.zeros_like(car_ref)
            dbf_ref[...] = jnp.zeros_like(dbf_ref)

        dlog = _cumsum_rev(dc_ref[...]) + car_ref[0:1, :]
        car_ref[...] = dlog[:SUBLANES]
        z = f_ref[...] + bf_ref[...]
        df = dlog * _sigmoid(-z)
        df_ref[...] = df.astype(BF16)
        dbf_ref[...] += jnp.sum(df, axis=0, keepdims=True)

    return pl.pallas_call(
        body, name=name, grid=(nt,),
        in_specs=[pl.BlockSpec((ts, LANES), lambda i: (nt - 1 - i, 0)),
                  pl.BlockSpec((ts, LANES), lambda i: (nt - 1 - i, F_COL)),
                  pl.BlockSpec((1, LANES), lambda i: (0, 0))],
        out_specs=(pl.BlockSpec((ts, LANES), lambda i: (nt - 1 - i, 0)), pl.BlockSpec((1, LANES), lambda i: (0, 0))),
        out_shape=(jax.ShapeDtypeStruct((s, LANES), BF16), jax.ShapeDtypeStruct((1, LANES), F32)),
        scratch_shapes=[pltpu.VMEM((SUBLANES, LANES), F32)],
    )(dcs, p1, bf)


def _fox_scores(qm, kb, cq, ck, diagonal, tq, tk):
    sc = _dot_nt(qm, kb) + jnp.tile(cq, (1, tk // LANES)) - ck
    if diagonal:
        sc = jnp.where(_lanes((tq, tk)) <= _rows((tq, tk)), sc, NEG)
    return sc


def _fox_fwd(p1, cq, ck, *, tq=512, name):
    s = p1.shape[0]
    tq = _tile_rows(tq, s)
    tk = tq
    nq = s // tq

    def body(q_ref, k_ref, v_ref, cq_ref, ck_ref, o_ref, lse_ref, m_ref, l_ref, acc_ref):
        qi = pl.program_id(1)
        kj = pl.program_id(2)

        @pl.when(kj == 0)
        def _():
            m_ref[...] = jnp.full_like(m_ref, NEG)
            l_ref[...] = jnp.zeros_like(l_ref)
            acc_ref[...] = jnp.zeros_like(acc_ref)

        def step(diagonal):
            q = q_ref[...] * 0.125
            kb = k_ref[...].astype(BF16)
            vb = v_ref[...].astype(BF16)
            lane = _lanes((tq, LANES))
            outs = []
            for hh in range(2):
                sel = (lane < 64) if hh == 0 else (lane >= 64)
                qm = jnp.where(sel, q, 0.0).astype(BF16)
                sc = _fox_scores(qm, kb, cq_ref[:, hh * LANES:(hh + 1) * LANES], ck_ref[hh], diagonal, tq, tk)
                m_prev = m_ref[hh]
                m_new = jnp.maximum(m_prev, jnp.max(sc, axis=1, keepdims=True))
                pm = jnp.exp(sc - jnp.tile(m_new, (1, tk // LANES)))
                alpha = jnp.exp(m_prev - m_new)
                l_ref[hh] = alpha * l_ref[hh] + jnp.sum(pm, axis=1, keepdims=True)
                m_ref[hh] = m_new
                outs.append(acc_ref[...] * alpha + _dot(pm.astype(BF16), vb))
            acc_ref[...] = jnp.where(lane < 64, outs[0], outs[1])

        pl.when(kj < qi)(functools.partial(step, False))
        pl.when(kj == qi)(functools.partial(step, True))

        @pl.when(kj == qi)
        def _():
            lane = _lanes((tq, LANES))
            l_sel = jnp.where(lane < 64, l_ref[0], l_ref[1])
            o_ref[...] = (acc_ref[...] / l_sel).astype(BF16)
            lse_ref[:, 0:LANES] = m_ref[0] + jnp.log(l_ref[0])
            lse_ref[:, LANES:2 * LANES] = m_ref[1] + jnp.log(l_ref[1])

    def kmap(col0):
        return lambda j, qi, kj: (jnp.minimum(kj, qi), col0 + j)

    return pl.pallas_call(
        body, name=name, grid=(4, nq, nq),
        in_specs=[pl.BlockSpec((tq, LANES), lambda j, qi, kj: (qi, 8 + j)),
                  pl.BlockSpec((tk, LANES), kmap(12)), pl.BlockSpec((tk, LANES), kmap(16)),
                  pl.BlockSpec((tq, 2 * LANES), lambda j, qi, kj: (qi, j)),
                  pl.BlockSpec((2, 1, tk), lambda j, qi, kj: (j, 0, jnp.minimum(kj, qi)))],
        out_specs=(pl.BlockSpec((tq, LANES), lambda j, qi, kj: (qi, j)),
                   pl.BlockSpec((tq, 2 * LANES), lambda j, qi, kj: (qi, j))),
        out_shape=(jax.ShapeDtypeStruct((s, 4 * LANES), BF16), jax.ShapeDtypeStruct((s, 8 * LANES), F32)),
        scratch_shapes=[pltpu.VMEM((2, tq, LANES), F32), pltpu.VMEM((2, tq, LANES), F32),
                        pltpu.VMEM((tq, LANES), F32)],
    )(p1, p1, p1, cq, ck)


def _fox_delta(dy, o, sel, *, ts=512, name):
    s = o.shape[0]
    ts = _tile_rows(ts, s)

    def body(do_ref, o_ref, sel_ref, d_ref):
        prod = do_ref[...] * o_ref[...].astype(F32)
        d_ref[:, 0:LANES] = _dot_split(prod, sel_ref[0])
        d_ref[:, LANES:2 * LANES] = _dot_split(prod, sel_ref[1])

    return pl.pallas_call(
        body, name=name, grid=(4, s // ts),
        in_specs=[pl.BlockSpec((ts, LANES), lambda j, i: (i, 4 + j)),
                  pl.BlockSpec((ts, LANES), lambda j, i: (i, j)),
                  pl.BlockSpec((2, LANES, LANES), lambda j, i: (0, 0, 0))],
        out_specs=pl.BlockSpec((ts, 2 * LANES), lambda j, i: (i, j)),
        out_shape=jax.ShapeDtypeStruct((s, 8 * LANES), F32),
    )(dy, o, sel)


def _fox_bwd(p1, dy, lse, delta, cq, ck, *, tq=512, name):
    s = p1.shape[0]
    tq = _tile_rows(tq, s)
    tk = tq
    nq = s // tq

    def body(q_ref, k_ref, v_ref, do_ref, lse_ref, dl_ref, cq_ref, ck_ref,
             dq_ref, dk_ref, dv_ref, dck_ref, dcq_ref, dka_ref, dva_ref, dca_ref, dqa_ref, dra_ref):
        kj = pl.program_id(1)
        qi = pl.program_id(2)

        @pl.when((kj == 0) & (qi == 0))
        def _():
            dqa_ref[...] = jnp.zeros_like(dqa_ref)
            dra_ref[...] = jnp.zeros_like(dra_ref)

        @pl.when(qi == 0)
        def _():
            dka_ref[...] = jnp.zeros_like(dka_ref)
            dva_ref[...] = jnp.zeros_like(dva_ref)
            dca_ref[...] = jnp.zeros_like(dca_ref)

        def step(diagonal):
            q = q_ref[...] * 0.125
            kf = k_ref[...]
            kb = kf.astype(BF16)
            vb = v_ref[...].astype(BF16)
            do = do_ref[...]
            lane = _lanes((tq, LANES))
            klane = _lanes((tk, LANES))
            rows = pl.ds(pl.multiple_of(qi * tq, tq), tq)
            for hh in range(2):
                sel = (lane < 64) if hh == 0 else (lane >= 64)
                ksel = (klane < 64) if hh == 0 else (klane >= 64)
                qm = jnp.where(sel, q, 0.0).astype(BF16)
                dom = jnp.where(sel, do, 0.0).astype(BF16)
                km = jnp.where(ksel, kf, 0.0).astype(BF16)
                sc = _fox_scores(qm, kb, cq_ref[:, hh * LANES:(hh + 1) * LANES], ck_ref[hh], diagonal, tq, tk)
                pm = jnp.exp(sc - jnp.tile(lse_ref[:, hh * LANES:(hh + 1) * LANES], (1, tk // LANES)))
                dva_ref[...] += _dot_tn(pm.astype(BF16), dom)
                dp = _dot_nt(dom, vb)
                ds = pm * (dp - jnp.tile(dl_ref[:, hh * LANES:(hh + 1) * LANES], (1, tk // LANES)))
                dsb = ds.astype(BF16)
                dka_ref[...] += _dot_tn(dsb, qm)
                dca_ref[hh] -= jnp.sum(ds, axis=0, keepdims=True)
                dqa_ref[rows, :] += _dot(dsb, km)
                dra_ref[hh, rows, :] += jnp.sum(ds, axis=1, keepdims=True)

        pl.when(qi > kj)(functools.partial(step, False))
        pl.when(qi == kj)(functools.partial(step, True))

        @pl.when(qi == nq - 1)
        def _():
            dk_ref[...] = dka_ref[...].astype(BF16)
            dv_ref[...] = dva_ref[...].astype(BF16)
            dck_ref[...] = dca_ref[...]

        @pl.when((kj == nq - 1) & (qi == nq - 1))
        def _():
            dq_ref[...] = (dqa_ref[...] * 0.125).astype(BF16)
            dcq_ref[:, 0:LANES] = dra_ref[0]
            dcq_ref[:, LANES:2 * LANES] = dra_ref[1]

    def qmap(col0):
        return lambda j, kj, qi: (jnp.maximum(qi, kj), col0 + j)

    pair = pl.BlockSpec((tq, 2 * LANES), qmap(0))
    kblk = pl.BlockSpec((tk, LANES), lambda j, kj, qi: (kj, j))
    f = jax.ShapeDtypeStruct
    return pl.pallas_call(
        body, name=name, grid=(4, nq, nq),
        in_specs=[pl.BlockSpec((tq, LANES), qmap(8)),
                  pl.BlockSpec((tk, LANES), lambda j, kj, qi: (kj, 12 + j)),
                  pl.BlockSpec((tk, LANES), lambda j, kj, qi: (kj, 16 + j)),
                  pl.BlockSpec((tq, LANES), qmap(4)), pair, pair, pair,
                  pl.BlockSpec((2, 1, tk), lambda j, kj, qi: (j, 0, kj))],
        out_specs=(pl.BlockSpec((s, LANES), lambda j, kj, qi: (0, j)), kblk, kblk,
                   pl.BlockSpec((2, 1, tk), lambda j, kj, qi: (j, 0, kj)),
                   pl.BlockSpec((s, 2 * LANES), lambda j, kj, qi: (0, j))),
        out_shape=(f((s, 4 * LANES), BF16), f((s, 4 * LANES), BF16), f((s, 4 * LANES), BF16),
                   f((8, 1, s), F32), f((s, 8 * LANES), F32)),
        scratch_shapes=[pltpu.VMEM((tk, LANES), F32), pltpu.VMEM((tk, LANES), F32), pltpu.VMEM((2, 1, tk), F32),
                        pltpu.VMEM((s, LANES), F32), pltpu.VMEM((2, s, LANES), F32)],
    )(p1, p1, p1, dy, lse, delta, cq, ck)


def _row_block(r, cap=256):
    best = None
    for rb in range(2 * SUBLANES, min(r, cap) + 1, 2 * SUBLANES):
        if r % rb == 0:
            best = rb
    return r if best is None else best


def _adamw(w, g, m, v, *, name):
    r, c = w.shape
    rb = _row_block(r)

    def body(w_ref, g_ref, m_ref, v_ref, d_ref, nm_ref, nv_ref):
        gv = g_ref[...]
        mn = ADAM_B1 * m_ref[...] + (1.0 - ADAM_B1) * gv
        vn = ADAM_B2 * v_ref[...] + (1.0 - ADAM_B2) * (gv * gv)
        m_hat = mn / ADAM_C1
        v_hat = vn / ADAM_C2
        d_ref[...] = (-ADAM_LR) * (m_hat / (jnp.sqrt(v_hat) + ADAM_EPS) + ADAM_WD * w_ref[...])
        nm_ref[...] = mn
        nv_ref[...] = vn

    blk = pl.BlockSpec((rb, c), lambda i: (i, 0))
    shp = jax.ShapeDtypeStruct((r, c), F32)
    return pl.pallas_call(
        body, name=name, grid=(r // rb,), in_specs=[blk] * 4, out_specs=(blk,) * 3, out_shape=(shp,) * 3,
    )(w, g, m, v)


def _pair_sum(g, col, ra, core, *, name):
    _, rh, c = ra.shape
    rb = _row_block(rh)

    def body(core_ref, g_ref, ra_ref, h_ref, h16_ref):
        tot = g_ref[...] + ra_ref[...]
        h_ref[...] = tot
        h16_ref[...] = tot.astype(BF16)

    if col:
        g_spec = pl.BlockSpec((None, rb, c), lambda k, i, core_ref: (core_ref[0], i, k))
    else:
        g_spec = pl.BlockSpec((None, None, rb, c), lambda k, i, core_ref: (k, core_ref[0], i, 0))
    slot = pl.BlockSpec((None, rb, c), lambda k, i, core_ref: (k, i, 0))
    return pl.pallas_call(
        body, name=name,
        grid_spec=pltpu.PrefetchScalarGridSpec(
            num_scalar_prefetch=1, grid=(N_CHIPS, rh // rb), in_specs=[g_spec, slot], out_specs=(slot, slot)),
        out_shape=(jax.ShapeDtypeStruct((N_CHIPS, rh, c), F32), jax.ShapeDtypeStruct((N_CHIPS, rh, c), BF16)),
    )(core, g, ra)


def _first_sum(h, r1, keep, *, name):
    _, rh, c = h.shape
    rb = _row_block(rh)

    def body(keep_ref, h_ref, r_ref, s_ref, s16_ref):
        tot = h_ref[...] + r_ref[...].astype(F32)
        s_ref[...] = tot
        s16_ref[...] = tot.astype(BF16)

    slot = pl.BlockSpec((None, rb, c), lambda t, i, keep_ref: (t, i, 0))
    return pl.pallas_call(
        body, name=name,
        grid_spec=pltpu.PrefetchScalarGridSpec(
            num_scalar_prefetch=1, grid=(2, rh // rb),
            in_specs=[pl.BlockSpec((None, rb, c), lambda t, i, keep_ref: (keep_ref[t], i, 0)), slot],
            out_specs=(slot, slot)),
        out_shape=(jax.ShapeDtypeStruct((2, rh, c), F32), jax.ShapeDtypeStruct((2, rh, c), BF16)),
    )(keep, h, r1)


def _second_sum(s1, r2, sel, *, name):
    _, rh, c = s1.shape
    rb = _row_block(rh)

    def body(sel_ref, s_ref, r_ref, t_ref):
        t_ref[...] = s_ref[...] + r_ref[...].astype(F32)

    return pl.pallas_call(
        body, name=name,
        grid_spec=pltpu.PrefetchScalarGridSpec(
            num_scalar_prefetch=1, grid=(rh // rb,),
            in_specs=[pl.BlockSpec((None, rb, c), lambda i, sel_ref: (sel_ref[0], i, 0)),
                      pl.BlockSpec((rb, c), lambda i, sel_ref: (i, 0))],
            out_specs=pl.BlockSpec((None, rb, c), lambda i, sel_ref: (sel_ref[1], i, 0))),
        out_shape=jax.ShapeDtypeStruct((2, rh, c), F32),
    )(sel, s1, r2)


def _place(shard, col, chip, dtype, *, name):
    r, c = shard.shape
    rh = r // 2
    rb = _row_block(rh)

    def body(chip_ref, s_ref, o_ref):
        o_ref[...] = s_ref[...].astype(o_ref.dtype)

    if col:
        out_spec = pl.BlockSpec((None, rb, c), lambda h, i, chip_ref: (h, i, chip_ref[0]))
        shape = (2, rh, N_CHIPS * c)
    else:
        out_spec = pl.BlockSpec((None, None, rb, c), lambda h, i, chip_ref: (chip_ref[0], h, i, 0))
        shape = (N_CHIPS, 2, rh, c)
    per = rh // rb
    return pl.pallas_call(
        body, name=name,
        grid_spec=pltpu.PrefetchScalarGridSpec(
            num_scalar_prefetch=1, grid=(2, per),
            in_specs=[pl.BlockSpec((rb, c), lambda h, i, chip_ref: (h * per + i, 0))], out_specs=out_spec),
        out_shape=jax.ShapeDtypeStruct(shape, dtype),
    )(chip, shard)


ANY = pl.BlockSpec(memory_space=pl.ANY)


def _mesh_pos():
    return lax.axis_index("x"), lax.axis_index("y"), lax.axis_index("c")


def _other_chips(x, y):
    return [(1 - x, y), (x, 1 - y), (1 - x, 1 - y)]


def _remote(src, dst, ssem, rsem, dev):
    return pltpu.make_async_remote_copy(src_ref=src, dst_ref=dst, send_sem=ssem, recv_sem=rsem,
                                        device_id=dev, device_id_type=MESH)


def _flip(a, b):
    return a + b - 2 * a * b


def _slab(ref, col, width, k, h):
    if not col:
        return ref.at[k, h]
    start = k * width if isinstance(k, int) else pl.multiple_of(k * width, LANES)
    return ref.at[h, :, pl.ds(start, width)]


def _all_gather(bufs, cols, *, name):
    n = len(bufs)
    widths = [b.shape[2] // N_CHIPS if col else b.shape[3] for b, col in zip(bufs, cols)]

    def body(*refs):
        outs = refs[n:2 * n]
        ssem, rsem = refs[2 * n:]
        x, y, c = _mesh_pos()
        me = 2 * x + y
        sib = (x, y, 1 - c)
        n1 = (_flip(x, 1 - c), _flip(y, c))
        n2 = (_flip(x, c), _flip(y, 1 - c))
        k1 = 2 * n1[0] + n1[1]
        k2 = 2 * n2[0] + n2[1]
        kd = 2 * (1 - x) + (1 - y)

        def slab(a, k, h):
            return _slab(outs[a], cols[a], widths[a], k, h)

        def copy(a, j, src, dst, dev):
            return _remote(src, dst, ssem.at[a, j], rsem.at[a, j], dev)

        sends = []
        for a in range(n):
            for j, nb in ((0, n1), (1, n2)):
                own = slab(a, me, c)
                cp = copy(a, j, own, own, nb + (c,))
                cp.start()
                sends.append(cp)
        arrivals = ((0, k1, n1, 3), (1, k2, n2, 4), (2, kd, n2, 5))
        for j, k, nb, fwd in arrivals:
            for a in range(n):
                got = slab(a, k, c)
                copy(a, j, got, got, nb + (c,)).wait_recv()
                if j == 0:
                    cp = copy(a, 2, got, got, n2 + (c,))
                    cp.start()
                    sends.append(cp)
                cp = copy(a, fwd, got, got, sib)
                cp.start()
                sends.append(cp)
        for fwd, k in ((3, k2), (4, k1), (5, kd)):
            for a in range(n):
                got = slab(a, k, 1 - c)
                copy(a, fwd, got, got, sib).wait_recv()
        for cp in sends:
            cp.wait_send()

    return pl.pallas_call(
        body, name=name, in_specs=[ANY] * n, out_specs=[ANY] * n,
        out_shape=[jax.ShapeDtypeStruct(b.shape, b.dtype) for b in bufs],
        input_output_aliases={a: a for a in range(n)},
        scratch_shapes=[pltpu.SemaphoreType.DMA((n, 6)), pltpu.SemaphoreType.DMA((n, 6))],
    )(*bufs)


def _send_other_half(grads, cols, *, name):
    n = len(grads)

    def shard_shape(g, col):
        if col:
            return (g.shape[1], g.shape[2] // N_CHIPS)
        return g.shape[2:]

    shapes = [shard_shape(g, col) for g, col in zip(grads, cols)]

    def body(*refs):
        ins, outs = refs[:n], refs[n:2 * n]
        ssem, rsem = refs[2 * n:]
        x, y, c = _mesh_pos()
        sib = (x, y, 1 - c)
        sends = []
        for a in range(n):
            for k in range(N_CHIPS):
                src = _slab(ins[a], cols[a], shapes[a][1], k, 1 - c)
                cp = _remote(src, outs[a].at[k], ssem.at[a, k], rsem.at[a, k], sib)
                cp.start()
                sends.append(cp)
        for cp in sends:
            cp.wait()

    return pl.pallas_call(
        body, name=name, in_specs=[ANY] * n, out_specs=[ANY] * n,
        out_shape=[jax.ShapeDtypeStruct((N_CHIPS,) + shp, g.dtype) for g, shp in zip(grads, shapes)],
        scratch_shapes=[pltpu.SemaphoreType.DMA((n, N_CHIPS)), pltpu.SemaphoreType.DMA((n, N_CHIPS))],
    )(*grads)


def _send_first(sums, *, name):
    n = len(sums)

    def body(*refs):
        ins, outs = refs[:n], refs[n:2 * n]
        ssem, rsem = refs[2 * n:]
        x, y, c = _mesh_pos()
        nb = (_flip(x, c), _flip(y, 1 - c), c)
        sends = []
        for a in range(n):
            for t in range(2):
                k = 2 * (c * (1 - x) + (1 - c) * t) + (c * t + (1 - c) * (1 - y))
                cp = _remote(ins[a].at[k], outs[a].at[t], ssem.at[a, t], rsem.at[a, t], nb)
                cp.start()
                sends.append(cp)
        for cp in sends:
            cp.wait()

    return pl.pallas_call(
        body, name=name, in_specs=[ANY] * n, out_specs=[ANY] * n,
        out_shape=[jax.ShapeDtypeStruct((2,) + h.shape[1:], h.dtype) for h in sums],
        scratch_shapes=[pltpu.SemaphoreType.DMA((n, 2)), pltpu.SemaphoreType.DMA((n, 2))],
    )(*sums)


def _send_second(sums, *, name):
    n = len(sums)

    def body(*refs):
        ins, outs = refs[:n], refs[n:2 * n]
        ssem, rsem = refs[2 * n:]
        x, y, c = _mesh_pos()
        nb = (_flip(x, 1 - c), _flip(y, c), c)
        other = 1 - (c * y + (1 - c) * x)
        sends = []
        for a in range(n):
            cp = _remote(ins[a].at[other], outs[a], ssem.at[a], rsem.at[a], nb)
            cp.start()
            sends.append(cp)
        for cp in sends:
            cp.wait()

    return pl.pallas_call(
        body, name=name, in_specs=[ANY] * n, out_specs=[ANY] * n,
        out_shape=[jax.ShapeDtypeStruct(s.shape[1:], s.dtype) for s in sums],
        scratch_shapes=[pltpu.SemaphoreType.DMA((n,)), pltpu.SemaphoreType.DMA((n,))],
    )(*sums)


def _swap_halves(bufs, *, name):
    n = len(bufs)

    def body(*refs):
        outs = refs[n:2 * n]
        ssem, rsem = refs[2 * n:]
        x, y, c = _mesh_pos()
        sib = (x, y, 1 - c)
        cps = []
        for a in range(n):
            cp = _remote(outs[a].at[c], outs[a].at[c], ssem.at[a], rsem.at[a], sib)
            cp.start()
            cps.append(cp)
        for a, cp in enumerate(cps):
            cp.wait_send()
            theirs = outs[a].at[1 - c]
            _remote(theirs, theirs, ssem.at[a], rsem.at[a], sib).wait_recv()

    return pl.pallas_call(
        body, name=name, in_specs=[ANY] * n, out_specs=[ANY] * n,
        out_shape=[jax.ShapeDtypeStruct(b.shape, b.dtype) for b in bufs],
        input_output_aliases={a: a for a in range(n)},
        scratch_shapes=[pltpu.SemaphoreType.DMA((n,)), pltpu.SemaphoreType.DMA((n,))],
    )(*bufs)


def _all_reduce_small(buf, *, name):
    r = buf.shape[0]
    rh = r // 2

    def body(in_ref, out_ref, x1_ref, x2_ref, ssem, rsem):
        x, y, c = _mesh_pos()
        me = 2 * x + y
        sib = (x, y, 1 - c)
        chips = _other_chips(x, y)
        cp = _remote(in_ref, x1_ref, ssem.at[0], rsem.at[0], sib)
        cp.start()
        cp.wait()
        off = pl.multiple_of(c * rh, SUBLANES)
        x2_ref[me] = in_ref[pl.ds(off, rh), :] + x1_ref[pl.ds(off, rh), :]
        sends = []
        for j, (cx, cy) in enumerate(chips):
            s = _remote(x2_ref.at[me], x2_ref.at[me], ssem.at[1 + j], rsem.at[1 + j], (cx, cy, c))
            s.start()
            sends.append(s)
        for j, (cx, cy) in enumerate(chips):
            slot = x2_ref.at[2 * cx + cy]
            _remote(slot, slot, ssem.at[1 + j], rsem.at[1 + j], (cx, cy, c)).wait_recv()
        out_ref[pl.ds(off, rh), :] = ((x2_ref[0] + x2_ref[1]) + x2_ref[2]) + x2_ref[3]
        for s in sends:
            s.wait_send()
        mine = out_ref.at[pl.ds(off, rh), :]
        s3 = _remote(mine, mine, ssem.at[4], rsem.at[4], sib)
        s3.start()
        off2 = pl.multiple_of((1 - c) * rh, SUBLANES)
        theirs = out_ref.at[pl.ds(off2, rh), :]
        _remote(theirs, theirs, ssem.at[4], rsem.at[4], sib).wait_recv()
        s3.wait_send()

    vm = pl.BlockSpec(memory_space=pltpu.VMEM)
    return pl.pallas_call(
        body, name=name, in_specs=[vm], out_specs=vm,
        out_shape=jax.ShapeDtypeStruct((r, LANES), F32),
        scratch_shapes=[pltpu.VMEM((r, LANES), F32), pltpu.VMEM((N_CHIPS, rh, LANES), F32),
                        pltpu.SemaphoreType.DMA((5,)), pltpu.SemaphoreType.DMA((5,))],
    )(buf)


PACK_ALIGN = 2 * SUBLANES * LANES


def _pack(arrays, rows_multiple=2 * SUBLANES):
    parts, offs, off = [], [], 0
    for a in arrays:
        flat = a.reshape(-1).astype(F32)
        padded = -(-flat.shape[0] // PACK_ALIGN) * PACK_ALIGN
        parts.append(jnp.pad(flat, (0, padded - flat.shape[0])))
        offs.append(off)
        off += padded
    buf = jnp.concatenate(parts).reshape(-1, LANES)
    return buf, offs


def _unpack(buf, offs, shapes):
    flat = buf.reshape(-1)
    out = []
    for off, shp in zip(offs, shapes):
        size = 1
        for d in shp:
            size *= d
        out.append(flat[off:off + size].reshape(shp))
    return out


def _cols_from_shards(g4):
    _, k, ns = g4.shape
    return jnp.transpose(g4, (1, 0, 2)).reshape(k, N_CHIPS * ns)


def _cols_to_shards(w):
    k, n = w.shape
    return jnp.transpose(w.reshape(k, N_CHIPS, n // N_CHIPS), (1, 0, 2))


def _block_cols(w, parts, blocks):
    lead = w.shape[:-1]
    width = w.shape[-1] // (parts * blocks)
    w = w.reshape(lead + (parts, blocks, width))
    w = jnp.swapaxes(w, -3, -2)
    return w.reshape(lead + (parts * blocks * width,))


def _unblock_cols(w, parts, blocks):
    lead = w.shape[:-1]
    width = w.shape[-1] // (parts * blocks)
    w = w.reshape(lead + (blocks, parts, width))
    w = jnp.swapaxes(w, -3, -2)
    return w.reshape(lead + (parts * blocks * width,))


def _pair_blockdiag(w8):
    w = w8.reshape(4, 2, 64, 64)
    z = jnp.zeros((4, 64, 64), w8.dtype)
    top = jnp.concatenate([w[:, 0], z], axis=2)
    bot = jnp.concatenate([z, w[:, 1]], axis=2)
    return jnp.concatenate([top, bot], axis=1)


def _pair_diag_blocks(w4):
    a = w4[:, :64, :64]
    b = w4[:, 64:, 64:]
    return jnp.stack([a, b], axis=1).reshape(8, 64, 64)


def _local_step(x, target, wts):
    s = x.shape[0]
    g = {}

    win0 = wts["w_in0"]
    wout0 = wts["w_out0"]
    win1 = wts["w_in1"]
    wout1 = wts["w_out1"]
    wup = wts["w_up"]
    wdown = wts["w_down"]
    w4, b4, w3, b3 = wts["w4"], wts["b4"], wts["w3"], wts["b3"]
    wa, wx = wts["wa"], wts["wx"]
    wat, wxt = jnp.swapaxes(wa, 1, 2), jnp.swapaxes(wx, 1, 2)
    ba, bx, lam = wts["ba"], wts["bx"], wts["lam"]
    fcw, fcb = wts["ffn_cw"], wts["ffn_cb"]
    sgu_w, sgu_wt = wts["sgu_w"], wts["sgu_wt"]
    sgu_bias, sgu_gn = wts["sgu_bias"], wts["sgu_gn"]
    bf = wts["bf"]

    lane = jnp.arange(LANES)
    seg = jnp.where((lane[:, None] // 64) == (lane[None, :] // 64), 1.0 / 64.0, 0.0).astype(BF16)
    sel = jnp.stack([jnp.broadcast_to((lane[:, None] < 64), (LANES, LANES)),
                     jnp.broadcast_to((lane[:, None] >= 64), (LANES, LANES))]).astype(BF16)
    tril = (lane[:, None] >= lane[None, :]).astype(F32)

    n0 = _norm_fwd(x, wts["g_mix0"], name="norm_mix0")
    p0 = _mm([n0], win0, nb=640, name="mm_in0")
    ya, yb, hl = _even_core_fwd(p0, w4, b4, wa, ba, wx, bx, lam, w3, b3, name="even_fwd")
    h1 = _mm([ya, yb], wout0, res=x, name="mm_out0")

    def ffn_fwd(h, layer):
        n = _norm_fwd(h, wts["g_ffn"][layer], name=f"norm_ffn{layer}")
        up = _mm([n], wup[layer], out_dtype=BF16, nb=1408, name=f"mm_up{layer}")
        act = _ffn_core_fwd(up, fcw[layer], fcb[layer], name=f"ffn_fwd{layer}")
        hn = _mm([act], wdown[layer], res=h, name=f"mm_down{layer}")
        return n, up, act, hn

    n1, up0, act0, h2 = ffn_fwd(h1, 0)

    n2 = _norm_fwd(h2, wts["g_mix1"], name="norm_mix1")
    p1 = _mm([n2], win1, nb=896, name="mm_in1")
    yc = _sgu_fwd(p1, sgu_gn, sgu_w, sgu_bias, seg, name="sgu_fwd")
    cum = _fcum_fwd(p1, bf, name="fcum_fwd")
    c8 = cum[:, :8]
    cq = jnp.broadcast_to(c8[:, :, None], (s, 8, LANES)).reshape(s, 8 * LANES)
    ck = jnp.transpose(c8).reshape(8, 1, s)
    yd, lse = _fox_fwd(p1, cq, ck, name="fox_fwd")
    h3 = _mm([yc, yd], wout1, res=h2, name="mm_out1")

    n3, up1, act1, h4 = ffn_fwd(h3, 1)
    dh4, loss, g["final_norm"] = _final(h4, wts["g_final"], target, name="final")

    def ffn_bwd(dh, h, n, up, act, layer):
        dact = _mm([dh], wdown[layer], trans_w=True, out_dtype=BF16, nb=1408, name=f"mm_dact{layer}")
        dwd = _mm_tn([act], [dh], nb=512, name=f"mm_dwdown{layer}")
        dgate, dval, dcwg, dcwv, dcbg, dcbv = _ffn_core_bwd(dact, up, fcw[layer], fcb[layer], name=f"ffn_bwd{layer}")
        dn = _mm([dgate, dval], wup[layer], trans_w=True, nb=512, name=f"mm_dn_ffn{layer}")
        dwu = _mm_tn([n], [dgate, dval], nb=1408, name=f"mm_dwup{layer}")
        dhn, dg = _norm_bwd(dn, h, wts["g_ffn"][layer], dh, name=f"norm_bwd_ffn{layer}")
        dcw = jnp.concatenate([dcwg, dcwv], axis=1)
        dcb = jnp.concatenate([dcbg, dcbv], axis=1)
        return dhn, dwd, dwu, dcw, dcb, dg

    dh3, g["w_down1"], g["w_up1"], g["ffn_cw1"], g["ffn_cb1"], g["g_ffn1"] = ffn_bwd(dh4, h3, n3, up1, act1, 1)

    dy1 = _mm([dh3], wout1, trans_w=True, name="mm_dy1")
    g["w_out1"] = _mm_tn([yc, yd], [dh3], nb=512, name="mm_dwout1")
    dzu, dzg, g["sgu_w"], g["sgu_bias"], g["sgu_gn"] = _sgu_bwd(
        p1, dy1, sgu_gn, sgu_w, sgu_wt, sgu_bias, seg, tril, name="sgu_bwd")
    delta = _fox_delta(dy1, yd, sel, name="fox_delta")
    dq, dk, dv, dck, dcq = _fox_bwd(p1, dy1, lse, delta, cq, ck, name="fox_bwd")
    dcs = jnp.pad(jnp.transpose(dck.reshape(8, s)) + dcq.reshape(s, 8, LANES)[:, :, 0], ((0, 0), (0, LANES - 8)))
    df, g["bf"] = _fcum_bwd(dcs, p1, bf, name="fcum_bwd")
    dp1 = jnp.concatenate([dzu, dzg, dq, dk, dv, df], axis=1)
    dn2 = _mm([dp1], win1, trans_w=True, name="mm_dn_mix1")
    g["w_in1"] = _mm_tn([n2], [dp1], nb=896, name="mm_dwin1")
    dh2, g["g_mix1"] = _norm_bwd(dn2, h2, wts["g_mix1"], dh3, name="norm_bwd_mix1")

    dh1, g["w_down0"], g["w_up0"], g["ffn_cw0"], g["ffn_cb0"], g["g_ffn0"] = ffn_bwd(dh2, h1, n1, up0, act0, 0)

    dy0 = _mm([dh1], wout0, trans_w=True, name="mm_dy0")
    g["w_out0"] = _mm_tn([ya, yb], [dh1], nb=512, name="mm_dwout0")
    (dp0, g["w4"], g["b4"], g["wa"], g["ba"], g["wx"], g["bx"], g["lam"], g["w3"], g["b3"]) = _even_core_bwd(
        dy0, p0, hl, w4, b4, wa, wat, ba, wx, wxt, bx, lam, w3, b3, name="even_bwd")
    dn0 = _mm([dp0], win0, trans_w=True, name="mm_dn_mix0")
    g["w_in0"] = _mm_tn([n0], [dp0], nb=640, name="mm_dwin0")
    grad_x, g["g_mix0"] = _norm_bwd(dn0, x, wts["g_mix0"], dh1, name="norm_bwd_mix0")
    return loss, grad_x, g


def _prepare_weights(nat):
    lane = jnp.arange(LANES)
    tril = (lane[:, None] >= lane[None, :]).astype(F32)
    sgu_tril = nat["sgu_w"][0] * tril
    w_in1 = nat["mix1_w_in"]
    nblk = D_FF // FFN_CB
    return {
        "w_in0": _block_cols(nat["mix0_w_in"], 5, 4),
        "w_out0": nat["mix0_w_out"],
        "w_in1": jnp.pad(w_in1, ((0, 0), (0, 21 * LANES - w_in1.shape[1]))),
        "w_out1": nat["mix1_w_out"],
        "w_up": [nat["ffn_up"][l] for l in range(2)],
        "w_down": [nat["ffn_down"][l] for l in range(2)],
        "w4": nat["lru_conv_w"], "b4": nat["lru_conv_b"], "w3": nat["sconv_w"], "b3": nat["sconv_b"],
        "wa": _pair_blockdiag(nat["lru_wa"][0]).astype(BF16), "wx": _pair_blockdiag(nat["lru_wx"][0]).astype(BF16),
        "ba": nat["lru_ba"], "bx": nat["lru_bx"], "lam": nat["lru_lambda"],
        "ffn_cw": [nat["ffn_conv_w"][l] for l in range(2)],
        "ffn_cb": [nat["ffn_conv_b"][l:l + 1] for l in range(2)],
        "sgu_w": sgu_tril.astype(BF16), "sgu_wt": jnp.swapaxes(sgu_tril, 1, 2).astype(BF16),
        "sgu_bias": jnp.repeat(jnp.transpose(nat["sgu_b"][0]), 64, axis=1), "sgu_gn": nat["sgu_norm"],
        "bf": jnp.pad(nat["fox_bf"], ((0, 0), (0, LANES - 8))),
        "g_mix0": nat["mix0_norm"], "g_mix1": nat["mix1_norm"],
        "g_ffn": [nat["ffn_norm"][0:1], nat["ffn_norm"][1:2]], "g_final": nat["final_norm"].reshape(1, D_MODEL),
    }


def _natural_grads(g):
    nblk = D_FF // FFN_CB
    small = {
        "mix0_norm": g["g_mix0"], "lru_conv_b": g["b4"],
        "lru_wa": _pair_diag_blocks(g["wa"])[None], "lru_ba": g["ba"],
        "lru_wx": _pair_diag_blocks(g["wx"])[None], "lru_bx": g["bx"],
        "lru_lambda": g["lam"], "sconv_b": g["b3"],
        "sgu_w": g["sgu_w"][None],
        "sgu_b": jnp.transpose(g["sgu_bias"].reshape(CHUNK, 8, 64).sum(axis=2))[None],
        "fox_bf": g["bf"][:, :8],
        "ffn_norm": jnp.concatenate([g["g_ffn0"], g["g_ffn1"]], axis=0),
        "ffn_conv_b": jnp.concatenate([g["ffn_cb0"], g["ffn_cb1"]], axis=0),
        "final_norm": g["final_norm"].reshape(D_MODEL),
        "lru_conv_w": g["w4"][None], "sconv_w": g["w3"][None],
        "ffn_conv_w": jnp.stack([g["ffn_cw0"], g["ffn_cw1"]]),
        "mix1_norm": g["g_mix1"], "sgu_norm": g["sgu_gn"],
    }
    big = {
        "mix0_w_in": _unblock_cols(g["w_in0"], 5, 4), "mix0_w_out": g["w_out0"],
        "mix1_w_in": g["w_in1"][:, :2568], "mix1_w_out": g["w_out1"],
        "ffn_up0": g["w_up0"], "ffn_up1": g["w_up1"],
        "ffn_down0": g["w_down0"], "ffn_down1": g["w_down1"],
    }
    return small, big


COL_SHARDED = ("mix0_w_in", "mix1_w_in", "ffn_up0", "ffn_up1")
COL_ALIGNED = ("mix0_w_in", "ffn_up0", "ffn_up1")
SMALL_SHARDED = ("lru_conv_w", "sconv_w", "ffn_conv_w", "mix1_norm", "sgu_norm")
SMALL_REPLICATED = ("mix0_norm", "lru_conv_b", "lru_wa", "lru_ba", "lru_wx", "lru_bx", "lru_lambda", "sconv_b",
                    "sgu_w", "sgu_b", "fox_bf", "ffn_norm", "ffn_conv_b", "final_norm")
BIG = ("mix0_w_in", "mix0_w_out", "mix1_w_in", "mix1_w_out", "ffn_up0", "ffn_up1", "ffn_down0", "ffn_down1")
WEIGHT_ORDER = ("mix0_norm", "mix0_w_in", "lru_conv_w", "lru_conv_b", "lru_wa", "lru_ba", "lru_wx", "lru_bx",
                "lru_lambda", "sconv_w", "sconv_b", "mix0_w_out", "mix1_norm", "mix1_w_in", "sgu_norm", "sgu_w",
                "sgu_b", "fox_bf", "mix1_w_out", "ffn_norm", "ffn_up", "ffn_conv_w", "ffn_conv_b", "ffn_down",
                "final_norm")


def _halves(a):
    r = a.shape[0]
    return a.reshape((2, r // 2) + a.shape[1:])


def _train_step(x, target, w, m, v):
    x2 = x[0]
    t2 = target[0]
    chip = 2 * lax.axis_index("x") + lax.axis_index("y")
    core = lax.axis_index("c")

    big_shards = {
        "mix0_w_in": w["mix0_w_in"][0], "mix0_w_out": w["mix0_w_out"][0],
        "mix1_w_in": w["mix1_w_in"][0], "mix1_w_out": w["mix1_w_out"][0],
        "ffn_up0": w["ffn_up"][0], "ffn_up1": w["ffn_up"][1],
        "ffn_down0": w["ffn_down"][0], "ffn_down1": w["ffn_down"][1],
    }
    small_shards = [w[k] for k in SMALL_SHARDED]
    small_buf, small_offs = _pack(small_shards)
    cols = [k in COL_ALIGNED for k in BIG]
    chip_arr = chip.reshape(1).astype(jnp.int32)
    placed = [_place(big_shards[k], col, chip_arr, BF16, name=f"place_{k}") for k, col in zip(BIG, cols)]
    placed.append(_place(small_buf, False, chip_arr, F32, name="place_small"))
    gathered = _all_gather(placed, cols + [False], name="gather_weights")
    full = {}
    for k, arr in zip(BIG, gathered[:-1]):
        if k in COL_ALIGNED:
            full[k] = arr.reshape(arr.shape[0] * arr.shape[1], arr.shape[2])
        elif k in COL_SHARDED:
            full[k] = _cols_from_shards(arr.reshape((N_CHIPS, arr.shape[1] * arr.shape[2], arr.shape[3])))
        else:
            full[k] = arr.reshape(-1, arr.shape[3])
    small_all = gathered[-1].reshape(N_CHIPS, -1, LANES)
    per_chip = [_unpack(small_all[k], small_offs, [a.shape for a in small_shards]) for k in range(N_CHIPS)]
    lru_conv_w = jnp.concatenate([per_chip[k][0] for k in range(N_CHIPS)], axis=-1)[0]
    sconv_w = jnp.concatenate([per_chip[k][1] for k in range(N_CHIPS)], axis=-1)[0]
    ffn_conv_w = jnp.concatenate([per_chip[k][2] for k in range(N_CHIPS)], axis=-1)
    mix1_norm = jnp.concatenate([per_chip[k][3] for k in range(N_CHIPS)], axis=-1)
    sgu_norm = jnp.concatenate([per_chip[k][4] for k in range(N_CHIPS)], axis=-1)

    nat = {
        "mix0_w_in": full["mix0_w_in"], "mix0_w_out": full["mix0_w_out"],
        "mix1_w_in": full["mix1_w_in"], "mix1_w_out": full["mix1_w_out"],
        "ffn_up": [full["ffn_up0"], full["ffn_up1"]], "ffn_down": [full["ffn_down0"], full["ffn_down1"]],
        "lru_conv_w": lru_conv_w, "sconv_w": sconv_w, "ffn_conv_w": ffn_conv_w, "mix1_norm": mix1_norm,
        "sgu_norm": sgu_norm,
    }
    for k in SMALL_REPLICATED:
        nat[k] = w[k]
    wts = _prepare_weights(nat)
    loss, grad_x, g = _local_step(x2, t2, wts)

    grads_small, grads_big = _natural_grads(g)

    small_names = SMALL_REPLICATED + SMALL_SHARDED
    small_list = [grads_small[k] for k in small_names] + [loss[:, :1]]
    sbuf, soffs = _pack(small_list)
    sred = _all_reduce_small(sbuf, name="reduce_small")
    small_red = _unpack(sred, soffs, [a.shape for a in small_list])
    loss_total = small_red[-1][0, 0]
    gsum = dict(zip(small_names, small_red[:-1]))
    for k in SMALL_SHARDED:
        width = w[k].shape[-1]
        gsum[k] = lax.dynamic_slice_in_dim(gsum[k], chip * width, width, axis=gsum[k].ndim - 1)

    def grad_view(k):
        a = grads_big[k]
        if k in COL_ALIGNED:
            return a.reshape(2, a.shape[0] // 2, a.shape[1])
        if k in COL_SHARDED:
            a = _cols_to_shards(a)
            return a.reshape(N_CHIPS, 2, a.shape[1] // 2, a.shape[2])
        rows = a.shape[0] // (2 * N_CHIPS)
        return a.reshape(N_CHIPS, 2, rows, a.shape[1])

    gviews = [grad_view(k) for k in BIG]
    xi, yi = lax.axis_index("x"), lax.axis_index("y")
    core_arr = core.reshape(1).astype(jnp.int32)
    keep_arr = jnp.stack([core * (2 * xi + t) + (1 - core) * (2 * t + yi) for t in range(2)]).astype(jnp.int32)
    mine_arr = jnp.stack([core * yi + (1 - core) * xi, core]).astype(jnp.int32)
    from_sib = _send_other_half(gviews, cols, name="rs_pair")
    pair = [_pair_sum(a, col, b, core_arr, name=f"rs_pair_sum_{k}")
            for k, a, col, b in zip(BIG, gviews, cols, from_sib)]
    first_in = _send_first([p16 for _, p16 in pair], name="rs_first")
    first = [_first_sum(p32, r1, keep_arr, name=f"rs_first_sum_{k}") for k, (p32, _), r1 in zip(BIG, pair, first_in)]
    second_in = _send_second([s16 for _, s16 in first], name="rs_second")
    mine = [_second_sum(s32, r2, mine_arr, name=f"rs_second_sum_{k}")
            for k, (s32, _), r2 in zip(BIG, first, second_in)]
    both = _swap_halves(mine, name="rs_swap")
    gbig = {k: a.reshape((a.shape[0] * a.shape[1],) + a.shape[2:]) for k, a in zip(BIG, both)}

    out_g, out_d, out_m, out_v = {}, {}, {}, {}
    small_w = [w[k] for k in small_names]
    pg, offs = _pack([gsum[k] for k in small_names])
    pw, _ = _pack(small_w)
    pm, _ = _pack([m[k] for k in small_names])
    pv, _ = _pack([v[k] for k in small_names])
    sd, sm, sv = _adamw(pw, pg, pm, pv, name="adamw_small")
    shapes = [a.shape for a in small_w]
    for k, dd, mm, vv in zip(small_names, _unpack(sd, offs, shapes), _unpack(sm, offs, shapes),
                             _unpack(sv, offs, shapes)):
        out_g[k], out_d[k], out_m[k], out_v[k] = gsum[k].reshape(w[k].shape), dd, mm, vv

    def big_adam(name, wk, mk, vk, gk):
        shp = wk.shape
        w2, m2, v2 = (a.reshape(gk.shape) for a in (wk, mk, vk))
        d, nm, nv = _adamw(w2, gk, m2, v2, name=f"adamw_{name}")
        return gk.reshape(shp), d.reshape(shp), nm.reshape(shp), nv.reshape(shp)

    for k in ("mix0_w_in", "mix0_w_out", "mix1_w_in", "mix1_w_out"):
        out_g[k], out_d[k], out_m[k], out_v[k] = big_adam(k, w[k][0], m[k][0], v[k][0], gbig[k])
        out_g[k], out_d[k], out_m[k], out_v[k] = (a[None] for a in (out_g[k], out_d[k], out_m[k], out_v[k]))
    for k in ("ffn_up", "ffn_down"):
        res = [big_adam(f"{k}{l}", w[k][l], m[k][l], v[k][l], gbig[f"{k}{l}"]) for l in range(2)]
        out_g[k], out_d[k], out_m[k], out_v[k] = (jnp.stack([res[0][i], res[1][i]]) for i in range(4))

    outs = [loss_total, grad_x[None]]
    for d in (out_g, out_d, out_m, out_v):
        outs.extend(d[k] for k in WEIGHT_ORDER)
    return tuple(outs)


def kernel(x, mix0_norm, mix0_w_in, lru_conv_w, lru_conv_b, lru_wa, lru_ba, lru_wx, lru_bx, lru_lambda, sconv_w, sconv_b, mix0_w_out, mix1_norm, mix1_w_in, sgu_norm, sgu_w, sgu_b, fox_bf, mix1_w_out, ffn_norm, ffn_up, ffn_conv_w, ffn_conv_b, ffn_down, final_norm, loss_target, m_mix0_norm, m_mix0_w_in, m_lru_conv_w, m_lru_conv_b, m_lru_wa, m_lru_ba, m_lru_wx, m_lru_bx, m_lru_lambda, m_sconv_w, m_sconv_b, m_mix0_w_out, m_mix1_norm, m_mix1_w_in, m_sgu_norm, m_sgu_w, m_sgu_b, m_fox_bf, m_mix1_w_out, m_ffn_norm, m_ffn_up, m_ffn_conv_w, m_ffn_conv_b, m_ffn_down, m_final_norm, v_mix0_norm, v_mix0_w_in, v_lru_conv_w, v_lru_conv_b, v_lru_wa, v_lru_ba, v_lru_wx, v_lru_bx, v_lru_lambda, v_sconv_w, v_sconv_b, v_mix0_w_out, v_mix1_norm, v_mix1_w_in, v_sgu_norm, v_sgu_w, v_sgu_b, v_fox_bf, v_mix1_w_out, v_ffn_norm, v_ffn_up, v_ffn_conv_w, v_ffn_conv_b, v_ffn_down, v_final_norm):
    w = dict(zip(WEIGHT_ORDER, (mix0_norm, mix0_w_in, lru_conv_w, lru_conv_b, lru_wa, lru_ba, lru_wx, lru_bx, lru_lambda, sconv_w, sconv_b, mix0_w_out, mix1_norm, mix1_w_in, sgu_norm, sgu_w, sgu_b, fox_bf, mix1_w_out, ffn_norm, ffn_up, ffn_conv_w, ffn_conv_b, ffn_down, final_norm)))
    m = dict(zip(WEIGHT_ORDER, (m_mix0_norm, m_mix0_w_in, m_lru_conv_w, m_lru_conv_b, m_lru_wa, m_lru_ba, m_lru_wx, m_lru_bx, m_lru_lambda, m_sconv_w, m_sconv_b, m_mix0_w_out, m_mix1_norm, m_mix1_w_in, m_sgu_norm, m_sgu_w, m_sgu_b, m_fox_bf, m_mix1_w_out, m_ffn_norm, m_ffn_up, m_ffn_conv_w, m_ffn_conv_b, m_ffn_down, m_final_norm)))
    v = dict(zip(WEIGHT_ORDER, (v_mix0_norm, v_mix0_w_in, v_lru_conv_w, v_lru_conv_b, v_lru_wa, v_lru_ba, v_lru_wx, v_lru_bx, v_lru_lambda, v_sconv_w, v_sconv_b, v_mix0_w_out, v_mix1_norm, v_mix1_w_in, v_sgu_norm, v_sgu_w, v_sgu_b, v_fox_bf, v_mix1_w_out, v_ffn_norm, v_ffn_up, v_ffn_conv_w, v_ffn_conv_b, v_ffn_down, v_final_norm)))
    return _train_step(x, loss_target, w, m, v)
```

```python
import functools

import jax
import jax.numpy as jnp
from jax import lax
from jax.experimental import pallas as pl
from jax.experimental.pallas import tpu as pltpu

F32 = jnp.float32
BF16 = jnp.bfloat16
MESH = pl.DeviceIdType.MESH

D_MODEL = 1024
LANES = 128
SUBLANES = 8
N_CHIPS = 4
EPS = 1e-6
LRU_C = 8.0
D_FF = 2816
FFN_CB = 256
CHUNK = 128
NEG = -1e30

ADAM_LR = 0.001
ADAM_B1 = 0.9
ADAM_B2 = 0.999
ADAM_EPS = 1e-08
ADAM_WD = 0.01
ADAM_STEP = 10
ADAM_C1 = 1.0 - ADAM_B1 ** ADAM_STEP
ADAM_C2 = 1.0 - ADAM_B2 ** ADAM_STEP

_GELU_C = 0.7978845608028654
_GELU_A = 0.044715


def _sigmoid(x):
    return 1.0 / (1.0 + jnp.exp(-x))


def _log1p_pos(e):
    w = 1.0 + e
    return jnp.where(w == 1.0, e, jnp.log(w) * (e / (w - 1.0)))


def _softplus(x):
    return jnp.maximum(x, 0.0) + _log1p_pos(jnp.exp(-jnp.abs(x)))


def _gelu(x):
    t = jnp.tanh(_GELU_C * (x + _GELU_A * (x * x * x)))
    return 0.5 * x * (1.0 + t), t


def _gelu_grad(x, t):
    return 0.5 * (1.0 + t) + 0.5 * x * (1.0 - t * t) * (_GELU_C * (1.0 + 3.0 * _GELU_A * x * x))


def _rows(shape):
    return lax.broadcasted_iota(jnp.int32, shape, 0)


def _lanes(shape):
    return lax.broadcasted_iota(jnp.int32, shape, 1)


def _shift_down(x, halo8, j):
    if j == 0:
        return x
    r = pltpu.roll(x, j, 0)
    hr = pltpu.roll(halo8, j, 0)
    top = jnp.where(_rows(hr.shape) < j, hr, r[:SUBLANES])
    return jnp.concatenate([top, r[SUBLANES:]], axis=0)


def _shift_up(x, next8, j):
    if j == 0:
        return x
    n = x.shape[0]
    r = pltpu.roll(x, n - j, 0)
    nr = pltpu.roll(next8, SUBLANES - j, 0)
    bot = jnp.where(_rows(nr.shape) >= SUBLANES - j, nr, r[n - SUBLANES:])
    return jnp.concatenate([r[:n - SUBLANES], bot], axis=0)


def _scan_fwd(a, u):
    n = a.shape[0]
    row = _rows(a.shape)
    h = u
    k = 1
    while k < n:
        keep = row >= k
        h_sh = jnp.where(keep, pltpu.roll(h, k, 0), 0.0)
        a_sh = jnp.where(keep, pltpu.roll(a, k, 0), 1.0)
        h = a * h_sh + h
        a = a * a_sh
        k *= 2
    return h, a


def _scan_rev(b, d):
    n = b.shape[0]
    row = _rows(b.shape)
    g = d
    k = 1
    while k < n:
        keep = row < n - k
        g_sh = jnp.where(keep, pltpu.roll(g, n - k, 0), 0.0)
        b_sh = jnp.where(keep, pltpu.roll(b, n - k, 0), 1.0)
        g = b * g_sh + g
        b = b * b_sh
        k *= 2
    return g, b


def _cumsum_fwd(x):
    n = x.shape[0]
    row = _rows(x.shape)
    k = 1
    while k < n:
        x = x + jnp.where(row >= k, pltpu.roll(x, k, 0), 0.0)
        k *= 2
    return x


def _cumsum_rev(x):
    n = x.shape[0]
    row = _rows(x.shape)
    k = 1
    while k < n:
        x = x + jnp.where(row < n - k, pltpu.roll(x, n - k, 0), 0.0)
        k *= 2
    return x


def _dot(a, b):
    return lax.dot_general(a, b, (((1,), (0,)), ((), ())), preferred_element_type=F32)


def _dot_nt(a, b):
    return lax.dot_general(a, b, (((1,), (1,)), ((), ())), preferred_element_type=F32)


def _dot_tn(a, b):
    return lax.dot_general(a, b, (((0,), (0,)), ((), ())), preferred_element_type=F32)


def _dot_split(x, m_bf16):
    hi = x.astype(BF16)
    lo = (x - hi.astype(F32)).astype(BF16)
    return _dot(hi, m_bf16) + _dot(lo, m_bf16)


def _tile_rows(ts, s):
    return min(ts, s)


def _mm(a_list, w, *, trans_w=False, res=None, out_dtype=F32, ts=512, nb=None, name):
    s = a_list[0].shape[0]
    ks = [a.shape[1] for a in a_list]
    k = sum(ks)
    n = w.shape[0] if trans_w else w.shape[1]
    ts = _tile_rows(ts, s)
    nb = n if nb is None else nb
    na = len(a_list)
    has_res = res is not None

    def body(*refs):
        a_refs = refs[:na]
        w_ref = refs[na]
        o_ref = refs[-1]
        parts = [r[...].astype(BF16) for r in a_refs]
        a = parts[0] if na == 1 else jnp.concatenate(parts, axis=1)
        acc = _dot_nt(a, w_ref[...]) if trans_w else _dot(a, w_ref[...])
        if has_res:
            acc = acc + refs[na + 1][...]
        o_ref[...] = acc.astype(out_dtype)

    in_specs = [pl.BlockSpec((ts, kk), lambda j, i: (i, 0)) for kk in ks]
    if trans_w:
        in_specs.append(pl.BlockSpec((nb, k), lambda j, i: (j, 0)))
    else:
        in_specs.append(pl.BlockSpec((k, nb), lambda j, i: (0, j)))
    args = list(a_list) + [w]
    if has_res:
        in_specs.append(pl.BlockSpec((ts, nb), lambda j, i: (i, j)))
        args.append(res)
    return pl.pallas_call(
        body, name=name, grid=(n // nb, s // ts), in_specs=in_specs,
        out_specs=pl.BlockSpec((ts, nb), lambda j, i: (i, j)),
        out_shape=jax.ShapeDtypeStruct((s, n), out_dtype),
    )(*args)


def _mm_tn(a_list, b_list, *, ts=512, nb=None, name):
    s = b_list[0].shape[0]
    ks = [a.shape[1] for a in a_list]
    k = sum(ks)
    width = b_list[0].shape[1]
    n = width * len(b_list)
    ts = _tile_rows(ts, s)
    nb = width if nb is None else nb
    per = width // nb
    na = len(a_list)
    nparts = len(b_list)

    def body(*refs):
        a_refs = refs[:na]
        b_refs = refs[na:na + nparts]
        o_ref = refs[-1]
        j = pl.program_id(0)
        i = pl.program_id(1)
        parts = [r[...].astype(BF16) for r in a_refs]
        a = parts[0] if na == 1 else jnp.concatenate(parts, axis=1)

        def accumulate(b_ref):
            upd = _dot_tn(a, b_ref[...].astype(BF16))

            @pl.when(i == 0)
            def _():
                o_ref[...] = upd

            @pl.when(i > 0)
            def _():
                o_ref[...] += upd

        if nparts == 1:
            accumulate(b_refs[0])
        else:
            for part, b_ref in enumerate(b_refs):
                pl.when(j // per == part)(functools.partial(accumulate, b_ref))

    in_specs = [pl.BlockSpec((ts, kk), lambda j, i: (i, 0)) for kk in ks]
    for part in range(nparts):
        in_specs.append(pl.BlockSpec(
            (ts, nb), lambda j, i, part=part: (i, jnp.clip(j - part * per, 0, per - 1))))
    return pl.pallas_call(
        body, name=name, grid=(n // nb, s // ts), in_specs=in_specs,
        out_specs=pl.BlockSpec((k, nb), lambda j, i: (0, j)),
        out_shape=jax.ShapeDtypeStruct((k, n), F32),
    )(*a_list, *b_list)


def _norm_fwd(h, g, *, ts=512, name):
    s, d = h.shape
    ts = _tile_rows(ts, s)

    def body(h_ref, g_ref, n_ref):
        x = h_ref[...]
        r = lax.rsqrt(jnp.mean(x * x, axis=-1, keepdims=True) + EPS)
        n_ref[...] = ((x * r) * g_ref[...]).astype(BF16)

    return pl.pallas_call(
        body, name=name, grid=(s // ts,),
        in_specs=[pl.BlockSpec((ts, d), lambda i: (i, 0)), pl.BlockSpec((1, d), lambda i: (0, 0))],
        out_specs=pl.BlockSpec((ts, d), lambda i: (i, 0)),
        out_shape=jax.ShapeDtypeStruct((s, d), BF16),
    )(h, g)


def _norm_bwd(dn, h, g, dres, *, ts=512, name):
    s, d = h.shape
    ts = _tile_rows(ts, s)

    def body(dn_ref, h_ref, g_ref, dres_ref, dh_ref, dg_ref):
        i = pl.program_id(0)
        x = h_ref[...]
        dnv = dn_ref[...]
        r = lax.rsqrt(jnp.mean(x * x, axis=-1, keepdims=True) + EPS)
        xhat = x * r
        part = jnp.sum(dnv * xhat, axis=0, keepdims=True)

        @pl.when(i == 0)
        def _():
            dg_ref[...] = part

        @pl.when(i > 0)
        def _():
            dg_ref[...] += part

        dxh = dnv * g_ref[...]
        dh_ref[...] = dres_ref[...] + r * (dxh - xhat * jnp.mean(dxh * xhat, axis=-1, keepdims=True))

    tile = pl.BlockSpec((ts, d), lambda i: (i, 0))
    vec = pl.BlockSpec((1, d), lambda i: (0, 0))
    return pl.pallas_call(
        body, name=name, grid=(s // ts,), in_specs=[tile, tile, vec, tile],
        out_specs=(tile, vec),
        out_shape=(jax.ShapeDtypeStruct((s, d), F32), jax.ShapeDtypeStruct((1, d), F32)),
    )(dn, h, g, dres)


def _final(h, g, target, *, ts=512, name):
    s, d = h.shape
    ts = _tile_rows(ts, s)
    nt = s // ts

    def body(h_ref, g_ref, t_ref, dh_ref, loss_ref, dg_ref, acc_ref):
        i = pl.program_id(0)
        x = h_ref[...]
        r = lax.rsqrt(jnp.mean(x * x, axis=-1, keepdims=True) + EPS)
        xhat = x * r
        gv = g_ref[...]
        err = xhat * gv - t_ref[...]
        sq = jnp.sum(err * err, axis=0, keepdims=True)
        dy = err * (1.0 / d)
        part = jnp.sum(dy * xhat, axis=0, keepdims=True)

        @pl.when(i == 0)
        def _():
            acc_ref[...] = sq
            dg_ref[...] = part

        @pl.when(i > 0)
        def _():
            acc_ref[...] += sq
            dg_ref[...] += part

        dxh = dy * gv
        dh_ref[...] = r * (dxh - xhat * jnp.mean(dxh * xhat, axis=-1, keepdims=True))

        @pl.when(i == nt - 1)
        def _():
            tot = jnp.sum(acc_ref[...], axis=1, keepdims=True) * (0.5 / d)
            loss_ref[...] = jnp.broadcast_to(tot, (1, LANES))

    tile = pl.BlockSpec((ts, d), lambda i: (i, 0))
    vec = pl.BlockSpec((1, d), lambda i: (0, 0))
    return pl.pallas_call(
        body, name=name, grid=(nt,), in_specs=[tile, vec, tile],
        out_specs=(tile, pl.BlockSpec((1, LANES), lambda i: (0, 0)), vec),
        out_shape=(jax.ShapeDtypeStruct((s, d), F32), jax.ShapeDtypeStruct((1, LANES), F32),
                   jax.ShapeDtypeStruct((1, d), F32)),
        scratch_shapes=[pltpu.VMEM((1, d), F32)],
    )(h, g, target)


def _halo_map(ts, width_blocks):
    per = ts // SUBLANES

    def index(j, i):
        return (jnp.maximum(i * per - 1, 0), width_blocks(j))

    return index


def _even_gates(xc, wa, ba, wx, bx, sp):
    xb = xc.astype(BF16)
    r = _sigmoid(_dot(xb, wa) + ba)
    ig = _sigmoid(_dot(xb, wx) + bx)
    la = (-LRU_C) * r * sp
    a = jnp.exp(la)
    a2 = a * a
    m = jnp.sqrt(-jnp.tanh(la) * (1.0 + a2))
    return r, ig, la, a, a2, m


def _even_core_fwd(p, w4, b4, wa, ba, wx, bx, lam, w3, b3, *, ts=512, name):
    s = p.shape[0]
    ts = _tile_rows(ts, s)
    nt = s // ts
    nblk = 4

    def body(p_ref, ph_ref, w4_ref, b4_ref, wa_ref, ba_ref, wx_ref, bx_ref, lam_ref, w3_ref, b3_ref,
             ya_ref, yb_ref, hl_ref, hcar_ref):
        i = pl.program_id(1)
        first = (i > 0).astype(F32)
        xa = p_ref[:, 0:LANES]
        ga = p_ref[:, LANES:2 * LANES]
        cp = p_ref[:, 2 * LANES:3 * LANES]
        bp = p_ref[:, 3 * LANES:4 * LANES]
        vb = p_ref[:, 4 * LANES:5 * LANES]
        xa_h = ph_ref[:, 0:LANES] * first
        s_h = ph_ref[:, 2 * LANES:3 * LANES] * ph_ref[:, 4 * LANES:5 * LANES] * first

        xc = b4_ref[...] + w4_ref[3:4, :] * xa
        for k in range(3):
            xc = xc + w4_ref[k:k + 1, :] * _shift_down(xa, xa_h, 3 - k)
        sp = _softplus(-lam_ref[...])
        _, ig, _, a, _, m = _even_gates(xc, wa_ref[0], ba_ref[...], wx_ref[0], bx_ref[...], sp)
        u = m * (ig * xc)
        hs, acum = _scan_fwd(a, u)

        @pl.when(i == 0)
        def _():
            hcar_ref[...] = jnp.zeros_like(hcar_ref)

        hs = hs + acum * hcar_ref[0:1, :]
        hl_ref[...] = hs
        hcar_ref[0:1, :] = hl_ref[ts - 1:ts, :]
        ge, _ = _gelu(ga)
        ya_ref[...] = (hs * ge).astype(BF16)

        sv = cp * vb
        sc = b3_ref[...] + w3_ref[2:3, :] * sv
        for k in range(2):
            sc = sc + w3_ref[k:k + 1, :] * _shift_down(sv, s_h, 2 - k)
        yb_ref[...] = (bp * sc).astype(BF16)

    blk = pl.BlockSpec((ts, 5 * LANES), lambda j, i: (i, j))
    halo = pl.BlockSpec((SUBLANES, 5 * LANES), _halo_map(ts, lambda j: j))
    vec = pl.BlockSpec((1, LANES), lambda j, i: (0, j))
    out = pl.BlockSpec((ts, LANES), lambda j, i: (i, j))
    return pl.pallas_call(
        body, name=name, grid=(nblk, nt),
        in_specs=[blk, halo,
                  pl.BlockSpec((4, LANES), lambda j, i: (0, j)), vec,
                  pl.BlockSpec((1, LANES, LANES), lambda j, i: (j, 0, 0)), vec,
                  pl.BlockSpec((1, LANES, LANES), lambda j, i: (j, 0, 0)), vec, vec,
                  pl.BlockSpec((3, LANES), lambda j, i: (0, j)), vec],
        out_specs=(out, out, out),
        out_shape=(jax.ShapeDtypeStruct((s, 4 * LANES), BF16), jax.ShapeDtypeStruct((s, 4 * LANES), BF16),
                   jax.ShapeDtypeStruct((s, 4 * LANES), F32)),
        scratch_shapes=[pltpu.VMEM((SUBLANES, LANES), F32)],
    )(p, p, w4, b4, wa, ba, wx, bx, lam, w3, b3)


def _even_core_bwd(dy, p, hl, w4, b4, wa, wat, ba, wx, wxt, bx, lam, w3, b3, *, ts=256, name):
    s = p.shape[0]
    ts = _tile_rows(ts, s)
    nt = s // ts
    nblk = 4
    per = ts // SUBLANES

    def body(dya_ref, dyb_ref, p_ref, ph_ref, hl_ref, hh_ref,
             w4_ref, b4_ref, wa_ref, wat_ref, ba_ref, wx_ref, wxt_ref, bx_ref, lam_ref, w3_ref, b3_ref,
             dp_ref, dw4_ref, db4_ref, dwa_ref, dba_ref, dwx_ref, dbx_ref, dlam_ref, dw3_ref, db3_ref,
             dxc_nx, dsc_nx, cg_ref):
        i = pl.program_id(1)
        ti = nt - 1 - i
        first = (ti > 0).astype(F32)
        xa = p_ref[:, 0:LANES]
        ga = p_ref[:, LANES:2 * LANES]
        cp = p_ref[:, 2 * LANES:3 * LANES]
        bp = p_ref[:, 3 * LANES:4 * LANES]
        vb = p_ref[:, 4 * LANES:5 * LANES]
        xa_h = ph_ref[:, 0:LANES] * first
        s_h = ph_ref[:, 2 * LANES:3 * LANES] * ph_ref[:, 4 * LANES:5 * LANES] * first
        h_h = hh_ref[...] * first

        @pl.when(i == 0)
        def _():
            dxc_nx[...] = jnp.zeros_like(dxc_nx)
            dsc_nx[...] = jnp.zeros_like(dsc_nx)
            cg_ref[...] = jnp.zeros_like(cg_ref)
            for ref in (dw4_ref, db4_ref, dwa_ref, dba_ref, dwx_ref, dbx_ref, dlam_ref, dw3_ref, db3_ref):
                ref[...] = jnp.zeros_like(ref)

        xa_sh = [_shift_down(xa, xa_h, 3 - k) for k in range(3)] + [xa]
        xc = b4_ref[...]
        for k in range(4):
            xc = xc + w4_ref[k:k + 1, :] * xa_sh[k]
        lamv = lam_ref[...]
        sp = _softplus(-lamv)
        r, ig, _, a, a2, m = _even_gates(xc, wa_ref[0], ba_ref[...], wx_ref[0], bx_ref[...], sp)
        sv = cp * vb
        sv_sh = [_shift_down(sv, s_h, 2 - k) for k in range(2)] + [sv]
        sc = b3_ref[...]
        for k in range(3):
            sc = sc + w3_ref[k:k + 1, :] * sv_sh[k]
        hs = hl_ref[...]
        h_prev = _shift_down(hs, h_h, 1)

        dya = dya_ref[...]
        dyb = dyb_ref[...]
        ge, gt = _gelu(ga)
        dga = dya * hs * _gelu_grad(ga, gt)
        dh = dya * ge

        ones8 = jnp.ones((SUBLANES, LANES), F32)
        b = _shift_up(a, ones8, 1)
        g, bcum = _scan_rev(b, dh)
        g = g + bcum * cg_ref[0:1, :]
        ag = a * g
        cg_ref[...] = ag[:SUBLANES]

        da = g * h_prev
        xi = ig * xc
        dm = g * xi
        dig = g * m * xc
        dxc = g * m * ig
        dla = da * a - dm * (a2 / m)
        dr = dla * ((-LRU_C) * sp)
        dlam_ref[...] += jnp.sum(dla * r, axis=0, keepdims=True) * (LRU_C * _sigmoid(-lamv))
        dra = dr * r * (1.0 - r)
        dia = dig * ig * (1.0 - ig)
        drab = dra.astype(BF16)
        diab = dia.astype(BF16)
        xcb = xc.astype(BF16)
        dxc = dxc + _dot(drab, wat_ref[0]) + _dot(diab, wxt_ref[0])
        dwa_ref[0] += _dot_tn(xcb, drab)
        dwx_ref[0] += _dot_tn(xcb, diab)
        dba_ref[...] += jnp.sum(dra, axis=0, keepdims=True)
        dbx_ref[...] += jnp.sum(dia, axis=0, keepdims=True)

        nx = dxc_nx[...]
        dxa = w4_ref[3:4, :] * dxc
        for k in range(3):
            dxa = dxa + w4_ref[k:k + 1, :] * _shift_up(dxc, nx, 3 - k)
        for k in range(4):
            dw4_ref[k:k + 1, :] += jnp.sum(dxc * xa_sh[k], axis=0, keepdims=True)
        db4_ref[...] += jnp.sum(dxc, axis=0, keepdims=True)
        dxc_nx[...] = dxc[:SUBLANES]

        dbp = dyb * sc
        dsc = dyb * bp
        nsc = dsc_nx[...]
        ds = w3_ref[2:3, :] * dsc
        for k in range(2):
            ds = ds + w3_ref[k:k + 1, :] * _shift_up(dsc, nsc, 2 - k)
        for k in range(3):
            dw3_ref[k:k + 1, :] += jnp.sum(dsc * sv_sh[k], axis=0, keepdims=True)
        db3_ref[...] += jnp.sum(dsc, axis=0, keepdims=True)
        dsc_nx[...] = dsc[:SUBLANES]

        dp_ref[:, 0:LANES] = dxa.astype(BF16)
        dp_ref[:, LANES:2 * LANES] = dga.astype(BF16)
        dp_ref[:, 2 * LANES:3 * LANES] = (ds * vb).astype(BF16)
        dp_ref[:, 3 * LANES:4 * LANES] = dbp.astype(BF16)
        dp_ref[:, 4 * LANES:5 * LANES] = (ds * cp).astype(BF16)

    def rev(j, i):
        return (nt - 1 - i, j)

    def rev_halo(col):
        def index(j, i):
            return (jnp.maximum((nt - 1 - i) * per - 1, 0), col(j))
        return index

    blk = pl.BlockSpec((ts, 5 * LANES), rev)
    one = pl.BlockSpec((ts, LANES), rev)
    vec = pl.BlockSpec((1, LANES), lambda j, i: (0, j))
    mat = pl.BlockSpec((1, LANES, LANES), lambda j, i: (j, 0, 0))
    w4s = pl.BlockSpec((4, LANES), lambda j, i: (0, j))
    w3s = pl.BlockSpec((3, LANES), lambda j, i: (0, j))
    f = jax.ShapeDtypeStruct
    return pl.pallas_call(
        body, name=name, grid=(nblk, nt),
        in_specs=[one, pl.BlockSpec((ts, LANES), lambda j, i: (nt - 1 - i, 4 + j)),
                  blk, pl.BlockSpec((SUBLANES, 5 * LANES), rev_halo(lambda j: j)),
                  one, pl.BlockSpec((SUBLANES, LANES), rev_halo(lambda j: j)),
                  w4s, vec, mat, mat, vec, mat, mat, vec, vec, w3s, vec],
        out_specs=(blk, w4s, vec, mat, vec, mat, vec, vec, w3s, vec),
        out_shape=(f((s, 20 * LANES), BF16), f((4, 4 * LANES), F32), f((1, 4 * LANES), F32),
                   f((4, LANES, LANES), F32), f((1, 4 * LANES), F32),
                   f((4, LANES, LANES), F32), f((1, 4 * LANES), F32), f((1, 4 * LANES), F32),
                   f((3, 4 * LANES), F32), f((1, 4 * LANES), F32)),
        scratch_shapes=[pltpu.VMEM((SUBLANES, LANES), F32), pltpu.VMEM((SUBLANES, LANES), F32),
                        pltpu.VMEM((SUBLANES, LANES), F32)],
    )(dy, dy, p, p, hl, hl, w4, b4, wa, wat, ba, wx, wxt, bx, lam, w3, b3)


def _ffn_conv(u_ref, uh_ref, w_ref, b_ref, first):
    u = u_ref[...].astype(F32)
    u_h = uh_ref[...].astype(F32)[SUBLANES:] * first
    u_sh = [_shift_down(u, u_h, 2 - k) for k in range(2)] + [u]
    hc = b_ref[...]
    for k in range(3):
        hc = hc + w_ref[k:k + 1, :] * u_sh[k]
    return hc, u_sh


def _ffn_specs(ts, row, halo_row):
    nblk = D_FF // FFN_CB
    specs = []
    for off in (0, nblk):
        specs.append(pl.BlockSpec((ts, FFN_CB), lambda j, i, off=off: (row(i), off + j)))
        specs.append(pl.BlockSpec((16, FFN_CB), lambda j, i, off=off: (halo_row(i), off + j)))
        specs.append(pl.BlockSpec((3, FFN_CB), lambda j, i, off=off: (0, off + j)))
        specs.append(pl.BlockSpec((1, FFN_CB), lambda j, i, off=off: (0, off + j)))
    return specs


def _ffn_core_fwd(up, w, b, *, ts=512, name):
    s = up.shape[0]
    ts = _tile_rows(ts, s)
    nt = s // ts
    nblk = D_FF // FFN_CB
    per = ts // 16

    def body(g_ref, gh_ref, wg_ref, bg_ref, v_ref, vh_ref, wv_ref, bv_ref, act_ref):
        first = (pl.program_id(1) > 0).astype(F32)
        gate, _ = _ffn_conv(g_ref, gh_ref, wg_ref, bg_ref, first)
        val, _ = _ffn_conv(v_ref, vh_ref, wv_ref, bv_ref, first)
        act_ref[...] = (gate * _sigmoid(gate) * val).astype(BF16)

    return pl.pallas_call(
        body, name=name, grid=(nblk, nt),
        in_specs=_ffn_specs(ts, lambda i: i, lambda i: jnp.maximum(i * per - 1, 0)),
        out_specs=pl.BlockSpec((ts, FFN_CB), lambda j, i: (i, j)),
        out_shape=jax.ShapeDtypeStruct((s, D_FF), BF16),
    )(up, up, w, b, up, up, w, b)


def _ffn_core_bwd(dact, up, w, b, *, ts=512, name):
    s = up.shape[0]
    ts = _tile_rows(ts, s)
    nt = s // ts
    nblk = D_FF // FFN_CB
    per = ts // 16

    def conv_bwd(dhc, u_sh, w_ref, nx_ref, du_ref, dw_ref, db_ref):
        nx = nx_ref[...]
        du = w_ref[2:3, :] * dhc
        for k in range(2):
            du = du + w_ref[k:k + 1, :] * _shift_up(dhc, nx, 2 - k)
        du_ref[...] = du.astype(BF16)
        for k in range(3):
            dw_ref[k:k + 1, :] += jnp.sum(dhc * u_sh[k], axis=0, keepdims=True)
        db_ref[...] += jnp.sum(dhc, axis=0, keepdims=True)
        nx_ref[...] = dhc[:SUBLANES]

    def body(da_ref, g_ref, gh_ref, wg_ref, bg_ref, v_ref, vh_ref, wv_ref, bv_ref,
             dg_ref, dv_ref, dwg_ref, dwv_ref, dbg_ref, dbv_ref, nxg_ref, nxv_ref):
        i = pl.program_id(1)
        first = (nt - 1 - i > 0).astype(F32)
        gate, g_sh = _ffn_conv(g_ref, gh_ref, wg_ref, bg_ref, first)
        val, v_sh = _ffn_conv(v_ref, vh_ref, wv_ref, bv_ref, first)
        da = da_ref[...].astype(F32)
        sg = _sigmoid(gate)
        dgate = da * val * (sg * (1.0 + gate * (1.0 - sg)))
        dval = da * (gate * sg)

        @pl.when(i == 0)
        def _():
            for ref in (nxg_ref, nxv_ref, dwg_ref, dwv_ref, dbg_ref, dbv_ref):
                ref[...] = jnp.zeros_like(ref)

        conv_bwd(dgate, g_sh, wg_ref, nxg_ref, dg_ref, dwg_ref, dbg_ref)
        conv_bwd(dval, v_sh, wv_ref, nxv_ref, dv_ref, dwv_ref, dbv_ref)

    def rev(i):
        return nt - 1 - i

    tile = pl.BlockSpec((ts, FFN_CB), lambda j, i: (rev(i), j))
    w_out = pl.BlockSpec((3, FFN_CB), lambda j, i: (0, j))
    b_out = pl.BlockSpec((1, FFN_CB), lambda j, i: (0, j))
    f = jax.ShapeDtypeStruct
    return pl.pallas_call(
        body, name=name, grid=(nblk, nt),
        in_specs=[tile] + _ffn_specs(ts, rev, lambda i: jnp.maximum(rev(i) * per - 1, 0)),
        out_specs=(tile, tile, w_out, w_out, b_out, b_out),
        out_shape=(f((s, D_FF), BF16), f((s, D_FF), BF16), f((3, D_FF), F32), f((3, D_FF), F32),
                   f((1, D_FF), F32), f((1, D_FF), F32)),
        scratch_shapes=[pltpu.VMEM((SUBLANES, FFN_CB), F32), pltpu.VMEM((SUBLANES, FFN_CB), F32)],
    )(dact, up, up, w, b, up, up, w, b)


def _sgu_forward_block(zu, zg, gn, w_ref, bias, seg):
    u, tu = _gelu(zu)
    g, tg = _gelu(zg)
    ms = _dot_split(g * g, seg)
    rs = lax.rsqrt(ms + EPS)
    ghat = g * rs
    gv = ghat * gn
    gvb = gv.astype(BF16)
    lane = _lanes((CHUNK, LANES))
    chunks = []
    for c in range(zu.shape[0] // CHUNK):
        gc = gvb[c * CHUNK:(c + 1) * CHUNK]
        mix = jnp.where(lane < 64, _dot(w_ref[0], gc), _dot(w_ref[1], gc)) + bias
        chunks.append(mix)
    mixed = chunks[0] if len(chunks) == 1 else jnp.concatenate(chunks, axis=0)
    return u, tu, g, tg, rs, ghat, gvb, mixed


def _sgu_fwd(p1, gn, w, bias, seg, *, ts=512, name):
    s = p1.shape[0]
    ts = _tile_rows(ts, s)

    def body(zu_ref, zg_ref, gn_ref, w_ref, bias_ref, seg_ref, yc_ref):
        u, _, _, _, _, _, _, mixed = _sgu_forward_block(
            zu_ref[...], zg_ref[...], gn_ref[...], w_ref, bias_ref[...], seg_ref[...])
        yc_ref[...] = (u * mixed).astype(BF16)

    return pl.pallas_call(
        body, name=name, grid=(4, s // ts),
        in_specs=[pl.BlockSpec((ts, LANES), lambda j, i: (i, j)),
                  pl.BlockSpec((ts, LANES), lambda j, i: (i, 4 + j)),
                  pl.BlockSpec((1, LANES), lambda j, i: (0, j)),
                  pl.BlockSpec((2, CHUNK, CHUNK), lambda j, i: (j, 0, 0)),
                  pl.BlockSpec((CHUNK, LANES), lambda j, i: (0, j)),
                  pl.BlockSpec((LANES, LANES), lambda j, i: (0, 0))],
        out_specs=pl.BlockSpec((ts, LANES), lambda j, i: (i, j)),
        out_shape=jax.ShapeDtypeStruct((s, 4 * LANES), BF16),
    )(p1, p1, gn, w, bias, seg)


def _sgu_bwd(p1, dy, gn, w, wt, bias, seg, tril, *, ts=512, name):
    s = p1.shape[0]
    ts = _tile_rows(ts, s)
    nt = s // ts

    def body(zu_ref, zg_ref, dy_ref, gn_ref, w_ref, wt_ref, bias_ref, seg_ref, tril_ref,
             dzu_ref, dzg_ref, dw_ref, dbias_ref, dgn_ref):
        i = pl.program_id(1)
        zu = zu_ref[...]
        zg = zg_ref[...]
        gn_v = gn_ref[...]
        segv = seg_ref[...]
        u, tu, g, tg, rs, ghat, gvb, mixed = _sgu_forward_block(zu, zg, gn_v, w_ref, bias_ref[...], segv)
        dyv = dy_ref[...]
        du = dyv * mixed
        dmx = dyv * u

        @pl.when(i == 0)
        def _():
            dw_ref[...] = jnp.zeros_like(dw_ref)
            dbias_ref[...] = jnp.zeros_like(dbias_ref)
            dgn_ref[...] = jnp.zeros_like(dgn_ref)

        lane = _lanes((CHUNK, LANES))
        dgv_chunks = []
        dbias = jnp.zeros((CHUNK, LANES), F32)
        for c in range(ts // CHUNK):
            dmc = dmx[c * CHUNK:(c + 1) * CHUNK]
            gc = gvb[c * CHUNK:(c + 1) * CHUNK]
            dm_a = jnp.where(lane < 64, dmc, 0.0).astype(BF16)
            dm_b = jnp.where(lane >= 64, dmc, 0.0).astype(BF16)
            dw_ref[0] += _dot_nt(dm_a, gc)
            dw_ref[1] += _dot_nt(dm_b, gc)
            dgv_chunks.append(_dot(wt_ref[0], dm_a) + _dot(wt_ref[1], dm_b))
            dbias = dbias + dmc
        dbias_ref[...] += dbias
        dgv = dgv_chunks[0] if len(dgv_chunks) == 1 else jnp.concatenate(dgv_chunks, axis=0)
        dgn_ref[...] += jnp.sum(dgv * ghat, axis=0, keepdims=True)
        dgh = dgv * gn_v
        dg = rs * (dgh - ghat * _dot_split(dgh * ghat, segv))
        dzu_ref[...] = (du * _gelu_grad(zu, tu)).astype(BF16)
        dzg_ref[...] = (dg * _gelu_grad(zg, tg)).astype(BF16)

        @pl.when(i == nt - 1)
        def _():
            dw_ref[0] = dw_ref[0] * tril_ref[...]
            dw_ref[1] = dw_ref[1] * tril_ref[...]

    f = jax.ShapeDtypeStruct
    colj = pl.BlockSpec((ts, LANES), lambda j, i: (i, j))
    wsp = pl.BlockSpec((2, CHUNK, CHUNK), lambda j, i: (j, 0, 0))
    sq = pl.BlockSpec((LANES, LANES), lambda j, i: (0, 0))
    return pl.pallas_call(
        body, name=name, grid=(4, nt),
        in_specs=[colj, pl.BlockSpec((ts, LANES), lambda j, i: (i, 4 + j)), colj,
                  pl.BlockSpec((1, LANES), lambda j, i: (0, j)), wsp, wsp,
                  pl.BlockSpec((CHUNK, LANES), lambda j, i: (0, j)), sq, sq],
        out_specs=(colj, colj, wsp, pl.BlockSpec((CHUNK, LANES), lambda j, i: (0, j)),
                   pl.BlockSpec((1, LANES), lambda j, i: (0, j))),
        out_shape=(f((s, 4 * LANES), BF16), f((s, 4 * LANES), BF16), f((8, CHUNK, CHUNK), F32),
                   f((CHUNK, 4 * LANES), F32), f((1, 4 * LANES), F32)),
    )(p1, p1, dy, gn, w, wt, bias, seg, tril)


F_COL = 20


def _fcum_fwd(p1, bf, *, ts=512, name):
    s = p1.shape[0]
    ts = _tile_rows(ts, s)

    def body(f_ref, bf_ref, c_ref, car_ref):
        i = pl.program_id(0)
        z = f_ref[...] + bf_ref[...]
        logf = jnp.minimum(z, 0.0) - _log1p_pos(jnp.exp(-jnp.abs(z)))

        @pl.when(i == 0)
        def _():
            car_ref[...] = jnp.zeros_like(car_ref)

        c_ref[...] = _cumsum_fwd(logf) + car_ref[0:1, :]
        car_ref[0:1, :] = c_ref[ts - 1:ts, :]

    return pl.pallas_call(
        body, name=name, grid=(s // ts,),
        in_specs=[pl.BlockSpec((ts, LANES), lambda i: (i, F_COL)), pl.BlockSpec((1, LANES), lambda i: (0, 0))],
        out_specs=pl.BlockSpec((ts, LANES), lambda i: (i, 0)),
        out_shape=jax.ShapeDtypeStruct((s, LANES), F32),
        scratch_shapes=[pltpu.VMEM((SUBLANES, LANES), F32)],
    )(p1, bf)


def _fcum_bwd(dcs, p1, bf, *, ts=512, name):
    s = p1.shape[0]
    ts = _tile_rows(ts, s)
    nt = s // ts

    def body(dc_ref, f_ref, bf_ref, df_ref, dbf_ref, car_ref):
        i = pl.program_id(0)

        @pl.when(i == 0)
        def _():
            car_ref[...] = jnp.zeros_like(car_ref)
            dbf_ref[...] = jnp.zeros_like(dbf_ref)

        dlog = _cumsum_rev(dc_ref[...]) + car_ref[0:1, :]
        car_ref[...] = dlog[:SUBLANES]
        z = f_ref[...] + bf_ref[...]
        df = dlog * _sigmoid(-z)
        df_ref[...] = df.astype(BF16)
        dbf_ref[...] += jnp.sum(df, axis=0, keepdims=True)

    return pl.pallas_call(
        body, name=name, grid=(nt,),
        in_specs=[pl.BlockSpec((ts, LANES), lambda i: (nt - 1 - i, 0)),
                  pl.BlockSpec((ts, LANES), lambda i: (nt - 1 - i, F_COL)),
                  pl.BlockSpec((1, LANES), lambda i: (0, 0))],
        out_specs=(pl.BlockSpec((ts, LANES), lambda i: (nt - 1 - i, 0)), pl.BlockSpec((1, LANES), lambda i: (0, 0))),
        out_shape=(jax.ShapeDtypeStruct((s, LANES), BF16), jax.ShapeDtypeStruct((1, LANES), F32)),
        scratch_shapes=[pltpu.VMEM((SUBLANES, LANES), F32)],
    )(dcs, p1, bf)


def _fox_scores(qm, kb, bias, ck, diagonal):
    sc = _dot_nt(qm, kb) + bias - ck
    if diagonal:
        sc = jnp.where(_lanes(sc.shape) <= _rows(sc.shape), sc, NEG)
    return sc


def _head_masks(shape):
    lane = _lanes(shape)
    return lane < 64, lane >= 64


def _fox_fwd(p1, cq, ck, *, tq=512, name):
    s = p1.shape[0]
    tq = _tile_rows(tq, s)
    tk = tq
    nq = s // tq

    def body(q_ref, k_ref, v_ref, cq_ref, ck_ref, o_ref, lb_ref):
        qi = pl.program_id(1)
        q = q_ref[...] * 0.125
        first, second = _head_masks((tq, LANES))
        qms = [jnp.where(sel, q, 0.0).astype(BF16) for sel in (first, second)]
        cqs = [cq_ref[:, hh * LANES:(hh + 1) * LANES] for hh in range(2)]
        biases = [jnp.tile(cqh, (1, tk // LANES)) for cqh in cqs]

        def step(kj, carry, diagonal):
            cols = pl.ds(pl.multiple_of(kj * tk, tk), tk)
            kb = k_ref[cols, :].astype(BF16)
            vb = v_ref[cols, :].astype(BF16)
            new, outs = [], []
            acc = carry[4]
            for hh in range(2):
                m_prev, l_prev = carry[2 * hh], carry[2 * hh + 1]
                sc = _fox_scores(qms[hh], kb, biases[hh], ck_ref[hh, :, cols], diagonal)
                m_new = jnp.maximum(m_prev, jnp.max(sc, axis=1, keepdims=True))
                pm = jnp.exp(sc - jnp.tile(m_new, (1, tk // LANES)))
                alpha = jnp.exp(m_prev - m_new)
                new += [m_new, alpha * l_prev + jnp.sum(pm, axis=1, keepdims=True)]
                outs.append(acc * alpha + _dot(pm.astype(BF16), vb))
            return tuple(new) + (jnp.where(first, outs[0], outs[1]),)

        zero = jnp.zeros((tq, LANES), F32)
        low = jnp.full((tq, LANES), NEG, F32)
        carry = lax.fori_loop(0, qi, lambda kj, c: step(kj, c, False), (low, zero, low, zero, zero))
        m0, l0, m1, l1, acc = step(qi, carry, True)
        o_ref[...] = (acc / jnp.where(first, l0, l1)).astype(BF16)
        lb_ref[:, 0:LANES] = cqs[0] - (m0 + jnp.log(l0))
        lb_ref[:, LANES:2 * LANES] = cqs[1] - (m1 + jnp.log(l1))

    return pl.pallas_call(
        body, name=name, grid=(4, nq),
        in_specs=[pl.BlockSpec((tq, LANES), lambda j, qi: (qi, 8 + j)),
                  pl.BlockSpec((s, LANES), lambda j, qi: (0, 12 + j)),
                  pl.BlockSpec((s, LANES), lambda j, qi: (0, 16 + j)),
                  pl.BlockSpec((tq, 2 * LANES), lambda j, qi: (qi, j)),
                  pl.BlockSpec((2, 1, s), lambda j, qi: (j, 0, 0))],
        out_specs=(pl.BlockSpec((tq, LANES), lambda j, qi: (qi, j)),
                   pl.BlockSpec((tq, 2 * LANES), lambda j, qi: (qi, j))),
        out_shape=(jax.ShapeDtypeStruct((s, 4 * LANES), BF16), jax.ShapeDtypeStruct((s, 8 * LANES), F32)),
    )(p1, p1, p1, cq, ck)


def _fox_delta(dy, o, sel, *, ts=512, name):
    s = o.shape[0]
    ts = _tile_rows(ts, s)

    def body(do_ref, o_ref, sel_ref, d_ref):
        prod = do_ref[...] * o_ref[...].astype(F32)
        d_ref[:, 0:LANES] = _dot_split(prod, sel_ref[0])
        d_ref[:, LANES:2 * LANES] = _dot_split(prod, sel_ref[1])

    return pl.pallas_call(
        body, name=name, grid=(4, s // ts),
        in_specs=[pl.BlockSpec((ts, LANES), lambda j, i: (i, 4 + j)),
                  pl.BlockSpec((ts, LANES), lambda j, i: (i, j)),
                  pl.BlockSpec((2, LANES, LANES), lambda j, i: (0, 0, 0))],
        out_specs=pl.BlockSpec((ts, 2 * LANES), lambda j, i: (i, j)),
        out_shape=jax.ShapeDtypeStruct((s, 8 * LANES), F32),
    )(dy, o, sel)


def _fox_bwd(p1, dy, lb, delta, ck, *, tq=512, name):
    s = p1.shape[0]
    tq = _tile_rows(tq, s)
    tk = tq
    nq = s // tq

    def body(q_ref, k_ref, v_ref, do_ref, lb_ref, dl_ref, ck_ref,
             dq_ref, dk_ref, dv_ref, dck_ref, dcq_ref, dqa_ref, dra_ref):
        kj = pl.program_id(1)

        @pl.when(kj == 0)
        def _():
            dqa_ref[...] = jnp.zeros_like(dqa_ref)
            dra_ref[...] = jnp.zeros_like(dra_ref)

        kf = k_ref[...]
        kb = kf.astype(BF16)
        vb = v_ref[...].astype(BF16)
        first, second = _head_masks((tk, LANES))
        kms = [jnp.where(sel, kf, 0.0).astype(BF16) for sel in (first, second)]
        cks = [ck_ref[hh] for hh in range(2)]

        def step(qi, carry, diagonal):
            dk_acc, dv_acc, dc0, dc1 = carry
            dcs = [dc0, dc1]
            rows = pl.ds(pl.multiple_of(qi * tq, tq), tq)
            q = q_ref[rows, :] * 0.125
            do = do_ref[rows, :]
            for hh, sel in enumerate((first, second)):
                qm = jnp.where(sel, q, 0.0).astype(BF16)
                dom = jnp.where(sel, do, 0.0).astype(BF16)
                bias = jnp.tile(lb_ref[rows, hh * LANES:(hh + 1) * LANES], (1, tk // LANES))
                pm = jnp.exp(_fox_scores(qm, kb, bias, cks[hh], diagonal))
                dv_acc = dv_acc + _dot_tn(pm.astype(BF16), dom)
                dp = _dot_nt(dom, vb)
                ds = pm * (dp - jnp.tile(dl_ref[rows, hh * LANES:(hh + 1) * LANES], (1, tk // LANES)))
                dsb = ds.astype(BF16)
                dk_acc = dk_acc + _dot_tn(dsb, qm)
                dcs[hh] = dcs[hh] - jnp.sum(ds, axis=0, keepdims=True)
                dqa_ref[rows, :] += _dot(dsb, kms[hh])
                dra_ref[hh, rows, :] += jnp.sum(ds, axis=1, keepdims=True)
            return dk_acc, dv_acc, dcs[0], dcs[1]

        zero = jnp.zeros((tk, LANES), F32)
        zrow = jnp.zeros((1, tk), F32)
        carry = step(kj, (zero, zero, zrow, zrow), True)
        dk_acc, dv_acc, dc0, dc1 = lax.fori_loop(kj + 1, nq, lambda qi, c: step(qi, c, False), carry)
        dk_ref[...] = dk_acc.astype(BF16)
        dv_ref[...] = dv_acc.astype(BF16)
        dck_ref[0] = dc0
        dck_ref[1] = dc1

        @pl.when(kj == nq - 1)
        def _():
            dq_ref[...] = (dqa_ref[...] * 0.125).astype(BF16)
            dcq_ref[:, 0:LANES] = dra_ref[0]
            dcq_ref[:, LANES:2 * LANES] = dra_ref[1]

    def full(width, col0):
        return pl.BlockSpec((s, width), lambda j, kj: (0, col0 + j))

    kblk = pl.BlockSpec((tk, LANES), lambda j, kj: (kj, j))
    f = jax.ShapeDtypeStruct
    return pl.pallas_call(
        body, name=name, grid=(4, nq),
        in_specs=[full(LANES, 8),
                  pl.BlockSpec((tk, LANES), lambda j, kj: (kj, 12 + j)),
                  pl.BlockSpec((tk, LANES), lambda j, kj: (kj, 16 + j)),
                  full(LANES, 4), full(2 * LANES, 0), full(2 * LANES, 0),
                  pl.BlockSpec((2, 1, tk), lambda j, kj: (j, 0, kj))],
        out_specs=(full(LANES, 0), kblk, kblk, pl.BlockSpec((2, 1, tk), lambda j, kj: (j, 0, kj)),
                   full(2 * LANES, 0)),
        out_shape=(f((s, 4 * LANES), BF16), f((s, 4 * LANES), BF16), f((s, 4 * LANES), BF16),
                   f((8, 1, s), F32), f((s, 8 * LANES), F32)),
        scratch_shapes=[pltpu.VMEM((s, LANES), F32), pltpu.VMEM((2, s, LANES), F32)],
    )(p1, p1, p1, dy, lb, delta, ck)


def _row_block(r, cap=256):
    best = None
    for rb in range(2 * SUBLANES, min(r, cap) + 1, 2 * SUBLANES):
        if r % rb == 0:
            best = rb
    return r if best is None else best


def _adamw(w, g, m, v, *, name):
    r, c = w.shape
    rb = _row_block(r)

    def body(w_ref, g_ref, m_ref, v_ref, d_ref, nm_ref, nv_ref):
        gv = g_ref[...]
        mn = ADAM_B1 * m_ref[...] + (1.0 - ADAM_B1) * gv
        vn = ADAM_B2 * v_ref[...] + (1.0 - ADAM_B2) * (gv * gv)
        m_hat = mn / ADAM_C1
        v_hat = vn / ADAM_C2
        d_ref[...] = (-ADAM_LR) * (m_hat / (jnp.sqrt(v_hat) + ADAM_EPS) + ADAM_WD * w_ref[...])
        nm_ref[...] = mn
        nv_ref[...] = vn

    blk = pl.BlockSpec((rb, c), lambda i: (i, 0))
    shp = jax.ShapeDtypeStruct((r, c), F32)
    return pl.pallas_call(
        body, name=name, grid=(r // rb,), in_specs=[blk] * 4, out_specs=(blk,) * 3, out_shape=(shp,) * 3,
    )(w, g, m, v)


def _pair_sum(g, col, ra, core, *, name):
    _, rh, c = ra.shape
    rb = _row_block(rh)

    def body(core_ref, g_ref, ra_ref, h_ref, h16_ref):
        tot = g_ref[...] + ra_ref[...]
        h_ref[...] = tot
        h16_ref[...] = tot.astype(BF16)

    if col:
        g_spec = pl.BlockSpec((None, rb, c), lambda k, i, core_ref: (core_ref[0], i, k))
    else:
        g_spec = pl.BlockSpec((None, None, rb, c), lambda k, i, core_ref: (k, core_ref[0], i, 0))
    slot = pl.BlockSpec((None, rb, c), lambda k, i, core_ref: (k, i, 0))
    return pl.pallas_call(
        body, name=name,
        grid_spec=pltpu.PrefetchScalarGridSpec(
            num_scalar_prefetch=1, grid=(N_CHIPS, rh // rb), in_specs=[g_spec, slot], out_specs=(slot, slot)),
        out_shape=(jax.ShapeDtypeStruct((N_CHIPS, rh, c), F32), jax.ShapeDtypeStruct((N_CHIPS, rh, c), BF16)),
    )(core, g, ra)


def _first_sum(h, r1, keep, *, name):
    _, rh, c = h.shape
    rb = _row_block(rh)

    def body(keep_ref, h_ref, r_ref, s_ref, s16_ref):
        tot = h_ref[...] + r_ref[...].astype(F32)
        s_ref[...] = tot
        s16_ref[...] = tot.astype(BF16)

    slot = pl.BlockSpec((None, rb, c), lambda t, i, keep_ref: (t, i, 0))
    return pl.pallas_call(
        body, name=name,
        grid_spec=pltpu.PrefetchScalarGridSpec(
            num_scalar_prefetch=1, grid=(2, rh // rb),
            in_specs=[pl.BlockSpec((None, rb, c), lambda t, i, keep_ref: (keep_ref[t], i, 0)), slot],
            out_specs=(slot, slot)),
        out_shape=(jax.ShapeDtypeStruct((2, rh, c), F32), jax.ShapeDtypeStruct((2, rh, c), BF16)),
    )(keep, h, r1)


def _second_sum(s1, r2, sel, *, name):
    _, rh, c = s1.shape
    rb = _row_block(rh)

    def body(sel_ref, s_ref, r_ref, t_ref):
        t_ref[...] = s_ref[...] + r_ref[...].astype(F32)

    return pl.pallas_call(
        body, name=name,
        grid_spec=pltpu.PrefetchScalarGridSpec(
            num_scalar_prefetch=1, grid=(rh // rb,),
            in_specs=[pl.BlockSpec((None, rb, c), lambda i, sel_ref: (sel_ref[0], i, 0)),
                      pl.BlockSpec((rb, c), lambda i, sel_ref: (i, 0))],
            out_specs=pl.BlockSpec((None, rb, c), lambda i, sel_ref: (sel_ref[1], i, 0))),
        out_shape=jax.ShapeDtypeStruct((2, rh, c), F32),
    )(sel, s1, r2)


def _place(shard, col, chip, dtype, *, name):
    r, c = shard.shape
    rh = r // 2
    rb = _row_block(rh)

    def body(chip_ref, s_ref, o_ref):
        o_ref[...] = s_ref[...].astype(o_ref.dtype)

    if col:
        out_spec = pl.BlockSpec((None, rb, c), lambda h, i, chip_ref: (h, i, chip_ref[0]))
        shape = (2, rh, N_CHIPS * c)
    else:
        out_spec = pl.BlockSpec((None, None, rb, c), lambda h, i, chip_ref: (chip_ref[0], h, i, 0))
        shape = (N_CHIPS, 2, rh, c)
    per = rh // rb
    return pl.pallas_call(
        body, name=name,
        grid_spec=pltpu.PrefetchScalarGridSpec(
            num_scalar_prefetch=1, grid=(2, per),
            in_specs=[pl.BlockSpec((rb, c), lambda h, i, chip_ref: (h * per + i, 0))], out_specs=out_spec),
        out_shape=jax.ShapeDtypeStruct(shape, dtype),
    )(chip, shard)


ANY = pl.BlockSpec(memory_space=pl.ANY)


def _mesh_pos():
    return lax.axis_index("x"), lax.axis_index("y"), lax.axis_index("c")


def _other_chips(x, y):
    return [(1 - x, y), (x, 1 - y), (1 - x, 1 - y)]


def _remote(src, dst, ssem, rsem, dev):
    return pltpu.make_async_remote_copy(src_ref=src, dst_ref=dst, send_sem=ssem, recv_sem=rsem,
                                        device_id=dev, device_id_type=MESH)


def _flip(a, b):
    return a + b - 2 * a * b


def _slab(ref, col, width, k, h):
    if not col:
        return ref.at[k, h]
    start = k * width if isinstance(k, int) else pl.multiple_of(k * width, LANES)
    return ref.at[h, :, pl.ds(start, width)]


def _all_gather(bufs, cols, *, name):
    n = len(bufs)
    widths = [b.shape[2] // N_CHIPS if col else b.shape[3] for b, col in zip(bufs, cols)]

    def body(*refs):
        outs = refs[n:2 * n]
        ssem, rsem = refs[2 * n:]
        x, y, c = _mesh_pos()
        me = 2 * x + y
        sib = (x, y, 1 - c)
        n1 = (_flip(x, 1 - c), _flip(y, c))
        n2 = (_flip(x, c), _flip(y, 1 - c))
        k1 = 2 * n1[0] + n1[1]
        k2 = 2 * n2[0] + n2[1]
        kd = 2 * (1 - x) + (1 - y)

        def slab(a, k, h):
            return _slab(outs[a], cols[a], widths[a], k, h)

        def copy(a, j, src, dst, dev):
            return _remote(src, dst, ssem.at[a, j], rsem.at[a, j], dev)

        sends = []
        for a in range(n):
            for j, nb in ((0, n1), (1, n2)):
                own = slab(a, me, c)
                cp = copy(a, j, own, own, nb + (c,))
                cp.start()
                sends.append(cp)
        arrivals = ((0, k1, n1, 3), (1, k2, n2, 4), (2, kd, n2, 5))
        for j, k, nb, fwd in arrivals:
            for a in range(n):
                got = slab(a, k, c)
                copy(a, j, got, got, nb + (c,)).wait_recv()
                if j == 0:
                    cp = copy(a, 2, got, got, n2 + (c,))
                    cp.start()
                    sends.append(cp)
                cp = copy(a, fwd, got, got, sib)
                cp.start()
                sends.append(cp)
        for fwd, k in ((3, k2), (4, k1), (5, kd)):
            for a in range(n):
                got = slab(a, k, 1 - c)
                copy(a, fwd, got, got, sib).wait_recv()
        for cp in sends:
            cp.wait_send()

    return pl.pallas_call(
        body, name=name, in_specs=[ANY] * n, out_specs=[ANY] * n,
        out_shape=[jax.ShapeDtypeStruct(b.shape, b.dtype) for b in bufs],
        input_output_aliases={a: a for a in range(n)},
        scratch_shapes=[pltpu.SemaphoreType.DMA((n, 6)), pltpu.SemaphoreType.DMA((n, 6))],
    )(*bufs)


def _send_other_half(grads, cols, *, name):
    n = len(grads)

    def shard_shape(g, col):
        if col:
            return (g.shape[1], g.shape[2] // N_CHIPS)
        return g.shape[2:]

    shapes = [shard_shape(g, col) for g, col in zip(grads, cols)]

    def body(*refs):
        ins, outs = refs[:n], refs[n:2 * n]
        ssem, rsem = refs[2 * n:]
        x, y, c = _mesh_pos()
        sib = (x, y, 1 - c)
        sends = []
        for a in range(n):
            for k in range(N_CHIPS):
                src = _slab(ins[a], cols[a], shapes[a][1], k, 1 - c)
                cp = _remote(src, outs[a].at[k], ssem.at[a, k], rsem.at[a, k], sib)
                cp.start()
                sends.append(cp)
        for cp in sends:
            cp.wait()

    return pl.pallas_call(
        body, name=name, in_specs=[ANY] * n, out_specs=[ANY] * n,
        out_shape=[jax.ShapeDtypeStruct((N_CHIPS,) + shp, g.dtype) for g, shp in zip(grads, shapes)],
        scratch_shapes=[pltpu.SemaphoreType.DMA((n, N_CHIPS)), pltpu.SemaphoreType.DMA((n, N_CHIPS))],
    )(*grads)


def _send_first(sums, *, name):
    n = len(sums)

    def body(*refs):
        ins, outs = refs[:n], refs[n:2 * n]
        ssem, rsem = refs[2 * n:]
        x, y, c = _mesh_pos()
        nb = (_flip(x, c), _flip(y, 1 - c), c)
        sends = []
        for a in range(n):
            for t in range(2):
                k = 2 * (c * (1 - x) + (1 - c) * t) + (c * t + (1 - c) * (1 - y))
                cp = _remote(ins[a].at[k], outs[a].at[t], ssem.at[a, t], rsem.at[a, t], nb)
                cp.start()
                sends.append(cp)
        for cp in sends:
            cp.wait()

    return pl.pallas_call(
        body, name=name, in_specs=[ANY] * n, out_specs=[ANY] * n,
        out_shape=[jax.ShapeDtypeStruct((2,) + h.shape[1:], h.dtype) for h in sums],
        scratch_shapes=[pltpu.SemaphoreType.DMA((n, 2)), pltpu.SemaphoreType.DMA((n, 2))],
    )(*sums)


def _send_second(sums, *, name):
    n = len(sums)

    def body(*refs):
        ins, outs = refs[:n], refs[n:2 * n]
        ssem, rsem = refs[2 * n:]
        x, y, c = _mesh_pos()
        nb = (_flip(x, 1 - c), _flip(y, c), c)
        other = 1 - (c * y + (1 - c) * x)
        sends = []
        for a in range(n):
            cp = _remote(ins[a].at[other], outs[a], ssem.at[a], rsem.at[a], nb)
            cp.start()
            sends.append(cp)
        for cp in sends:
            cp.wait()

    return pl.pallas_call(
        body, name=name, in_specs=[ANY] * n, out_specs=[ANY] * n,
        out_shape=[jax.ShapeDtypeStruct(s.shape[1:], s.dtype) for s in sums],
        scratch_shapes=[pltpu.SemaphoreType.DMA((n,)), pltpu.SemaphoreType.DMA((n,))],
    )(*sums)


def _swap_halves(bufs, *, name):
    n = len(bufs)

    def body(*refs):
        outs = refs[n:2 * n]
        ssem, rsem = refs[2 * n:]
        x, y, c = _mesh_pos()
        sib = (x, y, 1 - c)
        cps = []
        for a in range(n):
            cp = _remote(outs[a].at[c], outs[a].at[c], ssem.at[a], rsem.at[a], sib)
            cp.start()
            cps.append(cp)
        for a, cp in enumerate(cps):
            cp.wait_send()
            theirs = outs[a].at[1 - c]
            _remote(theirs, theirs, ssem.at[a], rsem.at[a], sib).wait_recv()

    return pl.pallas_call(
        body, name=name, in_specs=[ANY] * n, out_specs=[ANY] * n,
        out_shape=[jax.ShapeDtypeStruct(b.shape, b.dtype) for b in bufs],
        input_output_aliases={a: a for a in range(n)},
        scratch_shapes=[pltpu.SemaphoreType.DMA((n,)), pltpu.SemaphoreType.DMA((n,))],
    )(*bufs)


def _all_reduce_small(buf, *, name):
    r = buf.shape[0]
    rh = r // 2

    def body(in_ref, out_ref, x1_ref, x2_ref, ssem, rsem):
        x, y, c = _mesh_pos()
        me = 2 * x + y
        sib = (x, y, 1 - c)
        chips = _other_chips(x, y)
        cp = _remote(in_ref, x1_ref, ssem.at[0], rsem.at[0], sib)
        cp.start()
        cp.wait()
        off = pl.multiple_of(c * rh, SUBLANES)
        x2_ref[me] = in_ref[pl.ds(off, rh), :] + x1_ref[pl.ds(off, rh), :]
        sends = []
        for j, (cx, cy) in enumerate(chips):
            s = _remote(x2_ref.at[me], x2_ref.at[me], ssem.at[1 + j], rsem.at[1 + j], (cx, cy, c))
            s.start()
            sends.append(s)
        for j, (cx, cy) in enumerate(chips):
            slot = x2_ref.at[2 * cx + cy]
            _remote(slot, slot, ssem.at[1 + j], rsem.at[1 + j], (cx, cy, c)).wait_recv()
        out_ref[pl.ds(off, rh), :] = ((x2_ref[0] + x2_ref[1]) + x2_ref[2]) + x2_ref[3]
        for s in sends:
            s.wait_send()
        mine = out_ref.at[pl.ds(off, rh), :]
        s3 = _remote(mine, mine, ssem.at[4], rsem.at[4], sib)
        s3.start()
        off2 = pl.multiple_of((1 - c) * rh, SUBLANES)
        theirs = out_ref.at[pl.ds(off2, rh), :]
        _remote(theirs, theirs, ssem.at[4], rsem.at[4], sib).wait_recv()
        s3.wait_send()

    vm = pl.BlockSpec(memory_space=pltpu.VMEM)
    return pl.pallas_call(
        body, name=name, in_specs=[vm], out_specs=vm,
        out_shape=jax.ShapeDtypeStruct((r, LANES), F32),
        scratch_shapes=[pltpu.VMEM((r, LANES), F32), pltpu.VMEM((N_CHIPS, rh, LANES), F32),
                        pltpu.SemaphoreType.DMA((5,)), pltpu.SemaphoreType.DMA((5,))],
    )(buf)


PACK_ALIGN = 2 * SUBLANES * LANES


def _pack(arrays, rows_multiple=2 * SUBLANES):
    parts, offs, off = [], [], 0
    for a in arrays:
        flat = a.reshape(-1).astype(F32)
        padded = -(-flat.shape[0] // PACK_ALIGN) * PACK_ALIGN
        parts.append(jnp.pad(flat, (0, padded - flat.shape[0])))
        offs.append(off)
        off += padded
    buf = jnp.concatenate(parts).reshape(-1, LANES)
    return buf, offs


def _unpack(buf, offs, shapes):
    flat = buf.reshape(-1)
    out = []
    for off, shp in zip(offs, shapes):
        size = 1
        for d in shp:
            size *= d
        out.append(flat[off:off + size].reshape(shp))
    return out


def _cols_from_shards(g4):
    _, k, ns = g4.shape
    return jnp.transpose(g4, (1, 0, 2)).reshape(k, N_CHIPS * ns)


def _cols_to_shards(w):
    k, n = w.shape
    return jnp.transpose(w.reshape(k, N_CHIPS, n // N_CHIPS), (1, 0, 2))


def _block_cols(w, parts, blocks):
    lead = w.shape[:-1]
    width = w.shape[-1] // (parts * blocks)
    w = w.reshape(lead + (parts, blocks, width))
    w = jnp.swapaxes(w, -3, -2)
    return w.reshape(lead + (parts * blocks * width,))


def _unblock_cols(w, parts, blocks):
    lead = w.shape[:-1]
    width = w.shape[-1] // (parts * blocks)
    w = w.reshape(lead + (blocks, parts, width))
    w = jnp.swapaxes(w, -3, -2)
    return w.reshape(lead + (parts * blocks * width,))


def _pair_blockdiag(w8):
    w = w8.reshape(4, 2, 64, 64)
    z = jnp.zeros((4, 64, 64), w8.dtype)
    top = jnp.concatenate([w[:, 0], z], axis=2)
    bot = jnp.concatenate([z, w[:, 1]], axis=2)
    return jnp.concatenate([top, bot], axis=1)


def _pair_diag_blocks(w4):
    a = w4[:, :64, :64]
    b = w4[:, 64:, 64:]
    return jnp.stack([a, b], axis=1).reshape(8, 64, 64)


def _local_step(x, target, wts):
    s = x.shape[0]
    g = {}

    win0 = wts["w_in0"]
    wout0 = wts["w_out0"]
    win1 = wts["w_in1"]
    wout1 = wts["w_out1"]
    wup = wts["w_up"]
    wdown = wts["w_down"]
    w4, b4, w3, b3 = wts["w4"], wts["b4"], wts["w3"], wts["b3"]
    wa, wx = wts["wa"], wts["wx"]
    wat, wxt = jnp.swapaxes(wa, 1, 2), jnp.swapaxes(wx, 1, 2)
    ba, bx, lam = wts["ba"], wts["bx"], wts["lam"]
    fcw, fcb = wts["ffn_cw"], wts["ffn_cb"]
    sgu_w, sgu_wt = wts["sgu_w"], wts["sgu_wt"]
    sgu_bias, sgu_gn = wts["sgu_bias"], wts["sgu_gn"]
    bf = wts["bf"]

    lane = jnp.arange(LANES)
    seg = jnp.where((lane[:, None] // 64) == (lane[None, :] // 64), 1.0 / 64.0, 0.0).astype(BF16)
    sel = jnp.stack([jnp.broadcast_to((lane[:, None] < 64), (LANES, LANES)),
                     jnp.broadcast_to((lane[:, None] >= 64), (LANES, LANES))]).astype(BF16)
    tril = (lane[:, None] >= lane[None, :]).astype(F32)

    n0 = _norm_fwd(x, wts["g_mix0"], name="norm_mix0")
    p0 = _mm([n0], win0, nb=640, name="mm_in0")
    ya, yb, hl = _even_core_fwd(p0, w4, b4, wa, ba, wx, bx, lam, w3, b3, name="even_fwd")
    h1 = _mm([ya, yb], wout0, res=x, name="mm_out0")

    def ffn_fwd(h, layer):
        n = _norm_fwd(h, wts["g_ffn"][layer], name=f"norm_ffn{layer}")
        up = _mm([n], wup[layer], out_dtype=BF16, nb=1408, name=f"mm_up{layer}")
        act = _ffn_core_fwd(up, fcw[layer], fcb[layer], name=f"ffn_fwd{layer}")
        hn = _mm([act], wdown[layer], res=h, name=f"mm_down{layer}")
        return n, up, act, hn

    n1, up0, act0, h2 = ffn_fwd(h1, 0)

    n2 = _norm_fwd(h2, wts["g_mix1"], name="norm_mix1")
    p1 = _mm([n2], win1, nb=896, name="mm_in1")
    yc = _sgu_fwd(p1, sgu_gn, sgu_w, sgu_bias, seg, name="sgu_fwd")
    cum = _fcum_fwd(p1, bf, name="fcum_fwd")
    c8 = cum[:, :8]
    cq = jnp.broadcast_to(c8[:, :, None], (s, 8, LANES)).reshape(s, 8 * LANES)
    ck = jnp.transpose(c8).reshape(8, 1, s)
    yd, lb = _fox_fwd(p1, cq, ck, name="fox_fwd")
    h3 = _mm([yc, yd], wout1, res=h2, name="mm_out1")

    n3, up1, act1, h4 = ffn_fwd(h3, 1)
    dh4, loss, g["final_norm"] = _final(h4, wts["g_final"], target, name="final")

    def ffn_bwd(dh, h, n, up, act, layer):
        dact = _mm([dh], wdown[layer], trans_w=True, out_dtype=BF16, nb=1408, name=f"mm_dact{layer}")
        dwd = _mm_tn([act], [dh], nb=512, name=f"mm_dwdown{layer}")
        dgate, dval, dcwg, dcwv, dcbg, dcbv = _ffn_core_bwd(dact, up, fcw[layer], fcb[layer], name=f"ffn_bwd{layer}")
        dn = _mm([dgate, dval], wup[layer], trans_w=True, nb=512, name=f"mm_dn_ffn{layer}")
        dwu = _mm_tn([n], [dgate, dval], nb=1408, name=f"mm_dwup{layer}")
        dhn, dg = _norm_bwd(dn, h, wts["g_ffn"][layer], dh, name=f"norm_bwd_ffn{layer}")
        dcw = jnp.concatenate([dcwg, dcwv], axis=1)
        dcb = jnp.concatenate([dcbg, dcbv], axis=1)
        return dhn, dwd, dwu, dcw, dcb, dg

    dh3, g["w_down1"], g["w_up1"], g["ffn_cw1"], g["ffn_cb1"], g["g_ffn1"] = ffn_bwd(dh4, h3, n3, up1, act1, 1)

    dy1 = _mm([dh3], wout1, trans_w=True, name="mm_dy1")
    g["w_out1"] = _mm_tn([yc, yd], [dh3], nb=512, name="mm_dwout1")
    dzu, dzg, g["sgu_w"], g["sgu_bias"], g["sgu_gn"] = _sgu_bwd(
        p1, dy1, sgu_gn, sgu_w, sgu_wt, sgu_bias, seg, tril, name="sgu_bwd")
    delta = _fox_delta(dy1, yd, sel, name="fox_delta")
    dq, dk, dv, dck, dcq = _fox_bwd(p1, dy1, lb, delta, ck, name="fox_bwd")
    dcs = jnp.pad(jnp.transpose(dck.reshape(8, s)) + dcq.reshape(s, 8, LANES)[:, :, 0], ((0, 0), (0, LANES - 8)))
    df, g["bf"] = _fcum_bwd(dcs, p1, bf, name="fcum_bwd")
    dp1 = jnp.concatenate([dzu, dzg, dq, dk, dv, df], axis=1)
    dn2 = _mm([dp1], win1, trans_w=True, name="mm_dn_mix1")
    g["w_in1"] = _mm_tn([n2], [dp1], nb=896, name="mm_dwin1")
    dh2, g["g_mix1"] = _norm_bwd(dn2, h2, wts["g_mix1"], dh3, name="norm_bwd_mix1")

    dh1, g["w_down0"], g["w_up0"], g["ffn_cw0"], g["ffn_cb0"], g["g_ffn0"] = ffn_bwd(dh2, h1, n1, up0, act0, 0)

    dy0 = _mm([dh1], wout0, trans_w=True, name="mm_dy0")
    g["w_out0"] = _mm_tn([ya, yb], [dh1], nb=512, name="mm_dwout0")
    (dp0, g["w4"], g["b4"], g["wa"], g["ba"], g["wx"], g["bx"], g["lam"], g["w3"], g["b3"]) = _even_core_bwd(
        dy0, p0, hl, w4, b4, wa, wat, ba, wx, wxt, bx, lam, w3, b3, name="even_bwd")
    dn0 = _mm([dp0], win0, trans_w=True, name="mm_dn_mix0")
    g["w_in0"] = _mm_tn([n0], [dp0], nb=640, name="mm_dwin0")
    grad_x, g["g_mix0"] = _norm_bwd(dn0, x, wts["g_mix0"], dh1, name="norm_bwd_mix0")
    return loss, grad_x, g


def _prepare_weights(nat):
    lane = jnp.arange(LANES)
    tril = (lane[:, None] >= lane[None, :]).astype(F32)
    sgu_tril = nat["sgu_w"][0] * tril
    w_in1 = nat["mix1_w_in"]
    nblk = D_FF // FFN_CB
    return {
        "w_in0": _block_cols(nat["mix0_w_in"], 5, 4),
        "w_out0": nat["mix0_w_out"],
        "w_in1": jnp.pad(w_in1, ((0, 0), (0, 21 * LANES - w_in1.shape[1]))),
        "w_out1": nat["mix1_w_out"],
        "w_up": [nat["ffn_up"][l] for l in range(2)],
        "w_down": [nat["ffn_down"][l] for l in range(2)],
        "w4": nat["lru_conv_w"], "b4": nat["lru_conv_b"], "w3": nat["sconv_w"], "b3": nat["sconv_b"],
        "wa": _pair_blockdiag(nat["lru_wa"][0]).astype(BF16), "wx": _pair_blockdiag(nat["lru_wx"][0]).astype(BF16),
        "ba": nat["lru_ba"], "bx": nat["lru_bx"], "lam": nat["lru_lambda"],
        "ffn_cw": [nat["ffn_conv_w"][l] for l in range(2)],
        "ffn_cb": [nat["ffn_conv_b"][l:l + 1] for l in range(2)],
        "sgu_w": sgu_tril.astype(BF16), "sgu_wt": jnp.swapaxes(sgu_tril, 1, 2).astype(BF16),
        "sgu_bias": jnp.repeat(jnp.transpose(nat["sgu_b"][0]), 64, axis=1), "sgu_gn": nat["sgu_norm"],
        "bf": jnp.pad(nat["fox_bf"], ((0, 0), (0, LANES - 8))),
        "g_mix0": nat["mix0_norm"], "g_mix1": nat["mix1_norm"],
        "g_ffn": [nat["ffn_norm"][0:1], nat["ffn_norm"][1:2]], "g_final": nat["final_norm"].reshape(1, D_MODEL),
    }


def _natural_grads(g):
    nblk = D_FF // FFN_CB
    small = {
        "mix0_norm": g["g_mix0"], "lru_conv_b": g["b4"],
        "lru_wa": _pair_diag_blocks(g["wa"])[None], "lru_ba": g["ba"],
        "lru_wx": _pair_diag_blocks(g["wx"])[None], "lru_bx": g["bx"],
        "lru_lambda": g["lam"], "sconv_b": g["b3"],
        "sgu_w": g["sgu_w"][None],
        "sgu_b": jnp.transpose(g["sgu_bias"].reshape(CHUNK, 8, 64).sum(axis=2))[None],
        "fox_bf": g["bf"][:, :8],
        "ffn_norm": jnp.concatenate([g["g_ffn0"], g["g_ffn1"]], axis=0),
        "ffn_conv_b": jnp.concatenate([g["ffn_cb0"], g["ffn_cb1"]], axis=0),
        "final_norm": g["final_norm"].reshape(D_MODEL),
        "lru_conv_w": g["w4"][None], "sconv_w": g["w3"][None],
        "ffn_conv_w": jnp.stack([g["ffn_cw0"], g["ffn_cw1"]]),
        "mix1_norm": g["g_mix1"], "sgu_norm": g["sgu_gn"],
    }
    big = {
        "mix0_w_in": _unblock_cols(g["w_in0"], 5, 4), "mix0_w_out": g["w_out0"],
        "mix1_w_in": g["w_in1"][:, :2568], "mix1_w_out": g["w_out1"],
        "ffn_up0": g["w_up0"], "ffn_up1": g["w_up1"],
        "ffn_down0": g["w_down0"], "ffn_down1": g["w_down1"],
    }
    return small, big


COL_SHARDED = ("mix0_w_in", "mix1_w_in", "ffn_up0", "ffn_up1")
COL_ALIGNED = ("mix0_w_in", "ffn_up0", "ffn_up1")
SMALL_SHARDED = ("lru_conv_w", "sconv_w", "ffn_conv_w", "mix1_norm", "sgu_norm")
SMALL_REPLICATED = ("mix0_norm", "lru_conv_b", "lru_wa", "lru_ba", "lru_wx", "lru_bx", "lru_lambda", "sconv_b",
                    "sgu_w", "sgu_b", "fox_bf", "ffn_norm", "ffn_conv_b", "final_norm")
BIG = ("mix0_w_in", "mix0_w_out", "mix1_w_in", "mix1_w_out", "ffn_up0", "ffn_up1", "ffn_down0", "ffn_down1")
WEIGHT_ORDER = ("mix0_norm", "mix0_w_in", "lru_conv_w", "lru_conv_b", "lru_wa", "lru_ba", "lru_wx", "lru_bx",
                "lru_lambda", "sconv_w", "sconv_b", "mix0_w_out", "mix1_norm", "mix1_w_in", "sgu_norm", "sgu_w",
                "sgu_b", "fox_bf", "mix1_w_out", "ffn_norm", "ffn_up", "ffn_conv_w", "ffn_conv_b", "ffn_down",
                "final_norm")


def _halves(a):
    r = a.shape[0]
    return a.reshape((2, r // 2) + a.shape[1:])


def _train_step(x, target, w, m, v):
    x2 = x[0]
    t2 = target[0]
    chip = 2 * lax.axis_index("x") + lax.axis_index("y")
    core = lax.axis_index("c")

    big_shards = {
        "mix0_w_in": w["mix0_w_in"][0], "mix0_w_out": w["mix0_w_out"][0],
        "mix1_w_in": w["mix1_w_in"][0], "mix1_w_out": w["mix1_w_out"][0],
        "ffn_up0": w["ffn_up"][0], "ffn_up1": w["ffn_up"][1],
        "ffn_down0": w["ffn_down"][0], "ffn_down1": w["ffn_down"][1],
    }
    small_shards = [w[k] for k in SMALL_SHARDED]
    small_buf, small_offs = _pack(small_shards)
    cols = [k in COL_ALIGNED for k in BIG]
    chip_arr = chip.reshape(1).astype(jnp.int32)
    placed = [_place(big_shards[k], col, chip_arr, BF16, name=f"place_{k}") for k, col in zip(BIG, cols)]
    placed.append(_place(small_buf, False, chip_arr, F32, name="place_small"))
    gathered = _all_gather(placed, cols + [False], name="gather_weights")
    full = {}
    for k, arr in zip(BIG, gathered[:-1]):
        if k in COL_ALIGNED:
            full[k] = arr.reshape(arr.shape[0] * arr.shape[1], arr.shape[2])
        elif k in COL_SHARDED:
            full[k] = _cols_from_shards(arr.reshape((N_CHIPS, arr.shape[1] * arr.shape[2], arr.shape[3])))
        else:
            full[k] = arr.reshape(-1, arr.shape[3])
    small_all = gathered[-1].reshape(N_CHIPS, -1, LANES)
    per_chip = [_unpack(small_all[k], small_offs, [a.shape for a in small_shards]) for k in range(N_CHIPS)]
    lru_conv_w = jnp.concatenate([per_chip[k][0] for k in range(N_CHIPS)], axis=-1)[0]
    sconv_w = jnp.concatenate([per_chip[k][1] for k in range(N_CHIPS)], axis=-1)[0]
    ffn_conv_w = jnp.concatenate([per_chip[k][2] for k in range(N_CHIPS)], axis=-1)
    mix1_norm = jnp.concatenate([per_chip[k][3] for k in range(N_CHIPS)], axis=-1)
    sgu_norm = jnp.concatenate([per_chip[k][4] for k in range(N_CHIPS)], axis=-1)

    nat = {
        "mix0_w_in": full["mix0_w_in"], "mix0_w_out": full["mix0_w_out"],
        "mix1_w_in": full["mix1_w_in"], "mix1_w_out": full["mix1_w_out"],
        "ffn_up": [full["ffn_up0"], full["ffn_up1"]], "ffn_down": [full["ffn_down0"], full["ffn_down1"]],
        "lru_conv_w": lru_conv_w, "sconv_w": sconv_w, "ffn_conv_w": ffn_conv_w, "mix1_norm": mix1_norm,
        "sgu_norm": sgu_norm,
    }
    for k in SMALL_REPLICATED:
        nat[k] = w[k]
    wts = _prepare_weights(nat)
    loss, grad_x, g = _local_step(x2, t2, wts)

    grads_small, grads_big = _natural_grads(g)

    small_names = SMALL_REPLICATED + SMALL_SHARDED
    small_list = [grads_small[k] for k in small_names] + [loss[:, :1]]
    sbuf, soffs = _pack(small_list)
    sred = _all_reduce_small(sbuf, name="reduce_small")
    small_red = _unpack(sred, soffs, [a.shape for a in small_list])
    loss_total = small_red[-1][0, 0]
    gsum = dict(zip(small_names, small_red[:-1]))
    for k in SMALL_SHARDED:
        width = w[k].shape[-1]
        gsum[k] = lax.dynamic_slice_in_dim(gsum[k], chip * width, width, axis=gsum[k].ndim - 1)

    def grad_view(k):
        a = grads_big[k]
        if k in COL_ALIGNED:
            return a.reshape(2, a.shape[0] // 2, a.shape[1])
        if k in COL_SHARDED:
            a = _cols_to_shards(a)
            return a.reshape(N_CHIPS, 2, a.shape[1] // 2, a.shape[2])
        rows = a.shape[0] // (2 * N_CHIPS)
        return a.reshape(N_CHIPS, 2, rows, a.shape[1])

    gviews = [grad_view(k) for k in BIG]
    xi, yi = lax.axis_index("x"), lax.axis_index("y")
    core_arr = core.reshape(1).astype(jnp.int32)
    keep_arr = jnp.stack([core * (2 * xi + t) + (1 - core) * (2 * t + yi) for t in range(2)]).astype(jnp.int32)
    mine_arr = jnp.stack([core * yi + (1 - core) * xi, core]).astype(jnp.int32)
    from_sib = _send_other_half(gviews, cols, name="rs_pair")
    pair = [_pair_sum(a, col, b, core_arr, name=f"rs_pair_sum_{k}")
            for k, a, col, b in zip(BIG, gviews, cols, from_sib)]
    first_in = _send_first([p16 for _, p16 in pair], name="rs_first")
    first = [_first_sum(p32, r1, keep_arr, name=f"rs_first_sum_{k}") for k, (p32, _), r1 in zip(BIG, pair, first_in)]
    second_in = _send_second([s16 for _, s16 in first], name="rs_second")
    mine = [_second_sum(s32, r2, mine_arr, name=f"rs_second_sum_{k}")
            for k, (s32, _), r2 in zip(BIG, first, second_in)]
    both = _swap_halves(mine, name="rs_swap")
    gbig = {k: a.reshape((a.shape[0] * a.shape[1],) + a.shape[2:]) for k, a in zip(BIG, both)}

    out_g, out_d, out_m, out_v = {}, {}, {}, {}
    small_w = [w[k] for k in small_names]
    pg, offs = _pack([gsum[k] for k in small_names])
    pw, _ = _pack(small_w)
    pm, _ = _pack([m[k] for k in small_names])
    pv, _ = _pack([v[k] for k in small_names])
    sd, sm, sv = _adamw(pw, pg, pm, pv, name="adamw_small")
    shapes = [a.shape for a in small_w]
    for k, dd, mm, vv in zip(small_names, _unpack(sd, offs, shapes), _unpack(sm, offs, shapes),
                             _unpack(sv, offs, shapes)):
        out_g[k], out_d[k], out_m[k], out_v[k] = gsum[k].reshape(w[k].shape), dd, mm, vv

    def big_adam(name, wk, mk, vk, gk):
        shp = wk.shape
        w2, m2, v2 = (a.reshape(gk.shape) for a in (wk, mk, vk))
        d, nm, nv = _adamw(w2, gk, m2, v2, name=f"adamw_{name}")
        return gk.reshape(shp), d.reshape(shp), nm.reshape(shp), nv.reshape(shp)

    for k in ("mix0_w_in", "mix0_w_out", "mix1_w_in", "mix1_w_out"):
        out_g[k], out_d[k], out_m[k], out_v[k] = big_adam(k, w[k][0], m[k][0], v[k][0], gbig[k])
        out_g[k], out_d[k], out_m[k], out_v[k] = (a[None] for a in (out_g[k], out_d[k], out_m[k], out_v[k]))
    for k in ("ffn_up", "ffn_down"):
        res = [big_adam(f"{k}{l}", w[k][l], m[k][l], v[k][l], gbig[f"{k}{l}"]) for l in range(2)]
        out_g[k], out_d[k], out_m[k], out_v[k] = (jnp.stack([res[0][i], res[1][i]]) for i in range(4))

    outs = [loss_total, grad_x[None]]
    for d in (out_g, out_d, out_m, out_v):
        outs.extend(d[k] for k in WEIGHT_ORDER)
    return tuple(outs)


def kernel(x, mix0_norm, mix0_w_in, lru_conv_w, lru_conv_b, lru_wa, lru_ba, lru_wx, lru_bx, lru_lambda, sconv_w, sconv_b, mix0_w_out, mix1_norm, mix1_w_in, sgu_norm, sgu_w, sgu_b, fox_bf, mix1_w_out, ffn_norm, ffn_up, ffn_conv_w, ffn_conv_b, ffn_down, final_norm, loss_target, m_mix0_norm, m_mix0_w_in, m_lru_conv_w, m_lru_conv_b, m_lru_wa, m_lru_ba, m_lru_wx, m_lru_bx, m_lru_lambda, m_sconv_w, m_sconv_b, m_mix0_w_out, m_mix1_norm, m_mix1_w_in, m_sgu_norm, m_sgu_w, m_sgu_b, m_fox_bf, m_mix1_w_out, m_ffn_norm, m_ffn_up, m_ffn_conv_w, m_ffn_conv_b, m_ffn_down, m_final_norm, v_mix0_norm, v_mix0_w_in, v_lru_conv_w, v_lru_conv_b, v_lru_wa, v_lru_ba, v_lru_wx, v_lru_bx, v_lru_lambda, v_sconv_w, v_sconv_b, v_mix0_w_out, v_mix1_norm, v_mix1_w_in, v_sgu_norm, v_sgu_w, v_sgu_b, v_fox_bf, v_mix1_w_out, v_ffn_norm, v_ffn_up, v_ffn_conv_w, v_ffn_conv_b, v_ffn_down, v_final_norm):
    w = dict(zip(WEIGHT_ORDER, (mix0_norm, mix0_w_in, lru_conv_w, lru_conv_b, lru_wa, lru_ba, lru_wx, lru_bx, lru_lambda, sconv_w, sconv_b, mix0_w_out, mix1_norm, mix1_w_in, sgu_norm, sgu_w, sgu_b, fox_bf, mix1_w_out, ffn_norm, ffn_up, ffn_conv_w, ffn_conv_b, ffn_down, final_norm)))
    m = dict(zip(WEIGHT_ORDER, (m_mix0_norm, m_mix0_w_in, m_lru_conv_w, m_lru_conv_b, m_lru_wa, m_lru_ba, m_lru_wx, m_lru_bx, m_lru_lambda, m_sconv_w, m_sconv_b, m_mix0_w_out, m_mix1_norm, m_mix1_w_in, m_sgu_norm, m_sgu_w, m_sgu_b, m_fox_bf, m_mix1_w_out, m_ffn_norm, m_ffn_up, m_ffn_conv_w, m_ffn_conv_b, m_ffn_down, m_final_norm)))
    v = dict(zip(WEIGHT_ORDER, (v_mix0_norm, v_mix0_w_in, v_lru_conv_w, v_lru_conv_b, v_lru_wa, v_lru_ba, v_lru_wx, v_lru_bx, v_lru_lambda, v_sconv_w, v_sconv_b, v_mix0_w_out, v_mix1_norm, v_mix1_w_in, v_sgu_norm, v_sgu_w, v_sgu_b, v_fox_bf, v_mix1_w_out, v_ffn_norm, v_ffn_up, v_ffn_conv_w, v_ffn_conv_b, v_ffn_down, v_final_norm)))
    return _train_step(x, loss_target, w, m, v)
```

```python
import functools

import jax
import jax.numpy as jnp
from jax import lax
from jax.experimental import pallas as pl
from jax.experimental.pallas import tpu as pltpu
from jax.experimental.pallas import tpu_sc as plsc

F32 = jnp.float32
BF16 = jnp.bfloat16
MESH = pl.DeviceIdType.MESH

D_MODEL = 1024
LANES = 128
SUBLANES = 8
N_CHIPS = 4
EPS = 1e-6
LRU_C = 8.0
D_FF = 2816
FFN_CB = 256
CHUNK = 128
NEG = -1e30

ADAM_LR = 0.001
ADAM_B1 = 0.9
ADAM_B2 = 0.999
ADAM_EPS = 1e-08
ADAM_WD = 0.01
ADAM_STEP = 10
ADAM_C1 = 1.0 - ADAM_B1 ** ADAM_STEP
ADAM_C2 = 1.0 - ADAM_B2 ** ADAM_STEP

_GELU_C = 0.7978845608028654
_GELU_A = 0.044715


def _sigmoid(x):
    return 1.0 / (1.0 + jnp.exp(-x))


def _log1p_pos(e):
    w = 1.0 + e
    return jnp.where(w == 1.0, e, jnp.log(w) * (e / (w - 1.0)))


def _softplus(x):
    return jnp.maximum(x, 0.0) + _log1p_pos(jnp.exp(-jnp.abs(x)))


def _gelu(x):
    t = jnp.tanh(_GELU_C * (x + _GELU_A * (x * x * x)))
    return 0.5 * x * (1.0 + t), t


def _gelu_grad(x, t):
    return 0.5 * (1.0 + t) + 0.5 * x * (1.0 - t * t) * (_GELU_C * (1.0 + 3.0 * _GELU_A * x * x))


def _rows(shape):
    return lax.broadcasted_iota(jnp.int32, shape, 0)


def _lanes(shape):
    return lax.broadcasted_iota(jnp.int32, shape, 1)


def _shift_down(x, halo8, j):
    if j == 0:
        return x
    r = pltpu.roll(x, j, 0)
    hr = pltpu.roll(halo8, j, 0)
    top = jnp.where(_rows(hr.shape) < j, hr, r[:SUBLANES])
    return jnp.concatenate([top, r[SUBLANES:]], axis=0)


def _shift_up(x, next8, j):
    if j == 0:
        return x
    n = x.shape[0]
    r = pltpu.roll(x, n - j, 0)
    nr = pltpu.roll(next8, SUBLANES - j, 0)
    bot = jnp.where(_rows(nr.shape) >= SUBLANES - j, nr, r[n - SUBLANES:])
    return jnp.concatenate([r[:n - SUBLANES], bot], axis=0)


def _scan_fwd(a, u):
    n = a.shape[0]
    row = _rows(a.shape)
    h = u
    k = 1
    while k < n:
        keep = row >= k
        h_sh = jnp.where(keep, pltpu.roll(h, k, 0), 0.0)
        a_sh = jnp.where(keep, pltpu.roll(a, k, 0), 1.0)
        h = a * h_sh + h
        a = a * a_sh
        k *= 2
    return h, a


def _scan_rev(b, d):
    n = b.shape[0]
    row = _rows(b.shape)
    g = d
    k = 1
    while k < n:
        keep = row < n - k
        g_sh = jnp.where(keep, pltpu.roll(g, n - k, 0), 0.0)
        b_sh = jnp.where(keep, pltpu.roll(b, n - k, 0), 1.0)
        g = b * g_sh + g
        b = b * b_sh
        k *= 2
    return g, b


def _cumsum_fwd(x):
    n = x.shape[0]
    row = _rows(x.shape)
    k = 1
    while k < n:
        x = x + jnp.where(row >= k, pltpu.roll(x, k, 0), 0.0)
        k *= 2
    return x


def _cumsum_rev(x):
    n = x.shape[0]
    row = _rows(x.shape)
    k = 1
    while k < n:
        x = x + jnp.where(row < n - k, pltpu.roll(x, n - k, 0), 0.0)
        k *= 2
    return x


def _dot(a, b):
    return lax.dot_general(a, b, (((1,), (0,)), ((), ())), preferred_element_type=F32)


def _dot_nt(a, b):
    return lax.dot_general(a, b, (((1,), (1,)), ((), ())), preferred_element_type=F32)


def _dot_tn(a, b):
    return lax.dot_general(a, b, (((0,), (0,)), ((), ())), preferred_element_type=F32)


def _dot_split(x, m_bf16):
    hi = x.astype(BF16)
    lo = (x - hi.astype(F32)).astype(BF16)
    return _dot(hi, m_bf16) + _dot(lo, m_bf16)


def _tile_rows(ts, s):
    return min(ts, s)


def _mm(a_list, w, *, trans_w=False, res=None, out_dtype=F32, ts=512, nb=None, name):
    s = a_list[0].shape[0]
    ks = [a.shape[1] for a in a_list]
    k = sum(ks)
    n = w.shape[0] if trans_w else w.shape[1]
    ts = _tile_rows(ts, s)
    nb = n if nb is None else nb
    na = len(a_list)
    has_res = res is not None

    def body(*refs):
        a_refs = refs[:na]
        w_ref = refs[na]
        o_ref = refs[-1]
        parts = [r[...].astype(BF16) for r in a_refs]
        a = parts[0] if na == 1 else jnp.concatenate(parts, axis=1)
        acc = _dot_nt(a, w_ref[...]) if trans_w else _dot(a, w_ref[...])
        if has_res:
            acc = acc + refs[na + 1][...]
        o_ref[...] = acc.astype(out_dtype)

    in_specs = [pl.BlockSpec((ts, kk), lambda j, i: (i, 0)) for kk in ks]
    if trans_w:
        in_specs.append(pl.BlockSpec((nb, k), lambda j, i: (j, 0)))
    else:
        in_specs.append(pl.BlockSpec((k, nb), lambda j, i: (0, j)))
    args = list(a_list) + [w]
    if has_res:
        in_specs.append(pl.BlockSpec((ts, nb), lambda j, i: (i, j)))
        args.append(res)
    return pl.pallas_call(
        body, name=name, grid=(n // nb, s // ts), in_specs=in_specs,
        out_specs=pl.BlockSpec((ts, nb), lambda j, i: (i, j)),
        out_shape=jax.ShapeDtypeStruct((s, n), out_dtype),
    )(*args)


def _mm_tn(a_list, b_list, *, ts=512, nb=None, name):
    s = b_list[0].shape[0]
    ks = [a.shape[1] for a in a_list]
    k = sum(ks)
    width = b_list[0].shape[1]
    n = width * len(b_list)
    ts = _tile_rows(ts, s)
    nb = width if nb is None else nb
    per = width // nb
    na = len(a_list)
    nparts = len(b_list)

    def body(*refs):
        a_refs = refs[:na]
        b_refs = refs[na:na + nparts]
        o_ref = refs[-1]
        j = pl.program_id(0)
        i = pl.program_id(1)
        parts = [r[...].astype(BF16) for r in a_refs]
        a = parts[0] if na == 1 else jnp.concatenate(parts, axis=1)

        def accumulate(b_ref):
            upd = _dot_tn(a, b_ref[...].astype(BF16))

            @pl.when(i == 0)
            def _():
                o_ref[...] = upd

            @pl.when(i > 0)
            def _():
                o_ref[...] += upd

        if nparts == 1:
            accumulate(b_refs[0])
        else:
            for part, b_ref in enumerate(b_refs):
                pl.when(j // per == part)(functools.partial(accumulate, b_ref))

    in_specs = [pl.BlockSpec((ts, kk), lambda j, i: (i, 0)) for kk in ks]
    for part in range(nparts):
        in_specs.append(pl.BlockSpec(
            (ts, nb), lambda j, i, part=part: (i, jnp.clip(j - part * per, 0, per - 1))))
    return pl.pallas_call(
        body, name=name, grid=(n // nb, s // ts), in_specs=in_specs,
        out_specs=pl.BlockSpec((k, nb), lambda j, i: (0, j)),
        out_shape=jax.ShapeDtypeStruct((k, n), F32),
    )(*a_list, *b_list)


def _norm_fwd(h, g, *, ts=512, name):
    s, d = h.shape
    ts = _tile_rows(ts, s)

    def body(h_ref, g_ref, n_ref):
        x = h_ref[...]
        r = lax.rsqrt(jnp.mean(x * x, axis=-1, keepdims=True) + EPS)
        n_ref[...] = ((x * r) * g_ref[...]).astype(BF16)

    return pl.pallas_call(
        body, name=name, grid=(s // ts,),
        in_specs=[pl.BlockSpec((ts, d), lambda i: (i, 0)), pl.BlockSpec((1, d), lambda i: (0, 0))],
        out_specs=pl.BlockSpec((ts, d), lambda i: (i, 0)),
        out_shape=jax.ShapeDtypeStruct((s, d), BF16),
    )(h, g)


def _norm_bwd(dn, h, g, dres, *, ts=512, name):
    s, d = h.shape
    ts = _tile_rows(ts, s)

    def body(dn_ref, h_ref, g_ref, dres_ref, dh_ref, dg_ref):
        i = pl.program_id(0)
        x = h_ref[...]
        dnv = dn_ref[...]
        r = lax.rsqrt(jnp.mean(x * x, axis=-1, keepdims=True) + EPS)
        xhat = x * r
        part = jnp.sum(dnv * xhat, axis=0, keepdims=True)

        @pl.when(i == 0)
        def _():
            dg_ref[...] = part

        @pl.when(i > 0)
        def _():
            dg_ref[...] += part

        dxh = dnv * g_ref[...]
        dh_ref[...] = dres_ref[...] + r * (dxh - xhat * jnp.mean(dxh * xhat, axis=-1, keepdims=True))

    tile = pl.BlockSpec((ts, d), lambda i: (i, 0))
    vec = pl.BlockSpec((1, d), lambda i: (0, 0))
    return pl.pallas_call(
        body, name=name, grid=(s // ts,), in_specs=[tile, tile, vec, tile],
        out_specs=(tile, vec),
        out_shape=(jax.ShapeDtypeStruct((s, d), F32), jax.ShapeDtypeStruct((1, d), F32)),
    )(dn, h, g, dres)


def _final(h, g, target, *, ts=512, name):
    s, d = h.shape
    ts = _tile_rows(ts, s)
    nt = s // ts

    def body(h_ref, g_ref, t_ref, dh_ref, loss_ref, dg_ref, acc_ref):
        i = pl.program_id(0)
        x = h_ref[...]
        r = lax.rsqrt(jnp.mean(x * x, axis=-1, keepdims=True) + EPS)
        xhat = x * r
        gv = g_ref[...]
        err = xhat * gv - t_ref[...]
        sq = jnp.sum(err * err, axis=0, keepdims=True)
        dy = err * (1.0 / d)
        part = jnp.sum(dy * xhat, axis=0, keepdims=True)

        @pl.when(i == 0)
        def _():
            acc_ref[...] = sq
            dg_ref[...] = part

        @pl.when(i > 0)
        def _():
            acc_ref[...] += sq
            dg_ref[...] += part

        dxh = dy * gv
        dh_ref[...] = r * (dxh - xhat * jnp.mean(dxh * xhat, axis=-1, keepdims=True))

        @pl.when(i == nt - 1)
        def _():
            tot = jnp.sum(acc_ref[...], axis=1, keepdims=True) * (0.5 / d)
            loss_ref[...] = jnp.broadcast_to(tot, (1, LANES))

    tile = pl.BlockSpec((ts, d), lambda i: (i, 0))
    vec = pl.BlockSpec((1, d), lambda i: (0, 0))
    return pl.pallas_call(
        body, name=name, grid=(nt,), in_specs=[tile, vec, tile],
        out_specs=(tile, pl.BlockSpec((1, LANES), lambda i: (0, 0)), vec),
        out_shape=(jax.ShapeDtypeStruct((s, d), F32), jax.ShapeDtypeStruct((1, LANES), F32),
                   jax.ShapeDtypeStruct((1, d), F32)),
        scratch_shapes=[pltpu.VMEM((1, d), F32)],
    )(h, g, target)


def _halo_map(ts, width_blocks):
    per = ts // SUBLANES

    def index(j, i):
        return (jnp.maximum(i * per - 1, 0), width_blocks(j))

    return index


def _even_gates(xc, wa, ba, wx, bx, sp):
    xb = xc.astype(BF16)
    r = _sigmoid(_dot(xb, wa) + ba)
    ig = _sigmoid(_dot(xb, wx) + bx)
    la = (-LRU_C) * r * sp
    a = jnp.exp(la)
    a2 = a * a
    m = jnp.sqrt(-jnp.tanh(la) * (1.0 + a2))
    return r, ig, la, a, a2, m


def _even_core_fwd(p, w4, b4, wa, ba, wx, bx, lam, w3, b3, *, ts=512, name):
    s = p.shape[0]
    ts = _tile_rows(ts, s)
    nt = s // ts
    nblk = 4

    def body(p_ref, ph_ref, w4_ref, b4_ref, wa_ref, ba_ref, wx_ref, bx_ref, lam_ref, w3_ref, b3_ref,
             ya_ref, yb_ref, hl_ref, hcar_ref):
        i = pl.program_id(1)
        first = (i > 0).astype(F32)
        xa = p_ref[:, 0:LANES]
        ga = p_ref[:, LANES:2 * LANES]
        cp = p_ref[:, 2 * LANES:3 * LANES]
        bp = p_ref[:, 3 * LANES:4 * LANES]
        vb = p_ref[:, 4 * LANES:5 * LANES]
        xa_h = ph_ref[:, 0:LANES] * first
        s_h = ph_ref[:, 2 * LANES:3 * LANES] * ph_ref[:, 4 * LANES:5 * LANES] * first

        xc = b4_ref[...] + w4_ref[3:4, :] * xa
        for k in range(3):
            xc = xc + w4_ref[k:k + 1, :] * _shift_down(xa, xa_h, 3 - k)
        sp = _softplus(-lam_ref[...])
        _, ig, _, a, _, m = _even_gates(xc, wa_ref[0], ba_ref[...], wx_ref[0], bx_ref[...], sp)
        u = m * (ig * xc)
        hs, acum = _scan_fwd(a, u)

        @pl.when(i == 0)
        def _():
            hcar_ref[...] = jnp.zeros_like(hcar_ref)

        hs = hs + acum * hcar_ref[0:1, :]
        hl_ref[...] = hs
        hcar_ref[0:1, :] = hl_ref[ts - 1:ts, :]
        ge, _ = _gelu(ga)
        ya_ref[...] = (hs * ge).astype(BF16)

        sv = cp * vb
        sc = b3_ref[...] + w3_ref[2:3, :] * sv
        for k in range(2):
            sc = sc + w3_ref[k:k + 1, :] * _shift_down(sv, s_h, 2 - k)
        yb_ref[...] = (bp * sc).astype(BF16)

    blk = pl.BlockSpec((ts, 5 * LANES), lambda j, i: (i, j))
    halo = pl.BlockSpec((SUBLANES, 5 * LANES), _halo_map(ts, lambda j: j))
    vec = pl.BlockSpec((1, LANES), lambda j, i: (0, j))
    out = pl.BlockSpec((ts, LANES), lambda j, i: (i, j))
    return pl.pallas_call(
        body, name=name, grid=(nblk, nt),
        in_specs=[blk, halo,
                  pl.BlockSpec((4, LANES), lambda j, i: (0, j)), vec,
                  pl.BlockSpec((1, LANES, LANES), lambda j, i: (j, 0, 0)), vec,
                  pl.BlockSpec((1, LANES, LANES), lambda j, i: (j, 0, 0)), vec, vec,
                  pl.BlockSpec((3, LANES), lambda j, i: (0, j)), vec],
        out_specs=(out, out, out),
        out_shape=(jax.ShapeDtypeStruct((s, 4 * LANES), BF16), jax.ShapeDtypeStruct((s, 4 * LANES), BF16),
                   jax.ShapeDtypeStruct((s, 4 * LANES), F32)),
        scratch_shapes=[pltpu.VMEM((SUBLANES, LANES), F32)],
    )(p, p, w4, b4, wa, ba, wx, bx, lam, w3, b3)


def _even_core_bwd(dy, p, hl, w4, b4, wa, wat, ba, wx, wxt, bx, lam, w3, b3, *, ts=256, name):
    s = p.shape[0]
    ts = _tile_rows(ts, s)
    nt = s // ts
    nblk = 4
    per = ts // SUBLANES

    def body(dya_ref, dyb_ref, p_ref, ph_ref, hl_ref, hh_ref,
             w4_ref, b4_ref, wa_ref, wat_ref, ba_ref, wx_ref, wxt_ref, bx_ref, lam_ref, w3_ref, b3_ref,
             dp_ref, dw4_ref, db4_ref, dwa_ref, dba_ref, dwx_ref, dbx_ref, dlam_ref, dw3_ref, db3_ref,
             dxc_nx, dsc_nx, cg_ref):
        i = pl.program_id(1)
        ti = nt - 1 - i
        first = (ti > 0).astype(F32)
        xa = p_ref[:, 0:LANES]
        ga = p_ref[:, LANES:2 * LANES]
        cp = p_ref[:, 2 * LANES:3 * LANES]
        bp = p_ref[:, 3 * LANES:4 * LANES]
        vb = p_ref[:, 4 * LANES:5 * LANES]
        xa_h = ph_ref[:, 0:LANES] * first
        s_h = ph_ref[:, 2 * LANES:3 * LANES] * ph_ref[:, 4 * LANES:5 * LANES] * first
        h_h = hh_ref[...] * first

        @pl.when(i == 0)
        def _():
            dxc_nx[...] = jnp.zeros_like(dxc_nx)
            dsc_nx[...] = jnp.zeros_like(dsc_nx)
            cg_ref[...] = jnp.zeros_like(cg_ref)
            for ref in (dw4_ref, db4_ref, dwa_ref, dba_ref, dwx_ref, dbx_ref, dlam_ref, dw3_ref, db3_ref):
                ref[...] = jnp.zeros_like(ref)

        xa_sh = [_shift_down(xa, xa_h, 3 - k) for k in range(3)] + [xa]
        xc = b4_ref[...]
        for k in range(4):
            xc = xc + w4_ref[k:k + 1, :] * xa_sh[k]
        lamv = lam_ref[...]
        sp = _softplus(-lamv)
        r, ig, _, a, a2, m = _even_gates(xc, wa_ref[0], ba_ref[...], wx_ref[0], bx_ref[...], sp)
        sv = cp * vb
        sv_sh = [_shift_down(sv, s_h, 2 - k) for k in range(2)] + [sv]
        sc = b3_ref[...]
        for k in range(3):
            sc = sc + w3_ref[k:k + 1, :] * sv_sh[k]
        hs = hl_ref[...]
        h_prev = _shift_down(hs, h_h, 1)

        dya = dya_ref[...]
        dyb = dyb_ref[...]
        ge, gt = _gelu(ga)
        dga = dya * hs * _gelu_grad(ga, gt)
        dh = dya * ge

        ones8 = jnp.ones((SUBLANES, LANES), F32)
        b = _shift_up(a, ones8, 1)
        g, bcum = _scan_rev(b, dh)
        g = g + bcum * cg_ref[0:1, :]
        ag = a * g
        cg_ref[...] = ag[:SUBLANES]

        da = g * h_prev
        xi = ig * xc
        dm = g * xi
        dig = g * m * xc
        dxc = g * m * ig
        dla = da * a - dm * (a2 / m)
        dr = dla * ((-LRU_C) * sp)
        dlam_ref[...] += jnp.sum(dla * r, axis=0, keepdims=True) * (LRU_C * _sigmoid(-lamv))
        dra = dr * r * (1.0 - r)
        dia = dig * ig * (1.0 - ig)
        drab = dra.astype(BF16)
        diab = dia.astype(BF16)
        xcb = xc.astype(BF16)
        dxc = dxc + _dot(drab, wat_ref[0]) + _dot(diab, wxt_ref[0])
        dwa_ref[0] += _dot_tn(xcb, drab)
        dwx_ref[0] += _dot_tn(xcb, diab)
        dba_ref[...] += jnp.sum(dra, axis=0, keepdims=True)
        dbx_ref[...] += jnp.sum(dia, axis=0, keepdims=True)

        nx = dxc_nx[...]
        dxa = w4_ref[3:4, :] * dxc
        for k in range(3):
            dxa = dxa + w4_ref[k:k + 1, :] * _shift_up(dxc, nx, 3 - k)
        for k in range(4):
            dw4_ref[k:k + 1, :] += jnp.sum(dxc * xa_sh[k], axis=0, keepdims=True)
        db4_ref[...] += jnp.sum(dxc, axis=0, keepdims=True)
        dxc_nx[...] = dxc[:SUBLANES]

        dbp = dyb * sc
        dsc = dyb * bp
        nsc = dsc_nx[...]
        ds = w3_ref[2:3, :] * dsc
        for k in range(2):
            ds = ds + w3_ref[k:k + 1, :] * _shift_up(dsc, nsc, 2 - k)
        for k in range(3):
            dw3_ref[k:k + 1, :] += jnp.sum(dsc * sv_sh[k], axis=0, keepdims=True)
        db3_ref[...] += jnp.sum(dsc, axis=0, keepdims=True)
        dsc_nx[...] = dsc[:SUBLANES]

        dp_ref[:, 0:LANES] = dxa.astype(BF16)
        dp_ref[:, LANES:2 * LANES] = dga.astype(BF16)
        dp_ref[:, 2 * LANES:3 * LANES] = (ds * vb).astype(BF16)
        dp_ref[:, 3 * LANES:4 * LANES] = dbp.astype(BF16)
        dp_ref[:, 4 * LANES:5 * LANES] = (ds * cp).astype(BF16)

    def rev(j, i):
        return (nt - 1 - i, j)

    def rev_halo(col):
        def index(j, i):
            return (jnp.maximum((nt - 1 - i) * per - 1, 0), col(j))
        return index

    blk = pl.BlockSpec((ts, 5 * LANES), rev)
    one = pl.BlockSpec((ts, LANES), rev)
    vec = pl.BlockSpec((1, LANES), lambda j, i: (0, j))
    mat = pl.BlockSpec((1, LANES, LANES), lambda j, i: (j, 0, 0))
    w4s = pl.BlockSpec((4, LANES), lambda j, i: (0, j))
    w3s = pl.BlockSpec((3, LANES), lambda j, i: (0, j))
    f = jax.ShapeDtypeStruct
    return pl.pallas_call(
        body, name=name, grid=(nblk, nt),
        in_specs=[one, pl.BlockSpec((ts, LANES), lambda j, i: (nt - 1 - i, 4 + j)),
                  blk, pl.BlockSpec((SUBLANES, 5 * LANES), rev_halo(lambda j: j)),
                  one, pl.BlockSpec((SUBLANES, LANES), rev_halo(lambda j: j)),
                  w4s, vec, mat, mat, vec, mat, mat, vec, vec, w3s, vec],
        out_specs=(blk, w4s, vec, mat, vec, mat, vec, vec, w3s, vec),
        out_shape=(f((s, 20 * LANES), BF16), f((4, 4 * LANES), F32), f((1, 4 * LANES), F32),
                   f((4, LANES, LANES), F32), f((1, 4 * LANES), F32),
                   f((4, LANES, LANES), F32), f((1, 4 * LANES), F32), f((1, 4 * LANES), F32),
                   f((3, 4 * LANES), F32), f((1, 4 * LANES), F32)),
        scratch_shapes=[pltpu.VMEM((SUBLANES, LANES), F32), pltpu.VMEM((SUBLANES, LANES), F32),
                        pltpu.VMEM((SUBLANES, LANES), F32)],
    )(dy, dy, p, p, hl, hl, w4, b4, wa, wat, ba, wx, wxt, bx, lam, w3, b3)


def _ffn_conv(u_ref, uh_ref, w_ref, b_ref, first):
    u = u_ref[...].astype(F32)
    u_h = uh_ref[...].astype(F32)[SUBLANES:] * first
    u_sh = [_shift_down(u, u_h, 2 - k) for k in range(2)] + [u]
    hc = b_ref[...]
    for k in range(3):
        hc = hc + w_ref[k:k + 1, :] * u_sh[k]
    return hc, u_sh


def _ffn_specs(ts, row, halo_row):
    nblk = D_FF // FFN_CB
    specs = []
    for off in (0, nblk):
        specs.append(pl.BlockSpec((ts, FFN_CB), lambda j, i, off=off: (row(i), off + j)))
        specs.append(pl.BlockSpec((16, FFN_CB), lambda j, i, off=off: (halo_row(i), off + j)))
        specs.append(pl.BlockSpec((3, FFN_CB), lambda j, i, off=off: (0, off + j)))
        specs.append(pl.BlockSpec((1, FFN_CB), lambda j, i, off=off: (0, off + j)))
    return specs


def _ffn_core_fwd(up, w, b, *, ts=512, name):
    s = up.shape[0]
    ts = _tile_rows(ts, s)
    nt = s // ts
    nblk = D_FF // FFN_CB
    per = ts // 16

    def body(g_ref, gh_ref, wg_ref, bg_ref, v_ref, vh_ref, wv_ref, bv_ref, act_ref):
        first = (pl.program_id(1) > 0).astype(F32)
        gate, _ = _ffn_conv(g_ref, gh_ref, wg_ref, bg_ref, first)
        val, _ = _ffn_conv(v_ref, vh_ref, wv_ref, bv_ref, first)
        act_ref[...] = (gate * _sigmoid(gate) * val).astype(BF16)

    return pl.pallas_call(
        body, name=name, grid=(nblk, nt),
        in_specs=_ffn_specs(ts, lambda i: i, lambda i: jnp.maximum(i * per - 1, 0)),
        out_specs=pl.BlockSpec((ts, FFN_CB), lambda j, i: (i, j)),
        out_shape=jax.ShapeDtypeStruct((s, D_FF), BF16),
    )(up, up, w, b, up, up, w, b)


def _ffn_core_bwd(dact, up, w, b, *, ts=512, name):
    s = up.shape[0]
    ts = _tile_rows(ts, s)
    nt = s // ts
    nblk = D_FF // FFN_CB
    per = ts // 16

    def conv_bwd(dhc, u_sh, w_ref, nx_ref, du_ref, dw_ref, db_ref):
        nx = nx_ref[...]
        du = w_ref[2:3, :] * dhc
        for k in range(2):
            du = du + w_ref[k:k + 1, :] * _shift_up(dhc, nx, 2 - k)
        du_ref[...] = du.astype(BF16)
        for k in range(3):
            dw_ref[k:k + 1, :] += jnp.sum(dhc * u_sh[k], axis=0, keepdims=True)
        db_ref[...] += jnp.sum(dhc, axis=0, keepdims=True)
        nx_ref[...] = dhc[:SUBLANES]

    def body(da_ref, g_ref, gh_ref, wg_ref, bg_ref, v_ref, vh_ref, wv_ref, bv_ref,
             dg_ref, dv_ref, dwg_ref, dwv_ref, dbg_ref, dbv_ref, nxg_ref, nxv_ref):
        i = pl.program_id(1)
        first = (nt - 1 - i > 0).astype(F32)
        gate, g_sh = _ffn_conv(g_ref, gh_ref, wg_ref, bg_ref, first)
        val, v_sh = _ffn_conv(v_ref, vh_ref, wv_ref, bv_ref, first)
        da = da_ref[...].astype(F32)
        sg = _sigmoid(gate)
        dgate = da * val * (sg * (1.0 + gate * (1.0 - sg)))
        dval = da * (gate * sg)

        @pl.when(i == 0)
        def _():
            for ref in (nxg_ref, nxv_ref, dwg_ref, dwv_ref, dbg_ref, dbv_ref):
                ref[...] = jnp.zeros_like(ref)

        conv_bwd(dgate, g_sh, wg_ref, nxg_ref, dg_ref, dwg_ref, dbg_ref)
        conv_bwd(dval, v_sh, wv_ref, nxv_ref, dv_ref, dwv_ref, dbv_ref)

    def rev(i):
        return nt - 1 - i

    tile = pl.BlockSpec((ts, FFN_CB), lambda j, i: (rev(i), j))
    w_out = pl.BlockSpec((3, FFN_CB), lambda j, i: (0, j))
    b_out = pl.BlockSpec((1, FFN_CB), lambda j, i: (0, j))
    f = jax.ShapeDtypeStruct
    return pl.pallas_call(
        body, name=name, grid=(nblk, nt),
        in_specs=[tile] + _ffn_specs(ts, rev, lambda i: jnp.maximum(rev(i) * per - 1, 0)),
        out_specs=(tile, tile, w_out, w_out, b_out, b_out),
        out_shape=(f((s, D_FF), BF16), f((s, D_FF), BF16), f((3, D_FF), F32), f((3, D_FF), F32),
                   f((1, D_FF), F32), f((1, D_FF), F32)),
        scratch_shapes=[pltpu.VMEM((SUBLANES, FFN_CB), F32), pltpu.VMEM((SUBLANES, FFN_CB), F32)],
    )(dact, up, up, w, b, up, up, w, b)


def _sgu_forward_block(zu, zg, gn, w_ref, bias, seg):
    u, tu = _gelu(zu)
    g, tg = _gelu(zg)
    ms = _dot_split(g * g, seg)
    rs = lax.rsqrt(ms + EPS)
    ghat = g * rs
    gv = ghat * gn
    gvb = gv.astype(BF16)
    lane = _lanes((CHUNK, LANES))
    chunks = []
    for c in range(zu.shape[0] // CHUNK):
        gc = gvb[c * CHUNK:(c + 1) * CHUNK]
        mix = jnp.where(lane < 64, _dot(w_ref[0], gc), _dot(w_ref[1], gc)) + bias
        chunks.append(mix)
    mixed = chunks[0] if len(chunks) == 1 else jnp.concatenate(chunks, axis=0)
    return u, tu, g, tg, rs, ghat, gvb, mixed


def _sgu_fwd(p1, gn, w, bias, seg, *, ts=512, name):
    s = p1.shape[0]
    ts = _tile_rows(ts, s)

    def body(zu_ref, zg_ref, gn_ref, w_ref, bias_ref, seg_ref, yc_ref):
        u, _, _, _, _, _, _, mixed = _sgu_forward_block(
            zu_ref[...], zg_ref[...], gn_ref[...], w_ref, bias_ref[...], seg_ref[...])
        yc_ref[...] = (u * mixed).astype(BF16)

    return pl.pallas_call(
        body, name=name, grid=(4, s // ts),
        in_specs=[pl.BlockSpec((ts, LANES), lambda j, i: (i, j)),
                  pl.BlockSpec((ts, LANES), lambda j, i: (i, 4 + j)),
                  pl.BlockSpec((1, LANES), lambda j, i: (0, j)),
                  pl.BlockSpec((2, CHUNK, CHUNK), lambda j, i: (j, 0, 0)),
                  pl.BlockSpec((CHUNK, LANES), lambda j, i: (0, j)),
                  pl.BlockSpec((LANES, LANES), lambda j, i: (0, 0))],
        out_specs=pl.BlockSpec((ts, LANES), lambda j, i: (i, j)),
        out_shape=jax.ShapeDtypeStruct((s, 4 * LANES), BF16),
    )(p1, p1, gn, w, bias, seg)


def _sgu_bwd(p1, dy, gn, w, wt, bias, seg, tril, *, ts=512, name):
    s = p1.shape[0]
    ts = _tile_rows(ts, s)
    nt = s // ts

    def body(zu_ref, zg_ref, dy_ref, gn_ref, w_ref, wt_ref, bias_ref, seg_ref, tril_ref,
             dzu_ref, dzg_ref, dw_ref, dbias_ref, dgn_ref):
        i = pl.program_id(1)
        zu = zu_ref[...]
        zg = zg_ref[...]
        gn_v = gn_ref[...]
        segv = seg_ref[...]
        u, tu, g, tg, rs, ghat, gvb, mixed = _sgu_forward_block(zu, zg, gn_v, w_ref, bias_ref[...], segv)
        dyv = dy_ref[...]
        du = dyv * mixed
        dmx = dyv * u

        @pl.when(i == 0)
        def _():
            dw_ref[...] = jnp.zeros_like(dw_ref)
            dbias_ref[...] = jnp.zeros_like(dbias_ref)
            dgn_ref[...] = jnp.zeros_like(dgn_ref)

        lane = _lanes((CHUNK, LANES))
        dgv_chunks = []
        dbias = jnp.zeros((CHUNK, LANES), F32)
        for c in range(ts // CHUNK):
            dmc = dmx[c * CHUNK:(c + 1) * CHUNK]
            gc = gvb[c * CHUNK:(c + 1) * CHUNK]
            dm_a = jnp.where(lane < 64, dmc, 0.0).astype(BF16)
            dm_b = jnp.where(lane >= 64, dmc, 0.0).astype(BF16)
            dw_ref[0] += _dot_nt(dm_a, gc)
            dw_ref[1] += _dot_nt(dm_b, gc)
            dgv_chunks.append(_dot(wt_ref[0], dm_a) + _dot(wt_ref[1], dm_b))
            dbias = dbias + dmc
        dbias_ref[...] += dbias
        dgv = dgv_chunks[0] if len(dgv_chunks) == 1 else jnp.concatenate(dgv_chunks, axis=0)
        dgn_ref[...] += jnp.sum(dgv * ghat, axis=0, keepdims=True)
        dgh = dgv * gn_v
        dg = rs * (dgh - ghat * _dot_split(dgh * ghat, segv))
        dzu_ref[...] = (du * _gelu_grad(zu, tu)).astype(BF16)
        dzg_ref[...] = (dg * _gelu_grad(zg, tg)).astype(BF16)

        @pl.when(i == nt - 1)
        def _():
            dw_ref[0] = dw_ref[0] * tril_ref[...]
            dw_ref[1] = dw_ref[1] * tril_ref[...]

    f = jax.ShapeDtypeStruct
    colj = pl.BlockSpec((ts, LANES), lambda j, i: (i, j))
    wsp = pl.BlockSpec((2, CHUNK, CHUNK), lambda j, i: (j, 0, 0))
    sq = pl.BlockSpec((LANES, LANES), lambda j, i: (0, 0))
    return pl.pallas_call(
        body, name=name, grid=(4, nt),
        in_specs=[colj, pl.BlockSpec((ts, LANES), lambda j, i: (i, 4 + j)), colj,
                  pl.BlockSpec((1, LANES), lambda j, i: (0, j)), wsp, wsp,
                  pl.BlockSpec((CHUNK, LANES), lambda j, i: (0, j)), sq, sq],
        out_specs=(colj, colj, wsp, pl.BlockSpec((CHUNK, LANES), lambda j, i: (0, j)),
                   pl.BlockSpec((1, LANES), lambda j, i: (0, j))),
        out_shape=(f((s, 4 * LANES), BF16), f((s, 4 * LANES), BF16), f((8, CHUNK, CHUNK), F32),
                   f((CHUNK, 4 * LANES), F32), f((1, 4 * LANES), F32)),
    )(p1, p1, dy, gn, w, wt, bias, seg, tril)


F_COL = 20


def _fcum_fwd(p1, bf, *, ts=512, name):
    s = p1.shape[0]
    ts = _tile_rows(ts, s)

    def body(f_ref, bf_ref, c_ref, car_ref):
        i = pl.program_id(0)
        z = f_ref[...] + bf_ref[...]
        logf = jnp.minimum(z, 0.0) - _log1p_pos(jnp.exp(-jnp.abs(z)))

        @pl.when(i == 0)
        def _():
            car_ref[...] = jnp.zeros_like(car_ref)

        c_ref[...] = _cumsum_fwd(logf) + car_ref[0:1, :]
        car_ref[0:1, :] = c_ref[ts - 1:ts, :]

    return pl.pallas_call(
        body, name=name, grid=(s // ts,),
        in_specs=[pl.BlockSpec((ts, LANES), lambda i: (i, F_COL)), pl.BlockSpec((1, LANES), lambda i: (0, 0))],
        out_specs=pl.BlockSpec((ts, LANES), lambda i: (i, 0)),
        out_shape=jax.ShapeDtypeStruct((s, LANES), F32),
        scratch_shapes=[pltpu.VMEM((SUBLANES, LANES), F32)],
    )(p1, bf)


def _fcum_bwd(dcs, p1, bf, *, ts=512, name):
    s = p1.shape[0]
    ts = _tile_rows(ts, s)
    nt = s // ts

    def body(dc_ref, f_ref, bf_ref, df_ref, dbf_ref, car_ref):
        i = pl.program_id(0)

        @pl.when(i == 0)
        def _():
            car_ref[...] = jnp.zeros_like(car_ref)
            dbf_ref[...] = jnp.zeros_like(dbf_ref)

        dlog = _cumsum_rev(dc_ref[...]) + car_ref[0:1, :]
        car_ref[...] = dlog[:SUBLANES]
        z = f_ref[...] + bf_ref[...]
        df = dlog * _sigmoid(-z)
        df_ref[...] = df.astype(BF16)
        dbf_ref[...] += jnp.sum(df, axis=0, keepdims=True)

    return pl.pallas_call(
        body, name=name, grid=(nt,),
        in_specs=[pl.BlockSpec((ts, LANES), lambda i: (nt - 1 - i, 0)),
                  pl.BlockSpec((ts, LANES), lambda i: (nt - 1 - i, F_COL)),
                  pl.BlockSpec((1, LANES), lambda i: (0, 0))],
        out_specs=(pl.BlockSpec((ts, LANES), lambda i: (nt - 1 - i, 0)), pl.BlockSpec((1, LANES), lambda i: (0, 0))),
        out_shape=(jax.ShapeDtypeStruct((s, LANES), BF16), jax.ShapeDtypeStruct((1, LANES), F32)),
        scratch_shapes=[pltpu.VMEM((SUBLANES, LANES), F32)],
    )(dcs, p1, bf)


def _fox_scores(qm, kb, bias, ck, diagonal):
    sc = _dot_nt(qm, kb) + bias - ck
    if diagonal:
        sc = jnp.where(_lanes(sc.shape) <= _rows(sc.shape), sc, NEG)
    return sc


def _head_masks(shape):
    lane = _lanes(shape)
    return lane < 64, lane >= 64


def _fox_fwd(p1, cq, ck, *, tq=512, name):
    s = p1.shape[0]
    tq = _tile_rows(tq, s)
    tk = tq
    nq = s // tq

    def body(q_ref, k_ref, v_ref, cq_ref, ck_ref, o_ref, lb_ref):
        qi = pl.program_id(1)
        q = q_ref[...] * 0.125
        first, second = _head_masks((tq, LANES))
        qms = [jnp.where(sel, q, 0.0).astype(BF16) for sel in (first, second)]
        cqs = [cq_ref[:, hh * LANES:(hh + 1) * LANES] for hh in range(2)]
        biases = [jnp.tile(cqh, (1, tk // LANES)) for cqh in cqs]

        def step(kj, carry, diagonal):
            cols = pl.ds(pl.multiple_of(kj * tk, tk), tk)
            kb = k_ref[cols, :].astype(BF16)
            vb = v_ref[cols, :].astype(BF16)
            new, outs = [], []
            acc = carry[4]
            for hh in range(2):
                m_prev, l_prev = carry[2 * hh], carry[2 * hh + 1]
                sc = _fox_scores(qms[hh], kb, biases[hh], ck_ref[hh, :, cols], diagonal)
                m_new = jnp.maximum(m_prev, jnp.max(sc, axis=1, keepdims=True))
                pm = jnp.exp(sc - jnp.tile(m_new, (1, tk // LANES)))
                alpha = jnp.exp(m_prev - m_new)
                new += [m_new, alpha * l_prev + jnp.sum(pm, axis=1, keepdims=True)]
                outs.append(acc * alpha + _dot(pm.astype(BF16), vb))
            return tuple(new) + (jnp.where(first, outs[0], outs[1]),)

        zero = jnp.zeros((tq, LANES), F32)
        low = jnp.full((tq, LANES), NEG, F32)
        carry = lax.fori_loop(0, qi, lambda kj, c: step(kj, c, False), (low, zero, low, zero, zero))
        m0, l0, m1, l1, acc = step(qi, carry, True)
        o_ref[...] = (acc / jnp.where(first, l0, l1)).astype(BF16)
        lb_ref[:, 0:LANES] = cqs[0] - (m0 + jnp.log(l0))
        lb_ref[:, LANES:2 * LANES] = cqs[1] - (m1 + jnp.log(l1))

    return pl.pallas_call(
        body, name=name, grid=(4, nq),
        in_specs=[pl.BlockSpec((tq, LANES), lambda j, qi: (qi, 8 + j)),
                  pl.BlockSpec((s, LANES), lambda j, qi: (0, 12 + j)),
                  pl.BlockSpec((s, LANES), lambda j, qi: (0, 16 + j)),
                  pl.BlockSpec((tq, 2 * LANES), lambda j, qi: (qi, j)),
                  pl.BlockSpec((2, 1, s), lambda j, qi: (j, 0, 0))],
        out_specs=(pl.BlockSpec((tq, LANES), lambda j, qi: (qi, j)),
                   pl.BlockSpec((tq, 2 * LANES), lambda j, qi: (qi, j))),
        out_shape=(jax.ShapeDtypeStruct((s, 4 * LANES), BF16), jax.ShapeDtypeStruct((s, 8 * LANES), F32)),
    )(p1, p1, p1, cq, ck)


def _fox_delta(dy, o, sel, *, ts=512, name):
    s = o.shape[0]
    ts = _tile_rows(ts, s)

    def body(do_ref, o_ref, sel_ref, d_ref):
        prod = do_ref[...] * o_ref[...].astype(F32)
        d_ref[:, 0:LANES] = _dot_split(prod, sel_ref[0])
        d_ref[:, LANES:2 * LANES] = _dot_split(prod, sel_ref[1])

    return pl.pallas_call(
        body, name=name, grid=(4, s // ts),
        in_specs=[pl.BlockSpec((ts, LANES), lambda j, i: (i, 4 + j)),
                  pl.BlockSpec((ts, LANES), lambda j, i: (i, j)),
                  pl.BlockSpec((2, LANES, LANES), lambda j, i: (0, 0, 0))],
        out_specs=pl.BlockSpec((ts, 2 * LANES), lambda j, i: (i, j)),
        out_shape=jax.ShapeDtypeStruct((s, 8 * LANES), F32),
    )(dy, o, sel)


def _fox_bwd(p1, dy, lb, delta, ck, *, tq=512, name):
    s = p1.shape[0]
    tq = _tile_rows(tq, s)
    tk = tq
    nq = s // tq

    def body(q_ref, k_ref, v_ref, do_ref, lb_ref, dl_ref, ck_ref,
             dq_ref, dk_ref, dv_ref, dck_ref, dcq_ref, dqa_ref, dra_ref):
        kj = pl.program_id(1)

        @pl.when(kj == 0)
        def _():
            dqa_ref[...] = jnp.zeros_like(dqa_ref)
            dra_ref[...] = jnp.zeros_like(dra_ref)

        kf = k_ref[...]
        kb = kf.astype(BF16)
        vb = v_ref[...].astype(BF16)
        first, second = _head_masks((tk, LANES))
        kms = [jnp.where(sel, kf, 0.0).astype(BF16) for sel in (first, second)]
        cks = [ck_ref[hh] for hh in range(2)]

        def step(qi, carry, diagonal):
            dk_acc, dv_acc, dc0, dc1 = carry
            dcs = [dc0, dc1]
            rows = pl.ds(pl.multiple_of(qi * tq, tq), tq)
            q = q_ref[rows, :] * 0.125
            do = do_ref[rows, :]
            for hh, sel in enumerate((first, second)):
                qm = jnp.where(sel, q, 0.0).astype(BF16)
                dom = jnp.where(sel, do, 0.0).astype(BF16)
                bias = jnp.tile(lb_ref[rows, hh * LANES:(hh + 1) * LANES], (1, tk // LANES))
                pm = jnp.exp(_fox_scores(qm, kb, bias, cks[hh], diagonal))
                dv_acc = dv_acc + _dot_tn(pm.astype(BF16), dom)
                dp = _dot_nt(dom, vb)
                ds = pm * (dp - jnp.tile(dl_ref[rows, hh * LANES:(hh + 1) * LANES], (1, tk // LANES)))
                dsb = ds.astype(BF16)
                dk_acc = dk_acc + _dot_tn(dsb, qm)
                dcs[hh] = dcs[hh] - jnp.sum(ds, axis=0, keepdims=True)
                dqa_ref[rows, :] += _dot(dsb, kms[hh])
                dra_ref[hh, rows, :] += jnp.sum(ds, axis=1, keepdims=True)
            return dk_acc, dv_acc, dcs[0], dcs[1]

        zero = jnp.zeros((tk, LANES), F32)
        zrow = jnp.zeros((1, tk), F32)
        carry = step(kj, (zero, zero, zrow, zrow), True)
        dk_acc, dv_acc, dc0, dc1 = lax.fori_loop(kj + 1, nq, lambda qi, c: step(qi, c, False), carry)
        dk_ref[...] = dk_acc.astype(BF16)
        dv_ref[...] = dv_acc.astype(BF16)
        dck_ref[0] = dc0
        dck_ref[1] = dc1

        @pl.when(kj == nq - 1)
        def _():
            dq_ref[...] = (dqa_ref[...] * 0.125).astype(BF16)
            dcq_ref[:, 0:LANES] = dra_ref[0]
            dcq_ref[:, LANES:2 * LANES] = dra_ref[1]

    def full(width, col0):
        return pl.BlockSpec((s, width), lambda j, kj: (0, col0 + j))

    kblk = pl.BlockSpec((tk, LANES), lambda j, kj: (kj, j))
    f = jax.ShapeDtypeStruct
    return pl.pallas_call(
        body, name=name, grid=(4, nq),
        in_specs=[full(LANES, 8),
                  pl.BlockSpec((tk, LANES), lambda j, kj: (kj, 12 + j)),
                  pl.BlockSpec((tk, LANES), lambda j, kj: (kj, 16 + j)),
                  full(LANES, 4), full(2 * LANES, 0), full(2 * LANES, 0),
                  pl.BlockSpec((2, 1, tk), lambda j, kj: (j, 0, kj))],
        out_specs=(full(LANES, 0), kblk, kblk, pl.BlockSpec((2, 1, tk), lambda j, kj: (j, 0, kj)),
                   full(2 * LANES, 0)),
        out_shape=(f((s, 4 * LANES), BF16), f((s, 4 * LANES), BF16), f((s, 4 * LANES), BF16),
                   f((8, 1, s), F32), f((s, 8 * LANES), F32)),
        scratch_shapes=[pltpu.VMEM((s, LANES), F32), pltpu.VMEM((2, s, LANES), F32)],
    )(p1, p1, p1, dy, lb, delta, ck)


def _row_block(r, cap=256):
    best = None
    for rb in range(2 * SUBLANES, min(r, cap) + 1, 2 * SUBLANES):
        if r % rb == 0:
            best = rb
    return r if best is None else best


def _adamw(w, g, m, v, *, name):
    r, c = w.shape
    rb = _row_block(r)

    def body(w_ref, g_ref, m_ref, v_ref, d_ref, nm_ref, nv_ref):
        gv = g_ref[...]
        mn = ADAM_B1 * m_ref[...] + (1.0 - ADAM_B1) * gv
        vn = ADAM_B2 * v_ref[...] + (1.0 - ADAM_B2) * (gv * gv)
        m_hat = mn / ADAM_C1
        v_hat = vn / ADAM_C2
        d_ref[...] = (-ADAM_LR) * (m_hat / (jnp.sqrt(v_hat) + ADAM_EPS) + ADAM_WD * w_ref[...])
        nm_ref[...] = mn
        nv_ref[...] = vn

    blk = pl.BlockSpec((rb, c), lambda i: (i, 0))
    shp = jax.ShapeDtypeStruct((r, c), F32)
    return pl.pallas_call(
        body, name=name, grid=(r // rb,), in_specs=[blk] * 4, out_specs=(blk,) * 3, out_shape=(shp,) * 3,
    )(w, g, m, v)


def _adamw_halves(w, mine, theirs, m, v, core, *, name):
    r, c = w.shape
    rh = r // 2
    rb = _row_block(rh)
    per = rh // rb

    def body(core_ref, w_ref, mine_ref, theirs_ref, m_ref, v_ref, g_ref, d_ref, nm_ref, nv_ref):
        gv = jnp.where(pl.program_id(0) == core_ref[0], mine_ref[...], theirs_ref[...])
        g_ref[...] = gv
        mn = ADAM_B1 * m_ref[...] + (1.0 - ADAM_B1) * gv
        vn = ADAM_B2 * v_ref[...] + (1.0 - ADAM_B2) * (gv * gv)
        m_hat = mn / ADAM_C1
        v_hat = vn / ADAM_C2
        d_ref[...] = (-ADAM_LR) * (m_hat / (jnp.sqrt(v_hat) + ADAM_EPS) + ADAM_WD * w_ref[...])
        nm_ref[...] = mn
        nv_ref[...] = vn

    full = pl.BlockSpec((rb, c), lambda h, i, core_ref: (h * per + i, 0))
    half = pl.BlockSpec((rb, c), lambda h, i, core_ref: (i, 0))
    shp = jax.ShapeDtypeStruct((r, c), F32)
    return pl.pallas_call(
        body, name=name,
        grid_spec=pltpu.PrefetchScalarGridSpec(
            num_scalar_prefetch=1, grid=(2, per), in_specs=[full, half, half, full, full], out_specs=(full,) * 4),
        out_shape=(shp,) * 4,
    )(core, w, mine, theirs, m, v)


def _pair_sum(g, col, ra, core, after, *, name):
    _, rh, c = ra.shape
    rb = _row_block(rh)

    def body(core_ref, g_ref, ra_ref, after_ref, h_ref, h16_ref):
        tot = g_ref[...] + ra_ref[...]
        h_ref[...] = tot
        h16_ref[...] = tot.astype(BF16)

    if col:
        g_spec = pl.BlockSpec((None, rb, c), lambda k, i, core_ref: (core_ref[0], i, k))
    else:
        g_spec = pl.BlockSpec((None, None, rb, c), lambda k, i, core_ref: (k, core_ref[0], i, 0))
    slot = pl.BlockSpec((None, rb, c), lambda k, i, core_ref: (k, i, 0))
    return pl.pallas_call(
        body, name=name,
        grid_spec=pltpu.PrefetchScalarGridSpec(
            num_scalar_prefetch=1, grid=(N_CHIPS, rh // rb), in_specs=[g_spec, slot, ANY], out_specs=(slot, slot)),
        out_shape=(jax.ShapeDtypeStruct((N_CHIPS, rh, c), F32), jax.ShapeDtypeStruct((N_CHIPS, rh, c), BF16)),
    )(core, g, ra, after)


def _first_sum(h, r1, keep, after, *, name):
    _, rh, c = h.shape
    rb = _row_block(rh)

    def body(keep_ref, h_ref, r_ref, after_ref, s_ref, s16_ref):
        tot = h_ref[...] + r_ref[...].astype(F32)
        s_ref[...] = tot
        s16_ref[...] = tot.astype(BF16)

    slot = pl.BlockSpec((None, rb, c), lambda t, i, keep_ref: (t, i, 0))
    return pl.pallas_call(
        body, name=name,
        grid_spec=pltpu.PrefetchScalarGridSpec(
            num_scalar_prefetch=1, grid=(2, rh // rb),
            in_specs=[pl.BlockSpec((None, rb, c), lambda t, i, keep_ref: (keep_ref[t], i, 0)), slot, ANY],
            out_specs=(slot, slot)),
        out_shape=(jax.ShapeDtypeStruct((2, rh, c), F32), jax.ShapeDtypeStruct((2, rh, c), BF16)),
    )(keep, h, r1, after)


def _second_sum(s1, r2, mine, after, *, name):
    _, rh, c = s1.shape
    rb = _row_block(rh)

    def body(mine_ref, s_ref, r_ref, after_ref, t_ref):
        t_ref[...] = s_ref[...] + r_ref[...].astype(F32)

    flat = pl.BlockSpec((rb, c), lambda i, mine_ref: (i, 0))
    return pl.pallas_call(
        body, name=name,
        grid_spec=pltpu.PrefetchScalarGridSpec(
            num_scalar_prefetch=1, grid=(rh // rb,),
            in_specs=[pl.BlockSpec((None, rb, c), lambda i, mine_ref: (mine_ref[0], i, 0)), flat, ANY],
            out_specs=flat),
        out_shape=jax.ShapeDtypeStruct((rh, c), F32),
    )(mine, s1, r2, after)


def _place(shard, col, chip, dtype, *, name):
    r, c = shard.shape
    rh = r // 2
    rb = _row_block(rh)

    def body(chip_ref, s_ref, o_ref):
        o_ref[...] = s_ref[...].astype(o_ref.dtype)

    if col:
        out_spec = pl.BlockSpec((None, rb, c), lambda h, i, chip_ref: (h, i, chip_ref[0]))
        shape = (2, rh, N_CHIPS * c)
    else:
        out_spec = pl.BlockSpec((None, None, rb, c), lambda h, i, chip_ref: (chip_ref[0], h, i, 0))
        shape = (N_CHIPS, 2, rh, c)
    per = rh // rb
    return pl.pallas_call(
        body, name=name,
        grid_spec=pltpu.PrefetchScalarGridSpec(
            num_scalar_prefetch=1, grid=(2, per),
            in_specs=[pl.BlockSpec((rb, c), lambda h, i, chip_ref: (h * per + i, 0))], out_specs=out_spec),
        out_shape=jax.ShapeDtypeStruct(shape, dtype),
    )(chip, shard)


ANY = pl.BlockSpec(memory_space=pl.ANY)


def _mesh_pos():
    return lax.axis_index("x"), lax.axis_index("y"), lax.axis_index("c")


def _other_chips(x, y):
    return [(1 - x, y), (x, 1 - y), (1 - x, 1 - y)]


def _remote(src, dst, ssem, rsem, dev):
    return pltpu.make_async_remote_copy(src_ref=src, dst_ref=dst, send_sem=ssem, recv_sem=rsem,
                                        device_id=dev, device_id_type=MESH)


def _flip(a, b):
    return a + b - 2 * a * b


def _handshake(peers):
    barrier = pltpu.get_barrier_semaphore()
    for peer in peers:
        pl.semaphore_signal(barrier, inc=1, device_id=peer, device_id_type=MESH)
    pl.semaphore_wait(barrier, len(peers))


def _slab(ref, col, width, k, h):
    if not col:
        return ref.at[k, h]
    start = k * width if isinstance(k, int) else pl.multiple_of(k * width, LANES)
    return ref.at[h, :, pl.ds(start, width)]


def _all_gather(bufs, cols, *, collective_id, name):
    n = len(bufs)
    widths = [b.shape[2] // N_CHIPS if col else b.shape[3] for b, col in zip(bufs, cols)]
    outs = [jax.new_ref(b, memory_space=pltpu.MemorySpace.HBM) for b in bufs]

    def body(ssem, rsem):
        x, y, c = _mesh_pos()
        me = 2 * x + y
        sib = (x, y, 1 - c)
        n1 = (_flip(x, 1 - c), _flip(y, c))
        n2 = (_flip(x, c), _flip(y, 1 - c))
        k1 = 2 * n1[0] + n1[1]
        k2 = 2 * n2[0] + n2[1]
        kd = 2 * (1 - x) + (1 - y)
        _handshake([n1 + (c,), n2 + (c,), sib])

        def slab(a, k, h):
            return _slab(outs[a], cols[a], widths[a], k, h)

        def copy(a, j, src, dst, dev):
            return _remote(src, dst, ssem.at[a, j], rsem.at[a, j], dev)

        sends = []
        for a in range(n):
            for j, nb in ((0, n1), (1, n2)):
                own = slab(a, me, c)
                cp = copy(a, j, own, own, nb + (c,))
                cp.start()
                sends.append(cp)
        arrivals = ((0, k1, n1, 3), (1, k2, n2, 4), (2, kd, n2, 5))
        for j, k, nb, fwd in arrivals:
            for a in range(n):
                got = slab(a, k, c)
                copy(a, j, got, got, nb + (c,)).wait_recv()
                if j == 0:
                    cp = copy(a, 2, got, got, n2 + (c,))
                    cp.start()
                    sends.append(cp)
                cp = copy(a, fwd, got, got, sib)
                cp.start()
                sends.append(cp)
        for fwd, k in ((3, k2), (4, k1), (5, kd)):
            for a in range(n):
                got = slab(a, k, 1 - c)
                copy(a, fwd, got, got, sib).wait_recv()
        for cp in sends:
            cp.wait_send()

    _sequencer_call(body, (), [(n, 6), (n, 6)], collective_id, name)()
    return [ref[...] for ref in outs]


def _sequencer_call(body, out_types, sem_shapes, collective_id, name):
    return pl.kernel(
        body, name=name, out_type=out_types,
        mesh=plsc.ScalarSubcoreMesh(axis_name="sequencer", num_cores=1),
        scratch_types=[pltpu.SemaphoreType.DMA(shape) for shape in sem_shapes],
        compiler_params=pltpu.CompilerParams(collective_id=collective_id))


def _send_other_half(grads, cols, *, collective_id, name):
    n = len(grads)

    def shard_shape(g, col):
        if col:
            return (g.shape[1], g.shape[2] // N_CHIPS)
        return g.shape[2:]

    shapes = [shard_shape(g, col) for g, col in zip(grads, cols)]

    def body(*refs):
        ins, outs = refs[:n], refs[n:2 * n]
        ssem, rsem = refs[2 * n:]
        x, y, c = _mesh_pos()
        sib = (x, y, 1 - c)
        _handshake([sib])
        sends = []
        for a in range(n):
            for k in range(N_CHIPS):
                src = _slab(ins[a], cols[a], shapes[a][1], k, 1 - c)
                cp = _remote(src, outs[a].at[k], ssem.at[a, k], rsem.at[a, k], sib)
                cp.start()
                sends.append(cp)
        for cp in sends:
            cp.wait()

    out_types = [jax.ShapeDtypeStruct((N_CHIPS,) + shp, g.dtype) for g, shp in zip(grads, shapes)]
    return _sequencer_call(body, out_types, [(n, N_CHIPS), (n, N_CHIPS)], collective_id, name)(*grads)


def _send_first(sums, *, collective_id, name):
    n = len(sums)

    def body(*refs):
        ins, outs = refs[:n], refs[n:2 * n]
        ssem, rsem = refs[2 * n:]
        x, y, c = _mesh_pos()
        nb = (_flip(x, c), _flip(y, 1 - c), c)
        _handshake([nb])
        sends = []
        for a in range(n):
            for t in range(2):
                k = 2 * (c * (1 - x) + (1 - c) * t) + (c * t + (1 - c) * (1 - y))
                cp = _remote(ins[a].at[k], outs[a].at[t], ssem.at[a, t], rsem.at[a, t], nb)
                cp.start()
                sends.append(cp)
        for cp in sends:
            cp.wait()

    out_types = [jax.ShapeDtypeStruct((2,) + h.shape[1:], h.dtype) for h in sums]
    return _sequencer_call(body, out_types, [(n, 2), (n, 2)], collective_id, name)(*sums)


def _send_second(sums, *, collective_id, name):
    n = len(sums)

    def body(*refs):
        ins, outs = refs[:n], refs[n:2 * n]
        ssem, rsem = refs[2 * n:]
        x, y, c = _mesh_pos()
        nb = (_flip(x, 1 - c), _flip(y, c), c)
        other = 1 - (c * y + (1 - c) * x)
        _handshake([nb])
        sends = []
        for a in range(n):
            cp = _remote(ins[a].at[other], outs[a], ssem.at[a], rsem.at[a], nb)
            cp.start()
            sends.append(cp)
        for cp in sends:
            cp.wait()

    out_types = [jax.ShapeDtypeStruct(s.shape[1:], s.dtype) for s in sums]
    return _sequencer_call(body, out_types, [(n,), (n,)], collective_id, name)(*sums)


def _swap_halves(halves, *, collective_id, name):
    n = len(halves)

    def body(*refs):
        ins, outs = refs[:n], refs[n:2 * n]
        ssem, rsem = refs[2 * n:]
        x, y, c = _mesh_pos()
        sib = (x, y, 1 - c)
        _handshake([sib])
        cps = []
        for a in range(n):
            cp = _remote(ins[a], outs[a], ssem.at[a], rsem.at[a], sib)
            cp.start()
            cps.append(cp)
        for cp in cps:
            cp.wait()

    out_types = [jax.ShapeDtypeStruct(h.shape, h.dtype) for h in halves]
    return _sequencer_call(body, out_types, [(n,), (n,)], collective_id, name)(*halves)


def _all_reduce_small(buf, *, name):
    r = buf.shape[0]
    rh = r // 2

    def body(in_ref, out_ref, x1_ref, x2_ref, ssem, rsem):
        x, y, c = _mesh_pos()
        me = 2 * x + y
        sib = (x, y, 1 - c)
        chips = _other_chips(x, y)
        cp = _remote(in_ref, x1_ref, ssem.at[0], rsem.at[0], sib)
        cp.start()
        cp.wait()
        off = pl.multiple_of(c * rh, SUBLANES)
        x2_ref[me] = in_ref[pl.ds(off, rh), :] + x1_ref[pl.ds(off, rh), :]
        sends = []
        for j, (cx, cy) in enumerate(chips):
            s = _remote(x2_ref.at[me], x2_ref.at[me], ssem.at[1 + j], rsem.at[1 + j], (cx, cy, c))
            s.start()
            sends.append(s)
        for j, (cx, cy) in enumerate(chips):
            slot = x2_ref.at[2 * cx + cy]
            _remote(slot, slot, ssem.at[1 + j], rsem.at[1 + j], (cx, cy, c)).wait_recv()
        out_ref[pl.ds(off, rh), :] = ((x2_ref[0] + x2_ref[1]) + x2_ref[2]) + x2_ref[3]
        for s in sends:
            s.wait_send()
        mine = out_ref.at[pl.ds(off, rh), :]
        s3 = _remote(mine, mine, ssem.at[4], rsem.at[4], sib)
        s3.start()
        off2 = pl.multiple_of((1 - c) * rh, SUBLANES)
        theirs = out_ref.at[pl.ds(off2, rh), :]
        _remote(theirs, theirs, ssem.at[4], rsem.at[4], sib).wait_recv()
        s3.wait_send()

    vm = pl.BlockSpec(memory_space=pltpu.VMEM)
    return pl.pallas_call(
        body, name=name, in_specs=[vm], out_specs=vm,
        out_shape=jax.ShapeDtypeStruct((r, LANES), F32),
        scratch_shapes=[pltpu.VMEM((r, LANES), F32), pltpu.VMEM((N_CHIPS, rh, LANES), F32),
                        pltpu.SemaphoreType.DMA((5,)), pltpu.SemaphoreType.DMA((5,))],
    )(buf)


PACK_ALIGN = 2 * SUBLANES * LANES


def _pack(arrays, rows_multiple=2 * SUBLANES):
    parts, offs, off = [], [], 0
    for a in arrays:
        flat = a.reshape(-1).astype(F32)
        padded = -(-flat.shape[0] // PACK_ALIGN) * PACK_ALIGN
        parts.append(jnp.pad(flat, (0, padded - flat.shape[0])))
        offs.append(off)
        off += padded
    buf = jnp.concatenate(parts).reshape(-1, LANES)
    return buf, offs


def _unpack(buf, offs, shapes):
    flat = buf.reshape(-1)
    out = []
    for off, shp in zip(offs, shapes):
        size = 1
        for d in shp:
            size *= d
        out.append(flat[off:off + size].reshape(shp))
    return out


def _cols_from_shards(g4):
    _, k, ns = g4.shape
    return jnp.transpose(g4, (1, 0, 2)).reshape(k, N_CHIPS * ns)


def _cols_to_shards(w):
    k, n = w.shape
    return jnp.transpose(w.reshape(k, N_CHIPS, n // N_CHIPS), (1, 0, 2))


def _block_cols(w, parts, blocks):
    lead = w.shape[:-1]
    width = w.shape[-1] // (parts * blocks)
    w = w.reshape(lead + (parts, blocks, width))
    w = jnp.swapaxes(w, -3, -2)
    return w.reshape(lead + (parts * blocks * width,))


def _unblock_cols(w, parts, blocks):
    lead = w.shape[:-1]
    width = w.shape[-1] // (parts * blocks)
    w = w.reshape(lead + (blocks, parts, width))
    w = jnp.swapaxes(w, -3, -2)
    return w.reshape(lead + (parts * blocks * width,))


def _pair_blockdiag(w8):
    w = w8.reshape(4, 2, 64, 64)
    z = jnp.zeros((4, 64, 64), w8.dtype)
    top = jnp.concatenate([w[:, 0], z], axis=2)
    bot = jnp.concatenate([z, w[:, 1]], axis=2)
    return jnp.concatenate([top, bot], axis=1)


def _pair_diag_blocks(w4):
    a = w4[:, :64, :64]
    b = w4[:, 64:, 64:]
    return jnp.stack([a, b], axis=1).reshape(8, 64, 64)


def _local_step(x, target, wts, on_event=None):
    s = x.shape[0]
    g = {}

    def event(name, token):
        if on_event is not None:
            on_event(name, g, token)

    win0 = wts["w_in0"]
    wout0 = wts["w_out0"]
    win1 = wts["w_in1"]
    wout1 = wts["w_out1"]
    wup = wts["w_up"]
    wdown = wts["w_down"]
    w4, b4, w3, b3 = wts["w4"], wts["b4"], wts["w3"], wts["b3"]
    wa, wx = wts["wa"], wts["wx"]
    wat, wxt = jnp.swapaxes(wa, 1, 2), jnp.swapaxes(wx, 1, 2)
    ba, bx, lam = wts["ba"], wts["bx"], wts["lam"]
    fcw, fcb = wts["ffn_cw"], wts["ffn_cb"]
    sgu_w, sgu_wt = wts["sgu_w"], wts["sgu_wt"]
    sgu_bias, sgu_gn = wts["sgu_bias"], wts["sgu_gn"]
    bf = wts["bf"]

    lane = jnp.arange(LANES)
    seg = jnp.where((lane[:, None] // 64) == (lane[None, :] // 64), 1.0 / 64.0, 0.0).astype(BF16)
    sel = jnp.stack([jnp.broadcast_to((lane[:, None] < 64), (LANES, LANES)),
                     jnp.broadcast_to((lane[:, None] >= 64), (LANES, LANES))]).astype(BF16)
    tril = (lane[:, None] >= lane[None, :]).astype(F32)

    n0 = _norm_fwd(x, wts["g_mix0"], name="norm_mix0")
    p0 = _mm([n0], win0, nb=640, name="mm_in0")
    ya, yb, hl = _even_core_fwd(p0, w4, b4, wa, ba, wx, bx, lam, w3, b3, name="even_fwd")
    h1 = _mm([ya, yb], wout0, res=x, name="mm_out0")

    def ffn_fwd(h, layer):
        n = _norm_fwd(h, wts["g_ffn"][layer], name=f"norm_ffn{layer}")
        up = _mm([n], wup[layer], out_dtype=BF16, nb=1408, name=f"mm_up{layer}")
        act = _ffn_core_fwd(up, fcw[layer], fcb[layer], name=f"ffn_fwd{layer}")
        hn = _mm([act], wdown[layer], res=h, name=f"mm_down{layer}")
        return n, up, act, hn

    n1, up0, act0, h2 = ffn_fwd(h1, 0)

    n2 = _norm_fwd(h2, wts["g_mix1"], name="norm_mix1")
    p1 = _mm([n2], win1, nb=896, name="mm_in1")
    yc = _sgu_fwd(p1, sgu_gn, sgu_w, sgu_bias, seg, name="sgu_fwd")
    cum = _fcum_fwd(p1, bf, name="fcum_fwd")
    c8 = cum[:, :8]
    cq = jnp.broadcast_to(c8[:, :, None], (s, 8, LANES)).reshape(s, 8 * LANES)
    ck = jnp.transpose(c8).reshape(8, 1, s)
    yd, lb = _fox_fwd(p1, cq, ck, name="fox_fwd")
    h3 = _mm([yc, yd], wout1, res=h2, name="mm_out1")

    n3, up1, act1, h4 = ffn_fwd(h3, 1)
    dh4, loss, g["final_norm"] = _final(h4, wts["g_final"], target, name="final")

    def ffn_bwd(dh, h, n, up, act, layer):
        dact = _mm([dh], wdown[layer], trans_w=True, out_dtype=BF16, nb=1408, name=f"mm_dact{layer}")
        g[f"w_down{layer}"] = _mm_tn([act], [dh], nb=512, name=f"mm_dwdown{layer}")
        event(f"dwdown{layer}", g[f"w_down{layer}"])
        dgate, dval, dcwg, dcwv, dcbg, dcbv = _ffn_core_bwd(dact, up, fcw[layer], fcb[layer], name=f"ffn_bwd{layer}")
        event(f"ffn_bwd{layer}", dgate)
        dn = _mm([dgate, dval], wup[layer], trans_w=True, nb=512, name=f"mm_dn_ffn{layer}")
        g[f"w_up{layer}"] = _mm_tn([n], [dgate, dval], nb=1408, name=f"mm_dwup{layer}")
        event(f"dwup{layer}", g[f"w_up{layer}"])
        dhn, g[f"g_ffn{layer}"] = _norm_bwd(dn, h, wts["g_ffn"][layer], dh, name=f"norm_bwd_ffn{layer}")
        g[f"ffn_cw{layer}"] = jnp.concatenate([dcwg, dcwv], axis=1)
        g[f"ffn_cb{layer}"] = jnp.concatenate([dcbg, dcbv], axis=1)
        return dhn

    dh3 = ffn_bwd(dh4, h3, n3, up1, act1, 1)

    dy1 = _mm([dh3], wout1, trans_w=True, name="mm_dy1")
    g["w_out1"] = _mm_tn([yc, yd], [dh3], nb=512, name="mm_dwout1")
    event("dwout1", g["w_out1"])
    dzu, dzg, g["sgu_w"], g["sgu_bias"], g["sgu_gn"] = _sgu_bwd(
        p1, dy1, sgu_gn, sgu_w, sgu_wt, sgu_bias, seg, tril, name="sgu_bwd")
    delta = _fox_delta(dy1, yd, sel, name="fox_delta")
    dq, dk, dv, dck, dcq = _fox_bwd(p1, dy1, lb, delta, ck, name="fox_bwd")
    event("fox_bwd", dq)
    dcs = jnp.pad(jnp.transpose(dck.reshape(8, s)) + dcq.reshape(s, 8, LANES)[:, :, 0], ((0, 0), (0, LANES - 8)))
    df, g["bf"] = _fcum_bwd(dcs, p1, bf, name="fcum_bwd")
    dp1 = jnp.concatenate([dzu, dzg, dq, dk, dv, df], axis=1)
    dn2 = _mm([dp1], win1, trans_w=True, name="mm_dn_mix1")
    g["w_in1"] = _mm_tn([n2], [dp1], nb=896, name="mm_dwin1")
    event("dwin1", g["w_in1"])
    dh2, g["g_mix1"] = _norm_bwd(dn2, h2, wts["g_mix1"], dh3, name="norm_bwd_mix1")

    dh1 = ffn_bwd(dh2, h1, n1, up0, act0, 0)

    dy0 = _mm([dh1], wout0, trans_w=True, name="mm_dy0")
    g["w_out0"] = _mm_tn([ya, yb], [dh1], nb=512, name="mm_dwout0")
    event("dwout0", g["w_out0"])
    (dp0, g["w4"], g["b4"], g["wa"], g["ba"], g["wx"], g["bx"], g["lam"], g["w3"], g["b3"]) = _even_core_bwd(
        dy0, p0, hl, w4, b4, wa, wat, ba, wx, wxt, bx, lam, w3, b3, name="even_bwd")
    event("even_bwd", dp0)
    dn0 = _mm([dp0], win0, trans_w=True, name="mm_dn_mix0")
    g["w_in0"] = _mm_tn([n0], [dp0], nb=640, name="mm_dwin0")
    event("dwin0", g["w_in0"])
    grad_x, g["g_mix0"] = _norm_bwd(dn0, x, wts["g_mix0"], dh1, name="norm_bwd_mix0")
    return loss, grad_x, g


def _prepare_weights(nat):
    lane = jnp.arange(LANES)
    tril = (lane[:, None] >= lane[None, :]).astype(F32)
    sgu_tril = nat["sgu_w"][0] * tril
    w_in1 = nat["mix1_w_in"]
    nblk = D_FF // FFN_CB
    return {
        "w_in0": _block_cols(nat["mix0_w_in"], 5, 4),
        "w_out0": nat["mix0_w_out"],
        "w_in1": jnp.pad(w_in1, ((0, 0), (0, 21 * LANES - w_in1.shape[1]))),
        "w_out1": nat["mix1_w_out"],
        "w_up": [nat["ffn_up"][l] for l in range(2)],
        "w_down": [nat["ffn_down"][l] for l in range(2)],
        "w4": nat["lru_conv_w"], "b4": nat["lru_conv_b"], "w3": nat["sconv_w"], "b3": nat["sconv_b"],
        "wa": _pair_blockdiag(nat["lru_wa"][0]).astype(BF16), "wx": _pair_blockdiag(nat["lru_wx"][0]).astype(BF16),
        "ba": nat["lru_ba"], "bx": nat["lru_bx"], "lam": nat["lru_lambda"],
        "ffn_cw": [nat["ffn_conv_w"][l] for l in range(2)],
        "ffn_cb": [nat["ffn_conv_b"][l:l + 1] for l in range(2)],
        "sgu_w": sgu_tril.astype(BF16), "sgu_wt": jnp.swapaxes(sgu_tril, 1, 2).astype(BF16),
        "sgu_bias": jnp.repeat(jnp.transpose(nat["sgu_b"][0]), 64, axis=1), "sgu_gn": nat["sgu_norm"],
        "bf": jnp.pad(nat["fox_bf"], ((0, 0), (0, LANES - 8))),
        "g_mix0": nat["mix0_norm"], "g_mix1": nat["mix1_norm"],
        "g_ffn": [nat["ffn_norm"][0:1], nat["ffn_norm"][1:2]], "g_final": nat["final_norm"].reshape(1, D_MODEL),
    }


def _natural_grads(g):
    nblk = D_FF // FFN_CB
    small = {
        "mix0_norm": g["g_mix0"], "lru_conv_b": g["b4"],
        "lru_wa": _pair_diag_blocks(g["wa"])[None], "lru_ba": g["ba"],
        "lru_wx": _pair_diag_blocks(g["wx"])[None], "lru_bx": g["bx"],
        "lru_lambda": g["lam"], "sconv_b": g["b3"],
        "sgu_w": g["sgu_w"][None],
        "sgu_b": jnp.transpose(g["sgu_bias"].reshape(CHUNK, 8, 64).sum(axis=2))[None],
        "fox_bf": g["bf"][:, :8],
        "ffn_norm": jnp.concatenate([g["g_ffn0"], g["g_ffn1"]], axis=0),
        "ffn_conv_b": jnp.concatenate([g["ffn_cb0"], g["ffn_cb1"]], axis=0),
        "final_norm": g["final_norm"].reshape(D_MODEL),
        "lru_conv_w": g["w4"][None], "sconv_w": g["w3"][None],
        "ffn_conv_w": jnp.stack([g["ffn_cw0"], g["ffn_cw1"]]),
        "mix1_norm": g["g_mix1"], "sgu_norm": g["sgu_gn"],
    }
    big = {
        "mix0_w_in": _unblock_cols(g["w_in0"], 5, 4), "mix0_w_out": g["w_out0"],
        "mix1_w_in": g["w_in1"][:, :2568], "mix1_w_out": g["w_out1"],
        "ffn_up0": g["w_up0"], "ffn_up1": g["w_up1"],
        "ffn_down0": g["w_down0"], "ffn_down1": g["w_down1"],
    }
    return small, big


COL_SHARDED = ("mix0_w_in", "mix1_w_in", "ffn_up0", "ffn_up1")
COL_ALIGNED = ("mix0_w_in", "ffn_up0", "ffn_up1")
SMALL_SHARDED = ("lru_conv_w", "sconv_w", "ffn_conv_w", "mix1_norm", "sgu_norm")
SMALL_REPLICATED = ("mix0_norm", "lru_conv_b", "lru_wa", "lru_ba", "lru_wx", "lru_bx", "lru_lambda", "sconv_b",
                    "sgu_w", "sgu_b", "fox_bf", "ffn_norm", "ffn_conv_b", "final_norm")
BIG = ("mix0_w_in", "mix0_w_out", "mix1_w_in", "mix1_w_out", "ffn_up0", "ffn_up1", "ffn_down0", "ffn_down1")
WEIGHT_ORDER = ("mix0_norm", "mix0_w_in", "lru_conv_w", "lru_conv_b", "lru_wa", "lru_ba", "lru_wx", "lru_bx",
                "lru_lambda", "sconv_w", "sconv_b", "mix0_w_out", "mix1_norm", "mix1_w_in", "sgu_norm", "sgu_w",
                "sgu_b", "fox_bf", "mix1_w_out", "ffn_norm", "ffn_up", "ffn_conv_w", "ffn_conv_b", "ffn_down",
                "final_norm")


GATHER_GROUPS = (("mix0_w_in", "mix0_w_out"), ("ffn_up0", "ffn_down0", "mix1_w_in"),
                 ("mix1_w_out", "ffn_up1", "ffn_down1"))
CID_GATHER, CID_PAIR, CID_FIRST, CID_SECOND, CID_SWAP = 1, 2, 3, 4, 5


class _GradReducer:
    def __init__(self):
        x, y, c = _mesh_pos()
        self.core = c.reshape(1).astype(jnp.int32)
        self.keep = jnp.stack([c * (2 * x + t) + (1 - c) * (2 * t + y) for t in range(2)]).astype(jnp.int32)
        self.mine = (c * y + (1 - c) * x).reshape(1).astype(jnp.int32)
        self.groups = {}

    @staticmethod
    def _view(name, a):
        if name in COL_ALIGNED:
            return a.reshape(2, a.shape[0] // 2, a.shape[1])
        if name in COL_SHARDED:
            a = _cols_to_shards(a)
            return a.reshape(N_CHIPS, 2, a.shape[1] // 2, a.shape[2])
        rows = a.shape[0] // (2 * N_CHIPS)
        return a.reshape(N_CHIPS, 2, rows, a.shape[1])

    def start(self, group, grads):
        names = tuple(grads)
        views = [self._view(k, grads[k]) for k in names]
        cols = [k in COL_ALIGNED for k in names]
        data = _send_other_half(views, cols, collective_id=CID_PAIR, name=f"rs_pair_{group}")
        self.groups[group] = dict(names=names, stage=0, views=views, cols=cols, data=data)

    def step(self, group, after):
        st = self.groups[group]
        names = st["names"]
        if st["stage"] == 0:
            sums = [_pair_sum(a, col, b, self.core, after, name=f"rs_pair_sum_{k}")
                    for k, a, col, b in zip(names, st["views"], st["cols"], st["data"])]
            st["keep"] = [s32 for s32, _ in sums]
            st["data"] = _send_first([s16 for _, s16 in sums], collective_id=CID_FIRST, name=f"rs_first_{group}")
        elif st["stage"] == 1:
            sums = [_first_sum(s32, r, self.keep, after, name=f"rs_first_sum_{k}")
                    for k, s32, r in zip(names, st["keep"], st["data"])]
            st["keep"] = [s32 for s32, _ in sums]
            st["data"] = _send_second([s16 for _, s16 in sums], collective_id=CID_SECOND, name=f"rs_second_{group}")
        else:
            st["mine"] = [_second_sum(s32, r, self.mine, after, name=f"rs_second_sum_{k}")
                          for k, s32, r in zip(names, st["keep"], st["data"])]
            st["data"] = _swap_halves(st["mine"], collective_id=CID_SWAP, name=f"rs_swap_{group}")
        st["stage"] += 1

    def result(self, group):
        st = self.groups[group]
        return {k: (a, b) for k, a, b in zip(st["names"], st["mine"], st["data"])}


def _train_step(x, target, w, m, v):
    x2 = x[0]
    t2 = target[0]
    chip = 2 * lax.axis_index("x") + lax.axis_index("y")
    core_arr = lax.axis_index("c").reshape(1).astype(jnp.int32)
    chip_arr = chip.reshape(1).astype(jnp.int32)

    big_shards = {
        "mix0_w_in": w["mix0_w_in"][0], "mix0_w_out": w["mix0_w_out"][0],
        "mix1_w_in": w["mix1_w_in"][0], "mix1_w_out": w["mix1_w_out"][0],
        "ffn_up0": w["ffn_up"][0], "ffn_up1": w["ffn_up"][1],
        "ffn_down0": w["ffn_down"][0], "ffn_down1": w["ffn_down"][1],
    }
    small_shards = [w[k] for k in SMALL_SHARDED]
    small_buf, small_offs = _pack(small_shards)
    full = {}
    small_all = None
    for gi, names in enumerate(GATHER_GROUPS):
        cols = [k in COL_ALIGNED for k in names]
        placed = [_place(big_shards[k], col, chip_arr, BF16, name=f"place_{k}") for k, col in zip(names, cols)]
        if gi == 0:
            placed.append(_place(small_buf, False, chip_arr, F32, name="place_small"))
            cols = cols + [False]
        gathered = _all_gather(placed, cols, collective_id=CID_GATHER, name=f"gather_weights{gi}")
        if gi == 0:
            small_all = gathered[-1].reshape(N_CHIPS, -1, LANES)
        for k, arr in zip(names, gathered):
            if k in COL_ALIGNED:
                full[k] = arr.reshape(arr.shape[0] * arr.shape[1], arr.shape[2])
            elif k in COL_SHARDED:
                full[k] = _cols_from_shards(arr.reshape((N_CHIPS, arr.shape[1] * arr.shape[2], arr.shape[3])))
            else:
                full[k] = arr.reshape(-1, arr.shape[3])
    per_chip = [_unpack(small_all[k], small_offs, [a.shape for a in small_shards]) for k in range(N_CHIPS)]
    lru_conv_w = jnp.concatenate([per_chip[k][0] for k in range(N_CHIPS)], axis=-1)[0]
    sconv_w = jnp.concatenate([per_chip[k][1] for k in range(N_CHIPS)], axis=-1)[0]
    ffn_conv_w = jnp.concatenate([per_chip[k][2] for k in range(N_CHIPS)], axis=-1)
    mix1_norm = jnp.concatenate([per_chip[k][3] for k in range(N_CHIPS)], axis=-1)
    sgu_norm = jnp.concatenate([per_chip[k][4] for k in range(N_CHIPS)], axis=-1)

    nat = {
        "mix0_w_in": full["mix0_w_in"], "mix0_w_out": full["mix0_w_out"],
        "mix1_w_in": full["mix1_w_in"], "mix1_w_out": full["mix1_w_out"],
        "ffn_up": [full["ffn_up0"], full["ffn_up1"]], "ffn_down": [full["ffn_down0"], full["ffn_down1"]],
        "lru_conv_w": lru_conv_w, "sconv_w": sconv_w, "ffn_conv_w": ffn_conv_w, "mix1_norm": mix1_norm,
        "sgu_norm": sgu_norm,
    }
    for k in SMALL_REPLICATED:
        nat[k] = w[k]
    wts = _prepare_weights(nat)

    reducer = _GradReducer()

    def on_event(name, g, token):
        if name == "dwup1":
            reducer.start("ffn1", {"ffn_up1": g["w_up1"], "ffn_down1": g["w_down1"]})
        elif name in ("dwout1", "fox_bwd"):
            reducer.step("ffn1", token)
        elif name == "dwin1":
            reducer.step("ffn1", token)
            reducer.start("mix1", {"mix1_w_in": g["w_in1"][:, :2568], "mix1_w_out": g["w_out1"]})
        elif name in ("dwdown0", "ffn_bwd0"):
            reducer.step("mix1", token)
        elif name == "dwup0":
            reducer.step("mix1", token)
            reducer.start("ffn0", {"ffn_up0": g["w_up0"], "ffn_down0": g["w_down0"]})
        elif name in ("dwout0", "even_bwd"):
            reducer.step("ffn0", token)
        elif name == "dwin0":
            reducer.step("ffn0", token)
            reducer.start("mix0", {"mix0_w_in": _unblock_cols(g["w_in0"], 5, 4), "mix0_w_out": g["w_out0"]})

    loss, grad_x, g = _local_step(x2, t2, wts, on_event)
    grads_small, _ = _natural_grads(g)

    small_names = SMALL_REPLICATED + SMALL_SHARDED
    small_list = [grads_small[k] for k in small_names] + [loss[:, :1]]
    sbuf, soffs = _pack(small_list)
    sred = _all_reduce_small(sbuf, name="reduce_small")
    small_red = _unpack(sred, soffs, [a.shape for a in small_list])
    loss_total = small_red[-1][0, 0]
    gsum = dict(zip(small_names, small_red[:-1]))
    for k in SMALL_SHARDED:
        width = w[k].shape[-1]
        gsum[k] = lax.dynamic_slice_in_dim(gsum[k], chip * width, width, axis=gsum[k].ndim - 1)

    out_g, out_d, out_m, out_v = {}, {}, {}, {}
    params = {
        "mix0_w_in": ("mix0_w_in", 0), "mix0_w_out": ("mix0_w_out", 0),
        "mix1_w_in": ("mix1_w_in", 0), "mix1_w_out": ("mix1_w_out", 0),
        "ffn_up0": ("ffn_up", 0), "ffn_up1": ("ffn_up", 1), "ffn_down0": ("ffn_down", 0), "ffn_down1": ("ffn_down", 1),
    }
    updated = {}

    def update_group(group):
        token = None
        for k, (mine, theirs) in reducer.result(group).items():
            pname, layer = params[k]
            shape = (2 * mine.shape[0], mine.shape[1])
            w2, m2, v2 = (a[pname][layer].reshape(shape) for a in (w, m, v))
            updated[k] = _adamw_halves(w2, mine, theirs, m2, v2, core_arr, name=f"adamw_{k}")
            token = updated[k][1]
        return token

    reducer.step("mix0", update_group("ffn1"))
    small_w = [w[k] for k in small_names]
    pg, offs = _pack([gsum[k] for k in small_names])
    pw, _ = _pack(small_w)
    pm, _ = _pack([m[k] for k in small_names])
    pv, _ = _pack([v[k] for k in small_names])
    sd, sm, sv = _adamw(pw, pg, pm, pv, name="adamw_small")
    reducer.step("mix0", update_group("mix1"))
    reducer.step("mix0", update_group("ffn0"))
    update_group("mix0")

    shapes = [a.shape for a in small_w]
    for k, dd, mm, vv in zip(small_names, _unpack(sd, offs, shapes), _unpack(sm, offs, shapes),
                             _unpack(sv, offs, shapes)):
        out_g[k], out_d[k], out_m[k], out_v[k] = gsum[k].reshape(w[k].shape), dd, mm, vv
    for k in ("mix0_w_in", "mix0_w_out", "mix1_w_in", "mix1_w_out"):
        shp = w[k].shape
        out_g[k], out_d[k], out_m[k], out_v[k] = (a.reshape(shp) for a in updated[k])
    for k in ("ffn_up", "ffn_down"):
        shp = w[k].shape[1:]
        out_g[k], out_d[k], out_m[k], out_v[k] = (
            jnp.stack([updated[f"{k}0"][i].reshape(shp), updated[f"{k}1"][i].reshape(shp)]) for i in range(4))

    outs = [loss_total, grad_x[None]]
    for d in (out_g, out_d, out_m, out_v):
        outs.extend(d[k] for k in WEIGHT_ORDER)
    return tuple(outs)


def kernel(x, mix0_norm, mix0_w_in, lru_conv_w, lru_conv_b, lru_wa, lru_ba, lru_wx, lru_bx, lru_lambda, sconv_w, sconv_b, mix0_w_out, mix1_norm, mix1_w_in, sgu_norm, sgu_w, sgu_b, fox_bf, mix1_w_out, ffn_norm, ffn_up, ffn_conv_w, ffn_conv_b, ffn_down, final_norm, loss_target, m_mix0_norm, m_mix0_w_in, m_lru_conv_w, m_lru_conv_b, m_lru_wa, m_lru_ba, m_lru_wx, m_lru_bx, m_lru_lambda, m_sconv_w, m_sconv_b, m_mix0_w_out, m_mix1_norm, m_mix1_w_in, m_sgu_norm, m_sgu_w, m_sgu_b, m_fox_bf, m_mix1_w_out, m_ffn_norm, m_ffn_up, m_ffn_conv_w, m_ffn_conv_b, m_ffn_down, m_final_norm, v_mix0_norm, v_mix0_w_in, v_lru_conv_w, v_lru_conv_b, v_lru_wa, v_lru_ba, v_lru_wx, v_lru_bx, v_lru_lambda, v_sconv_w, v_sconv_b, v_mix0_w_out, v_mix1_norm, v_mix1_w_in, v_sgu_norm, v_sgu_w, v_sgu_b, v_fox_bf, v_mix1_w_out, v_ffn_norm, v_ffn_up, v_ffn_conv_w, v_ffn_conv_b, v_ffn_down, v_final_norm):
    w = dict(zip(WEIGHT_ORDER, (mix0_norm, mix0_w_in, lru_conv_w, lru_conv_b, lru_wa, lru_ba, lru_wx, lru_bx, lru_lambda, sconv_w, sconv_b, mix0_w_out, mix1_norm, mix1_w_in, sgu_norm, sgu_w, sgu_b, fox_bf, mix1_w_out, ffn_norm, ffn_up, ffn_conv_w, ffn_conv_b, ffn_down, final_norm)))
    m = dict(zip(WEIGHT_ORDER, (m_mix0_norm, m_mix0_w_in, m_lru_conv_w, m_lru_conv_b, m_lru_wa, m_lru_ba, m_lru_wx, m_lru_bx, m_lru_lambda, m_sconv_w, m_sconv_b, m_mix0_w_out, m_mix1_norm, m_mix1_w_in, m_sgu_norm, m_sgu_w, m_sgu_b, m_fox_bf, m_mix1_w_out, m_ffn_norm, m_ffn_up, m_ffn_conv_w, m_ffn_conv_b, m_ffn_down, m_final_norm)))
    v = dict(zip(WEIGHT_ORDER, (v_mix0_norm, v_mix0_w_in, v_lru_conv_w, v_lru_conv_b, v_lru_wa, v_lru_ba, v_lru_wx, v_lru_bx, v_lru_lambda, v_sconv_w, v_sconv_b, v_mix0_w_out, v_mix1_norm, v_mix1_w_in, v_sgu_norm, v_sgu_w, v_sgu_b, v_fox_bf, v_mix1_w_out, v_ffn_norm, v_ffn_up, v_ffn_conv_w, v_ffn_conv_b, v_ffn_down, v_final_norm)))
    return _train_step(x, loss_target, w, m, v)
```

```python
import functools

import jax
import jax.numpy as jnp
from jax import lax
from jax.experimental import pallas as pl
from jax.experimental.pallas import tpu as pltpu
from jax.experimental.pallas import tpu_sc as plsc

F32 = jnp.float32
BF16 = jnp.bfloat16
MESH = pl.DeviceIdType.MESH

D_MODEL = 1024
LANES = 128
SUBLANES = 8
N_CHIPS = 4
EPS = 1e-6
LRU_C = 8.0
D_FF = 2816
FFN_CB = 256
CHUNK = 128
NEG = -1e30

ADAM_LR = 0.001
ADAM_B1 = 0.9
ADAM_B2 = 0.999
ADAM_EPS = 1e-08
ADAM_WD = 0.01
ADAM_STEP = 10
ADAM_C1 = 1.0 - ADAM_B1 ** ADAM_STEP
ADAM_C2 = 1.0 - ADAM_B2 ** ADAM_STEP

_GELU_C = 0.7978845608028654
_GELU_A = 0.044715


def _sigmoid(x):
    return 1.0 / (1.0 + jnp.exp(-x))


def _sigmoid_tanh(x):
    return 0.5 * jnp.tanh(0.5 * x) + 0.5


def _log1p_pos(e):
    w = 1.0 + e
    return jnp.where(w == 1.0, e, jnp.log(w) * (e / (w - 1.0)))


def _softplus(x):
    return jnp.maximum(x, 0.0) + _log1p_pos(jnp.exp(-jnp.abs(x)))


def _gelu(x):
    t = jnp.tanh(_GELU_C * (x + _GELU_A * (x * x * x)))
    return 0.5 * x * (1.0 + t), t


def _gelu_grad(x, t):
    return 0.5 * (1.0 + t) + 0.5 * x * (1.0 - t * t) * (_GELU_C * (1.0 + 3.0 * _GELU_A * x * x))


def _rows(shape):
    return lax.broadcasted_iota(jnp.int32, shape, 0)


def _lanes(shape):
    return lax.broadcasted_iota(jnp.int32, shape, 1)


def _shift_down(x, halo8, j):
    if j == 0:
        return x
    r = pltpu.roll(x, j, 0)
    hr = pltpu.roll(halo8, j, 0)
    top = jnp.where(_rows(hr.shape) < j, hr, r[:SUBLANES])
    return jnp.concatenate([top, r[SUBLANES:]], axis=0)


def _shift_up(x, next8, j):
    if j == 0:
        return x
    n = x.shape[0]
    r = pltpu.roll(x, n - j, 0)
    nr = pltpu.roll(next8, SUBLANES - j, 0)
    bot = jnp.where(_rows(nr.shape) >= SUBLANES - j, nr, r[n - SUBLANES:])
    return jnp.concatenate([r[:n - SUBLANES], bot], axis=0)


def _scan_fwd(a, u):
    n = a.shape[0]
    row = _rows(a.shape)
    h = u
    k = 1
    while k < n:
        keep = row >= k
        h_sh = jnp.where(keep, pltpu.roll(h, k, 0), 0.0)
        a_sh = jnp.where(keep, pltpu.roll(a, k, 0), 1.0)
        h = a * h_sh + h
        a = a * a_sh
        k *= 2
    return h, a


def _scan_rev(b, d):
    n = b.shape[0]
    row = _rows(b.shape)
    g = d
    k = 1
    while k < n:
        keep = row < n - k
        g_sh = jnp.where(keep, pltpu.roll(g, n - k, 0), 0.0)
        b_sh = jnp.where(keep, pltpu.roll(b, n - k, 0), 1.0)
        g = b * g_sh + g
        b = b * b_sh
        k *= 2
    return g, b


def _cumsum_fwd(x):
    n = x.shape[0]
    row = _rows(x.shape)
    k = 1
    while k < n:
        x = x + jnp.where(row >= k, pltpu.roll(x, k, 0), 0.0)
        k *= 2
    return x


def _cumsum_rev(x):
    n = x.shape[0]
    row = _rows(x.shape)
    k = 1
    while k < n:
        x = x + jnp.where(row < n - k, pltpu.roll(x, n - k, 0), 0.0)
        k *= 2
    return x


def _dot(a, b):
    return lax.dot_general(a, b, (((1,), (0,)), ((), ())), preferred_element_type=F32)


def _dot_nt(a, b):
    return lax.dot_general(a, b, (((1,), (1,)), ((), ())), preferred_element_type=F32)


def _dot_tn(a, b):
    return lax.dot_general(a, b, (((0,), (0,)), ((), ())), preferred_element_type=F32)


def _dot_split(x, m_bf16):
    hi = x.astype(BF16)
    lo = (x - hi.astype(F32)).astype(BF16)
    return _dot(hi, m_bf16) + _dot(lo, m_bf16)


def _tile_rows(ts, s):
    return min(ts, s)


def _mm(a_list, w, *, trans_w=False, res=None, out_dtype=F32, ts=512, nb=None, name):
    s = a_list[0].shape[0]
    ks = [a.shape[1] for a in a_list]
    k = sum(ks)
    n = w.shape[0] if trans_w else w.shape[1]
    ts = _tile_rows(ts, s)
    nb = n if nb is None else nb
    na = len(a_list)
    has_res = res is not None

    def body(*refs):
        a_refs = refs[:na]
        w_ref = refs[na]
        o_ref = refs[-1]
        parts = [r[...].astype(BF16) for r in a_refs]
        a = parts[0] if na == 1 else jnp.concatenate(parts, axis=1)
        acc = _dot_nt(a, w_ref[...]) if trans_w else _dot(a, w_ref[...])
        if has_res:
            acc = acc + refs[na + 1][...]
        o_ref[...] = acc.astype(out_dtype)

    in_specs = [pl.BlockSpec((ts, kk), lambda j, i: (i, 0)) for kk in ks]
    if trans_w:
        in_specs.append(pl.BlockSpec((nb, k), lambda j, i: (j, 0)))
    else:
        in_specs.append(pl.BlockSpec((k, nb), lambda j, i: (0, j)))
    args = list(a_list) + [w]
    if has_res:
        in_specs.append(pl.BlockSpec((ts, nb), lambda j, i: (i, j)))
        args.append(res)
    return pl.pallas_call(
        body, name=name, grid=(n // nb, s // ts), in_specs=in_specs,
        out_specs=pl.BlockSpec((ts, nb), lambda j, i: (i, j)),
        out_shape=jax.ShapeDtypeStruct((s, n), out_dtype),
    )(*args)


def _mm_tn(a_list, b_list, *, ts=512, nb=None, name):
    s = b_list[0].shape[0]
    ks = [a.shape[1] for a in a_list]
    k = sum(ks)
    width = b_list[0].shape[1]
    n = width * len(b_list)
    ts = _tile_rows(ts, s)
    nb = width if nb is None else nb
    per = width // nb
    na = len(a_list)
    nparts = len(b_list)

    def body(*refs):
        a_refs = refs[:na]
        b_refs = refs[na:na + nparts]
        o_ref = refs[-1]
        j = pl.program_id(0)
        i = pl.program_id(1)
        parts = [r[...].astype(BF16) for r in a_refs]
        a = parts[0] if na == 1 else jnp.concatenate(parts, axis=1)

        def accumulate(b_ref):
            upd = _dot_tn(a, b_ref[...].astype(BF16))

            @pl.when(i == 0)
            def _():
                o_ref[...] = upd

            @pl.when(i > 0)
            def _():
                o_ref[...] += upd

        if nparts == 1:
            accumulate(b_refs[0])
        else:
            for part, b_ref in enumerate(b_refs):
                pl.when(j // per == part)(functools.partial(accumulate, b_ref))

    in_specs = [pl.BlockSpec((ts, kk), lambda j, i: (i, 0)) for kk in ks]
    for part in range(nparts):
        in_specs.append(pl.BlockSpec(
            (ts, nb), lambda j, i, part=part: (i, jnp.clip(j - part * per, 0, per - 1))))
    return pl.pallas_call(
        body, name=name, grid=(n // nb, s // ts), in_specs=in_specs,
        out_specs=pl.BlockSpec((k, nb), lambda j, i: (0, j)),
        out_shape=jax.ShapeDtypeStruct((k, n), F32),
    )(*a_list, *b_list)


def _norm_fwd(h, g, *, ts=512, name):
    s, d = h.shape
    ts = _tile_rows(ts, s)

    def body(h_ref, g_ref, n_ref):
        x = h_ref[...]
        r = lax.rsqrt(jnp.mean(x * x, axis=-1, keepdims=True) + EPS)
        n_ref[...] = ((x * r) * g_ref[...]).astype(BF16)

    return pl.pallas_call(
        body, name=name, grid=(s // ts,),
        in_specs=[pl.BlockSpec((ts, d), lambda i: (i, 0)), pl.BlockSpec((1, d), lambda i: (0, 0))],
        out_specs=pl.BlockSpec((ts, d), lambda i: (i, 0)),
        out_shape=jax.ShapeDtypeStruct((s, d), BF16),
    )(h, g)


def _norm_bwd(dn, h, g, dres, *, ts=512, name):
    s, d = h.shape
    ts = _tile_rows(ts, s)

    def body(dn_ref, h_ref, g_ref, dres_ref, dh_ref, dg_ref):
        i = pl.program_id(0)
        x = h_ref[...]
        dnv = dn_ref[...]
        r = lax.rsqrt(jnp.mean(x * x, axis=-1, keepdims=True) + EPS)
        xhat = x * r
        part = jnp.sum(dnv * xhat, axis=0, keepdims=True)

        @pl.when(i == 0)
        def _():
            dg_ref[...] = part

        @pl.when(i > 0)
        def _():
            dg_ref[...] += part

        dxh = dnv * g_ref[...]
        dh_ref[...] = dres_ref[...] + r * (dxh - xhat * jnp.mean(dxh * xhat, axis=-1, keepdims=True))

    tile = pl.BlockSpec((ts, d), lambda i: (i, 0))
    vec = pl.BlockSpec((1, d), lambda i: (0, 0))
    return pl.pallas_call(
        body, name=name, grid=(s // ts,), in_specs=[tile, tile, vec, tile],
        out_specs=(tile, vec),
        out_shape=(jax.ShapeDtypeStruct((s, d), F32), jax.ShapeDtypeStruct((1, d), F32)),
    )(dn, h, g, dres)


def _final(h, g, target, *, ts=512, name):
    s, d = h.shape
    ts = _tile_rows(ts, s)
    nt = s // ts

    def body(h_ref, g_ref, t_ref, dh_ref, loss_ref, dg_ref, acc_ref):
        i = pl.program_id(0)
        x = h_ref[...]
        r = lax.rsqrt(jnp.mean(x * x, axis=-1, keepdims=True) + EPS)
        xhat = x * r
        gv = g_ref[...]
        err = xhat * gv - t_ref[...]
        sq = jnp.sum(err * err, axis=0, keepdims=True)
        dy = err * (1.0 / d)
        part = jnp.sum(dy * xhat, axis=0, keepdims=True)

        @pl.when(i == 0)
        def _():
            acc_ref[...] = sq
            dg_ref[...] = part

        @pl.when(i > 0)
        def _():
            acc_ref[...] += sq
            dg_ref[...] += part

        dxh = dy * gv
        dh_ref[...] = r * (dxh - xhat * jnp.mean(dxh * xhat, axis=-1, keepdims=True))

        @pl.when(i == nt - 1)
        def _():
            tot = jnp.sum(acc_ref[...], axis=1, keepdims=True) * (0.5 / d)
            loss_ref[...] = jnp.broadcast_to(tot, (1, LANES))

    tile = pl.BlockSpec((ts, d), lambda i: (i, 0))
    vec = pl.BlockSpec((1, d), lambda i: (0, 0))
    return pl.pallas_call(
        body, name=name, grid=(nt,), in_specs=[tile, vec, tile],
        out_specs=(tile, pl.BlockSpec((1, LANES), lambda i: (0, 0)), vec),
        out_shape=(jax.ShapeDtypeStruct((s, d), F32), jax.ShapeDtypeStruct((1, LANES), F32),
                   jax.ShapeDtypeStruct((1, d), F32)),
        scratch_shapes=[pltpu.VMEM((1, d), F32)],
    )(h, g, target)


def _halo_map(ts, width_blocks):
    per = ts // SUBLANES

    def index(j, i):
        return (jnp.maximum(i * per - 1, 0), width_blocks(j))

    return index


def _even_gates(xc, wa, ba, wx, bx, sp):
    xb = xc.astype(BF16)
    r = _sigmoid(_dot(xb, wa) + ba)
    ig = _sigmoid(_dot(xb, wx) + bx)
    la = (-LRU_C) * r * sp
    a = jnp.exp(la)
    a2 = a * a
    m = jnp.sqrt(-jnp.tanh(la) * (1.0 + a2))
    return r, ig, la, a, a2, m


def _even_core_fwd(p, w4, b4, wa, ba, wx, bx, lam, w3, b3, *, ts=512, name):
    s = p.shape[0]
    ts = _tile_rows(ts, s)
    nt = s // ts
    nblk = 4

    def body(p_ref, ph_ref, w4_ref, b4_ref, wa_ref, ba_ref, wx_ref, bx_ref, lam_ref, w3_ref, b3_ref,
             ya_ref, yb_ref, hl_ref, hcar_ref):
        i = pl.program_id(1)
        first = (i > 0).astype(F32)
        xa = p_ref[:, 0:LANES]
        ga = p_ref[:, LANES:2 * LANES]
        cp = p_ref[:, 2 * LANES:3 * LANES]
        bp = p_ref[:, 3 * LANES:4 * LANES]
        vb = p_ref[:, 4 * LANES:5 * LANES]
        xa_h = ph_ref[:, 0:LANES] * first
        s_h = ph_ref[:, 2 * LANES:3 * LANES] * ph_ref[:, 4 * LANES:5 * LANES] * first

        xc = b4_ref[...] + w4_ref[3:4, :] * xa
        for k in range(3):
            xc = xc + w4_ref[k:k + 1, :] * _shift_down(xa, xa_h, 3 - k)
        sp = _softplus(-lam_ref[...])
        _, ig, _, a, _, m = _even_gates(xc, wa_ref[0], ba_ref[...], wx_ref[0], bx_ref[...], sp)
        u = m * (ig * xc)
        hs, acum = _scan_fwd(a, u)

        @pl.when(i == 0)
        def _():
            hcar_ref[...] = jnp.zeros_like(hcar_ref)

        hs = hs + acum * hcar_ref[0:1, :]
        hl_ref[...] = hs
        hcar_ref[0:1, :] = hl_ref[ts - 1:ts, :]
        ge, _ = _gelu(ga)
        ya_ref[...] = (hs * ge).astype(BF16)

        sv = cp * vb
        sc = b3_ref[...] + w3_ref[2:3, :] * sv
        for k in range(2):
            sc = sc + w3_ref[k:k + 1, :] * _shift_down(sv, s_h, 2 - k)
        yb_ref[...] = (bp * sc).astype(BF16)

    blk = pl.BlockSpec((ts, 5 * LANES), lambda j, i: (i, j))
    halo = pl.BlockSpec((SUBLANES, 5 * LANES), _halo_map(ts, lambda j: j))
    vec = pl.BlockSpec((1, LANES), lambda j, i: (0, j))
    out = pl.BlockSpec((ts, LANES), lambda j, i: (i, j))
    return pl.pallas_call(
        body, name=name, grid=(nblk, nt),
        in_specs=[blk, halo,
                  pl.BlockSpec((4, LANES), lambda j, i: (0, j)), vec,
                  pl.BlockSpec((1, LANES, LANES), lambda j, i: (j, 0, 0)), vec,
                  pl.BlockSpec((1, LANES, LANES), lambda j, i: (j, 0, 0)), vec, vec,
                  pl.BlockSpec((3, LANES), lambda j, i: (0, j)), vec],
        out_specs=(out, out, out),
        out_shape=(jax.ShapeDtypeStruct((s, 4 * LANES), BF16), jax.ShapeDtypeStruct((s, 4 * LANES), BF16),
                   jax.ShapeDtypeStruct((s, 4 * LANES), F32)),
        scratch_shapes=[pltpu.VMEM((SUBLANES, LANES), F32)],
    )(p, p, w4, b4, wa, ba, wx, bx, lam, w3, b3)


def _even_core_bwd(dy, p, hl, w4, b4, wa, wat, ba, wx, wxt, bx, lam, w3, b3, *, ts=256, name):
    s = p.shape[0]
    ts = _tile_rows(ts, s)
    nt = s // ts
    nblk = 4
    per = ts // SUBLANES

    def body(dya_ref, dyb_ref, p_ref, ph_ref, hl_ref, hh_ref,
             w4_ref, b4_ref, wa_ref, wat_ref, ba_ref, wx_ref, wxt_ref, bx_ref, lam_ref, w3_ref, b3_ref,
             dp_ref, dw4_ref, db4_ref, dwa_ref, dba_ref, dwx_ref, dbx_ref, dlam_ref, dw3_ref, db3_ref,
             dxc_nx, dsc_nx, cg_ref):
        i = pl.program_id(1)
        ti = nt - 1 - i
        first = (ti > 0).astype(F32)
        xa = p_ref[:, 0:LANES]
        ga = p_ref[:, LANES:2 * LANES]
        cp = p_ref[:, 2 * LANES:3 * LANES]
        bp = p_ref[:, 3 * LANES:4 * LANES]
        vb = p_ref[:, 4 * LANES:5 * LANES]
        xa_h = ph_ref[:, 0:LANES] * first
        s_h = ph_ref[:, 2 * LANES:3 * LANES] * ph_ref[:, 4 * LANES:5 * LANES] * first
        h_h = hh_ref[...] * first

        @pl.when(i == 0)
        def _():
            dxc_nx[...] = jnp.zeros_like(dxc_nx)
            dsc_nx[...] = jnp.zeros_like(dsc_nx)
            cg_ref[...] = jnp.zeros_like(cg_ref)
            for ref in (dw4_ref, db4_ref, dwa_ref, dba_ref, dwx_ref, dbx_ref, dlam_ref, dw3_ref, db3_ref):
                ref[...] = jnp.zeros_like(ref)

        xa_sh = [_shift_down(xa, xa_h, 3 - k) for k in range(3)] + [xa]
        xc = b4_ref[...]
        for k in range(4):
            xc = xc + w4_ref[k:k + 1, :] * xa_sh[k]
        lamv = lam_ref[...]
        sp = _softplus(-lamv)
        r, ig, _, a, a2, m = _even_gates(xc, wa_ref[0], ba_ref[...], wx_ref[0], bx_ref[...], sp)
        sv = cp * vb
        sv_sh = [_shift_down(sv, s_h, 2 - k) for k in range(2)] + [sv]
        sc = b3_ref[...]
        for k in range(3):
            sc = sc + w3_ref[k:k + 1, :] * sv_sh[k]
        hs = hl_ref[...]
        h_prev = _shift_down(hs, h_h, 1)

        dya = dya_ref[...]
        dyb = dyb_ref[...]
        ge, gt = _gelu(ga)
        dga = dya * hs * _gelu_grad(ga, gt)
        dh = dya * ge

        ones8 = jnp.ones((SUBLANES, LANES), F32)
        b = _shift_up(a, ones8, 1)
        g, bcum = _scan_rev(b, dh)
        g = g + bcum * cg_ref[0:1, :]
        ag = a * g
        cg_ref[...] = ag[:SUBLANES]

        da = g * h_prev
        xi = ig * xc
        dm = g * xi
        dig = g * m * xc
        dxc = g * m * ig
        dla = da * a - dm * (a2 / m)
        dr = dla * ((-LRU_C) * sp)
        dlam_ref[...] += jnp.sum(dla * r, axis=0, keepdims=True) * (LRU_C * _sigmoid(-lamv))
        dra = dr * r * (1.0 - r)
        dia = dig * ig * (1.0 - ig)
        drab = dra.astype(BF16)
        diab = dia.astype(BF16)
        xcb = xc.astype(BF16)
        dxc = dxc + _dot(drab, wat_ref[0]) + _dot(diab, wxt_ref[0])
        dwa_ref[0] += _dot_tn(xcb, drab)
        dwx_ref[0] += _dot_tn(xcb, diab)
        dba_ref[...] += jnp.sum(dra, axis=0, keepdims=True)
        dbx_ref[...] += jnp.sum(dia, axis=0, keepdims=True)

        nx = dxc_nx[...]
        dxa = w4_ref[3:4, :] * dxc
        for k in range(3):
            dxa = dxa + w4_ref[k:k + 1, :] * _shift_up(dxc, nx, 3 - k)
        for k in range(4):
            dw4_ref[k:k + 1, :] += jnp.sum(dxc * xa_sh[k], axis=0, keepdims=True)
        db4_ref[...] += jnp.sum(dxc, axis=0, keepdims=True)
        dxc_nx[...] = dxc[:SUBLANES]

        dbp = dyb * sc
        dsc = dyb * bp
        nsc = dsc_nx[...]
        ds = w3_ref[2:3, :] * dsc
        for k in range(2):
            ds = ds + w3_ref[k:k + 1, :] * _shift_up(dsc, nsc, 2 - k)
        for k in range(3):
            dw3_ref[k:k + 1, :] += jnp.sum(dsc * sv_sh[k], axis=0, keepdims=True)
        db3_ref[...] += jnp.sum(dsc, axis=0, keepdims=True)
        dsc_nx[...] = dsc[:SUBLANES]

        dp_ref[:, 0:LANES] = dxa.astype(BF16)
        dp_ref[:, LANES:2 * LANES] = dga.astype(BF16)
        dp_ref[:, 2 * LANES:3 * LANES] = (ds * vb).astype(BF16)
        dp_ref[:, 3 * LANES:4 * LANES] = dbp.astype(BF16)
        dp_ref[:, 4 * LANES:5 * LANES] = (ds * cp).astype(BF16)

    def rev(j, i):
        return (nt - 1 - i, j)

    def rev_halo(col):
        def index(j, i):
            return (jnp.maximum((nt - 1 - i) * per - 1, 0), col(j))
        return index

    blk = pl.BlockSpec((ts, 5 * LANES), rev)
    one = pl.BlockSpec((ts, LANES), rev)
    vec = pl.BlockSpec((1, LANES), lambda j, i: (0, j))
    mat = pl.BlockSpec((1, LANES, LANES), lambda j, i: (j, 0, 0))
    w4s = pl.BlockSpec((4, LANES), lambda j, i: (0, j))
    w3s = pl.BlockSpec((3, LANES), lambda j, i: (0, j))
    f = jax.ShapeDtypeStruct
    return pl.pallas_call(
        body, name=name, grid=(nblk, nt),
        in_specs=[one, pl.BlockSpec((ts, LANES), lambda j, i: (nt - 1 - i, 4 + j)),
                  blk, pl.BlockSpec((SUBLANES, 5 * LANES), rev_halo(lambda j: j)),
                  one, pl.BlockSpec((SUBLANES, LANES), rev_halo(lambda j: j)),
                  w4s, vec, mat, mat, vec, mat, mat, vec, vec, w3s, vec],
        out_specs=(blk, w4s, vec, mat, vec, mat, vec, vec, w3s, vec),
        out_shape=(f((s, 20 * LANES), BF16), f((4, 4 * LANES), F32), f((1, 4 * LANES), F32),
                   f((4, LANES, LANES), F32), f((1, 4 * LANES), F32),
                   f((4, LANES, LANES), F32), f((1, 4 * LANES), F32), f((1, 4 * LANES), F32),
                   f((3, 4 * LANES), F32), f((1, 4 * LANES), F32)),
        scratch_shapes=[pltpu.VMEM((SUBLANES, LANES), F32), pltpu.VMEM((SUBLANES, LANES), F32),
                        pltpu.VMEM((SUBLANES, LANES), F32)],
    )(dy, dy, p, p, hl, hl, w4, b4, wa, wat, ba, wx, wxt, bx, lam, w3, b3)


def _ffn_conv(u_ref, uh_ref, w_ref, b_ref, first):
    u = u_ref[...].astype(F32)
    u_h = uh_ref[...].astype(F32)[SUBLANES:] * first
    u_sh = [_shift_down(u, u_h, 2 - k) for k in range(2)] + [u]
    hc = b_ref[...]
    for k in range(3):
        hc = hc + w_ref[k:k + 1, :] * u_sh[k]
    return hc, u_sh


def _ffn_specs(ts, row, halo_row):
    nblk = D_FF // FFN_CB
    specs = []
    for off in (0, nblk):
        specs.append(pl.BlockSpec((ts, FFN_CB), lambda j, i, off=off: (row(i), off + j)))
        specs.append(pl.BlockSpec((16, FFN_CB), lambda j, i, off=off: (halo_row(i), off + j)))
        specs.append(pl.BlockSpec((3, FFN_CB), lambda j, i, off=off: (0, off + j)))
        specs.append(pl.BlockSpec((1, FFN_CB), lambda j, i, off=off: (0, off + j)))
    return specs


def _ffn_core_fwd(up, w, b, *, ts=512, name):
    s = up.shape[0]
    ts = _tile_rows(ts, s)
    nt = s // ts
    nblk = D_FF // FFN_CB
    per = ts // 16

    def body(g_ref, gh_ref, wg_ref, bg_ref, v_ref, vh_ref, wv_ref, bv_ref, act_ref):
        first = (pl.program_id(1) > 0).astype(F32)
        gate, _ = _ffn_conv(g_ref, gh_ref, wg_ref, bg_ref, first)
        val, _ = _ffn_conv(v_ref, vh_ref, wv_ref, bv_ref, first)
        act_ref[...] = (gate * _sigmoid_tanh(gate) * val).astype(BF16)

    return pl.pallas_call(
        body, name=name, grid=(nblk, nt),
        in_specs=_ffn_specs(ts, lambda i: i, lambda i: jnp.maximum(i * per - 1, 0)),
        out_specs=pl.BlockSpec((ts, FFN_CB), lambda j, i: (i, j)),
        out_shape=jax.ShapeDtypeStruct((s, D_FF), BF16),
    )(up, up, w, b, up, up, w, b)


def _ffn_core_bwd(dact, up, w, b, *, ts=512, name):
    s = up.shape[0]
    ts = _tile_rows(ts, s)
    nt = s // ts
    nblk = D_FF // FFN_CB
    per = ts // 16

    def conv_bwd(dhc, u_sh, w_ref, nx_ref, du_ref, dw_ref, db_ref):
        nx = nx_ref[...]
        du = w_ref[2:3, :] * dhc
        for k in range(2):
            du = du + w_ref[k:k + 1, :] * _shift_up(dhc, nx, 2 - k)
        du_ref[...] = du.astype(BF16)
        for k in range(3):
            dw_ref[k:k + 1, :] += jnp.sum(dhc * u_sh[k], axis=0, keepdims=True)
        db_ref[...] += jnp.sum(dhc, axis=0, keepdims=True)
        nx_ref[...] = dhc[:SUBLANES]

    def body(da_ref, g_ref, gh_ref, wg_ref, bg_ref, v_ref, vh_ref, wv_ref, bv_ref,
             dg_ref, dv_ref, dwg_ref, dwv_ref, dbg_ref, dbv_ref, nxg_ref, nxv_ref):
        i = pl.program_id(1)
        first = (nt - 1 - i > 0).astype(F32)
        gate, g_sh = _ffn_conv(g_ref, gh_ref, wg_ref, bg_ref, first)
        val, v_sh = _ffn_conv(v_ref, vh_ref, wv_ref, bv_ref, first)
        da = da_ref[...].astype(F32)
        sg = _sigmoid_tanh(gate)
        dgate = da * val * (sg * (1.0 + gate * (1.0 - sg)))
        dval = da * (gate * sg)

        @pl.when(i == 0)
        def _():
            for ref in (nxg_ref, nxv_ref, dwg_ref, dwv_ref, dbg_ref, dbv_ref):
                ref[...] = jnp.zeros_like(ref)

        conv_bwd(dgate, g_sh, wg_ref, nxg_ref, dg_ref, dwg_ref, dbg_ref)
        conv_bwd(dval, v_sh, wv_ref, nxv_ref, dv_ref, dwv_ref, dbv_ref)

    def rev(i):
        return nt - 1 - i

    tile = pl.BlockSpec((ts, FFN_CB), lambda j, i: (rev(i), j))
    w_out = pl.BlockSpec((3, FFN_CB), lambda j, i: (0, j))
    b_out = pl.BlockSpec((1, FFN_CB), lambda j, i: (0, j))
    f = jax.ShapeDtypeStruct
    return pl.pallas_call(
        body, name=name, grid=(nblk, nt),
        in_specs=[tile] + _ffn_specs(ts, rev, lambda i: jnp.maximum(rev(i) * per - 1, 0)),
        out_specs=(tile, tile, w_out, w_out, b_out, b_out),
        out_shape=(f((s, D_FF), BF16), f((s, D_FF), BF16), f((3, D_FF), F32), f((3, D_FF), F32),
                   f((1, D_FF), F32), f((1, D_FF), F32)),
        scratch_shapes=[pltpu.VMEM((SUBLANES, FFN_CB), F32), pltpu.VMEM((SUBLANES, FFN_CB), F32)],
    )(dact, up, up, w, b, up, up, w, b)


def _sgu_forward_block(zu, zg, gn, w_ref, bias, seg):
    u, tu = _gelu(zu)
    g, tg = _gelu(zg)
    ms = _dot_split(g * g, seg)
    rs = lax.rsqrt(ms + EPS)
    ghat = g * rs
    gv = ghat * gn
    gvb = gv.astype(BF16)
    lane = _lanes((CHUNK, LANES))
    chunks = []
    for c in range(zu.shape[0] // CHUNK):
        gc = gvb[c * CHUNK:(c + 1) * CHUNK]
        mix = jnp.where(lane < 64, _dot(w_ref[0], gc), _dot(w_ref[1], gc)) + bias
        chunks.append(mix)
    mixed = chunks[0] if len(chunks) == 1 else jnp.concatenate(chunks, axis=0)
    return u, tu, g, tg, rs, ghat, gvb, mixed


def _sgu_fwd(p1, gn, w, bias, seg, *, ts=512, name):
    s = p1.shape[0]
    ts = _tile_rows(ts, s)

    def body(zu_ref, zg_ref, gn_ref, w_ref, bias_ref, seg_ref, yc_ref):
        u, _, _, _, _, _, _, mixed = _sgu_forward_block(
            zu_ref[...], zg_ref[...], gn_ref[...], w_ref, bias_ref[...], seg_ref[...])
        yc_ref[...] = (u * mixed).astype(BF16)

    return pl.pallas_call(
        body, name=name, grid=(4, s // ts),
        in_specs=[pl.BlockSpec((ts, LANES), lambda j, i: (i, j)),
                  pl.BlockSpec((ts, LANES), lambda j, i: (i, 4 + j)),
                  pl.BlockSpec((1, LANES), lambda j, i: (0, j)),
                  pl.BlockSpec((2, CHUNK, CHUNK), lambda j, i: (j, 0, 0)),
                  pl.BlockSpec((CHUNK, LANES), lambda j, i: (0, j)),
                  pl.BlockSpec((LANES, LANES), lambda j, i: (0, 0))],
        out_specs=pl.BlockSpec((ts, LANES), lambda j, i: (i, j)),
        out_shape=jax.ShapeDtypeStruct((s, 4 * LANES), BF16),
    )(p1, p1, gn, w, bias, seg)


def _sgu_bwd(p1, dy, gn, w, wt, bias, seg, tril, *, ts=512, name):
    s = p1.shape[0]
    ts = _tile_rows(ts, s)
    nt = s // ts

    def body(zu_ref, zg_ref, dy_ref, gn_ref, w_ref, wt_ref, bias_ref, seg_ref, tril_ref,
             dzu_ref, dzg_ref, dw_ref, dbias_ref, dgn_ref):
        i = pl.program_id(1)
        zu = zu_ref[...]
        zg = zg_ref[...]
        gn_v = gn_ref[...]
        segv = seg_ref[...]
        u, tu, g, tg, rs, ghat, gvb, mixed = _sgu_forward_block(zu, zg, gn_v, w_ref, bias_ref[...], segv)
        dyv = dy_ref[...]
        du = dyv * mixed
        dmx = dyv * u

        @pl.when(i == 0)
        def _():
            dw_ref[...] = jnp.zeros_like(dw_ref)
            dbias_ref[...] = jnp.zeros_like(dbias_ref)
            dgn_ref[...] = jnp.zeros_like(dgn_ref)

        lane = _lanes((CHUNK, LANES))
        dgv_chunks = []
        dbias = jnp.zeros((CHUNK, LANES), F32)
        for c in range(ts // CHUNK):
            dmc = dmx[c * CHUNK:(c + 1) * CHUNK]
            gc = gvb[c * CHUNK:(c + 1) * CHUNK]
            dm_a = jnp.where(lane < 64, dmc, 0.0).astype(BF16)
            dm_b = jnp.where(lane >= 64, dmc, 0.0).astype(BF16)
            dw_ref[0] += _dot_nt(dm_a, gc)
            dw_ref[1] += _dot_nt(dm_b, gc)
            dgv_chunks.append(_dot(wt_ref[0], dm_a) + _dot(wt_ref[1], dm_b))
            dbias = dbias + dmc
        dbias_ref[...] += dbias
        dgv = dgv_chunks[0] if len(dgv_chunks) == 1 else jnp.concatenate(dgv_chunks, axis=0)
        dgn_ref[...] += jnp.sum(dgv * ghat, axis=0, keepdims=True)
        dgh = dgv * gn_v
        dg = rs * (dgh - ghat * _dot_split(dgh * ghat, segv))
        dzu_ref[...] = (du * _gelu_grad(zu, tu)).astype(BF16)
        dzg_ref[...] = (dg * _gelu_grad(zg, tg)).astype(BF16)

        @pl.when(i == nt - 1)
        def _():
            dw_ref[0] = dw_ref[0] * tril_ref[...]
            dw_ref[1] = dw_ref[1] * tril_ref[...]

    f = jax.ShapeDtypeStruct
    colj = pl.BlockSpec((ts, LANES), lambda j, i: (i, j))
    wsp = pl.BlockSpec((2, CHUNK, CHUNK), lambda j, i: (j, 0, 0))
    sq = pl.BlockSpec((LANES, LANES), lambda j, i: (0, 0))
    return pl.pallas_call(
        body, name=name, grid=(4, nt),
        in_specs=[colj, pl.BlockSpec((ts, LANES), lambda j, i: (i, 4 + j)), colj,
                  pl.BlockSpec((1, LANES), lambda j, i: (0, j)), wsp, wsp,
                  pl.BlockSpec((CHUNK, LANES), lambda j, i: (0, j)), sq, sq],
        out_specs=(colj, colj, wsp, pl.BlockSpec((CHUNK, LANES), lambda j, i: (0, j)),
                   pl.BlockSpec((1, LANES), lambda j, i: (0, j))),
        out_shape=(f((s, 4 * LANES), BF16), f((s, 4 * LANES), BF16), f((8, CHUNK, CHUNK), F32),
                   f((CHUNK, 4 * LANES), F32), f((1, 4 * LANES), F32)),
    )(p1, p1, dy, gn, w, wt, bias, seg, tril)


F_COL = 20


def _fcum_fwd(p1, bf, *, ts=512, name):
    s = p1.shape[0]
    ts = _tile_rows(ts, s)

    def body(f_ref, bf_ref, c_ref, car_ref):
        i = pl.program_id(0)
        z = f_ref[...] + bf_ref[...]
        logf = jnp.minimum(z, 0.0) - _log1p_pos(jnp.exp(-jnp.abs(z)))

        @pl.when(i == 0)
        def _():
            car_ref[...] = jnp.zeros_like(car_ref)

        c_ref[...] = _cumsum_fwd(logf) + car_ref[0:1, :]
        car_ref[0:1, :] = c_ref[ts - 1:ts, :]

    return pl.pallas_call(
        body, name=name, grid=(s // ts,),
        in_specs=[pl.BlockSpec((ts, LANES), lambda i: (i, F_COL)), pl.BlockSpec((1, LANES), lambda i: (0, 0))],
        out_specs=pl.BlockSpec((ts, LANES), lambda i: (i, 0)),
        out_shape=jax.ShapeDtypeStruct((s, LANES), F32),
        scratch_shapes=[pltpu.VMEM((SUBLANES, LANES), F32)],
    )(p1, bf)


def _fcum_bwd(dcs, dcq, p1, bf, *, ts=512, name):
    s = p1.shape[0]
    ts = _tile_rows(ts, s)
    nt = s // ts

    def body(dc_ref, dcq_ref, f_ref, bf_ref, df_ref, dbf_ref, car_ref):
        i = pl.program_id(0)

        @pl.when(i == 0)
        def _():
            car_ref[...] = jnp.zeros_like(car_ref)
            dbf_ref[...] = jnp.zeros_like(dbf_ref)

        dc = dc_ref[...]
        lane = _lanes((ts, LANES))
        for h in range(8):
            dc = dc + jnp.where(lane == h, dcq_ref[:, h * LANES:(h + 1) * LANES], 0.0)
        dlog = _cumsum_rev(dc) + car_ref[0:1, :]
        car_ref[...] = dlog[:SUBLANES]
        z = f_ref[...] + bf_ref[...]
        df = dlog * _sigmoid(-z)
        df_ref[...] = df.astype(BF16)
        dbf_ref[...] += jnp.sum(df, axis=0, keepdims=True)

    return pl.pallas_call(
        body, name=name, grid=(nt,),
        in_specs=[pl.BlockSpec((ts, LANES), lambda i: (nt - 1 - i, 0)),
                  pl.BlockSpec((ts, 8 * LANES), lambda i: (nt - 1 - i, 0)),
                  pl.BlockSpec((ts, LANES), lambda i: (nt - 1 - i, F_COL)),
                  pl.BlockSpec((1, LANES), lambda i: (0, 0))],
        out_specs=(pl.BlockSpec((ts, LANES), lambda i: (nt - 1 - i, 0)), pl.BlockSpec((1, LANES), lambda i: (0, 0))),
        out_shape=(jax.ShapeDtypeStruct((s, LANES), BF16), jax.ShapeDtypeStruct((1, LANES), F32)),
        scratch_shapes=[pltpu.VMEM((SUBLANES, LANES), F32)],
    )(dcs, dcq, p1, bf)


def _fox_scores(qm, kb, bias, ck, diagonal):
    sc = _dot_nt(qm, kb) + bias - ck
    if diagonal:
        sc = jnp.where(_lanes(sc.shape) <= _rows(sc.shape), sc, NEG)
    return sc


def _head_masks(shape):
    lane = _lanes(shape)
    return lane < 64, lane >= 64


def _fox_fwd(p1, cq, ck, *, tq=512, name):
    s = p1.shape[0]
    tq = _tile_rows(tq, s)
    tk = tq
    nq = s // tq

    def body(q_ref, k_ref, v_ref, cq_ref, ck_ref, o_ref, lb_ref):
        qi = pl.program_id(1)
        q = q_ref[...] * 0.125
        first, second = _head_masks((tq, LANES))
        qms = [jnp.where(sel, q, 0.0).astype(BF16) for sel in (first, second)]
        cqs = [cq_ref[:, hh * LANES:(hh + 1) * LANES] for hh in range(2)]
        biases = [jnp.tile(cqh, (1, tk // LANES)) for cqh in cqs]

        def step(kj, carry, diagonal):
            cols = pl.ds(pl.multiple_of(kj * tk, tk), tk)
            kb = k_ref[cols, :].astype(BF16)
            vb = v_ref[cols, :].astype(BF16)
            new, outs = [], []
            acc = carry[4]
            for hh in range(2):
                m_prev, l_prev = carry[2 * hh], carry[2 * hh + 1]
                sc = _fox_scores(qms[hh], kb, biases[hh], ck_ref[hh, :, cols], diagonal)
                m_new = jnp.maximum(m_prev, jnp.max(sc, axis=1, keepdims=True))
                pm = jnp.exp(sc - jnp.tile(m_new, (1, tk // LANES)))
                alpha = jnp.exp(m_prev - m_new)
                new += [m_new, alpha * l_prev + jnp.sum(pm, axis=1, keepdims=True)]
                outs.append(acc * alpha + _dot(pm.astype(BF16), vb))
            return tuple(new) + (jnp.where(first, outs[0], outs[1]),)

        zero = jnp.zeros((tq, LANES), F32)
        low = jnp.full((tq, LANES), NEG, F32)
        carry = lax.fori_loop(0, qi, lambda kj, c: step(kj, c, False), (low, zero, low, zero, zero))
        m0, l0, m1, l1, acc = step(qi, carry, True)
        o_ref[...] = (acc / jnp.where(first, l0, l1)).astype(BF16)
        lb_ref[:, 0:LANES] = cqs[0] - (m0 + jnp.log(l0))
        lb_ref[:, LANES:2 * LANES] = cqs[1] - (m1 + jnp.log(l1))

    return pl.pallas_call(
        body, name=name, grid=(4, nq),
        in_specs=[pl.BlockSpec((tq, LANES), lambda j, qi: (qi, 8 + j)),
                  pl.BlockSpec((s, LANES), lambda j, qi: (0, 12 + j)),
                  pl.BlockSpec((s, LANES), lambda j, qi: (0, 16 + j)),
                  pl.BlockSpec((tq, 2 * LANES), lambda j, qi: (qi, j)),
                  pl.BlockSpec((2, 1, s), lambda j, qi: (j, 0, 0))],
        out_specs=(pl.BlockSpec((tq, LANES), lambda j, qi: (qi, j)),
                   pl.BlockSpec((tq, 2 * LANES), lambda j, qi: (qi, j))),
        out_shape=(jax.ShapeDtypeStruct((s, 4 * LANES), BF16), jax.ShapeDtypeStruct((s, 8 * LANES), F32)),
    )(p1, p1, p1, cq, ck)


def _fox_delta(dy, o, sel, *, ts=512, name):
    s = o.shape[0]
    ts = _tile_rows(ts, s)

    def body(do_ref, o_ref, sel_ref, d_ref):
        prod = do_ref[...] * o_ref[...].astype(F32)
        d_ref[:, 0:LANES] = _dot_split(prod, sel_ref[0])
        d_ref[:, LANES:2 * LANES] = _dot_split(prod, sel_ref[1])

    return pl.pallas_call(
        body, name=name, grid=(4, s // ts),
        in_specs=[pl.BlockSpec((ts, LANES), lambda j, i: (i, 4 + j)),
                  pl.BlockSpec((ts, LANES), lambda j, i: (i, j)),
                  pl.BlockSpec((2, LANES, LANES), lambda j, i: (0, 0, 0))],
        out_specs=pl.BlockSpec((ts, 2 * LANES), lambda j, i: (i, j)),
        out_shape=jax.ShapeDtypeStruct((s, 8 * LANES), F32),
    )(dy, o, sel)


def _fox_bwd(p1, dy, lb, delta, ck, *, tq=512, name):
    s = p1.shape[0]
    tq = _tile_rows(tq, s)
    tk = tq
    nq = s // tq

    def body(q_ref, k_ref, v_ref, do_ref, lb_ref, dl_ref, ck_ref,
             dq_ref, dk_ref, dv_ref, dck_ref, dcq_ref, dqa_ref, dra_ref):
        kj = pl.program_id(1)

        @pl.when(kj == 0)
        def _():
            dqa_ref[...] = jnp.zeros_like(dqa_ref)
            dra_ref[...] = jnp.zeros_like(dra_ref)

        kf = k_ref[...]
        kb = kf.astype(BF16)
        vb = v_ref[...].astype(BF16)
        first, second = _head_masks((tk, LANES))
        kms = [jnp.where(sel, kf, 0.0).astype(BF16) for sel in (first, second)]
        cks = [ck_ref[hh] for hh in range(2)]

        def step(qi, carry, diagonal):
            dk_acc, dv_acc, dc0, dc1 = carry
            dcs = [dc0, dc1]
            rows = pl.ds(pl.multiple_of(qi * tq, tq), tq)
            q = q_ref[rows, :] * 0.125
            do = do_ref[rows, :]
            for hh, sel in enumerate((first, second)):
                qm = jnp.where(sel, q, 0.0).astype(BF16)
                dom = jnp.where(sel, do, 0.0).astype(BF16)
                bias = jnp.tile(lb_ref[rows, hh * LANES:(hh + 1) * LANES], (1, tk // LANES))
                pm = jnp.exp(_fox_scores(qm, kb, bias, cks[hh], diagonal))
                dv_acc = dv_acc + _dot_tn(pm.astype(BF16), dom)
                dp = _dot_nt(dom, vb)
                ds = pm * (dp - jnp.tile(dl_ref[rows, hh * LANES:(hh + 1) * LANES], (1, tk // LANES)))
                dsb = ds.astype(BF16)
                dk_acc = dk_acc + _dot_tn(dsb, qm)
                dcs[hh] = dcs[hh] - jnp.sum(ds, axis=0, keepdims=True)
                dqa_ref[rows, :] += _dot(dsb, kms[hh])
                dra_ref[hh, rows, :] += jnp.sum(ds, axis=1, keepdims=True)
            return dk_acc, dv_acc, dcs[0], dcs[1]

        zero = jnp.zeros((tk, LANES), F32)
        zrow = jnp.zeros((1, tk), F32)
        carry = step(kj, (zero, zero, zrow, zrow), True)
        dk_acc, dv_acc, dc0, dc1 = lax.fori_loop(kj + 1, nq, lambda qi, c: step(qi, c, False), carry)
        dk_ref[...] = dk_acc.astype(BF16)
        dv_ref[...] = dv_acc.astype(BF16)
        dck_ref[0] = dc0
        dck_ref[1] = dc1

        @pl.when(kj == nq - 1)
        def _():
            dq_ref[...] = (dqa_ref[...] * 0.125).astype(BF16)
            dcq_ref[:, 0:LANES] = dra_ref[0]
            dcq_ref[:, LANES:2 * LANES] = dra_ref[1]

    def full(width, col0):
        return pl.BlockSpec((s, width), lambda j, kj: (0, col0 + j))

    kblk = pl.BlockSpec((tk, LANES), lambda j, kj: (kj, j))
    f = jax.ShapeDtypeStruct
    return pl.pallas_call(
        body, name=name, grid=(4, nq),
        in_specs=[full(LANES, 8),
                  pl.BlockSpec((tk, LANES), lambda j, kj: (kj, 12 + j)),
                  pl.BlockSpec((tk, LANES), lambda j, kj: (kj, 16 + j)),
                  full(LANES, 4), full(2 * LANES, 0), full(2 * LANES, 0),
                  pl.BlockSpec((2, 1, tk), lambda j, kj: (j, 0, kj))],
        out_specs=(full(LANES, 0), kblk, kblk, pl.BlockSpec((2, 1, tk), lambda j, kj: (j, 0, kj)),
                   full(2 * LANES, 0)),
        out_shape=(f((s, 4 * LANES), BF16), f((s, 4 * LANES), BF16), f((s, 4 * LANES), BF16),
                   f((8, 1, s), F32), f((s, 8 * LANES), F32)),
        scratch_shapes=[pltpu.VMEM((s, LANES), F32), pltpu.VMEM((2, s, LANES), F32)],
    )(p1, p1, p1, dy, lb, delta, ck)


def _row_block(r, cap=256):
    best = None
    for rb in range(2 * SUBLANES, min(r, cap) + 1, 2 * SUBLANES):
        if r % rb == 0:
            best = rb
    return r if best is None else best


def _adamw(w, g, m, v, *, name):
    r, c = w.shape
    rb = _row_block(r)

    def body(w_ref, g_ref, m_ref, v_ref, d_ref, nm_ref, nv_ref):
        gv = g_ref[...]
        mn = ADAM_B1 * m_ref[...] + (1.0 - ADAM_B1) * gv
        vn = ADAM_B2 * v_ref[...] + (1.0 - ADAM_B2) * (gv * gv)
        m_hat = mn / ADAM_C1
        v_hat = vn / ADAM_C2
        d_ref[...] = (-ADAM_LR) * (m_hat / (jnp.sqrt(v_hat) + ADAM_EPS) + ADAM_WD * w_ref[...])
        nm_ref[...] = mn
        nv_ref[...] = vn

    blk = pl.BlockSpec((rb, c), lambda i: (i, 0))
    shp = jax.ShapeDtypeStruct((r, c), F32)
    return pl.pallas_call(
        body, name=name, grid=(r // rb,), in_specs=[blk] * 4, out_specs=(blk,) * 3, out_shape=(shp,) * 3,
    )(w, g, m, v)


def _adamw_halves(w, mine, theirs, m, v, core, *, name):
    layers, r, c = w.shape
    rh = r // 2
    rb = _row_block(rh)
    per = rh // rb

    def body(core_ref, w_ref, *refs):
        g_refs = refs[:2 * layers]
        m_ref, v_ref, g_ref, d_ref, nm_ref, nv_ref = refs[2 * layers:]
        own = pl.program_id(1) == core_ref[0]
        gv = jnp.where(own, g_refs[0][...], g_refs[layers][...])
        for l in range(1, layers):
            gv = jnp.where(pl.program_id(0) == l, jnp.where(own, g_refs[l][...], g_refs[layers + l][...]), gv)
        g_ref[...] = gv
        mn = ADAM_B1 * m_ref[...] + (1.0 - ADAM_B1) * gv
        vn = ADAM_B2 * v_ref[...] + (1.0 - ADAM_B2) * (gv * gv)
        m_hat = mn / ADAM_C1
        v_hat = vn / ADAM_C2
        d_ref[...] = (-ADAM_LR) * (m_hat / (jnp.sqrt(v_hat) + ADAM_EPS) + ADAM_WD * w_ref[...])
        nm_ref[...] = mn
        nv_ref[...] = vn

    full = pl.BlockSpec((None, rb, c), lambda l, h, i, core_ref: (l, h * per + i, 0))
    half = pl.BlockSpec((rb, c), lambda l, h, i, core_ref: (i, 0))
    shp = jax.ShapeDtypeStruct((layers, r, c), F32)
    return pl.pallas_call(
        body, name=name,
        grid_spec=pltpu.PrefetchScalarGridSpec(
            num_scalar_prefetch=1, grid=(layers, 2, per),
            in_specs=[full] + [half] * (2 * layers) + [full, full], out_specs=(full,) * 4),
        out_shape=(shp,) * 4,
    )(core, w, *mine, *theirs, m, v)


def _pair_sum(g, col, ra, core, after, *, name):
    _, rh, c = ra.shape
    rb = _row_block(rh)

    def body(core_ref, g_ref, ra_ref, after_ref, h_ref, h16_ref):
        tot = g_ref[...] + ra_ref[...]
        h_ref[...] = tot
        h16_ref[...] = tot.astype(BF16)

    if col:
        g_spec = pl.BlockSpec((None, rb, c), lambda k, i, core_ref: (core_ref[0], i, k))
    else:
        g_spec = pl.BlockSpec((None, None, rb, c), lambda k, i, core_ref: (k, core_ref[0], i, 0))
    slot = pl.BlockSpec((None, rb, c), lambda k, i, core_ref: (k, i, 0))
    return pl.pallas_call(
        body, name=name,
        grid_spec=pltpu.PrefetchScalarGridSpec(
            num_scalar_prefetch=1, grid=(N_CHIPS, rh // rb), in_specs=[g_spec, slot, ANY], out_specs=(slot, slot)),
        out_shape=(jax.ShapeDtypeStruct((N_CHIPS, rh, c), F32), jax.ShapeDtypeStruct((N_CHIPS, rh, c), BF16)),
    )(core, g, ra, after)


def _first_sum(h, r1, keep, after, *, name):
    _, rh, c = h.shape
    rb = _row_block(rh)

    def body(keep_ref, h_ref, r_ref, after_ref, s_ref, s16_ref):
        tot = h_ref[...] + r_ref[...].astype(F32)
        s_ref[...] = tot
        s16_ref[...] = tot.astype(BF16)

    slot = pl.BlockSpec((None, rb, c), lambda t, i, keep_ref: (t, i, 0))
    return pl.pallas_call(
        body, name=name,
        grid_spec=pltpu.PrefetchScalarGridSpec(
            num_scalar_prefetch=1, grid=(2, rh // rb),
            in_specs=[pl.BlockSpec((None, rb, c), lambda t, i, keep_ref: (keep_ref[t], i, 0)), slot, ANY],
            out_specs=(slot, slot)),
        out_shape=(jax.ShapeDtypeStruct((2, rh, c), F32), jax.ShapeDtypeStruct((2, rh, c), BF16)),
    )(keep, h, r1, after)


def _second_sum(s1, r2, mine, after, *, name):
    _, rh, c = s1.shape
    rb = _row_block(rh)

    def body(mine_ref, s_ref, r_ref, after_ref, t_ref):
        t_ref[...] = s_ref[...] + r_ref[...].astype(F32)

    flat = pl.BlockSpec((rb, c), lambda i, mine_ref: (i, 0))
    return pl.pallas_call(
        body, name=name,
        grid_spec=pltpu.PrefetchScalarGridSpec(
            num_scalar_prefetch=1, grid=(rh // rb,),
            in_specs=[pl.BlockSpec((None, rb, c), lambda i, mine_ref: (mine_ref[0], i, 0)), flat, ANY],
            out_specs=flat),
        out_shape=jax.ShapeDtypeStruct((rh, c), F32),
    )(mine, s1, r2, after)


def _place(shard, col, chip, dtype, *, name):
    r, c = shard.shape
    rh = r // 2
    rb = _row_block(rh)

    def body(chip_ref, s_ref, o_ref):
        o_ref[...] = s_ref[...].astype(o_ref.dtype)

    if col:
        out_spec = pl.BlockSpec((None, rb, c), lambda h, i, chip_ref: (h, i, chip_ref[0]))
        shape = (2, rh, N_CHIPS * c)
    else:
        out_spec = pl.BlockSpec((None, None, rb, c), lambda h, i, chip_ref: (chip_ref[0], h, i, 0))
        shape = (N_CHIPS, 2, rh, c)
    per = rh // rb
    return pl.pallas_call(
        body, name=name,
        grid_spec=pltpu.PrefetchScalarGridSpec(
            num_scalar_prefetch=1, grid=(2, per),
            in_specs=[pl.BlockSpec((rb, c), lambda h, i, chip_ref: (h * per + i, 0))], out_specs=out_spec),
        out_shape=jax.ShapeDtypeStruct(shape, dtype),
    )(chip, shard)


ANY = pl.BlockSpec(memory_space=pl.ANY)


def _mesh_pos():
    return lax.axis_index("x"), lax.axis_index("y"), lax.axis_index("c")


def _other_chips(x, y):
    return [(1 - x, y), (x, 1 - y), (1 - x, 1 - y)]


def _remote(src, dst, ssem, rsem, dev):
    return pltpu.make_async_remote_copy(src_ref=src, dst_ref=dst, send_sem=ssem, recv_sem=rsem,
                                        device_id=dev, device_id_type=MESH)


def _flip(a, b):
    return a + b - 2 * a * b


def _handshake(peers):
    barrier = pltpu.get_barrier_semaphore()
    for peer in peers:
        pl.semaphore_signal(barrier, inc=1, device_id=peer, device_id_type=MESH)
    pl.semaphore_wait(barrier, len(peers))


def _slab(ref, col, width, k, h):
    if not col:
        return ref.at[k, h]
    start = k * width if isinstance(k, int) else pl.multiple_of(k * width, LANES)
    return ref.at[h, :, pl.ds(start, width)]


def _all_gather(bufs, cols, *, collective_id, name):
    n = len(bufs)
    widths = [b.shape[2] // N_CHIPS if col else b.shape[3] for b, col in zip(bufs, cols)]
    outs = [jax.new_ref(b, memory_space=pltpu.MemorySpace.HBM) for b in bufs]

    def body(ssem, rsem):
        x, y, c = _mesh_pos()
        me = 2 * x + y
        sib = (x, y, 1 - c)
        n1 = (_flip(x, 1 - c), _flip(y, c))
        n2 = (_flip(x, c), _flip(y, 1 - c))
        k1 = 2 * n1[0] + n1[1]
        k2 = 2 * n2[0] + n2[1]
        kd = 2 * (1 - x) + (1 - y)
        _handshake([n1 + (c,), n2 + (c,), sib])

        def slab(a, k, h):
            return _slab(outs[a], cols[a], widths[a], k, h)

        def copy(a, j, src, dst, dev):
            return _remote(src, dst, ssem.at[a, j], rsem.at[a, j], dev)

        sends = []
        for a in range(n):
            for j, nb in ((0, n1), (1, n2)):
                own = slab(a, me, c)
                cp = copy(a, j, own, own, nb + (c,))
                cp.start()
                sends.append(cp)
        arrivals = ((0, k1, n1, 3), (1, k2, n2, 4), (2, kd, n2, 5))
        for j, k, nb, fwd in arrivals:
            for a in range(n):
                got = slab(a, k, c)
                copy(a, j, got, got, nb + (c,)).wait_recv()
                if j == 0:
                    cp = copy(a, 2, got, got, n2 + (c,))
                    cp.start()
                    sends.append(cp)
                cp = copy(a, fwd, got, got, sib)
                cp.start()
                sends.append(cp)
        for fwd, k in ((3, k2), (4, k1), (5, kd)):
            for a in range(n):
                got = slab(a, k, 1 - c)
                copy(a, fwd, got, got, sib).wait_recv()
        for cp in sends:
            cp.wait_send()

    _sequencer_call(body, (), [(n, 6), (n, 6)], collective_id, name)()
    return [ref[...] for ref in outs]


def _sequencer_call(body, out_types, sem_shapes, collective_id, name):
    return pl.kernel(
        body, name=name, out_type=out_types,
        mesh=plsc.ScalarSubcoreMesh(axis_name="sequencer", num_cores=1),
        scratch_types=[pltpu.SemaphoreType.DMA(shape) for shape in sem_shapes],
        compiler_params=pltpu.CompilerParams(collective_id=collective_id))


def _send_other_half(grads, cols, *, collective_id, name):
    n = len(grads)

    def shard_shape(g, col):
        if col:
            return (g.shape[1], g.shape[2] // N_CHIPS)
        return g.shape[2:]

    shapes = [shard_shape(g, col) for g, col in zip(grads, cols)]

    def body(*refs):
        ins, outs = refs[:n], refs[n:2 * n]
        ssem, rsem = refs[2 * n:]
        x, y, c = _mesh_pos()
        sib = (x, y, 1 - c)
        _handshake([sib])
        sends = []
        for a in range(n):
            for k in range(N_CHIPS):
                src = _slab(ins[a], cols[a], shapes[a][1], k, 1 - c)
                cp = _remote(src, outs[a].at[k], ssem.at[a, k], rsem.at[a, k], sib)
                cp.start()
                sends.append(cp)
        for cp in sends:
            cp.wait()

    out_types = [jax.ShapeDtypeStruct((N_CHIPS,) + shp, g.dtype) for g, shp in zip(grads, shapes)]
    return _sequencer_call(body, out_types, [(n, N_CHIPS), (n, N_CHIPS)], collective_id, name)(*grads)


def _send_first(sums, *, collective_id, name):
    n = len(sums)

    def body(*refs):
        ins, outs = refs[:n], refs[n:2 * n]
        ssem, rsem = refs[2 * n:]
        x, y, c = _mesh_pos()
        nb = (_flip(x, c), _flip(y, 1 - c), c)
        _handshake([nb])
        sends = []
        for a in range(n):
            for t in range(2):
                k = 2 * (c * (1 - x) + (1 - c) * t) + (c * t + (1 - c) * (1 - y))
                cp = _remote(ins[a].at[k], outs[a].at[t], ssem.at[a, t], rsem.at[a, t], nb)
                cp.start()
                sends.append(cp)
        for cp in sends:
            cp.wait()

    out_types = [jax.ShapeDtypeStruct((2,) + h.shape[1:], h.dtype) for h in sums]
    return _sequencer_call(body, out_types, [(n, 2), (n, 2)], collective_id, name)(*sums)


def _send_second(sums, *, collective_id, name):
    n = len(sums)

    def body(*refs):
        ins, outs = refs[:n], refs[n:2 * n]
        ssem, rsem = refs[2 * n:]
        x, y, c = _mesh_pos()
        nb = (_flip(x, 1 - c), _flip(y, c), c)
        other = 1 - (c * y + (1 - c) * x)
        _handshake([nb])
        sends = []
        for a in range(n):
            cp = _remote(ins[a].at[other], outs[a], ssem.at[a], rsem.at[a], nb)
            cp.start()
            sends.append(cp)
        for cp in sends:
            cp.wait()

    out_types = [jax.ShapeDtypeStruct(s.shape[1:], s.dtype) for s in sums]
    return _sequencer_call(body, out_types, [(n,), (n,)], collective_id, name)(*sums)


def _swap_halves(halves, *, collective_id, name):
    n = len(halves)

    def body(*refs):
        ins, outs = refs[:n], refs[n:2 * n]
        ssem, rsem = refs[2 * n:]
        x, y, c = _mesh_pos()
        sib = (x, y, 1 - c)
        _handshake([sib])
        cps = []
        for a in range(n):
            cp = _remote(ins[a], outs[a], ssem.at[a], rsem.at[a], sib)
            cp.start()
            cps.append(cp)
        for cp in cps:
            cp.wait()

    out_types = [jax.ShapeDtypeStruct(h.shape, h.dtype) for h in halves]
    return _sequencer_call(body, out_types, [(n,), (n,)], collective_id, name)(*halves)


def _all_reduce_small(buf, *, name):
    r = buf.shape[0]
    rh = r // 2

    def body(in_ref, out_ref, x1_ref, x2_ref, ssem, rsem):
        x, y, c = _mesh_pos()
        me = 2 * x + y
        sib = (x, y, 1 - c)
        chips = _other_chips(x, y)
        cp = _remote(in_ref, x1_ref, ssem.at[0], rsem.at[0], sib)
        cp.start()
        cp.wait()
        off = pl.multiple_of(c * rh, SUBLANES)
        x2_ref[me] = in_ref[pl.ds(off, rh), :] + x1_ref[pl.ds(off, rh), :]
        sends = []
        for j, (cx, cy) in enumerate(chips):
            s = _remote(x2_ref.at[me], x2_ref.at[me], ssem.at[1 + j], rsem.at[1 + j], (cx, cy, c))
            s.start()
            sends.append(s)
        for j, (cx, cy) in enumerate(chips):
            slot = x2_ref.at[2 * cx + cy]
            _remote(slot, slot, ssem.at[1 + j], rsem.at[1 + j], (cx, cy, c)).wait_recv()
        out_ref[pl.ds(off, rh), :] = ((x2_ref[0] + x2_ref[1]) + x2_ref[2]) + x2_ref[3]
        for s in sends:
            s.wait_send()
        mine = out_ref.at[pl.ds(off, rh), :]
        s3 = _remote(mine, mine, ssem.at[4], rsem.at[4], sib)
        s3.start()
        off2 = pl.multiple_of((1 - c) * rh, SUBLANES)
        theirs = out_ref.at[pl.ds(off2, rh), :]
        _remote(theirs, theirs, ssem.at[4], rsem.at[4], sib).wait_recv()
        s3.wait_send()

    vm = pl.BlockSpec(memory_space=pltpu.VMEM)
    return pl.pallas_call(
        body, name=name, in_specs=[vm], out_specs=vm,
        out_shape=jax.ShapeDtypeStruct((r, LANES), F32),
        scratch_shapes=[pltpu.VMEM((r, LANES), F32), pltpu.VMEM((N_CHIPS, rh, LANES), F32),
                        pltpu.SemaphoreType.DMA((5,)), pltpu.SemaphoreType.DMA((5,))],
    )(buf)


PACK_ALIGN = 2 * SUBLANES * LANES


def _pack(arrays, rows_multiple=2 * SUBLANES):
    parts, offs, off = [], [], 0
    for a in arrays:
        flat = a.reshape(-1).astype(F32)
        padded = -(-flat.shape[0] // PACK_ALIGN) * PACK_ALIGN
        parts.append(jnp.pad(flat, (0, padded - flat.shape[0])))
        offs.append(off)
        off += padded
    buf = jnp.concatenate(parts).reshape(-1, LANES)
    return buf, offs


def _unpack(buf, offs, shapes):
    flat = buf.reshape(-1)
    out = []
    for off, shp in zip(offs, shapes):
        size = 1
        for d in shp:
            size *= d
        out.append(flat[off:off + size].reshape(shp))
    return out


def _cols_from_shards(g4):
    _, k, ns = g4.shape
    return jnp.transpose(g4, (1, 0, 2)).reshape(k, N_CHIPS * ns)


def _cols_to_shards(w):
    k, n = w.shape
    return jnp.transpose(w.reshape(k, N_CHIPS, n // N_CHIPS), (1, 0, 2))


def _block_cols(w, parts, blocks):
    lead = w.shape[:-1]
    width = w.shape[-1] // (parts * blocks)
    w = w.reshape(lead + (parts, blocks, width))
    w = jnp.swapaxes(w, -3, -2)
    return w.reshape(lead + (parts * blocks * width,))


def _unblock_cols(w, parts, blocks):
    lead = w.shape[:-1]
    width = w.shape[-1] // (parts * blocks)
    w = w.reshape(lead + (blocks, parts, width))
    w = jnp.swapaxes(w, -3, -2)
    return w.reshape(lead + (parts * blocks * width,))


def _pair_blockdiag(w8):
    w = w8.reshape(4, 2, 64, 64)
    z = jnp.zeros((4, 64, 64), w8.dtype)
    top = jnp.concatenate([w[:, 0], z], axis=2)
    bot = jnp.concatenate([z, w[:, 1]], axis=2)
    return jnp.concatenate([top, bot], axis=1)


def _pair_diag_blocks(w4):
    a = w4[:, :64, :64]
    b = w4[:, 64:, 64:]
    return jnp.stack([a, b], axis=1).reshape(8, 64, 64)


def _local_step(x, target, wts, on_event=None):
    s = x.shape[0]
    g = {}

    def event(name, token):
        if on_event is not None:
            on_event(name, g, token)

    win0 = wts["w_in0"]
    wout0 = wts["w_out0"]
    win1 = wts["w_in1"]
    wout1 = wts["w_out1"]
    wup = wts["w_up"]
    wdown = wts["w_down"]
    w4, b4, w3, b3 = wts["w4"], wts["b4"], wts["w3"], wts["b3"]
    wa, wx = wts["wa"], wts["wx"]
    wat, wxt = jnp.swapaxes(wa, 1, 2), jnp.swapaxes(wx, 1, 2)
    ba, bx, lam = wts["ba"], wts["bx"], wts["lam"]
    fcw, fcb = wts["ffn_cw"], wts["ffn_cb"]
    sgu_w, sgu_wt = wts["sgu_w"], wts["sgu_wt"]
    sgu_bias, sgu_gn = wts["sgu_bias"], wts["sgu_gn"]
    bf = wts["bf"]

    lane = jnp.arange(LANES)
    seg = jnp.where((lane[:, None] // 64) == (lane[None, :] // 64), 1.0 / 64.0, 0.0).astype(BF16)
    sel = jnp.stack([jnp.broadcast_to((lane[:, None] < 64), (LANES, LANES)),
                     jnp.broadcast_to((lane[:, None] >= 64), (LANES, LANES))]).astype(BF16)
    tril = (lane[:, None] >= lane[None, :]).astype(F32)

    n0 = _norm_fwd(x, wts["g_mix0"], name="norm_mix0")
    p0 = _mm([n0], win0, nb=640, name="mm_in0")
    ya, yb, hl = _even_core_fwd(p0, w4, b4, wa, ba, wx, bx, lam, w3, b3, name="even_fwd")
    h1 = _mm([ya, yb], wout0, res=x, name="mm_out0")

    def ffn_fwd(h, layer):
        n = _norm_fwd(h, wts["g_ffn"][layer], name=f"norm_ffn{layer}")
        up = _mm([n], wup[layer], out_dtype=BF16, nb=1408, name=f"mm_up{layer}")
        act = _ffn_core_fwd(up, fcw[layer], fcb[layer], name=f"ffn_fwd{layer}")
        hn = _mm([act], wdown[layer], res=h, name=f"mm_down{layer}")
        return n, up, act, hn

    n1, up0, act0, h2 = ffn_fwd(h1, 0)

    n2 = _norm_fwd(h2, wts["g_mix1"], name="norm_mix1")
    p1 = _mm([n2], win1, nb=896, name="mm_in1")
    yc = _sgu_fwd(p1, sgu_gn, sgu_w, sgu_bias, seg, name="sgu_fwd")
    cum = _fcum_fwd(p1, bf, name="fcum_fwd")
    c8 = cum[:, :8]
    cq = jnp.broadcast_to(c8[:, :, None], (s, 8, LANES)).reshape(s, 8 * LANES)
    ck = jnp.transpose(c8).reshape(8, 1, s)
    yd, lb = _fox_fwd(p1, cq, ck, name="fox_fwd")
    h3 = _mm([yc, yd], wout1, res=h2, name="mm_out1")

    n3, up1, act1, h4 = ffn_fwd(h3, 1)
    dh4, loss, g["final_norm"] = _final(h4, wts["g_final"], target, name="final")

    def ffn_bwd(dh, h, n, up, act, layer):
        dact = _mm([dh], wdown[layer], trans_w=True, out_dtype=BF16, nb=1408, name=f"mm_dact{layer}")
        g[f"w_down{layer}"] = _mm_tn([act], [dh], nb=512, name=f"mm_dwdown{layer}")
        event(f"dwdown{layer}", g[f"w_down{layer}"])
        dgate, dval, dcwg, dcwv, dcbg, dcbv = _ffn_core_bwd(dact, up, fcw[layer], fcb[layer], name=f"ffn_bwd{layer}")
        event(f"ffn_bwd{layer}", dgate)
        dn = _mm([dgate, dval], wup[layer], trans_w=True, nb=512, name=f"mm_dn_ffn{layer}")
        g[f"w_up{layer}"] = _mm_tn([n], [dgate, dval], nb=1408, name=f"mm_dwup{layer}")
        event(f"dwup{layer}", g[f"w_up{layer}"])
        dhn, g[f"g_ffn{layer}"] = _norm_bwd(dn, h, wts["g_ffn"][layer], dh, name=f"norm_bwd_ffn{layer}")
        g[f"ffn_cw{layer}"] = jnp.concatenate([dcwg, dcwv], axis=1)
        g[f"ffn_cb{layer}"] = jnp.concatenate([dcbg, dcbv], axis=1)
        return dhn

    dh3 = ffn_bwd(dh4, h3, n3, up1, act1, 1)

    dy1 = _mm([dh3], wout1, trans_w=True, name="mm_dy1")
    g["w_out1"] = _mm_tn([yc, yd], [dh3], nb=512, name="mm_dwout1")
    event("dwout1", g["w_out1"])
    dzu, dzg, g["sgu_w"], g["sgu_bias"], g["sgu_gn"] = _sgu_bwd(
        p1, dy1, sgu_gn, sgu_w, sgu_wt, sgu_bias, seg, tril, name="sgu_bwd")
    delta = _fox_delta(dy1, yd, sel, name="fox_delta")
    dq, dk, dv, dck, dcq = _fox_bwd(p1, dy1, lb, delta, ck, name="fox_bwd")
    event("fox_bwd", dq)
    dcs = jnp.pad(jnp.transpose(dck.reshape(8, s)), ((0, 0), (0, LANES - 8)))
    df, g["bf"] = _fcum_bwd(dcs, dcq, p1, bf, name="fcum_bwd")
    dp1 = jnp.concatenate([dzu, dzg, dq, dk, dv, df], axis=1)
    dn2 = _mm([dp1], win1, trans_w=True, name="mm_dn_mix1")
    g["w_in1"] = _mm_tn([n2], [dp1], nb=896, name="mm_dwin1")
    event("dwin1", g["w_in1"])
    dh2, g["g_mix1"] = _norm_bwd(dn2, h2, wts["g_mix1"], dh3, name="norm_bwd_mix1")

    dh1 = ffn_bwd(dh2, h1, n1, up0, act0, 0)

    dy0 = _mm([dh1], wout0, trans_w=True, name="mm_dy0")
    g["w_out0"] = _mm_tn([ya, yb], [dh1], nb=512, name="mm_dwout0")
    event("dwout0", g["w_out0"])
    (dp0, g["w4"], g["b4"], g["wa"], g["ba"], g["wx"], g["bx"], g["lam"], g["w3"], g["b3"]) = _even_core_bwd(
        dy0, p0, hl, w4, b4, wa, wat, ba, wx, wxt, bx, lam, w3, b3, name="even_bwd")
    event("even_bwd", dp0)
    dn0 = _mm([dp0], win0, trans_w=True, name="mm_dn_mix0")
    g["w_in0"] = _mm_tn([n0], [dp0], nb=640, name="mm_dwin0")
    event("dwin0", g["w_in0"])
    grad_x, g["g_mix0"] = _norm_bwd(dn0, x, wts["g_mix0"], dh1, name="norm_bwd_mix0")
    return loss, grad_x, g


def _prepare_weights(nat):
    lane = jnp.arange(LANES)
    tril = (lane[:, None] >= lane[None, :]).astype(F32)
    sgu_tril = nat["sgu_w"][0] * tril
    w_in1 = nat["mix1_w_in"]
    nblk = D_FF // FFN_CB
    return {
        "w_in0": _block_cols(nat["mix0_w_in"], 5, 4),
        "w_out0": nat["mix0_w_out"],
        "w_in1": jnp.pad(w_in1, ((0, 0), (0, 21 * LANES - w_in1.shape[1]))),
        "w_out1": nat["mix1_w_out"],
        "w_up": [nat["ffn_up"][l] for l in range(2)],
        "w_down": [nat["ffn_down"][l] for l in range(2)],
        "w4": nat["lru_conv_w"], "b4": nat["lru_conv_b"], "w3": nat["sconv_w"], "b3": nat["sconv_b"],
        "wa": _pair_blockdiag(nat["lru_wa"][0]).astype(BF16), "wx": _pair_blockdiag(nat["lru_wx"][0]).astype(BF16),
        "ba": nat["lru_ba"], "bx": nat["lru_bx"], "lam": nat["lru_lambda"],
        "ffn_cw": [nat["ffn_conv_w"][l] for l in range(2)],
        "ffn_cb": [nat["ffn_conv_b"][l:l + 1] for l in range(2)],
        "sgu_w": sgu_tril.astype(BF16), "sgu_wt": jnp.swapaxes(sgu_tril, 1, 2).astype(BF16),
        "sgu_bias": jnp.repeat(jnp.transpose(nat["sgu_b"][0]), 64, axis=1), "sgu_gn": nat["sgu_norm"],
        "bf": jnp.pad(nat["fox_bf"], ((0, 0), (0, LANES - 8))),
        "g_mix0": nat["mix0_norm"], "g_mix1": nat["mix1_norm"],
        "g_ffn": [nat["ffn_norm"][0:1], nat["ffn_norm"][1:2]], "g_final": nat["final_norm"].reshape(1, D_MODEL),
    }


def _natural_grads(g):
    nblk = D_FF // FFN_CB
    small = {
        "mix0_norm": g["g_mix0"], "lru_conv_b": g["b4"],
        "lru_wa": _pair_diag_blocks(g["wa"])[None], "lru_ba": g["ba"],
        "lru_wx": _pair_diag_blocks(g["wx"])[None], "lru_bx": g["bx"],
        "lru_lambda": g["lam"], "sconv_b": g["b3"],
        "sgu_w": g["sgu_w"][None],
        "sgu_b": jnp.transpose(g["sgu_bias"].reshape(CHUNK, 8, 64).sum(axis=2))[None],
        "fox_bf": g["bf"][:, :8],
        "ffn_norm": jnp.concatenate([g["g_ffn0"], g["g_ffn1"]], axis=0),
        "ffn_conv_b": jnp.concatenate([g["ffn_cb0"], g["ffn_cb1"]], axis=0),
        "final_norm": g["final_norm"].reshape(D_MODEL),
        "lru_conv_w": g["w4"][None], "sconv_w": g["w3"][None],
        "ffn_conv_w": jnp.stack([g["ffn_cw0"], g["ffn_cw1"]]),
        "mix1_norm": g["g_mix1"], "sgu_norm": g["sgu_gn"],
    }
    big = {
        "mix0_w_in": _unblock_cols(g["w_in0"], 5, 4), "mix0_w_out": g["w_out0"],
        "mix1_w_in": g["w_in1"][:, :2568], "mix1_w_out": g["w_out1"],
        "ffn_up0": g["w_up0"], "ffn_up1": g["w_up1"],
        "ffn_down0": g["w_down0"], "ffn_down1": g["w_down1"],
    }
    return small, big


COL_SHARDED = ("mix0_w_in", "mix1_w_in", "ffn_up0", "ffn_up1")
COL_ALIGNED = ("mix0_w_in", "ffn_up0", "ffn_up1")
SMALL_SHARDED = ("lru_conv_w", "sconv_w", "ffn_conv_w", "mix1_norm", "sgu_norm")
SMALL_REPLICATED = ("mix0_norm", "lru_conv_b", "lru_wa", "lru_ba", "lru_wx", "lru_bx", "lru_lambda", "sconv_b",
                    "sgu_w", "sgu_b", "fox_bf", "ffn_norm", "ffn_conv_b", "final_norm")
BIG = ("mix0_w_in", "mix0_w_out", "mix1_w_in", "mix1_w_out", "ffn_up0", "ffn_up1", "ffn_down0", "ffn_down1")
WEIGHT_ORDER = ("mix0_norm", "mix0_w_in", "lru_conv_w", "lru_conv_b", "lru_wa", "lru_ba", "lru_wx", "lru_bx",
                "lru_lambda", "sconv_w", "sconv_b", "mix0_w_out", "mix1_norm", "mix1_w_in", "sgu_norm", "sgu_w",
                "sgu_b", "fox_bf", "mix1_w_out", "ffn_norm", "ffn_up", "ffn_conv_w", "ffn_conv_b", "ffn_down",
                "final_norm")


GATHER_GROUPS = (("mix0_w_in", "mix0_w_out"), ("ffn_up0",), ("ffn_down0", "mix1_w_in"),
                 ("mix1_w_out", "ffn_up1", "ffn_down1"))
CID_GATHER, CID_PAIR, CID_FIRST, CID_SECOND, CID_SWAP = 1, 2, 3, 4, 5


class _GradReducer:
    def __init__(self):
        x, y, c = _mesh_pos()
        self.core = c.reshape(1).astype(jnp.int32)
        self.keep = jnp.stack([c * (2 * x + t) + (1 - c) * (2 * t + y) for t in range(2)]).astype(jnp.int32)
        self.mine = (c * y + (1 - c) * x).reshape(1).astype(jnp.int32)
        self.groups = {}

    @staticmethod
    def _view(name, a):
        if name in COL_ALIGNED:
            return a.reshape(2, a.shape[0] // 2, a.shape[1])
        if name in COL_SHARDED:
            a = _cols_to_shards(a)
            return a.reshape(N_CHIPS, 2, a.shape[1] // 2, a.shape[2])
        rows = a.shape[0] // (2 * N_CHIPS)
        return a.reshape(N_CHIPS, 2, rows, a.shape[1])

    def start(self, group, grads):
        names = tuple(grads)
        views = [self._view(k, grads[k]) for k in names]
        cols = [k in COL_ALIGNED for k in names]
        data = _send_other_half(views, cols, collective_id=CID_PAIR, name=f"rs_pair_{group}")
        self.groups[group] = dict(names=names, stage=0, views=views, cols=cols, data=data)

    def step(self, group, after):
        st = self.groups[group]
        names = st["names"]
        if st["stage"] == 0:
            sums = [_pair_sum(a, col, b, self.core, after, name=f"rs_pair_sum_{k}")
                    for k, a, col, b in zip(names, st["views"], st["cols"], st["data"])]
            st["keep"] = [s32 for s32, _ in sums]
            st["data"] = _send_first([s16 for _, s16 in sums], collective_id=CID_FIRST, name=f"rs_first_{group}")
        elif st["stage"] == 1:
            sums = [_first_sum(s32, r, self.keep, after, name=f"rs_first_sum_{k}")
                    for k, s32, r in zip(names, st["keep"], st["data"])]
            st["keep"] = [s32 for s32, _ in sums]
            st["data"] = _send_second([s16 for _, s16 in sums], collective_id=CID_SECOND, name=f"rs_second_{group}")
        else:
            st["mine"] = [_second_sum(s32, r, self.mine, after, name=f"rs_second_sum_{k}")
                          for k, s32, r in zip(names, st["keep"], st["data"])]
            st["data"] = _swap_halves(st["mine"], collective_id=CID_SWAP, name=f"rs_swap_{group}")
        st["stage"] += 1

    def result(self, group):
        st = self.groups[group]
        return {k: (a, b) for k, a, b in zip(st["names"], st["mine"], st["data"])}


def _train_step(x, target, w, m, v):
    x2 = x[0]
    t2 = target[0]
    chip = 2 * lax.axis_index("x") + lax.axis_index("y")
    core_arr = lax.axis_index("c").reshape(1).astype(jnp.int32)
    chip_arr = chip.reshape(1).astype(jnp.int32)

    big_shards = {
        "mix0_w_in": w["mix0_w_in"][0], "mix0_w_out": w["mix0_w_out"][0],
        "mix1_w_in": w["mix1_w_in"][0], "mix1_w_out": w["mix1_w_out"][0],
        "ffn_up0": w["ffn_up"][0], "ffn_up1": w["ffn_up"][1],
        "ffn_down0": w["ffn_down"][0], "ffn_down1": w["ffn_down"][1],
    }
    small_shards = [w[k] for k in SMALL_SHARDED]
    small_buf, small_offs = _pack(small_shards)
    full = {}
    small_all = None
    for gi, names in enumerate(GATHER_GROUPS):
        cols = [k in COL_ALIGNED for k in names]
        placed = [_place(big_shards[k], col, chip_arr, BF16, name=f"place_{k}") for k, col in zip(names, cols)]
        if gi == 0:
            placed.append(_place(small_buf, False, chip_arr, F32, name="place_small"))
            cols = cols + [False]
        gathered = _all_gather(placed, cols, collective_id=CID_GATHER, name=f"gather_weights{gi}")
        if gi == 0:
            small_all = gathered[-1].reshape(N_CHIPS, -1, LANES)
        for k, arr in zip(names, gathered):
            if k in COL_ALIGNED:
                full[k] = arr.reshape(arr.shape[0] * arr.shape[1], arr.shape[2])
            elif k in COL_SHARDED:
                full[k] = _cols_from_shards(arr.reshape((N_CHIPS, arr.shape[1] * arr.shape[2], arr.shape[3])))
            else:
                full[k] = arr.reshape(-1, arr.shape[3])
    per_chip = [_unpack(small_all[k], small_offs, [a.shape for a in small_shards]) for k in range(N_CHIPS)]
    lru_conv_w = jnp.concatenate([per_chip[k][0] for k in range(N_CHIPS)], axis=-1)[0]
    sconv_w = jnp.concatenate([per_chip[k][1] for k in range(N_CHIPS)], axis=-1)[0]
    ffn_conv_w = jnp.concatenate([per_chip[k][2] for k in range(N_CHIPS)], axis=-1)
    mix1_norm = jnp.concatenate([per_chip[k][3] for k in range(N_CHIPS)], axis=-1)
    sgu_norm = jnp.concatenate([per_chip[k][4] for k in range(N_CHIPS)], axis=-1)

    nat = {
        "mix0_w_in": full["mix0_w_in"], "mix0_w_out": full["mix0_w_out"],
        "mix1_w_in": full["mix1_w_in"], "mix1_w_out": full["mix1_w_out"],
        "ffn_up": [full["ffn_up0"], full["ffn_up1"]], "ffn_down": [full["ffn_down0"], full["ffn_down1"]],
        "lru_conv_w": lru_conv_w, "sconv_w": sconv_w, "ffn_conv_w": ffn_conv_w, "mix1_norm": mix1_norm,
        "sgu_norm": sgu_norm,
    }
    for k in SMALL_REPLICATED:
        nat[k] = w[k]
    wts = _prepare_weights(nat)

    reducer = _GradReducer()

    def on_event(name, g, token):
        if name == "dwup1":
            reducer.start("ffn1", {"ffn_up1": g["w_up1"], "ffn_down1": g["w_down1"]})
        elif name in ("dwout1", "fox_bwd"):
            reducer.step("ffn1", token)
        elif name == "dwin1":
            reducer.step("ffn1", token)
            reducer.start("mix1", {"mix1_w_in": g["w_in1"][:, :2568], "mix1_w_out": g["w_out1"]})
        elif name in ("dwdown0", "ffn_bwd0"):
            reducer.step("mix1", token)
        elif name == "dwup0":
            reducer.step("mix1", token)
            reducer.start("ffn0", {"ffn_up0": g["w_up0"], "ffn_down0": g["w_down0"]})
        elif name in ("dwout0", "even_bwd"):
            reducer.step("ffn0", token)
        elif name == "dwin0":
            reducer.step("ffn0", token)
            reducer.start("mix0", {"mix0_w_in": _unblock_cols(g["w_in0"], 5, 4), "mix0_w_out": g["w_out0"]})

    loss, grad_x, g = _local_step(x2, t2, wts, on_event)
    grads_small, _ = _natural_grads(g)

    small_names = SMALL_REPLICATED + SMALL_SHARDED
    small_list = [grads_small[k] for k in small_names] + [loss[:, :1]]
    sbuf, soffs = _pack(small_list)
    sred = _all_reduce_small(sbuf, name="reduce_small")
    small_red = _unpack(sred, soffs, [a.shape for a in small_list])
    loss_total = small_red[-1][0, 0]
    gsum = dict(zip(small_names, small_red[:-1]))
    for k in SMALL_SHARDED:
        width = w[k].shape[-1]
        gsum[k] = lax.dynamic_slice_in_dim(gsum[k], chip * width, width, axis=gsum[k].ndim - 1)

    out_g, out_d, out_m, out_v = {}, {}, {}, {}
    reduced = {}
    for group in ("ffn1", "mix1", "ffn0"):
        reduced.update(reducer.result(group))

    def update(pname, keys):
        mine = [reduced[k][0] for k in keys]
        theirs = [reduced[k][1] for k in keys]
        out_g[pname], out_d[pname], out_m[pname], out_v[pname] = _adamw_halves(
            w[pname], mine, theirs, m[pname], v[pname], core_arr, name=f"adamw_{pname}")
        return out_d[pname]

    reducer.step("mix0", update("ffn_up", ("ffn_up0", "ffn_up1")))
    small_w = [w[k] for k in small_names]
    pg, offs = _pack([gsum[k] for k in small_names])
    pw, _ = _pack(small_w)
    pm, _ = _pack([m[k] for k in small_names])
    pv, _ = _pack([v[k] for k in small_names])
    sd, sm, sv = _adamw(pw, pg, pm, pv, name="adamw_small")
    reducer.step("mix0", update("ffn_down", ("ffn_down0", "ffn_down1")))
    update("mix1_w_in", ("mix1_w_in",))
    reducer.step("mix0", update("mix1_w_out", ("mix1_w_out",)))
    reduced.update(reducer.result("mix0"))
    update("mix0_w_in", ("mix0_w_in",))
    update("mix0_w_out", ("mix0_w_out",))

    shapes = [a.shape for a in small_w]
    for k, dd, mm, vv in zip(small_names, _unpack(sd, offs, shapes), _unpack(sm, offs, shapes),
                             _unpack(sv, offs, shapes)):
        out_g[k], out_d[k], out_m[k], out_v[k] = gsum[k].reshape(w[k].shape), dd, mm, vv

    outs = [loss_total, grad_x[None]]
    for d in (out_g, out_d, out_m, out_v):
        outs.extend(d[k] for k in WEIGHT_ORDER)
    return tuple(outs)


def kernel(x, mix0_norm, mix0_w_in, lru_conv_w, lru_conv_b, lru_wa, lru_ba, lru_wx, lru_bx, lru_lambda, sconv_w, sconv_b, mix0_w_out, mix1_norm, mix1_w_in, sgu_norm, sgu_w, sgu_b, fox_bf, mix1_w_out, ffn_norm, ffn_up, ffn_conv_w, ffn_conv_b, ffn_down, final_norm, loss_target, m_mix0_norm, m_mix0_w_in, m_lru_conv_w, m_lru_conv_b, m_lru_wa, m_lru_ba, m_lru_wx, m_lru_bx, m_lru_lambda, m_sconv_w, m_sconv_b, m_mix0_w_out, m_mix1_norm, m_mix1_w_in, m_sgu_norm, m_sgu_w, m_sgu_b, m_fox_bf, m_mix1_w_out, m_ffn_norm, m_ffn_up, m_ffn_conv_w, m_ffn_conv_b, m_ffn_down, m_final_norm, v_mix0_norm, v_mix0_w_in, v_lru_conv_w, v_lru_conv_b, v_lru_wa, v_lru_ba, v_lru_wx, v_lru_bx, v_lru_lambda, v_sconv_w, v_sconv_b, v_mix0_w_out, v_mix1_norm, v_mix1_w_in, v_sgu_norm, v_sgu_w, v_sgu_b, v_fox_bf, v_mix1_w_out, v_ffn_norm, v_ffn_up, v_ffn_conv_w, v_ffn_conv_b, v_ffn_down, v_final_norm):
    w = dict(zip(WEIGHT_ORDER, (mix0_norm, mix0_w_in, lru_conv_w, lru_conv_b, lru_wa, lru_ba, lru_wx, lru_bx, lru_lambda, sconv_w, sconv_b, mix0_w_out, mix1_norm, mix1_w_in, sgu_norm, sgu_w, sgu_b, fox_bf, mix1_w_out, ffn_norm, ffn_up, ffn_conv_w, ffn_conv_b, ffn_down, final_norm)))
    m = dict(zip(WEIGHT_ORDER, (m_mix0_norm, m_mix0_w_in, m_lru_conv_w, m_lru_conv_b, m_lru_wa, m_lru_ba, m_lru_wx, m_lru_bx, m_lru_lambda, m_sconv_w, m_sconv_b, m_mix0_w_out, m_mix1_norm, m_mix1_w_in, m_sgu_norm, m_sgu_w, m_sgu_b, m_fox_bf, m_mix1_w_out, m_ffn_norm, m_ffn_up, m_ffn_conv_w, m_ffn_conv_b, m_ffn_down, m_final_norm)))
    v = dict(zip(WEIGHT_ORDER, (v_mix0_norm, v_mix0_w_in, v_lru_conv_w, v_lru_conv_b, v_lru_wa, v_lru_ba, v_lru_wx, v_lru_bx, v_lru_lambda, v_sconv_w, v_sconv_b, v_mix0_w_out, v_mix1_norm, v_mix1_w_in, v_sgu_norm, v_sgu_w, v_sgu_b, v_fox_bf, v_mix1_w_out, v_ffn_norm, v_ffn_up, v_ffn_conv_w, v_ffn_conv_b, v_ffn_down, v_final_norm)))
    return _train_step(x, loss_target, w, m, v)
```

```python
import functools

import jax
import jax.numpy as jnp
from jax import lax
from jax.experimental import pallas as pl
from jax.experimental.pallas import tpu as pltpu
from jax.experimental.pallas import tpu_sc as plsc

F32 = jnp.float32
BF16 = jnp.bfloat16
MESH = pl.DeviceIdType.MESH

D_MODEL = 1024
LANES = 128
SUBLANES = 8
N_CHIPS = 4
EPS = 1e-6
LRU_C = 8.0
D_FF = 2816
FFN_CB = 256
CHUNK = 128
NEG = -1e30

ADAM_LR = 0.001
ADAM_B1 = 0.9
ADAM_B2 = 0.999
ADAM_EPS = 1e-08
ADAM_WD = 0.01
ADAM_STEP = 10
ADAM_C1 = 1.0 - ADAM_B1 ** ADAM_STEP
ADAM_C2 = 1.0 - ADAM_B2 ** ADAM_STEP

_GELU_C = 0.7978845608028654
_GELU_A = 0.044715


def _sigmoid(x):
    return 1.0 / (1.0 + jnp.exp(-x))


def _sigmoid_tanh(x):
    return 0.5 * jnp.tanh(0.5 * x) + 0.5


def _log1p_pos(e):
    w = 1.0 + e
    return jnp.where(w == 1.0, e, jnp.log(w) * (e / (w - 1.0)))


def _softplus(x):
    return jnp.maximum(x, 0.0) + _log1p_pos(jnp.exp(-jnp.abs(x)))


def _gelu(x):
    t = jnp.tanh(_GELU_C * (x + _GELU_A * (x * x * x)))
    return 0.5 * x * (1.0 + t), t


def _gelu_grad(x, t):
    return 0.5 * (1.0 + t) + 0.5 * x * (1.0 - t * t) * (_GELU_C * (1.0 + 3.0 * _GELU_A * x * x))


def _rows(shape):
    return lax.broadcasted_iota(jnp.int32, shape, 0)


def _lanes(shape):
    return lax.broadcasted_iota(jnp.int32, shape, 1)


def _shift_down(x, halo8, j):
    if j == 0:
        return x
    r = pltpu.roll(x, j, 0)
    hr = pltpu.roll(halo8, j, 0)
    top = jnp.where(_rows(hr.shape) < j, hr, r[:SUBLANES])
    return jnp.concatenate([top, r[SUBLANES:]], axis=0)


def _shift_up(x, next8, j):
    if j == 0:
        return x
    n = x.shape[0]
    r = pltpu.roll(x, n - j, 0)
    nr = pltpu.roll(next8, SUBLANES - j, 0)
    bot = jnp.where(_rows(nr.shape) >= SUBLANES - j, nr, r[n - SUBLANES:])
    return jnp.concatenate([r[:n - SUBLANES], bot], axis=0)


def _scan_fwd(a, u):
    n = a.shape[0]
    row = _rows(a.shape)
    h = u
    k = 1
    while k < n:
        keep = row >= k
        h_sh = jnp.where(keep, pltpu.roll(h, k, 0), 0.0)
        a_sh = jnp.where(keep, pltpu.roll(a, k, 0), 1.0)
        h = a * h_sh + h
        a = a * a_sh
        k *= 2
    return h, a


def _scan_rev(b, d):
    n = b.shape[0]
    row = _rows(b.shape)
    g = d
    k = 1
    while k < n:
        keep = row < n - k
        g_sh = jnp.where(keep, pltpu.roll(g, n - k, 0), 0.0)
        b_sh = jnp.where(keep, pltpu.roll(b, n - k, 0), 1.0)
        g = b * g_sh + g
        b = b * b_sh
        k *= 2
    return g, b


def _cumsum_fwd(x):
    n = x.shape[0]
    row = _rows(x.shape)
    k = 1
    while k < n:
        x = x + jnp.where(row >= k, pltpu.roll(x, k, 0), 0.0)
        k *= 2
    return x


def _cumsum_rev(x):
    n = x.shape[0]
    row = _rows(x.shape)
    k = 1
    while k < n:
        x = x + jnp.where(row < n - k, pltpu.roll(x, n - k, 0), 0.0)
        k *= 2
    return x


def _dot(a, b):
    return lax.dot_general(a, b, (((1,), (0,)), ((), ())), preferred_element_type=F32)


def _dot_nt(a, b):
    return lax.dot_general(a, b, (((1,), (1,)), ((), ())), preferred_element_type=F32)


def _dot_tn(a, b):
    return lax.dot_general(a, b, (((0,), (0,)), ((), ())), preferred_element_type=F32)


def _dot_split(x, m_bf16):
    hi = x.astype(BF16)
    lo = (x - hi.astype(F32)).astype(BF16)
    return _dot(hi, m_bf16) + _dot(lo, m_bf16)


def _tile_rows(ts, s):
    return min(ts, s)


def _mm(a_list, w, *, trans_w=False, res=None, norm_bwd=None, out_dtype=F32, ts=512, nb=None, name):
    s = a_list[0].shape[0]
    ks = [a.shape[1] for a in a_list]
    k = sum(ks)
    n = w.shape[0] if trans_w else w.shape[1]
    ts = _tile_rows(ts, s)
    nb = n if nb is None else nb
    na = len(a_list)
    has_res = res is not None
    fused = norm_bwd is not None
    offs = [sum(ks[:p]) for p in range(na)]

    def body(*refs):
        a_refs = refs[:na]
        w_ref = refs[na]
        acc = None
        for a_ref, off, kk in zip(a_refs, offs, ks):
            a = a_ref[...].astype(BF16)
            if trans_w:
                part = _dot_nt(a, w_ref[:, off:off + kk])
            else:
                part = _dot(a, w_ref[off:off + kk, :])
            acc = part if acc is None else acc + part
        if has_res:
            acc = acc + refs[na + 1][...]
        if not fused:
            refs[-1][...] = acc.astype(out_dtype)
            return
        h_ref, g_ref, dres_ref, dh_ref, dg_ref = refs[na + 1:]
        i = pl.program_id(1)
        x = h_ref[...]
        r = lax.rsqrt(jnp.mean(x * x, axis=-1, keepdims=True) + EPS)
        xhat = x * r
        part = jnp.sum(acc * xhat, axis=0, keepdims=True)

        @pl.when(i == 0)
        def _():
            dg_ref[...] = part

        @pl.when(i > 0)
        def _():
            dg_ref[...] += part

        dxh = acc * g_ref[...]
        dh_ref[...] = dres_ref[...] + r * (dxh - xhat * jnp.mean(dxh * xhat, axis=-1, keepdims=True))

    in_specs = [pl.BlockSpec((ts, kk), lambda j, i: (i, 0)) for kk in ks]
    if trans_w:
        in_specs.append(pl.BlockSpec((nb, k), lambda j, i: (j, 0)))
    else:
        in_specs.append(pl.BlockSpec((k, nb), lambda j, i: (0, j)))
    args = list(a_list) + [w]
    tile = pl.BlockSpec((ts, nb), lambda j, i: (i, j))
    if has_res:
        in_specs.append(tile)
        args.append(res)
    if fused:
        assert nb == n and not has_res
        vec = pl.BlockSpec((1, n), lambda j, i: (0, 0))
        h, g, dres = norm_bwd
        return pl.pallas_call(
            body, name=name, grid=(1, s // ts), in_specs=in_specs + [tile, vec, tile],
            out_specs=(tile, vec),
            out_shape=(jax.ShapeDtypeStruct((s, n), F32), jax.ShapeDtypeStruct((1, n), F32)),
        )(*args, h, g, dres)
    return pl.pallas_call(
        body, name=name, grid=(n // nb, s // ts), in_specs=in_specs, out_specs=tile,
        out_shape=jax.ShapeDtypeStruct((s, n), out_dtype),
    )(*args)


def _mm_tn(a_list, b_list, *, ts=512, nb=None, name):
    s = b_list[0].shape[0]
    ks = [a.shape[1] for a in a_list]
    k = sum(ks)
    width = b_list[0].shape[1]
    n = width * len(b_list)
    ts = _tile_rows(ts, s)
    nb = width if nb is None else nb
    per = width // nb
    na = len(a_list)
    nparts = len(b_list)

    def body(*refs):
        a_refs = refs[:na]
        b_refs = refs[na:na + nparts]
        o_ref = refs[-1]
        j = pl.program_id(0)
        i = pl.program_id(1)
        parts = [r[...].astype(BF16) for r in a_refs]
        a = parts[0] if na == 1 else jnp.concatenate(parts, axis=1)

        def accumulate(b_ref):
            upd = _dot_tn(a, b_ref[...].astype(BF16))

            @pl.when(i == 0)
            def _():
                o_ref[...] = upd

            @pl.when(i > 0)
            def _():
                o_ref[...] += upd

        if nparts == 1:
            accumulate(b_refs[0])
        else:
            for part, b_ref in enumerate(b_refs):
                pl.when(j // per == part)(functools.partial(accumulate, b_ref))

    in_specs = [pl.BlockSpec((ts, kk), lambda j, i: (i, 0)) for kk in ks]
    for part in range(nparts):
        in_specs.append(pl.BlockSpec(
            (ts, nb), lambda j, i, part=part: (i, jnp.clip(j - part * per, 0, per - 1))))
    return pl.pallas_call(
        body, name=name, grid=(n // nb, s // ts), in_specs=in_specs,
        out_specs=pl.BlockSpec((k, nb), lambda j, i: (0, j)),
        out_shape=jax.ShapeDtypeStruct((k, n), F32),
    )(*a_list, *b_list)


def _norm_fwd(h, g, *, ts=512, name):
    s, d = h.shape
    ts = _tile_rows(ts, s)

    def body(h_ref, g_ref, n_ref):
        x = h_ref[...]
        r = lax.rsqrt(jnp.mean(x * x, axis=-1, keepdims=True) + EPS)
        n_ref[...] = ((x * r) * g_ref[...]).astype(BF16)

    return pl.pallas_call(
        body, name=name, grid=(s // ts,),
        in_specs=[pl.BlockSpec((ts, d), lambda i: (i, 0)), pl.BlockSpec((1, d), lambda i: (0, 0))],
        out_specs=pl.BlockSpec((ts, d), lambda i: (i, 0)),
        out_shape=jax.ShapeDtypeStruct((s, d), BF16),
    )(h, g)


def _norm_bwd(dn, h, g, dres, *, ts=512, name):
    s, d = h.shape
    ts = _tile_rows(ts, s)

    def body(dn_ref, h_ref, g_ref, dres_ref, dh_ref, dg_ref):
        i = pl.program_id(0)
        x = h_ref[...]
        dnv = dn_ref[...]
        r = lax.rsqrt(jnp.mean(x * x, axis=-1, keepdims=True) + EPS)
        xhat = x * r
        part = jnp.sum(dnv * xhat, axis=0, keepdims=True)

        @pl.when(i == 0)
        def _():
            dg_ref[...] = part

        @pl.when(i > 0)
        def _():
            dg_ref[...] += part

        dxh = dnv * g_ref[...]
        dh_ref[...] = dres_ref[...] + r * (dxh - xhat * jnp.mean(dxh * xhat, axis=-1, keepdims=True))

    tile = pl.BlockSpec((ts, d), lambda i: (i, 0))
    vec = pl.BlockSpec((1, d), lambda i: (0, 0))
    return pl.pallas_call(
        body, name=name, grid=(s // ts,), in_specs=[tile, tile, vec, tile],
        out_specs=(tile, vec),
        out_shape=(jax.ShapeDtypeStruct((s, d), F32), jax.ShapeDtypeStruct((1, d), F32)),
    )(dn, h, g, dres)


def _final(h, g, target, *, ts=512, name):
    s, d = h.shape
    ts = _tile_rows(ts, s)
    nt = s // ts

    def body(h_ref, g_ref, t_ref, dh_ref, loss_ref, dg_ref, acc_ref):
        i = pl.program_id(0)
        x = h_ref[...]
        r = lax.rsqrt(jnp.mean(x * x, axis=-1, keepdims=True) + EPS)
        xhat = x * r
        gv = g_ref[...]
        err = xhat * gv - t_ref[...]
        sq = jnp.sum(err * err, axis=0, keepdims=True)
        dy = err * (1.0 / d)
        part = jnp.sum(dy * xhat, axis=0, keepdims=True)

        @pl.when(i == 0)
        def _():
            acc_ref[...] = sq
            dg_ref[...] = part

        @pl.when(i > 0)
        def _():
            acc_ref[...] += sq
            dg_ref[...] += part

        dxh = dy * gv
        dh_ref[...] = r * (dxh - xhat * jnp.mean(dxh * xhat, axis=-1, keepdims=True))

        @pl.when(i == nt - 1)
        def _():
            tot = jnp.sum(acc_ref[...], axis=1, keepdims=True) * (0.5 / d)
            loss_ref[...] = jnp.broadcast_to(tot, (1, LANES))

    tile = pl.BlockSpec((ts, d), lambda i: (i, 0))
    vec = pl.BlockSpec((1, d), lambda i: (0, 0))
    return pl.pallas_call(
        body, name=name, grid=(nt,), in_specs=[tile, vec, tile],
        out_specs=(tile, pl.BlockSpec((1, LANES), lambda i: (0, 0)), vec),
        out_shape=(jax.ShapeDtypeStruct((s, d), F32), jax.ShapeDtypeStruct((1, LANES), F32),
                   jax.ShapeDtypeStruct((1, d), F32)),
        scratch_shapes=[pltpu.VMEM((1, d), F32)],
    )(h, g, target)


def _halo_map(ts, width_blocks):
    per = ts // SUBLANES

    def index(j, i):
        return (jnp.maximum(i * per - 1, 0), width_blocks(j))

    return index


def _even_gates(xc, wa, ba, wx, bx, sp):
    xb = xc.astype(BF16)
    r = _sigmoid(_dot(xb, wa) + ba)
    ig = _sigmoid(_dot(xb, wx) + bx)
    la = (-LRU_C) * r * sp
    a = jnp.exp(la)
    a2 = a * a
    m = jnp.sqrt(-jnp.tanh(la) * (1.0 + a2))
    return r, ig, la, a, a2, m


def _even_core_fwd(p, w4, b4, wa, ba, wx, bx, lam, w3, b3, *, ts=512, name):
    s = p.shape[0]
    ts = _tile_rows(ts, s)
    nt = s // ts
    nblk = 4

    def body(p_ref, ph_ref, w4_ref, b4_ref, wa_ref, ba_ref, wx_ref, bx_ref, lam_ref, w3_ref, b3_ref,
             ya_ref, yb_ref, hl_ref, hcar_ref):
        i = pl.program_id(1)
        first = (i > 0).astype(F32)
        xa = p_ref[:, 0:LANES]
        ga = p_ref[:, LANES:2 * LANES]
        cp = p_ref[:, 2 * LANES:3 * LANES]
        bp = p_ref[:, 3 * LANES:4 * LANES]
        vb = p_ref[:, 4 * LANES:5 * LANES]
        xa_h = ph_ref[:, 0:LANES] * first
        s_h = ph_ref[:, 2 * LANES:3 * LANES] * ph_ref[:, 4 * LANES:5 * LANES] * first

        xc = b4_ref[...] + w4_ref[3:4, :] * xa
        for k in range(3):
            xc = xc + w4_ref[k:k + 1, :] * _shift_down(xa, xa_h, 3 - k)
        sp = _softplus(-lam_ref[...])
        _, ig, _, a, _, m = _even_gates(xc, wa_ref[0], ba_ref[...], wx_ref[0], bx_ref[...], sp)
        u = m * (ig * xc)
        hs, acum = _scan_fwd(a, u)

        @pl.when(i == 0)
        def _():
            hcar_ref[...] = jnp.zeros_like(hcar_ref)

        hs = hs + acum * hcar_ref[0:1, :]
        hl_ref[...] = hs
        hcar_ref[0:1, :] = hl_ref[ts - 1:ts, :]
        ge, _ = _gelu(ga)
        ya_ref[...] = (hs * ge).astype(BF16)

        sv = cp * vb
        sc = b3_ref[...] + w3_ref[2:3, :] * sv
        for k in range(2):
            sc = sc + w3_ref[k:k + 1, :] * _shift_down(sv, s_h, 2 - k)
        yb_ref[...] = (bp * sc).astype(BF16)

    blk = pl.BlockSpec((ts, 5 * LANES), lambda j, i: (i, j))
    halo = pl.BlockSpec((SUBLANES, 5 * LANES), _halo_map(ts, lambda j: j))
    vec = pl.BlockSpec((1, LANES), lambda j, i: (0, j))
    out = pl.BlockSpec((ts, LANES), lambda j, i: (i, j))
    return pl.pallas_call(
        body, name=name, grid=(nblk, nt),
        in_specs=[blk, halo,
                  pl.BlockSpec((4, LANES), lambda j, i: (0, j)), vec,
                  pl.BlockSpec((1, LANES, LANES), lambda j, i: (j, 0, 0)), vec,
                  pl.BlockSpec((1, LANES, LANES), lambda j, i: (j, 0, 0)), vec, vec,
                  pl.BlockSpec((3, LANES), lambda j, i: (0, j)), vec],
        out_specs=(out, out, out),
        out_shape=(jax.ShapeDtypeStruct((s, 4 * LANES), BF16), jax.ShapeDtypeStruct((s, 4 * LANES), BF16),
                   jax.ShapeDtypeStruct((s, 4 * LANES), F32)),
        scratch_shapes=[pltpu.VMEM((SUBLANES, LANES), F32)],
    )(p, p, w4, b4, wa, ba, wx, bx, lam, w3, b3)


def _even_core_bwd(dy, p, hl, w4, b4, wa, wat, ba, wx, wxt, bx, lam, w3, b3, *, ts=256, name):
    s = p.shape[0]
    ts = _tile_rows(ts, s)
    nt = s // ts
    nblk = 4
    per = ts // SUBLANES

    def body(dya_ref, dyb_ref, p_ref, ph_ref, hl_ref, hh_ref,
             w4_ref, b4_ref, wa_ref, wat_ref, ba_ref, wx_ref, wxt_ref, bx_ref, lam_ref, w3_ref, b3_ref,
             dp_ref, dw4_ref, db4_ref, dwa_ref, dba_ref, dwx_ref, dbx_ref, dlam_ref, dw3_ref, db3_ref,
             dxc_nx, dsc_nx, cg_ref):
        i = pl.program_id(1)
        ti = nt - 1 - i
        first = (ti > 0).astype(F32)
        xa = p_ref[:, 0:LANES]
        ga = p_ref[:, LANES:2 * LANES]
        cp = p_ref[:, 2 * LANES:3 * LANES]
        bp = p_ref[:, 3 * LANES:4 * LANES]
        vb = p_ref[:, 4 * LANES:5 * LANES]
        xa_h = ph_ref[:, 0:LANES] * first
        s_h = ph_ref[:, 2 * LANES:3 * LANES] * ph_ref[:, 4 * LANES:5 * LANES] * first
        h_h = hh_ref[...] * first

        @pl.when(i == 0)
        def _():
            dxc_nx[...] = jnp.zeros_like(dxc_nx)
            dsc_nx[...] = jnp.zeros_like(dsc_nx)
            cg_ref[...] = jnp.zeros_like(cg_ref)
            for ref in (dw4_ref, db4_ref, dwa_ref, dba_ref, dwx_ref, dbx_ref, dlam_ref, dw3_ref, db3_ref):
                ref[...] = jnp.zeros_like(ref)

        xa_sh = [_shift_down(xa, xa_h, 3 - k) for k in range(3)] + [xa]
        xc = b4_ref[...]
        for k in range(4):
            xc = xc + w4_ref[k:k + 1, :] * xa_sh[k]
        lamv = lam_ref[...]
        sp = _softplus(-lamv)
        r, ig, _, a, a2, m = _even_gates(xc, wa_ref[0], ba_ref[...], wx_ref[0], bx_ref[...], sp)
        sv = cp * vb
        sv_sh = [_shift_down(sv, s_h, 2 - k) for k in range(2)] + [sv]
        sc = b3_ref[...]
        for k in range(3):
            sc = sc + w3_ref[k:k + 1, :] * sv_sh[k]
        hs = hl_ref[...]
        h_prev = _shift_down(hs, h_h, 1)

        dya = dya_ref[...]
        dyb = dyb_ref[...]
        ge, gt = _gelu(ga)
        dga = dya * hs * _gelu_grad(ga, gt)
        dh = dya * ge

        ones8 = jnp.ones((SUBLANES, LANES), F32)
        b = _shift_up(a, ones8, 1)
        g, bcum = _scan_rev(b, dh)
        g = g + bcum * cg_ref[0:1, :]
        ag = a * g
        cg_ref[...] = ag[:SUBLANES]

        da = g * h_prev
        xi = ig * xc
        dm = g * xi
        dig = g * m * xc
        dxc = g * m * ig
        dla = da * a - dm * (a2 / m)
        dr = dla * ((-LRU_C) * sp)
        dlam_ref[...] += jnp.sum(dla * r, axis=0, keepdims=True) * (LRU_C * _sigmoid(-lamv))
        dra = dr * r * (1.0 - r)
        dia = dig * ig * (1.0 - ig)
        drab = dra.astype(BF16)
        diab = dia.astype(BF16)
        xcb = xc.astype(BF16)
        dxc = dxc + _dot(drab, wat_ref[0]) + _dot(diab, wxt_ref[0])
        dwa_ref[0] += _dot_tn(xcb, drab)
        dwx_ref[0] += _dot_tn(xcb, diab)
        dba_ref[...] += jnp.sum(dra, axis=0, keepdims=True)
        dbx_ref[...] += jnp.sum(dia, axis=0, keepdims=True)

        nx = dxc_nx[...]
        dxa = w4_ref[3:4, :] * dxc
        for k in range(3):
            dxa = dxa + w4_ref[k:k + 1, :] * _shift_up(dxc, nx, 3 - k)
        for k in range(4):
            dw4_ref[k:k + 1, :] += jnp.sum(dxc * xa_sh[k], axis=0, keepdims=True)
        db4_ref[...] += jnp.sum(dxc, axis=0, keepdims=True)
        dxc_nx[...] = dxc[:SUBLANES]

        dbp = dyb * sc
        dsc = dyb * bp
        nsc = dsc_nx[...]
        ds = w3_ref[2:3, :] * dsc
        for k in range(2):
            ds = ds + w3_ref[k:k + 1, :] * _shift_up(dsc, nsc, 2 - k)
        for k in range(3):
            dw3_ref[k:k + 1, :] += jnp.sum(dsc * sv_sh[k], axis=0, keepdims=True)
        db3_ref[...] += jnp.sum(dsc, axis=0, keepdims=True)
        dsc_nx[...] = dsc[:SUBLANES]

        dp_ref[:, 0:LANES] = dxa.astype(BF16)
        dp_ref[:, LANES:2 * LANES] = dga.astype(BF16)
        dp_ref[:, 2 * LANES:3 * LANES] = (ds * vb).astype(BF16)
        dp_ref[:, 3 * LANES:4 * LANES] = dbp.astype(BF16)
        dp_ref[:, 4 * LANES:5 * LANES] = (ds * cp).astype(BF16)

    def rev(j, i):
        return (nt - 1 - i, j)

    def rev_halo(col):
        def index(j, i):
            return (jnp.maximum((nt - 1 - i) * per - 1, 0), col(j))
        return index

    blk = pl.BlockSpec((ts, 5 * LANES), rev)
    one = pl.BlockSpec((ts, LANES), rev)
    vec = pl.BlockSpec((1, LANES), lambda j, i: (0, j))
    mat = pl.BlockSpec((1, LANES, LANES), lambda j, i: (j, 0, 0))
    w4s = pl.BlockSpec((4, LANES), lambda j, i: (0, j))
    w3s = pl.BlockSpec((3, LANES), lambda j, i: (0, j))
    f = jax.ShapeDtypeStruct
    return pl.pallas_call(
        body, name=name, grid=(nblk, nt),
        in_specs=[one, pl.BlockSpec((ts, LANES), lambda j, i: (nt - 1 - i, 4 + j)),
                  blk, pl.BlockSpec((SUBLANES, 5 * LANES), rev_halo(lambda j: j)),
                  one, pl.BlockSpec((SUBLANES, LANES), rev_halo(lambda j: j)),
                  w4s, vec, mat, mat, vec, mat, mat, vec, vec, w3s, vec],
        out_specs=(blk, w4s, vec, mat, vec, mat, vec, vec, w3s, vec),
        out_shape=(f((s, 20 * LANES), BF16), f((4, 4 * LANES), F32), f((1, 4 * LANES), F32),
                   f((4, LANES, LANES), F32), f((1, 4 * LANES), F32),
                   f((4, LANES, LANES), F32), f((1, 4 * LANES), F32), f((1, 4 * LANES), F32),
                   f((3, 4 * LANES), F32), f((1, 4 * LANES), F32)),
        scratch_shapes=[pltpu.VMEM((SUBLANES, LANES), F32), pltpu.VMEM((SUBLANES, LANES), F32),
                        pltpu.VMEM((SUBLANES, LANES), F32)],
    )(dy, dy, p, p, hl, hl, w4, b4, wa, wat, ba, wx, wxt, bx, lam, w3, b3)


def _ffn_conv(u_ref, uh_ref, w_ref, b_ref, first):
    u = u_ref[...].astype(F32)
    u_h = uh_ref[...].astype(F32)[SUBLANES:] * first
    u_sh = [_shift_down(u, u_h, 2 - k) for k in range(2)] + [u]
    hc = b_ref[...]
    for k in range(3):
        hc = hc + w_ref[k:k + 1, :] * u_sh[k]
    return hc, u_sh


def _ffn_specs(ts, row, halo_row):
    nblk = D_FF // FFN_CB
    specs = []
    for off in (0, nblk):
        specs.append(pl.BlockSpec((ts, FFN_CB), lambda j, i, off=off: (row(i), off + j)))
        specs.append(pl.BlockSpec((16, FFN_CB), lambda j, i, off=off: (halo_row(i), off + j)))
        specs.append(pl.BlockSpec((3, FFN_CB), lambda j, i, off=off: (0, off + j)))
        specs.append(pl.BlockSpec((1, FFN_CB), lambda j, i, off=off: (0, off + j)))
    return specs


def _ffn_core_fwd(up, w, b, *, ts=512, name):
    s = up.shape[0]
    ts = _tile_rows(ts, s)
    nt = s // ts
    nblk = D_FF // FFN_CB
    per = ts // 16

    def body(g_ref, gh_ref, wg_ref, bg_ref, v_ref, vh_ref, wv_ref, bv_ref, act_ref):
        first = (pl.program_id(1) > 0).astype(F32)
        gate, _ = _ffn_conv(g_ref, gh_ref, wg_ref, bg_ref, first)
        val, _ = _ffn_conv(v_ref, vh_ref, wv_ref, bv_ref, first)
        act_ref[...] = (gate * _sigmoid_tanh(gate) * val).astype(BF16)

    return pl.pallas_call(
        body, name=name, grid=(nblk, nt),
        in_specs=_ffn_specs(ts, lambda i: i, lambda i: jnp.maximum(i * per - 1, 0)),
        out_specs=pl.BlockSpec((ts, FFN_CB), lambda j, i: (i, j)),
        out_shape=jax.ShapeDtypeStruct((s, D_FF), BF16),
    )(up, up, w, b, up, up, w, b)


def _ffn_core_bwd(dact, up, w, b, *, ts=512, name):
    s = up.shape[0]
    ts = _tile_rows(ts, s)
    nt = s // ts
    nblk = D_FF // FFN_CB
    per = ts // 16

    def conv_bwd(dhc, u_sh, w_ref, nx_ref, du_ref, dw_ref, db_ref):
        nx = nx_ref[...]
        du = w_ref[2:3, :] * dhc
        for k in range(2):
            du = du + w_ref[k:k + 1, :] * _shift_up(dhc, nx, 2 - k)
        du_ref[...] = du.astype(BF16)
        for k in range(3):
            dw_ref[k:k + 1, :] += jnp.sum(dhc * u_sh[k], axis=0, keepdims=True)
        db_ref[...] += jnp.sum(dhc, axis=0, keepdims=True)
        nx_ref[...] = dhc[:SUBLANES]

    def body(da_ref, g_ref, gh_ref, wg_ref, bg_ref, v_ref, vh_ref, wv_ref, bv_ref,
             dg_ref, dv_ref, dwg_ref, dwv_ref, dbg_ref, dbv_ref, nxg_ref, nxv_ref):
        i = pl.program_id(1)
        first = (nt - 1 - i > 0).astype(F32)
        gate, g_sh = _ffn_conv(g_ref, gh_ref, wg_ref, bg_ref, first)
        val, v_sh = _ffn_conv(v_ref, vh_ref, wv_ref, bv_ref, first)
        da = da_ref[...].astype(F32)
        sg = _sigmoid_tanh(gate)
        dgate = da * val * (sg * (1.0 + gate * (1.0 - sg)))
        dval = da * (gate * sg)

        @pl.when(i == 0)
        def _():
            for ref in (nxg_ref, nxv_ref, dwg_ref, dwv_ref, dbg_ref, dbv_ref):
                ref[...] = jnp.zeros_like(ref)

        conv_bwd(dgate, g_sh, wg_ref, nxg_ref, dg_ref, dwg_ref, dbg_ref)
        conv_bwd(dval, v_sh, wv_ref, nxv_ref, dv_ref, dwv_ref, dbv_ref)

    def rev(i):
        return nt - 1 - i

    tile = pl.BlockSpec((ts, FFN_CB), lambda j, i: (rev(i), j))
    w_out = pl.BlockSpec((3, FFN_CB), lambda j, i: (0, j))
    b_out = pl.BlockSpec((1, FFN_CB), lambda j, i: (0, j))
    f = jax.ShapeDtypeStruct
    return pl.pallas_call(
        body, name=name, grid=(nblk, nt),
        in_specs=[tile] + _ffn_specs(ts, rev, lambda i: jnp.maximum(rev(i) * per - 1, 0)),
        out_specs=(tile, tile, w_out, w_out, b_out, b_out),
        out_shape=(f((s, D_FF), BF16), f((s, D_FF), BF16), f((3, D_FF), F32), f((3, D_FF), F32),
                   f((1, D_FF), F32), f((1, D_FF), F32)),
        scratch_shapes=[pltpu.VMEM((SUBLANES, FFN_CB), F32), pltpu.VMEM((SUBLANES, FFN_CB), F32)],
    )(dact, up, up, w, b, up, up, w, b)


def _sgu_forward_block(zu, zg, gn, w_ref, bias, seg):
    u, tu = _gelu(zu)
    g, tg = _gelu(zg)
    ms = _dot_split(g * g, seg)
    rs = lax.rsqrt(ms + EPS)
    ghat = g * rs
    gv = ghat * gn
    gvb = gv.astype(BF16)
    lane = _lanes((CHUNK, LANES))
    chunks = []
    for c in range(zu.shape[0] // CHUNK):
        gc = gvb[c * CHUNK:(c + 1) * CHUNK]
        mix = jnp.where(lane < 64, _dot(w_ref[0], gc), _dot(w_ref[1], gc)) + bias
        chunks.append(mix)
    mixed = chunks[0] if len(chunks) == 1 else jnp.concatenate(chunks, axis=0)
    return u, tu, g, tg, rs, ghat, gvb, mixed


def _sgu_fwd(p1, gn, w, bias, seg, *, ts=512, name):
    s = p1.shape[0]
    ts = _tile_rows(ts, s)

    def body(zu_ref, zg_ref, gn_ref, w_ref, bias_ref, seg_ref, yc_ref):
        u, _, _, _, _, _, _, mixed = _sgu_forward_block(
            zu_ref[...], zg_ref[...], gn_ref[...], w_ref, bias_ref[...], seg_ref[...])
        yc_ref[...] = (u * mixed).astype(BF16)

    return pl.pallas_call(
        body, name=name, grid=(4, s // ts),
        in_specs=[pl.BlockSpec((ts, LANES), lambda j, i: (i, j)),
                  pl.BlockSpec((ts, LANES), lambda j, i: (i, 4 + j)),
                  pl.BlockSpec((1, LANES), lambda j, i: (0, j)),
                  pl.BlockSpec((2, CHUNK, CHUNK), lambda j, i: (j, 0, 0)),
                  pl.BlockSpec((CHUNK, LANES), lambda j, i: (0, j)),
                  pl.BlockSpec((LANES, LANES), lambda j, i: (0, 0))],
        out_specs=pl.BlockSpec((ts, LANES), lambda j, i: (i, j)),
        out_shape=jax.ShapeDtypeStruct((s, 4 * LANES), BF16),
    )(p1, p1, gn, w, bias, seg)


def _sgu_bwd(p1, dy, gn, w, wt, bias, seg, tril, *, ts=512, name):
    s = p1.shape[0]
    ts = _tile_rows(ts, s)
    nt = s // ts

    def body(zu_ref, zg_ref, dy_ref, gn_ref, w_ref, wt_ref, bias_ref, seg_ref, tril_ref,
             dzu_ref, dzg_ref, dw_ref, dbias_ref, dgn_ref):
        i = pl.program_id(1)
        zu = zu_ref[...]
        zg = zg_ref[...]
        gn_v = gn_ref[...]
        segv = seg_ref[...]
        u, tu, g, tg, rs, ghat, gvb, mixed = _sgu_forward_block(zu, zg, gn_v, w_ref, bias_ref[...], segv)
        dyv = dy_ref[...]
        du = dyv * mixed
        dmx = dyv * u

        @pl.when(i == 0)
        def _():
            dw_ref[...] = jnp.zeros_like(dw_ref)
            dbias_ref[...] = jnp.zeros_like(dbias_ref)
            dgn_ref[...] = jnp.zeros_like(dgn_ref)

        lane = _lanes((CHUNK, LANES))
        dgv_chunks = []
        dbias = jnp.zeros((CHUNK, LANES), F32)
        for c in range(ts // CHUNK):
            dmc = dmx[c * CHUNK:(c + 1) * CHUNK]
            gc = gvb[c * CHUNK:(c + 1) * CHUNK]
            dm_a = jnp.where(lane < 64, dmc, 0.0).astype(BF16)
            dm_b = jnp.where(lane >= 64, dmc, 0.0).astype(BF16)
            dw_ref[0] += _dot_nt(dm_a, gc)
            dw_ref[1] += _dot_nt(dm_b, gc)
            dgv_chunks.append(_dot(wt_ref[0], dm_a) + _dot(wt_ref[1], dm_b))
            dbias = dbias + dmc
        dbias_ref[...] += dbias
        dgv = dgv_chunks[0] if len(dgv_chunks) == 1 else jnp.concatenate(dgv_chunks, axis=0)
        dgn_ref[...] += jnp.sum(dgv * ghat, axis=0, keepdims=True)
        dgh = dgv * gn_v
        dg = rs * (dgh - ghat * _dot_split(dgh * ghat, segv))
        dzu_ref[...] = (du * _gelu_grad(zu, tu)).astype(BF16)
        dzg_ref[...] = (dg * _gelu_grad(zg, tg)).astype(BF16)

        @pl.when(i == nt - 1)
        def _():
            dw_ref[0] = dw_ref[0] * tril_ref[...]
            dw_ref[1] = dw_ref[1] * tril_ref[...]

    f = jax.ShapeDtypeStruct
    colj = pl.BlockSpec((ts, LANES), lambda j, i: (i, j))
    wsp = pl.BlockSpec((2, CHUNK, CHUNK), lambda j, i: (j, 0, 0))
    sq = pl.BlockSpec((LANES, LANES), lambda j, i: (0, 0))
    return pl.pallas_call(
        body, name=name, grid=(4, nt),
        in_specs=[colj, pl.BlockSpec((ts, LANES), lambda j, i: (i, 4 + j)), colj,
                  pl.BlockSpec((1, LANES), lambda j, i: (0, j)), wsp, wsp,
                  pl.BlockSpec((CHUNK, LANES), lambda j, i: (0, j)), sq, sq],
        out_specs=(colj, colj, wsp, pl.BlockSpec((CHUNK, LANES), lambda j, i: (0, j)),
                   pl.BlockSpec((1, LANES), lambda j, i: (0, j))),
        out_shape=(f((s, 4 * LANES), BF16), f((s, 4 * LANES), BF16), f((8, CHUNK, CHUNK), F32),
                   f((CHUNK, 4 * LANES), F32), f((1, 4 * LANES), F32)),
    )(p1, p1, dy, gn, w, wt, bias, seg, tril)


F_COL = 20


def _fcum_fwd(p1, bf, *, ts=512, name):
    s = p1.shape[0]
    ts = _tile_rows(ts, s)

    def body(f_ref, bf_ref, c_ref, car_ref):
        i = pl.program_id(0)
        z = f_ref[...] + bf_ref[...]
        logf = jnp.minimum(z, 0.0) - _log1p_pos(jnp.exp(-jnp.abs(z)))

        @pl.when(i == 0)
        def _():
            car_ref[...] = jnp.zeros_like(car_ref)

        c_ref[...] = _cumsum_fwd(logf) + car_ref[0:1, :]
        car_ref[0:1, :] = c_ref[ts - 1:ts, :]

    return pl.pallas_call(
        body, name=name, grid=(s // ts,),
        in_specs=[pl.BlockSpec((ts, LANES), lambda i: (i, F_COL)), pl.BlockSpec((1, LANES), lambda i: (0, 0))],
        out_specs=pl.BlockSpec((ts, LANES), lambda i: (i, 0)),
        out_shape=jax.ShapeDtypeStruct((s, LANES), F32),
        scratch_shapes=[pltpu.VMEM((SUBLANES, LANES), F32)],
    )(p1, bf)


def _fcum_bwd(dcs, dcq, p1, bf, *, ts=512, name):
    s = p1.shape[0]
    ts = _tile_rows(ts, s)
    nt = s // ts

    def body(dc_ref, dcq_ref, f_ref, bf_ref, df_ref, dbf_ref, car_ref):
        i = pl.program_id(0)

        @pl.when(i == 0)
        def _():
            car_ref[...] = jnp.zeros_like(car_ref)
            dbf_ref[...] = jnp.zeros_like(dbf_ref)

        dc = dc_ref[...]
        lane = _lanes((ts, LANES))
        for h in range(8):
            dc = dc + jnp.where(lane == h, dcq_ref[:, h * LANES:(h + 1) * LANES], 0.0)
        dlog = _cumsum_rev(dc) + car_ref[0:1, :]
        car_ref[...] = dlog[:SUBLANES]
        z = f_ref[...] + bf_ref[...]
        df = dlog * _sigmoid(-z)
        df_ref[...] = df.astype(BF16)
        dbf_ref[...] += jnp.sum(df, axis=0, keepdims=True)

    return pl.pallas_call(
        body, name=name, grid=(nt,),
        in_specs=[pl.BlockSpec((ts, LANES), lambda i: (nt - 1 - i, 0)),
                  pl.BlockSpec((ts, 8 * LANES), lambda i: (nt - 1 - i, 0)),
                  pl.BlockSpec((ts, LANES), lambda i: (nt - 1 - i, F_COL)),
                  pl.BlockSpec((1, LANES), lambda i: (0, 0))],
        out_specs=(pl.BlockSpec((ts, LANES), lambda i: (nt - 1 - i, 0)), pl.BlockSpec((1, LANES), lambda i: (0, 0))),
        out_shape=(jax.ShapeDtypeStruct((s, LANES), BF16), jax.ShapeDtypeStruct((1, LANES), F32)),
        scratch_shapes=[pltpu.VMEM((SUBLANES, LANES), F32)],
    )(dcs, dcq, p1, bf)


def _fox_scores(qm, kb, bias, ck, diagonal):
    sc = _dot_nt(qm, kb) + bias - ck
    if diagonal:
        sc = jnp.where(_lanes(sc.shape) <= _rows(sc.shape), sc, NEG)
    return sc


def _head_masks(shape):
    lane = _lanes(shape)
    return lane < 64, lane >= 64


def _fox_fwd(p1, cq, ck, *, tq=512, name):
    s = p1.shape[0]
    tq = _tile_rows(tq, s)
    tk = tq
    nq = s // tq

    def body(q_ref, k_ref, v_ref, cq_ref, ck_ref, o_ref, lb_ref):
        qi = pl.program_id(1)
        q = q_ref[...] * 0.125
        first, second = _head_masks((tq, LANES))
        qms = [jnp.where(sel, q, 0.0).astype(BF16) for sel in (first, second)]
        cqs = [cq_ref[:, hh * LANES:(hh + 1) * LANES] for hh in range(2)]
        biases = [jnp.tile(cqh, (1, tk // LANES)) for cqh in cqs]

        def step(kj, carry, diagonal):
            cols = pl.ds(pl.multiple_of(kj * tk, tk), tk)
            kb = k_ref[cols, :].astype(BF16)
            vb = v_ref[cols, :].astype(BF16)
            new, outs = [], []
            acc = carry[4]
            for hh in range(2):
                m_prev, l_prev = carry[2 * hh], carry[2 * hh + 1]
                sc = _fox_scores(qms[hh], kb, biases[hh], ck_ref[hh, :, cols], diagonal)
                m_new = jnp.maximum(m_prev, jnp.max(sc, axis=1, keepdims=True))
                pm = jnp.exp(sc - jnp.tile(m_new, (1, tk // LANES)))
                alpha = jnp.exp(m_prev - m_new)
                new += [m_new, alpha * l_prev + jnp.sum(pm, axis=1, keepdims=True)]
                outs.append(acc * alpha + _dot(pm.astype(BF16), vb))
            return tuple(new) + (jnp.where(first, outs[0], outs[1]),)

        zero = jnp.zeros((tq, LANES), F32)
        low = jnp.full((tq, LANES), NEG, F32)
        carry = lax.fori_loop(0, qi, lambda kj, c: step(kj, c, False), (low, zero, low, zero, zero))
        m0, l0, m1, l1, acc = step(qi, carry, True)
        o_ref[...] = (acc / jnp.where(first, l0, l1)).astype(BF16)
        lb_ref[:, 0:LANES] = cqs[0] - (m0 + jnp.log(l0))
        lb_ref[:, LANES:2 * LANES] = cqs[1] - (m1 + jnp.log(l1))

    return pl.pallas_call(
        body, name=name, grid=(4, nq),
        in_specs=[pl.BlockSpec((tq, LANES), lambda j, qi: (qi, 8 + j)),
                  pl.BlockSpec((s, LANES), lambda j, qi: (0, 12 + j)),
                  pl.BlockSpec((s, LANES), lambda j, qi: (0, 16 + j)),
                  pl.BlockSpec((tq, 2 * LANES), lambda j, qi: (qi, j)),
                  pl.BlockSpec((2, 1, s), lambda j, qi: (j, 0, 0))],
        out_specs=(pl.BlockSpec((tq, LANES), lambda j, qi: (qi, j)),
                   pl.BlockSpec((tq, 2 * LANES), lambda j, qi: (qi, j))),
        out_shape=(jax.ShapeDtypeStruct((s, 4 * LANES), BF16), jax.ShapeDtypeStruct((s, 8 * LANES), F32)),
    )(p1, p1, p1, cq, ck)


def _fox_delta(dy, o, sel, *, ts=512, name):
    s = o.shape[0]
    ts = _tile_rows(ts, s)

    def body(do_ref, o_ref, sel_ref, d_ref):
        prod = do_ref[...] * o_ref[...].astype(F32)
        d_ref[:, 0:LANES] = _dot_split(prod, sel_ref[0])
        d_ref[:, LANES:2 * LANES] = _dot_split(prod, sel_ref[1])

    return pl.pallas_call(
        body, name=name, grid=(4, s // ts),
        in_specs=[pl.BlockSpec((ts, LANES), lambda j, i: (i, 4 + j)),
                  pl.BlockSpec((ts, LANES), lambda j, i: (i, j)),
                  pl.BlockSpec((2, LANES, LANES), lambda j, i: (0, 0, 0))],
        out_specs=pl.BlockSpec((ts, 2 * LANES), lambda j, i: (i, j)),
        out_shape=jax.ShapeDtypeStruct((s, 8 * LANES), F32),
    )(dy, o, sel)


def _fox_bwd(p1, dy, lb, delta, ck, *, tq=512, name):
    s = p1.shape[0]
    tq = _tile_rows(tq, s)
    tk = tq
    nq = s // tq

    def body(q_ref, k_ref, v_ref, do_ref, lb_ref, dl_ref, ck_ref,
             dq_ref, dk_ref, dv_ref, dck_ref, dcq_ref, dqa_ref, dra_ref):
        kj = pl.program_id(1)

        @pl.when(kj == 0)
        def _():
            dqa_ref[...] = jnp.zeros_like(dqa_ref)
            dra_ref[...] = jnp.zeros_like(dra_ref)

        kf = k_ref[...]
        kb = kf.astype(BF16)
        vb = v_ref[...].astype(BF16)
        first, second = _head_masks((tk, LANES))
        kms = [jnp.where(sel, kf, 0.0).astype(BF16) for sel in (first, second)]
        cks = [ck_ref[hh] for hh in range(2)]

        def step(qi, carry, diagonal):
            dk_acc, dv_acc, dc0, dc1 = carry
            dcs = [dc0, dc1]
            rows = pl.ds(pl.multiple_of(qi * tq, tq), tq)
            q = q_ref[rows, :] * 0.125
            do = do_ref[rows, :]
            for hh, sel in enumerate((first, second)):
                qm = jnp.where(sel, q, 0.0).astype(BF16)
                dom = jnp.where(sel, do, 0.0).astype(BF16)
                bias = jnp.tile(lb_ref[rows, hh * LANES:(hh + 1) * LANES], (1, tk // LANES))
                pm = jnp.exp(_fox_scores(qm, kb, bias, cks[hh], diagonal))
                dv_acc = dv_acc + _dot_tn(pm.astype(BF16), dom)
                dp = _dot_nt(dom, vb)
                ds = pm * (dp - jnp.tile(dl_ref[rows, hh * LANES:(hh + 1) * LANES], (1, tk // LANES)))
                dsb = ds.astype(BF16)
                dk_acc = dk_acc + _dot_tn(dsb, qm)
                dcs[hh] = dcs[hh] - jnp.sum(ds, axis=0, keepdims=True)
                dqa_ref[rows, :] += _dot(dsb, kms[hh])
                dra_ref[hh, rows, :] += jnp.sum(ds, axis=1, keepdims=True)
            return dk_acc, dv_acc, dcs[0], dcs[1]

        zero = jnp.zeros((tk, LANES), F32)
        zrow = jnp.zeros((1, tk), F32)
        carry = step(kj, (zero, zero, zrow, zrow), True)
        dk_acc, dv_acc, dc0, dc1 = lax.fori_loop(kj + 1, nq, lambda qi, c: step(qi, c, False), carry)
        dk_ref[...] = dk_acc.astype(BF16)
        dv_ref[...] = dv_acc.astype(BF16)
        dck_ref[0] = dc0
        dck_ref[1] = dc1

        @pl.when(kj == nq - 1)
        def _():
            dq_ref[...] = (dqa_ref[...] * 0.125).astype(BF16)
            dcq_ref[:, 0:LANES] = dra_ref[0]
            dcq_ref[:, LANES:2 * LANES] = dra_ref[1]

    def full(width, col0):
        return pl.BlockSpec((s, width), lambda j, kj: (0, col0 + j))

    kblk = pl.BlockSpec((tk, LANES), lambda j, kj: (kj, j))
    f = jax.ShapeDtypeStruct
    return pl.pallas_call(
        body, name=name, grid=(4, nq),
        in_specs=[full(LANES, 8),
                  pl.BlockSpec((tk, LANES), lambda j, kj: (kj, 12 + j)),
                  pl.BlockSpec((tk, LANES), lambda j, kj: (kj, 16 + j)),
                  full(LANES, 4), full(2 * LANES, 0), full(2 * LANES, 0),
                  pl.BlockSpec((2, 1, tk), lambda j, kj: (j, 0, kj))],
        out_specs=(full(LANES, 0), kblk, kblk, pl.BlockSpec((2, 1, tk), lambda j, kj: (j, 0, kj)),
                   full(2 * LANES, 0)),
        out_shape=(f((s, 4 * LANES), BF16), f((s, 4 * LANES), BF16), f((s, 4 * LANES), BF16),
                   f((8, 1, s), F32), f((s, 8 * LANES), F32)),
        scratch_shapes=[pltpu.VMEM((s, LANES), F32), pltpu.VMEM((2, s, LANES), F32)],
    )(p1, p1, p1, dy, lb, delta, ck)


def _row_block(r, cap=256):
    best = None
    for rb in range(2 * SUBLANES, min(r, cap) + 1, 2 * SUBLANES):
        if r % rb == 0:
            best = rb
    return r if best is None else best


def _adamw(w, g, m, v, *, name):
    r, c = w.shape
    rb = _row_block(r)

    def body(w_ref, g_ref, m_ref, v_ref, d_ref, nm_ref, nv_ref):
        gv = g_ref[...]
        mn = ADAM_B1 * m_ref[...] + (1.0 - ADAM_B1) * gv
        vn = ADAM_B2 * v_ref[...] + (1.0 - ADAM_B2) * (gv * gv)
        m_hat = mn / ADAM_C1
        v_hat = vn / ADAM_C2
        d_ref[...] = (-ADAM_LR) * (m_hat / (jnp.sqrt(v_hat) + ADAM_EPS) + ADAM_WD * w_ref[...])
        nm_ref[...] = mn
        nv_ref[...] = vn

    blk = pl.BlockSpec((rb, c), lambda i: (i, 0))
    shp = jax.ShapeDtypeStruct((r, c), F32)
    return pl.pallas_call(
        body, name=name, grid=(r // rb,), in_specs=[blk] * 4, out_specs=(blk,) * 3, out_shape=(shp,) * 3,
    )(w, g, m, v)


def _adamw_halves(w, mine, theirs, m, v, core, *, name):
    layers, r, c = w.shape
    rh = r // 2
    rb = _row_block(rh)
    per = rh // rb

    def body(core_ref, w_ref, *refs):
        g_refs = refs[:2 * layers]
        m_ref, v_ref, g_ref, d_ref, nm_ref, nv_ref = refs[2 * layers:]
        own = pl.program_id(1) == core_ref[0]
        gv = jnp.where(own, g_refs[0][...], g_refs[layers][...])
        for l in range(1, layers):
            gv = jnp.where(pl.program_id(0) == l, jnp.where(own, g_refs[l][...], g_refs[layers + l][...]), gv)
        g_ref[...] = gv
        mn = ADAM_B1 * m_ref[...] + (1.0 - ADAM_B1) * gv
        vn = ADAM_B2 * v_ref[...] + (1.0 - ADAM_B2) * (gv * gv)
        m_hat = mn / ADAM_C1
        v_hat = vn / ADAM_C2
        d_ref[...] = (-ADAM_LR) * (m_hat / (jnp.sqrt(v_hat) + ADAM_EPS) + ADAM_WD * w_ref[...])
        nm_ref[...] = mn
        nv_ref[...] = vn

    full = pl.BlockSpec((None, rb, c), lambda l, h, i, core_ref: (l, h * per + i, 0))
    half = pl.BlockSpec((rb, c), lambda l, h, i, core_ref: (i, 0))
    shp = jax.ShapeDtypeStruct((layers, r, c), F32)
    return pl.pallas_call(
        body, name=name,
        grid_spec=pltpu.PrefetchScalarGridSpec(
            num_scalar_prefetch=1, grid=(layers, 2, per),
            in_specs=[full] + [half] * (2 * layers) + [full, full], out_specs=(full,) * 4),
        out_shape=(shp,) * 4,
    )(core, w, *mine, *theirs, m, v)


def _pair_sum(g, col, ra, core, after, *, name):
    _, rh, c = ra.shape
    rb = _row_block(rh)

    def body(core_ref, g_ref, ra_ref, after_ref, h_ref, h16_ref):
        tot = g_ref[...] + ra_ref[...]
        h_ref[...] = tot
        h16_ref[...] = tot.astype(BF16)

    if col:
        g_spec = pl.BlockSpec((None, rb, c), lambda k, i, core_ref: (core_ref[0], i, k))
    else:
        g_spec = pl.BlockSpec((None, None, rb, c), lambda k, i, core_ref: (k, core_ref[0], i, 0))
    slot = pl.BlockSpec((None, rb, c), lambda k, i, core_ref: (k, i, 0))
    return pl.pallas_call(
        body, name=name,
        grid_spec=pltpu.PrefetchScalarGridSpec(
            num_scalar_prefetch=1, grid=(N_CHIPS, rh // rb), in_specs=[g_spec, slot, ANY], out_specs=(slot, slot)),
        out_shape=(jax.ShapeDtypeStruct((N_CHIPS, rh, c), F32), jax.ShapeDtypeStruct((N_CHIPS, rh, c), BF16)),
    )(core, g, ra, after)


def _first_sum(h, r1, keep, after, *, name):
    _, rh, c = h.shape
    rb = _row_block(rh)

    def body(keep_ref, h_ref, r_ref, after_ref, s_ref, s16_ref):
        tot = h_ref[...] + r_ref[...].astype(F32)
        s_ref[...] = tot
        s16_ref[...] = tot.astype(BF16)

    slot = pl.BlockSpec((None, rb, c), lambda t, i, keep_ref: (t, i, 0))
    return pl.pallas_call(
        body, name=name,
        grid_spec=pltpu.PrefetchScalarGridSpec(
            num_scalar_prefetch=1, grid=(2, rh // rb),
            in_specs=[pl.BlockSpec((None, rb, c), lambda t, i, keep_ref: (keep_ref[t], i, 0)), slot, ANY],
            out_specs=(slot, slot)),
        out_shape=(jax.ShapeDtypeStruct((2, rh, c), F32), jax.ShapeDtypeStruct((2, rh, c), BF16)),
    )(keep, h, r1, after)


def _second_sum(s1, r2, mine, after, *, name):
    _, rh, c = s1.shape
    rb = _row_block(rh)

    def body(mine_ref, s_ref, r_ref, after_ref, t_ref):
        t_ref[...] = s_ref[...] + r_ref[...].astype(F32)

    flat = pl.BlockSpec((rb, c), lambda i, mine_ref: (i, 0))
    return pl.pallas_call(
        body, name=name,
        grid_spec=pltpu.PrefetchScalarGridSpec(
            num_scalar_prefetch=1, grid=(rh // rb,),
            in_specs=[pl.BlockSpec((None, rb, c), lambda i, mine_ref: (mine_ref[0], i, 0)), flat, ANY],
            out_specs=flat),
        out_shape=jax.ShapeDtypeStruct((rh, c), F32),
    )(mine, s1, r2, after)


def _place(shard, col, chip, dtype, *, name):
    r, c = shard.shape
    rh = r // 2
    rb = _row_block(rh)

    def body(chip_ref, s_ref, o_ref):
        o_ref[...] = s_ref[...].astype(o_ref.dtype)

    if col:
        out_spec = pl.BlockSpec((None, rb, c), lambda h, i, chip_ref: (h, i, chip_ref[0]))
        shape = (2, rh, N_CHIPS * c)
    else:
        out_spec = pl.BlockSpec((None, None, rb, c), lambda h, i, chip_ref: (chip_ref[0], h, i, 0))
        shape = (N_CHIPS, 2, rh, c)
    per = rh // rb
    return pl.pallas_call(
        body, name=name,
        grid_spec=pltpu.PrefetchScalarGridSpec(
            num_scalar_prefetch=1, grid=(2, per),
            in_specs=[pl.BlockSpec((rb, c), lambda h, i, chip_ref: (h * per + i, 0))], out_specs=out_spec),
        out_shape=jax.ShapeDtypeStruct(shape, dtype),
    )(chip, shard)


ANY = pl.BlockSpec(memory_space=pl.ANY)


def _mesh_pos():
    return lax.axis_index("x"), lax.axis_index("y"), lax.axis_index("c")


def _other_chips(x, y):
    return [(1 - x, y), (x, 1 - y), (1 - x, 1 - y)]


def _remote(src, dst, ssem, rsem, dev):
    return pltpu.make_async_remote_copy(src_ref=src, dst_ref=dst, send_sem=ssem, recv_sem=rsem,
                                        device_id=dev, device_id_type=MESH)


def _flip(a, b):
    return a + b - 2 * a * b


def _handshake(peers):
    barrier = pltpu.get_barrier_semaphore()
    for peer in peers:
        pl.semaphore_signal(barrier, inc=1, device_id=peer, device_id_type=MESH)
    pl.semaphore_wait(barrier, len(peers))


def _slab(ref, col, width, k, h):
    if not col:
        return ref.at[k, h]
    start = k * width if isinstance(k, int) else pl.multiple_of(k * width, LANES)
    return ref.at[h, :, pl.ds(start, width)]


def _all_gather(bufs, cols, *, collective_id, name):
    n = len(bufs)
    widths = [b.shape[2] // N_CHIPS if col else b.shape[3] for b, col in zip(bufs, cols)]
    outs = [jax.new_ref(b, memory_space=pltpu.MemorySpace.HBM) for b in bufs]

    def body(ssem, rsem):
        x, y, c = _mesh_pos()
        me = 2 * x + y
        sib = (x, y, 1 - c)
        n1 = (_flip(x, 1 - c), _flip(y, c))
        n2 = (_flip(x, c), _flip(y, 1 - c))
        k1 = 2 * n1[0] + n1[1]
        k2 = 2 * n2[0] + n2[1]
        kd = 2 * (1 - x) + (1 - y)
        _handshake([n1 + (c,), n2 + (c,), sib])

        def slab(a, k, h):
            return _slab(outs[a], cols[a], widths[a], k, h)

        def copy(a, j, src, dst, dev):
            return _remote(src, dst, ssem.at[a, j], rsem.at[a, j], dev)

        sends = []
        for a in range(n):
            for j, nb in ((0, n1), (1, n2)):
                own = slab(a, me, c)
                cp = copy(a, j, own, own, nb + (c,))
                cp.start()
                sends.append(cp)
        arrivals = ((0, k1, n1, 3), (1, k2, n2, 4), (2, kd, n2, 5))
        for j, k, nb, fwd in arrivals:
            for a in range(n):
                got = slab(a, k, c)
                copy(a, j, got, got, nb + (c,)).wait_recv()
                if j == 0:
                    cp = copy(a, 2, got, got, n2 + (c,))
                    cp.start()
                    sends.append(cp)
                cp = copy(a, fwd, got, got, sib)
                cp.start()
                sends.append(cp)
        for fwd, k in ((3, k2), (4, k1), (5, kd)):
            for a in range(n):
                got = slab(a, k, 1 - c)
                copy(a, fwd, got, got, sib).wait_recv()
        for cp in sends:
            cp.wait_send()

    _sequencer_call(body, (), [(n, 6), (n, 6)], collective_id, name)()
    return [ref[...] for ref in outs]


def _sequencer_call(body, out_types, sem_shapes, collective_id, name):
    return pl.kernel(
        body, name=name, out_type=out_types,
        mesh=plsc.ScalarSubcoreMesh(axis_name="sequencer", num_cores=1),
        scratch_types=[pltpu.SemaphoreType.DMA(shape) for shape in sem_shapes],
        compiler_params=pltpu.CompilerParams(collective_id=collective_id))


def _send_other_half(grads, cols, *, collective_id, name):
    n = len(grads)

    def shard_shape(g, col):
        if col:
            return (g.shape[1], g.shape[2] // N_CHIPS)
        return g.shape[2:]

    shapes = [shard_shape(g, col) for g, col in zip(grads, cols)]

    def body(*refs):
        ins, outs = refs[:n], refs[n:2 * n]
        ssem, rsem = refs[2 * n:]
        x, y, c = _mesh_pos()
        sib = (x, y, 1 - c)
        _handshake([sib])
        sends = []
        for a in range(n):
            for k in range(N_CHIPS):
                src = _slab(ins[a], cols[a], shapes[a][1], k, 1 - c)
                cp = _remote(src, outs[a].at[k], ssem.at[a, k], rsem.at[a, k], sib)
                cp.start()
                sends.append(cp)
        for cp in sends:
            cp.wait()

    out_types = [jax.ShapeDtypeStruct((N_CHIPS,) + shp, g.dtype) for g, shp in zip(grads, shapes)]
    return _sequencer_call(body, out_types, [(n, N_CHIPS), (n, N_CHIPS)], collective_id, name)(*grads)


def _send_first(sums, *, collective_id, name):
    n = len(sums)

    def body(*refs):
        ins, outs = refs[:n], refs[n:2 * n]
        ssem, rsem = refs[2 * n:]
        x, y, c = _mesh_pos()
        nb = (_flip(x, c), _flip(y, 1 - c), c)
        _handshake([nb])
        sends = []
        for a in range(n):
            for t in range(2):
                k = 2 * (c * (1 - x) + (1 - c) * t) + (c * t + (1 - c) * (1 - y))
                cp = _remote(ins[a].at[k], outs[a].at[t], ssem.at[a, t], rsem.at[a, t], nb)
                cp.start()
                sends.append(cp)
        for cp in sends:
            cp.wait()

    out_types = [jax.ShapeDtypeStruct((2,) + h.shape[1:], h.dtype) for h in sums]
    return _sequencer_call(body, out_types, [(n, 2), (n, 2)], collective_id, name)(*sums)


def _send_second(sums, *, collective_id, name):
    n = len(sums)

    def body(*refs):
        ins, outs = refs[:n], refs[n:2 * n]
        ssem, rsem = refs[2 * n:]
        x, y, c = _mesh_pos()
        nb = (_flip(x, 1 - c), _flip(y, c), c)
        other = 1 - (c * y + (1 - c) * x)
        _handshake([nb])
        sends = []
        for a in range(n):
            cp = _remote(ins[a].at[other], outs[a], ssem.at[a], rsem.at[a], nb)
            cp.start()
            sends.append(cp)
        for cp in sends:
            cp.wait()

    out_types = [jax.ShapeDtypeStruct(s.shape[1:], s.dtype) for s in sums]
    return _sequencer_call(body, out_types, [(n,), (n,)], collective_id, name)(*sums)


def _swap_halves(halves, *, collective_id, name):
    n = len(halves)

    def body(*refs):
        ins, outs = refs[:n], refs[n:2 * n]
        ssem, rsem = refs[2 * n:]
        x, y, c = _mesh_pos()
        sib = (x, y, 1 - c)
        _handshake([sib])
        cps = []
        for a in range(n):
            cp = _remote(ins[a], outs[a], ssem.at[a], rsem.at[a], sib)
            cp.start()
            cps.append(cp)
        for cp in cps:
            cp.wait()

    out_types = [jax.ShapeDtypeStruct(h.shape, h.dtype) for h in halves]
    return _sequencer_call(body, out_types, [(n,), (n,)], collective_id, name)(*halves)


def _all_reduce_small(buf, *, name):
    r = buf.shape[0]
    rh = r // 2

    def body(in_ref, out_ref, x1_ref, x2_ref, ssem, rsem):
        x, y, c = _mesh_pos()
        me = 2 * x + y
        sib = (x, y, 1 - c)
        chips = _other_chips(x, y)
        cp = _remote(in_ref, x1_ref, ssem.at[0], rsem.at[0], sib)
        cp.start()
        cp.wait()
        off = pl.multiple_of(c * rh, SUBLANES)
        x2_ref[me] = in_ref[pl.ds(off, rh), :] + x1_ref[pl.ds(off, rh), :]
        sends = []
        for j, (cx, cy) in enumerate(chips):
            s = _remote(x2_ref.at[me], x2_ref.at[me], ssem.at[1 + j], rsem.at[1 + j], (cx, cy, c))
            s.start()
            sends.append(s)
        for j, (cx, cy) in enumerate(chips):
            slot = x2_ref.at[2 * cx + cy]
            _remote(slot, slot, ssem.at[1 + j], rsem.at[1 + j], (cx, cy, c)).wait_recv()
        out_ref[pl.ds(off, rh), :] = ((x2_ref[0] + x2_ref[1]) + x2_ref[2]) + x2_ref[3]
        for s in sends:
            s.wait_send()
        mine = out_ref.at[pl.ds(off, rh), :]
        s3 = _remote(mine, mine, ssem.at[4], rsem.at[4], sib)
        s3.start()
        off2 = pl.multiple_of((1 - c) * rh, SUBLANES)
        theirs = out_ref.at[pl.ds(off2, rh), :]
        _remote(theirs, theirs, ssem.at[4], rsem.at[4], sib).wait_recv()
        s3.wait_send()

    vm = pl.BlockSpec(memory_space=pltpu.VMEM)
    return pl.pallas_call(
        body, name=name, in_specs=[vm], out_specs=vm,
        out_shape=jax.ShapeDtypeStruct((r, LANES), F32),
        scratch_shapes=[pltpu.VMEM((r, LANES), F32), pltpu.VMEM((N_CHIPS, rh, LANES), F32),
                        pltpu.SemaphoreType.DMA((5,)), pltpu.SemaphoreType.DMA((5,))],
    )(buf)


PACK_ALIGN = 2 * SUBLANES * LANES


def _pack(arrays, rows_multiple=2 * SUBLANES):
    parts, offs, off = [], [], 0
    for a in arrays:
        flat = a.reshape(-1).astype(F32)
        padded = -(-flat.shape[0] // PACK_ALIGN) * PACK_ALIGN
        parts.append(jnp.pad(flat, (0, padded - flat.shape[0])))
        offs.append(off)
        off += padded
    buf = jnp.concatenate(parts).reshape(-1, LANES)
    return buf, offs


def _unpack(buf, offs, shapes):
    flat = buf.reshape(-1)
    out = []
    for off, shp in zip(offs, shapes):
        size = 1
        for d in shp:
            size *= d
        out.append(flat[off:off + size].reshape(shp))
    return out


def _cols_from_shards(g4):
    _, k, ns = g4.shape
    return jnp.transpose(g4, (1, 0, 2)).reshape(k, N_CHIPS * ns)


def _cols_to_shards(w):
    k, n = w.shape
    return jnp.transpose(w.reshape(k, N_CHIPS, n // N_CHIPS), (1, 0, 2))


def _block_cols(w, parts, blocks):
    lead = w.shape[:-1]
    width = w.shape[-1] // (parts * blocks)
    w = w.reshape(lead + (parts, blocks, width))
    w = jnp.swapaxes(w, -3, -2)
    return w.reshape(lead + (parts * blocks * width,))


def _unblock_cols(w, parts, blocks):
    lead = w.shape[:-1]
    width = w.shape[-1] // (parts * blocks)
    w = w.reshape(lead + (blocks, parts, width))
    w = jnp.swapaxes(w, -3, -2)
    return w.reshape(lead + (parts * blocks * width,))


def _pair_blockdiag(w8):
    w = w8.reshape(4, 2, 64, 64)
    z = jnp.zeros((4, 64, 64), w8.dtype)
    top = jnp.concatenate([w[:, 0], z], axis=2)
    bot = jnp.concatenate([z, w[:, 1]], axis=2)
    return jnp.concatenate([top, bot], axis=1)


def _pair_diag_blocks(w4):
    a = w4[:, :64, :64]
    b = w4[:, 64:, 64:]
    return jnp.stack([a, b], axis=1).reshape(8, 64, 64)


def _local_step(x, target, wts, on_event=None):
    s = x.shape[0]
    g = {}

    def event(name, token):
        if on_event is not None:
            on_event(name, g, token)

    win0 = wts["w_in0"]
    wout0 = wts["w_out0"]
    win1 = wts["w_in1"]
    wout1 = wts["w_out1"]
    wup = wts["w_up"]
    wdown = wts["w_down"]
    w4, b4, w3, b3 = wts["w4"], wts["b4"], wts["w3"], wts["b3"]
    wa, wx = wts["wa"], wts["wx"]
    wat, wxt = jnp.swapaxes(wa, 1, 2), jnp.swapaxes(wx, 1, 2)
    ba, bx, lam = wts["ba"], wts["bx"], wts["lam"]
    fcw, fcb = wts["ffn_cw"], wts["ffn_cb"]
    sgu_w, sgu_wt = wts["sgu_w"], wts["sgu_wt"]
    sgu_bias, sgu_gn = wts["sgu_bias"], wts["sgu_gn"]
    bf = wts["bf"]

    lane = jnp.arange(LANES)
    seg = jnp.where((lane[:, None] // 64) == (lane[None, :] // 64), 1.0 / 64.0, 0.0).astype(BF16)
    sel = jnp.stack([jnp.broadcast_to((lane[:, None] < 64), (LANES, LANES)),
                     jnp.broadcast_to((lane[:, None] >= 64), (LANES, LANES))]).astype(BF16)
    tril = (lane[:, None] >= lane[None, :]).astype(F32)

    n0 = _norm_fwd(x, wts["g_mix0"], name="norm_mix0")
    p0 = _mm([n0], win0, nb=1280, name="mm_in0")
    ya, yb, hl = _even_core_fwd(p0, w4, b4, wa, ba, wx, bx, lam, w3, b3, name="even_fwd")
    h1 = _mm([ya, yb], wout0, res=x, ts=1024, name="mm_out0")

    def ffn_fwd(h, layer):
        n = _norm_fwd(h, wts["g_ffn"][layer], name=f"norm_ffn{layer}")
        up = _mm([n], wup[layer], out_dtype=BF16, ts=1024, nb=1408, name=f"mm_up{layer}")
        act = _ffn_core_fwd(up, fcw[layer], fcb[layer], name=f"ffn_fwd{layer}")
        hn = _mm([act], wdown[layer], res=h, name=f"mm_down{layer}")
        return n, up, act, hn

    n1, up0, act0, h2 = ffn_fwd(h1, 0)

    n2 = _norm_fwd(h2, wts["g_mix1"], name="norm_mix1")
    p1 = _mm([n2], win1, name="mm_in1")
    yc = _sgu_fwd(p1, sgu_gn, sgu_w, sgu_bias, seg, name="sgu_fwd")
    cum = _fcum_fwd(p1, bf, name="fcum_fwd")
    c8 = cum[:, :8]
    cq = jnp.broadcast_to(c8[:, :, None], (s, 8, LANES)).reshape(s, 8 * LANES)
    ck = jnp.transpose(c8).reshape(8, 1, s)
    yd, lb = _fox_fwd(p1, cq, ck, name="fox_fwd")
    h3 = _mm([yc, yd], wout1, res=h2, ts=1024, name="mm_out1")

    n3, up1, act1, h4 = ffn_fwd(h3, 1)
    dh4, loss, g["final_norm"] = _final(h4, wts["g_final"], target, name="final")

    def ffn_bwd(dh, h, n, up, act, layer):
        dact = _mm([dh], wdown[layer], trans_w=True, out_dtype=BF16, ts=1024, nb=1408, name=f"mm_dact{layer}")
        g[f"w_down{layer}"] = _mm_tn([act], [dh], ts=1024, nb=512, name=f"mm_dwdown{layer}")
        event(f"dwdown{layer}", g[f"w_down{layer}"])
        dgate, dval, dcwg, dcwv, dcbg, dcbv = _ffn_core_bwd(dact, up, fcw[layer], fcb[layer], name=f"ffn_bwd{layer}")
        event(f"ffn_bwd{layer}", dgate)
        g[f"w_up{layer}"] = _mm_tn([n], [dgate, dval], ts=1024, nb=1408, name=f"mm_dwup{layer}")
        event(f"dwup{layer}", g[f"w_up{layer}"])
        dhn, g[f"g_ffn{layer}"] = _mm([dgate, dval], wup[layer], trans_w=True, ts=256,
                                      norm_bwd=(h, wts["g_ffn"][layer], dh), name=f"mm_dn_ffn{layer}")
        g[f"ffn_cw{layer}"] = jnp.concatenate([dcwg, dcwv], axis=1)
        g[f"ffn_cb{layer}"] = jnp.concatenate([dcbg, dcbv], axis=1)
        return dhn

    dh3 = ffn_bwd(dh4, h3, n3, up1, act1, 1)

    dy1 = _mm([dh3], wout1, trans_w=True, ts=1024, name="mm_dy1")
    g["w_out1"] = _mm_tn([yc, yd], [dh3], ts=1024, nb=512, name="mm_dwout1")
    event("dwout1", g["w_out1"])
    dzu, dzg, g["sgu_w"], g["sgu_bias"], g["sgu_gn"] = _sgu_bwd(
        p1, dy1, sgu_gn, sgu_w, sgu_wt, sgu_bias, seg, tril, name="sgu_bwd")
    delta = _fox_delta(dy1, yd, sel, name="fox_delta")
    dq, dk, dv, dck, dcq = _fox_bwd(p1, dy1, lb, delta, ck, name="fox_bwd")
    event("fox_bwd", dq)
    dcs = jnp.pad(jnp.transpose(dck.reshape(8, s)), ((0, 0), (0, LANES - 8)))
    df, g["bf"] = _fcum_bwd(dcs, dcq, p1, bf, name="fcum_bwd")
    dp1 = jnp.concatenate([dzu, dzg, dq, dk, dv, df], axis=1)
    g["w_in1"] = _mm_tn([n2], [dp1], ts=1024, nb=896, name="mm_dwin1")
    event("dwin1", g["w_in1"])
    dh2, g["g_mix1"] = _mm([dp1], win1, trans_w=True, norm_bwd=(h2, wts["g_mix1"], dh3), name="mm_dn_mix1")

    dh1 = ffn_bwd(dh2, h1, n1, up0, act0, 0)

    dy0 = _mm([dh1], wout0, trans_w=True, ts=1024, name="mm_dy0")
    g["w_out0"] = _mm_tn([ya, yb], [dh1], ts=1024, nb=512, name="mm_dwout0")
    event("dwout0", g["w_out0"])
    (dp0, g["w4"], g["b4"], g["wa"], g["ba"], g["wx"], g["bx"], g["lam"], g["w3"], g["b3"]) = _even_core_bwd(
        dy0, p0, hl, w4, b4, wa, wat, ba, wx, wxt, bx, lam, w3, b3, name="even_bwd")
    event("even_bwd", dp0)
    g["w_in0"] = _mm_tn([n0], [dp0], ts=1024, nb=640, name="mm_dwin0")
    event("dwin0", g["w_in0"])
    grad_x, g["g_mix0"] = _mm([dp0], win0, trans_w=True, norm_bwd=(x, wts["g_mix0"], dh1), name="mm_dn_mix0")
    return loss, grad_x, g


def _prepare_weights(nat):
    lane = jnp.arange(LANES)
    tril = (lane[:, None] >= lane[None, :]).astype(F32)
    sgu_tril = nat["sgu_w"][0] * tril
    w_in1 = nat["mix1_w_in"]
    nblk = D_FF // FFN_CB
    return {
        "w_in0": _block_cols(nat["mix0_w_in"], 5, 4),
        "w_out0": nat["mix0_w_out"],
        "w_in1": jnp.pad(w_in1, ((0, 0), (0, 21 * LANES - w_in1.shape[1]))),
        "w_out1": nat["mix1_w_out"],
        "w_up": [nat["ffn_up"][l] for l in range(2)],
        "w_down": [nat["ffn_down"][l] for l in range(2)],
        "w4": nat["lru_conv_w"], "b4": nat["lru_conv_b"], "w3": nat["sconv_w"], "b3": nat["sconv_b"],
        "wa": _pair_blockdiag(nat["lru_wa"][0]).astype(BF16), "wx": _pair_blockdiag(nat["lru_wx"][0]).astype(BF16),
        "ba": nat["lru_ba"], "bx": nat["lru_bx"], "lam": nat["lru_lambda"],
        "ffn_cw": [nat["ffn_conv_w"][l] for l in range(2)],
        "ffn_cb": [nat["ffn_conv_b"][l:l + 1] for l in range(2)],
        "sgu_w": sgu_tril.astype(BF16), "sgu_wt": jnp.swapaxes(sgu_tril, 1, 2).astype(BF16),
        "sgu_bias": jnp.repeat(jnp.transpose(nat["sgu_b"][0]), 64, axis=1), "sgu_gn": nat["sgu_norm"],
        "bf": jnp.pad(nat["fox_bf"], ((0, 0), (0, LANES - 8))),
        "g_mix0": nat["mix0_norm"], "g_mix1": nat["mix1_norm"],
        "g_ffn": [nat["ffn_norm"][0:1], nat["ffn_norm"][1:2]], "g_final": nat["final_norm"].reshape(1, D_MODEL),
    }


def _natural_grads(g):
    nblk = D_FF // FFN_CB
    small = {
        "mix0_norm": g["g_mix0"], "lru_conv_b": g["b4"],
        "lru_wa": _pair_diag_blocks(g["wa"])[None], "lru_ba": g["ba"],
        "lru_wx": _pair_diag_blocks(g["wx"])[None], "lru_bx": g["bx"],
        "lru_lambda": g["lam"], "sconv_b": g["b3"],
        "sgu_w": g["sgu_w"][None],
        "sgu_b": jnp.transpose(g["sgu_bias"].reshape(CHUNK, 8, 64).sum(axis=2))[None],
        "fox_bf": g["bf"][:, :8],
        "ffn_norm": jnp.concatenate([g["g_ffn0"], g["g_ffn1"]], axis=0),
        "ffn_conv_b": jnp.concatenate([g["ffn_cb0"], g["ffn_cb1"]], axis=0),
        "final_norm": g["final_norm"].reshape(D_MODEL),
        "lru_conv_w": g["w4"][None], "sconv_w": g["w3"][None],
        "ffn_conv_w": jnp.stack([g["ffn_cw0"], g["ffn_cw1"]]),
        "mix1_norm": g["g_mix1"], "sgu_norm": g["sgu_gn"],
    }
    big = {
        "mix0_w_in": _unblock_cols(g["w_in0"], 5, 4), "mix0_w_out": g["w_out0"],
        "mix1_w_in": g["w_in1"][:, :2568], "mix1_w_out": g["w_out1"],
        "ffn_up0": g["w_up0"], "ffn_up1": g["w_up1"],
        "ffn_down0": g["w_down0"], "ffn_down1": g["w_down1"],
    }
    return small, big


COL_SHARDED = ("mix0_w_in", "mix1_w_in", "ffn_up0", "ffn_up1")
COL_ALIGNED = ("mix0_w_in", "ffn_up0", "ffn_up1")
SMALL_SHARDED = ("lru_conv_w", "sconv_w", "ffn_conv_w", "mix1_norm", "sgu_norm")
SMALL_REPLICATED = ("mix0_norm", "lru_conv_b", "lru_wa", "lru_ba", "lru_wx", "lru_bx", "lru_lambda", "sconv_b",
                    "sgu_w", "sgu_b", "fox_bf", "ffn_norm", "ffn_conv_b", "final_norm")
BIG = ("mix0_w_in", "mix0_w_out", "mix1_w_in", "mix1_w_out", "ffn_up0", "ffn_up1", "ffn_down0", "ffn_down1")
WEIGHT_ORDER = ("mix0_norm", "mix0_w_in", "lru_conv_w", "lru_conv_b", "lru_wa", "lru_ba", "lru_wx", "lru_bx",
                "lru_lambda", "sconv_w", "sconv_b", "mix0_w_out", "mix1_norm", "mix1_w_in", "sgu_norm", "sgu_w",
                "sgu_b", "fox_bf", "mix1_w_out", "ffn_norm", "ffn_up", "ffn_conv_w", "ffn_conv_b", "ffn_down",
                "final_norm")


GATHER_GROUPS = (("mix0_w_in", "mix0_w_out"), ("ffn_up0",), ("ffn_down0", "mix1_w_in"),
                 ("mix1_w_out", "ffn_up1", "ffn_down1"))
CID_GATHER, CID_PAIR, CID_FIRST, CID_SECOND, CID_SWAP = 1, 2, 3, 4, 5


class _GradReducer:
    def __init__(self):
        x, y, c = _mesh_pos()
        self.core = c.reshape(1).astype(jnp.int32)
        self.keep = jnp.stack([c * (2 * x + t) + (1 - c) * (2 * t + y) for t in range(2)]).astype(jnp.int32)
        self.mine = (c * y + (1 - c) * x).reshape(1).astype(jnp.int32)
        self.groups = {}

    @staticmethod
    def _view(name, a):
        if name in COL_ALIGNED:
            return a.reshape(2, a.shape[0] // 2, a.shape[1])
        if name in COL_SHARDED:
            a = _cols_to_shards(a)
            return a.reshape(N_CHIPS, 2, a.shape[1] // 2, a.shape[2])
        rows = a.shape[0] // (2 * N_CHIPS)
        return a.reshape(N_CHIPS, 2, rows, a.shape[1])

    def start(self, group, grads):
        names = tuple(grads)
        views = [self._view(k, grads[k]) for k in names]
        cols = [k in COL_ALIGNED for k in names]
        data = _send_other_half(views, cols, collective_id=CID_PAIR, name=f"rs_pair_{group}")
        self.groups[group] = dict(names=names, stage=0, views=views, cols=cols, data=data)

    def step(self, group, after):
        st = self.groups[group]
        names = st["names"]
        if st["stage"] == 0:
            sums = [_pair_sum(a, col, b, self.core, after, name=f"rs_pair_sum_{k}")
                    for k, a, col, b in zip(names, st["views"], st["cols"], st["data"])]
            st["keep"] = [s32 for s32, _ in sums]
            st["data"] = _send_first([s16 for _, s16 in sums], collective_id=CID_FIRST, name=f"rs_first_{group}")
        elif st["stage"] == 1:
            sums = [_first_sum(s32, r, self.keep, after, name=f"rs_first_sum_{k}")
                    for k, s32, r in zip(names, st["keep"], st["data"])]
            st["keep"] = [s32 for s32, _ in sums]
            st["data"] = _send_second([s16 for _, s16 in sums], collective_id=CID_SECOND, name=f"rs_second_{group}")
        else:
            st["mine"] = [_second_sum(s32, r, self.mine, after, name=f"rs_second_sum_{k}")
                          for k, s32, r in zip(names, st["keep"], st["data"])]
            st["data"] = _swap_halves(st["mine"], collective_id=CID_SWAP, name=f"rs_swap_{group}")
        st["stage"] += 1

    def result(self, group):
        st = self.groups[group]
        return {k: (a, b) for k, a, b in zip(st["names"], st["mine"], st["data"])}


def _train_step(x, target, w, m, v):
    x2 = x[0]
    t2 = target[0]
    chip = 2 * lax.axis_index("x") + lax.axis_index("y")
    core_arr = lax.axis_index("c").reshape(1).astype(jnp.int32)
    chip_arr = chip.reshape(1).astype(jnp.int32)

    big_shards = {
        "mix0_w_in": w["mix0_w_in"][0], "mix0_w_out": w["mix0_w_out"][0],
        "mix1_w_in": w["mix1_w_in"][0], "mix1_w_out": w["mix1_w_out"][0],
        "ffn_up0": w["ffn_up"][0], "ffn_up1": w["ffn_up"][1],
        "ffn_down0": w["ffn_down"][0], "ffn_down1": w["ffn_down"][1],
    }
    small_shards = [w[k] for k in SMALL_SHARDED]
    small_buf, small_offs = _pack(small_shards)
    full = {}
    small_all = None
    for gi, names in enumerate(GATHER_GROUPS):
        cols = [k in COL_ALIGNED for k in names]
        placed = [_place(big_shards[k], col, chip_arr, BF16, name=f"place_{k}") for k, col in zip(names, cols)]
        if gi == 0:
            placed.append(_place(small_buf, False, chip_arr, F32, name="place_small"))
            cols = cols + [False]
        gathered = _all_gather(placed, cols, collective_id=CID_GATHER, name=f"gather_weights{gi}")
        if gi == 0:
            small_all = gathered[-1].reshape(N_CHIPS, -1, LANES)
        for k, arr in zip(names, gathered):
            if k in COL_ALIGNED:
                full[k] = arr.reshape(arr.shape[0] * arr.shape[1], arr.shape[2])
            elif k in COL_SHARDED:
                full[k] = _cols_from_shards(arr.reshape((N_CHIPS, arr.shape[1] * arr.shape[2], arr.shape[3])))
            else:
                full[k] = arr.reshape(-1, arr.shape[3])
    per_chip = [_unpack(small_all[k], small_offs, [a.shape for a in small_shards]) for k in range(N_CHIPS)]
    lru_conv_w = jnp.concatenate([per_chip[k][0] for k in range(N_CHIPS)], axis=-1)[0]
    sconv_w = jnp.concatenate([per_chip[k][1] for k in range(N_CHIPS)], axis=-1)[0]
    ffn_conv_w = jnp.concatenate([per_chip[k][2] for k in range(N_CHIPS)], axis=-1)
    mix1_norm = jnp.concatenate([per_chip[k][3] for k in range(N_CHIPS)], axis=-1)
    sgu_norm = jnp.concatenate([per_chip[k][4] for k in range(N_CHIPS)], axis=-1)

    nat = {
        "mix0_w_in": full["mix0_w_in"], "mix0_w_out": full["mix0_w_out"],
        "mix1_w_in": full["mix1_w_in"], "mix1_w_out": full["mix1_w_out"],
        "ffn_up": [full["ffn_up0"], full["ffn_up1"]], "ffn_down": [full["ffn_down0"], full["ffn_down1"]],
        "lru_conv_w": lru_conv_w, "sconv_w": sconv_w, "ffn_conv_w": ffn_conv_w, "mix1_norm": mix1_norm,
        "sgu_norm": sgu_norm,
    }
    for k in SMALL_REPLICATED:
        nat[k] = w[k]
    wts = _prepare_weights(nat)

    reducer = _GradReducer()

    def on_event(name, g, token):
        if name == "dwup1":
            reducer.start("ffn1", {"ffn_up1": g["w_up1"], "ffn_down1": g["w_down1"]})
        elif name in ("dwout1", "fox_bwd"):
            reducer.step("ffn1", token)
        elif name == "dwin1":
            reducer.step("ffn1", token)
            reducer.start("mix1", {"mix1_w_in": g["w_in1"][:, :2568], "mix1_w_out": g["w_out1"]})
        elif name in ("dwdown0", "ffn_bwd0"):
            reducer.step("mix1", token)
        elif name == "dwup0":
            reducer.step("mix1", token)
            reducer.start("ffn0", {"ffn_up0": g["w_up0"], "ffn_down0": g["w_down0"]})
        elif name in ("dwout0", "even_bwd"):
            reducer.step("ffn0", token)
        elif name == "dwin0":
            reducer.step("ffn0", token)
            reducer.start("mix0", {"mix0_w_in": _unblock_cols(g["w_in0"], 5, 4), "mix0_w_out": g["w_out0"]})

    loss, grad_x, g = _local_step(x2, t2, wts, on_event)
    grads_small, _ = _natural_grads(g)

    small_names = SMALL_REPLICATED + SMALL_SHARDED
    small_list = [grads_small[k] for k in small_names] + [loss[:, :1]]
    sbuf, soffs = _pack(small_list)
    sred = _all_reduce_small(sbuf, name="reduce_small")
    small_red = _unpack(sred, soffs, [a.shape for a in small_list])
    loss_total = small_red[-1][0, 0]
    gsum = dict(zip(small_names, small_red[:-1]))
    for k in SMALL_SHARDED:
        width = w[k].shape[-1]
        gsum[k] = lax.dynamic_slice_in_dim(gsum[k], chip * width, width, axis=gsum[k].ndim - 1)

    out_g, out_d, out_m, out_v = {}, {}, {}, {}
    reduced = {}
    for group in ("ffn1", "mix1", "ffn0"):
        reduced.update(reducer.result(group))

    def update(pname, keys):
        mine = [reduced[k][0] for k in keys]
        theirs = [reduced[k][1] for k in keys]
        out_g[pname], out_d[pname], out_m[pname], out_v[pname] = _adamw_halves(
            w[pname], mine, theirs, m[pname], v[pname], core_arr, name=f"adamw_{pname}")
        return out_d[pname]

    reducer.step("mix0", update("ffn_up", ("ffn_up0", "ffn_up1")))
    small_w = [w[k] for k in small_names]
    pg, offs = _pack([gsum[k] for k in small_names])
    pw, _ = _pack(small_w)
    pm, _ = _pack([m[k] for k in small_names])
    pv, _ = _pack([v[k] for k in small_names])
    sd, sm, sv = _adamw(pw, pg, pm, pv, name="adamw_small")
    reducer.step("mix0", update("ffn_down", ("ffn_down0", "ffn_down1")))
    update("mix1_w_in", ("mix1_w_in",))
    reducer.step("mix0", update("mix1_w_out", ("mix1_w_out",)))
    reduced.update(reducer.result("mix0"))
    update("mix0_w_in", ("mix0_w_in",))
    update("mix0_w_out", ("mix0_w_out",))

    shapes = [a.shape for a in small_w]
    for k, dd, mm, vv in zip(small_names, _unpack(sd, offs, shapes), _unpack(sm, offs, shapes),
                             _unpack(sv, offs, shapes)):
        out_g[k], out_d[k], out_m[k], out_v[k] = gsum[k].reshape(w[k].shape), dd, mm, vv

    outs = [loss_total, grad_x[None]]
    for d in (out_g, out_d, out_m, out_v):
        outs.extend(d[k] for k in WEIGHT_ORDER)
    return tuple(outs)


def kernel(x, mix0_norm, mix0_w_in, lru_conv_w, lru_conv_b, lru_wa, lru_ba, lru_wx, lru_bx, lru_lambda, sconv_w, sconv_b, mix0_w_out, mix1_norm, mix1_w_in, sgu_norm, sgu_w, sgu_b, fox_bf, mix1_w_out, ffn_norm, ffn_up, ffn_conv_w, ffn_conv_b, ffn_down, final_norm, loss_target, m_mix0_norm, m_mix0_w_in, m_lru_conv_w, m_lru_conv_b, m_lru_wa, m_lru_ba, m_lru_wx, m_lru_bx, m_lru_lambda, m_sconv_w, m_sconv_b, m_mix0_w_out, m_mix1_norm, m_mix1_w_in, m_sgu_norm, m_sgu_w, m_sgu_b, m_fox_bf, m_mix1_w_out, m_ffn_norm, m_ffn_up, m_ffn_conv_w, m_ffn_conv_b, m_ffn_down, m_final_norm, v_mix0_norm, v_mix0_w_in, v_lru_conv_w, v_lru_conv_b, v_lru_wa, v_lru_ba, v_lru_wx, v_lru_bx, v_lru_lambda, v_sconv_w, v_sconv_b, v_mix0_w_out, v_mix1_norm, v_mix1_w_in, v_sgu_norm, v_sgu_w, v_sgu_b, v_fox_bf, v_mix1_w_out, v_ffn_norm, v_ffn_up, v_ffn_conv_w, v_ffn_conv_b, v_ffn_down, v_final_norm):
    w = dict(zip(WEIGHT_ORDER, (mix0_norm, mix0_w_in, lru_conv_w, lru_conv_b, lru_wa, lru_ba, lru_wx, lru_bx, lru_lambda, sconv_w, sconv_b, mix0_w_out, mix1_norm, mix1_w_in, sgu_norm, sgu_w, sgu_b, fox_bf, mix1_w_out, ffn_norm, ffn_up, ffn_conv_w, ffn_conv_b, ffn_down, final_norm)))
    m = dict(zip(WEIGHT_ORDER, (m_mix0_norm, m_mix0_w_in, m_lru_conv_w, m_lru_conv_b, m_lru_wa, m_lru_ba, m_lru_wx, m_lru_bx, m_lru_lambda, m_sconv_w, m_sconv_b, m_mix0_w_out, m_mix1_norm, m_mix1_w_in, m_sgu_norm, m_sgu_w, m_sgu_b, m_fox_bf, m_mix1_w_out, m_ffn_norm, m_ffn_up, m_ffn_conv_w, m_ffn_conv_b, m_ffn_down, m_final_norm)))
    v = dict(zip(WEIGHT_ORDER, (v_mix0_norm, v_mix0_w_in, v_lru_conv_w, v_lru_conv_b, v_lru_wa, v_lru_ba, v_lru_wx, v_lru_bx, v_lru_lambda, v_sconv_w, v_sconv_b, v_mix0_w_out, v_mix1_norm, v_mix1_w_in, v_sgu_norm, v_sgu_w, v_sgu_b, v_fox_bf, v_mix1_w_out, v_ffn_norm, v_ffn_up, v_ffn_conv_w, v_ffn_conv_b, v_ffn_down, v_final_norm)))
    return _train_step(x, loss_target, w, m, v)
```

```python
import functools

import jax
import jax.numpy as jnp
from jax import lax
from jax.experimental import pallas as pl
from jax.experimental.pallas import tpu as pltpu
from jax.experimental.pallas import tpu_sc as plsc

F32 = jnp.float32
BF16 = jnp.bfloat16
MESH = pl.DeviceIdType.MESH

D_MODEL = 1024
LANES = 128
SUBLANES = 8
N_CHIPS = 4
EPS = 1e-6
LRU_C = 8.0
D_FF = 2816
FFN_CB = 256
CHUNK = 128
NEG = -1e30

ADAM_LR = 0.001
ADAM_B1 = 0.9
ADAM_B2 = 0.999
ADAM_EPS = 1e-08
ADAM_WD = 0.01
ADAM_STEP = 10
ADAM_C1 = 1.0 - ADAM_B1 ** ADAM_STEP
ADAM_C2 = 1.0 - ADAM_B2 ** ADAM_STEP

_GELU_C = 0.7978845608028654
_GELU_A = 0.044715


def _sigmoid(x):
    return 1.0 / (1.0 + jnp.exp(-x))


def _sigmoid_tanh(x):
    return 0.5 * jnp.tanh(0.5 * x) + 0.5


def _log1p_pos(e):
    w = 1.0 + e
    return jnp.where(w == 1.0, e, jnp.log(w) * (e / (w - 1.0)))


def _softplus(x):
    return jnp.maximum(x, 0.0) + _log1p_pos(jnp.exp(-jnp.abs(x)))


def _gelu(x):
    t = jnp.tanh(_GELU_C * (x + _GELU_A * (x * x * x)))
    return 0.5 * x * (1.0 + t), t


def _gelu_grad(x, t):
    return 0.5 * (1.0 + t) + 0.5 * x * (1.0 - t * t) * (_GELU_C * (1.0 + 3.0 * _GELU_A * x * x))


def _rows(shape):
    return lax.broadcasted_iota(jnp.int32, shape, 0)


def _lanes(shape):
    return lax.broadcasted_iota(jnp.int32, shape, 1)


def _shift_down(x, halo8, j):
    if j == 0:
        return x
    r = pltpu.roll(x, j, 0)
    hr = pltpu.roll(halo8, j, 0)
    top = jnp.where(_rows(hr.shape) < j, hr, r[:SUBLANES])
    return jnp.concatenate([top, r[SUBLANES:]], axis=0)


def _shift_up(x, next8, j):
    if j == 0:
        return x
    n = x.shape[0]
    r = pltpu.roll(x, n - j, 0)
    nr = pltpu.roll(next8, SUBLANES - j, 0)
    bot = jnp.where(_rows(nr.shape) >= SUBLANES - j, nr, r[n - SUBLANES:])
    return jnp.concatenate([r[:n - SUBLANES], bot], axis=0)


def _scan_fwd(a, u):
    n = a.shape[0]
    row = _rows(a.shape)
    h = u
    k = 1
    while k < n:
        keep = row >= k
        h_sh = jnp.where(keep, pltpu.roll(h, k, 0), 0.0)
        a_sh = jnp.where(keep, pltpu.roll(a, k, 0), 1.0)
        h = a * h_sh + h
        a = a * a_sh
        k *= 2
    return h, a


def _scan_rev(b, d):
    n = b.shape[0]
    row = _rows(b.shape)
    g = d
    k = 1
    while k < n:
        keep = row < n - k
        g_sh = jnp.where(keep, pltpu.roll(g, n - k, 0), 0.0)
        b_sh = jnp.where(keep, pltpu.roll(b, n - k, 0), 1.0)
        g = b * g_sh + g
        b = b * b_sh
        k *= 2
    return g, b


def _cumsum_fwd(x):
    n = x.shape[0]
    row = _rows(x.shape)
    k = 1
    while k < n:
        x = x + jnp.where(row >= k, pltpu.roll(x, k, 0), 0.0)
        k *= 2
    return x


def _cumsum_rev(x):
    n = x.shape[0]
    row = _rows(x.shape)
    k = 1
    while k < n:
        x = x + jnp.where(row < n - k, pltpu.roll(x, n - k, 0), 0.0)
        k *= 2
    return x


def _dot(a, b):
    return lax.dot_general(a, b, (((1,), (0,)), ((), ())), preferred_element_type=F32)


def _dot_nt(a, b):
    return lax.dot_general(a, b, (((1,), (1,)), ((), ())), preferred_element_type=F32)


def _dot_tn(a, b):
    return lax.dot_general(a, b, (((0,), (0,)), ((), ())), preferred_element_type=F32)


def _dot_split(x, m_bf16):
    hi = x.astype(BF16)
    lo = (x - hi.astype(F32)).astype(BF16)
    return _dot(hi, m_bf16) + _dot(lo, m_bf16)


def _tile_rows(ts, s):
    return min(ts, s)


def _mm(a_list, w, *, trans_w=False, res=None, norm_bwd=None, out_dtype=F32, ts=512, nb=None, name):
    s = a_list[0].shape[0]
    ks = [a.shape[1] for a in a_list]
    k = sum(ks)
    n = w.shape[0] if trans_w else w.shape[1]
    ts = _tile_rows(ts, s)
    nb = n if nb is None else nb
    na = len(a_list)
    has_res = res is not None
    fused = norm_bwd is not None
    offs = [sum(ks[:p]) for p in range(na)]

    def body(*refs):
        a_refs = refs[:na]
        w_ref = refs[na]
        acc = None
        for a_ref, off, kk in zip(a_refs, offs, ks):
            a = a_ref[...].astype(BF16)
            if trans_w:
                part = _dot_nt(a, w_ref[:, off:off + kk])
            else:
                part = _dot(a, w_ref[off:off + kk, :])
            acc = part if acc is None else acc + part
        if has_res:
            acc = acc + refs[na + 1][...]
        if not fused:
            refs[-1][...] = acc.astype(out_dtype)
            return
        h_ref, g_ref, dres_ref, dh_ref, dg_ref = refs[na + 1:]
        i = pl.program_id(1)
        x = h_ref[...]
        r = lax.rsqrt(jnp.mean(x * x, axis=-1, keepdims=True) + EPS)
        xhat = x * r
        part = jnp.sum(acc * xhat, axis=0, keepdims=True)

        @pl.when(i == 0)
        def _():
            dg_ref[...] = part

        @pl.when(i > 0)
        def _():
            dg_ref[...] += part

        dxh = acc * g_ref[...]
        dh_ref[...] = dres_ref[...] + r * (dxh - xhat * jnp.mean(dxh * xhat, axis=-1, keepdims=True))

    in_specs = [pl.BlockSpec((ts, kk), lambda j, i: (i, 0)) for kk in ks]
    if trans_w:
        in_specs.append(pl.BlockSpec((nb, k), lambda j, i: (j, 0)))
    else:
        in_specs.append(pl.BlockSpec((k, nb), lambda j, i: (0, j)))
    args = list(a_list) + [w]
    tile = pl.BlockSpec((ts, nb), lambda j, i: (i, j))
    if has_res:
        in_specs.append(tile)
        args.append(res)
    if fused:
        assert nb == n and not has_res
        vec = pl.BlockSpec((1, n), lambda j, i: (0, 0))
        h, g, dres = norm_bwd
        return pl.pallas_call(
            body, name=name, grid=(1, s // ts), in_specs=in_specs + [tile, vec, tile],
            out_specs=(tile, vec),
            out_shape=(jax.ShapeDtypeStruct((s, n), F32), jax.ShapeDtypeStruct((1, n), F32)),
        )(*args, h, g, dres)
    return pl.pallas_call(
        body, name=name, grid=(n // nb, s // ts), in_specs=in_specs, out_specs=tile,
        out_shape=jax.ShapeDtypeStruct((s, n), out_dtype),
    )(*args)


def _mm_tn(a_list, b_list, *, ts=512, nb=None, name):
    s = b_list[0].shape[0]
    ks = [a.shape[1] for a in a_list]
    k = sum(ks)
    width = b_list[0].shape[1]
    n = width * len(b_list)
    ts = _tile_rows(ts, s)
    nb = width if nb is None else nb
    per = width // nb
    na = len(a_list)
    nparts = len(b_list)

    def body(*refs):
        a_refs = refs[:na]
        b_refs = refs[na:na + nparts]
        o_ref = refs[-1]
        j = pl.program_id(0)
        i = pl.program_id(1)
        parts = [r[...].astype(BF16) for r in a_refs]
        a = parts[0] if na == 1 else jnp.concatenate(parts, axis=1)

        def accumulate(b_ref):
            upd = _dot_tn(a, b_ref[...].astype(BF16))

            @pl.when(i == 0)
            def _():
                o_ref[...] = upd

            @pl.when(i > 0)
            def _():
                o_ref[...] += upd

        if nparts == 1:
            accumulate(b_refs[0])
        else:
            for part, b_ref in enumerate(b_refs):
                pl.when(j // per == part)(functools.partial(accumulate, b_ref))

    in_specs = [pl.BlockSpec((ts, kk), lambda j, i: (i, 0)) for kk in ks]
    for part in range(nparts):
        in_specs.append(pl.BlockSpec(
            (ts, nb), lambda j, i, part=part: (i, jnp.clip(j - part * per, 0, per - 1))))
    return pl.pallas_call(
        body, name=name, grid=(n // nb, s // ts), in_specs=in_specs,
        out_specs=pl.BlockSpec((k, nb), lambda j, i: (0, j)),
        out_shape=jax.ShapeDtypeStruct((k, n), F32),
    )(*a_list, *b_list)


def _norm_fwd(h, g, *, ts=512, name):
    s, d = h.shape
    ts = _tile_rows(ts, s)

    def body(h_ref, g_ref, n_ref):
        x = h_ref[...]
        r = lax.rsqrt(jnp.mean(x * x, axis=-1, keepdims=True) + EPS)
        n_ref[...] = ((x * r) * g_ref[...]).astype(BF16)

    return pl.pallas_call(
        body, name=name, grid=(s // ts,),
        in_specs=[pl.BlockSpec((ts, d), lambda i: (i, 0)), pl.BlockSpec((1, d), lambda i: (0, 0))],
        out_specs=pl.BlockSpec((ts, d), lambda i: (i, 0)),
        out_shape=jax.ShapeDtypeStruct((s, d), BF16),
    )(h, g)


def _norm_bwd(dn, h, g, dres, *, ts=512, name):
    s, d = h.shape
    ts = _tile_rows(ts, s)

    def body(dn_ref, h_ref, g_ref, dres_ref, dh_ref, dg_ref):
        i = pl.program_id(0)
        x = h_ref[...]
        dnv = dn_ref[...]
        r = lax.rsqrt(jnp.mean(x * x, axis=-1, keepdims=True) + EPS)
        xhat = x * r
        part = jnp.sum(dnv * xhat, axis=0, keepdims=True)

        @pl.when(i == 0)
        def _():
            dg_ref[...] = part

        @pl.when(i > 0)
        def _():
            dg_ref[...] += part

        dxh = dnv * g_ref[...]
        dh_ref[...] = dres_ref[...] + r * (dxh - xhat * jnp.mean(dxh * xhat, axis=-1, keepdims=True))

    tile = pl.BlockSpec((ts, d), lambda i: (i, 0))
    vec = pl.BlockSpec((1, d), lambda i: (0, 0))
    return pl.pallas_call(
        body, name=name, grid=(s // ts,), in_specs=[tile, tile, vec, tile],
        out_specs=(tile, vec),
        out_shape=(jax.ShapeDtypeStruct((s, d), F32), jax.ShapeDtypeStruct((1, d), F32)),
    )(dn, h, g, dres)


def _final(h, g, target, *, ts=512, name):
    s, d = h.shape
    ts = _tile_rows(ts, s)
    nt = s // ts

    def body(h_ref, g_ref, t_ref, dh_ref, loss_ref, dg_ref, acc_ref):
        i = pl.program_id(0)
        x = h_ref[...]
        r = lax.rsqrt(jnp.mean(x * x, axis=-1, keepdims=True) + EPS)
        xhat = x * r
        gv = g_ref[...]
        err = xhat * gv - t_ref[...]
        sq = jnp.sum(err * err, axis=0, keepdims=True)
        dy = err * (1.0 / d)
        part = jnp.sum(dy * xhat, axis=0, keepdims=True)

        @pl.when(i == 0)
        def _():
            acc_ref[...] = sq
            dg_ref[...] = part

        @pl.when(i > 0)
        def _():
            acc_ref[...] += sq
            dg_ref[...] += part

        dxh = dy * gv
        dh_ref[...] = r * (dxh - xhat * jnp.mean(dxh * xhat, axis=-1, keepdims=True))

        @pl.when(i == nt - 1)
        def _():
            tot = jnp.sum(acc_ref[...], axis=1, keepdims=True) * (0.5 / d)
            loss_ref[...] = jnp.broadcast_to(tot, (1, LANES))

    tile = pl.BlockSpec((ts, d), lambda i: (i, 0))
    vec = pl.BlockSpec((1, d), lambda i: (0, 0))
    return pl.pallas_call(
        body, name=name, grid=(nt,), in_specs=[tile, vec, tile],
        out_specs=(tile, pl.BlockSpec((1, LANES), lambda i: (0, 0)), vec),
        out_shape=(jax.ShapeDtypeStruct((s, d), F32), jax.ShapeDtypeStruct((1, LANES), F32),
                   jax.ShapeDtypeStruct((1, d), F32)),
        scratch_shapes=[pltpu.VMEM((1, d), F32)],
    )(h, g, target)


def _halo_map(ts, width_blocks):
    per = ts // SUBLANES

    def index(j, i):
        return (jnp.maximum(i * per - 1, 0), width_blocks(j))

    return index


def _even_gates(xc, wa, ba, wx, bx, sp):
    xb = xc.astype(BF16)
    r = _sigmoid(_dot(xb, wa) + ba)
    ig = _sigmoid(_dot(xb, wx) + bx)
    la = (-LRU_C) * r * sp
    a = jnp.exp(la)
    a2 = a * a
    m = jnp.sqrt(-jnp.tanh(la) * (1.0 + a2))
    return r, ig, la, a, a2, m


def _even_core_fwd(p, w4, b4, wa, ba, wx, bx, lam, w3, b3, *, ts=512, name):
    s = p.shape[0]
    ts = _tile_rows(ts, s)
    nt = s // ts
    nblk = 4

    def body(p_ref, ph_ref, w4_ref, b4_ref, wa_ref, ba_ref, wx_ref, bx_ref, lam_ref, w3_ref, b3_ref,
             ya_ref, yb_ref, hl_ref, hcar_ref):
        i = pl.program_id(1)
        first = (i > 0).astype(F32)
        xa = p_ref[:, 0:LANES]
        ga = p_ref[:, LANES:2 * LANES]
        cp = p_ref[:, 2 * LANES:3 * LANES]
        bp = p_ref[:, 3 * LANES:4 * LANES]
        vb = p_ref[:, 4 * LANES:5 * LANES]
        xa_h = ph_ref[:, 0:LANES] * first
        s_h = ph_ref[:, 2 * LANES:3 * LANES] * ph_ref[:, 4 * LANES:5 * LANES] * first

        xc = b4_ref[...] + w4_ref[3:4, :] * xa
        for k in range(3):
            xc = xc + w4_ref[k:k + 1, :] * _shift_down(xa, xa_h, 3 - k)
        sp = _softplus(-lam_ref[...])
        _, ig, _, a, _, m = _even_gates(xc, wa_ref[0], ba_ref[...], wx_ref[0], bx_ref[...], sp)
        u = m * (ig * xc)
        hs, acum = _scan_fwd(a, u)

        @pl.when(i == 0)
        def _():
            hcar_ref[...] = jnp.zeros_like(hcar_ref)

        hs = hs + acum * hcar_ref[0:1, :]
        hl_ref[...] = hs
        hcar_ref[0:1, :] = hl_ref[ts - 1:ts, :]
        ge, _ = _gelu(ga)
        ya_ref[...] = (hs * ge).astype(BF16)

        sv = cp * vb
        sc = b3_ref[...] + w3_ref[2:3, :] * sv
        for k in range(2):
            sc = sc + w3_ref[k:k + 1, :] * _shift_down(sv, s_h, 2 - k)
        yb_ref[...] = (bp * sc).astype(BF16)

    blk = pl.BlockSpec((ts, 5 * LANES), lambda j, i: (i, j))
    halo = pl.BlockSpec((SUBLANES, 5 * LANES), _halo_map(ts, lambda j: j))
    vec = pl.BlockSpec((1, LANES), lambda j, i: (0, j))
    out = pl.BlockSpec((ts, LANES), lambda j, i: (i, j))
    return pl.pallas_call(
        body, name=name, grid=(nblk, nt),
        in_specs=[blk, halo,
                  pl.BlockSpec((4, LANES), lambda j, i: (0, j)), vec,
                  pl.BlockSpec((1, LANES, LANES), lambda j, i: (j, 0, 0)), vec,
                  pl.BlockSpec((1, LANES, LANES), lambda j, i: (j, 0, 0)), vec, vec,
                  pl.BlockSpec((3, LANES), lambda j, i: (0, j)), vec],
        out_specs=(out, out, out),
        out_shape=(jax.ShapeDtypeStruct((s, 4 * LANES), BF16), jax.ShapeDtypeStruct((s, 4 * LANES), BF16),
                   jax.ShapeDtypeStruct((s, 4 * LANES), F32)),
        scratch_shapes=[pltpu.VMEM((SUBLANES, LANES), F32)],
    )(p, p, w4, b4, wa, ba, wx, bx, lam, w3, b3)


def _even_core_bwd(dy, p, hl, w4, b4, wa, wat, ba, wx, wxt, bx, lam, w3, b3, *, ts=256, name):
    s = p.shape[0]
    ts = _tile_rows(ts, s)
    nt = s // ts
    nblk = 4
    per = ts // SUBLANES

    def body(dya_ref, dyb_ref, p_ref, ph_ref, hl_ref, hh_ref,
             w4_ref, b4_ref, wa_ref, wat_ref, ba_ref, wx_ref, wxt_ref, bx_ref, lam_ref, w3_ref, b3_ref,
             dp_ref, dw4_ref, db4_ref, dwa_ref, dba_ref, dwx_ref, dbx_ref, dlam_ref, dw3_ref, db3_ref,
             dxc_nx, dsc_nx, cg_ref):
        i = pl.program_id(1)
        ti = nt - 1 - i
        first = (ti > 0).astype(F32)
        xa = p_ref[:, 0:LANES]
        ga = p_ref[:, LANES:2 * LANES]
        cp = p_ref[:, 2 * LANES:3 * LANES]
        bp = p_ref[:, 3 * LANES:4 * LANES]
        vb = p_ref[:, 4 * LANES:5 * LANES]
        xa_h = ph_ref[:, 0:LANES] * first
        s_h = ph_ref[:, 2 * LANES:3 * LANES] * ph_ref[:, 4 * LANES:5 * LANES] * first
        h_h = hh_ref[...] * first

        @pl.when(i == 0)
        def _():
            dxc_nx[...] = jnp.zeros_like(dxc_nx)
            dsc_nx[...] = jnp.zeros_like(dsc_nx)
            cg_ref[...] = jnp.zeros_like(cg_ref)
            for ref in (dw4_ref, db4_ref, dwa_ref, dba_ref, dwx_ref, dbx_ref, dlam_ref, dw3_ref, db3_ref):
                ref[...] = jnp.zeros_like(ref)

        xa_sh = [_shift_down(xa, xa_h, 3 - k) for k in range(3)] + [xa]
        xc = b4_ref[...]
        for k in range(4):
            xc = xc + w4_ref[k:k + 1, :] * xa_sh[k]
        lamv = lam_ref[...]
        sp = _softplus(-lamv)
        r, ig, _, a, a2, m = _even_gates(xc, wa_ref[0], ba_ref[...], wx_ref[0], bx_ref[...], sp)
        sv = cp * vb
        sv_sh = [_shift_down(sv, s_h, 2 - k) for k in range(2)] + [sv]
        sc = b3_ref[...]
        for k in range(3):
            sc = sc + w3_ref[k:k + 1, :] * sv_sh[k]
        hs = hl_ref[...]
        h_prev = _shift_down(hs, h_h, 1)

        dya = dya_ref[...]
        dyb = dyb_ref[...]
        ge, gt = _gelu(ga)
        dga = dya * hs * _gelu_grad(ga, gt)
        dh = dya * ge

        ones8 = jnp.ones((SUBLANES, LANES), F32)
        b = _shift_up(a, ones8, 1)
        g, bcum = _scan_rev(b, dh)
        g = g + bcum * cg_ref[0:1, :]
        ag = a * g
        cg_ref[...] = ag[:SUBLANES]

        da = g * h_prev
        xi = ig * xc
        dm = g * xi
        dig = g * m * xc
        dxc = g * m * ig
        dla = da * a - dm * (a2 / m)
        dr = dla * ((-LRU_C) * sp)
        dlam_ref[...] += jnp.sum(dla * r, axis=0, keepdims=True) * (LRU_C * _sigmoid(-lamv))
        dra = dr * r * (1.0 - r)
        dia = dig * ig * (1.0 - ig)
        drab = dra.astype(BF16)
        diab = dia.astype(BF16)
        xcb = xc.astype(BF16)
        dxc = dxc + _dot(drab, wat_ref[0]) + _dot(diab, wxt_ref[0])
        dwa_ref[0] += _dot_tn(xcb, drab)
        dwx_ref[0] += _dot_tn(xcb, diab)
        dba_ref[...] += jnp.sum(dra, axis=0, keepdims=True)
        dbx_ref[...] += jnp.sum(dia, axis=0, keepdims=True)

        nx = dxc_nx[...]
        dxa = w4_ref[3:4, :] * dxc
        for k in range(3):
            dxa = dxa + w4_ref[k:k + 1, :] * _shift_up(dxc, nx, 3 - k)
        for k in range(4):
            dw4_ref[k:k + 1, :] += jnp.sum(dxc * xa_sh[k], axis=0, keepdims=True)
        db4_ref[...] += jnp.sum(dxc, axis=0, keepdims=True)
        dxc_nx[...] = dxc[:SUBLANES]

        dbp = dyb * sc
        dsc = dyb * bp
        nsc = dsc_nx[...]
        ds = w3_ref[2:3, :] * dsc
        for k in range(2):
            ds = ds + w3_ref[k:k + 1, :] * _shift_up(dsc, nsc, 2 - k)
        for k in range(3):
            dw3_ref[k:k + 1, :] += jnp.sum(dsc * sv_sh[k], axis=0, keepdims=True)
        db3_ref[...] += jnp.sum(dsc, axis=0, keepdims=True)
        dsc_nx[...] = dsc[:SUBLANES]

        dp_ref[:, 0:LANES] = dxa.astype(BF16)
        dp_ref[:, LANES:2 * LANES] = dga.astype(BF16)
        dp_ref[:, 2 * LANES:3 * LANES] = (ds * vb).astype(BF16)
        dp_ref[:, 3 * LANES:4 * LANES] = dbp.astype(BF16)
        dp_ref[:, 4 * LANES:5 * LANES] = (ds * cp).astype(BF16)

    def rev(j, i):
        return (nt - 1 - i, j)

    def rev_halo(col):
        def index(j, i):
            return (jnp.maximum((nt - 1 - i) * per - 1, 0), col(j))
        return index

    blk = pl.BlockSpec((ts, 5 * LANES), rev)
    one = pl.BlockSpec((ts, LANES), rev)
    vec = pl.BlockSpec((1, LANES), lambda j, i: (0, j))
    mat = pl.BlockSpec((1, LANES, LANES), lambda j, i: (j, 0, 0))
    w4s = pl.BlockSpec((4, LANES), lambda j, i: (0, j))
    w3s = pl.BlockSpec((3, LANES), lambda j, i: (0, j))
    f = jax.ShapeDtypeStruct
    return pl.pallas_call(
        body, name=name, grid=(nblk, nt),
        in_specs=[one, pl.BlockSpec((ts, LANES), lambda j, i: (nt - 1 - i, 4 + j)),
                  blk, pl.BlockSpec((SUBLANES, 5 * LANES), rev_halo(lambda j: j)),
                  one, pl.BlockSpec((SUBLANES, LANES), rev_halo(lambda j: j)),
                  w4s, vec, mat, mat, vec, mat, mat, vec, vec, w3s, vec],
        out_specs=(blk, w4s, vec, mat, vec, mat, vec, vec, w3s, vec),
        out_shape=(f((s, 20 * LANES), BF16), f((4, 4 * LANES), F32), f((1, 4 * LANES), F32),
                   f((4, LANES, LANES), F32), f((1, 4 * LANES), F32),
                   f((4, LANES, LANES), F32), f((1, 4 * LANES), F32), f((1, 4 * LANES), F32),
                   f((3, 4 * LANES), F32), f((1, 4 * LANES), F32)),
        scratch_shapes=[pltpu.VMEM((SUBLANES, LANES), F32), pltpu.VMEM((SUBLANES, LANES), F32),
                        pltpu.VMEM((SUBLANES, LANES), F32)],
    )(dy, dy, p, p, hl, hl, w4, b4, wa, wat, ba, wx, wxt, bx, lam, w3, b3)


def _ffn_conv(u_ref, uh_ref, w_ref, b_ref, first):
    u = u_ref[...].astype(F32)
    u_h = uh_ref[...].astype(F32)[SUBLANES:] * first
    u_sh = [_shift_down(u, u_h, 2 - k) for k in range(2)] + [u]
    hc = b_ref[...]
    for k in range(3):
        hc = hc + w_ref[k:k + 1, :] * u_sh[k]
    return hc, u_sh


def _ffn_specs(ts, row, halo_row):
    nblk = D_FF // FFN_CB
    specs = []
    for off in (0, nblk):
        specs.append(pl.BlockSpec((ts, FFN_CB), lambda j, i, off=off: (row(i), off + j)))
        specs.append(pl.BlockSpec((16, FFN_CB), lambda j, i, off=off: (halo_row(i), off + j)))
        specs.append(pl.BlockSpec((3, FFN_CB), lambda j, i, off=off: (0, off + j)))
        specs.append(pl.BlockSpec((1, FFN_CB), lambda j, i, off=off: (0, off + j)))
    return specs


def _ffn_core_fwd(up, w, b, *, ts=512, name):
    s = up.shape[0]
    ts = _tile_rows(ts, s)
    nt = s // ts
    nblk = D_FF // FFN_CB
    per = ts // 16

    def body(g_ref, gh_ref, wg_ref, bg_ref, v_ref, vh_ref, wv_ref, bv_ref, act_ref):
        first = (pl.program_id(1) > 0).astype(F32)
        gate, _ = _ffn_conv(g_ref, gh_ref, wg_ref, bg_ref, first)
        val, _ = _ffn_conv(v_ref, vh_ref, wv_ref, bv_ref, first)
        act_ref[...] = (gate * _sigmoid_tanh(gate) * val).astype(BF16)

    return pl.pallas_call(
        body, name=name, grid=(nblk, nt),
        in_specs=_ffn_specs(ts, lambda i: i, lambda i: jnp.maximum(i * per - 1, 0)),
        out_specs=pl.BlockSpec((ts, FFN_CB), lambda j, i: (i, j)),
        out_shape=jax.ShapeDtypeStruct((s, D_FF), BF16),
    )(up, up, w, b, up, up, w, b)


def _ffn_core_bwd(dact, up, w, b, *, ts=512, name):
    s = up.shape[0]
    ts = _tile_rows(ts, s)
    nt = s // ts
    nblk = D_FF // FFN_CB
    per = ts // 16

    strip = 4 * SUBLANES
    halo = 2 * SUBLANES
    nstrips = ts // strip

    def fold(x):
        out = x[:SUBLANES]
        for r0 in range(SUBLANES, strip, SUBLANES):
            out = out + x[r0:r0 + SUBLANES]
        return out

    def body(da_ref, g_ref, gh_ref, wg_ref, bg_ref, v_ref, vh_ref, wv_ref, bv_ref,
             dg_ref, dv_ref, dwg_ref, dwv_ref, dbg_ref, dbv_ref, nxg_ref, nxv_ref, ug_ref, uv_ref):
        i = pl.program_id(1)
        first = nt - 1 - i > 0

        @pl.when(i == 0)
        def _():
            for ref in (nxg_ref, nxv_ref, dwg_ref, dwv_ref, dbg_ref, dbv_ref):
                ref[...] = jnp.zeros_like(ref)

        for u_ref, uh_ref, dst in ((g_ref, gh_ref, ug_ref), (v_ref, vh_ref, uv_ref)):
            dst[0:halo, :] = jnp.where(first, uh_ref[...], jnp.zeros_like(uh_ref))
            dst[halo:, :] = u_ref[...]
        wg, wv, bg, bv = wg_ref[...], wv_ref[...], bg_ref[...], bv_ref[...]

        def conv(u_ref, r, w, b):
            win = u_ref[pl.ds(r, halo + strip), :].astype(F32)
            cur, before = win[halo:], win[SUBLANES:halo]
            sh = [_shift_down(cur, before, 2 - k) for k in range(2)] + [cur]
            return b + w[0:1] * sh[0] + w[1:2] * sh[1] + w[2:3] * sh[2], sh

        def conv_t(d, nxt, w):
            out = w[2:3] * d
            for k in range(2):
                out = out + w[k:k + 1] * _shift_up(d, nxt, 2 - k)
            return out

        def step(t, carry):
            nxg, nxv, awg, awv, abg, abv = carry
            r = pl.multiple_of((nstrips - 1 - t) * strip, strip)
            gate, g_sh = conv(ug_ref, r, wg, bg)
            val, v_sh = conv(uv_ref, r, wv, bv)
            da = da_ref[pl.ds(r, strip), :].astype(F32)
            sg = _sigmoid_tanh(gate)
            dgate = da * val * (sg * (1.0 + gate * (1.0 - sg)))
            dval = da * (gate * sg)
            dg_ref[pl.ds(r, strip), :] = conv_t(dgate, nxg, wg).astype(BF16)
            dv_ref[pl.ds(r, strip), :] = conv_t(dval, nxv, wv).astype(BF16)
            awg = tuple(a + fold(dgate * sh) for a, sh in zip(awg, g_sh))
            awv = tuple(a + fold(dval * sh) for a, sh in zip(awv, v_sh))
            return dgate[:SUBLANES], dval[:SUBLANES], awg, awv, abg + fold(dgate), abv + fold(dval)

        zero = jnp.zeros((SUBLANES, FFN_CB), F32)
        init = (nxg_ref[...], nxv_ref[...], (zero,) * 3, (zero,) * 3, zero, zero)
        nxg, nxv, awg, awv, abg, abv = lax.fori_loop(0, nstrips, step, init)
        nxg_ref[...] = nxg
        nxv_ref[...] = nxv
        for k in range(3):
            dwg_ref[k:k + 1, :] += jnp.sum(awg[k], axis=0, keepdims=True)
            dwv_ref[k:k + 1, :] += jnp.sum(awv[k], axis=0, keepdims=True)
        dbg_ref[...] += jnp.sum(abg, axis=0, keepdims=True)
        dbv_ref[...] += jnp.sum(abv, axis=0, keepdims=True)

    def rev(i):
        return nt - 1 - i

    tile = pl.BlockSpec((ts, FFN_CB), lambda j, i: (rev(i), j))
    w_out = pl.BlockSpec((3, FFN_CB), lambda j, i: (0, j))
    b_out = pl.BlockSpec((1, FFN_CB), lambda j, i: (0, j))
    f = jax.ShapeDtypeStruct
    return pl.pallas_call(
        body, name=name, grid=(nblk, nt),
        in_specs=[tile] + _ffn_specs(ts, rev, lambda i: jnp.maximum(rev(i) * per - 1, 0)),
        out_specs=(tile, tile, w_out, w_out, b_out, b_out),
        out_shape=(f((s, D_FF), BF16), f((s, D_FF), BF16), f((3, D_FF), F32), f((3, D_FF), F32),
                   f((1, D_FF), F32), f((1, D_FF), F32)),
        scratch_shapes=[pltpu.VMEM((SUBLANES, FFN_CB), F32), pltpu.VMEM((SUBLANES, FFN_CB), F32),
                        pltpu.VMEM((ts + halo, FFN_CB), BF16), pltpu.VMEM((ts + halo, FFN_CB), BF16)],
    )(dact, up, up, w, b, up, up, w, b)


def _sgu_forward_block(zu, zg, gn, w_ref, bias, seg):
    u, tu = _gelu(zu)
    g, tg = _gelu(zg)
    ms = _dot_split(g * g, seg)
    rs = lax.rsqrt(ms + EPS)
    ghat = g * rs
    gv = ghat * gn
    gvb = gv.astype(BF16)
    lane = _lanes((CHUNK, LANES))
    chunks = []
    for c in range(zu.shape[0] // CHUNK):
        gc = gvb[c * CHUNK:(c + 1) * CHUNK]
        mix = jnp.where(lane < 64, _dot(w_ref[0], gc), _dot(w_ref[1], gc)) + bias
        chunks.append(mix)
    mixed = chunks[0] if len(chunks) == 1 else jnp.concatenate(chunks, axis=0)
    return u, tu, g, tg, rs, ghat, gvb, mixed


def _sgu_fwd(p1, gn, w, bias, seg, *, ts=512, name):
    s = p1.shape[0]
    ts = _tile_rows(ts, s)

    def body(zu_ref, zg_ref, gn_ref, w_ref, bias_ref, seg_ref, yc_ref):
        u, _, _, _, _, _, _, mixed = _sgu_forward_block(
            zu_ref[...], zg_ref[...], gn_ref[...], w_ref, bias_ref[...], seg_ref[...])
        yc_ref[...] = (u * mixed).astype(BF16)

    return pl.pallas_call(
        body, name=name, grid=(4, s // ts),
        in_specs=[pl.BlockSpec((ts, LANES), lambda j, i: (i, j)),
                  pl.BlockSpec((ts, LANES), lambda j, i: (i, 4 + j)),
                  pl.BlockSpec((1, LANES), lambda j, i: (0, j)),
                  pl.BlockSpec((2, CHUNK, CHUNK), lambda j, i: (j, 0, 0)),
                  pl.BlockSpec((CHUNK, LANES), lambda j, i: (0, j)),
                  pl.BlockSpec((LANES, LANES), lambda j, i: (0, 0))],
        out_specs=pl.BlockSpec((ts, LANES), lambda j, i: (i, j)),
        out_shape=jax.ShapeDtypeStruct((s, 4 * LANES), BF16),
    )(p1, p1, gn, w, bias, seg)


def _sgu_bwd(p1, dy, gn, w, wt, bias, seg, tril, *, ts=512, name):
    s = p1.shape[0]
    ts = _tile_rows(ts, s)
    nt = s // ts

    def body(zu_ref, zg_ref, dy_ref, gn_ref, w_ref, wt_ref, bias_ref, seg_ref, tril_ref,
             dzu_ref, dzg_ref, dw_ref, dbias_ref, dgn_ref):
        i = pl.program_id(1)
        zu = zu_ref[...]
        zg = zg_ref[...]
        gn_v = gn_ref[...]
        segv = seg_ref[...]
        u, tu, g, tg, rs, ghat, gvb, mixed = _sgu_forward_block(zu, zg, gn_v, w_ref, bias_ref[...], segv)
        dyv = dy_ref[...]
        du = dyv * mixed
        dmx = dyv * u

        @pl.when(i == 0)
        def _():
            dw_ref[...] = jnp.zeros_like(dw_ref)
            dbias_ref[...] = jnp.zeros_like(dbias_ref)
            dgn_ref[...] = jnp.zeros_like(dgn_ref)

        lane = _lanes((CHUNK, LANES))
        dgv_chunks = []
        dbias = jnp.zeros((CHUNK, LANES), F32)
        for c in range(ts // CHUNK):
            dmc = dmx[c * CHUNK:(c + 1) * CHUNK]
            gc = gvb[c * CHUNK:(c + 1) * CHUNK]
            dm_a = jnp.where(lane < 64, dmc, 0.0).astype(BF16)
            dm_b = jnp.where(lane >= 64, dmc, 0.0).astype(BF16)
            dw_ref[0] += _dot_nt(dm_a, gc)
            dw_ref[1] += _dot_nt(dm_b, gc)
            dgv_chunks.append(_dot(wt_ref[0], dm_a) + _dot(wt_ref[1], dm_b))
            dbias = dbias + dmc
        dbias_ref[...] += dbias
        dgv = dgv_chunks[0] if len(dgv_chunks) == 1 else jnp.concatenate(dgv_chunks, axis=0)
        dgn_ref[...] += jnp.sum(dgv * ghat, axis=0, keepdims=True)
        dgh = dgv * gn_v
        dg = rs * (dgh - ghat * _dot_split(dgh * ghat, segv))
        dzu_ref[...] = (du * _gelu_grad(zu, tu)).astype(BF16)
        dzg_ref[...] = (dg * _gelu_grad(zg, tg)).astype(BF16)

        @pl.when(i == nt - 1)
        def _():
            dw_ref[0] = dw_ref[0] * tril_ref[...]
            dw_ref[1] = dw_ref[1] * tril_ref[...]

    f = jax.ShapeDtypeStruct
    colj = pl.BlockSpec((ts, LANES), lambda j, i: (i, j))
    wsp = pl.BlockSpec((2, CHUNK, CHUNK), lambda j, i: (j, 0, 0))
    sq = pl.BlockSpec((LANES, LANES), lambda j, i: (0, 0))
    return pl.pallas_call(
        body, name=name, grid=(4, nt),
        in_specs=[colj, pl.BlockSpec((ts, LANES), lambda j, i: (i, 4 + j)), colj,
                  pl.BlockSpec((1, LANES), lambda j, i: (0, j)), wsp, wsp,
                  pl.BlockSpec((CHUNK, LANES), lambda j, i: (0, j)), sq, sq],
        out_specs=(colj, colj, wsp, pl.BlockSpec((CHUNK, LANES), lambda j, i: (0, j)),
                   pl.BlockSpec((1, LANES), lambda j, i: (0, j))),
        out_shape=(f((s, 4 * LANES), BF16), f((s, 4 * LANES), BF16), f((8, CHUNK, CHUNK), F32),
                   f((CHUNK, 4 * LANES), F32), f((1, 4 * LANES), F32)),
    )(p1, p1, dy, gn, w, wt, bias, seg, tril)


F_COL = 20


def _fcum_fwd(p1, bf, *, ts=512, name):
    s = p1.shape[0]
    ts = _tile_rows(ts, s)

    def body(f_ref, bf_ref, c_ref, car_ref):
        i = pl.program_id(0)
        z = f_ref[...] + bf_ref[...]
        logf = jnp.minimum(z, 0.0) - _log1p_pos(jnp.exp(-jnp.abs(z)))

        @pl.when(i == 0)
        def _():
            car_ref[...] = jnp.zeros_like(car_ref)

        c_ref[...] = _cumsum_fwd(logf) + car_ref[0:1, :]
        car_ref[0:1, :] = c_ref[ts - 1:ts, :]

    return pl.pallas_call(
        body, name=name, grid=(s // ts,),
        in_specs=[pl.BlockSpec((ts, LANES), lambda i: (i, F_COL)), pl.BlockSpec((1, LANES), lambda i: (0, 0))],
        out_specs=pl.BlockSpec((ts, LANES), lambda i: (i, 0)),
        out_shape=jax.ShapeDtypeStruct((s, LANES), F32),
        scratch_shapes=[pltpu.VMEM((SUBLANES, LANES), F32)],
    )(p1, bf)


def _fcum_bwd(dcs, dcq, p1, bf, *, ts=512, name):
    s = p1.shape[0]
    ts = _tile_rows(ts, s)
    nt = s // ts

    def body(dc_ref, dcq_ref, f_ref, bf_ref, df_ref, dbf_ref, car_ref):
        i = pl.program_id(0)

        @pl.when(i == 0)
        def _():
            car_ref[...] = jnp.zeros_like(car_ref)
            dbf_ref[...] = jnp.zeros_like(dbf_ref)

        dc = dc_ref[...]
        lane = _lanes((ts, LANES))
        for h in range(8):
            dc = dc + jnp.where(lane == h, dcq_ref[:, h * LANES:(h + 1) * LANES], 0.0)
        dlog = _cumsum_rev(dc) + car_ref[0:1, :]
        car_ref[...] = dlog[:SUBLANES]
        z = f_ref[...] + bf_ref[...]
        df = dlog * _sigmoid(-z)
        df_ref[...] = df.astype(BF16)
        dbf_ref[...] += jnp.sum(df, axis=0, keepdims=True)

    return pl.pallas_call(
        body, name=name, grid=(nt,),
        in_specs=[pl.BlockSpec((ts, LANES), lambda i: (nt - 1 - i, 0)),
                  pl.BlockSpec((ts, 8 * LANES), lambda i: (nt - 1 - i, 0)),
                  pl.BlockSpec((ts, LANES), lambda i: (nt - 1 - i, F_COL)),
                  pl.BlockSpec((1, LANES), lambda i: (0, 0))],
        out_specs=(pl.BlockSpec((ts, LANES), lambda i: (nt - 1 - i, 0)), pl.BlockSpec((1, LANES), lambda i: (0, 0))),
        out_shape=(jax.ShapeDtypeStruct((s, LANES), BF16), jax.ShapeDtypeStruct((1, LANES), F32)),
        scratch_shapes=[pltpu.VMEM((SUBLANES, LANES), F32)],
    )(dcs, dcq, p1, bf)


def _fox_scores(qm, kb, bias, ck, diagonal):
    sc = _dot_nt(qm, kb) + bias - ck
    if diagonal:
        sc = jnp.where(_lanes(sc.shape) <= _rows(sc.shape), sc, NEG)
    return sc


def _head_masks(shape):
    lane = _lanes(shape)
    return lane < 64, lane >= 64


def _fox_fwd(p1, cq, ck, *, tq=512, name):
    s = p1.shape[0]
    tq = _tile_rows(tq, s)
    tk = tq
    nq = s // tq

    def body(q_ref, k_ref, v_ref, cq_ref, ck_ref, o_ref, lb_ref):
        qi = pl.program_id(1)
        q = q_ref[...] * 0.125
        first, second = _head_masks((tq, LANES))
        qms = [jnp.where(sel, q, 0.0).astype(BF16) for sel in (first, second)]
        cqs = [cq_ref[:, hh * LANES:(hh + 1) * LANES] for hh in range(2)]
        biases = [jnp.tile(cqh, (1, tk // LANES)) for cqh in cqs]

        def step(kj, carry, diagonal):
            cols = pl.ds(pl.multiple_of(kj * tk, tk), tk)
            kb = k_ref[cols, :].astype(BF16)
            vb = v_ref[cols, :].astype(BF16)
            new, outs = [], []
            acc = carry[4]
            for hh in range(2):
                m_prev, l_prev = carry[2 * hh], carry[2 * hh + 1]
                sc = _fox_scores(qms[hh], kb, biases[hh], ck_ref[hh, :, cols], diagonal)
                m_new = jnp.maximum(m_prev, jnp.max(sc, axis=1, keepdims=True))
                pm = jnp.exp(sc - jnp.tile(m_new, (1, tk // LANES)))
                alpha = jnp.exp(m_prev - m_new)
                new += [m_new, alpha * l_prev + jnp.sum(pm, axis=1, keepdims=True)]
                outs.append(acc * alpha + _dot(pm.astype(BF16), vb))
            return tuple(new) + (jnp.where(first, outs[0], outs[1]),)

        zero = jnp.zeros((tq, LANES), F32)
        low = jnp.full((tq, LANES), NEG, F32)
        carry = lax.fori_loop(0, qi, lambda kj, c: step(kj, c, False), (low, zero, low, zero, zero))
        m0, l0, m1, l1, acc = step(qi, carry, True)
        o_ref[...] = (acc / jnp.where(first, l0, l1)).astype(BF16)
        lb_ref[:, 0:LANES] = cqs[0] - (m0 + jnp.log(l0))
        lb_ref[:, LANES:2 * LANES] = cqs[1] - (m1 + jnp.log(l1))

    return pl.pallas_call(
        body, name=name, grid=(4, nq),
        in_specs=[pl.BlockSpec((tq, LANES), lambda j, qi: (qi, 8 + j)),
                  pl.BlockSpec((s, LANES), lambda j, qi: (0, 12 + j)),
                  pl.BlockSpec((s, LANES), lambda j, qi: (0, 16 + j)),
                  pl.BlockSpec((tq, 2 * LANES), lambda j, qi: (qi, j)),
                  pl.BlockSpec((2, 1, s), lambda j, qi: (j, 0, 0))],
        out_specs=(pl.BlockSpec((tq, LANES), lambda j, qi: (qi, j)),
                   pl.BlockSpec((tq, 2 * LANES), lambda j, qi: (qi, j))),
        out_shape=(jax.ShapeDtypeStruct((s, 4 * LANES), BF16), jax.ShapeDtypeStruct((s, 8 * LANES), F32)),
    )(p1, p1, p1, cq, ck)


def _fox_delta(dy, o, sel, *, ts=512, name):
    s = o.shape[0]
    ts = _tile_rows(ts, s)

    def body(do_ref, o_ref, sel_ref, d_ref):
        prod = do_ref[...] * o_ref[...].astype(F32)
        d_ref[:, 0:LANES] = _dot_split(prod, sel_ref[0])
        d_ref[:, LANES:2 * LANES] = _dot_split(prod, sel_ref[1])

    return pl.pallas_call(
        body, name=name, grid=(4, s // ts),
        in_specs=[pl.BlockSpec((ts, LANES), lambda j, i: (i, 4 + j)),
                  pl.BlockSpec((ts, LANES), lambda j, i: (i, j)),
                  pl.BlockSpec((2, LANES, LANES), lambda j, i: (0, 0, 0))],
        out_specs=pl.BlockSpec((ts, 2 * LANES), lambda j, i: (i, j)),
        out_shape=jax.ShapeDtypeStruct((s, 8 * LANES), F32),
    )(dy, o, sel)


def _fox_bwd(p1, dy, lb, delta, ck, *, tq=512, name):
    s = p1.shape[0]
    tq = _tile_rows(tq, s)
    tk = tq
    nq = s // tq

    def body(q_ref, k_ref, v_ref, do_ref, lb_ref, dl_ref, ck_ref,
             dq_ref, dk_ref, dv_ref, dck_ref, dcq_ref, dqa_ref, dra_ref):
        kj = pl.program_id(1)

        @pl.when(kj == 0)
        def _():
            dqa_ref[...] = jnp.zeros_like(dqa_ref)
            dra_ref[...] = jnp.zeros_like(dra_ref)

        kf = k_ref[...]
        kb = kf.astype(BF16)
        vb = v_ref[...].astype(BF16)
        first, second = _head_masks((tk, LANES))
        kms = [jnp.where(sel, kf, 0.0).astype(BF16) for sel in (first, second)]
        cks = [ck_ref[hh] for hh in range(2)]

        def step(qi, carry, diagonal):
            dk_acc, dv_acc, dc0, dc1 = carry
            dcs = [dc0, dc1]
            rows = pl.ds(pl.multiple_of(qi * tq, tq), tq)
            q = q_ref[rows, :] * 0.125
            do = do_ref[rows, :]
            for hh, sel in enumerate((first, second)):
                qm = jnp.where(sel, q, 0.0).astype(BF16)
                dom = jnp.where(sel, do, 0.0).astype(BF16)
                bias = jnp.tile(lb_ref[rows, hh * LANES:(hh + 1) * LANES], (1, tk // LANES))
                pm = jnp.exp(_fox_scores(qm, kb, bias, cks[hh], diagonal))
                dv_acc = dv_acc + _dot_tn(pm.astype(BF16), dom)
                dp = _dot_nt(dom, vb)
                ds = pm * (dp - jnp.tile(dl_ref[rows, hh * LANES:(hh + 1) * LANES], (1, tk // LANES)))
                dsb = ds.astype(BF16)
                dk_acc = dk_acc + _dot_tn(dsb, qm)
                dcs[hh] = dcs[hh] - jnp.sum(ds, axis=0, keepdims=True)
                dqa_ref[rows, :] += _dot(dsb, kms[hh])
                dra_ref[hh, rows, :] += jnp.sum(ds, axis=1, keepdims=True)
            return dk_acc, dv_acc, dcs[0], dcs[1]

        zero = jnp.zeros((tk, LANES), F32)
        zrow = jnp.zeros((1, tk), F32)
        carry = step(kj, (zero, zero, zrow, zrow), True)
        dk_acc, dv_acc, dc0, dc1 = lax.fori_loop(kj + 1, nq, lambda qi, c: step(qi, c, False), carry)
        dk_ref[...] = dk_acc.astype(BF16)
        dv_ref[...] = dv_acc.astype(BF16)
        dck_ref[0] = dc0
        dck_ref[1] = dc1

        @pl.when(kj == nq - 1)
        def _():
            dq_ref[...] = (dqa_ref[...] * 0.125).astype(BF16)
            dcq_ref[:, 0:LANES] = dra_ref[0]
            dcq_ref[:, LANES:2 * LANES] = dra_ref[1]

    def full(width, col0):
        return pl.BlockSpec((s, width), lambda j, kj: (0, col0 + j))

    kblk = pl.BlockSpec((tk, LANES), lambda j, kj: (kj, j))
    f = jax.ShapeDtypeStruct
    return pl.pallas_call(
        body, name=name, grid=(4, nq),
        in_specs=[full(LANES, 8),
                  pl.BlockSpec((tk, LANES), lambda j, kj: (kj, 12 + j)),
                  pl.BlockSpec((tk, LANES), lambda j, kj: (kj, 16 + j)),
                  full(LANES, 4), full(2 * LANES, 0), full(2 * LANES, 0),
                  pl.BlockSpec((2, 1, tk), lambda j, kj: (j, 0, kj))],
        out_specs=(full(LANES, 0), kblk, kblk, pl.BlockSpec((2, 1, tk), lambda j, kj: (j, 0, kj)),
                   full(2 * LANES, 0)),
        out_shape=(f((s, 4 * LANES), BF16), f((s, 4 * LANES), BF16), f((s, 4 * LANES), BF16),
                   f((8, 1, s), F32), f((s, 8 * LANES), F32)),
        scratch_shapes=[pltpu.VMEM((s, LANES), F32), pltpu.VMEM((2, s, LANES), F32)],
    )(p1, p1, p1, dy, lb, delta, ck)


def _row_block(r, cap=256):
    best = None
    for rb in range(2 * SUBLANES, min(r, cap) + 1, 2 * SUBLANES):
        if r % rb == 0:
            best = rb
    return r if best is None else best


def _adamw(w, g, m, v, *, name):
    r, c = w.shape
    rb = _row_block(r)

    def body(w_ref, g_ref, m_ref, v_ref, d_ref, nm_ref, nv_ref):
        gv = g_ref[...]
        mn = ADAM_B1 * m_ref[...] + (1.0 - ADAM_B1) * gv
        vn = ADAM_B2 * v_ref[...] + (1.0 - ADAM_B2) * (gv * gv)
        m_hat = mn / ADAM_C1
        v_hat = vn / ADAM_C2
        d_ref[...] = (-ADAM_LR) * (m_hat / (jnp.sqrt(v_hat) + ADAM_EPS) + ADAM_WD * w_ref[...])
        nm_ref[...] = mn
        nv_ref[...] = vn

    blk = pl.BlockSpec((rb, c), lambda i: (i, 0))
    shp = jax.ShapeDtypeStruct((r, c), F32)
    return pl.pallas_call(
        body, name=name, grid=(r // rb,), in_specs=[blk] * 4, out_specs=(blk,) * 3, out_shape=(shp,) * 3,
    )(w, g, m, v)


def _adamw_halves(w, mine, theirs, m, v, core, *, name):
    layers, r, c = w.shape
    rh = r // 2
    rb = _row_block(rh)
    per = rh // rb

    def body(core_ref, w_ref, *refs):
        g_refs = refs[:2 * layers]
        m_ref, v_ref, g_ref, d_ref, nm_ref, nv_ref = refs[2 * layers:]
        own = pl.program_id(1) == core_ref[0]
        gv = jnp.where(own, g_refs[0][...], g_refs[layers][...])
        for l in range(1, layers):
            gv = jnp.where(pl.program_id(0) == l, jnp.where(own, g_refs[l][...], g_refs[layers + l][...]), gv)
        g_ref[...] = gv
        mn = ADAM_B1 * m_ref[...] + (1.0 - ADAM_B1) * gv
        vn = ADAM_B2 * v_ref[...] + (1.0 - ADAM_B2) * (gv * gv)
        m_hat = mn / ADAM_C1
        v_hat = vn / ADAM_C2
        d_ref[...] = (-ADAM_LR) * (m_hat / (jnp.sqrt(v_hat) + ADAM_EPS) + ADAM_WD * w_ref[...])
        nm_ref[...] = mn
        nv_ref[...] = vn

    full = pl.BlockSpec((None, rb, c), lambda l, h, i, core_ref: (l, h * per + i, 0))
    half = pl.BlockSpec((rb, c), lambda l, h, i, core_ref: (i, 0))
    shp = jax.ShapeDtypeStruct((layers, r, c), F32)
    return pl.pallas_call(
        body, name=name,
        grid_spec=pltpu.PrefetchScalarGridSpec(
            num_scalar_prefetch=1, grid=(layers, 2, per),
            in_specs=[full] + [half] * (2 * layers) + [full, full], out_specs=(full,) * 4),
        out_shape=(shp,) * 4,
    )(core, w, *mine, *theirs, m, v)


def _pair_sum(g, col, ra, core, after, *, name):
    _, rh, c = ra.shape
    rb = _row_block(rh)

    def body(core_ref, g_ref, ra_ref, after_ref, h_ref, h16_ref):
        tot = g_ref[...] + ra_ref[...]
        h_ref[...] = tot
        h16_ref[...] = tot.astype(BF16)

    if col:
        g_spec = pl.BlockSpec((None, rb, c), lambda k, i, core_ref: (core_ref[0], i, k))
    else:
        g_spec = pl.BlockSpec((None, None, rb, c), lambda k, i, core_ref: (k, core_ref[0], i, 0))
    slot = pl.BlockSpec((None, rb, c), lambda k, i, core_ref: (k, i, 0))
    return pl.pallas_call(
        body, name=name,
        grid_spec=pltpu.PrefetchScalarGridSpec(
            num_scalar_prefetch=1, grid=(N_CHIPS, rh // rb), in_specs=[g_spec, slot, ANY], out_specs=(slot, slot)),
        out_shape=(jax.ShapeDtypeStruct((N_CHIPS, rh, c), F32), jax.ShapeDtypeStruct((N_CHIPS, rh, c), BF16)),
    )(core, g, ra, after)


def _first_sum(h, r1, keep, after, *, name):
    _, rh, c = h.shape
    rb = _row_block(rh)

    def body(keep_ref, h_ref, r_ref, after_ref, s_ref, s16_ref):
        tot = h_ref[...] + r_ref[...].astype(F32)
        s_ref[...] = tot
        s16_ref[...] = tot.astype(BF16)

    slot = pl.BlockSpec((None, rb, c), lambda t, i, keep_ref: (t, i, 0))
    return pl.pallas_call(
        body, name=name,
        grid_spec=pltpu.PrefetchScalarGridSpec(
            num_scalar_prefetch=1, grid=(2, rh // rb),
            in_specs=[pl.BlockSpec((None, rb, c), lambda t, i, keep_ref: (keep_ref[t], i, 0)), slot, ANY],
            out_specs=(slot, slot)),
        out_shape=(jax.ShapeDtypeStruct((2, rh, c), F32), jax.ShapeDtypeStruct((2, rh, c), BF16)),
    )(keep, h, r1, after)


def _second_sum(s1, r2, mine, after, *, name):
    _, rh, c = s1.shape
    rb = _row_block(rh)

    def body(mine_ref, s_ref, r_ref, after_ref, t_ref):
        t_ref[...] = s_ref[...] + r_ref[...].astype(F32)

    flat = pl.BlockSpec((rb, c), lambda i, mine_ref: (i, 0))
    return pl.pallas_call(
        body, name=name,
        grid_spec=pltpu.PrefetchScalarGridSpec(
            num_scalar_prefetch=1, grid=(rh // rb,),
            in_specs=[pl.BlockSpec((None, rb, c), lambda i, mine_ref: (mine_ref[0], i, 0)), flat, ANY],
            out_specs=flat),
        out_shape=jax.ShapeDtypeStruct((rh, c), F32),
    )(mine, s1, r2, after)


def _place(shard, col, chip, dtype, *, name):
    r, c = shard.shape
    rh = r // 2
    rb = _row_block(rh)

    def body(chip_ref, s_ref, o_ref):
        o_ref[...] = s_ref[...].astype(o_ref.dtype)

    if col:
        out_spec = pl.BlockSpec((None, rb, c), lambda h, i, chip_ref: (h, i, chip_ref[0]))
        shape = (2, rh, N_CHIPS * c)
    else:
        out_spec = pl.BlockSpec((None, None, rb, c), lambda h, i, chip_ref: (chip_ref[0], h, i, 0))
        shape = (N_CHIPS, 2, rh, c)
    per = rh // rb
    return pl.pallas_call(
        body, name=name,
        grid_spec=pltpu.PrefetchScalarGridSpec(
            num_scalar_prefetch=1, grid=(2, per),
            in_specs=[pl.BlockSpec((rb, c), lambda h, i, chip_ref: (h * per + i, 0))], out_specs=out_spec),
        out_shape=jax.ShapeDtypeStruct(shape, dtype),
    )(chip, shard)


ANY = pl.BlockSpec(memory_space=pl.ANY)


def _mesh_pos():
    return lax.axis_index("x"), lax.axis_index("y"), lax.axis_index("c")


def _other_chips(x, y):
    return [(1 - x, y), (x, 1 - y), (1 - x, 1 - y)]


def _remote(src, dst, ssem, rsem, dev):
    return pltpu.make_async_remote_copy(src_ref=src, dst_ref=dst, send_sem=ssem, recv_sem=rsem,
                                        device_id=dev, device_id_type=MESH)


def _flip(a, b):
    return a + b - 2 * a * b


def _handshake(peers):
    barrier = pltpu.get_barrier_semaphore()
    for peer in peers:
        pl.semaphore_signal(barrier, inc=1, device_id=peer, device_id_type=MESH)
    pl.semaphore_wait(barrier, len(peers))


def _slab(ref, col, width, k, h):
    if not col:
        return ref.at[k, h]
    start = k * width if isinstance(k, int) else pl.multiple_of(k * width, LANES)
    return ref.at[h, :, pl.ds(start, width)]


def _all_gather(bufs, cols, *, collective_id, name):
    n = len(bufs)
    widths = [b.shape[2] // N_CHIPS if col else b.shape[3] for b, col in zip(bufs, cols)]
    outs = [jax.new_ref(b, memory_space=pltpu.MemorySpace.HBM) for b in bufs]

    def body(ssem, rsem):
        x, y, c = _mesh_pos()
        me = 2 * x + y
        sib = (x, y, 1 - c)
        n1 = (_flip(x, 1 - c), _flip(y, c))
        n2 = (_flip(x, c), _flip(y, 1 - c))
        k1 = 2 * n1[0] + n1[1]
        k2 = 2 * n2[0] + n2[1]
        kd = 2 * (1 - x) + (1 - y)
        _handshake([n1 + (c,), n2 + (c,), sib])

        def slab(a, k, h):
            return _slab(outs[a], cols[a], widths[a], k, h)

        def copy(a, j, src, dst, dev):
            return _remote(src, dst, ssem.at[a, j], rsem.at[a, j], dev)

        sends = []
        for a in range(n):
            for j, nb in ((0, n1), (1, n2)):
                own = slab(a, me, c)
                cp = copy(a, j, own, own, nb + (c,))
                cp.start()
                sends.append(cp)
        arrivals = ((0, k1, n1, 3), (1, k2, n2, 4), (2, kd, n2, 5))
        for j, k, nb, fwd in arrivals:
            for a in range(n):
                got = slab(a, k, c)
                copy(a, j, got, got, nb + (c,)).wait_recv()
                if j == 0:
                    cp = copy(a, 2, got, got, n2 + (c,))
                    cp.start()
                    sends.append(cp)
                cp = copy(a, fwd, got, got, sib)
                cp.start()
                sends.append(cp)
        for fwd, k in ((3, k2), (4, k1), (5, kd)):
            for a in range(n):
                got = slab(a, k, 1 - c)
                copy(a, fwd, got, got, sib).wait_recv()
        for cp in sends:
            cp.wait_send()

    _sequencer_call(body, (), [(n, 6), (n, 6)], collective_id, name)()
    return [ref[...] for ref in outs]


def _sequencer_call(body, out_types, sem_shapes, collective_id, name):
    return pl.kernel(
        body, name=name, out_type=out_types,
        mesh=plsc.ScalarSubcoreMesh(axis_name="sequencer", num_cores=1),
        scratch_types=[pltpu.SemaphoreType.DMA(shape) for shape in sem_shapes],
        compiler_params=pltpu.CompilerParams(collective_id=collective_id))


def _send_other_half(grads, cols, *, collective_id, name):
    n = len(grads)

    def shard_shape(g, col):
        if col:
            return (g.shape[1], g.shape[2] // N_CHIPS)
        return g.shape[2:]

    shapes = [shard_shape(g, col) for g, col in zip(grads, cols)]

    def body(*refs):
        ins, outs = refs[:n], refs[n:2 * n]
        ssem, rsem = refs[2 * n:]
        x, y, c = _mesh_pos()
        sib = (x, y, 1 - c)
        _handshake([sib])
        sends = []
        for a in range(n):
            for k in range(N_CHIPS):
                src = _slab(ins[a], cols[a], shapes[a][1], k, 1 - c)
                cp = _remote(src, outs[a].at[k], ssem.at[a, k], rsem.at[a, k], sib)
                cp.start()
                sends.append(cp)
        for cp in sends:
            cp.wait()

    out_types = [jax.ShapeDtypeStruct((N_CHIPS,) + shp, g.dtype) for g, shp in zip(grads, shapes)]
    return _sequencer_call(body, out_types, [(n, N_CHIPS), (n, N_CHIPS)], collective_id, name)(*grads)


def _send_first(sums, *, collective_id, name):
    n = len(sums)

    def body(*refs):
        ins, outs = refs[:n], refs[n:2 * n]
        ssem, rsem = refs[2 * n:]
        x, y, c = _mesh_pos()
        nb = (_flip(x, c), _flip(y, 1 - c), c)
        _handshake([nb])
        sends = []
        for a in range(n):
            for t in range(2):
                k = 2 * (c * (1 - x) + (1 - c) * t) + (c * t + (1 - c) * (1 - y))
                cp = _remote(ins[a].at[k], outs[a].at[t], ssem.at[a, t], rsem.at[a, t], nb)
                cp.start()
                sends.append(cp)
        for cp in sends:
            cp.wait()

    out_types = [jax.ShapeDtypeStruct((2,) + h.shape[1:], h.dtype) for h in sums]
    return _sequencer_call(body, out_types, [(n, 2), (n, 2)], collective_id, name)(*sums)


def _send_second(sums, *, collective_id, name):
    n = len(sums)

    def body(*refs):
        ins, outs = refs[:n], refs[n:2 * n]
        ssem, rsem = refs[2 * n:]
        x, y, c = _mesh_pos()
        nb = (_flip(x, 1 - c), _flip(y, c), c)
        other = 1 - (c * y + (1 - c) * x)
        _handshake([nb])
        sends = []
        for a in range(n):
            cp = _remote(ins[a].at[other], outs[a], ssem.at[a], rsem.at[a], nb)
            cp.start()
            sends.append(cp)
        for cp in sends:
            cp.wait()

    out_types = [jax.ShapeDtypeStruct(s.shape[1:], s.dtype) for s in sums]
    return _sequencer_call(body, out_types, [(n,), (n,)], collective_id, name)(*sums)


def _swap_halves(halves, *, collective_id, name):
    n = len(halves)

    def body(*refs):
        ins, outs = refs[:n], refs[n:2 * n]
        ssem, rsem = refs[2 * n:]
        x, y, c = _mesh_pos()
        sib = (x, y, 1 - c)
        _handshake([sib])
        cps = []
        for a in range(n):
            cp = _remote(ins[a], outs[a], ssem.at[a], rsem.at[a], sib)
            cp.start()
            cps.append(cp)
        for cp in cps:
            cp.wait()

    out_types = [jax.ShapeDtypeStruct(h.shape, h.dtype) for h in halves]
    return _sequencer_call(body, out_types, [(n,), (n,)], collective_id, name)(*halves)


def _all_reduce_small(buf, *, name):
    r = buf.shape[0]
    rh = r // 2

    def body(in_ref, out_ref, x1_ref, x2_ref, ssem, rsem):
        x, y, c = _mesh_pos()
        me = 2 * x + y
        sib = (x, y, 1 - c)
        chips = _other_chips(x, y)
        cp = _remote(in_ref, x1_ref, ssem.at[0], rsem.at[0], sib)
        cp.start()
        cp.wait()
        off = pl.multiple_of(c * rh, SUBLANES)
        x2_ref[me] = in_ref[pl.ds(off, rh), :] + x1_ref[pl.ds(off, rh), :]
        sends = []
        for j, (cx, cy) in enumerate(chips):
            s = _remote(x2_ref.at[me], x2_ref.at[me], ssem.at[1 + j], rsem.at[1 + j], (cx, cy, c))
            s.start()
            sends.append(s)
        for j, (cx, cy) in enumerate(chips):
            slot = x2_ref.at[2 * cx + cy]
            _remote(slot, slot, ssem.at[1 + j], rsem.at[1 + j], (cx, cy, c)).wait_recv()
        out_ref[pl.ds(off, rh), :] = ((x2_ref[0] + x2_ref[1]) + x2_ref[2]) + x2_ref[3]
        for s in sends:
            s.wait_send()
        mine = out_ref.at[pl.ds(off, rh), :]
        s3 = _remote(mine, mine, ssem.at[4], rsem.at[4], sib)
        s3.start()
        off2 = pl.multiple_of((1 - c) * rh, SUBLANES)
        theirs = out_ref.at[pl.ds(off2, rh), :]
        _remote(theirs, theirs, ssem.at[4], rsem.at[4], sib).wait_recv()
        s3.wait_send()

    vm = pl.BlockSpec(memory_space=pltpu.VMEM)
    return pl.pallas_call(
        body, name=name, in_specs=[vm], out_specs=vm,
        out_shape=jax.ShapeDtypeStruct((r, LANES), F32),
        scratch_shapes=[pltpu.VMEM((r, LANES), F32), pltpu.VMEM((N_CHIPS, rh, LANES), F32),
                        pltpu.SemaphoreType.DMA((5,)), pltpu.SemaphoreType.DMA((5,))],
    )(buf)


PACK_ALIGN = 2 * SUBLANES * LANES


def _pack(arrays, rows_multiple=2 * SUBLANES):
    parts, offs, off = [], [], 0
    for a in arrays:
        flat = a.reshape(-1).astype(F32)
        padded = -(-flat.shape[0] // PACK_ALIGN) * PACK_ALIGN
        parts.append(jnp.pad(flat, (0, padded - flat.shape[0])))
        offs.append(off)
        off += padded
    buf = jnp.concatenate(parts).reshape(-1, LANES)
    return buf, offs


def _unpack(buf, offs, shapes):
    flat = buf.reshape(-1)
    out = []
    for off, shp in zip(offs, shapes):
        size = 1
        for d in shp:
            size *= d
        out.append(flat[off:off + size].reshape(shp))
    return out


def _cols_from_shards(g4):
    _, k, ns = g4.shape
    return jnp.transpose(g4, (1, 0, 2)).reshape(k, N_CHIPS * ns)


def _cols_to_shards(w):
    k, n = w.shape
    return jnp.transpose(w.reshape(k, N_CHIPS, n // N_CHIPS), (1, 0, 2))


def _block_cols(w, parts, blocks):
    lead = w.shape[:-1]
    width = w.shape[-1] // (parts * blocks)
    w = w.reshape(lead + (parts, blocks, width))
    w = jnp.swapaxes(w, -3, -2)
    return w.reshape(lead + (parts * blocks * width,))


def _unblock_cols(w, parts, blocks):
    lead = w.shape[:-1]
    width = w.shape[-1] // (parts * blocks)
    w = w.reshape(lead + (blocks, parts, width))
    w = jnp.swapaxes(w, -3, -2)
    return w.reshape(lead + (parts * blocks * width,))


def _pair_blockdiag(w8):
    w = w8.reshape(4, 2, 64, 64)
    z = jnp.zeros((4, 64, 64), w8.dtype)
    top = jnp.concatenate([w[:, 0], z], axis=2)
    bot = jnp.concatenate([z, w[:, 1]], axis=2)
    return jnp.concatenate([top, bot], axis=1)


def _pair_diag_blocks(w4):
    a = w4[:, :64, :64]
    b = w4[:, 64:, 64:]
    return jnp.stack([a, b], axis=1).reshape(8, 64, 64)


def _local_step(x, target, wts, on_event=None):
    s = x.shape[0]
    g = {}

    def event(name, token):
        if on_event is not None:
            on_event(name, g, token)

    win0 = wts["w_in0"]
    wout0 = wts["w_out0"]
    win1 = wts["w_in1"]
    wout1 = wts["w_out1"]
    wup = wts["w_up"]
    wdown = wts["w_down"]
    w4, b4, w3, b3 = wts["w4"], wts["b4"], wts["w3"], wts["b3"]
    wa, wx = wts["wa"], wts["wx"]
    wat, wxt = jnp.swapaxes(wa, 1, 2), jnp.swapaxes(wx, 1, 2)
    ba, bx, lam = wts["ba"], wts["bx"], wts["lam"]
    fcw, fcb = wts["ffn_cw"], wts["ffn_cb"]
    sgu_w, sgu_wt = wts["sgu_w"], wts["sgu_wt"]
    sgu_bias, sgu_gn = wts["sgu_bias"], wts["sgu_gn"]
    bf = wts["bf"]

    lane = jnp.arange(LANES)
    seg = jnp.where((lane[:, None] // 64) == (lane[None, :] // 64), 1.0 / 64.0, 0.0).astype(BF16)
    sel = jnp.stack([jnp.broadcast_to((lane[:, None] < 64), (LANES, LANES)),
                     jnp.broadcast_to((lane[:, None] >= 64), (LANES, LANES))]).astype(BF16)
    tril = (lane[:, None] >= lane[None, :]).astype(F32)

    n0 = _norm_fwd(x, wts["g_mix0"], name="norm_mix0")
    p0 = _mm([n0], win0, nb=1280, name="mm_in0")
    ya, yb, hl = _even_core_fwd(p0, w4, b4, wa, ba, wx, bx, lam, w3, b3, name="even_fwd")
    h1 = _mm([ya, yb], wout0, res=x, ts=1024, name="mm_out0")

    def ffn_fwd(h, layer):
        n = _norm_fwd(h, wts["g_ffn"][layer], name=f"norm_ffn{layer}")
        up = _mm([n], wup[layer], out_dtype=BF16, ts=1024, nb=1408, name=f"mm_up{layer}")
        act = _ffn_core_fwd(up, fcw[layer], fcb[layer], name=f"ffn_fwd{layer}")
        hn = _mm([act], wdown[layer], res=h, name=f"mm_down{layer}")
        return n, up, act, hn

    n1, up0, act0, h2 = ffn_fwd(h1, 0)

    n2 = _norm_fwd(h2, wts["g_mix1"], name="norm_mix1")
    p1 = _mm([n2], win1, name="mm_in1")
    yc = _sgu_fwd(p1, sgu_gn, sgu_w, sgu_bias, seg, name="sgu_fwd")
    cum = _fcum_fwd(p1, bf, name="fcum_fwd")
    c8 = cum[:, :8]
    cq = jnp.broadcast_to(c8[:, :, None], (s, 8, LANES)).reshape(s, 8 * LANES)
    ck = jnp.transpose(c8).reshape(8, 1, s)
    yd, lb = _fox_fwd(p1, cq, ck, name="fox_fwd")
    h3 = _mm([yc, yd], wout1, res=h2, ts=1024, name="mm_out1")

    n3, up1, act1, h4 = ffn_fwd(h3, 1)
    dh4, loss, g["final_norm"] = _final(h4, wts["g_final"], target, name="final")

    def ffn_bwd(dh, h, n, up, act, layer):
        dact = _mm([dh], wdown[layer], trans_w=True, out_dtype=BF16, ts=1024, nb=1408, name=f"mm_dact{layer}")
        g[f"w_down{layer}"] = _mm_tn([act], [dh], ts=1024, nb=512, name=f"mm_dwdown{layer}")
        event(f"dwdown{layer}", g[f"w_down{layer}"])
        dgate, dval, dcwg, dcwv, dcbg, dcbv = _ffn_core_bwd(dact, up, fcw[layer], fcb[layer], name=f"ffn_bwd{layer}")
        event(f"ffn_bwd{layer}", dgate)
        g[f"w_up{layer}"] = _mm_tn([n], [dgate, dval], ts=1024, nb=1408, name=f"mm_dwup{layer}")
        event(f"dwup{layer}", g[f"w_up{layer}"])
        dhn, g[f"g_ffn{layer}"] = _mm([dgate, dval], wup[layer], trans_w=True, ts=256,
                                      norm_bwd=(h, wts["g_ffn"][layer], dh), name=f"mm_dn_ffn{layer}")
        g[f"ffn_cw{layer}"] = jnp.concatenate([dcwg, dcwv], axis=1)
        g[f"ffn_cb{layer}"] = jnp.concatenate([dcbg, dcbv], axis=1)
        return dhn

    dh3 = ffn_bwd(dh4, h3, n3, up1, act1, 1)

    dy1 = _mm([dh3], wout1, trans_w=True, ts=1024, name="mm_dy1")
    g["w_out1"] = _mm_tn([yc, yd], [dh3], ts=1024, nb=512, name="mm_dwout1")
    event("dwout1", g["w_out1"])
    dzu, dzg, g["sgu_w"], g["sgu_bias"], g["sgu_gn"] = _sgu_bwd(
        p1, dy1, sgu_gn, sgu_w, sgu_wt, sgu_bias, seg, tril, name="sgu_bwd")
    delta = _fox_delta(dy1, yd, sel, name="fox_delta")
    dq, dk, dv, dck, dcq = _fox_bwd(p1, dy1, lb, delta, ck, name="fox_bwd")
    event("fox_bwd", dq)
    dcs = jnp.pad(jnp.transpose(dck.reshape(8, s)), ((0, 0), (0, LANES - 8)))
    df, g["bf"] = _fcum_bwd(dcs, dcq, p1, bf, name="fcum_bwd")
    dp1 = jnp.concatenate([dzu, dzg, dq, dk, dv, df], axis=1)
    g["w_in1"] = _mm_tn([n2], [dp1], ts=1024, nb=896, name="mm_dwin1")
    event("dwin1", g["w_in1"])
    dh2, g["g_mix1"] = _mm([dp1], win1, trans_w=True, norm_bwd=(h2, wts["g_mix1"], dh3), name="mm_dn_mix1")

    dh1 = ffn_bwd(dh2, h1, n1, up0, act0, 0)

    dy0 = _mm([dh1], wout0, trans_w=True, ts=1024, name="mm_dy0")
    g["w_out0"] = _mm_tn([ya, yb], [dh1], ts=1024, nb=512, name="mm_dwout0")
    event("dwout0", g["w_out0"])
    (dp0, g["w4"], g["b4"], g["wa"], g["ba"], g["wx"], g["bx"], g["lam"], g["w3"], g["b3"]) = _even_core_bwd(
        dy0, p0, hl, w4, b4, wa, wat, ba, wx, wxt, bx, lam, w3, b3, name="even_bwd")
    event("even_bwd", dp0)
    g["w_in0"] = _mm_tn([n0], [dp0], ts=1024, nb=640, name="mm_dwin0")
    event("dwin0", g["w_in0"])
    grad_x, g["g_mix0"] = _mm([dp0], win0, trans_w=True, norm_bwd=(x, wts["g_mix0"], dh1), name="mm_dn_mix0")
    return loss, grad_x, g


def _prepare_weights(nat):
    lane = jnp.arange(LANES)
    tril = (lane[:, None] >= lane[None, :]).astype(F32)
    sgu_tril = nat["sgu_w"][0] * tril
    w_in1 = nat["mix1_w_in"]
    nblk = D_FF // FFN_CB
    return {
        "w_in0": _block_cols(nat["mix0_w_in"], 5, 4),
        "w_out0": nat["mix0_w_out"],
        "w_in1": jnp.pad(w_in1, ((0, 0), (0, 21 * LANES - w_in1.shape[1]))),
        "w_out1": nat["mix1_w_out"],
        "w_up": [nat["ffn_up"][l] for l in range(2)],
        "w_down": [nat["ffn_down"][l] for l in range(2)],
        "w4": nat["lru_conv_w"], "b4": nat["lru_conv_b"], "w3": nat["sconv_w"], "b3": nat["sconv_b"],
        "wa": _pair_blockdiag(nat["lru_wa"][0]).astype(BF16), "wx": _pair_blockdiag(nat["lru_wx"][0]).astype(BF16),
        "ba": nat["lru_ba"], "bx": nat["lru_bx"], "lam": nat["lru_lambda"],
        "ffn_cw": [nat["ffn_conv_w"][l] for l in range(2)],
        "ffn_cb": [nat["ffn_conv_b"][l:l + 1] for l in range(2)],
        "sgu_w": sgu_tril.astype(BF16), "sgu_wt": jnp.swapaxes(sgu_tril, 1, 2).astype(BF16),
        "sgu_bias": jnp.repeat(jnp.transpose(nat["sgu_b"][0]), 64, axis=1), "sgu_gn": nat["sgu_norm"],
        "bf": jnp.pad(nat["fox_bf"], ((0, 0), (0, LANES - 8))),
        "g_mix0": nat["mix0_norm"], "g_mix1": nat["mix1_norm"],
        "g_ffn": [nat["ffn_norm"][0:1], nat["ffn_norm"][1:2]], "g_final": nat["final_norm"].reshape(1, D_MODEL),
    }


def _natural_grads(g):
    nblk = D_FF // FFN_CB
    small = {
        "mix0_norm": g["g_mix0"], "lru_conv_b": g["b4"],
        "lru_wa": _pair_diag_blocks(g["wa"])[None], "lru_ba": g["ba"],
        "lru_wx": _pair_diag_blocks(g["wx"])[None], "lru_bx": g["bx"],
        "lru_lambda": g["lam"], "sconv_b": g["b3"],
        "sgu_w": g["sgu_w"][None],
        "sgu_b": jnp.transpose(g["sgu_bias"].reshape(CHUNK, 8, 64).sum(axis=2))[None],
        "fox_bf": g["bf"][:, :8],
        "ffn_norm": jnp.concatenate([g["g_ffn0"], g["g_ffn1"]], axis=0),
        "ffn_conv_b": jnp.concatenate([g["ffn_cb0"], g["ffn_cb1"]], axis=0),
        "final_norm": g["final_norm"].reshape(D_MODEL),
        "lru_conv_w": g["w4"][None], "sconv_w": g["w3"][None],
        "ffn_conv_w": jnp.stack([g["ffn_cw0"], g["ffn_cw1"]]),
        "mix1_norm": g["g_mix1"], "sgu_norm": g["sgu_gn"],
    }
    big = {
        "mix0_w_in": _unblock_cols(g["w_in0"], 5, 4), "mix0_w_out": g["w_out0"],
        "mix1_w_in": g["w_in1"][:, :2568], "mix1_w_out": g["w_out1"],
        "ffn_up0": g["w_up0"], "ffn_up1": g["w_up1"],
        "ffn_down0": g["w_down0"], "ffn_down1": g["w_down1"],
    }
    return small, big


COL_SHARDED = ("mix0_w_in", "mix1_w_in", "ffn_up0", "ffn_up1")
COL_ALIGNED = ("mix0_w_in", "ffn_up0", "ffn_up1")
SMALL_SHARDED = ("lru_conv_w", "sconv_w", "ffn_conv_w", "mix1_norm", "sgu_norm")
SMALL_REPLICATED = ("mix0_norm", "lru_conv_b", "lru_wa", "lru_ba", "lru_wx", "lru_bx", "lru_lambda", "sconv_b",
                    "sgu_w", "sgu_b", "fox_bf", "ffn_norm", "ffn_conv_b", "final_norm")
BIG = ("mix0_w_in", "mix0_w_out", "mix1_w_in", "mix1_w_out", "ffn_up0", "ffn_up1", "ffn_down0", "ffn_down1")
WEIGHT_ORDER = ("mix0_norm", "mix0_w_in", "lru_conv_w", "lru_conv_b", "lru_wa", "lru_ba", "lru_wx", "lru_bx",
                "lru_lambda", "sconv_w", "sconv_b", "mix0_w_out", "mix1_norm", "mix1_w_in", "sgu_norm", "sgu_w",
                "sgu_b", "fox_bf", "mix1_w_out", "ffn_norm", "ffn_up", "ffn_conv_w", "ffn_conv_b", "ffn_down",
                "final_norm")


GATHER_GROUPS = (("mix0_w_in", "mix0_w_out"), ("ffn_up0",), ("ffn_down0", "mix1_w_in"),
                 ("mix1_w_out", "ffn_up1", "ffn_down1"))
CID_GATHER, CID_PAIR, CID_FIRST, CID_SECOND, CID_SWAP = 1, 2, 3, 4, 5


class _GradReducer:
    def __init__(self):
        x, y, c = _mesh_pos()
        self.core = c.reshape(1).astype(jnp.int32)
        self.keep = jnp.stack([c * (2 * x + t) + (1 - c) * (2 * t + y) for t in range(2)]).astype(jnp.int32)
        self.mine = (c * y + (1 - c) * x).reshape(1).astype(jnp.int32)
        self.groups = {}

    @staticmethod
    def _view(name, a):
        if name in COL_ALIGNED:
            return a.reshape(2, a.shape[0] // 2, a.shape[1])
        if name in COL_SHARDED:
            a = _cols_to_shards(a)
            return a.reshape(N_CHIPS, 2, a.shape[1] // 2, a.shape[2])
        rows = a.shape[0] // (2 * N_CHIPS)
        return a.reshape(N_CHIPS, 2, rows, a.shape[1])

    def start(self, group, grads):
        names = tuple(grads)
        views = [self._view(k, grads[k]) for k in names]
        cols = [k in COL_ALIGNED for k in names]
        data = _send_other_half(views, cols, collective_id=CID_PAIR, name=f"rs_pair_{group}")
        self.groups[group] = dict(names=names, stage=0, views=views, cols=cols, data=data)

    def step(self, group, after):
        st = self.groups[group]
        names = st["names"]
        if st["stage"] == 0:
            sums = [_pair_sum(a, col, b, self.core, after, name=f"rs_pair_sum_{k}")
                    for k, a, col, b in zip(names, st["views"], st["cols"], st["data"])]
            st["keep"] = [s32 for s32, _ in sums]
            st["data"] = _send_first([s16 for _, s16 in sums], collective_id=CID_FIRST, name=f"rs_first_{group}")
        elif st["stage"] == 1:
            sums = [_first_sum(s32, r, self.keep, after, name=f"rs_first_sum_{k}")
                    for k, s32, r in zip(names, st["keep"], st["data"])]
            st["keep"] = [s32 for s32, _ in sums]
            st["data"] = _send_second([s16 for _, s16 in sums], collective_id=CID_SECOND, name=f"rs_second_{group}")
        else:
            st["mine"] = [_second_sum(s32, r, self.mine, after, name=f"rs_second_sum_{k}")
                          for k, s32, r in zip(names, st["keep"], st["data"])]
            st["data"] = _swap_halves(st["mine"], collective_id=CID_SWAP, name=f"rs_swap_{group}")
        st["stage"] += 1

    def result(self, group):
        st = self.groups[group]
        return {k: (a, b) for k, a, b in zip(st["names"], st["mine"], st["data"])}


def _train_step(x, target, w, m, v):
    x2 = x[0]
    t2 = target[0]
    chip = 2 * lax.axis_index("x") + lax.axis_index("y")
    core_arr = lax.axis_index("c").reshape(1).astype(jnp.int32)
    chip_arr = chip.reshape(1).astype(jnp.int32)

    big_shards = {
        "mix0_w_in": w["mix0_w_in"][0], "mix0_w_out": w["mix0_w_out"][0],
        "mix1_w_in": w["mix1_w_in"][0], "mix1_w_out": w["mix1_w_out"][0],
        "ffn_up0": w["ffn_up"][0], "ffn_up1": w["ffn_up"][1],
        "ffn_down0": w["ffn_down"][0], "ffn_down1": w["ffn_down"][1],
    }
    small_shards = [w[k] for k in SMALL_SHARDED]
    small_buf, small_offs = _pack(small_shards)
    full = {}
    small_all = None
    for gi, names in enumerate(GATHER_GROUPS):
        cols = [k in COL_ALIGNED for k in names]
        placed = [_place(big_shards[k], col, chip_arr, BF16, name=f"place_{k}") for k, col in zip(names, cols)]
        if gi == 0:
            placed.append(_place(small_buf, False, chip_arr, F32, name="place_small"))
            cols = cols + [False]
        gathered = _all_gather(placed, cols, collective_id=CID_GATHER, name=f"gather_weights{gi}")
        if gi == 0:
            small_all = gathered[-1].reshape(N_CHIPS, -1, LANES)
        for k, arr in zip(names, gathered):
            if k in COL_ALIGNED:
                full[k] = arr.reshape(arr.shape[0] * arr.shape[1], arr.shape[2])
            elif k in COL_SHARDED:
                full[k] = _cols_from_shards(arr.reshape((N_CHIPS, arr.shape[1] * arr.shape[2], arr.shape[3])))
            else:
                full[k] = arr.reshape(-1, arr.shape[3])
    per_chip = [_unpack(small_all[k], small_offs, [a.shape for a in small_shards]) for k in range(N_CHIPS)]
    lru_conv_w = jnp.concatenate([per_chip[k][0] for k in range(N_CHIPS)], axis=-1)[0]
    sconv_w = jnp.concatenate([per_chip[k][1] for k in range(N_CHIPS)], axis=-1)[0]
    ffn_conv_w = jnp.concatenate([per_chip[k][2] for k in range(N_CHIPS)], axis=-1)
    mix1_norm = jnp.concatenate([per_chip[k][3] for k in range(N_CHIPS)], axis=-1)
    sgu_norm = jnp.concatenate([per_chip[k][4] for k in range(N_CHIPS)], axis=-1)

    nat = {
        "mix0_w_in": full["mix0_w_in"], "mix0_w_out": full["mix0_w_out"],
        "mix1_w_in": full["mix1_w_in"], "mix1_w_out": full["mix1_w_out"],
        "ffn_up": [full["ffn_up0"], full["ffn_up1"]], "ffn_down": [full["ffn_down0"], full["ffn_down1"]],
        "lru_conv_w": lru_conv_w, "sconv_w": sconv_w, "ffn_conv_w": ffn_conv_w, "mix1_norm": mix1_norm,
        "sgu_norm": sgu_norm,
    }
    for k in SMALL_REPLICATED:
        nat[k] = w[k]
    wts = _prepare_weights(nat)

    reducer = _GradReducer()

    def on_event(name, g, token):
        if name == "dwup1":
            reducer.start("ffn1", {"ffn_up1": g["w_up1"], "ffn_down1": g["w_down1"]})
        elif name in ("dwout1", "fox_bwd"):
            reducer.step("ffn1", token)
        elif name == "dwin1":
            reducer.step("ffn1", token)
            reducer.start("mix1", {"mix1_w_in": g["w_in1"][:, :2568], "mix1_w_out": g["w_out1"]})
        elif name in ("dwdown0", "ffn_bwd0"):
            reducer.step("mix1", token)
        elif name == "dwup0":
            reducer.step("mix1", token)
            reducer.start("ffn0", {"ffn_up0": g["w_up0"], "ffn_down0": g["w_down0"]})
        elif name in ("dwout0", "even_bwd"):
            reducer.step("ffn0", token)
        elif name == "dwin0":
            reducer.step("ffn0", token)
            reducer.start("mix0", {"mix0_w_in": _unblock_cols(g["w_in0"], 5, 4), "mix0_w_out": g["w_out0"]})

    loss, grad_x, g = _local_step(x2, t2, wts, on_event)
    grads_small, _ = _natural_grads(g)

    small_names = SMALL_REPLICATED + SMALL_SHARDED
    small_list = [grads_small[k] for k in small_names] + [loss[:, :1]]
    sbuf, soffs = _pack(small_list)
    sred = _all_reduce_small(sbuf, name="reduce_small")
    small_red = _unpack(sred, soffs, [a.shape for a in small_list])
    loss_total = small_red[-1][0, 0]
    gsum = dict(zip(small_names, small_red[:-1]))
    for k in SMALL_SHARDED:
        width = w[k].shape[-1]
        gsum[k] = lax.dynamic_slice_in_dim(gsum[k], chip * width, width, axis=gsum[k].ndim - 1)

    out_g, out_d, out_m, out_v = {}, {}, {}, {}
    reduced = {}
    for group in ("ffn1", "mix1", "ffn0"):
        reduced.update(reducer.result(group))

    def update(pname, keys):
        mine = [reduced[k][0] for k in keys]
        theirs = [reduced[k][1] for k in keys]
        out_g[pname], out_d[pname], out_m[pname], out_v[pname] = _adamw_halves(
            w[pname], mine, theirs, m[pname], v[pname], core_arr, name=f"adamw_{pname}")
        return out_d[pname]

    reducer.step("mix0", update("ffn_up", ("ffn_up0", "ffn_up1")))
    small_w = [w[k] for k in small_names]
    pg, offs = _pack([gsum[k] for k in small_names])
    pw, _ = _pack(small_w)
    pm, _ = _pack([m[k] for k in small_names])
    pv, _ = _pack([v[k] for k in small_names])
    sd, sm, sv = _adamw(pw, pg, pm, pv, name="adamw_small")
    reducer.step("mix0", update("ffn_down", ("ffn_down0", "ffn_down1")))
    update("mix1_w_in", ("mix1_w_in",))
    reducer.step("mix0", update("mix1_w_out", ("mix1_w_out",)))
    reduced.update(reducer.result("mix0"))
    update("mix0_w_in", ("mix0_w_in",))
    update("mix0_w_out", ("mix0_w_out",))

    shapes = [a.shape for a in small_w]
    for k, dd, mm, vv in zip(small_names, _unpack(sd, offs, shapes), _unpack(sm, offs, shapes),
                             _unpack(sv, offs, shapes)):
        out_g[k], out_d[k], out_m[k], out_v[k] = gsum[k].reshape(w[k].shape), dd, mm, vv

    outs = [loss_total, grad_x[None]]
    for d in (out_g, out_d, out_m, out_v):
        outs.extend(d[k] for k in WEIGHT_ORDER)
    return tuple(outs)


def kernel(x, mix0_norm, mix0_w_in, lru_conv_w, lru_conv_b, lru_wa, lru_ba, lru_wx, lru_bx, lru_lambda, sconv_w, sconv_b, mix0_w_out, mix1_norm, mix1_w_in, sgu_norm, sgu_w, sgu_b, fox_bf, mix1_w_out, ffn_norm, ffn_up, ffn_conv_w, ffn_conv_b, ffn_down, final_norm, loss_target, m_mix0_norm, m_mix0_w_in, m_lru_conv_w, m_lru_conv_b, m_lru_wa, m_lru_ba, m_lru_wx, m_lru_bx, m_lru_lambda, m_sconv_w, m_sconv_b, m_mix0_w_out, m_mix1_norm, m_mix1_w_in, m_sgu_norm, m_sgu_w, m_sgu_b, m_fox_bf, m_mix1_w_out, m_ffn_norm, m_ffn_up, m_ffn_conv_w, m_ffn_conv_b, m_ffn_down, m_final_norm, v_mix0_norm, v_mix0_w_in, v_lru_conv_w, v_lru_conv_b, v_lru_wa, v_lru_ba, v_lru_wx, v_lru_bx, v_lru_lambda, v_sconv_w, v_sconv_b, v_mix0_w_out, v_mix1_norm, v_mix1_w_in, v_sgu_norm, v_sgu_w, v_sgu_b, v_fox_bf, v_mix1_w_out, v_ffn_norm, v_ffn_up, v_ffn_conv_w, v_ffn_conv_b, v_ffn_down, v_final_norm):
    w = dict(zip(WEIGHT_ORDER, (mix0_norm, mix0_w_in, lru_conv_w, lru_conv_b, lru_wa, lru_ba, lru_wx, lru_bx, lru_lambda, sconv_w, sconv_b, mix0_w_out, mix1_norm, mix1_w_in, sgu_norm, sgu_w, sgu_b, fox_bf, mix1_w_out, ffn_norm, ffn_up, ffn_conv_w, ffn_conv_b, ffn_down, final_norm)))
    m = dict(zip(WEIGHT_ORDER, (m_mix0_norm, m_mix0_w_in, m_lru_conv_w, m_lru_conv_b, m_lru_wa, m_lru_ba, m_lru_wx, m_lru_bx, m_lru_lambda, m_sconv_w, m_sconv_b, m_mix0_w_out, m_mix1_norm, m_mix1_w_in, m_sgu_norm, m_sgu_w, m_sgu_b, m_fox_bf, m_mix1_w_out, m_ffn_norm, m_ffn_up, m_ffn_conv_w, m_ffn_conv_b, m_ffn_down, m_final_norm)))
    v = dict(zip(WEIGHT_ORDER, (v_mix0_norm, v_mix0_w_in, v_lru_conv_w, v_lru_conv_b, v_lru_wa, v_lru_ba, v_lru_wx, v_lru_bx, v_lru_lambda, v_sconv_w, v_sconv_b, v_mix0_w_out, v_mix1_norm, v_mix1_w_in, v_sgu_norm, v_sgu_w, v_sgu_b, v_fox_bf, v_mix1_w_out, v_ffn_norm, v_ffn_up, v_ffn_conv_w, v_ffn_conv_b, v_ffn_down, v_final_norm)))
    return _train_step(x, loss_target, w, m, v)
```

```python
import functools

import jax
import jax.numpy as jnp
from jax import lax
from jax.experimental import pallas as pl
from jax.experimental.pallas import tpu as pltpu
from jax.experimental.pallas import tpu_sc as plsc

F32 = jnp.float32
BF16 = jnp.bfloat16
MESH = pl.DeviceIdType.MESH

D_MODEL = 1024
LANES = 128
SUBLANES = 8
N_CHIPS = 4
EPS = 1e-6
LRU_C = 8.0
D_FF = 2816
FFN_CB = 256
CHUNK = 128
NEG = -1e30

ADAM_LR = 0.001
ADAM_B1 = 0.9
ADAM_B2 = 0.999
ADAM_EPS = 1e-08
ADAM_WD = 0.01
ADAM_STEP = 10
ADAM_C1 = 1.0 - ADAM_B1 ** ADAM_STEP
ADAM_C2 = 1.0 - ADAM_B2 ** ADAM_STEP

_GELU_C = 0.7978845608028654
_GELU_A = 0.044715


def _sigmoid(x):
    return 1.0 / (1.0 + jnp.exp(-x))


def _sigmoid_tanh(x):
    return 0.5 * jnp.tanh(0.5 * x) + 0.5


def _log1p_pos(e):
    w = 1.0 + e
    return jnp.where(w == 1.0, e, jnp.log(w) * (e / (w - 1.0)))


def _softplus(x):
    return jnp.maximum(x, 0.0) + _log1p_pos(jnp.exp(-jnp.abs(x)))


def _gelu(x):
    t = jnp.tanh(_GELU_C * (x + _GELU_A * (x * x * x)))
    return 0.5 * x * (1.0 + t), t


def _gelu_grad(x, t):
    return 0.5 * (1.0 + t) + 0.5 * x * (1.0 - t * t) * (_GELU_C * (1.0 + 3.0 * _GELU_A * x * x))


def _rows(shape):
    return lax.broadcasted_iota(jnp.int32, shape, 0)


def _lanes(shape):
    return lax.broadcasted_iota(jnp.int32, shape, 1)


def _shift_down(x, halo8, j):
    if j == 0:
        return x
    r = pltpu.roll(x, j, 0)
    hr = pltpu.roll(halo8, j, 0)
    top = jnp.where(_rows(hr.shape) < j, hr, r[:SUBLANES])
    return jnp.concatenate([top, r[SUBLANES:]], axis=0)


def _shift_up(x, next8, j):
    if j == 0:
        return x
    n = x.shape[0]
    r = pltpu.roll(x, n - j, 0)
    nr = pltpu.roll(next8, SUBLANES - j, 0)
    bot = jnp.where(_rows(nr.shape) >= SUBLANES - j, nr, r[n - SUBLANES:])
    return jnp.concatenate([r[:n - SUBLANES], bot], axis=0)


def _scan_fwd(a, u):
    n = a.shape[0]
    row = _rows(a.shape)
    h = u
    k = 1
    while k < n:
        keep = row >= k
        h_sh = jnp.where(keep, pltpu.roll(h, k, 0), 0.0)
        a_sh = jnp.where(keep, pltpu.roll(a, k, 0), 1.0)
        h = a * h_sh + h
        a = a * a_sh
        k *= 2
    return h, a


def _scan_rev(b, d):
    n = b.shape[0]
    row = _rows(b.shape)
    g = d
    k = 1
    while k < n:
        keep = row < n - k
        g_sh = jnp.where(keep, pltpu.roll(g, n - k, 0), 0.0)
        b_sh = jnp.where(keep, pltpu.roll(b, n - k, 0), 1.0)
        g = b * g_sh + g
        b = b * b_sh
        k *= 2
    return g, b


def _cumsum_fwd(x):
    n = x.shape[0]
    row = _rows(x.shape)
    k = 1
    while k < n:
        x = x + jnp.where(row >= k, pltpu.roll(x, k, 0), 0.0)
        k *= 2
    return x


def _cumsum_rev(x):
    n = x.shape[0]
    row = _rows(x.shape)
    k = 1
    while k < n:
        x = x + jnp.where(row < n - k, pltpu.roll(x, n - k, 0), 0.0)
        k *= 2
    return x


def _dot(a, b):
    return lax.dot_general(a, b, (((1,), (0,)), ((), ())), preferred_element_type=F32)


def _dot_nt(a, b):
    return lax.dot_general(a, b, (((1,), (1,)), ((), ())), preferred_element_type=F32)


def _dot_tn(a, b):
    return lax.dot_general(a, b, (((0,), (0,)), ((), ())), preferred_element_type=F32)


def _dot_split(x, m_bf16):
    hi = x.astype(BF16)
    lo = (x - hi.astype(F32)).astype(BF16)
    return _dot(hi, m_bf16) + _dot(lo, m_bf16)


def _tile_rows(ts, s):
    return min(ts, s)


def _mm(a_list, w, *, trans_w=False, res=None, norm_bwd=None, norm_out=None, out_dtype=F32, ts=512, nb=None,
        name):
    s = a_list[0].shape[0]
    ks = [a.shape[1] for a in a_list]
    k = sum(ks)
    n = w.shape[0] if trans_w else w.shape[1]
    ts = _tile_rows(ts, s)
    nb = n if nb is None else nb
    na = len(a_list)
    has_res = res is not None
    fused = norm_bwd is not None
    normed = norm_out is not None
    offs = [sum(ks[:p]) for p in range(na)]

    def body(*refs):
        a_refs = refs[:na]
        w_ref = refs[na]
        acc = None
        for a_ref, off, kk in zip(a_refs, offs, ks):
            a = a_ref[...].astype(BF16)
            if trans_w:
                part = _dot_nt(a, w_ref[:, off:off + kk])
            else:
                part = _dot(a, w_ref[off:off + kk, :])
            acc = part if acc is None else acc + part
        if has_res:
            acc = acc + refs[na + 1][...]
        if normed:
            gn_ref, o_ref, n_ref = refs[-3:]
            o_ref[...] = acc.astype(out_dtype)
            r = lax.rsqrt(jnp.mean(acc * acc, axis=-1, keepdims=True) + EPS)
            n_ref[...] = ((acc * r) * gn_ref[...]).astype(BF16)
            return
        if not fused:
            refs[-1][...] = acc.astype(out_dtype)
            return
        h_ref, g_ref, dres_ref, dh_ref, dg_ref = refs[na + 1:]
        i = pl.program_id(1)
        x = h_ref[...]
        r = lax.rsqrt(jnp.mean(x * x, axis=-1, keepdims=True) + EPS)
        xhat = x * r
        part = jnp.sum(acc * xhat, axis=0, keepdims=True)

        @pl.when(i == 0)
        def _():
            dg_ref[...] = part

        @pl.when(i > 0)
        def _():
            dg_ref[...] += part

        dxh = acc * g_ref[...]
        dh_ref[...] = dres_ref[...] + r * (dxh - xhat * jnp.mean(dxh * xhat, axis=-1, keepdims=True))

    in_specs = [pl.BlockSpec((ts, kk), lambda j, i: (i, 0)) for kk in ks]
    if trans_w:
        in_specs.append(pl.BlockSpec((nb, k), lambda j, i: (j, 0)))
    else:
        in_specs.append(pl.BlockSpec((k, nb), lambda j, i: (0, j)))
    args = list(a_list) + [w]
    tile = pl.BlockSpec((ts, nb), lambda j, i: (i, j))
    if has_res:
        in_specs.append(tile)
        args.append(res)
    if fused:
        assert nb == n and not has_res
        vec = pl.BlockSpec((1, n), lambda j, i: (0, 0))
        h, g, dres = norm_bwd
        return pl.pallas_call(
            body, name=name, grid=(1, s // ts), in_specs=in_specs + [tile, vec, tile],
            out_specs=(tile, vec),
            out_shape=(jax.ShapeDtypeStruct((s, n), F32), jax.ShapeDtypeStruct((1, n), F32)),
        )(*args, h, g, dres)
    if normed:
        assert nb == n and out_dtype == F32
        vec = pl.BlockSpec((1, n), lambda j, i: (0, 0))
        return pl.pallas_call(
            body, name=name, grid=(1, s // ts), in_specs=in_specs + [vec], out_specs=(tile, tile),
            out_shape=(jax.ShapeDtypeStruct((s, n), F32), jax.ShapeDtypeStruct((s, n), BF16)),
        )(*args, norm_out)
    return pl.pallas_call(
        body, name=name, grid=(n // nb, s // ts), in_specs=in_specs, out_specs=tile,
        out_shape=jax.ShapeDtypeStruct((s, n), out_dtype),
    )(*args)


def _mm_tn(a_list, b_list, *, ts=512, nb=None, name):
    s = b_list[0].shape[0]
    ks = [a.shape[1] for a in a_list]
    k = sum(ks)
    width = b_list[0].shape[1]
    n = width * len(b_list)
    ts = _tile_rows(ts, s)
    nb = width if nb is None else nb
    per = width // nb
    na = len(a_list)
    nparts = len(b_list)

    def body(*refs):
        a_refs = refs[:na]
        b_refs = refs[na:na + nparts]
        o_ref = refs[-1]
        j = pl.program_id(0)
        i = pl.program_id(1)
        parts = [r[...].astype(BF16) for r in a_refs]
        a = parts[0] if na == 1 else jnp.concatenate(parts, axis=1)

        def accumulate(b_ref):
            upd = _dot_tn(a, b_ref[...].astype(BF16))

            @pl.when(i == 0)
            def _():
                o_ref[...] = upd

            @pl.when(i > 0)
            def _():
                o_ref[...] += upd

        if nparts == 1:
            accumulate(b_refs[0])
        else:
            for part, b_ref in enumerate(b_refs):
                pl.when(j // per == part)(functools.partial(accumulate, b_ref))

    in_specs = [pl.BlockSpec((ts, kk), lambda j, i: (i, 0)) for kk in ks]
    for part in range(nparts):
        in_specs.append(pl.BlockSpec(
            (ts, nb), lambda j, i, part=part: (i, jnp.clip(j - part * per, 0, per - 1))))
    return pl.pallas_call(
        body, name=name, grid=(n // nb, s // ts), in_specs=in_specs,
        out_specs=pl.BlockSpec((k, nb), lambda j, i: (0, j)),
        out_shape=jax.ShapeDtypeStruct((k, n), F32),
    )(*a_list, *b_list)


def _norm_fwd(h, g, *, ts=512, name):
    s, d = h.shape
    ts = _tile_rows(ts, s)

    def body(h_ref, g_ref, n_ref):
        x = h_ref[...]
        r = lax.rsqrt(jnp.mean(x * x, axis=-1, keepdims=True) + EPS)
        n_ref[...] = ((x * r) * g_ref[...]).astype(BF16)

    return pl.pallas_call(
        body, name=name, grid=(s // ts,),
        in_specs=[pl.BlockSpec((ts, d), lambda i: (i, 0)), pl.BlockSpec((1, d), lambda i: (0, 0))],
        out_specs=pl.BlockSpec((ts, d), lambda i: (i, 0)),
        out_shape=jax.ShapeDtypeStruct((s, d), BF16),
    )(h, g)


def _norm_bwd(dn, h, g, dres, *, ts=512, name):
    s, d = h.shape
    ts = _tile_rows(ts, s)

    def body(dn_ref, h_ref, g_ref, dres_ref, dh_ref, dg_ref):
        i = pl.program_id(0)
        x = h_ref[...]
        dnv = dn_ref[...]
        r = lax.rsqrt(jnp.mean(x * x, axis=-1, keepdims=True) + EPS)
        xhat = x * r
        part = jnp.sum(dnv * xhat, axis=0, keepdims=True)

        @pl.when(i == 0)
        def _():
            dg_ref[...] = part

        @pl.when(i > 0)
        def _():
            dg_ref[...] += part

        dxh = dnv * g_ref[...]
        dh_ref[...] = dres_ref[...] + r * (dxh - xhat * jnp.mean(dxh * xhat, axis=-1, keepdims=True))

    tile = pl.BlockSpec((ts, d), lambda i: (i, 0))
    vec = pl.BlockSpec((1, d), lambda i: (0, 0))
    return pl.pallas_call(
        body, name=name, grid=(s // ts,), in_specs=[tile, tile, vec, tile],
        out_specs=(tile, vec),
        out_shape=(jax.ShapeDtypeStruct((s, d), F32), jax.ShapeDtypeStruct((1, d), F32)),
    )(dn, h, g, dres)


def _final(h, g, target, *, ts=512, name):
    s, d = h.shape
    ts = _tile_rows(ts, s)
    nt = s // ts

    def body(h_ref, g_ref, t_ref, dh_ref, loss_ref, dg_ref, acc_ref):
        i = pl.program_id(0)
        x = h_ref[...]
        r = lax.rsqrt(jnp.mean(x * x, axis=-1, keepdims=True) + EPS)
        xhat = x * r
        gv = g_ref[...]
        err = xhat * gv - t_ref[...]
        sq = jnp.sum(err * err, axis=0, keepdims=True)
        dy = err * (1.0 / d)
        part = jnp.sum(dy * xhat, axis=0, keepdims=True)

        @pl.when(i == 0)
        def _():
            acc_ref[...] = sq
            dg_ref[...] = part

        @pl.when(i > 0)
        def _():
            acc_ref[...] += sq
            dg_ref[...] += part

        dxh = dy * gv
        dh_ref[...] = r * (dxh - xhat * jnp.mean(dxh * xhat, axis=-1, keepdims=True))

        @pl.when(i == nt - 1)
        def _():
            tot = jnp.sum(acc_ref[...], axis=1, keepdims=True) * (0.5 / d)
            loss_ref[...] = jnp.broadcast_to(tot, (1, LANES))

    tile = pl.BlockSpec((ts, d), lambda i: (i, 0))
    vec = pl.BlockSpec((1, d), lambda i: (0, 0))
    return pl.pallas_call(
        body, name=name, grid=(nt,), in_specs=[tile, vec, tile],
        out_specs=(tile, pl.BlockSpec((1, LANES), lambda i: (0, 0)), vec),
        out_shape=(jax.ShapeDtypeStruct((s, d), F32), jax.ShapeDtypeStruct((1, LANES), F32),
                   jax.ShapeDtypeStruct((1, d), F32)),
        scratch_shapes=[pltpu.VMEM((1, d), F32)],
    )(h, g, target)


def _halo_map(ts, width_blocks):
    per = ts // SUBLANES

    def index(j, i):
        return (jnp.maximum(i * per - 1, 0), width_blocks(j))

    return index


def _even_gates(xc, wa, ba, wx, bx, sp):
    xb = xc.astype(BF16)
    r = _sigmoid(_dot(xb, wa) + ba)
    ig = _sigmoid(_dot(xb, wx) + bx)
    la = (-LRU_C) * r * sp
    a = jnp.exp(la)
    a2 = a * a
    m = jnp.sqrt(-jnp.tanh(la) * (1.0 + a2))
    return r, ig, la, a, a2, m


def _even_core_fwd(p, w4, b4, wa, ba, wx, bx, lam, w3, b3, *, ts=512, name):
    s = p.shape[0]
    ts = _tile_rows(ts, s)
    nt = s // ts
    nblk = 4

    def body(p_ref, ph_ref, w4_ref, b4_ref, wa_ref, ba_ref, wx_ref, bx_ref, lam_ref, w3_ref, b3_ref,
             ya_ref, yb_ref, hl_ref, hcar_ref):
        i = pl.program_id(1)
        first = (i > 0).astype(F32)
        xa = p_ref[:, 0:LANES]
        ga = p_ref[:, LANES:2 * LANES]
        cp = p_ref[:, 2 * LANES:3 * LANES]
        bp = p_ref[:, 3 * LANES:4 * LANES]
        vb = p_ref[:, 4 * LANES:5 * LANES]
        xa_h = ph_ref[:, 0:LANES] * first
        s_h = ph_ref[:, 2 * LANES:3 * LANES] * ph_ref[:, 4 * LANES:5 * LANES] * first

        xc = b4_ref[...] + w4_ref[3:4, :] * xa
        for k in range(3):
            xc = xc + w4_ref[k:k + 1, :] * _shift_down(xa, xa_h, 3 - k)
        sp = _softplus(-lam_ref[...])
        _, ig, _, a, _, m = _even_gates(xc, wa_ref[0], ba_ref[...], wx_ref[0], bx_ref[...], sp)
        u = m * (ig * xc)
        hs, acum = _scan_fwd(a, u)

        @pl.when(i == 0)
        def _():
            hcar_ref[...] = jnp.zeros_like(hcar_ref)

        hs = hs + acum * hcar_ref[0:1, :]
        hl_ref[...] = hs
        hcar_ref[0:1, :] = hl_ref[ts - 1:ts, :]
        ge, _ = _gelu(ga)
        ya_ref[...] = (hs * ge).astype(BF16)

        sv = cp * vb
        sc = b3_ref[...] + w3_ref[2:3, :] * sv
        for k in range(2):
            sc = sc + w3_ref[k:k + 1, :] * _shift_down(sv, s_h, 2 - k)
        yb_ref[...] = (bp * sc).astype(BF16)

    blk = pl.BlockSpec((ts, 5 * LANES), lambda j, i: (i, j))
    halo = pl.BlockSpec((SUBLANES, 5 * LANES), _halo_map(ts, lambda j: j))
    vec = pl.BlockSpec((1, LANES), lambda j, i: (0, j))
    out = pl.BlockSpec((ts, LANES), lambda j, i: (i, j))
    return pl.pallas_call(
        body, name=name, grid=(nblk, nt),
        in_specs=[blk, halo,
                  pl.BlockSpec((4, LANES), lambda j, i: (0, j)), vec,
                  pl.BlockSpec((1, LANES, LANES), lambda j, i: (j, 0, 0)), vec,
                  pl.BlockSpec((1, LANES, LANES), lambda j, i: (j, 0, 0)), vec, vec,
                  pl.BlockSpec((3, LANES), lambda j, i: (0, j)), vec],
        out_specs=(out, out, out),
        out_shape=(jax.ShapeDtypeStruct((s, 4 * LANES), BF16), jax.ShapeDtypeStruct((s, 4 * LANES), BF16),
                   jax.ShapeDtypeStruct((s, 4 * LANES), F32)),
        scratch_shapes=[pltpu.VMEM((SUBLANES, LANES), F32)],
    )(p, p, w4, b4, wa, ba, wx, bx, lam, w3, b3)


def _even_core_bwd(dy, p, hl, w4, b4, wa, wat, ba, wx, wxt, bx, lam, w3, b3, *, ts=256, name):
    s = p.shape[0]
    ts = _tile_rows(ts, s)
    nt = s // ts
    nblk = 4
    per = ts // SUBLANES

    def body(dya_ref, dyb_ref, p_ref, ph_ref, hl_ref, hh_ref,
             w4_ref, b4_ref, wa_ref, wat_ref, ba_ref, wx_ref, wxt_ref, bx_ref, lam_ref, w3_ref, b3_ref,
             dp_ref, dw4_ref, db4_ref, dwa_ref, dba_ref, dwx_ref, dbx_ref, dlam_ref, dw3_ref, db3_ref,
             dxc_nx, dsc_nx, cg_ref):
        i = pl.program_id(1)
        ti = nt - 1 - i
        first = (ti > 0).astype(F32)
        xa = p_ref[:, 0:LANES]
        ga = p_ref[:, LANES:2 * LANES]
        cp = p_ref[:, 2 * LANES:3 * LANES]
        bp = p_ref[:, 3 * LANES:4 * LANES]
        vb = p_ref[:, 4 * LANES:5 * LANES]
        xa_h = ph_ref[:, 0:LANES] * first
        s_h = ph_ref[:, 2 * LANES:3 * LANES] * ph_ref[:, 4 * LANES:5 * LANES] * first
        h_h = hh_ref[...] * first

        @pl.when(i == 0)
        def _():
            dxc_nx[...] = jnp.zeros_like(dxc_nx)
            dsc_nx[...] = jnp.zeros_like(dsc_nx)
            cg_ref[...] = jnp.zeros_like(cg_ref)
            for ref in (dw4_ref, db4_ref, dwa_ref, dba_ref, dwx_ref, dbx_ref, dlam_ref, dw3_ref, db3_ref):
                ref[...] = jnp.zeros_like(ref)

        xa_sh = [_shift_down(xa, xa_h, 3 - k) for k in range(3)] + [xa]
        xc = b4_ref[...]
        for k in range(4):
            xc = xc + w4_ref[k:k + 1, :] * xa_sh[k]
        lamv = lam_ref[...]
        sp = _softplus(-lamv)
        r, ig, _, a, a2, m = _even_gates(xc, wa_ref[0], ba_ref[...], wx_ref[0], bx_ref[...], sp)
        sv = cp * vb
        sv_sh = [_shift_down(sv, s_h, 2 - k) for k in range(2)] + [sv]
        sc = b3_ref[...]
        for k in range(3):
            sc = sc + w3_ref[k:k + 1, :] * sv_sh[k]
        hs = hl_ref[...]
        h_prev = _shift_down(hs, h_h, 1)

        dya = dya_ref[...]
        dyb = dyb_ref[...]
        ge, gt = _gelu(ga)
        dga = dya * hs * _gelu_grad(ga, gt)
        dh = dya * ge

        ones8 = jnp.ones((SUBLANES, LANES), F32)
        b = _shift_up(a, ones8, 1)
        g, bcum = _scan_rev(b, dh)
        g = g + bcum * cg_ref[0:1, :]
        ag = a * g
        cg_ref[...] = ag[:SUBLANES]

        da = g * h_prev
        xi = ig * xc
        dm = g * xi
        dig = g * m * xc
        dxc = g * m * ig
        dla = da * a - dm * (a2 / m)
        dr = dla * ((-LRU_C) * sp)
        dlam_ref[...] += jnp.sum(dla * r, axis=0, keepdims=True) * (LRU_C * _sigmoid(-lamv))
        dra = dr * r * (1.0 - r)
        dia = dig * ig * (1.0 - ig)
        drab = dra.astype(BF16)
        diab = dia.astype(BF16)
        xcb = xc.astype(BF16)
        dxc = dxc + _dot(drab, wat_ref[0]) + _dot(diab, wxt_ref[0])
        dwa_ref[0] += _dot_tn(xcb, drab)
        dwx_ref[0] += _dot_tn(xcb, diab)
        dba_ref[...] += jnp.sum(dra, axis=0, keepdims=True)
        dbx_ref[...] += jnp.sum(dia, axis=0, keepdims=True)

        nx = dxc_nx[...]
        dxa = w4_ref[3:4, :] * dxc
        for k in range(3):
            dxa = dxa + w4_ref[k:k + 1, :] * _shift_up(dxc, nx, 3 - k)
        for k in range(4):
            dw4_ref[k:k + 1, :] += jnp.sum(dxc * xa_sh[k], axis=0, keepdims=True)
        db4_ref[...] += jnp.sum(dxc, axis=0, keepdims=True)
        dxc_nx[...] = dxc[:SUBLANES]

        dbp = dyb * sc
        dsc = dyb * bp
        nsc = dsc_nx[...]
        ds = w3_ref[2:3, :] * dsc
        for k in range(2):
            ds = ds + w3_ref[k:k + 1, :] * _shift_up(dsc, nsc, 2 - k)
        for k in range(3):
            dw3_ref[k:k + 1, :] += jnp.sum(dsc * sv_sh[k], axis=0, keepdims=True)
        db3_ref[...] += jnp.sum(dsc, axis=0, keepdims=True)
        dsc_nx[...] = dsc[:SUBLANES]

        dp_ref[:, 0:LANES] = dxa.astype(BF16)
        dp_ref[:, LANES:2 * LANES] = dga.astype(BF16)
        dp_ref[:, 2 * LANES:3 * LANES] = (ds * vb).astype(BF16)
        dp_ref[:, 3 * LANES:4 * LANES] = dbp.astype(BF16)
        dp_ref[:, 4 * LANES:5 * LANES] = (ds * cp).astype(BF16)

    def rev(j, i):
        return (nt - 1 - i, j)

    def rev_halo(col):
        def index(j, i):
            return (jnp.maximum((nt - 1 - i) * per - 1, 0), col(j))
        return index

    blk = pl.BlockSpec((ts, 5 * LANES), rev)
    one = pl.BlockSpec((ts, LANES), rev)
    vec = pl.BlockSpec((1, LANES), lambda j, i: (0, j))
    mat = pl.BlockSpec((1, LANES, LANES), lambda j, i: (j, 0, 0))
    w4s = pl.BlockSpec((4, LANES), lambda j, i: (0, j))
    w3s = pl.BlockSpec((3, LANES), lambda j, i: (0, j))
    f = jax.ShapeDtypeStruct
    return pl.pallas_call(
        body, name=name, grid=(nblk, nt),
        in_specs=[one, pl.BlockSpec((ts, LANES), lambda j, i: (nt - 1 - i, 4 + j)),
                  blk, pl.BlockSpec((SUBLANES, 5 * LANES), rev_halo(lambda j: j)),
                  one, pl.BlockSpec((SUBLANES, LANES), rev_halo(lambda j: j)),
                  w4s, vec, mat, mat, vec, mat, mat, vec, vec, w3s, vec],
        out_specs=(blk, w4s, vec, mat, vec, mat, vec, vec, w3s, vec),
        out_shape=(f((s, 20 * LANES), BF16), f((4, 4 * LANES), F32), f((1, 4 * LANES), F32),
                   f((4, LANES, LANES), F32), f((1, 4 * LANES), F32),
                   f((4, LANES, LANES), F32), f((1, 4 * LANES), F32), f((1, 4 * LANES), F32),
                   f((3, 4 * LANES), F32), f((1, 4 * LANES), F32)),
        scratch_shapes=[pltpu.VMEM((SUBLANES, LANES), F32), pltpu.VMEM((SUBLANES, LANES), F32),
                        pltpu.VMEM((SUBLANES, LANES), F32)],
    )(dy, dy, p, p, hl, hl, w4, b4, wa, wat, ba, wx, wxt, bx, lam, w3, b3)


def _ffn_conv(u_ref, uh_ref, w_ref, b_ref, first):
    u = u_ref[...].astype(F32)
    u_h = uh_ref[...].astype(F32)[SUBLANES:] * first
    u_sh = [_shift_down(u, u_h, 2 - k) for k in range(2)] + [u]
    hc = b_ref[...]
    for k in range(3):
        hc = hc + w_ref[k:k + 1, :] * u_sh[k]
    return hc, u_sh


def _ffn_specs(ts, row, halo_row):
    nblk = D_FF // FFN_CB
    specs = []
    for off in (0, nblk):
        specs.append(pl.BlockSpec((ts, FFN_CB), lambda j, i, off=off: (row(i), off + j)))
        specs.append(pl.BlockSpec((16, FFN_CB), lambda j, i, off=off: (halo_row(i), off + j)))
        specs.append(pl.BlockSpec((3, FFN_CB), lambda j, i, off=off: (0, off + j)))
        specs.append(pl.BlockSpec((1, FFN_CB), lambda j, i, off=off: (0, off + j)))
    return specs


def _ffn_core_fwd(up, w, b, *, ts=512, name):
    s = up.shape[0]
    ts = _tile_rows(ts, s)
    nt = s // ts
    nblk = D_FF // FFN_CB
    per = ts // 16

    def body(g_ref, gh_ref, wg_ref, bg_ref, v_ref, vh_ref, wv_ref, bv_ref, act_ref):
        first = (pl.program_id(1) > 0).astype(F32)
        gate, _ = _ffn_conv(g_ref, gh_ref, wg_ref, bg_ref, first)
        val, _ = _ffn_conv(v_ref, vh_ref, wv_ref, bv_ref, first)
        act_ref[...] = (gate * _sigmoid_tanh(gate) * val).astype(BF16)

    return pl.pallas_call(
        body, name=name, grid=(nblk, nt),
        in_specs=_ffn_specs(ts, lambda i: i, lambda i: jnp.maximum(i * per - 1, 0)),
        out_specs=pl.BlockSpec((ts, FFN_CB), lambda j, i: (i, j)),
        out_shape=jax.ShapeDtypeStruct((s, D_FF), BF16),
    )(up, up, w, b, up, up, w, b)


def _ffn_core_bwd(dact, up, w, b, *, ts=512, name):
    s = up.shape[0]
    ts = _tile_rows(ts, s)
    nt = s // ts
    nblk = D_FF // FFN_CB
    per = ts // 16

    strip = 4 * SUBLANES
    halo = 2 * SUBLANES
    nstrips = ts // strip

    def fold(x):
        out = x[:SUBLANES]
        for r0 in range(SUBLANES, strip, SUBLANES):
            out = out + x[r0:r0 + SUBLANES]
        return out

    def body(da_ref, g_ref, gh_ref, wg_ref, bg_ref, v_ref, vh_ref, wv_ref, bv_ref,
             dg_ref, dv_ref, dwg_ref, dwv_ref, dbg_ref, dbv_ref, nxg_ref, nxv_ref, ug_ref, uv_ref):
        i = pl.program_id(1)
        first = nt - 1 - i > 0

        @pl.when(i == 0)
        def _():
            for ref in (nxg_ref, nxv_ref, dwg_ref, dwv_ref, dbg_ref, dbv_ref):
                ref[...] = jnp.zeros_like(ref)

        for u_ref, uh_ref, dst in ((g_ref, gh_ref, ug_ref), (v_ref, vh_ref, uv_ref)):
            dst[0:halo, :] = jnp.where(first, uh_ref[...], jnp.zeros_like(uh_ref))
            dst[halo:, :] = u_ref[...]
        wg, wv, bg, bv = wg_ref[...], wv_ref[...], bg_ref[...], bv_ref[...]

        def conv(u_ref, r, w, b):
            win = u_ref[pl.ds(r, halo + strip), :].astype(F32)
            cur, before = win[halo:], win[SUBLANES:halo]
            sh = [_shift_down(cur, before, 2 - k) for k in range(2)] + [cur]
            return b + w[0:1] * sh[0] + w[1:2] * sh[1] + w[2:3] * sh[2], sh

        def conv_t(d, nxt, w):
            out = w[2:3] * d
            for k in range(2):
                out = out + w[k:k + 1] * _shift_up(d, nxt, 2 - k)
            return out

        def step(t, carry):
            nxg, nxv, awg, awv, abg, abv = carry
            r = pl.multiple_of((nstrips - 1 - t) * strip, strip)
            gate, g_sh = conv(ug_ref, r, wg, bg)
            val, v_sh = conv(uv_ref, r, wv, bv)
            da = da_ref[pl.ds(r, strip), :].astype(F32)
            sg = _sigmoid_tanh(gate)
            dgate = da * val * (sg * (1.0 + gate * (1.0 - sg)))
            dval = da * (gate * sg)
            dg_ref[pl.ds(r, strip), :] = conv_t(dgate, nxg, wg).astype(BF16)
            dv_ref[pl.ds(r, strip), :] = conv_t(dval, nxv, wv).astype(BF16)
            awg = tuple(a + fold(dgate * sh) for a, sh in zip(awg, g_sh))
            awv = tuple(a + fold(dval * sh) for a, sh in zip(awv, v_sh))
            return dgate[:SUBLANES], dval[:SUBLANES], awg, awv, abg + fold(dgate), abv + fold(dval)

        zero = jnp.zeros((SUBLANES, FFN_CB), F32)
        init = (nxg_ref[...], nxv_ref[...], (zero,) * 3, (zero,) * 3, zero, zero)
        nxg, nxv, awg, awv, abg, abv = lax.fori_loop(0, nstrips, step, init)
        nxg_ref[...] = nxg
        nxv_ref[...] = nxv
        for k in range(3):
            dwg_ref[k:k + 1, :] += jnp.sum(awg[k], axis=0, keepdims=True)
            dwv_ref[k:k + 1, :] += jnp.sum(awv[k], axis=0, keepdims=True)
        dbg_ref[...] += jnp.sum(abg, axis=0, keepdims=True)
        dbv_ref[...] += jnp.sum(abv, axis=0, keepdims=True)

    def rev(i):
        return nt - 1 - i

    tile = pl.BlockSpec((ts, FFN_CB), lambda j, i: (rev(i), j))
    w_out = pl.BlockSpec((3, FFN_CB), lambda j, i: (0, j))
    b_out = pl.BlockSpec((1, FFN_CB), lambda j, i: (0, j))
    f = jax.ShapeDtypeStruct
    return pl.pallas_call(
        body, name=name, grid=(nblk, nt),
        in_specs=[tile] + _ffn_specs(ts, rev, lambda i: jnp.maximum(rev(i) * per - 1, 0)),
        out_specs=(tile, tile, w_out, w_out, b_out, b_out),
        out_shape=(f((s, D_FF), BF16), f((s, D_FF), BF16), f((3, D_FF), F32), f((3, D_FF), F32),
                   f((1, D_FF), F32), f((1, D_FF), F32)),
        scratch_shapes=[pltpu.VMEM((SUBLANES, FFN_CB), F32), pltpu.VMEM((SUBLANES, FFN_CB), F32),
                        pltpu.VMEM((ts + halo, FFN_CB), BF16), pltpu.VMEM((ts + halo, FFN_CB), BF16)],
    )(dact, up, up, w, b, up, up, w, b)


def _sgu_forward_block(zu, zg, gn, w_ref, bias, seg):
    u, tu = _gelu(zu)
    g, tg = _gelu(zg)
    ms = _dot_split(g * g, seg)
    rs = lax.rsqrt(ms + EPS)
    ghat = g * rs
    gv = ghat * gn
    gvb = gv.astype(BF16)
    lane = _lanes((CHUNK, LANES))
    chunks = []
    for c in range(zu.shape[0] // CHUNK):
        gc = gvb[c * CHUNK:(c + 1) * CHUNK]
        mix = jnp.where(lane < 64, _dot(w_ref[0], gc), _dot(w_ref[1], gc)) + bias
        chunks.append(mix)
    mixed = chunks[0] if len(chunks) == 1 else jnp.concatenate(chunks, axis=0)
    return u, tu, g, tg, rs, ghat, gvb, mixed


def _sgu_fwd(p1, gn, w, bias, seg, *, ts=512, name):
    s = p1.shape[0]
    ts = _tile_rows(ts, s)

    def body(zu_ref, zg_ref, gn_ref, w_ref, bias_ref, seg_ref, yc_ref):
        u, _, _, _, _, _, _, mixed = _sgu_forward_block(
            zu_ref[...], zg_ref[...], gn_ref[...], w_ref, bias_ref[...], seg_ref[...])
        yc_ref[...] = (u * mixed).astype(BF16)

    return pl.pallas_call(
        body, name=name, grid=(4, s // ts),
        in_specs=[pl.BlockSpec((ts, LANES), lambda j, i: (i, j)),
                  pl.BlockSpec((ts, LANES), lambda j, i: (i, 4 + j)),
                  pl.BlockSpec((1, LANES), lambda j, i: (0, j)),
                  pl.BlockSpec((2, CHUNK, CHUNK), lambda j, i: (j, 0, 0)),
                  pl.BlockSpec((CHUNK, LANES), lambda j, i: (0, j)),
                  pl.BlockSpec((LANES, LANES), lambda j, i: (0, 0))],
        out_specs=pl.BlockSpec((ts, LANES), lambda j, i: (i, j)),
        out_shape=jax.ShapeDtypeStruct((s, 4 * LANES), BF16),
    )(p1, p1, gn, w, bias, seg)


def _sgu_bwd(p1, dy, gn, w, wt, bias, seg, tril, *, ts=512, name):
    s = p1.shape[0]
    ts = _tile_rows(ts, s)
    nt = s // ts

    def body(zu_ref, zg_ref, dy_ref, gn_ref, w_ref, wt_ref, bias_ref, seg_ref, tril_ref,
             dzu_ref, dzg_ref, dw_ref, dbias_ref, dgn_ref):
        i = pl.program_id(1)
        zu = zu_ref[...]
        zg = zg_ref[...]
        gn_v = gn_ref[...]
        segv = seg_ref[...]
        u, tu, g, tg, rs, ghat, gvb, mixed = _sgu_forward_block(zu, zg, gn_v, w_ref, bias_ref[...], segv)
        dyv = dy_ref[...]
        du = dyv * mixed
        dmx = dyv * u

        @pl.when(i == 0)
        def _():
            dw_ref[...] = jnp.zeros_like(dw_ref)
            dbias_ref[...] = jnp.zeros_like(dbias_ref)
            dgn_ref[...] = jnp.zeros_like(dgn_ref)

        lane = _lanes((CHUNK, LANES))
        dgv_chunks = []
        dbias = jnp.zeros((CHUNK, LANES), F32)
        for c in range(ts // CHUNK):
            dmc = dmx[c * CHUNK:(c + 1) * CHUNK]
            gc = gvb[c * CHUNK:(c + 1) * CHUNK]
            dm_a = jnp.where(lane < 64, dmc, 0.0).astype(BF16)
            dm_b = jnp.where(lane >= 64, dmc, 0.0).astype(BF16)
            dw_ref[0] += _dot_nt(dm_a, gc)
            dw_ref[1] += _dot_nt(dm_b, gc)
            dgv_chunks.append(_dot(wt_ref[0], dm_a) + _dot(wt_ref[1], dm_b))
            dbias = dbias + dmc
        dbias_ref[...] += dbias
        dgv = dgv_chunks[0] if len(dgv_chunks) == 1 else jnp.concatenate(dgv_chunks, axis=0)
        dgn_ref[...] += jnp.sum(dgv * ghat, axis=0, keepdims=True)
        dgh = dgv * gn_v
        dg = rs * (dgh - ghat * _dot_split(dgh * ghat, segv))
        dzu_ref[...] = (du * _gelu_grad(zu, tu)).astype(BF16)
        dzg_ref[...] = (dg * _gelu_grad(zg, tg)).astype(BF16)

        @pl.when(i == nt - 1)
        def _():
            dw_ref[0] = dw_ref[0] * tril_ref[...]
            dw_ref[1] = dw_ref[1] * tril_ref[...]

    f = jax.ShapeDtypeStruct
    colj = pl.BlockSpec((ts, LANES), lambda j, i: (i, j))
    wsp = pl.BlockSpec((2, CHUNK, CHUNK), lambda j, i: (j, 0, 0))
    sq = pl.BlockSpec((LANES, LANES), lambda j, i: (0, 0))
    return pl.pallas_call(
        body, name=name, grid=(4, nt),
        in_specs=[colj, pl.BlockSpec((ts, LANES), lambda j, i: (i, 4 + j)), colj,
                  pl.BlockSpec((1, LANES), lambda j, i: (0, j)), wsp, wsp,
                  pl.BlockSpec((CHUNK, LANES), lambda j, i: (0, j)), sq, sq],
        out_specs=(colj, colj, wsp, pl.BlockSpec((CHUNK, LANES), lambda j, i: (0, j)),
                   pl.BlockSpec((1, LANES), lambda j, i: (0, j))),
        out_shape=(f((s, 4 * LANES), BF16), f((s, 4 * LANES), BF16), f((8, CHUNK, CHUNK), F32),
                   f((CHUNK, 4 * LANES), F32), f((1, 4 * LANES), F32)),
    )(p1, p1, dy, gn, w, wt, bias, seg, tril)


F_COL = 20


def _fcum_fwd(p1, bf, *, ts=512, name):
    s = p1.shape[0]
    ts = _tile_rows(ts, s)

    def body(f_ref, bf_ref, c_ref, car_ref):
        i = pl.program_id(0)
        z = f_ref[...] + bf_ref[...]
        logf = jnp.minimum(z, 0.0) - _log1p_pos(jnp.exp(-jnp.abs(z)))

        @pl.when(i == 0)
        def _():
            car_ref[...] = jnp.zeros_like(car_ref)

        c_ref[...] = _cumsum_fwd(logf) + car_ref[0:1, :]
        car_ref[0:1, :] = c_ref[ts - 1:ts, :]

    return pl.pallas_call(
        body, name=name, grid=(s // ts,),
        in_specs=[pl.BlockSpec((ts, LANES), lambda i: (i, F_COL)), pl.BlockSpec((1, LANES), lambda i: (0, 0))],
        out_specs=pl.BlockSpec((ts, LANES), lambda i: (i, 0)),
        out_shape=jax.ShapeDtypeStruct((s, LANES), F32),
        scratch_shapes=[pltpu.VMEM((SUBLANES, LANES), F32)],
    )(p1, bf)


def _fcum_bwd(dcs, dcq, p1, bf, *, ts=512, name):
    s = p1.shape[0]
    ts = _tile_rows(ts, s)
    nt = s // ts

    def body(dc_ref, dcq_ref, f_ref, bf_ref, df_ref, dbf_ref, car_ref):
        i = pl.program_id(0)

        @pl.when(i == 0)
        def _():
            car_ref[...] = jnp.zeros_like(car_ref)
            dbf_ref[...] = jnp.zeros_like(dbf_ref)

        dc = dc_ref[...]
        lane = _lanes((ts, LANES))
        for h in range(8):
            dc = dc + jnp.where(lane == h, dcq_ref[:, h * LANES:(h + 1) * LANES], 0.0)
        dlog = _cumsum_rev(dc) + car_ref[0:1, :]
        car_ref[...] = dlog[:SUBLANES]
        z = f_ref[...] + bf_ref[...]
        df = dlog * _sigmoid(-z)
        df_ref[...] = df.astype(BF16)
        dbf_ref[...] += jnp.sum(df, axis=0, keepdims=True)

    return pl.pallas_call(
        body, name=name, grid=(nt,),
        in_specs=[pl.BlockSpec((ts, LANES), lambda i: (nt - 1 - i, 0)),
                  pl.BlockSpec((ts, 8 * LANES), lambda i: (nt - 1 - i, 0)),
                  pl.BlockSpec((ts, LANES), lambda i: (nt - 1 - i, F_COL)),
                  pl.BlockSpec((1, LANES), lambda i: (0, 0))],
        out_specs=(pl.BlockSpec((ts, LANES), lambda i: (nt - 1 - i, 0)), pl.BlockSpec((1, LANES), lambda i: (0, 0))),
        out_shape=(jax.ShapeDtypeStruct((s, LANES), BF16), jax.ShapeDtypeStruct((1, LANES), F32)),
        scratch_shapes=[pltpu.VMEM((SUBLANES, LANES), F32)],
    )(dcs, dcq, p1, bf)


def _fox_scores(qm, kb, bias, ck, diagonal):
    sc = _dot_nt(qm, kb) + bias - ck
    if diagonal:
        sc = jnp.where(_lanes(sc.shape) <= _rows(sc.shape), sc, NEG)
    return sc


def _head_masks(shape):
    lane = _lanes(shape)
    return lane < 64, lane >= 64


def _fox_fwd(p1, cq, ck, *, tq=512, name):
    s = p1.shape[0]
    tq = _tile_rows(tq, s)
    tk = tq
    nq = s // tq

    def body(q_ref, k_ref, v_ref, cq_ref, ck_ref, o_ref, lb_ref):
        qi = pl.program_id(1)
        q = q_ref[...] * 0.125
        first, second = _head_masks((tq, LANES))
        qms = [jnp.where(sel, q, 0.0).astype(BF16) for sel in (first, second)]
        cqs = [cq_ref[:, hh * LANES:(hh + 1) * LANES] for hh in range(2)]
        biases = [jnp.tile(cqh, (1, tk // LANES)) for cqh in cqs]

        def step(kj, carry, diagonal):
            cols = pl.ds(pl.multiple_of(kj * tk, tk), tk)
            kb = k_ref[cols, :].astype(BF16)
            vb = v_ref[cols, :].astype(BF16)
            new, outs = [], []
            acc = carry[4]
            for hh in range(2):
                m_prev, l_prev = carry[2 * hh], carry[2 * hh + 1]
                sc = _fox_scores(qms[hh], kb, biases[hh], ck_ref[hh, :, cols], diagonal)
                m_new = jnp.maximum(m_prev, jnp.max(sc, axis=1, keepdims=True))
                pm = jnp.exp(sc - jnp.tile(m_new, (1, tk // LANES)))
                alpha = jnp.exp(m_prev - m_new)
                new += [m_new, alpha * l_prev + jnp.sum(pm, axis=1, keepdims=True)]
                outs.append(acc * alpha + _dot(pm.astype(BF16), vb))
            return tuple(new) + (jnp.where(first, outs[0], outs[1]),)

        zero = jnp.zeros((tq, LANES), F32)
        low = jnp.full((tq, LANES), NEG, F32)
        carry = lax.fori_loop(0, qi, lambda kj, c: step(kj, c, False), (low, zero, low, zero, zero))
        m0, l0, m1, l1, acc = step(qi, carry, True)
        o_ref[...] = (acc / jnp.where(first, l0, l1)).astype(BF16)
        lb_ref[:, 0:LANES] = cqs[0] - (m0 + jnp.log(l0))
        lb_ref[:, LANES:2 * LANES] = cqs[1] - (m1 + jnp.log(l1))

    return pl.pallas_call(
        body, name=name, grid=(4, nq),
        in_specs=[pl.BlockSpec((tq, LANES), lambda j, qi: (qi, 8 + j)),
                  pl.BlockSpec((s, LANES), lambda j, qi: (0, 12 + j)),
                  pl.BlockSpec((s, LANES), lambda j, qi: (0, 16 + j)),
                  pl.BlockSpec((tq, 2 * LANES), lambda j, qi: (qi, j)),
                  pl.BlockSpec((2, 1, s), lambda j, qi: (j, 0, 0))],
        out_specs=(pl.BlockSpec((tq, LANES), lambda j, qi: (qi, j)),
                   pl.BlockSpec((tq, 2 * LANES), lambda j, qi: (qi, j))),
        out_shape=(jax.ShapeDtypeStruct((s, 4 * LANES), BF16), jax.ShapeDtypeStruct((s, 8 * LANES), F32)),
    )(p1, p1, p1, cq, ck)


def _fox_delta(dy, o, sel, *, ts=512, name):
    s = o.shape[0]
    ts = _tile_rows(ts, s)

    def body(do_ref, o_ref, sel_ref, d_ref):
        prod = do_ref[...] * o_ref[...].astype(F32)
        d_ref[:, 0:LANES] = _dot_split(prod, sel_ref[0])
        d_ref[:, LANES:2 * LANES] = _dot_split(prod, sel_ref[1])

    return pl.pallas_call(
        body, name=name, grid=(4, s // ts),
        in_specs=[pl.BlockSpec((ts, LANES), lambda j, i: (i, 4 + j)),
                  pl.BlockSpec((ts, LANES), lambda j, i: (i, j)),
                  pl.BlockSpec((2, LANES, LANES), lambda j, i: (0, 0, 0))],
        out_specs=pl.BlockSpec((ts, 2 * LANES), lambda j, i: (i, j)),
        out_shape=jax.ShapeDtypeStruct((s, 8 * LANES), F32),
    )(dy, o, sel)


def _fox_bwd(p1, dy, lb, delta, ck, *, tq=512, name):
    s = p1.shape[0]
    tq = _tile_rows(tq, s)
    tk = tq
    nq = s // tq

    def body(q_ref, k_ref, v_ref, do_ref, lb_ref, dl_ref, ck_ref,
             dq_ref, dk_ref, dv_ref, dck_ref, dcq_ref, dqa_ref, dra_ref):
        kj = pl.program_id(1)

        @pl.when(kj == 0)
        def _():
            dqa_ref[...] = jnp.zeros_like(dqa_ref)
            dra_ref[...] = jnp.zeros_like(dra_ref)

        kf = k_ref[...]
        kb = kf.astype(BF16)
        vb = v_ref[...].astype(BF16)
        first, second = _head_masks((tk, LANES))
        kms = [jnp.where(sel, kf, 0.0).astype(BF16) for sel in (first, second)]
        cks = [ck_ref[hh] for hh in range(2)]

        def step(qi, carry, diagonal):
            dk_acc, dv_acc, dc0, dc1 = carry
            dcs = [dc0, dc1]
            rows = pl.ds(pl.multiple_of(qi * tq, tq), tq)
            q = q_ref[rows, :] * 0.125
            do = do_ref[rows, :]
            for hh, sel in enumerate((first, second)):
                qm = jnp.where(sel, q, 0.0).astype(BF16)
                dom = jnp.where(sel, do, 0.0).astype(BF16)
                bias = jnp.tile(lb_ref[rows, hh * LANES:(hh + 1) * LANES], (1, tk // LANES))
                pm = jnp.exp(_fox_scores(qm, kb, bias, cks[hh], diagonal))
                dv_acc = dv_acc + _dot_tn(pm.astype(BF16), dom)
                dp = _dot_nt(dom, vb)
                ds = pm * (dp - jnp.tile(dl_ref[rows, hh * LANES:(hh + 1) * LANES], (1, tk // LANES)))
                dsb = ds.astype(BF16)
                dk_acc = dk_acc + _dot_tn(dsb, qm)
                dcs[hh] = dcs[hh] - jnp.sum(ds, axis=0, keepdims=True)
                dqa_ref[rows, :] += _dot(dsb, kms[hh])
                dra_ref[hh, rows, :] += jnp.sum(ds, axis=1, keepdims=True)
            return dk_acc, dv_acc, dcs[0], dcs[1]

        zero = jnp.zeros((tk, LANES), F32)
        zrow = jnp.zeros((1, tk), F32)
        carry = step(kj, (zero, zero, zrow, zrow), True)
        dk_acc, dv_acc, dc0, dc1 = lax.fori_loop(kj + 1, nq, lambda qi, c: step(qi, c, False), carry)
        dk_ref[...] = dk_acc.astype(BF16)
        dv_ref[...] = dv_acc.astype(BF16)
        dck_ref[0] = dc0
        dck_ref[1] = dc1

        @pl.when(kj == nq - 1)
        def _():
            dq_ref[...] = (dqa_ref[...] * 0.125).astype(BF16)
            dcq_ref[:, 0:LANES] = dra_ref[0]
            dcq_ref[:, LANES:2 * LANES] = dra_ref[1]

    def full(width, col0):
        return pl.BlockSpec((s, width), lambda j, kj: (0, col0 + j))

    kblk = pl.BlockSpec((tk, LANES), lambda j, kj: (kj, j))
    f = jax.ShapeDtypeStruct
    return pl.pallas_call(
        body, name=name, grid=(4, nq),
        in_specs=[full(LANES, 8),
                  pl.BlockSpec((tk, LANES), lambda j, kj: (kj, 12 + j)),
                  pl.BlockSpec((tk, LANES), lambda j, kj: (kj, 16 + j)),
                  full(LANES, 4), full(2 * LANES, 0), full(2 * LANES, 0),
                  pl.BlockSpec((2, 1, tk), lambda j, kj: (j, 0, kj))],
        out_specs=(full(LANES, 0), kblk, kblk, pl.BlockSpec((2, 1, tk), lambda j, kj: (j, 0, kj)),
                   full(2 * LANES, 0)),
        out_shape=(f((s, 4 * LANES), BF16), f((s, 4 * LANES), BF16), f((s, 4 * LANES), BF16),
                   f((8, 1, s), F32), f((s, 8 * LANES), F32)),
        scratch_shapes=[pltpu.VMEM((s, LANES), F32), pltpu.VMEM((2, s, LANES), F32)],
    )(p1, p1, p1, dy, lb, delta, ck)


def _row_block(r, cap=256):
    best = None
    for rb in range(2 * SUBLANES, min(r, cap) + 1, 2 * SUBLANES):
        if r % rb == 0:
            best = rb
    return r if best is None else best


def _adamw(w, g, m, v, *, name):
    r, c = w.shape
    rb = _row_block(r)

    def body(w_ref, g_ref, m_ref, v_ref, d_ref, nm_ref, nv_ref):
        gv = g_ref[...]
        mn = ADAM_B1 * m_ref[...] + (1.0 - ADAM_B1) * gv
        vn = ADAM_B2 * v_ref[...] + (1.0 - ADAM_B2) * (gv * gv)
        m_hat = mn / ADAM_C1
        v_hat = vn / ADAM_C2
        d_ref[...] = (-ADAM_LR) * (m_hat / (jnp.sqrt(v_hat) + ADAM_EPS) + ADAM_WD * w_ref[...])
        nm_ref[...] = mn
        nv_ref[...] = vn

    blk = pl.BlockSpec((rb, c), lambda i: (i, 0))
    shp = jax.ShapeDtypeStruct((r, c), F32)
    return pl.pallas_call(
        body, name=name, grid=(r // rb,), in_specs=[blk] * 4, out_specs=(blk,) * 3, out_shape=(shp,) * 3,
    )(w, g, m, v)


def _adamw_halves(w, mine, theirs, m, v, core, *, name):
    layers, r, c = w.shape
    rh = r // 2
    rb = _row_block(rh)
    per = rh // rb

    def body(core_ref, w_ref, *refs):
        g_refs = refs[:2 * layers]
        m_ref, v_ref, g_ref, d_ref, nm_ref, nv_ref = refs[2 * layers:]
        own = pl.program_id(1) == core_ref[0]
        gv = jnp.where(own, g_refs[0][...], g_refs[layers][...])
        for l in range(1, layers):
            gv = jnp.where(pl.program_id(0) == l, jnp.where(own, g_refs[l][...], g_refs[layers + l][...]), gv)
        g_ref[...] = gv
        mn = ADAM_B1 * m_ref[...] + (1.0 - ADAM_B1) * gv
        vn = ADAM_B2 * v_ref[...] + (1.0 - ADAM_B2) * (gv * gv)
        m_hat = mn / ADAM_C1
        v_hat = vn / ADAM_C2
        d_ref[...] = (-ADAM_LR) * (m_hat / (jnp.sqrt(v_hat) + ADAM_EPS) + ADAM_WD * w_ref[...])
        nm_ref[...] = mn
        nv_ref[...] = vn

    full = pl.BlockSpec((None, rb, c), lambda l, h, i, core_ref: (l, h * per + i, 0))
    half = pl.BlockSpec((rb, c), lambda l, h, i, core_ref: (i, 0))
    shp = jax.ShapeDtypeStruct((layers, r, c), F32)
    return pl.pallas_call(
        body, name=name,
        grid_spec=pltpu.PrefetchScalarGridSpec(
            num_scalar_prefetch=1, grid=(layers, 2, per),
            in_specs=[full] + [half] * (2 * layers) + [full, full], out_specs=(full,) * 4),
        out_shape=(shp,) * 4,
    )(core, w, *mine, *theirs, m, v)


def _pair_specs(col, rb, c):
    if col:
        g_spec = pl.BlockSpec((None, rb, c), lambda t, i, sel: (sel[0], i, sel[1 + t]))
    else:
        g_spec = pl.BlockSpec((None, None, rb, c), lambda t, i, sel: (sel[1 + t], sel[0], i, 0))
    return g_spec, pl.BlockSpec((None, rb, c), lambda t, i, sel: (sel[1 + t], i, 0))


def _pair_sum(g, col, ra, sel, after, *, name):
    _, rh, c = ra.shape
    rb = _row_block(rh)

    def body(sel_ref, g_ref, ra_ref, after_ref, h16_ref):
        h16_ref[...] = (g_ref[...] + ra_ref[...]).astype(BF16)

    g_spec, ra_spec = _pair_specs(col, rb, c)
    return pl.pallas_call(
        body, name=name,
        grid_spec=pltpu.PrefetchScalarGridSpec(
            num_scalar_prefetch=1, grid=(2, rh // rb), in_specs=[g_spec, ra_spec, ANY],
            out_specs=pl.BlockSpec((None, rb, c), lambda t, i, sel: (t, i, 0))),
        out_shape=jax.ShapeDtypeStruct((2, rh, c), BF16),
    )(sel, g, ra, after)


def _first_sum(g, col, ra, r1, sel, after, *, name):
    _, rh, c = ra.shape
    rb = _row_block(rh)

    def body(sel_ref, g_ref, ra_ref, r_ref, after_ref, s_ref, s16_ref):
        tot = (g_ref[...] + ra_ref[...]) + r_ref[...].astype(F32)
        s_ref[...] = tot
        s16_ref[...] = tot.astype(BF16)

    g_spec, ra_spec = _pair_specs(col, rb, c)
    slot = pl.BlockSpec((None, rb, c), lambda t, i, sel_ref: (t, i, 0))
    return pl.pallas_call(
        body, name=name,
        grid_spec=pltpu.PrefetchScalarGridSpec(
            num_scalar_prefetch=1, grid=(2, rh // rb), in_specs=[g_spec, ra_spec, slot, ANY],
            out_specs=(slot, slot)),
        out_shape=(jax.ShapeDtypeStruct((2, rh, c), F32), jax.ShapeDtypeStruct((2, rh, c), BF16)),
    )(sel, g, ra, r1, after)


def _second_sum(s1, r2, mine, after, *, name):
    _, rh, c = s1.shape
    rb = _row_block(rh)

    def body(mine_ref, s_ref, r_ref, after_ref, t_ref):
        t_ref[...] = s_ref[...] + r_ref[...].astype(F32)

    flat = pl.BlockSpec((rb, c), lambda i, mine_ref: (i, 0))
    return pl.pallas_call(
        body, name=name,
        grid_spec=pltpu.PrefetchScalarGridSpec(
            num_scalar_prefetch=1, grid=(rh // rb,),
            in_specs=[pl.BlockSpec((None, rb, c), lambda i, mine_ref: (mine_ref[0], i, 0)), flat, ANY],
            out_specs=flat),
        out_shape=jax.ShapeDtypeStruct((rh, c), F32),
    )(mine, s1, r2, after)


def _place(shard, col, chip, dtype, *, name):
    r, c = shard.shape
    rh = r // 2
    rb = _row_block(rh)

    def body(chip_ref, s_ref, o_ref):
        o_ref[...] = s_ref[...].astype(o_ref.dtype)

    if col:
        out_spec = pl.BlockSpec((None, rb, c), lambda h, i, chip_ref: (h, i, chip_ref[0]))
        shape = (2, rh, N_CHIPS * c)
    else:
        out_spec = pl.BlockSpec((None, None, rb, c), lambda h, i, chip_ref: (chip_ref[0], h, i, 0))
        shape = (N_CHIPS, 2, rh, c)
    per = rh // rb
    return pl.pallas_call(
        body, name=name,
        grid_spec=pltpu.PrefetchScalarGridSpec(
            num_scalar_prefetch=1, grid=(2, per),
            in_specs=[pl.BlockSpec((rb, c), lambda h, i, chip_ref: (h * per + i, 0))], out_specs=out_spec),
        out_shape=jax.ShapeDtypeStruct(shape, dtype),
    )(chip, shard)


ANY = pl.BlockSpec(memory_space=pl.ANY)


def _mesh_pos():
    return lax.axis_index("x"), lax.axis_index("y"), lax.axis_index("c")


def _other_chips(x, y):
    return [(1 - x, y), (x, 1 - y), (1 - x, 1 - y)]


def _remote(src, dst, ssem, rsem, dev):
    return pltpu.make_async_remote_copy(src_ref=src, dst_ref=dst, send_sem=ssem, recv_sem=rsem,
                                        device_id=dev, device_id_type=MESH)


def _flip(a, b):
    return a + b - 2 * a * b


def _handshake(peers):
    barrier = pltpu.get_barrier_semaphore()
    for peer in peers:
        pl.semaphore_signal(barrier, inc=1, device_id=peer, device_id_type=MESH)
    pl.semaphore_wait(barrier, len(peers))


def _slab(ref, col, width, k, h):
    if not col:
        return ref.at[k, h]
    start = k * width if isinstance(k, int) else pl.multiple_of(k * width, LANES)
    return ref.at[h, :, pl.ds(start, width)]


def _all_gather(bufs, cols, *, collective_id, name):
    n = len(bufs)
    widths = [b.shape[2] // N_CHIPS if col else b.shape[3] for b, col in zip(bufs, cols)]
    outs = [jax.new_ref(b, memory_space=pltpu.MemorySpace.HBM) for b in bufs]

    def body(ssem, rsem):
        x, y, c = _mesh_pos()
        me = 2 * x + y
        sib = (x, y, 1 - c)
        n1 = (_flip(x, 1 - c), _flip(y, c))
        n2 = (_flip(x, c), _flip(y, 1 - c))
        k1 = 2 * n1[0] + n1[1]
        k2 = 2 * n2[0] + n2[1]
        kd = 2 * (1 - x) + (1 - y)
        _handshake([n1 + (c,), n2 + (c,), sib])

        def slab(a, k, h):
            return _slab(outs[a], cols[a], widths[a], k, h)

        def copy(a, j, src, dst, dev):
            return _remote(src, dst, ssem.at[a, j], rsem.at[a, j], dev)

        sends = []
        for a in range(n):
            for j, nb in ((0, n1), (1, n2)):
                own = slab(a, me, c)
                cp = copy(a, j, own, own, nb + (c,))
                cp.start()
                sends.append(cp)
        arrivals = ((0, k1, n1, 3), (1, k2, n2, 4), (2, kd, n2, 5))
        for j, k, nb, fwd in arrivals:
            for a in range(n):
                got = slab(a, k, c)
                copy(a, j, got, got, nb + (c,)).wait_recv()
                if j == 0:
                    cp = copy(a, 2, got, got, n2 + (c,))
                    cp.start()
                    sends.append(cp)
                cp = copy(a, fwd, got, got, sib)
                cp.start()
                sends.append(cp)
        for fwd, k in ((3, k2), (4, k1), (5, kd)):
            for a in range(n):
                got = slab(a, k, 1 - c)
                copy(a, fwd, got, got, sib).wait_recv()
        for cp in sends:
            cp.wait_send()

    _sequencer_call(body, (), [(n, 6), (n, 6)], collective_id, name)()
    return [ref[...] for ref in outs]


def _sequencer_call(body, out_types, sem_shapes, collective_id, name):
    return pl.kernel(
        body, name=name, out_type=out_types,
        mesh=plsc.ScalarSubcoreMesh(axis_name="sequencer", num_cores=1),
        scratch_types=[pltpu.SemaphoreType.DMA(shape) for shape in sem_shapes],
        compiler_params=pltpu.CompilerParams(collective_id=collective_id))


def _send_other_half(grads, cols, *, collective_id, name):
    n = len(grads)

    def shard_shape(g, col):
        if col:
            return (g.shape[1], g.shape[2] // N_CHIPS)
        return g.shape[2:]

    shapes = [shard_shape(g, col) for g, col in zip(grads, cols)]

    def body(*refs):
        ins, outs = refs[:n], refs[n:2 * n]
        ssem, rsem = refs[2 * n:]
        x, y, c = _mesh_pos()
        sib = (x, y, 1 - c)
        _handshake([sib])
        sends = []
        for a in range(n):
            for k in range(N_CHIPS):
                src = _slab(ins[a], cols[a], shapes[a][1], k, 1 - c)
                cp = _remote(src, outs[a].at[k], ssem.at[a, k], rsem.at[a, k], sib)
                cp.start()
                sends.append(cp)
        for cp in sends:
            cp.wait()

    out_types = [jax.ShapeDtypeStruct((N_CHIPS,) + shp, g.dtype) for g, shp in zip(grads, shapes)]
    return _sequencer_call(body, out_types, [(n, N_CHIPS), (n, N_CHIPS)], collective_id, name)(*grads)


def _send_first(sums, *, collective_id, name):
    n = len(sums)

    def body(*refs):
        ins, outs = refs[:n], refs[n:2 * n]
        ssem, rsem = refs[2 * n:]
        x, y, c = _mesh_pos()
        nb = (_flip(x, c), _flip(y, 1 - c), c)
        _handshake([nb])
        sends = []
        for a in range(n):
            for t in range(2):
                cp = _remote(ins[a].at[t], outs[a].at[t], ssem.at[a, t], rsem.at[a, t], nb)
                cp.start()
                sends.append(cp)
        for cp in sends:
            cp.wait()

    out_types = [jax.ShapeDtypeStruct(h.shape, h.dtype) for h in sums]
    return _sequencer_call(body, out_types, [(n, 2), (n, 2)], collective_id, name)(*sums)


def _send_second(sums, *, collective_id, name):
    n = len(sums)

    def body(*refs):
        ins, outs = refs[:n], refs[n:2 * n]
        ssem, rsem = refs[2 * n:]
        x, y, c = _mesh_pos()
        nb = (_flip(x, 1 - c), _flip(y, c), c)
        other = 1 - (c * y + (1 - c) * x)
        _handshake([nb])
        sends = []
        for a in range(n):
            cp = _remote(ins[a].at[other], outs[a], ssem.at[a], rsem.at[a], nb)
            cp.start()
            sends.append(cp)
        for cp in sends:
            cp.wait()

    out_types = [jax.ShapeDtypeStruct(s.shape[1:], s.dtype) for s in sums]
    return _sequencer_call(body, out_types, [(n,), (n,)], collective_id, name)(*sums)


def _swap_halves(halves, *, collective_id, name):
    n = len(halves)

    def body(*refs):
        ins, outs = refs[:n], refs[n:2 * n]
        ssem, rsem = refs[2 * n:]
        x, y, c = _mesh_pos()
        sib = (x, y, 1 - c)
        _handshake([sib])
        cps = []
        for a in range(n):
            cp = _remote(ins[a], outs[a], ssem.at[a], rsem.at[a], sib)
            cp.start()
            cps.append(cp)
        for cp in cps:
            cp.wait()

    out_types = [jax.ShapeDtypeStruct(h.shape, h.dtype) for h in halves]
    return _sequencer_call(body, out_types, [(n,), (n,)], collective_id, name)(*halves)


def _all_reduce_small(buf, *, name):
    r = buf.shape[0]
    rh = r // 2

    def body(in_ref, out_ref, x1_ref, x2_ref, ssem, rsem):
        x, y, c = _mesh_pos()
        me = 2 * x + y
        sib = (x, y, 1 - c)
        chips = _other_chips(x, y)
        cp = _remote(in_ref, x1_ref, ssem.at[0], rsem.at[0], sib)
        cp.start()
        cp.wait()
        off = pl.multiple_of(c * rh, SUBLANES)
        x2_ref[me] = in_ref[pl.ds(off, rh), :] + x1_ref[pl.ds(off, rh), :]
        sends = []
        for j, (cx, cy) in enumerate(chips):
            s = _remote(x2_ref.at[me], x2_ref.at[me], ssem.at[1 + j], rsem.at[1 + j], (cx, cy, c))
            s.start()
            sends.append(s)
        for j, (cx, cy) in enumerate(chips):
            slot = x2_ref.at[2 * cx + cy]
            _remote(slot, slot, ssem.at[1 + j], rsem.at[1 + j], (cx, cy, c)).wait_recv()
        out_ref[pl.ds(off, rh), :] = ((x2_ref[0] + x2_ref[1]) + x2_ref[2]) + x2_ref[3]
        for s in sends:
            s.wait_send()
        mine = out_ref.at[pl.ds(off, rh), :]
        s3 = _remote(mine, mine, ssem.at[4], rsem.at[4], sib)
        s3.start()
        off2 = pl.multiple_of((1 - c) * rh, SUBLANES)
        theirs = out_ref.at[pl.ds(off2, rh), :]
        _remote(theirs, theirs, ssem.at[4], rsem.at[4], sib).wait_recv()
        s3.wait_send()

    vm = pl.BlockSpec(memory_space=pltpu.VMEM)
    return pl.pallas_call(
        body, name=name, in_specs=[vm], out_specs=vm,
        out_shape=jax.ShapeDtypeStruct((r, LANES), F32),
        scratch_shapes=[pltpu.VMEM((r, LANES), F32), pltpu.VMEM((N_CHIPS, rh, LANES), F32),
                        pltpu.SemaphoreType.DMA((5,)), pltpu.SemaphoreType.DMA((5,))],
    )(buf)


PACK_ALIGN = 2 * SUBLANES * LANES


def _pack(arrays, rows_multiple=2 * SUBLANES):
    parts, offs, off = [], [], 0
    for a in arrays:
        flat = a.reshape(-1).astype(F32)
        padded = -(-flat.shape[0] // PACK_ALIGN) * PACK_ALIGN
        parts.append(jnp.pad(flat, (0, padded - flat.shape[0])))
        offs.append(off)
        off += padded
    buf = jnp.concatenate(parts).reshape(-1, LANES)
    return buf, offs


def _unpack(buf, offs, shapes):
    flat = buf.reshape(-1)
    out = []
    for off, shp in zip(offs, shapes):
        size = 1
        for d in shp:
            size *= d
        out.append(flat[off:off + size].reshape(shp))
    return out


def _cols_from_shards(g4):
    _, k, ns = g4.shape
    return jnp.transpose(g4, (1, 0, 2)).reshape(k, N_CHIPS * ns)


def _cols_to_shards(w):
    k, n = w.shape
    return jnp.transpose(w.reshape(k, N_CHIPS, n // N_CHIPS), (1, 0, 2))


def _block_cols(w, parts, blocks):
    lead = w.shape[:-1]
    width = w.shape[-1] // (parts * blocks)
    w = w.reshape(lead + (parts, blocks, width))
    w = jnp.swapaxes(w, -3, -2)
    return w.reshape(lead + (parts * blocks * width,))


def _unblock_cols(w, parts, blocks):
    lead = w.shape[:-1]
    width = w.shape[-1] // (parts * blocks)
    w = w.reshape(lead + (blocks, parts, width))
    w = jnp.swapaxes(w, -3, -2)
    return w.reshape(lead + (parts * blocks * width,))


def _pair_blockdiag(w8):
    w = w8.reshape(4, 2, 64, 64)
    z = jnp.zeros((4, 64, 64), w8.dtype)
    top = jnp.concatenate([w[:, 0], z], axis=2)
    bot = jnp.concatenate([z, w[:, 1]], axis=2)
    return jnp.concatenate([top, bot], axis=1)


def _pair_diag_blocks(w4):
    a = w4[:, :64, :64]
    b = w4[:, 64:, 64:]
    return jnp.stack([a, b], axis=1).reshape(8, 64, 64)


def _local_step(x, target, wts, on_event=None):
    s = x.shape[0]
    g = {}

    def event(name, token):
        if on_event is not None:
            on_event(name, g, token)

    win0 = wts["w_in0"]
    wout0 = wts["w_out0"]
    win1 = wts["w_in1"]
    wout1 = wts["w_out1"]
    wup = wts["w_up"]
    wdown = wts["w_down"]
    w4, b4, w3, b3 = wts["w4"], wts["b4"], wts["w3"], wts["b3"]
    wa, wx = wts["wa"], wts["wx"]
    wat, wxt = jnp.swapaxes(wa, 1, 2), jnp.swapaxes(wx, 1, 2)
    ba, bx, lam = wts["ba"], wts["bx"], wts["lam"]
    fcw, fcb = wts["ffn_cw"], wts["ffn_cb"]
    sgu_w, sgu_wt = wts["sgu_w"], wts["sgu_wt"]
    sgu_bias, sgu_gn = wts["sgu_bias"], wts["sgu_gn"]
    bf = wts["bf"]

    lane = jnp.arange(LANES)
    seg = jnp.where((lane[:, None] // 64) == (lane[None, :] // 64), 1.0 / 64.0, 0.0).astype(BF16)
    sel = jnp.stack([jnp.broadcast_to((lane[:, None] < 64), (LANES, LANES)),
                     jnp.broadcast_to((lane[:, None] >= 64), (LANES, LANES))]).astype(BF16)
    tril = (lane[:, None] >= lane[None, :]).astype(F32)

    n0 = _norm_fwd(x, wts["g_mix0"], name="norm_mix0")
    p0 = _mm([n0], win0, nb=1280, name="mm_in0")
    ya, yb, hl = _even_core_fwd(p0, w4, b4, wa, ba, wx, bx, lam, w3, b3, name="even_fwd")
    h1, n1 = _mm([ya, yb], wout0, res=x, norm_out=wts["g_ffn"][0], name="mm_out0")

    def ffn_fwd(h, n, layer, next_gain):
        up = _mm([n], wup[layer], out_dtype=BF16, ts=1024, nb=1408, name=f"mm_up{layer}")
        act = _ffn_core_fwd(up, fcw[layer], fcb[layer], name=f"ffn_fwd{layer}")
        if next_gain is None:
            return up, act, _mm([act], wdown[layer], res=h, name=f"mm_down{layer}"), None
        hn, nn = _mm([act], wdown[layer], res=h, norm_out=next_gain, name=f"mm_down{layer}")
        return up, act, hn, nn

    up0, act0, h2, n2 = ffn_fwd(h1, n1, 0, wts["g_mix1"])

    p1 = _mm([n2], win1, name="mm_in1")
    yc = _sgu_fwd(p1, sgu_gn, sgu_w, sgu_bias, seg, name="sgu_fwd")
    cum = _fcum_fwd(p1, bf, name="fcum_fwd")
    c8 = cum[:, :8]
    cq = jnp.broadcast_to(c8[:, :, None], (s, 8, LANES)).reshape(s, 8 * LANES)
    ck = jnp.transpose(c8).reshape(8, 1, s)
    yd, lb = _fox_fwd(p1, cq, ck, name="fox_fwd")
    h3, n3 = _mm([yc, yd], wout1, res=h2, norm_out=wts["g_ffn"][1], name="mm_out1")

    up1, act1, h4, _ = ffn_fwd(h3, n3, 1, None)
    dh4, loss, g["final_norm"] = _final(h4, wts["g_final"], target, name="final")

    def ffn_bwd(dh, h, n, up, act, layer):
        dact = _mm([dh], wdown[layer], trans_w=True, out_dtype=BF16, ts=1024, nb=1408, name=f"mm_dact{layer}")
        g[f"w_down{layer}"] = _mm_tn([act], [dh], ts=1024, nb=512, name=f"mm_dwdown{layer}")
        event(f"dwdown{layer}", g[f"w_down{layer}"])
        dgate, dval, dcwg, dcwv, dcbg, dcbv = _ffn_core_bwd(dact, up, fcw[layer], fcb[layer], name=f"ffn_bwd{layer}")
        event(f"ffn_bwd{layer}", dgate)
        g[f"w_up{layer}"] = _mm_tn([n], [dgate, dval], ts=1024, nb=1408, name=f"mm_dwup{layer}")
        event(f"dwup{layer}", g[f"w_up{layer}"])
        dhn, g[f"g_ffn{layer}"] = _mm([dgate, dval], wup[layer], trans_w=True, ts=256,
                                      norm_bwd=(h, wts["g_ffn"][layer], dh), name=f"mm_dn_ffn{layer}")
        g[f"ffn_cw{layer}"] = jnp.concatenate([dcwg, dcwv], axis=1)
        g[f"ffn_cb{layer}"] = jnp.concatenate([dcbg, dcbv], axis=1)
        return dhn

    dh3 = ffn_bwd(dh4, h3, n3, up1, act1, 1)

    dy1 = _mm([dh3], wout1, trans_w=True, ts=1024, name="mm_dy1")
    g["w_out1"] = _mm_tn([yc, yd], [dh3], ts=1024, nb=512, name="mm_dwout1")
    event("dwout1", g["w_out1"])
    dzu, dzg, g["sgu_w"], g["sgu_bias"], g["sgu_gn"] = _sgu_bwd(
        p1, dy1, sgu_gn, sgu_w, sgu_wt, sgu_bias, seg, tril, name="sgu_bwd")
    delta = _fox_delta(dy1, yd, sel, name="fox_delta")
    dq, dk, dv, dck, dcq = _fox_bwd(p1, dy1, lb, delta, ck, name="fox_bwd")
    event("fox_bwd", dq)
    dcs = jnp.pad(jnp.transpose(dck.reshape(8, s)), ((0, 0), (0, LANES - 8)))
    df, g["bf"] = _fcum_bwd(dcs, dcq, p1, bf, name="fcum_bwd")
    dp1 = jnp.concatenate([dzu, dzg, dq, dk, dv, df], axis=1)
    g["w_in1"] = _mm_tn([n2], [dp1], ts=1024, nb=896, name="mm_dwin1")
    event("dwin1", g["w_in1"])
    dh2, g["g_mix1"] = _mm([dp1], win1, trans_w=True, norm_bwd=(h2, wts["g_mix1"], dh3), name="mm_dn_mix1")

    dh1 = ffn_bwd(dh2, h1, n1, up0, act0, 0)

    dy0 = _mm([dh1], wout0, trans_w=True, ts=1024, name="mm_dy0")
    g["w_out0"] = _mm_tn([ya, yb], [dh1], ts=1024, nb=512, name="mm_dwout0")
    event("dwout0", g["w_out0"])
    (dp0, g["w4"], g["b4"], g["wa"], g["ba"], g["wx"], g["bx"], g["lam"], g["w3"], g["b3"]) = _even_core_bwd(
        dy0, p0, hl, w4, b4, wa, wat, ba, wx, wxt, bx, lam, w3, b3, name="even_bwd")
    event("even_bwd", dp0)
    g["w_in0"] = _mm_tn([n0], [dp0], ts=1024, nb=640, name="mm_dwin0")
    event("dwin0", g["w_in0"])
    grad_x, g["g_mix0"] = _mm([dp0], win0, trans_w=True, norm_bwd=(x, wts["g_mix0"], dh1), name="mm_dn_mix0")
    return loss, grad_x, g


def _prepare_weights(nat):
    lane = jnp.arange(LANES)
    tril = (lane[:, None] >= lane[None, :]).astype(F32)
    sgu_tril = nat["sgu_w"][0] * tril
    w_in1 = nat["mix1_w_in"]
    nblk = D_FF // FFN_CB
    return {
        "w_in0": _block_cols(nat["mix0_w_in"], 5, 4),
        "w_out0": nat["mix0_w_out"],
        "w_in1": jnp.pad(w_in1, ((0, 0), (0, 21 * LANES - w_in1.shape[1]))),
        "w_out1": nat["mix1_w_out"],
        "w_up": [nat["ffn_up"][l] for l in range(2)],
        "w_down": [nat["ffn_down"][l] for l in range(2)],
        "w4": nat["lru_conv_w"], "b4": nat["lru_conv_b"], "w3": nat["sconv_w"], "b3": nat["sconv_b"],
        "wa": _pair_blockdiag(nat["lru_wa"][0]).astype(BF16), "wx": _pair_blockdiag(nat["lru_wx"][0]).astype(BF16),
        "ba": nat["lru_ba"], "bx": nat["lru_bx"], "lam": nat["lru_lambda"],
        "ffn_cw": [nat["ffn_conv_w"][l] for l in range(2)],
        "ffn_cb": [nat["ffn_conv_b"][l:l + 1] for l in range(2)],
        "sgu_w": sgu_tril.astype(BF16), "sgu_wt": jnp.swapaxes(sgu_tril, 1, 2).astype(BF16),
        "sgu_bias": jnp.repeat(jnp.transpose(nat["sgu_b"][0]), 64, axis=1), "sgu_gn": nat["sgu_norm"],
        "bf": jnp.pad(nat["fox_bf"], ((0, 0), (0, LANES - 8))),
        "g_mix0": nat["mix0_norm"], "g_mix1": nat["mix1_norm"],
        "g_ffn": [nat["ffn_norm"][0:1], nat["ffn_norm"][1:2]], "g_final": nat["final_norm"].reshape(1, D_MODEL),
    }


def _natural_grads(g):
    nblk = D_FF // FFN_CB
    small = {
        "mix0_norm": g["g_mix0"], "lru_conv_b": g["b4"],
        "lru_wa": _pair_diag_blocks(g["wa"])[None], "lru_ba": g["ba"],
        "lru_wx": _pair_diag_blocks(g["wx"])[None], "lru_bx": g["bx"],
        "lru_lambda": g["lam"], "sconv_b": g["b3"],
        "sgu_w": g["sgu_w"][None],
        "sgu_b": jnp.transpose(g["sgu_bias"].reshape(CHUNK, 8, 64).sum(axis=2))[None],
        "fox_bf": g["bf"][:, :8],
        "ffn_norm": jnp.concatenate([g["g_ffn0"], g["g_ffn1"]], axis=0),
        "ffn_conv_b": jnp.concatenate([g["ffn_cb0"], g["ffn_cb1"]], axis=0),
        "final_norm": g["final_norm"].reshape(D_MODEL),
        "lru_conv_w": g["w4"][None], "sconv_w": g["w3"][None],
        "ffn_conv_w": jnp.stack([g["ffn_cw0"], g["ffn_cw1"]]),
        "mix1_norm": g["g_mix1"], "sgu_norm": g["sgu_gn"],
    }
    big = {
        "mix0_w_in": _unblock_cols(g["w_in0"], 5, 4), "mix0_w_out": g["w_out0"],
        "mix1_w_in": g["w_in1"][:, :2568], "mix1_w_out": g["w_out1"],
        "ffn_up0": g["w_up0"], "ffn_up1": g["w_up1"],
        "ffn_down0": g["w_down0"], "ffn_down1": g["w_down1"],
    }
    return small, big


COL_SHARDED = ("mix0_w_in", "mix1_w_in", "ffn_up0", "ffn_up1")
COL_ALIGNED = ("mix0_w_in", "ffn_up0", "ffn_up1")
SMALL_SHARDED = ("lru_conv_w", "sconv_w", "ffn_conv_w", "mix1_norm", "sgu_norm")
SMALL_REPLICATED = ("mix0_norm", "lru_conv_b", "lru_wa", "lru_ba", "lru_wx", "lru_bx", "lru_lambda", "sconv_b",
                    "sgu_w", "sgu_b", "fox_bf", "ffn_norm", "ffn_conv_b", "final_norm")
BIG = ("mix0_w_in", "mix0_w_out", "mix1_w_in", "mix1_w_out", "ffn_up0", "ffn_up1", "ffn_down0", "ffn_down1")
WEIGHT_ORDER = ("mix0_norm", "mix0_w_in", "lru_conv_w", "lru_conv_b", "lru_wa", "lru_ba", "lru_wx", "lru_bx",
                "lru_lambda", "sconv_w", "sconv_b", "mix0_w_out", "mix1_norm", "mix1_w_in", "sgu_norm", "sgu_w",
                "sgu_b", "fox_bf", "mix1_w_out", "ffn_norm", "ffn_up", "ffn_conv_w", "ffn_conv_b", "ffn_down",
                "final_norm")


GATHER_GROUPS = (("mix0_w_in", "mix0_w_out"), ("ffn_up0",), ("ffn_down0", "mix1_w_in"),
                 ("mix1_w_out", "ffn_up1", "ffn_down1"))
CID_GATHER, CID_PAIR, CID_FIRST, CID_SECOND, CID_SWAP = 1, 2, 3, 4, 5


class _GradReducer:
    def __init__(self):
        x, y, c = _mesh_pos()
        self.send = jnp.stack([c] + [2 * (c * (1 - x) + (1 - c) * t) + (c * t + (1 - c) * (1 - y))
                                     for t in range(2)]).astype(jnp.int32)
        self.keep = jnp.stack([c] + [c * (2 * x + t) + (1 - c) * (2 * t + y) for t in range(2)]).astype(jnp.int32)
        self.mine = (c * y + (1 - c) * x).reshape(1).astype(jnp.int32)
        self.groups = {}

    @staticmethod
    def _view(name, a):
        if name in COL_ALIGNED:
            return a.reshape(2, a.shape[0] // 2, a.shape[1])
        if name in COL_SHARDED:
            a = _cols_to_shards(a)
            return a.reshape(N_CHIPS, 2, a.shape[1] // 2, a.shape[2])
        rows = a.shape[0] // (2 * N_CHIPS)
        return a.reshape(N_CHIPS, 2, rows, a.shape[1])

    def start(self, group, grads):
        names = tuple(grads)
        views = [self._view(k, grads[k]) for k in names]
        cols = [k in COL_ALIGNED for k in names]
        data = _send_other_half(views, cols, collective_id=CID_PAIR, name=f"rs_pair_{group}")
        self.groups[group] = dict(names=names, stage=0, views=views, cols=cols, data=data)

    def step(self, group, after):
        st = self.groups[group]
        names = st["names"]
        if st["stage"] == 0:
            sums = [_pair_sum(a, col, b, self.send, after, name=f"rs_pair_sum_{k}")
                    for k, a, col, b in zip(names, st["views"], st["cols"], st["data"])]
            st["from_sib"] = st["data"]
            st["data"] = _send_first(sums, collective_id=CID_FIRST, name=f"rs_first_{group}")
        elif st["stage"] == 1:
            sums = [_first_sum(a, col, b, r, self.keep, after, name=f"rs_first_sum_{k}")
                    for k, a, col, b, r in zip(names, st["views"], st["cols"], st["from_sib"], st["data"])]
            st["keep"] = [s32 for s32, _ in sums]
            st["data"] = _send_second([s16 for _, s16 in sums], collective_id=CID_SECOND, name=f"rs_second_{group}")
        else:
            st["mine"] = [_second_sum(s32, r, self.mine, after, name=f"rs_second_sum_{k}")
                          for k, s32, r in zip(names, st["keep"], st["data"])]
            st["data"] = _swap_halves(st["mine"], collective_id=CID_SWAP, name=f"rs_swap_{group}")
        st["stage"] += 1

    def result(self, group):
        st = self.groups[group]
        return {k: (a, b) for k, a, b in zip(st["names"], st["mine"], st["data"])}


def _train_step(x, target, w, m, v):
    x2 = x[0]
    t2 = target[0]
    chip = 2 * lax.axis_index("x") + lax.axis_index("y")
    core_arr = lax.axis_index("c").reshape(1).astype(jnp.int32)
    chip_arr = chip.reshape(1).astype(jnp.int32)

    big_shards = {
        "mix0_w_in": w["mix0_w_in"][0], "mix0_w_out": w["mix0_w_out"][0],
        "mix1_w_in": w["mix1_w_in"][0], "mix1_w_out": w["mix1_w_out"][0],
        "ffn_up0": w["ffn_up"][0], "ffn_up1": w["ffn_up"][1],
        "ffn_down0": w["ffn_down"][0], "ffn_down1": w["ffn_down"][1],
    }
    small_shards = [w[k] for k in SMALL_SHARDED]
    small_buf, small_offs = _pack(small_shards)
    full = {}
    small_all = None
    for gi, names in enumerate(GATHER_GROUPS):
        cols = [k in COL_ALIGNED for k in names]
        placed = [_place(big_shards[k], col, chip_arr, BF16, name=f"place_{k}") for k, col in zip(names, cols)]
        if gi == 0:
            placed.append(_place(small_buf, False, chip_arr, F32, name="place_small"))
            cols = cols + [False]
        gathered = _all_gather(placed, cols, collective_id=CID_GATHER, name=f"gather_weights{gi}")
        if gi == 0:
            small_all = gathered[-1].reshape(N_CHIPS, -1, LANES)
        for k, arr in zip(names, gathered):
            if k in COL_ALIGNED:
                full[k] = arr.reshape(arr.shape[0] * arr.shape[1], arr.shape[2])
            elif k in COL_SHARDED:
                full[k] = _cols_from_shards(arr.reshape((N_CHIPS, arr.shape[1] * arr.shape[2], arr.shape[3])))
            else:
                full[k] = arr.reshape(-1, arr.shape[3])
    per_chip = [_unpack(small_all[k], small_offs, [a.shape for a in small_shards]) for k in range(N_CHIPS)]
    lru_conv_w = jnp.concatenate([per_chip[k][0] for k in range(N_CHIPS)], axis=-1)[0]
    sconv_w = jnp.concatenate([per_chip[k][1] for k in range(N_CHIPS)], axis=-1)[0]
    ffn_conv_w = jnp.concatenate([per_chip[k][2] for k in range(N_CHIPS)], axis=-1)
    mix1_norm = jnp.concatenate([per_chip[k][3] for k in range(N_CHIPS)], axis=-1)
    sgu_norm = jnp.concatenate([per_chip[k][4] for k in range(N_CHIPS)], axis=-1)

    nat = {
        "mix0_w_in": full["mix0_w_in"], "mix0_w_out": full["mix0_w_out"],
        "mix1_w_in": full["mix1_w_in"], "mix1_w_out": full["mix1_w_out"],
        "ffn_up": [full["ffn_up0"], full["ffn_up1"]], "ffn_down": [full["ffn_down0"], full["ffn_down1"]],
        "lru_conv_w": lru_conv_w, "sconv_w": sconv_w, "ffn_conv_w": ffn_conv_w, "mix1_norm": mix1_norm,
        "sgu_norm": sgu_norm,
    }
    for k in SMALL_REPLICATED:
        nat[k] = w[k]
    wts = _prepare_weights(nat)

    reducer = _GradReducer()

    def on_event(name, g, token):
        if name == "dwup1":
            reducer.start("ffn1", {"ffn_up1": g["w_up1"], "ffn_down1": g["w_down1"]})
        elif name in ("dwout1", "fox_bwd"):
            reducer.step("ffn1", token)
        elif name == "dwin1":
            reducer.step("ffn1", token)
            reducer.start("mix1", {"mix1_w_in": g["w_in1"][:, :2568], "mix1_w_out": g["w_out1"]})
        elif name in ("dwdown0", "ffn_bwd0"):
            reducer.step("mix1", token)
        elif name == "dwup0":
            reducer.step("mix1", token)
            reducer.start("ffn0", {"ffn_up0": g["w_up0"], "ffn_down0": g["w_down0"]})
        elif name in ("dwout0", "even_bwd"):
            reducer.step("ffn0", token)
        elif name == "dwin0":
            reducer.step("ffn0", token)
            reducer.start("mix0", {"mix0_w_in": _unblock_cols(g["w_in0"], 5, 4), "mix0_w_out": g["w_out0"]})

    loss, grad_x, g = _local_step(x2, t2, wts, on_event)
    grads_small, _ = _natural_grads(g)

    small_names = SMALL_REPLICATED + SMALL_SHARDED
    small_list = [grads_small[k] for k in small_names] + [loss[:, :1]]
    sbuf, soffs = _pack(small_list)
    sred = _all_reduce_small(sbuf, name="reduce_small")
    small_red = _unpack(sred, soffs, [a.shape for a in small_list])
    loss_total = small_red[-1][0, 0]
    gsum = dict(zip(small_names, small_red[:-1]))
    for k in SMALL_SHARDED:
        width = w[k].shape[-1]
        gsum[k] = lax.dynamic_slice_in_dim(gsum[k], chip * width, width, axis=gsum[k].ndim - 1)

    out_g, out_d, out_m, out_v = {}, {}, {}, {}
    reduced = {}
    for group in ("ffn1", "mix1", "ffn0"):
        reduced.update(reducer.result(group))

    def update(pname, keys):
        mine = [reduced[k][0] for k in keys]
        theirs = [reduced[k][1] for k in keys]
        out_g[pname], out_d[pname], out_m[pname], out_v[pname] = _adamw_halves(
            w[pname], mine, theirs, m[pname], v[pname], core_arr, name=f"adamw_{pname}")
        return out_d[pname]

    reducer.step("mix0", update("ffn_up", ("ffn_up0", "ffn_up1")))
    small_w = [w[k] for k in small_names]
    pg, offs = _pack([gsum[k] for k in small_names])
    pw, _ = _pack(small_w)
    pm, _ = _pack([m[k] for k in small_names])
    pv, _ = _pack([v[k] for k in small_names])
    sd, sm, sv = _adamw(pw, pg, pm, pv, name="adamw_small")
    reducer.step("mix0", update("ffn_down", ("ffn_down0", "ffn_down1")))
    update("mix1_w_in", ("mix1_w_in",))
    reducer.step("mix0", update("mix1_w_out", ("mix1_w_out",)))
    reduced.update(reducer.result("mix0"))
    update("mix0_w_in", ("mix0_w_in",))
    update("mix0_w_out", ("mix0_w_out",))

    shapes = [a.shape for a in small_w]
    for k, dd, mm, vv in zip(small_names, _unpack(sd, offs, shapes), _unpack(sm, offs, shapes),
                             _unpack(sv, offs, shapes)):
        out_g[k], out_d[k], out_m[k], out_v[k] = gsum[k].reshape(w[k].shape), dd, mm, vv

    outs = [loss_total, grad_x[None]]
    for d in (out_g, out_d, out_m, out_v):
        outs.extend(d[k] for k in WEIGHT_ORDER)
    return tuple(outs)


def kernel(x, mix0_norm, mix0_w_in, lru_conv_w, lru_conv_b, lru_wa, lru_ba, lru_wx, lru_bx, lru_lambda, sconv_w, sconv_b, mix0_w_out, mix1_norm, mix1_w_in, sgu_norm, sgu_w, sgu_b, fox_bf, mix1_w_out, ffn_norm, ffn_up, ffn_conv_w, ffn_conv_b, ffn_down, final_norm, loss_target, m_mix0_norm, m_mix0_w_in, m_lru_conv_w, m_lru_conv_b, m_lru_wa, m_lru_ba, m_lru_wx, m_lru_bx, m_lru_lambda, m_sconv_w, m_sconv_b, m_mix0_w_out, m_mix1_norm, m_mix1_w_in, m_sgu_norm, m_sgu_w, m_sgu_b, m_fox_bf, m_mix1_w_out, m_ffn_norm, m_ffn_up, m_ffn_conv_w, m_ffn_conv_b, m_ffn_down, m_final_norm, v_mix0_norm, v_mix0_w_in, v_lru_conv_w, v_lru_conv_b, v_lru_wa, v_lru_ba, v_lru_wx, v_lru_bx, v_lru_lambda, v_sconv_w, v_sconv_b, v_mix0_w_out, v_mix1_norm, v_mix1_w_in, v_sgu_norm, v_sgu_w, v_sgu_b, v_fox_bf, v_mix1_w_out, v_ffn_norm, v_ffn_up, v_ffn_conv_w, v_ffn_conv_b, v_ffn_down, v_final_norm):
    w = dict(zip(WEIGHT_ORDER, (mix0_norm, mix0_w_in, lru_conv_w, lru_conv_b, lru_wa, lru_ba, lru_wx, lru_bx, lru_lambda, sconv_w, sconv_b, mix0_w_out, mix1_norm, mix1_w_in, sgu_norm, sgu_w, sgu_b, fox_bf, mix1_w_out, ffn_norm, ffn_up, ffn_conv_w, ffn_conv_b, ffn_down, final_norm)))
    m = dict(zip(WEIGHT_ORDER, (m_mix0_norm, m_mix0_w_in, m_lru_conv_w, m_lru_conv_b, m_lru_wa, m_lru_ba, m_lru_wx, m_lru_bx, m_lru_lambda, m_sconv_w, m_sconv_b, m_mix0_w_out, m_mix1_norm, m_mix1_w_in, m_sgu_norm, m_sgu_w, m_sgu_b, m_fox_bf, m_mix1_w_out, m_ffn_norm, m_ffn_up, m_ffn_conv_w, m_ffn_conv_b, m_ffn_down, m_final_norm)))
    v = dict(zip(WEIGHT_ORDER, (v_mix0_norm, v_mix0_w_in, v_lru_conv_w, v_lru_conv_b, v_lru_wa, v_lru_ba, v_lru_wx, v_lru_bx, v_lru_lambda, v_sconv_w, v_sconv_b, v_mix0_w_out, v_mix1_norm, v_mix1_w_in, v_sgu_norm, v_sgu_w, v_sgu_b, v_fox_bf, v_mix1_w_out, v_ffn_norm, v_ffn_up, v_ffn_conv_w, v_ffn_conv_b, v_ffn_down, v_final_norm)))
    return _train_step(x, loss_target, w, m, v)
```

```python
import functools

import jax
import jax.numpy as jnp
from jax import lax
from jax.experimental import pallas as pl
from jax.experimental.pallas import tpu as pltpu
from jax.experimental.pallas import tpu_sc as plsc

F32 = jnp.float32
BF16 = jnp.bfloat16
MESH = pl.DeviceIdType.MESH

D_MODEL = 1024
LANES = 128
SUBLANES = 8
N_CHIPS = 4
EPS = 1e-6
LRU_C = 8.0
D_FF = 2816
FFN_CB = 256
CHUNK = 128
NEG = -1e30

ADAM_LR = 0.001
ADAM_B1 = 0.9
ADAM_B2 = 0.999
ADAM_EPS = 1e-08
ADAM_WD = 0.01
ADAM_STEP = 10
ADAM_C1 = 1.0 - ADAM_B1 ** ADAM_STEP
ADAM_C2 = 1.0 - ADAM_B2 ** ADAM_STEP

_GELU_C = 0.7978845608028654
_GELU_A = 0.044715


def _sigmoid(x):
    return 1.0 / (1.0 + jnp.exp(-x))


def _sigmoid_tanh(x):
    return 0.5 * jnp.tanh(0.5 * x) + 0.5


def _log1p_pos(e):
    w = 1.0 + e
    return jnp.where(w == 1.0, e, jnp.log(w) * (e / (w - 1.0)))


def _softplus(x):
    return jnp.maximum(x, 0.0) + _log1p_pos(jnp.exp(-jnp.abs(x)))


def _gelu(x):
    t = jnp.tanh(_GELU_C * (x + _GELU_A * (x * x * x)))
    return 0.5 * x * (1.0 + t), t


def _gelu_grad(x, t):
    return 0.5 * (1.0 + t) + 0.5 * x * (1.0 - t * t) * (_GELU_C * (1.0 + 3.0 * _GELU_A * x * x))


def _rows(shape):
    return lax.broadcasted_iota(jnp.int32, shape, 0)


def _lanes(shape):
    return lax.broadcasted_iota(jnp.int32, shape, 1)


def _shift_down(x, halo8, j):
    if j == 0:
        return x
    r = pltpu.roll(x, j, 0)
    hr = pltpu.roll(halo8, j, 0)
    top = jnp.where(_rows(hr.shape) < j, hr, r[:SUBLANES])
    return jnp.concatenate([top, r[SUBLANES:]], axis=0)


def _shift_up(x, next8, j):
    if j == 0:
        return x
    n = x.shape[0]
    r = pltpu.roll(x, n - j, 0)
    nr = pltpu.roll(next8, SUBLANES - j, 0)
    bot = jnp.where(_rows(nr.shape) >= SUBLANES - j, nr, r[n - SUBLANES:])
    return jnp.concatenate([r[:n - SUBLANES], bot], axis=0)


def _scan_fwd(a, u):
    n = a.shape[0]
    row = _rows(a.shape)
    h = u
    k = 1
    while k < n:
        keep = row >= k
        h_sh = jnp.where(keep, pltpu.roll(h, k, 0), 0.0)
        a_sh = jnp.where(keep, pltpu.roll(a, k, 0), 1.0)
        h = a * h_sh + h
        a = a * a_sh
        k *= 2
    return h, a


def _scan_rev(b, d):
    n = b.shape[0]
    row = _rows(b.shape)
    g = d
    k = 1
    while k < n:
        keep = row < n - k
        g_sh = jnp.where(keep, pltpu.roll(g, n - k, 0), 0.0)
        b_sh = jnp.where(keep, pltpu.roll(b, n - k, 0), 1.0)
        g = b * g_sh + g
        b = b * b_sh
        k *= 2
    return g, b


def _cumsum_fwd(x):
    n = x.shape[0]
    row = _rows(x.shape)
    k = 1
    while k < n:
        x = x + jnp.where(row >= k, pltpu.roll(x, k, 0), 0.0)
        k *= 2
    return x


def _cumsum_rev(x):
    n = x.shape[0]
    row = _rows(x.shape)
    k = 1
    while k < n:
        x = x + jnp.where(row < n - k, pltpu.roll(x, n - k, 0), 0.0)
        k *= 2
    return x


def _dot(a, b):
    return lax.dot_general(a, b, (((1,), (0,)), ((), ())), preferred_element_type=F32)


def _dot_nt(a, b):
    return lax.dot_general(a, b, (((1,), (1,)), ((), ())), preferred_element_type=F32)


def _dot_tn(a, b):
    return lax.dot_general(a, b, (((0,), (0,)), ((), ())), preferred_element_type=F32)


def _dot_split(x, m_bf16):
    hi = x.astype(BF16)
    lo = (x - hi.astype(F32)).astype(BF16)
    return _dot(hi, m_bf16) + _dot(lo, m_bf16)


def _tile_rows(ts, s):
    return min(ts, s)


def _mm(a_list, w, *, trans_w=False, res=None, norm_bwd=None, norm_out=None, out_dtype=F32, ts=512, nb=None,
        name):
    s = a_list[0].shape[0]
    ks = [a.shape[1] for a in a_list]
    k = sum(ks)
    n = w.shape[0] if trans_w else w.shape[1]
    ts = _tile_rows(ts, s)
    nb = n if nb is None else nb
    na = len(a_list)
    has_res = res is not None
    fused = norm_bwd is not None
    normed = norm_out is not None
    offs = [sum(ks[:p]) for p in range(na)]

    def body(*refs):
        a_refs = refs[:na]
        w_ref = refs[na]
        acc = None
        for a_ref, off, kk in zip(a_refs, offs, ks):
            a = a_ref[...].astype(BF16)
            if trans_w:
                part = _dot_nt(a, w_ref[:, off:off + kk])
            else:
                part = _dot(a, w_ref[off:off + kk, :])
            acc = part if acc is None else acc + part
        if has_res:
            acc = acc + refs[na + 1][...]
        if normed:
            gn_ref, o_ref, n_ref = refs[-3:]
            o_ref[...] = acc.astype(out_dtype)
            r = lax.rsqrt(jnp.mean(acc * acc, axis=-1, keepdims=True) + EPS)
            n_ref[...] = ((acc * r) * gn_ref[...]).astype(BF16)
            return
        if not fused:
            refs[-1][...] = acc.astype(out_dtype)
            return
        h_ref, g_ref, dres_ref, dh_ref, dg_ref = refs[na + 1:]
        i = pl.program_id(1)
        x = h_ref[...]
        r = lax.rsqrt(jnp.mean(x * x, axis=-1, keepdims=True) + EPS)
        xhat = x * r
        part = jnp.sum(acc * xhat, axis=0, keepdims=True)

        @pl.when(i == 0)
        def _():
            dg_ref[...] = part

        @pl.when(i > 0)
        def _():
            dg_ref[...] += part

        dxh = acc * g_ref[...]
        dh_ref[...] = dres_ref[...] + r * (dxh - xhat * jnp.mean(dxh * xhat, axis=-1, keepdims=True))

    in_specs = [pl.BlockSpec((ts, kk), lambda j, i: (i, 0)) for kk in ks]
    if trans_w:
        in_specs.append(pl.BlockSpec((nb, k), lambda j, i: (j, 0)))
    else:
        in_specs.append(pl.BlockSpec((k, nb), lambda j, i: (0, j)))
    args = list(a_list) + [w]
    tile = pl.BlockSpec((ts, nb), lambda j, i: (i, j))
    if has_res:
        in_specs.append(tile)
        args.append(res)
    if fused:
        assert nb == n and not has_res
        vec = pl.BlockSpec((1, n), lambda j, i: (0, 0))
        h, g, dres = norm_bwd
        return pl.pallas_call(
            body, name=name, grid=(1, s // ts), in_specs=in_specs + [tile, vec, tile],
            out_specs=(tile, vec),
            out_shape=(jax.ShapeDtypeStruct((s, n), F32), jax.ShapeDtypeStruct((1, n), F32)),
        )(*args, h, g, dres)
    if normed:
        assert nb == n and out_dtype == F32
        vec = pl.BlockSpec((1, n), lambda j, i: (0, 0))
        return pl.pallas_call(
            body, name=name, grid=(1, s // ts), in_specs=in_specs + [vec], out_specs=(tile, tile),
            out_shape=(jax.ShapeDtypeStruct((s, n), F32), jax.ShapeDtypeStruct((s, n), BF16)),
        )(*args, norm_out)
    return pl.pallas_call(
        body, name=name, grid=(n // nb, s // ts), in_specs=in_specs, out_specs=tile,
        out_shape=jax.ShapeDtypeStruct((s, n), out_dtype),
    )(*args)


def _mm_tn(a_list, b_list, *, ts=512, nb=None, name):
    s = b_list[0].shape[0]
    ks = [a.shape[1] for a in a_list]
    k = sum(ks)
    width = b_list[0].shape[1]
    n = width * len(b_list)
    ts = _tile_rows(ts, s)
    nb = width if nb is None else nb
    per = width // nb
    na = len(a_list)
    nparts = len(b_list)

    def body(*refs):
        a_refs = refs[:na]
        b_refs = refs[na:na + nparts]
        o_ref = refs[-1]
        j = pl.program_id(0)
        i = pl.program_id(1)
        parts = [r[...].astype(BF16) for r in a_refs]
        a = parts[0] if na == 1 else jnp.concatenate(parts, axis=1)

        def accumulate(b_ref):
            upd = _dot_tn(a, b_ref[...].astype(BF16))

            @pl.when(i == 0)
            def _():
                o_ref[...] = upd

            @pl.when(i > 0)
            def _():
                o_ref[...] += upd

        if nparts == 1:
            accumulate(b_refs[0])
        else:
            for part, b_ref in enumerate(b_refs):
                pl.when(j // per == part)(functools.partial(accumulate, b_ref))

    in_specs = [pl.BlockSpec((ts, kk), lambda j, i: (i, 0)) for kk in ks]
    for part in range(nparts):
        in_specs.append(pl.BlockSpec(
            (ts, nb), lambda j, i, part=part: (i, jnp.clip(j - part * per, 0, per - 1))))
    return pl.pallas_call(
        body, name=name, grid=(n // nb, s // ts), in_specs=in_specs,
        out_specs=pl.BlockSpec((k, nb), lambda j, i: (0, j)),
        out_shape=jax.ShapeDtypeStruct((k, n), F32),
    )(*a_list, *b_list)


def _norm_fwd(h, g, *, ts=512, name):
    s, d = h.shape
    ts = _tile_rows(ts, s)

    def body(h_ref, g_ref, n_ref):
        x = h_ref[...]
        r = lax.rsqrt(jnp.mean(x * x, axis=-1, keepdims=True) + EPS)
        n_ref[...] = ((x * r) * g_ref[...]).astype(BF16)

    return pl.pallas_call(
        body, name=name, grid=(s // ts,),
        in_specs=[pl.BlockSpec((ts, d), lambda i: (i, 0)), pl.BlockSpec((1, d), lambda i: (0, 0))],
        out_specs=pl.BlockSpec((ts, d), lambda i: (i, 0)),
        out_shape=jax.ShapeDtypeStruct((s, d), BF16),
    )(h, g)


def _norm_bwd(dn, h, g, dres, *, ts=512, name):
    s, d = h.shape
    ts = _tile_rows(ts, s)

    def body(dn_ref, h_ref, g_ref, dres_ref, dh_ref, dg_ref):
        i = pl.program_id(0)
        x = h_ref[...]
        dnv = dn_ref[...]
        r = lax.rsqrt(jnp.mean(x * x, axis=-1, keepdims=True) + EPS)
        xhat = x * r
        part = jnp.sum(dnv * xhat, axis=0, keepdims=True)

        @pl.when(i == 0)
        def _():
            dg_ref[...] = part

        @pl.when(i > 0)
        def _():
            dg_ref[...] += part

        dxh = dnv * g_ref[...]
        dh_ref[...] = dres_ref[...] + r * (dxh - xhat * jnp.mean(dxh * xhat, axis=-1, keepdims=True))

    tile = pl.BlockSpec((ts, d), lambda i: (i, 0))
    vec = pl.BlockSpec((1, d), lambda i: (0, 0))
    return pl.pallas_call(
        body, name=name, grid=(s // ts,), in_specs=[tile, tile, vec, tile],
        out_specs=(tile, vec),
        out_shape=(jax.ShapeDtypeStruct((s, d), F32), jax.ShapeDtypeStruct((1, d), F32)),
    )(dn, h, g, dres)


def _final(h, g, target, *, ts=512, name):
    s, d = h.shape
    ts = _tile_rows(ts, s)
    nt = s // ts

    def body(h_ref, g_ref, t_ref, dh_ref, loss_ref, dg_ref, acc_ref):
        i = pl.program_id(0)
        x = h_ref[...]
        r = lax.rsqrt(jnp.mean(x * x, axis=-1, keepdims=True) + EPS)
        xhat = x * r
        gv = g_ref[...]
        err = xhat * gv - t_ref[...]
        sq = jnp.sum(err * err, axis=0, keepdims=True)
        dy = err * (1.0 / d)
        part = jnp.sum(dy * xhat, axis=0, keepdims=True)

        @pl.when(i == 0)
        def _():
            acc_ref[...] = sq
            dg_ref[...] = part

        @pl.when(i > 0)
        def _():
            acc_ref[...] += sq
            dg_ref[...] += part

        dxh = dy * gv
        dh_ref[...] = r * (dxh - xhat * jnp.mean(dxh * xhat, axis=-1, keepdims=True))

        @pl.when(i == nt - 1)
        def _():
            tot = jnp.sum(acc_ref[...], axis=1, keepdims=True) * (0.5 / d)
            loss_ref[...] = jnp.broadcast_to(tot, (1, LANES))

    tile = pl.BlockSpec((ts, d), lambda i: (i, 0))
    vec = pl.BlockSpec((1, d), lambda i: (0, 0))
    return pl.pallas_call(
        body, name=name, grid=(nt,), in_specs=[tile, vec, tile],
        out_specs=(tile, pl.BlockSpec((1, LANES), lambda i: (0, 0)), vec),
        out_shape=(jax.ShapeDtypeStruct((s, d), F32), jax.ShapeDtypeStruct((1, LANES), F32),
                   jax.ShapeDtypeStruct((1, d), F32)),
        scratch_shapes=[pltpu.VMEM((1, d), F32)],
    )(h, g, target)


def _halo_map(ts, width_blocks):
    per = ts // SUBLANES

    def index(j, i):
        return (jnp.maximum(i * per - 1, 0), width_blocks(j))

    return index


def _even_gates(xc, wa, ba, wx, bx, sp):
    xb = xc.astype(BF16)
    r = _sigmoid(_dot(xb, wa) + ba)
    ig = _sigmoid(_dot(xb, wx) + bx)
    la = (-LRU_C) * r * sp
    a = jnp.exp(la)
    a2 = a * a
    m = jnp.sqrt(-jnp.tanh(la) * (1.0 + a2))
    return r, ig, la, a, a2, m


def _even_core_fwd(p, w4, b4, wa, ba, wx, bx, lam, w3, b3, *, ts=512, name):
    s = p.shape[0]
    ts = _tile_rows(ts, s)
    nt = s // ts
    nblk = 4

    def body(p_ref, ph_ref, w4_ref, b4_ref, wa_ref, ba_ref, wx_ref, bx_ref, lam_ref, w3_ref, b3_ref,
             ya_ref, yb_ref, hl_ref, hcar_ref):
        i = pl.program_id(1)
        first = (i > 0).astype(F32)
        xa = p_ref[:, 0:LANES]
        ga = p_ref[:, LANES:2 * LANES]
        cp = p_ref[:, 2 * LANES:3 * LANES]
        bp = p_ref[:, 3 * LANES:4 * LANES]
        vb = p_ref[:, 4 * LANES:5 * LANES]
        xa_h = ph_ref[:, 0:LANES] * first
        s_h = ph_ref[:, 2 * LANES:3 * LANES] * ph_ref[:, 4 * LANES:5 * LANES] * first

        xc = b4_ref[...] + w4_ref[3:4, :] * xa
        for k in range(3):
            xc = xc + w4_ref[k:k + 1, :] * _shift_down(xa, xa_h, 3 - k)
        sp = _softplus(-lam_ref[...])
        _, ig, _, a, _, m = _even_gates(xc, wa_ref[0], ba_ref[...], wx_ref[0], bx_ref[...], sp)
        u = m * (ig * xc)
        hs, acum = _scan_fwd(a, u)

        @pl.when(i == 0)
        def _():
            hcar_ref[...] = jnp.zeros_like(hcar_ref)

        hs = hs + acum * hcar_ref[0:1, :]
        hl_ref[...] = hs
        hcar_ref[0:1, :] = hl_ref[ts - 1:ts, :]
        ge, _ = _gelu(ga)
        ya_ref[...] = (hs * ge).astype(BF16)

        sv = cp * vb
        sc = b3_ref[...] + w3_ref[2:3, :] * sv
        for k in range(2):
            sc = sc + w3_ref[k:k + 1, :] * _shift_down(sv, s_h, 2 - k)
        yb_ref[...] = (bp * sc).astype(BF16)

    blk = pl.BlockSpec((ts, 5 * LANES), lambda j, i: (i, j))
    halo = pl.BlockSpec((SUBLANES, 5 * LANES), _halo_map(ts, lambda j: j))
    vec = pl.BlockSpec((1, LANES), lambda j, i: (0, j))
    out = pl.BlockSpec((ts, LANES), lambda j, i: (i, j))
    return pl.pallas_call(
        body, name=name, grid=(nblk, nt),
        in_specs=[blk, halo,
                  pl.BlockSpec((4, LANES), lambda j, i: (0, j)), vec,
                  pl.BlockSpec((1, LANES, LANES), lambda j, i: (j, 0, 0)), vec,
                  pl.BlockSpec((1, LANES, LANES), lambda j, i: (j, 0, 0)), vec, vec,
                  pl.BlockSpec((3, LANES), lambda j, i: (0, j)), vec],
        out_specs=(out, out, out),
        out_shape=(jax.ShapeDtypeStruct((s, 4 * LANES), BF16), jax.ShapeDtypeStruct((s, 4 * LANES), BF16),
                   jax.ShapeDtypeStruct((s, 4 * LANES), F32)),
        scratch_shapes=[pltpu.VMEM((SUBLANES, LANES), F32)],
    )(p, p, w4, b4, wa, ba, wx, bx, lam, w3, b3)


def _even_core_bwd(dy, p, hl, w4, b4, wa, wat, ba, wx, wxt, bx, lam, w3, b3, *, ts=256, name):
    s = p.shape[0]
    ts = _tile_rows(ts, s)
    nt = s // ts
    nblk = 4
    per = ts // SUBLANES

    def body(dya_ref, dyb_ref, p_ref, ph_ref, hl_ref, hh_ref,
             w4_ref, b4_ref, wa_ref, wat_ref, ba_ref, wx_ref, wxt_ref, bx_ref, lam_ref, w3_ref, b3_ref,
             dp_ref, dw4_ref, db4_ref, dwa_ref, dba_ref, dwx_ref, dbx_ref, dlam_ref, dw3_ref, db3_ref,
             dxc_nx, dsc_nx, cg_ref):
        i = pl.program_id(1)
        ti = nt - 1 - i
        first = (ti > 0).astype(F32)
        xa = p_ref[:, 0:LANES]
        ga = p_ref[:, LANES:2 * LANES]
        cp = p_ref[:, 2 * LANES:3 * LANES]
        bp = p_ref[:, 3 * LANES:4 * LANES]
        vb = p_ref[:, 4 * LANES:5 * LANES]
        xa_h = ph_ref[:, 0:LANES] * first
        s_h = ph_ref[:, 2 * LANES:3 * LANES] * ph_ref[:, 4 * LANES:5 * LANES] * first
        h_h = hh_ref[...] * first

        @pl.when(i == 0)
        def _():
            dxc_nx[...] = jnp.zeros_like(dxc_nx)
            dsc_nx[...] = jnp.zeros_like(dsc_nx)
            cg_ref[...] = jnp.zeros_like(cg_ref)
            for ref in (dw4_ref, db4_ref, dwa_ref, dba_ref, dwx_ref, dbx_ref, dlam_ref, dw3_ref, db3_ref):
                ref[...] = jnp.zeros_like(ref)

        xa_sh = [_shift_down(xa, xa_h, 3 - k) for k in range(3)] + [xa]
        xc = b4_ref[...]
        for k in range(4):
            xc = xc + w4_ref[k:k + 1, :] * xa_sh[k]
        lamv = lam_ref[...]
        sp = _softplus(-lamv)
        r, ig, _, a, a2, m = _even_gates(xc, wa_ref[0], ba_ref[...], wx_ref[0], bx_ref[...], sp)
        sv = cp * vb
        sv_sh = [_shift_down(sv, s_h, 2 - k) for k in range(2)] + [sv]
        sc = b3_ref[...]
        for k in range(3):
            sc = sc + w3_ref[k:k + 1, :] * sv_sh[k]
        hs = hl_ref[...]
        h_prev = _shift_down(hs, h_h, 1)

        dya = dya_ref[...]
        dyb = dyb_ref[...]
        ge, gt = _gelu(ga)
        dga = dya * hs * _gelu_grad(ga, gt)
        dh = dya * ge

        ones8 = jnp.ones((SUBLANES, LANES), F32)
        b = _shift_up(a, ones8, 1)
        g, bcum = _scan_rev(b, dh)
        g = g + bcum * cg_ref[0:1, :]
        ag = a * g
        cg_ref[...] = ag[:SUBLANES]

        da = g * h_prev
        xi = ig * xc
        dm = g * xi
        dig = g * m * xc
        dxc = g * m * ig
        dla = da * a - dm * (a2 / m)
        dr = dla * ((-LRU_C) * sp)
        dlam_ref[...] += jnp.sum(dla * r, axis=0, keepdims=True) * (LRU_C * _sigmoid(-lamv))
        dra = dr * r * (1.0 - r)
        dia = dig * ig * (1.0 - ig)
        drab = dra.astype(BF16)
        diab = dia.astype(BF16)
        xcb = xc.astype(BF16)
        dxc = dxc + _dot(drab, wat_ref[0]) + _dot(diab, wxt_ref[0])
        dwa_ref[0] += _dot_tn(xcb, drab)
        dwx_ref[0] += _dot_tn(xcb, diab)
        dba_ref[...] += jnp.sum(dra, axis=0, keepdims=True)
        dbx_ref[...] += jnp.sum(dia, axis=0, keepdims=True)

        nx = dxc_nx[...]
        dxa = w4_ref[3:4, :] * dxc
        for k in range(3):
            dxa = dxa + w4_ref[k:k + 1, :] * _shift_up(dxc, nx, 3 - k)
        for k in range(4):
            dw4_ref[k:k + 1, :] += jnp.sum(dxc * xa_sh[k], axis=0, keepdims=True)
        db4_ref[...] += jnp.sum(dxc, axis=0, keepdims=True)
        dxc_nx[...] = dxc[:SUBLANES]

        dbp = dyb * sc
        dsc = dyb * bp
        nsc = dsc_nx[...]
        ds = w3_ref[2:3, :] * dsc
        for k in range(2):
            ds = ds + w3_ref[k:k + 1, :] * _shift_up(dsc, nsc, 2 - k)
        for k in range(3):
            dw3_ref[k:k + 1, :] += jnp.sum(dsc * sv_sh[k], axis=0, keepdims=True)
        db3_ref[...] += jnp.sum(dsc, axis=0, keepdims=True)
        dsc_nx[...] = dsc[:SUBLANES]

        dp_ref[:, 0:LANES] = dxa.astype(BF16)
        dp_ref[:, LANES:2 * LANES] = dga.astype(BF16)
        dp_ref[:, 2 * LANES:3 * LANES] = (ds * vb).astype(BF16)
        dp_ref[:, 3 * LANES:4 * LANES] = dbp.astype(BF16)
        dp_ref[:, 4 * LANES:5 * LANES] = (ds * cp).astype(BF16)

    def rev(j, i):
        return (nt - 1 - i, j)

    def rev_halo(col):
        def index(j, i):
            return (jnp.maximum((nt - 1 - i) * per - 1, 0), col(j))
        return index

    blk = pl.BlockSpec((ts, 5 * LANES), rev)
    one = pl.BlockSpec((ts, LANES), rev)
    vec = pl.BlockSpec((1, LANES), lambda j, i: (0, j))
    mat = pl.BlockSpec((1, LANES, LANES), lambda j, i: (j, 0, 0))
    w4s = pl.BlockSpec((4, LANES), lambda j, i: (0, j))
    w3s = pl.BlockSpec((3, LANES), lambda j, i: (0, j))
    f = jax.ShapeDtypeStruct
    return pl.pallas_call(
        body, name=name, grid=(nblk, nt),
        in_specs=[one, pl.BlockSpec((ts, LANES), lambda j, i: (nt - 1 - i, 4 + j)),
                  blk, pl.BlockSpec((SUBLANES, 5 * LANES), rev_halo(lambda j: j)),
                  one, pl.BlockSpec((SUBLANES, LANES), rev_halo(lambda j: j)),
                  w4s, vec, mat, mat, vec, mat, mat, vec, vec, w3s, vec],
        out_specs=(blk, w4s, vec, mat, vec, mat, vec, vec, w3s, vec),
        out_shape=(f((s, 20 * LANES), BF16), f((4, 4 * LANES), F32), f((1, 4 * LANES), F32),
                   f((4, LANES, LANES), F32), f((1, 4 * LANES), F32),
                   f((4, LANES, LANES), F32), f((1, 4 * LANES), F32), f((1, 4 * LANES), F32),
                   f((3, 4 * LANES), F32), f((1, 4 * LANES), F32)),
        scratch_shapes=[pltpu.VMEM((SUBLANES, LANES), F32), pltpu.VMEM((SUBLANES, LANES), F32),
                        pltpu.VMEM((SUBLANES, LANES), F32)],
    )(dy, dy, p, p, hl, hl, w4, b4, wa, wat, ba, wx, wxt, bx, lam, w3, b3)


def _ffn_conv(u_ref, uh_ref, w_ref, b_ref, first):
    u = u_ref[...].astype(F32)
    u_h = uh_ref[...].astype(F32)[SUBLANES:] * first
    u_sh = [_shift_down(u, u_h, 2 - k) for k in range(2)] + [u]
    hc = b_ref[...]
    for k in range(3):
        hc = hc + w_ref[k:k + 1, :] * u_sh[k]
    return hc, u_sh


def _ffn_specs(ts, row, halo_row):
    nblk = D_FF // FFN_CB
    specs = []
    for off in (0, nblk):
        specs.append(pl.BlockSpec((ts, FFN_CB), lambda j, i, off=off: (row(i), off + j)))
        specs.append(pl.BlockSpec((16, FFN_CB), lambda j, i, off=off: (halo_row(i), off + j)))
        specs.append(pl.BlockSpec((3, FFN_CB), lambda j, i, off=off: (0, off + j)))
        specs.append(pl.BlockSpec((1, FFN_CB), lambda j, i, off=off: (0, off + j)))
    return specs


FFN_STRIP = 4 * SUBLANES
FFN_HALO = 2 * SUBLANES


def _ffn_stage(u_ref, uh_ref, dst_ref, first):
    dst_ref[0:FFN_HALO, :] = jnp.where(first, uh_ref[...], jnp.zeros_like(uh_ref))
    dst_ref[FFN_HALO:, :] = u_ref[...]


def _ffn_strip_conv(u_ref, r, w, b):
    win = u_ref[pl.ds(r, FFN_HALO + FFN_STRIP), :].astype(F32)
    cur, before = win[FFN_HALO:], win[SUBLANES:FFN_HALO]
    sh = [_shift_down(cur, before, 2 - k) for k in range(2)] + [cur]
    return b + w[0:1] * sh[0] + w[1:2] * sh[1] + w[2:3] * sh[2], sh


def _ffn_core_fwd(up, w, b, *, ts=512, name):
    s = up.shape[0]
    ts = _tile_rows(ts, s)
    nt = s // ts
    nblk = D_FF // FFN_CB
    per = ts // 16

    def body(g_ref, gh_ref, wg_ref, bg_ref, v_ref, vh_ref, wv_ref, bv_ref, act_ref):
        first = (pl.program_id(1) > 0).astype(F32)
        gate, _ = _ffn_conv(g_ref, gh_ref, wg_ref, bg_ref, first)
        val, _ = _ffn_conv(v_ref, vh_ref, wv_ref, bv_ref, first)
        act_ref[...] = (gate * _sigmoid_tanh(gate) * val).astype(BF16)

    return pl.pallas_call(
        body, name=name, grid=(nblk, nt),
        in_specs=_ffn_specs(ts, lambda i: i, lambda i: jnp.maximum(i * per - 1, 0)),
        out_specs=pl.BlockSpec((ts, FFN_CB), lambda j, i: (i, j)),
        out_shape=jax.ShapeDtypeStruct((s, D_FF), BF16),
    )(up, up, w, b, up, up, w, b)


def _ffn_core_bwd(dact, up, w, b, *, ts=1024, name):
    s = up.shape[0]
    ts = _tile_rows(ts, s)
    nt = s // ts
    nblk = D_FF // FFN_CB
    per = ts // 16
    strip, halo = FFN_STRIP, FFN_HALO
    nstrips = ts // strip

    def fold(x):
        out = x[:SUBLANES]
        for r0 in range(SUBLANES, strip, SUBLANES):
            out = out + x[r0:r0 + SUBLANES]
        return out

    def body(da_ref, g_ref, gh_ref, wg_ref, bg_ref, v_ref, vh_ref, wv_ref, bv_ref,
             dg_ref, dv_ref, dwg_ref, dwv_ref, dbg_ref, dbv_ref, nxg_ref, nxv_ref, ug_ref, uv_ref):
        i = pl.program_id(1)
        first = nt - 1 - i > 0

        @pl.when(i == 0)
        def _():
            for ref in (nxg_ref, nxv_ref, dwg_ref, dwv_ref, dbg_ref, dbv_ref):
                ref[...] = jnp.zeros_like(ref)

        _ffn_stage(g_ref, gh_ref, ug_ref, first)
        _ffn_stage(v_ref, vh_ref, uv_ref, first)
        wg, wv, bg, bv = wg_ref[...], wv_ref[...], bg_ref[...], bv_ref[...]
        conv = _ffn_strip_conv

        def conv_t(d, nxt, w):
            out = w[2:3] * d
            for k in range(2):
                out = out + w[k:k + 1] * _shift_up(d, nxt, 2 - k)
            return out

        def step(t, carry):
            nxg, nxv, awg, awv, abg, abv = carry
            r = pl.multiple_of((nstrips - 1 - t) * strip, strip)
            gate, g_sh = conv(ug_ref, r, wg, bg)
            val, v_sh = conv(uv_ref, r, wv, bv)
            da = da_ref[pl.ds(r, strip), :].astype(F32)
            sg = _sigmoid_tanh(gate)
            dgate = da * val * (sg * (1.0 + gate * (1.0 - sg)))
            dval = da * (gate * sg)
            dg_ref[pl.ds(r, strip), :] = conv_t(dgate, nxg, wg).astype(BF16)
            dv_ref[pl.ds(r, strip), :] = conv_t(dval, nxv, wv).astype(BF16)
            awg = tuple(a + fold(dgate * sh) for a, sh in zip(awg, g_sh))
            awv = tuple(a + fold(dval * sh) for a, sh in zip(awv, v_sh))
            return dgate[:SUBLANES], dval[:SUBLANES], awg, awv, abg + fold(dgate), abv + fold(dval)

        zero = jnp.zeros((SUBLANES, FFN_CB), F32)
        init = (nxg_ref[...], nxv_ref[...], (zero,) * 3, (zero,) * 3, zero, zero)
        nxg, nxv, awg, awv, abg, abv = lax.fori_loop(0, nstrips, step, init)
        nxg_ref[...] = nxg
        nxv_ref[...] = nxv
        for k in range(3):
            dwg_ref[k:k + 1, :] += jnp.sum(awg[k], axis=0, keepdims=True)
            dwv_ref[k:k + 1, :] += jnp.sum(awv[k], axis=0, keepdims=True)
        dbg_ref[...] += jnp.sum(abg, axis=0, keepdims=True)
        dbv_ref[...] += jnp.sum(abv, axis=0, keepdims=True)

    def rev(i):
        return nt - 1 - i

    tile = pl.BlockSpec((ts, FFN_CB), lambda j, i: (rev(i), j))
    w_out = pl.BlockSpec((3, FFN_CB), lambda j, i: (0, j))
    b_out = pl.BlockSpec((1, FFN_CB), lambda j, i: (0, j))
    f = jax.ShapeDtypeStruct
    return pl.pallas_call(
        body, name=name, grid=(nblk, nt),
        in_specs=[tile] + _ffn_specs(ts, rev, lambda i: jnp.maximum(rev(i) * per - 1, 0)),
        out_specs=(tile, tile, w_out, w_out, b_out, b_out),
        out_shape=(f((s, D_FF), BF16), f((s, D_FF), BF16), f((3, D_FF), F32), f((3, D_FF), F32),
                   f((1, D_FF), F32), f((1, D_FF), F32)),
        scratch_shapes=[pltpu.VMEM((SUBLANES, FFN_CB), F32), pltpu.VMEM((SUBLANES, FFN_CB), F32),
                        pltpu.VMEM((ts + halo, FFN_CB), BF16), pltpu.VMEM((ts + halo, FFN_CB), BF16)],
    )(dact, up, up, w, b, up, up, w, b)


def _sgu_forward_block(zu, zg, gn, w_ref, bias, seg):
    u, tu = _gelu(zu)
    g, tg = _gelu(zg)
    ms = _dot_split(g * g, seg)
    rs = lax.rsqrt(ms + EPS)
    ghat = g * rs
    gv = ghat * gn
    gvb = gv.astype(BF16)
    lane = _lanes((CHUNK, LANES))
    chunks = []
    for c in range(zu.shape[0] // CHUNK):
        gc = gvb[c * CHUNK:(c + 1) * CHUNK]
        mix = jnp.where(lane < 64, _dot(w_ref[0], gc), _dot(w_ref[1], gc)) + bias
        chunks.append(mix)
    mixed = chunks[0] if len(chunks) == 1 else jnp.concatenate(chunks, axis=0)
    return u, tu, g, tg, rs, ghat, gvb, mixed


def _sgu_fwd(p1, gn, w, bias, seg, *, ts=512, name):
    s = p1.shape[0]
    ts = _tile_rows(ts, s)

    def body(zu_ref, zg_ref, gn_ref, w_ref, bias_ref, seg_ref, yc_ref):
        u, _, _, _, _, _, _, mixed = _sgu_forward_block(
            zu_ref[...], zg_ref[...], gn_ref[...], w_ref, bias_ref[...], seg_ref[...])
        yc_ref[...] = (u * mixed).astype(BF16)

    return pl.pallas_call(
        body, name=name, grid=(4, s // ts),
        in_specs=[pl.BlockSpec((ts, LANES), lambda j, i: (i, j)),
                  pl.BlockSpec((ts, LANES), lambda j, i: (i, 4 + j)),
                  pl.BlockSpec((1, LANES), lambda j, i: (0, j)),
                  pl.BlockSpec((2, CHUNK, CHUNK), lambda j, i: (j, 0, 0)),
                  pl.BlockSpec((CHUNK, LANES), lambda j, i: (0, j)),
                  pl.BlockSpec((LANES, LANES), lambda j, i: (0, 0))],
        out_specs=pl.BlockSpec((ts, LANES), lambda j, i: (i, j)),
        out_shape=jax.ShapeDtypeStruct((s, 4 * LANES), BF16),
    )(p1, p1, gn, w, bias, seg)


def _sgu_bwd(p1, dy, gn, w, wt, bias, seg, tril, *, ts=512, name):
    s = p1.shape[0]
    ts = _tile_rows(ts, s)
    nt = s // ts

    def body(zu_ref, zg_ref, dy_ref, gn_ref, w_ref, wt_ref, bias_ref, seg_ref, tril_ref,
             dzu_ref, dzg_ref, dw_ref, dbias_ref, dgn_ref):
        i = pl.program_id(1)
        zu = zu_ref[...]
        zg = zg_ref[...]
        gn_v = gn_ref[...]
        segv = seg_ref[...]
        u, tu, g, tg, rs, ghat, gvb, mixed = _sgu_forward_block(zu, zg, gn_v, w_ref, bias_ref[...], segv)
        dyv = dy_ref[...]
        du = dyv * mixed
        dmx = dyv * u

        @pl.when(i == 0)
        def _():
            dw_ref[...] = jnp.zeros_like(dw_ref)
            dbias_ref[...] = jnp.zeros_like(dbias_ref)
            dgn_ref[...] = jnp.zeros_like(dgn_ref)

        lane = _lanes((CHUNK, LANES))
        dgv_chunks = []
        dbias = jnp.zeros((CHUNK, LANES), F32)
        for c in range(ts // CHUNK):
            dmc = dmx[c * CHUNK:(c + 1) * CHUNK]
            gc = gvb[c * CHUNK:(c + 1) * CHUNK]
            dm_a = jnp.where(lane < 64, dmc, 0.0).astype(BF16)
            dm_b = jnp.where(lane >= 64, dmc, 0.0).astype(BF16)
            dw_ref[0] += _dot_nt(dm_a, gc)
            dw_ref[1] += _dot_nt(dm_b, gc)
            dgv_chunks.append(_dot(wt_ref[0], dm_a) + _dot(wt_ref[1], dm_b))
            dbias = dbias + dmc
        dbias_ref[...] += dbias
        dgv = dgv_chunks[0] if len(dgv_chunks) == 1 else jnp.concatenate(dgv_chunks, axis=0)
        dgn_ref[...] += jnp.sum(dgv * ghat, axis=0, keepdims=True)
        dgh = dgv * gn_v
        dg = rs * (dgh - ghat * _dot_split(dgh * ghat, segv))
        dzu_ref[...] = (du * _gelu_grad(zu, tu)).astype(BF16)
        dzg_ref[...] = (dg * _gelu_grad(zg, tg)).astype(BF16)

        @pl.when(i == nt - 1)
        def _():
            dw_ref[0] = dw_ref[0] * tril_ref[...]
            dw_ref[1] = dw_ref[1] * tril_ref[...]

    f = jax.ShapeDtypeStruct
    colj = pl.BlockSpec((ts, LANES), lambda j, i: (i, j))
    wsp = pl.BlockSpec((2, CHUNK, CHUNK), lambda j, i: (j, 0, 0))
    sq = pl.BlockSpec((LANES, LANES), lambda j, i: (0, 0))
    return pl.pallas_call(
        body, name=name, grid=(4, nt),
        in_specs=[colj, pl.BlockSpec((ts, LANES), lambda j, i: (i, 4 + j)), colj,
                  pl.BlockSpec((1, LANES), lambda j, i: (0, j)), wsp, wsp,
                  pl.BlockSpec((CHUNK, LANES), lambda j, i: (0, j)), sq, sq],
        out_specs=(colj, colj, wsp, pl.BlockSpec((CHUNK, LANES), lambda j, i: (0, j)),
                   pl.BlockSpec((1, LANES), lambda j, i: (0, j))),
        out_shape=(f((s, 4 * LANES), BF16), f((s, 4 * LANES), BF16), f((8, CHUNK, CHUNK), F32),
                   f((CHUNK, 4 * LANES), F32), f((1, 4 * LANES), F32)),
    )(p1, p1, dy, gn, w, wt, bias, seg, tril)


F_COL = 20


def _fcum_fwd(p1, bf, *, ts=512, name):
    s = p1.shape[0]
    ts = _tile_rows(ts, s)

    def body(f_ref, bf_ref, c_ref, car_ref):
        i = pl.program_id(0)
        z = f_ref[...] + bf_ref[...]
        logf = jnp.minimum(z, 0.0) - _log1p_pos(jnp.exp(-jnp.abs(z)))

        @pl.when(i == 0)
        def _():
            car_ref[...] = jnp.zeros_like(car_ref)

        c_ref[...] = _cumsum_fwd(logf) + car_ref[0:1, :]
        car_ref[0:1, :] = c_ref[ts - 1:ts, :]

    return pl.pallas_call(
        body, name=name, grid=(s // ts,),
        in_specs=[pl.BlockSpec((ts, LANES), lambda i: (i, F_COL)), pl.BlockSpec((1, LANES), lambda i: (0, 0))],
        out_specs=pl.BlockSpec((ts, LANES), lambda i: (i, 0)),
        out_shape=jax.ShapeDtypeStruct((s, LANES), F32),
        scratch_shapes=[pltpu.VMEM((SUBLANES, LANES), F32)],
    )(p1, bf)


def _fcum_bwd(dcs, dcq, p1, bf, *, ts=512, name):
    s = p1.shape[0]
    ts = _tile_rows(ts, s)
    nt = s // ts

    def body(dc_ref, dcq_ref, f_ref, bf_ref, df_ref, dbf_ref, car_ref):
        i = pl.program_id(0)

        @pl.when(i == 0)
        def _():
            car_ref[...] = jnp.zeros_like(car_ref)
            dbf_ref[...] = jnp.zeros_like(dbf_ref)

        dc = dc_ref[...]
        lane = _lanes((ts, LANES))
        for h in range(8):
            dc = dc + jnp.where(lane == h, dcq_ref[:, h * LANES:(h + 1) * LANES], 0.0)
        dlog = _cumsum_rev(dc) + car_ref[0:1, :]
        car_ref[...] = dlog[:SUBLANES]
        z = f_ref[...] + bf_ref[...]
        df = dlog * _sigmoid(-z)
        df_ref[...] = df.astype(BF16)
        dbf_ref[...] += jnp.sum(df, axis=0, keepdims=True)

    return pl.pallas_call(
        body, name=name, grid=(nt,),
        in_specs=[pl.BlockSpec((ts, LANES), lambda i: (nt - 1 - i, 0)),
                  pl.BlockSpec((ts, 8 * LANES), lambda i: (nt - 1 - i, 0)),
                  pl.BlockSpec((ts, LANES), lambda i: (nt - 1 - i, F_COL)),
                  pl.BlockSpec((1, LANES), lambda i: (0, 0))],
        out_specs=(pl.BlockSpec((ts, LANES), lambda i: (nt - 1 - i, 0)), pl.BlockSpec((1, LANES), lambda i: (0, 0))),
        out_shape=(jax.ShapeDtypeStruct((s, LANES), BF16), jax.ShapeDtypeStruct((1, LANES), F32)),
        scratch_shapes=[pltpu.VMEM((SUBLANES, LANES), F32)],
    )(dcs, dcq, p1, bf)


def _fox_scores(qm, kb, bias, ck, diagonal):
    sc = _dot_nt(qm, kb) + bias - ck
    if diagonal:
        sc = jnp.where(_lanes(sc.shape) <= _rows(sc.shape), sc, NEG)
    return sc


def _head_masks(shape):
    lane = _lanes(shape)
    return lane < 64, lane >= 64


def _fox_fwd(p1, cq, ck, *, tq=512, name):
    s = p1.shape[0]
    tq = _tile_rows(tq, s)
    tk = tq
    nq = s // tq

    def body(q_ref, k_ref, v_ref, cq_ref, ck_ref, o_ref, lb_ref):
        qi = pl.program_id(1)
        q = q_ref[...] * 0.125
        first, second = _head_masks((tq, LANES))
        qms = [jnp.where(sel, q, 0.0).astype(BF16) for sel in (first, second)]
        cqs = [cq_ref[:, hh * LANES:(hh + 1) * LANES] for hh in range(2)]
        biases = [jnp.tile(cqh, (1, tk // LANES)) for cqh in cqs]

        def step(kj, carry, diagonal):
            cols = pl.ds(pl.multiple_of(kj * tk, tk), tk)
            kb = k_ref[cols, :].astype(BF16)
            vb = v_ref[cols, :].astype(BF16)
            new, outs = [], []
            acc = carry[4]
            for hh in range(2):
                m_prev, l_prev = carry[2 * hh], carry[2 * hh + 1]
                sc = _fox_scores(qms[hh], kb, biases[hh], ck_ref[hh, :, cols], diagonal)
                m_new = jnp.maximum(m_prev, jnp.max(sc, axis=1, keepdims=True))
                pm = jnp.exp(sc - jnp.tile(m_new, (1, tk // LANES)))
                alpha = jnp.exp(m_prev - m_new)
                new += [m_new, alpha * l_prev + jnp.sum(pm, axis=1, keepdims=True)]
                outs.append(acc * alpha + _dot(pm.astype(BF16), vb))
            return tuple(new) + (jnp.where(first, outs[0], outs[1]),)

        zero = jnp.zeros((tq, LANES), F32)
        low = jnp.full((tq, LANES), NEG, F32)
        carry = lax.fori_loop(0, qi, lambda kj, c: step(kj, c, False), (low, zero, low, zero, zero))
        m0, l0, m1, l1, acc = step(qi, carry, True)
        o_ref[...] = (acc / jnp.where(first, l0, l1)).astype(BF16)
        lb_ref[:, 0:LANES] = cqs[0] - (m0 + jnp.log(l0))
        lb_ref[:, LANES:2 * LANES] = cqs[1] - (m1 + jnp.log(l1))

    return pl.pallas_call(
        body, name=name, grid=(4, nq),
        in_specs=[pl.BlockSpec((tq, LANES), lambda j, qi: (qi, 8 + j)),
                  pl.BlockSpec((s, LANES), lambda j, qi: (0, 12 + j)),
                  pl.BlockSpec((s, LANES), lambda j, qi: (0, 16 + j)),
                  pl.BlockSpec((tq, 2 * LANES), lambda j, qi: (qi, j)),
                  pl.BlockSpec((2, 1, s), lambda j, qi: (j, 0, 0))],
        out_specs=(pl.BlockSpec((tq, LANES), lambda j, qi: (qi, j)),
                   pl.BlockSpec((tq, 2 * LANES), lambda j, qi: (qi, j))),
        out_shape=(jax.ShapeDtypeStruct((s, 4 * LANES), BF16), jax.ShapeDtypeStruct((s, 8 * LANES), F32)),
    )(p1, p1, p1, cq, ck)


def _fox_delta(dy, o, sel, *, ts=512, name):
    s = o.shape[0]
    ts = _tile_rows(ts, s)

    def body(do_ref, o_ref, sel_ref, d_ref):
        prod = do_ref[...] * o_ref[...].astype(F32)
        d_ref[:, 0:LANES] = _dot_split(prod, sel_ref[0])
        d_ref[:, LANES:2 * LANES] = _dot_split(prod, sel_ref[1])

    return pl.pallas_call(
        body, name=name, grid=(4, s // ts),
        in_specs=[pl.BlockSpec((ts, LANES), lambda j, i: (i, 4 + j)),
                  pl.BlockSpec((ts, LANES), lambda j, i: (i, j)),
                  pl.BlockSpec((2, LANES, LANES), lambda j, i: (0, 0, 0))],
        out_specs=pl.BlockSpec((ts, 2 * LANES), lambda j, i: (i, j)),
        out_shape=jax.ShapeDtypeStruct((s, 8 * LANES), F32),
    )(dy, o, sel)


def _fox_bwd(p1, dy, lb, delta, ck, *, tq=512, name):
    s = p1.shape[0]
    tq = _tile_rows(tq, s)
    tk = tq
    nq = s // tq

    def body(q_ref, k_ref, v_ref, do_ref, lb_ref, dl_ref, ck_ref,
             dq_ref, dk_ref, dv_ref, dck_ref, dcq_ref, dqa_ref, dra_ref):
        kj = pl.program_id(1)

        @pl.when(kj == 0)
        def _():
            dqa_ref[...] = jnp.zeros_like(dqa_ref)
            dra_ref[...] = jnp.zeros_like(dra_ref)

        kf = k_ref[...]
        kb = kf.astype(BF16)
        vb = v_ref[...].astype(BF16)
        first, second = _head_masks((tk, LANES))
        kms = [jnp.where(sel, kf, 0.0).astype(BF16) for sel in (first, second)]
        cks = [ck_ref[hh] for hh in range(2)]

        def step(qi, carry, diagonal):
            dk_acc, dv_acc, dc0, dc1 = carry
            dcs = [dc0, dc1]
            rows = pl.ds(pl.multiple_of(qi * tq, tq), tq)
            q = q_ref[rows, :] * 0.125
            do = do_ref[rows, :]
            for hh, sel in enumerate((first, second)):
                qm = jnp.where(sel, q, 0.0).astype(BF16)
                dom = jnp.where(sel, do, 0.0).astype(BF16)
                bias = jnp.tile(lb_ref[rows, hh * LANES:(hh + 1) * LANES], (1, tk // LANES))
                pm = jnp.exp(_fox_scores(qm, kb, bias, cks[hh], diagonal))
                dv_acc = dv_acc + _dot_tn(pm.astype(BF16), dom)
                dp = _dot_nt(dom, vb)
                ds = pm * (dp - jnp.tile(dl_ref[rows, hh * LANES:(hh + 1) * LANES], (1, tk // LANES)))
                dsb = ds.astype(BF16)
                dk_acc = dk_acc + _dot_tn(dsb, qm)
                dcs[hh] = dcs[hh] - jnp.sum(ds, axis=0, keepdims=True)
                dqa_ref[rows, :] += _dot(dsb, kms[hh])
                dra_ref[hh, rows, :] += jnp.sum(ds, axis=1, keepdims=True)
            return dk_acc, dv_acc, dcs[0], dcs[1]

        zero = jnp.zeros((tk, LANES), F32)
        zrow = jnp.zeros((1, tk), F32)
        carry = step(kj, (zero, zero, zrow, zrow), True)
        dk_acc, dv_acc, dc0, dc1 = lax.fori_loop(kj + 1, nq, lambda qi, c: step(qi, c, False), carry)
        dk_ref[...] = dk_acc.astype(BF16)
        dv_ref[...] = dv_acc.astype(BF16)
        dck_ref[0] = dc0
        dck_ref[1] = dc1

        @pl.when(kj == nq - 1)
        def _():
            dq_ref[...] = (dqa_ref[...] * 0.125).astype(BF16)
            dcq_ref[:, 0:LANES] = dra_ref[0]
            dcq_ref[:, LANES:2 * LANES] = dra_ref[1]

    def full(width, col0):
        return pl.BlockSpec((s, width), lambda j, kj: (0, col0 + j))

    kblk = pl.BlockSpec((tk, LANES), lambda j, kj: (kj, j))
    f = jax.ShapeDtypeStruct
    return pl.pallas_call(
        body, name=name, grid=(4, nq),
        in_specs=[full(LANES, 8),
                  pl.BlockSpec((tk, LANES), lambda j, kj: (kj, 12 + j)),
                  pl.BlockSpec((tk, LANES), lambda j, kj: (kj, 16 + j)),
                  full(LANES, 4), full(2 * LANES, 0), full(2 * LANES, 0),
                  pl.BlockSpec((2, 1, tk), lambda j, kj: (j, 0, kj))],
        out_specs=(full(LANES, 0), kblk, kblk, pl.BlockSpec((2, 1, tk), lambda j, kj: (j, 0, kj)),
                   full(2 * LANES, 0)),
        out_shape=(f((s, 4 * LANES), BF16), f((s, 4 * LANES), BF16), f((s, 4 * LANES), BF16),
                   f((8, 1, s), F32), f((s, 8 * LANES), F32)),
        scratch_shapes=[pltpu.VMEM((s, LANES), F32), pltpu.VMEM((2, s, LANES), F32)],
    )(p1, p1, p1, dy, lb, delta, ck)


def _row_block(r, cap=256):
    best = None
    for rb in range(2 * SUBLANES, min(r, cap) + 1, 2 * SUBLANES):
        if r % rb == 0:
            best = rb
    return r if best is None else best


def _adamw(w, g, m, v, *, name):
    r, c = w.shape
    rb = _row_block(r)

    def body(w_ref, g_ref, m_ref, v_ref, d_ref, nm_ref, nv_ref):
        gv = g_ref[...]
        mn = ADAM_B1 * m_ref[...] + (1.0 - ADAM_B1) * gv
        vn = ADAM_B2 * v_ref[...] + (1.0 - ADAM_B2) * (gv * gv)
        m_hat = mn / ADAM_C1
        v_hat = vn / ADAM_C2
        d_ref[...] = (-ADAM_LR) * (m_hat / (jnp.sqrt(v_hat) + ADAM_EPS) + ADAM_WD * w_ref[...])
        nm_ref[...] = mn
        nv_ref[...] = vn

    blk = pl.BlockSpec((rb, c), lambda i: (i, 0))
    shp = jax.ShapeDtypeStruct((r, c), F32)
    return pl.pallas_call(
        body, name=name, grid=(r // rb,), in_specs=[blk] * 4, out_specs=(blk,) * 3, out_shape=(shp,) * 3,
    )(w, g, m, v)


def _adamw_halves(w, mine, theirs, m, v, core, *, name):
    layers, r, c = w.shape
    rh = r // 2
    rb = _row_block(rh)
    per = rh // rb

    def body(core_ref, w_ref, *refs):
        g_refs = refs[:2 * layers]
        m_ref, v_ref, g_ref, d_ref, nm_ref, nv_ref = refs[2 * layers:]
        own = pl.program_id(1) == core_ref[0]
        gv = jnp.where(own, g_refs[0][...], g_refs[layers][...])
        for l in range(1, layers):
            gv = jnp.where(pl.program_id(0) == l, jnp.where(own, g_refs[l][...], g_refs[layers + l][...]), gv)
        g_ref[...] = gv
        mn = ADAM_B1 * m_ref[...] + (1.0 - ADAM_B1) * gv
        vn = ADAM_B2 * v_ref[...] + (1.0 - ADAM_B2) * (gv * gv)
        m_hat = mn / ADAM_C1
        v_hat = vn / ADAM_C2
        d_ref[...] = (-ADAM_LR) * (m_hat / (jnp.sqrt(v_hat) + ADAM_EPS) + ADAM_WD * w_ref[...])
        nm_ref[...] = mn
        nv_ref[...] = vn

    full = pl.BlockSpec((None, rb, c), lambda l, h, i, core_ref: (l, h * per + i, 0))
    half = pl.BlockSpec((rb, c), lambda l, h, i, core_ref: (i, 0))
    shp = jax.ShapeDtypeStruct((layers, r, c), F32)
    return pl.pallas_call(
        body, name=name,
        grid_spec=pltpu.PrefetchScalarGridSpec(
            num_scalar_prefetch=1, grid=(layers, 2, per),
            in_specs=[full] + [half] * (2 * layers) + [full, full], out_specs=(full,) * 4),
        out_shape=(shp,) * 4,
    )(core, w, *mine, *theirs, m, v)


def _pair_specs(col, rb, c):
    if col:
        g_spec = pl.BlockSpec((None, rb, c), lambda t, i, sel: (sel[0], i, sel[1 + t]))
    else:
        g_spec = pl.BlockSpec((None, None, rb, c), lambda t, i, sel: (sel[1 + t], sel[0], i, 0))
    return g_spec, pl.BlockSpec((None, rb, c), lambda t, i, sel: (sel[1 + t], i, 0))


def _pair_sum(g, col, ra, sel, after, *, name):
    _, rh, c = ra.shape
    rb = _row_block(rh)

    def body(sel_ref, g_ref, ra_ref, after_ref, h16_ref):
        h16_ref[...] = (g_ref[...] + ra_ref[...]).astype(BF16)

    g_spec, ra_spec = _pair_specs(col, rb, c)
    return pl.pallas_call(
        body, name=name,
        grid_spec=pltpu.PrefetchScalarGridSpec(
            num_scalar_prefetch=1, grid=(2, rh // rb), in_specs=[g_spec, ra_spec, ANY],
            out_specs=pl.BlockSpec((None, rb, c), lambda t, i, sel: (t, i, 0))),
        out_shape=jax.ShapeDtypeStruct((2, rh, c), BF16),
    )(sel, g, ra, after)


def _first_sum(g, col, ra, r1, sel, after, *, name):
    _, rh, c = ra.shape
    rb = _row_block(rh)

    def body(sel_ref, g_ref, ra_ref, r_ref, after_ref, s_ref, s16_ref):
        tot = (g_ref[...] + ra_ref[...]) + r_ref[...].astype(F32)
        s_ref[...] = tot
        s16_ref[...] = tot.astype(BF16)

    g_spec, ra_spec = _pair_specs(col, rb, c)
    slot = pl.BlockSpec((None, rb, c), lambda t, i, sel_ref: (t, i, 0))
    return pl.pallas_call(
        body, name=name,
        grid_spec=pltpu.PrefetchScalarGridSpec(
            num_scalar_prefetch=1, grid=(2, rh // rb), in_specs=[g_spec, ra_spec, slot, ANY],
            out_specs=(slot, slot)),
        out_shape=(jax.ShapeDtypeStruct((2, rh, c), F32), jax.ShapeDtypeStruct((2, rh, c), BF16)),
    )(sel, g, ra, r1, after)


def _second_sum(s1, r2, mine, after, *, name):
    _, rh, c = s1.shape
    rb = _row_block(rh)

    def body(mine_ref, s_ref, r_ref, after_ref, t_ref):
        t_ref[...] = s_ref[...] + r_ref[...].astype(F32)

    flat = pl.BlockSpec((rb, c), lambda i, mine_ref: (i, 0))
    return pl.pallas_call(
        body, name=name,
        grid_spec=pltpu.PrefetchScalarGridSpec(
            num_scalar_prefetch=1, grid=(rh // rb,),
            in_specs=[pl.BlockSpec((None, rb, c), lambda i, mine_ref: (mine_ref[0], i, 0)), flat, ANY],
            out_specs=flat),
        out_shape=jax.ShapeDtypeStruct((rh, c), F32),
    )(mine, s1, r2, after)


def _place(shard, col, chip, dtype, *, name):
    r, c = shard.shape
    rh = r // 2
    rb = _row_block(rh)

    def body(chip_ref, s_ref, o_ref):
        o_ref[...] = s_ref[...].astype(o_ref.dtype)

    if col:
        out_spec = pl.BlockSpec((None, rb, c), lambda h, i, chip_ref: (h, i, chip_ref[0]))
        shape = (2, rh, N_CHIPS * c)
    else:
        out_spec = pl.BlockSpec((None, None, rb, c), lambda h, i, chip_ref: (chip_ref[0], h, i, 0))
        shape = (N_CHIPS, 2, rh, c)
    per = rh // rb
    return pl.pallas_call(
        body, name=name,
        grid_spec=pltpu.PrefetchScalarGridSpec(
            num_scalar_prefetch=1, grid=(2, per),
            in_specs=[pl.BlockSpec((rb, c), lambda h, i, chip_ref: (h * per + i, 0))], out_specs=out_spec),
        out_shape=jax.ShapeDtypeStruct(shape, dtype),
    )(chip, shard)


ANY = pl.BlockSpec(memory_space=pl.ANY)


def _mesh_pos():
    return lax.axis_index("x"), lax.axis_index("y"), lax.axis_index("c")


def _other_chips(x, y):
    return [(1 - x, y), (x, 1 - y), (1 - x, 1 - y)]


def _remote(src, dst, ssem, rsem, dev):
    return pltpu.make_async_remote_copy(src_ref=src, dst_ref=dst, send_sem=ssem, recv_sem=rsem,
                                        device_id=dev, device_id_type=MESH)


def _flip(a, b):
    return a + b - 2 * a * b


def _handshake(peers):
    barrier = pltpu.get_barrier_semaphore()
    for peer in peers:
        pl.semaphore_signal(barrier, inc=1, device_id=peer, device_id_type=MESH)
    pl.semaphore_wait(barrier, len(peers))


def _slab(ref, col, width, k, h):
    if not col:
        return ref.at[k, h]
    start = k * width if isinstance(k, int) else pl.multiple_of(k * width, LANES)
    return ref.at[h, :, pl.ds(start, width)]


def _all_gather(bufs, cols, *, collective_id, name):
    n = len(bufs)
    widths = [b.shape[2] // N_CHIPS if col else b.shape[3] for b, col in zip(bufs, cols)]
    outs = [jax.new_ref(b, memory_space=pltpu.MemorySpace.HBM) for b in bufs]

    def body(ssem, rsem):
        x, y, c = _mesh_pos()
        me = 2 * x + y
        sib = (x, y, 1 - c)
        n1 = (_flip(x, 1 - c), _flip(y, c))
        n2 = (_flip(x, c), _flip(y, 1 - c))
        k1 = 2 * n1[0] + n1[1]
        k2 = 2 * n2[0] + n2[1]
        kd = 2 * (1 - x) + (1 - y)
        _handshake([n1 + (c,), n2 + (c,), sib])

        def slab(a, k, h):
            return _slab(outs[a], cols[a], widths[a], k, h)

        def copy(a, j, src, dst, dev):
            return _remote(src, dst, ssem.at[a, j], rsem.at[a, j], dev)

        sends = []
        for a in range(n):
            for j, nb in ((0, n1), (1, n2)):
                own = slab(a, me, c)
                cp = copy(a, j, own, own, nb + (c,))
                cp.start()
                sends.append(cp)
        arrivals = ((0, k1, n1, 3), (1, k2, n2, 4), (2, kd, n2, 5))
        for j, k, nb, fwd in arrivals:
            for a in range(n):
                got = slab(a, k, c)
                copy(a, j, got, got, nb + (c,)).wait_recv()
                if j == 0:
                    cp = copy(a, 2, got, got, n2 + (c,))
                    cp.start()
                    sends.append(cp)
                cp = copy(a, fwd, got, got, sib)
                cp.start()
                sends.append(cp)
        for fwd, k in ((3, k2), (4, k1), (5, kd)):
            for a in range(n):
                got = slab(a, k, 1 - c)
                copy(a, fwd, got, got, sib).wait_recv()
        for cp in sends:
            cp.wait_send()

    _sequencer_call(body, (), [(n, 6), (n, 6)], collective_id, name)()
    return [ref[...] for ref in outs]


def _sequencer_call(body, out_types, sem_shapes, collective_id, name):
    return pl.kernel(
        body, name=name, out_type=out_types,
        mesh=plsc.ScalarSubcoreMesh(axis_name="sequencer", num_cores=1),
        scratch_types=[pltpu.SemaphoreType.DMA(shape) for shape in sem_shapes],
        compiler_params=pltpu.CompilerParams(collective_id=collective_id))


def _send_other_half(grads, cols, *, collective_id, name):
    n = len(grads)

    def shard_shape(g, col):
        if col:
            return (g.shape[1], g.shape[2] // N_CHIPS)
        return g.shape[2:]

    shapes = [shard_shape(g, col) for g, col in zip(grads, cols)]

    def body(*refs):
        ins, outs = refs[:n], refs[n:2 * n]
        ssem, rsem = refs[2 * n:]
        x, y, c = _mesh_pos()
        sib = (x, y, 1 - c)
        _handshake([sib])
        sends = []
        for a in range(n):
            for k in range(N_CHIPS):
                src = _slab(ins[a], cols[a], shapes[a][1], k, 1 - c)
                cp = _remote(src, outs[a].at[k], ssem.at[a, k], rsem.at[a, k], sib)
                cp.start()
                sends.append(cp)
        for cp in sends:
            cp.wait()

    out_types = [jax.ShapeDtypeStruct((N_CHIPS,) + shp, g.dtype) for g, shp in zip(grads, shapes)]
    return _sequencer_call(body, out_types, [(n, N_CHIPS), (n, N_CHIPS)], collective_id, name)(*grads)


def _send_first(sums, *, collective_id, name):
    n = len(sums)

    def body(*refs):
        ins, outs = refs[:n], refs[n:2 * n]
        ssem, rsem = refs[2 * n:]
        x, y, c = _mesh_pos()
        nb = (_flip(x, c), _flip(y, 1 - c), c)
        _handshake([nb])
        sends = []
        for a in range(n):
            for t in range(2):
                cp = _remote(ins[a].at[t], outs[a].at[t], ssem.at[a, t], rsem.at[a, t], nb)
                cp.start()
                sends.append(cp)
        for cp in sends:
            cp.wait()

    out_types = [jax.ShapeDtypeStruct(h.shape, h.dtype) for h in sums]
    return _sequencer_call(body, out_types, [(n, 2), (n, 2)], collective_id, name)(*sums)


def _send_second(sums, *, collective_id, name):
    n = len(sums)

    def body(*refs):
        ins, outs = refs[:n], refs[n:2 * n]
        ssem, rsem = refs[2 * n:]
        x, y, c = _mesh_pos()
        nb = (_flip(x, 1 - c), _flip(y, c), c)
        other = 1 - (c * y + (1 - c) * x)
        _handshake([nb])
        sends = []
        for a in range(n):
            cp = _remote(ins[a].at[other], outs[a], ssem.at[a], rsem.at[a], nb)
            cp.start()
            sends.append(cp)
        for cp in sends:
            cp.wait()

    out_types = [jax.ShapeDtypeStruct(s.shape[1:], s.dtype) for s in sums]
    return _sequencer_call(body, out_types, [(n,), (n,)], collective_id, name)(*sums)


def _swap_halves(halves, *, collective_id, name):
    n = len(halves)

    def body(*refs):
        ins, outs = refs[:n], refs[n:2 * n]
        ssem, rsem = refs[2 * n:]
        x, y, c = _mesh_pos()
        sib = (x, y, 1 - c)
        _handshake([sib])
        cps = []
        for a in range(n):
            cp = _remote(ins[a], outs[a], ssem.at[a], rsem.at[a], sib)
            cp.start()
            cps.append(cp)
        for cp in cps:
            cp.wait()

    out_types = [jax.ShapeDtypeStruct(h.shape, h.dtype) for h in halves]
    return _sequencer_call(body, out_types, [(n,), (n,)], collective_id, name)(*halves)


def _all_reduce_small(buf, *, name):
    r = buf.shape[0]
    rh = r // 2

    def body(in_ref, out_ref, x1_ref, x2_ref, ssem, rsem):
        x, y, c = _mesh_pos()
        me = 2 * x + y
        sib = (x, y, 1 - c)
        chips = _other_chips(x, y)
        cp = _remote(in_ref, x1_ref, ssem.at[0], rsem.at[0], sib)
        cp.start()
        cp.wait()
        off = pl.multiple_of(c * rh, SUBLANES)
        x2_ref[me] = in_ref[pl.ds(off, rh), :] + x1_ref[pl.ds(off, rh), :]
        sends = []
        for j, (cx, cy) in enumerate(chips):
            s = _remote(x2_ref.at[me], x2_ref.at[me], ssem.at[1 + j], rsem.at[1 + j], (cx, cy, c))
            s.start()
            sends.append(s)
        for j, (cx, cy) in enumerate(chips):
            slot = x2_ref.at[2 * cx + cy]
            _remote(slot, slot, ssem.at[1 + j], rsem.at[1 + j], (cx, cy, c)).wait_recv()
        out_ref[pl.ds(off, rh), :] = ((x2_ref[0] + x2_ref[1]) + x2_ref[2]) + x2_ref[3]
        for s in sends:
            s.wait_send()
        mine = out_ref.at[pl.ds(off, rh), :]
        s3 = _remote(mine, mine, ssem.at[4], rsem.at[4], sib)
        s3.start()
        off2 = pl.multiple_of((1 - c) * rh, SUBLANES)
        theirs = out_ref.at[pl.ds(off2, rh), :]
        _remote(theirs, theirs, ssem.at[4], rsem.at[4], sib).wait_recv()
        s3.wait_send()

    vm = pl.BlockSpec(memory_space=pltpu.VMEM)
    return pl.pallas_call(
        body, name=name, in_specs=[vm], out_specs=vm,
        out_shape=jax.ShapeDtypeStruct((r, LANES), F32),
        scratch_shapes=[pltpu.VMEM((r, LANES), F32), pltpu.VMEM((N_CHIPS, rh, LANES), F32),
                        pltpu.SemaphoreType.DMA((5,)), pltpu.SemaphoreType.DMA((5,))],
    )(buf)


PACK_ALIGN = 2 * SUBLANES * LANES


def _pack(arrays, rows_multiple=2 * SUBLANES):
    parts, offs, off = [], [], 0
    for a in arrays:
        flat = a.reshape(-1).astype(F32)
        padded = -(-flat.shape[0] // PACK_ALIGN) * PACK_ALIGN
        parts.append(jnp.pad(flat, (0, padded - flat.shape[0])))
        offs.append(off)
        off += padded
    buf = jnp.concatenate(parts).reshape(-1, LANES)
    return buf, offs


def _unpack(buf, offs, shapes):
    flat = buf.reshape(-1)
    out = []
    for off, shp in zip(offs, shapes):
        size = 1
        for d in shp:
            size *= d
        out.append(flat[off:off + size].reshape(shp))
    return out


def _cols_from_shards(g4):
    _, k, ns = g4.shape
    return jnp.transpose(g4, (1, 0, 2)).reshape(k, N_CHIPS * ns)


def _cols_to_shards(w):
    k, n = w.shape
    return jnp.transpose(w.reshape(k, N_CHIPS, n // N_CHIPS), (1, 0, 2))


def _block_cols(w, parts, blocks):
    lead = w.shape[:-1]
    width = w.shape[-1] // (parts * blocks)
    w = w.reshape(lead + (parts, blocks, width))
    w = jnp.swapaxes(w, -3, -2)
    return w.reshape(lead + (parts * blocks * width,))


def _unblock_cols(w, parts, blocks):
    lead = w.shape[:-1]
    width = w.shape[-1] // (parts * blocks)
    w = w.reshape(lead + (blocks, parts, width))
    w = jnp.swapaxes(w, -3, -2)
    return w.reshape(lead + (parts * blocks * width,))


def _pair_blockdiag(w8):
    w = w8.reshape(4, 2, 64, 64)
    z = jnp.zeros((4, 64, 64), w8.dtype)
    top = jnp.concatenate([w[:, 0], z], axis=2)
    bot = jnp.concatenate([z, w[:, 1]], axis=2)
    return jnp.concatenate([top, bot], axis=1)


def _pair_diag_blocks(w4):
    a = w4[:, :64, :64]
    b = w4[:, 64:, 64:]
    return jnp.stack([a, b], axis=1).reshape(8, 64, 64)


def _local_step(x, target, wts, on_event=None):
    s = x.shape[0]
    g = {}

    def event(name, token):
        if on_event is not None:
            on_event(name, g, token)

    win0 = wts["w_in0"]
    wout0 = wts["w_out0"]
    win1 = wts["w_in1"]
    wout1 = wts["w_out1"]
    wup = wts["w_up"]
    wdown = wts["w_down"]
    w4, b4, w3, b3 = wts["w4"], wts["b4"], wts["w3"], wts["b3"]
    wa, wx = wts["wa"], wts["wx"]
    wat, wxt = jnp.swapaxes(wa, 1, 2), jnp.swapaxes(wx, 1, 2)
    ba, bx, lam = wts["ba"], wts["bx"], wts["lam"]
    fcw, fcb = wts["ffn_cw"], wts["ffn_cb"]
    sgu_w, sgu_wt = wts["sgu_w"], wts["sgu_wt"]
    sgu_bias, sgu_gn = wts["sgu_bias"], wts["sgu_gn"]
    bf = wts["bf"]

    lane = jnp.arange(LANES)
    seg = jnp.where((lane[:, None] // 64) == (lane[None, :] // 64), 1.0 / 64.0, 0.0).astype(BF16)
    sel = jnp.stack([jnp.broadcast_to((lane[:, None] < 64), (LANES, LANES)),
                     jnp.broadcast_to((lane[:, None] >= 64), (LANES, LANES))]).astype(BF16)
    tril = (lane[:, None] >= lane[None, :]).astype(F32)

    n0 = _norm_fwd(x, wts["g_mix0"], name="norm_mix0")
    p0 = _mm([n0], win0, nb=1280, name="mm_in0")
    ya, yb, hl = _even_core_fwd(p0, w4, b4, wa, ba, wx, bx, lam, w3, b3, name="even_fwd")
    h1, n1 = _mm([ya, yb], wout0, res=x, norm_out=wts["g_ffn"][0], name="mm_out0")

    def ffn_fwd(h, n, layer, next_gain):
        up = _mm([n], wup[layer], out_dtype=BF16, ts=1024, nb=1408, name=f"mm_up{layer}")
        act = _ffn_core_fwd(up, fcw[layer], fcb[layer], name=f"ffn_fwd{layer}")
        if next_gain is None:
            return up, act, _mm([act], wdown[layer], res=h, name=f"mm_down{layer}"), None
        hn, nn = _mm([act], wdown[layer], res=h, norm_out=next_gain, name=f"mm_down{layer}")
        return up, act, hn, nn

    up0, act0, h2, n2 = ffn_fwd(h1, n1, 0, wts["g_mix1"])

    p1 = _mm([n2], win1, name="mm_in1")
    yc = _sgu_fwd(p1, sgu_gn, sgu_w, sgu_bias, seg, name="sgu_fwd")
    cum = _fcum_fwd(p1, bf, name="fcum_fwd")
    c8 = cum[:, :8]
    cq = jnp.broadcast_to(c8[:, :, None], (s, 8, LANES)).reshape(s, 8 * LANES)
    ck = jnp.transpose(c8).reshape(8, 1, s)
    yd, lb = _fox_fwd(p1, cq, ck, name="fox_fwd")
    h3, n3 = _mm([yc, yd], wout1, res=h2, norm_out=wts["g_ffn"][1], name="mm_out1")

    up1, act1, h4, _ = ffn_fwd(h3, n3, 1, None)
    dh4, loss, g["final_norm"] = _final(h4, wts["g_final"], target, name="final")

    def ffn_bwd(dh, h, n, up, act, layer):
        dact = _mm([dh], wdown[layer], trans_w=True, out_dtype=BF16, ts=1024, nb=1408, name=f"mm_dact{layer}")
        g[f"w_down{layer}"] = _mm_tn([act], [dh], ts=1024, nb=512, name=f"mm_dwdown{layer}")
        event(f"dwdown{layer}", g[f"w_down{layer}"])
        dgate, dval, dcwg, dcwv, dcbg, dcbv = _ffn_core_bwd(dact, up, fcw[layer], fcb[layer], name=f"ffn_bwd{layer}")
        event(f"ffn_bwd{layer}", dgate)
        g[f"w_up{layer}"] = _mm_tn([n], [dgate, dval], ts=1024, nb=1408, name=f"mm_dwup{layer}")
        event(f"dwup{layer}", g[f"w_up{layer}"])
        dhn, g[f"g_ffn{layer}"] = _mm([dgate, dval], wup[layer], trans_w=True, ts=512,
                                      norm_bwd=(h, wts["g_ffn"][layer], dh), name=f"mm_dn_ffn{layer}")
        g[f"ffn_cw{layer}"] = jnp.concatenate([dcwg, dcwv], axis=1)
        g[f"ffn_cb{layer}"] = jnp.concatenate([dcbg, dcbv], axis=1)
        return dhn

    dh3 = ffn_bwd(dh4, h3, n3, up1, act1, 1)

    dy1 = _mm([dh3], wout1, trans_w=True, ts=1024, name="mm_dy1")
    g["w_out1"] = _mm_tn([yc, yd], [dh3], ts=1024, nb=512, name="mm_dwout1")
    event("dwout1", g["w_out1"])
    dzu, dzg, g["sgu_w"], g["sgu_bias"], g["sgu_gn"] = _sgu_bwd(
        p1, dy1, sgu_gn, sgu_w, sgu_wt, sgu_bias, seg, tril, name="sgu_bwd")
    delta = _fox_delta(dy1, yd, sel, name="fox_delta")
    dq, dk, dv, dck, dcq = _fox_bwd(p1, dy1, lb, delta, ck, name="fox_bwd")
    event("fox_bwd", dq)
    dcs = jnp.pad(jnp.transpose(dck.reshape(8, s)), ((0, 0), (0, LANES - 8)))
    df, g["bf"] = _fcum_bwd(dcs, dcq, p1, bf, name="fcum_bwd")
    dp1 = jnp.concatenate([dzu, dzg, dq, dk, dv, df], axis=1)
    g["w_in1"] = _mm_tn([n2], [dp1], ts=1024, nb=896, name="mm_dwin1")
    event("dwin1", g["w_in1"])
    dh2, g["g_mix1"] = _mm([dp1], win1, trans_w=True, norm_bwd=(h2, wts["g_mix1"], dh3), name="mm_dn_mix1")

    dh1 = ffn_bwd(dh2, h1, n1, up0, act0, 0)

    dy0 = _mm([dh1], wout0, trans_w=True, ts=1024, name="mm_dy0")
    g["w_out0"] = _mm_tn([ya, yb], [dh1], ts=1024, nb=512, name="mm_dwout0")
    event("dwout0", g["w_out0"])
    (dp0, g["w4"], g["b4"], g["wa"], g["ba"], g["wx"], g["bx"], g["lam"], g["w3"], g["b3"]) = _even_core_bwd(
        dy0, p0, hl, w4, b4, wa, wat, ba, wx, wxt, bx, lam, w3, b3, name="even_bwd")
    event("even_bwd", dp0)
    g["w_in0"] = _mm_tn([n0], [dp0], ts=1024, nb=640, name="mm_dwin0")
    event("dwin0", g["w_in0"])
    grad_x, g["g_mix0"] = _mm([dp0], win0, trans_w=True, norm_bwd=(x, wts["g_mix0"], dh1), name="mm_dn_mix0")
    return loss, grad_x, g


def _prepare_weights(nat):
    lane = jnp.arange(LANES)
    tril = (lane[:, None] >= lane[None, :]).astype(F32)
    sgu_tril = nat["sgu_w"][0] * tril
    w_in1 = nat["mix1_w_in"]
    nblk = D_FF // FFN_CB
    return {
        "w_in0": _block_cols(nat["mix0_w_in"], 5, 4),
        "w_out0": nat["mix0_w_out"],
        "w_in1": jnp.pad(w_in1, ((0, 0), (0, 21 * LANES - w_in1.shape[1]))),
        "w_out1": nat["mix1_w_out"],
        "w_up": [nat["ffn_up"][l] for l in range(2)],
        "w_down": [nat["ffn_down"][l] for l in range(2)],
        "w4": nat["lru_conv_w"], "b4": nat["lru_conv_b"], "w3": nat["sconv_w"], "b3": nat["sconv_b"],
        "wa": _pair_blockdiag(nat["lru_wa"][0]).astype(BF16), "wx": _pair_blockdiag(nat["lru_wx"][0]).astype(BF16),
        "ba": nat["lru_ba"], "bx": nat["lru_bx"], "lam": nat["lru_lambda"],
        "ffn_cw": [nat["ffn_conv_w"][l] for l in range(2)],
        "ffn_cb": [nat["ffn_conv_b"][l:l + 1] for l in range(2)],
        "sgu_w": sgu_tril.astype(BF16), "sgu_wt": jnp.swapaxes(sgu_tril, 1, 2).astype(BF16),
        "sgu_bias": jnp.repeat(jnp.transpose(nat["sgu_b"][0]), 64, axis=1), "sgu_gn": nat["sgu_norm"],
        "bf": jnp.pad(nat["fox_bf"], ((0, 0), (0, LANES - 8))),
        "g_mix0": nat["mix0_norm"], "g_mix1": nat["mix1_norm"],
        "g_ffn": [nat["ffn_norm"][0:1], nat["ffn_norm"][1:2]], "g_final": nat["final_norm"].reshape(1, D_MODEL),
    }


def _natural_grads(g):
    nblk = D_FF // FFN_CB
    small = {
        "mix0_norm": g["g_mix0"], "lru_conv_b": g["b4"],
        "lru_wa": _pair_diag_blocks(g["wa"])[None], "lru_ba": g["ba"],
        "lru_wx": _pair_diag_blocks(g["wx"])[None], "lru_bx": g["bx"],
        "lru_lambda": g["lam"], "sconv_b": g["b3"],
        "sgu_w": g["sgu_w"][None],
        "sgu_b": jnp.transpose(g["sgu_bias"].reshape(CHUNK, 8, 64).sum(axis=2))[None],
        "fox_bf": g["bf"][:, :8],
        "ffn_norm": jnp.concatenate([g["g_ffn0"], g["g_ffn1"]], axis=0),
        "ffn_conv_b": jnp.concatenate([g["ffn_cb0"], g["ffn_cb1"]], axis=0),
        "final_norm": g["final_norm"].reshape(D_MODEL),
        "lru_conv_w": g["w4"][None], "sconv_w": g["w3"][None],
        "ffn_conv_w": jnp.stack([g["ffn_cw0"], g["ffn_cw1"]]),
        "mix1_norm": g["g_mix1"], "sgu_norm": g["sgu_gn"],
    }
    big = {
        "mix0_w_in": _unblock_cols(g["w_in0"], 5, 4), "mix0_w_out": g["w_out0"],
        "mix1_w_in": g["w_in1"][:, :2568], "mix1_w_out": g["w_out1"],
        "ffn_up0": g["w_up0"], "ffn_up1": g["w_up1"],
        "ffn_down0": g["w_down0"], "ffn_down1": g["w_down1"],
    }
    return small, big


COL_SHARDED = ("mix0_w_in", "mix1_w_in", "ffn_up0", "ffn_up1")
COL_ALIGNED = ("mix0_w_in", "ffn_up0", "ffn_up1")
SMALL_SHARDED = ("lru_conv_w", "sconv_w", "ffn_conv_w", "mix1_norm", "sgu_norm")
SMALL_REPLICATED = ("mix0_norm", "lru_conv_b", "lru_wa", "lru_ba", "lru_wx", "lru_bx", "lru_lambda", "sconv_b",
                    "sgu_w", "sgu_b", "fox_bf", "ffn_norm", "ffn_conv_b", "final_norm")
BIG = ("mix0_w_in", "mix0_w_out", "mix1_w_in", "mix1_w_out", "ffn_up0", "ffn_up1", "ffn_down0", "ffn_down1")
WEIGHT_ORDER = ("mix0_norm", "mix0_w_in", "lru_conv_w", "lru_conv_b", "lru_wa", "lru_ba", "lru_wx", "lru_bx",
                "lru_lambda", "sconv_w", "sconv_b", "mix0_w_out", "mix1_norm", "mix1_w_in", "sgu_norm", "sgu_w",
                "sgu_b", "fox_bf", "mix1_w_out", "ffn_norm", "ffn_up", "ffn_conv_w", "ffn_conv_b", "ffn_down",
                "final_norm")


GATHER_GROUPS = (("mix0_w_in", "mix0_w_out"), ("ffn_up0",), ("ffn_down0", "mix1_w_in"),
                 ("mix1_w_out", "ffn_up1", "ffn_down1"))
CID_GATHER, CID_PAIR, CID_FIRST, CID_SECOND, CID_SWAP = 1, 2, 3, 4, 5


class _GradReducer:
    def __init__(self):
        x, y, c = _mesh_pos()
        self.send = jnp.stack([c] + [2 * (c * (1 - x) + (1 - c) * t) + (c * t + (1 - c) * (1 - y))
                                     for t in range(2)]).astype(jnp.int32)
        self.keep = jnp.stack([c] + [c * (2 * x + t) + (1 - c) * (2 * t + y) for t in range(2)]).astype(jnp.int32)
        self.mine = (c * y + (1 - c) * x).reshape(1).astype(jnp.int32)
        self.groups = {}

    @staticmethod
    def _view(name, a):
        if name in COL_ALIGNED:
            return a.reshape(2, a.shape[0] // 2, a.shape[1])
        if name in COL_SHARDED:
            a = _cols_to_shards(a)
            return a.reshape(N_CHIPS, 2, a.shape[1] // 2, a.shape[2])
        rows = a.shape[0] // (2 * N_CHIPS)
        return a.reshape(N_CHIPS, 2, rows, a.shape[1])

    def start(self, group, grads):
        names = tuple(grads)
        views = [self._view(k, grads[k]) for k in names]
        cols = [k in COL_ALIGNED for k in names]
        data = _send_other_half(views, cols, collective_id=CID_PAIR, name=f"rs_pair_{group}")
        self.groups[group] = dict(names=names, stage=0, views=views, cols=cols, data=data)

    def step(self, group, after):
        st = self.groups[group]
        names = st["names"]
        if st["stage"] == 0:
            sums = [_pair_sum(a, col, b, self.send, after, name=f"rs_pair_sum_{k}")
                    for k, a, col, b in zip(names, st["views"], st["cols"], st["data"])]
            st["from_sib"] = st["data"]
            st["data"] = _send_first(sums, collective_id=CID_FIRST, name=f"rs_first_{group}")
        elif st["stage"] == 1:
            sums = [_first_sum(a, col, b, r, self.keep, after, name=f"rs_first_sum_{k}")
                    for k, a, col, b, r in zip(names, st["views"], st["cols"], st["from_sib"], st["data"])]
            st["keep"] = [s32 for s32, _ in sums]
            st["data"] = _send_second([s16 for _, s16 in sums], collective_id=CID_SECOND, name=f"rs_second_{group}")
        else:
            st["mine"] = [_second_sum(s32, r, self.mine, after, name=f"rs_second_sum_{k}")
                          for k, s32, r in zip(names, st["keep"], st["data"])]
            st["data"] = _swap_halves(st["mine"], collective_id=CID_SWAP, name=f"rs_swap_{group}")
        st["stage"] += 1

    def result(self, group):
        st = self.groups[group]
        return {k: (a, b) for k, a, b in zip(st["names"], st["mine"], st["data"])}


def _train_step(x, target, w, m, v):
    x2 = x[0]
    t2 = target[0]
    chip = 2 * lax.axis_index("x") + lax.axis_index("y")
    core_arr = lax.axis_index("c").reshape(1).astype(jnp.int32)
    chip_arr = chip.reshape(1).astype(jnp.int32)

    big_shards = {
        "mix0_w_in": w["mix0_w_in"][0], "mix0_w_out": w["mix0_w_out"][0],
        "mix1_w_in": w["mix1_w_in"][0], "mix1_w_out": w["mix1_w_out"][0],
        "ffn_up0": w["ffn_up"][0], "ffn_up1": w["ffn_up"][1],
        "ffn_down0": w["ffn_down"][0], "ffn_down1": w["ffn_down"][1],
    }
    small_shards = [w[k] for k in SMALL_SHARDED]
    small_buf, small_offs = _pack(small_shards)
    full = {}
    small_all = None
    for gi, names in enumerate(GATHER_GROUPS):
        cols = [k in COL_ALIGNED for k in names]
        placed = [_place(big_shards[k], col, chip_arr, BF16, name=f"place_{k}") for k, col in zip(names, cols)]
        if gi == 0:
            placed.append(_place(small_buf, False, chip_arr, F32, name="place_small"))
            cols = cols + [False]
        gathered = _all_gather(placed, cols, collective_id=CID_GATHER, name=f"gather_weights{gi}")
        if gi == 0:
            small_all = gathered[-1].reshape(N_CHIPS, -1, LANES)
        for k, arr in zip(names, gathered):
            if k in COL_ALIGNED:
                full[k] = arr.reshape(arr.shape[0] * arr.shape[1], arr.shape[2])
            elif k in COL_SHARDED:
                full[k] = _cols_from_shards(arr.reshape((N_CHIPS, arr.shape[1] * arr.shape[2], arr.shape[3])))
            else:
                full[k] = arr.reshape(-1, arr.shape[3])
    per_chip = [_unpack(small_all[k], small_offs, [a.shape for a in small_shards]) for k in range(N_CHIPS)]
    lru_conv_w = jnp.concatenate([per_chip[k][0] for k in range(N_CHIPS)], axis=-1)[0]
    sconv_w = jnp.concatenate([per_chip[k][1] for k in range(N_CHIPS)], axis=-1)[0]
    ffn_conv_w = jnp.concatenate([per_chip[k][2] for k in range(N_CHIPS)], axis=-1)
    mix1_norm = jnp.concatenate([per_chip[k][3] for k in range(N_CHIPS)], axis=-1)
    sgu_norm = jnp.concatenate([per_chip[k][4] for k in range(N_CHIPS)], axis=-1)

    nat = {
        "mix0_w_in": full["mix0_w_in"], "mix0_w_out": full["mix0_w_out"],
        "mix1_w_in": full["mix1_w_in"], "mix1_w_out": full["mix1_w_out"],
        "ffn_up": [full["ffn_up0"], full["ffn_up1"]], "ffn_down": [full["ffn_down0"], full["ffn_down1"]],
        "lru_conv_w": lru_conv_w, "sconv_w": sconv_w, "ffn_conv_w": ffn_conv_w, "mix1_norm": mix1_norm,
        "sgu_norm": sgu_norm,
    }
    for k in SMALL_REPLICATED:
        nat[k] = w[k]
    wts = _prepare_weights(nat)

    reducer = _GradReducer()

    def on_event(name, g, token):
        if name == "dwup1":
            reducer.start("ffn1", {"ffn_up1": g["w_up1"], "ffn_down1": g["w_down1"]})
        elif name in ("dwout1", "fox_bwd"):
            reducer.step("ffn1", token)
        elif name == "dwin1":
            reducer.step("ffn1", token)
            reducer.start("mix1", {"mix1_w_in": g["w_in1"][:, :2568], "mix1_w_out": g["w_out1"]})
        elif name in ("dwdown0", "ffn_bwd0"):
            reducer.step("mix1", token)
        elif name == "dwup0":
            reducer.step("mix1", token)
            reducer.start("ffn0", {"ffn_up0": g["w_up0"], "ffn_down0": g["w_down0"]})
        elif name in ("dwout0", "even_bwd"):
            reducer.step("ffn0", token)
        elif name == "dwin0":
            reducer.step("ffn0", token)
            reducer.start("mix0", {"mix0_w_in": _unblock_cols(g["w_in0"], 5, 4), "mix0_w_out": g["w_out0"]})

    loss, grad_x, g = _local_step(x2, t2, wts, on_event)
    grads_small, _ = _natural_grads(g)

    small_names = SMALL_REPLICATED + SMALL_SHARDED
    small_list = [grads_small[k] for k in small_names] + [loss[:, :1]]
    sbuf, soffs = _pack(small_list)
    sred = _all_reduce_small(sbuf, name="reduce_small")
    small_red = _unpack(sred, soffs, [a.shape for a in small_list])
    loss_total = small_red[-1][0, 0]
    gsum = dict(zip(small_names, small_red[:-1]))
    for k in SMALL_SHARDED:
        width = w[k].shape[-1]
        gsum[k] = lax.dynamic_slice_in_dim(gsum[k], chip * width, width, axis=gsum[k].ndim - 1)

    out_g, out_d, out_m, out_v = {}, {}, {}, {}
    reduced = {}
    for group in ("ffn1", "mix1", "ffn0"):
        reduced.update(reducer.result(group))

    def update(pname, keys):
        mine = [reduced[k][0] for k in keys]
        theirs = [reduced[k][1] for k in keys]
        out_g[pname], out_d[pname], out_m[pname], out_v[pname] = _adamw_halves(
            w[pname], mine, theirs, m[pname], v[pname], core_arr, name=f"adamw_{pname}")
        return out_d[pname]

    reducer.step("mix0", update("ffn_up", ("ffn_up0", "ffn_up1")))
    small_w = [w[k] for k in small_names]
    pg, offs = _pack([gsum[k] for k in small_names])
    pw, _ = _pack(small_w)
    pm, _ = _pack([m[k] for k in small_names])
    pv, _ = _pack([v[k] for k in small_names])
    sd, sm, sv = _adamw(pw, pg, pm, pv, name="adamw_small")
    reducer.step("mix0", update("ffn_down", ("ffn_down0", "ffn_down1")))
    update("mix1_w_in", ("mix1_w_in",))
    reducer.step("mix0", update("mix1_w_out", ("mix1_w_out",)))
    reduced.update(reducer.result("mix0"))
    update("mix0_w_in", ("mix0_w_in",))
    update("mix0_w_out", ("mix0_w_out",))

    shapes = [a.shape for a in small_w]
    for k, dd, mm, vv in zip(small_names, _unpack(sd, offs, shapes), _unpack(sm, offs, shapes),
                             _unpack(sv, offs, shapes)):
        out_g[k], out_d[k], out_m[k], out_v[k] = gsum[k].reshape(w[k].shape), dd, mm, vv

    outs = [loss_total, grad_x[None]]
    for d in (out_g, out_d, out_m, out_v):
        outs.extend(d[k] for k in WEIGHT_ORDER)
    return tuple(outs)


def kernel(x, mix0_norm, mix0_w_in, lru_conv_w, lru_conv_b, lru_wa, lru_ba, lru_wx, lru_bx, lru_lambda, sconv_w, sconv_b, mix0_w_out, mix1_norm, mix1_w_in, sgu_norm, sgu_w, sgu_b, fox_bf, mix1_w_out, ffn_norm, ffn_up, ffn_conv_w, ffn_conv_b, ffn_down, final_norm, loss_target, m_mix0_norm, m_mix0_w_in, m_lru_conv_w, m_lru_conv_b, m_lru_wa, m_lru_ba, m_lru_wx, m_lru_bx, m_lru_lambda, m_sconv_w, m_sconv_b, m_mix0_w_out, m_mix1_norm, m_mix1_w_in, m_sgu_norm, m_sgu_w, m_sgu_b, m_fox_bf, m_mix1_w_out, m_ffn_norm, m_ffn_up, m_ffn_conv_w, m_ffn_conv_b, m_ffn_down, m_final_norm, v_mix0_norm, v_mix0_w_in, v_lru_conv_w, v_lru_conv_b, v_lru_wa, v_lru_ba, v_lru_wx, v_lru_bx, v_lru_lambda, v_sconv_w, v_sconv_b, v_mix0_w_out, v_mix1_norm, v_mix1_w_in, v_sgu_norm, v_sgu_w, v_sgu_b, v_fox_bf, v_mix1_w_out, v_ffn_norm, v_ffn_up, v_ffn_conv_w, v_ffn_conv_b, v_ffn_down, v_final_norm):
    w = dict(zip(WEIGHT_ORDER, (mix0_norm, mix0_w_in, lru_conv_w, lru_conv_b, lru_wa, lru_ba, lru_wx, lru_bx, lru_lambda, sconv_w, sconv_b, mix0_w_out, mix1_norm, mix1_w_in, sgu_norm, sgu_w, sgu_b, fox_bf, mix1_w_out, ffn_norm, ffn_up, ffn_conv_w, ffn_conv_b, ffn_down, final_norm)))
    m = dict(zip(WEIGHT_ORDER, (m_mix0_norm, m_mix0_w_in, m_lru_conv_w, m_lru_conv_b, m_lru_wa, m_lru_ba, m_lru_wx, m_lru_bx, m_lru_lambda, m_sconv_w, m_sconv_b, m_mix0_w_out, m_mix1_norm, m_mix1_w_in, m_sgu_norm, m_sgu_w, m_sgu_b, m_fox_bf, m_mix1_w_out, m_ffn_norm, m_ffn_up, m_ffn_conv_w, m_ffn_conv_b, m_ffn_down, m_final_norm)))
    v = dict(zip(WEIGHT_ORDER, (v_mix0_norm, v_mix0_w_in, v_lru_conv_w, v_lru_conv_b, v_lru_wa, v_lru_ba, v_lru_wx, v_lru_bx, v_lru_lambda, v_sconv_w, v_sconv_b, v_mix0_w_out, v_mix1_norm, v_mix1_w_in, v_sgu_norm, v_sgu_w, v_sgu_b, v_fox_bf, v_mix1_w_out, v_ffn_norm, v_ffn_up, v_ffn_conv_w, v_ffn_conv_b, v_ffn_down, v_final_norm)))
    return _train_step(x, loss_target, w, m, v)
```

```python
import functools

import jax
import jax.numpy as jnp
from jax import lax
from jax.experimental import pallas as pl
from jax.experimental.pallas import tpu as pltpu
from jax.experimental.pallas import tpu_sc as plsc

F32 = jnp.float32
BF16 = jnp.bfloat16
MESH = pl.DeviceIdType.MESH

D_MODEL = 1024
LANES = 128
SUBLANES = 8
N_CHIPS = 4
EPS = 1e-6
LRU_C = 8.0
D_FF = 2816
FFN_CB = 256
CHUNK = 128
NEG = -1e30

ADAM_LR = 0.001
ADAM_B1 = 0.9
ADAM_B2 = 0.999
ADAM_EPS = 1e-08
ADAM_WD = 0.01
ADAM_STEP = 10
ADAM_C1 = 1.0 - ADAM_B1 ** ADAM_STEP
ADAM_C2 = 1.0 - ADAM_B2 ** ADAM_STEP

_GELU_C = 0.7978845608028654
_GELU_A = 0.044715


def _sigmoid(x):
    return 1.0 / (1.0 + jnp.exp(-x))


def _sigmoid_tanh(x):
    return 0.5 * jnp.tanh(0.5 * x) + 0.5


def _log1p_pos(e):
    w = 1.0 + e
    return jnp.where(w == 1.0, e, jnp.log(w) * (e / (w - 1.0)))


def _softplus(x):
    return jnp.maximum(x, 0.0) + _log1p_pos(jnp.exp(-jnp.abs(x)))


def _gelu(x):
    t = jnp.tanh(_GELU_C * (x + _GELU_A * (x * x * x)))
    return 0.5 * x * (1.0 + t), t


def _gelu_grad(x, t):
    return 0.5 * (1.0 + t) + 0.5 * x * (1.0 - t * t) * (_GELU_C * (1.0 + 3.0 * _GELU_A * x * x))


def _rows(shape):
    return lax.broadcasted_iota(jnp.int32, shape, 0)


def _lanes(shape):
    return lax.broadcasted_iota(jnp.int32, shape, 1)


def _shift_down(x, halo8, j):
    if j == 0:
        return x
    r = pltpu.roll(x, j, 0)
    hr = pltpu.roll(halo8, j, 0)
    top = jnp.where(_rows(hr.shape) < j, hr, r[:SUBLANES])
    return jnp.concatenate([top, r[SUBLANES:]], axis=0)


def _shift_up(x, next8, j):
    if j == 0:
        return x
    n = x.shape[0]
    r = pltpu.roll(x, n - j, 0)
    nr = pltpu.roll(next8, SUBLANES - j, 0)
    bot = jnp.where(_rows(nr.shape) >= SUBLANES - j, nr, r[n - SUBLANES:])
    return jnp.concatenate([r[:n - SUBLANES], bot], axis=0)


def _scan_fwd(a, u):
    n = a.shape[0]
    row = _rows(a.shape)
    h = u
    k = 1
    while k < n:
        keep = row >= k
        h_sh = jnp.where(keep, pltpu.roll(h, k, 0), 0.0)
        a_sh = jnp.where(keep, pltpu.roll(a, k, 0), 1.0)
        h = a * h_sh + h
        a = a * a_sh
        k *= 2
    return h, a


def _scan_rev(b, d):
    n = b.shape[0]
    row = _rows(b.shape)
    g = d
    k = 1
    while k < n:
        keep = row < n - k
        g_sh = jnp.where(keep, pltpu.roll(g, n - k, 0), 0.0)
        b_sh = jnp.where(keep, pltpu.roll(b, n - k, 0), 1.0)
        g = b * g_sh + g
        b = b * b_sh
        k *= 2
    return g, b


def _cumsum_fwd(x):
    n = x.shape[0]
    row = _rows(x.shape)
    k = 1
    while k < n:
        x = x + jnp.where(row >= k, pltpu.roll(x, k, 0), 0.0)
        k *= 2
    return x


def _cumsum_rev(x):
    n = x.shape[0]
    row = _rows(x.shape)
    k = 1
    while k < n:
        x = x + jnp.where(row < n - k, pltpu.roll(x, n - k, 0), 0.0)
        k *= 2
    return x


def _dot(a, b):
    return lax.dot_general(a, b, (((1,), (0,)), ((), ())), preferred_element_type=F32)


def _dot_nt(a, b):
    return lax.dot_general(a, b, (((1,), (1,)), ((), ())), preferred_element_type=F32)


def _dot_tn(a, b):
    return lax.dot_general(a, b, (((0,), (0,)), ((), ())), preferred_element_type=F32)


def _dot_split(x, m_bf16):
    hi = x.astype(BF16)
    lo = (x - hi.astype(F32)).astype(BF16)
    return _dot(hi, m_bf16) + _dot(lo, m_bf16)


def _tile_rows(ts, s):
    return min(ts, s)


def _mm(a_list, w, *, trans_w=False, res=None, norm_bwd=None, norm_out=None, out_dtype=F32, ts=512, nb=None,
        name):
    s = a_list[0].shape[0]
    ks = [a.shape[1] for a in a_list]
    k = sum(ks)
    n = w.shape[0] if trans_w else w.shape[1]
    ts = _tile_rows(ts, s)
    nb = n if nb is None else nb
    na = len(a_list)
    has_res = res is not None
    fused = norm_bwd is not None
    normed = norm_out is not None
    offs = [sum(ks[:p]) for p in range(na)]

    def body(*refs):
        a_refs = refs[:na]
        w_ref = refs[na]
        acc = None
        for a_ref, off, kk in zip(a_refs, offs, ks):
            a = a_ref[...].astype(BF16)
            if trans_w:
                part = _dot_nt(a, w_ref[:, off:off + kk])
            else:
                part = _dot(a, w_ref[off:off + kk, :])
            acc = part if acc is None else acc + part
        if has_res:
            acc = acc + refs[na + 1][...]
        if normed:
            gn_ref, o_ref, n_ref = refs[-3:]
            o_ref[...] = acc.astype(out_dtype)
            r = lax.rsqrt(jnp.mean(acc * acc, axis=-1, keepdims=True) + EPS)
            n_ref[...] = ((acc * r) * gn_ref[...]).astype(BF16)
            return
        if not fused:
            refs[-1][...] = acc.astype(out_dtype)
            return
        h_ref, g_ref, dres_ref, dh_ref, dg_ref = refs[na + 1:]
        i = pl.program_id(1)
        x = h_ref[...]
        r = lax.rsqrt(jnp.mean(x * x, axis=-1, keepdims=True) + EPS)
        xhat = x * r
        part = jnp.sum(acc * xhat, axis=0, keepdims=True)

        @pl.when(i == 0)
        def _():
            dg_ref[...] = part

        @pl.when(i > 0)
        def _():
            dg_ref[...] += part

        dxh = acc * g_ref[...]
        dh_ref[...] = dres_ref[...] + r * (dxh - xhat * jnp.mean(dxh * xhat, axis=-1, keepdims=True))

    in_specs = [pl.BlockSpec((ts, kk), lambda j, i: (i, 0)) for kk in ks]
    if trans_w:
        in_specs.append(pl.BlockSpec((nb, k), lambda j, i: (j, 0)))
    else:
        in_specs.append(pl.BlockSpec((k, nb), lambda j, i: (0, j)))
    args = list(a_list) + [w]
    tile = pl.BlockSpec((ts, nb), lambda j, i: (i, j))
    if has_res:
        in_specs.append(tile)
        args.append(res)
    if fused:
        assert nb == n and not has_res
        vec = pl.BlockSpec((1, n), lambda j, i: (0, 0))
        h, g, dres = norm_bwd
        return pl.pallas_call(
            body, name=name, grid=(1, s // ts), in_specs=in_specs + [tile, vec, tile],
            out_specs=(tile, vec),
            out_shape=(jax.ShapeDtypeStruct((s, n), F32), jax.ShapeDtypeStruct((1, n), F32)),
        )(*args, h, g, dres)
    if normed:
        assert nb == n and out_dtype == F32
        vec = pl.BlockSpec((1, n), lambda j, i: (0, 0))
        return pl.pallas_call(
            body, name=name, grid=(1, s // ts), in_specs=in_specs + [vec], out_specs=(tile, tile),
            out_shape=(jax.ShapeDtypeStruct((s, n), F32), jax.ShapeDtypeStruct((s, n), BF16)),
        )(*args, norm_out)
    return pl.pallas_call(
        body, name=name, grid=(n // nb, s // ts), in_specs=in_specs, out_specs=tile,
        out_shape=jax.ShapeDtypeStruct((s, n), out_dtype),
    )(*args)


def _mm_tn(a_list, b_list, *, ts=512, nb=None, name):
    s = b_list[0].shape[0]
    ks = [a.shape[1] for a in a_list]
    k = sum(ks)
    width = b_list[0].shape[1]
    n = width * len(b_list)
    ts = _tile_rows(ts, s)
    nb = width if nb is None else nb
    per = width // nb
    na = len(a_list)
    nparts = len(b_list)

    def body(*refs):
        a_refs = refs[:na]
        b_refs = refs[na:na + nparts]
        o_ref = refs[-1]
        j = pl.program_id(0)
        i = pl.program_id(1)
        parts = [r[...].astype(BF16) for r in a_refs]
        a = parts[0] if na == 1 else jnp.concatenate(parts, axis=1)

        def accumulate(b_ref):
            upd = _dot_tn(a, b_ref[...].astype(BF16))

            @pl.when(i == 0)
            def _():
                o_ref[...] = upd

            @pl.when(i > 0)
            def _():
                o_ref[...] += upd

        if nparts == 1:
            accumulate(b_refs[0])
        else:
            for part, b_ref in enumerate(b_refs):
                pl.when(j // per == part)(functools.partial(accumulate, b_ref))

    in_specs = [pl.BlockSpec((ts, kk), lambda j, i: (i, 0)) for kk in ks]
    for part in range(nparts):
        in_specs.append(pl.BlockSpec(
            (ts, nb), lambda j, i, part=part: (i, jnp.clip(j - part * per, 0, per - 1))))
    return pl.pallas_call(
        body, name=name, grid=(n // nb, s // ts), in_specs=in_specs,
        out_specs=pl.BlockSpec((k, nb), lambda j, i: (0, j)),
        out_shape=jax.ShapeDtypeStruct((k, n), F32),
    )(*a_list, *b_list)


def _norm_fwd(h, g, *, ts=512, name):
    s, d = h.shape
    ts = _tile_rows(ts, s)

    def body(h_ref, g_ref, n_ref):
        x = h_ref[...]
        r = lax.rsqrt(jnp.mean(x * x, axis=-1, keepdims=True) + EPS)
        n_ref[...] = ((x * r) * g_ref[...]).astype(BF16)

    return pl.pallas_call(
        body, name=name, grid=(s // ts,),
        in_specs=[pl.BlockSpec((ts, d), lambda i: (i, 0)), pl.BlockSpec((1, d), lambda i: (0, 0))],
        out_specs=pl.BlockSpec((ts, d), lambda i: (i, 0)),
        out_shape=jax.ShapeDtypeStruct((s, d), BF16),
    )(h, g)


def _final(h, g, target, *, ts=512, name):
    s, d = h.shape
    ts = _tile_rows(ts, s)
    nt = s // ts

    def body(h_ref, g_ref, t_ref, dh_ref, loss_ref, dg_ref, acc_ref):
        i = pl.program_id(0)
        x = h_ref[...]
        r = lax.rsqrt(jnp.mean(x * x, axis=-1, keepdims=True) + EPS)
        xhat = x * r
        gv = g_ref[...]
        err = xhat * gv - t_ref[...]
        sq = jnp.sum(err * err, axis=0, keepdims=True)
        dy = err * (1.0 / d)
        part = jnp.sum(dy * xhat, axis=0, keepdims=True)

        @pl.when(i == 0)
        def _():
            acc_ref[...] = sq
            dg_ref[...] = part

        @pl.when(i > 0)
        def _():
            acc_ref[...] += sq
            dg_ref[...] += part

        dxh = dy * gv
        dh_ref[...] = r * (dxh - xhat * jnp.mean(dxh * xhat, axis=-1, keepdims=True))

        @pl.when(i == nt - 1)
        def _():
            tot = jnp.sum(acc_ref[...], axis=1, keepdims=True) * (0.5 / d)
            loss_ref[...] = jnp.broadcast_to(tot, (1, LANES))

    tile = pl.BlockSpec((ts, d), lambda i: (i, 0))
    vec = pl.BlockSpec((1, d), lambda i: (0, 0))
    return pl.pallas_call(
        body, name=name, grid=(nt,), in_specs=[tile, vec, tile],
        out_specs=(tile, pl.BlockSpec((1, LANES), lambda i: (0, 0)), vec),
        out_shape=(jax.ShapeDtypeStruct((s, d), F32), jax.ShapeDtypeStruct((1, LANES), F32),
                   jax.ShapeDtypeStruct((1, d), F32)),
        scratch_shapes=[pltpu.VMEM((1, d), F32)],
    )(h, g, target)


def _halo_map(ts, width_blocks):
    per = ts // SUBLANES

    def index(j, i):
        return (jnp.maximum(i * per - 1, 0), width_blocks(j))

    return index


def _even_gates(xc, wa, ba, wx, bx, sp):
    xb = xc.astype(BF16)
    r = _sigmoid(_dot(xb, wa) + ba)
    ig = _sigmoid(_dot(xb, wx) + bx)
    la = (-LRU_C) * r * sp
    a = jnp.exp(la)
    a2 = a * a
    m = jnp.sqrt(-jnp.tanh(la) * (1.0 + a2))
    return r, ig, la, a, a2, m


def _even_core_fwd(p, w4, b4, wa, ba, wx, bx, lam, w3, b3, *, ts=512, name):
    s = p.shape[0]
    ts = _tile_rows(ts, s)
    nt = s // ts
    nblk = 4

    def body(xa_ref, ga_ref, cp_ref, bp_ref, vb_ref, xah_ref, cph_ref, vbh_ref,
             w4_ref, b4_ref, wa_ref, ba_ref, wx_ref, bx_ref, lam_ref, w3_ref, b3_ref,
             ya_ref, yb_ref, hl_ref, hcar_ref):
        i = pl.program_id(1)
        first = (i > 0).astype(F32)
        xa, ga, cp, bp, vb = xa_ref[...], ga_ref[...], cp_ref[...], bp_ref[...], vb_ref[...]
        xa_h = xah_ref[...] * first
        s_h = cph_ref[...] * vbh_ref[...] * first

        xc = b4_ref[...] + w4_ref[3:4, :] * xa
        for k in range(3):
            xc = xc + w4_ref[k:k + 1, :] * _shift_down(xa, xa_h, 3 - k)
        sp = _softplus(-lam_ref[...])
        _, ig, _, a, _, m = _even_gates(xc, wa_ref[0], ba_ref[...], wx_ref[0], bx_ref[...], sp)
        u = m * (ig * xc)
        hs, acum = _scan_fwd(a, u)

        @pl.when(i == 0)
        def _():
            hcar_ref[...] = jnp.zeros_like(hcar_ref)

        hs = hs + acum * hcar_ref[0:1, :]
        hl_ref[...] = hs
        hcar_ref[0:1, :] = hl_ref[ts - 1:ts, :]
        ge, _ = _gelu(ga)
        ya_ref[...] = (hs * ge).astype(BF16)

        sv = cp * vb
        sc = b3_ref[...] + w3_ref[2:3, :] * sv
        for k in range(2):
            sc = sc + w3_ref[k:k + 1, :] * _shift_down(sv, s_h, 2 - k)
        yb_ref[...] = (bp * sc).astype(BF16)

    parts = [pl.BlockSpec((ts, LANES), lambda j, i, q=q: (i, 4 * q + j)) for q in range(5)]
    halos = [pl.BlockSpec((SUBLANES, LANES), _halo_map(ts, lambda j, q=q: 4 * q + j)) for q in (0, 2, 4)]
    vec = pl.BlockSpec((1, LANES), lambda j, i: (0, j))
    out = pl.BlockSpec((ts, LANES), lambda j, i: (i, j))
    return pl.pallas_call(
        body, name=name, grid=(nblk, nt),
        in_specs=parts + halos + [
                  pl.BlockSpec((4, LANES), lambda j, i: (0, j)), vec,
                  pl.BlockSpec((1, LANES, LANES), lambda j, i: (j, 0, 0)), vec,
                  pl.BlockSpec((1, LANES, LANES), lambda j, i: (j, 0, 0)), vec, vec,
                  pl.BlockSpec((3, LANES), lambda j, i: (0, j)), vec],
        out_specs=(out, out, out),
        out_shape=(jax.ShapeDtypeStruct((s, 4 * LANES), BF16), jax.ShapeDtypeStruct((s, 4 * LANES), BF16),
                   jax.ShapeDtypeStruct((s, 4 * LANES), F32)),
        scratch_shapes=[pltpu.VMEM((SUBLANES, LANES), F32)],
    )(*([p] * 8), w4, b4, wa, ba, wx, bx, lam, w3, b3)


def _even_core_bwd(dy, p, hl, w4, b4, wa, wat, ba, wx, wxt, bx, lam, w3, b3, *, ts=512, name):
    s = p.shape[0]
    ts = _tile_rows(ts, s)
    nt = s // ts
    nblk = 4
    per = ts // SUBLANES

    def body(dya_ref, dyb_ref, xa_ref, ga_ref, cp_ref, bp_ref, vb_ref, xah_ref, cph_ref, vbh_ref, hl_ref, hh_ref,
             w4_ref, b4_ref, wa_ref, wat_ref, ba_ref, wx_ref, wxt_ref, bx_ref, lam_ref, w3_ref, b3_ref,
             dxa_ref, dga_ref, dcp_ref, dbp_ref, dvb_ref,
             dw4_ref, db4_ref, dwa_ref, dba_ref, dwx_ref, dbx_ref, dlam_ref, dw3_ref, db3_ref,
             dxc_nx, dsc_nx, cg_ref):
        i = pl.program_id(1)
        ti = nt - 1 - i
        first = (ti > 0).astype(F32)
        xa, ga, cp, bp, vb = xa_ref[...], ga_ref[...], cp_ref[...], bp_ref[...], vb_ref[...]
        xa_h = xah_ref[...] * first
        s_h = cph_ref[...] * vbh_ref[...] * first
        h_h = hh_ref[...] * first

        @pl.when(i == 0)
        def _():
            dxc_nx[...] = jnp.zeros_like(dxc_nx)
            dsc_nx[...] = jnp.zeros_like(dsc_nx)
            cg_ref[...] = jnp.zeros_like(cg_ref)
            for ref in (dw4_ref, db4_ref, dwa_ref, dba_ref, dwx_ref, dbx_ref, dlam_ref, dw3_ref, db3_ref):
                ref[...] = jnp.zeros_like(ref)

        xa_sh = [_shift_down(xa, xa_h, 3 - k) for k in range(3)] + [xa]
        xc = b4_ref[...]
        for k in range(4):
            xc = xc + w4_ref[k:k + 1, :] * xa_sh[k]
        lamv = lam_ref[...]
        sp = _softplus(-lamv)
        r, ig, _, a, a2, m = _even_gates(xc, wa_ref[0], ba_ref[...], wx_ref[0], bx_ref[...], sp)
        sv = cp * vb
        sv_sh = [_shift_down(sv, s_h, 2 - k) for k in range(2)] + [sv]
        sc = b3_ref[...]
        for k in range(3):
            sc = sc + w3_ref[k:k + 1, :] * sv_sh[k]
        hs = hl_ref[...]
        h_prev = _shift_down(hs, h_h, 1)

        dya = dya_ref[...]
        dyb = dyb_ref[...]
        ge, gt = _gelu(ga)
        dga = dya * hs * _gelu_grad(ga, gt)
        dh = dya * ge

        ones8 = jnp.ones((SUBLANES, LANES), F32)
        b = _shift_up(a, ones8, 1)
        g, bcum = _scan_rev(b, dh)
        g = g + bcum * cg_ref[0:1, :]
        ag = a * g
        cg_ref[...] = ag[:SUBLANES]

        da = g * h_prev
        xi = ig * xc
        dm = g * xi
        dig = g * m * xc
        dxc = g * m * ig
        dla = da * a - dm * (a2 / m)
        dr = dla * ((-LRU_C) * sp)
        dlam_ref[...] += jnp.sum(dla * r, axis=0, keepdims=True) * (LRU_C * _sigmoid(-lamv))
        dra = dr * r * (1.0 - r)
        dia = dig * ig * (1.0 - ig)
        drab = dra.astype(BF16)
        diab = dia.astype(BF16)
        xcb = xc.astype(BF16)
        dxc = dxc + _dot(drab, wat_ref[0]) + _dot(diab, wxt_ref[0])
        dwa_ref[0] += _dot_tn(xcb, drab)
        dwx_ref[0] += _dot_tn(xcb, diab)
        dba_ref[...] += jnp.sum(dra, axis=0, keepdims=True)
        dbx_ref[...] += jnp.sum(dia, axis=0, keepdims=True)

        nx = dxc_nx[...]
        dxa = w4_ref[3:4, :] * dxc
        for k in range(3):
            dxa = dxa + w4_ref[k:k + 1, :] * _shift_up(dxc, nx, 3 - k)
        for k in range(4):
            dw4_ref[k:k + 1, :] += jnp.sum(dxc * xa_sh[k], axis=0, keepdims=True)
        db4_ref[...] += jnp.sum(dxc, axis=0, keepdims=True)
        dxc_nx[...] = dxc[:SUBLANES]

        dbp = dyb * sc
        dsc = dyb * bp
        nsc = dsc_nx[...]
        ds = w3_ref[2:3, :] * dsc
        for k in range(2):
            ds = ds + w3_ref[k:k + 1, :] * _shift_up(dsc, nsc, 2 - k)
        for k in range(3):
            dw3_ref[k:k + 1, :] += jnp.sum(dsc * sv_sh[k], axis=0, keepdims=True)
        db3_ref[...] += jnp.sum(dsc, axis=0, keepdims=True)
        dsc_nx[...] = dsc[:SUBLANES]

        dxa_ref[...] = dxa.astype(BF16)
        dga_ref[...] = dga.astype(BF16)
        dcp_ref[...] = (ds * vb).astype(BF16)
        dbp_ref[...] = dbp.astype(BF16)
        dvb_ref[...] = (ds * cp).astype(BF16)

    def rev(j, i):
        return (nt - 1 - i, j)

    def rev_halo(col):
        def index(j, i):
            return (jnp.maximum((nt - 1 - i) * per - 1, 0), col(j))
        return index

    parts = [pl.BlockSpec((ts, LANES), lambda j, i, q=q: (nt - 1 - i, 4 * q + j)) for q in range(5)]
    halos = [pl.BlockSpec((SUBLANES, LANES), rev_halo(lambda j, q=q: 4 * q + j)) for q in (0, 2, 4)]
    one = pl.BlockSpec((ts, LANES), rev)
    vec = pl.BlockSpec((1, LANES), lambda j, i: (0, j))
    mat = pl.BlockSpec((1, LANES, LANES), lambda j, i: (j, 0, 0))
    w4s = pl.BlockSpec((4, LANES), lambda j, i: (0, j))
    w3s = pl.BlockSpec((3, LANES), lambda j, i: (0, j))
    f = jax.ShapeDtypeStruct
    return pl.pallas_call(
        body, name=name, grid=(nblk, nt),
        in_specs=[one, pl.BlockSpec((ts, LANES), lambda j, i: (nt - 1 - i, 4 + j))] + parts + halos + [
                  one, pl.BlockSpec((SUBLANES, LANES), rev_halo(lambda j: j)),
                  w4s, vec, mat, mat, vec, mat, mat, vec, vec, w3s, vec],
        out_specs=(one,) * 5 + (w4s, vec, mat, vec, mat, vec, vec, w3s, vec),
        out_shape=(f((s, 4 * LANES), BF16),) * 5 + (
                   f((4, 4 * LANES), F32), f((1, 4 * LANES), F32),
                   f((4, LANES, LANES), F32), f((1, 4 * LANES), F32),
                   f((4, LANES, LANES), F32), f((1, 4 * LANES), F32), f((1, 4 * LANES), F32),
                   f((3, 4 * LANES), F32), f((1, 4 * LANES), F32)),
        scratch_shapes=[pltpu.VMEM((SUBLANES, LANES), F32), pltpu.VMEM((SUBLANES, LANES), F32),
                        pltpu.VMEM((SUBLANES, LANES), F32)],
    )(dy, dy, *([p] * 8), hl, hl, w4, b4, wa, wat, ba, wx, wxt, bx, lam, w3, b3)


def _ffn_conv(u_ref, uh_ref, w_ref, b_ref, first):
    u = u_ref[...].astype(F32)
    u_h = uh_ref[...].astype(F32)[SUBLANES:] * first
    u_sh = [_shift_down(u, u_h, 2 - k) for k in range(2)] + [u]
    hc = b_ref[...]
    for k in range(3):
        hc = hc + w_ref[k:k + 1, :] * u_sh[k]
    return hc, u_sh


def _ffn_specs(ts, row, halo_row):
    nblk = D_FF // FFN_CB
    specs = []
    for off in (0, nblk):
        specs.append(pl.BlockSpec((ts, FFN_CB), lambda j, i, off=off: (row(i), off + j)))
        specs.append(pl.BlockSpec((16, FFN_CB), lambda j, i, off=off: (halo_row(i), off + j)))
        specs.append(pl.BlockSpec((3, FFN_CB), lambda j, i, off=off: (0, off + j)))
        specs.append(pl.BlockSpec((1, FFN_CB), lambda j, i, off=off: (0, off + j)))
    return specs


FFN_STRIP = 4 * SUBLANES
FFN_HALO = 2 * SUBLANES


def _ffn_stage(u_ref, uh_ref, dst_ref, first):
    dst_ref[0:FFN_HALO, :] = jnp.where(first, uh_ref[...], jnp.zeros_like(uh_ref))
    dst_ref[FFN_HALO:, :] = u_ref[...]


def _ffn_strip_conv(u_ref, r, w, b):
    win = u_ref[pl.ds(r, FFN_HALO + FFN_STRIP), :].astype(F32)
    cur, before = win[FFN_HALO:], win[SUBLANES:FFN_HALO]
    sh = [_shift_down(cur, before, 2 - k) for k in range(2)] + [cur]
    return b + w[0:1] * sh[0] + w[1:2] * sh[1] + w[2:3] * sh[2], sh


def _ffn_core_fwd(up, w, b, *, ts=512, name):
    s = up.shape[0]
    ts = _tile_rows(ts, s)
    nt = s // ts
    nblk = D_FF // FFN_CB
    per = ts // 16

    def body(g_ref, gh_ref, wg_ref, bg_ref, v_ref, vh_ref, wv_ref, bv_ref, act_ref):
        first = (pl.program_id(1) > 0).astype(F32)
        gate, _ = _ffn_conv(g_ref, gh_ref, wg_ref, bg_ref, first)
        val, _ = _ffn_conv(v_ref, vh_ref, wv_ref, bv_ref, first)
        act_ref[...] = (gate * _sigmoid_tanh(gate) * val).astype(BF16)

    return pl.pallas_call(
        body, name=name, grid=(nblk, nt),
        in_specs=_ffn_specs(ts, lambda i: i, lambda i: jnp.maximum(i * per - 1, 0)),
        out_specs=pl.BlockSpec((ts, FFN_CB), lambda j, i: (i, j)),
        out_shape=jax.ShapeDtypeStruct((s, D_FF), BF16),
    )(up, up, w, b, up, up, w, b)


def _ffn_core_bwd(dact, up, w, b, *, ts=1024, name):
    s = up.shape[0]
    ts = _tile_rows(ts, s)
    nt = s // ts
    nblk = D_FF // FFN_CB
    per = ts // 16
    strip, halo = FFN_STRIP, FFN_HALO
    nstrips = ts // strip

    def fold(x):
        out = x[:SUBLANES]
        for r0 in range(SUBLANES, strip, SUBLANES):
            out = out + x[r0:r0 + SUBLANES]
        return out

    def body(da_ref, g_ref, gh_ref, wg_ref, bg_ref, v_ref, vh_ref, wv_ref, bv_ref,
             dg_ref, dv_ref, dwg_ref, dwv_ref, dbg_ref, dbv_ref, nxg_ref, nxv_ref, ug_ref, uv_ref):
        i = pl.program_id(1)
        first = nt - 1 - i > 0

        @pl.when(i == 0)
        def _():
            for ref in (nxg_ref, nxv_ref, dwg_ref, dwv_ref, dbg_ref, dbv_ref):
                ref[...] = jnp.zeros_like(ref)

        _ffn_stage(g_ref, gh_ref, ug_ref, first)
        _ffn_stage(v_ref, vh_ref, uv_ref, first)
        wg, wv, bg, bv = wg_ref[...], wv_ref[...], bg_ref[...], bv_ref[...]
        conv = _ffn_strip_conv

        def conv_t(d, nxt, w):
            out = w[2:3] * d
            for k in range(2):
                out = out + w[k:k + 1] * _shift_up(d, nxt, 2 - k)
            return out

        def step(t, carry):
            nxg, nxv, awg, awv, abg, abv = carry
            r = pl.multiple_of((nstrips - 1 - t) * strip, strip)
            gate, g_sh = conv(ug_ref, r, wg, bg)
            val, v_sh = conv(uv_ref, r, wv, bv)
            da = da_ref[pl.ds(r, strip), :].astype(F32)
            sg = _sigmoid_tanh(gate)
            dgate = da * val * (sg * (1.0 + gate * (1.0 - sg)))
            dval = da * (gate * sg)
            dg_ref[pl.ds(r, strip), :] = conv_t(dgate, nxg, wg).astype(BF16)
            dv_ref[pl.ds(r, strip), :] = conv_t(dval, nxv, wv).astype(BF16)
            awg = tuple(a + fold(dgate * sh) for a, sh in zip(awg, g_sh))
            awv = tuple(a + fold(dval * sh) for a, sh in zip(awv, v_sh))
            return dgate[:SUBLANES], dval[:SUBLANES], awg, awv, abg + fold(dgate), abv + fold(dval)

        zero = jnp.zeros((SUBLANES, FFN_CB), F32)
        init = (nxg_ref[...], nxv_ref[...], (zero,) * 3, (zero,) * 3, zero, zero)
        nxg, nxv, awg, awv, abg, abv = lax.fori_loop(0, nstrips, step, init)
        nxg_ref[...] = nxg
        nxv_ref[...] = nxv
        for k in range(3):
            dwg_ref[k:k + 1, :] += jnp.sum(awg[k], axis=0, keepdims=True)
            dwv_ref[k:k + 1, :] += jnp.sum(awv[k], axis=0, keepdims=True)
        dbg_ref[...] += jnp.sum(abg, axis=0, keepdims=True)
        dbv_ref[...] += jnp.sum(abv, axis=0, keepdims=True)

    def rev(i):
        return nt - 1 - i

    tile = pl.BlockSpec((ts, FFN_CB), lambda j, i: (rev(i), j))
    w_out = pl.BlockSpec((3, FFN_CB), lambda j, i: (0, j))
    b_out = pl.BlockSpec((1, FFN_CB), lambda j, i: (0, j))
    f = jax.ShapeDtypeStruct
    return pl.pallas_call(
        body, name=name, grid=(nblk, nt),
        in_specs=[tile] + _ffn_specs(ts, rev, lambda i: jnp.maximum(rev(i) * per - 1, 0)),
        out_specs=(tile, tile, w_out, w_out, b_out, b_out),
        out_shape=(f((s, D_FF), BF16), f((s, D_FF), BF16), f((3, D_FF), F32), f((3, D_FF), F32),
                   f((1, D_FF), F32), f((1, D_FF), F32)),
        scratch_shapes=[pltpu.VMEM((SUBLANES, FFN_CB), F32), pltpu.VMEM((SUBLANES, FFN_CB), F32),
                        pltpu.VMEM((ts + halo, FFN_CB), BF16), pltpu.VMEM((ts + halo, FFN_CB), BF16)],
    )(dact, up, up, w, b, up, up, w, b)


def _sgu_forward_block(zu, zg, gn, w_ref, bias, seg):
    u, tu = _gelu(zu)
    g, tg = _gelu(zg)
    ms = _dot_split(g * g, seg)
    rs = lax.rsqrt(ms + EPS)
    ghat = g * rs
    gv = ghat * gn
    gvb = gv.astype(BF16)
    lane = _lanes((CHUNK, LANES))
    chunks = []
    for c in range(zu.shape[0] // CHUNK):
        gc = gvb[c * CHUNK:(c + 1) * CHUNK]
        mix = jnp.where(lane < 64, _dot(w_ref[0], gc), _dot(w_ref[1], gc)) + bias
        chunks.append(mix)
    mixed = chunks[0] if len(chunks) == 1 else jnp.concatenate(chunks, axis=0)
    return u, tu, g, tg, rs, ghat, gvb, mixed


def _sgu_fwd(p1, gn, w, bias, seg, *, ts=512, name):
    s = p1.shape[0]
    ts = _tile_rows(ts, s)

    def body(zu_ref, zg_ref, gn_ref, w_ref, bias_ref, seg_ref, yc_ref):
        u, _, _, _, _, _, _, mixed = _sgu_forward_block(
            zu_ref[...], zg_ref[...], gn_ref[...], w_ref, bias_ref[...], seg_ref[...])
        yc_ref[...] = (u * mixed).astype(BF16)

    return pl.pallas_call(
        body, name=name, grid=(4, s // ts),
        in_specs=[pl.BlockSpec((ts, LANES), lambda j, i: (i, j)),
                  pl.BlockSpec((ts, LANES), lambda j, i: (i, 4 + j)),
                  pl.BlockSpec((1, LANES), lambda j, i: (0, j)),
                  pl.BlockSpec((2, CHUNK, CHUNK), lambda j, i: (j, 0, 0)),
                  pl.BlockSpec((CHUNK, LANES), lambda j, i: (0, j)),
                  pl.BlockSpec((LANES, LANES), lambda j, i: (0, 0))],
        out_specs=pl.BlockSpec((ts, LANES), lambda j, i: (i, j)),
        out_shape=jax.ShapeDtypeStruct((s, 4 * LANES), BF16),
    )(p1, p1, gn, w, bias, seg)


def _sgu_bwd(p1, dy, gn, w, wt, bias, seg, tril, *, ts=512, name):
    s = p1.shape[0]
    ts = _tile_rows(ts, s)
    nt = s // ts

    def body(zu_ref, zg_ref, dy_ref, gn_ref, w_ref, wt_ref, bias_ref, seg_ref, tril_ref,
             dzu_ref, dzg_ref, dw_ref, dbias_ref, dgn_ref):
        i = pl.program_id(1)
        zu = zu_ref[...]
        zg = zg_ref[...]
        gn_v = gn_ref[...]
        segv = seg_ref[...]
        u, tu, g, tg, rs, ghat, gvb, mixed = _sgu_forward_block(zu, zg, gn_v, w_ref, bias_ref[...], segv)
        dyv = dy_ref[...]
        du = dyv * mixed
        dmx = dyv * u

        @pl.when(i == 0)
        def _():
            dw_ref[...] = jnp.zeros_like(dw_ref)
            dbias_ref[...] = jnp.zeros_like(dbias_ref)
            dgn_ref[...] = jnp.zeros_like(dgn_ref)

        lane = _lanes((CHUNK, LANES))
        dgv_chunks = []
        dbias = jnp.zeros((CHUNK, LANES), F32)
        for c in range(ts // CHUNK):
            dmc = dmx[c * CHUNK:(c + 1) * CHUNK]
            gc = gvb[c * CHUNK:(c + 1) * CHUNK]
            dm_a = jnp.where(lane < 64, dmc, 0.0).astype(BF16)
            dm_b = jnp.where(lane >= 64, dmc, 0.0).astype(BF16)
            dw_ref[0] += _dot_nt(dm_a, gc)
            dw_ref[1] += _dot_nt(dm_b, gc)
            dgv_chunks.append(_dot(wt_ref[0], dm_a) + _dot(wt_ref[1], dm_b))
            dbias = dbias + dmc
        dbias_ref[...] += dbias
        dgv = dgv_chunks[0] if len(dgv_chunks) == 1 else jnp.concatenate(dgv_chunks, axis=0)
        dgn_ref[...] += jnp.sum(dgv * ghat, axis=0, keepdims=True)
        dgh = dgv * gn_v
        dg = rs * (dgh - ghat * _dot_split(dgh * ghat, segv))
        dzu_ref[...] = (du * _gelu_grad(zu, tu)).astype(BF16)
        dzg_ref[...] = (dg * _gelu_grad(zg, tg)).astype(BF16)

        @pl.when(i == nt - 1)
        def _():
            dw_ref[0] = dw_ref[0] * tril_ref[...]
            dw_ref[1] = dw_ref[1] * tril_ref[...]

    f = jax.ShapeDtypeStruct
    colj = pl.BlockSpec((ts, LANES), lambda j, i: (i, j))
    wsp = pl.BlockSpec((2, CHUNK, CHUNK), lambda j, i: (j, 0, 0))
    sq = pl.BlockSpec((LANES, LANES), lambda j, i: (0, 0))
    return pl.pallas_call(
        body, name=name, grid=(4, nt),
        in_specs=[colj, pl.BlockSpec((ts, LANES), lambda j, i: (i, 4 + j)), colj,
                  pl.BlockSpec((1, LANES), lambda j, i: (0, j)), wsp, wsp,
                  pl.BlockSpec((CHUNK, LANES), lambda j, i: (0, j)), sq, sq],
        out_specs=(colj, colj, wsp, pl.BlockSpec((CHUNK, LANES), lambda j, i: (0, j)),
                   pl.BlockSpec((1, LANES), lambda j, i: (0, j))),
        out_shape=(f((s, 4 * LANES), BF16), f((s, 4 * LANES), BF16), f((8, CHUNK, CHUNK), F32),
                   f((CHUNK, 4 * LANES), F32), f((1, 4 * LANES), F32)),
    )(p1, p1, dy, gn, w, wt, bias, seg, tril)


F_COL = 20


def _fcum_fwd(p1, bf, *, ts=512, name):
    s = p1.shape[0]
    ts = _tile_rows(ts, s)

    def body(f_ref, bf_ref, c_ref, car_ref):
        i = pl.program_id(0)
        z = f_ref[...] + bf_ref[...]
        logf = jnp.minimum(z, 0.0) - _log1p_pos(jnp.exp(-jnp.abs(z)))

        @pl.when(i == 0)
        def _():
            car_ref[...] = jnp.zeros_like(car_ref)

        c_ref[...] = _cumsum_fwd(logf) + car_ref[0:1, :]
        car_ref[0:1, :] = c_ref[ts - 1:ts, :]

    return pl.pallas_call(
        body, name=name, grid=(s // ts,),
        in_specs=[pl.BlockSpec((ts, LANES), lambda i: (i, F_COL)), pl.BlockSpec((1, LANES), lambda i: (0, 0))],
        out_specs=pl.BlockSpec((ts, LANES), lambda i: (i, 0)),
        out_shape=jax.ShapeDtypeStruct((s, LANES), F32),
        scratch_shapes=[pltpu.VMEM((SUBLANES, LANES), F32)],
    )(p1, bf)


def _fcum_bwd(dcs, dcq, p1, bf, *, ts=512, name):
    s = p1.shape[0]
    ts = _tile_rows(ts, s)
    nt = s // ts

    def body(dc_ref, dcq_ref, f_ref, bf_ref, df_ref, dbf_ref, car_ref):
        i = pl.program_id(0)

        @pl.when(i == 0)
        def _():
            car_ref[...] = jnp.zeros_like(car_ref)
            dbf_ref[...] = jnp.zeros_like(dbf_ref)

        dc = dc_ref[...]
        lane = _lanes((ts, LANES))
        for h in range(8):
            dc = dc + jnp.where(lane == h, dcq_ref[:, h * LANES:(h + 1) * LANES], 0.0)
        dlog = _cumsum_rev(dc) + car_ref[0:1, :]
        car_ref[...] = dlog[:SUBLANES]
        z = f_ref[...] + bf_ref[...]
        df = dlog * _sigmoid(-z)
        df_ref[...] = df.astype(BF16)
        dbf_ref[...] += jnp.sum(df, axis=0, keepdims=True)

    return pl.pallas_call(
        body, name=name, grid=(nt,),
        in_specs=[pl.BlockSpec((ts, LANES), lambda i: (nt - 1 - i, 0)),
                  pl.BlockSpec((ts, 8 * LANES), lambda i: (nt - 1 - i, 0)),
                  pl.BlockSpec((ts, LANES), lambda i: (nt - 1 - i, F_COL)),
                  pl.BlockSpec((1, LANES), lambda i: (0, 0))],
        out_specs=(pl.BlockSpec((ts, LANES), lambda i: (nt - 1 - i, 0)), pl.BlockSpec((1, LANES), lambda i: (0, 0))),
        out_shape=(jax.ShapeDtypeStruct((s, LANES), BF16), jax.ShapeDtypeStruct((1, LANES), F32)),
        scratch_shapes=[pltpu.VMEM((SUBLANES, LANES), F32)],
    )(dcs, dcq, p1, bf)


def _fox_scores(qm, kb, bias, ck, diagonal):
    sc = _dot_nt(qm, kb) + bias - ck
    if diagonal:
        sc = jnp.where(_lanes(sc.shape) <= _rows(sc.shape), sc, NEG)
    return sc


def _head_masks(shape):
    lane = _lanes(shape)
    return lane < 64, lane >= 64


def _fox_fwd(p1, cq, ck, *, tq=512, name):
    s = p1.shape[0]
    tq = _tile_rows(tq, s)
    tk = tq
    nq = s // tq

    def body(q_ref, k_ref, v_ref, cq_ref, ck_ref, o_ref, lb_ref):
        qi = pl.program_id(1)
        q = q_ref[...] * 0.125
        first, second = _head_masks((tq, LANES))
        qms = [jnp.where(sel, q, 0.0).astype(BF16) for sel in (first, second)]
        cqs = [cq_ref[:, hh * LANES:(hh + 1) * LANES] for hh in range(2)]
        biases = [jnp.tile(cqh, (1, tk // LANES)) for cqh in cqs]

        def step(kj, carry, diagonal):
            cols = pl.ds(pl.multiple_of(kj * tk, tk), tk)
            kb = k_ref[cols, :].astype(BF16)
            vb = v_ref[cols, :].astype(BF16)
            new, outs = [], []
            acc = carry[4]
            for hh in range(2):
                m_prev, l_prev = carry[2 * hh], carry[2 * hh + 1]
                sc = _fox_scores(qms[hh], kb, biases[hh], ck_ref[hh, :, cols], diagonal)
                m_new = jnp.maximum(m_prev, jnp.max(sc, axis=1, keepdims=True))
                pm = jnp.exp(sc - jnp.tile(m_new, (1, tk // LANES)))
                alpha = jnp.exp(m_prev - m_new)
                new += [m_new, alpha * l_prev + jnp.sum(pm, axis=1, keepdims=True)]
                outs.append(acc * alpha + _dot(pm.astype(BF16), vb))
            return tuple(new) + (jnp.where(first, outs[0], outs[1]),)

        zero = jnp.zeros((tq, LANES), F32)
        low = jnp.full((tq, LANES), NEG, F32)
        carry = lax.fori_loop(0, qi, lambda kj, c: step(kj, c, False), (low, zero, low, zero, zero))
        m0, l0, m1, l1, acc = step(qi, carry, True)
        o_ref[...] = (acc / jnp.where(first, l0, l1)).astype(BF16)
        lb_ref[:, 0:LANES] = cqs[0] - (m0 + jnp.log(l0))
        lb_ref[:, LANES:2 * LANES] = cqs[1] - (m1 + jnp.log(l1))

    return pl.pallas_call(
        body, name=name, grid=(4, nq),
        in_specs=[pl.BlockSpec((tq, LANES), lambda j, qi: (qi, 8 + j)),
                  pl.BlockSpec((s, LANES), lambda j, qi: (0, 12 + j)),
                  pl.BlockSpec((s, LANES), lambda j, qi: (0, 16 + j)),
                  pl.BlockSpec((tq, 2 * LANES), lambda j, qi: (qi, j)),
                  pl.BlockSpec((2, 1, s), lambda j, qi: (j, 0, 0))],
        out_specs=(pl.BlockSpec((tq, LANES), lambda j, qi: (qi, j)),
                   pl.BlockSpec((tq, 2 * LANES), lambda j, qi: (qi, j))),
        out_shape=(jax.ShapeDtypeStruct((s, 4 * LANES), BF16), jax.ShapeDtypeStruct((s, 8 * LANES), F32)),
    )(p1, p1, p1, cq, ck)


def _fox_delta(dy, o, sel, *, ts=512, name):
    s = o.shape[0]
    ts = _tile_rows(ts, s)

    def body(do_ref, o_ref, sel_ref, d_ref):
        prod = do_ref[...] * o_ref[...].astype(F32)
        d_ref[:, 0:LANES] = _dot_split(prod, sel_ref[0])
        d_ref[:, LANES:2 * LANES] = _dot_split(prod, sel_ref[1])

    return pl.pallas_call(
        body, name=name, grid=(4, s // ts),
        in_specs=[pl.BlockSpec((ts, LANES), lambda j, i: (i, 4 + j)),
                  pl.BlockSpec((ts, LANES), lambda j, i: (i, j)),
                  pl.BlockSpec((2, LANES, LANES), lambda j, i: (0, 0, 0))],
        out_specs=pl.BlockSpec((ts, 2 * LANES), lambda j, i: (i, j)),
        out_shape=jax.ShapeDtypeStruct((s, 8 * LANES), F32),
    )(dy, o, sel)


def _fox_bwd(p1, dy, lb, delta, ck, *, tq=512, name):
    s = p1.shape[0]
    tq = _tile_rows(tq, s)
    tk = tq
    nq = s // tq

    def body(q_ref, k_ref, v_ref, do_ref, lb_ref, dl_ref, ck_ref,
             dq_ref, dk_ref, dv_ref, dck_ref, dcq_ref, dqa_ref, dra_ref):
        kj = pl.program_id(1)

        @pl.when(kj == 0)
        def _():
            dqa_ref[...] = jnp.zeros_like(dqa_ref)
            dra_ref[...] = jnp.zeros_like(dra_ref)

        kf = k_ref[...]
        kb = kf.astype(BF16)
        vb = v_ref[...].astype(BF16)
        first, second = _head_masks((tk, LANES))
        kms = [jnp.where(sel, kf, 0.0).astype(BF16) for sel in (first, second)]
        cks = [ck_ref[hh] for hh in range(2)]

        def step(qi, carry, diagonal):
            dk_acc, dv_acc, dc0, dc1 = carry
            dcs = [dc0, dc1]
            rows = pl.ds(pl.multiple_of(qi * tq, tq), tq)
            q = q_ref[rows, :] * 0.125
            do = do_ref[rows, :]
            for hh, sel in enumerate((first, second)):
                qm = jnp.where(sel, q, 0.0).astype(BF16)
                dom = jnp.where(sel, do, 0.0).astype(BF16)
                bias = jnp.tile(lb_ref[rows, hh * LANES:(hh + 1) * LANES], (1, tk // LANES))
                pm = jnp.exp(_fox_scores(qm, kb, bias, cks[hh], diagonal))
                dv_acc = dv_acc + _dot_tn(pm.astype(BF16), dom)
                dp = _dot_nt(dom, vb)
                ds = pm * (dp - jnp.tile(dl_ref[rows, hh * LANES:(hh + 1) * LANES], (1, tk // LANES)))
                dsb = ds.astype(BF16)
                dk_acc = dk_acc + _dot_tn(dsb, qm)
                dcs[hh] = dcs[hh] - jnp.sum(ds, axis=0, keepdims=True)
                dqa_ref[rows, :] += _dot(dsb, kms[hh])
                dra_ref[hh, rows, :] += jnp.sum(ds, axis=1, keepdims=True)
            return dk_acc, dv_acc, dcs[0], dcs[1]

        zero = jnp.zeros((tk, LANES), F32)
        zrow = jnp.zeros((1, tk), F32)
        carry = step(kj, (zero, zero, zrow, zrow), True)
        dk_acc, dv_acc, dc0, dc1 = lax.fori_loop(kj + 1, nq, lambda qi, c: step(qi, c, False), carry)
        dk_ref[...] = dk_acc.astype(BF16)
        dv_ref[...] = dv_acc.astype(BF16)
        dck_ref[0] = dc0
        dck_ref[1] = dc1

        @pl.when(kj == nq - 1)
        def _():
            dq_ref[...] = (dqa_ref[...] * 0.125).astype(BF16)
            dcq_ref[:, 0:LANES] = dra_ref[0]
            dcq_ref[:, LANES:2 * LANES] = dra_ref[1]

    def full(width, col0):
        return pl.BlockSpec((s, width), lambda j, kj: (0, col0 + j))

    kblk = pl.BlockSpec((tk, LANES), lambda j, kj: (kj, j))
    f = jax.ShapeDtypeStruct
    return pl.pallas_call(
        body, name=name, grid=(4, nq),
        in_specs=[full(LANES, 8),
                  pl.BlockSpec((tk, LANES), lambda j, kj: (kj, 12 + j)),
                  pl.BlockSpec((tk, LANES), lambda j, kj: (kj, 16 + j)),
                  full(LANES, 4), full(2 * LANES, 0), full(2 * LANES, 0),
                  pl.BlockSpec((2, 1, tk), lambda j, kj: (j, 0, kj))],
        out_specs=(full(LANES, 0), kblk, kblk, pl.BlockSpec((2, 1, tk), lambda j, kj: (j, 0, kj)),
                   full(2 * LANES, 0)),
        out_shape=(f((s, 4 * LANES), BF16), f((s, 4 * LANES), BF16), f((s, 4 * LANES), BF16),
                   f((8, 1, s), F32), f((s, 8 * LANES), F32)),
        scratch_shapes=[pltpu.VMEM((s, LANES), F32), pltpu.VMEM((2, s, LANES), F32)],
    )(p1, p1, p1, dy, lb, delta, ck)


def _row_block(r, cap=256):
    best = None
    for rb in range(2 * SUBLANES, min(r, cap) + 1, 2 * SUBLANES):
        if r % rb == 0:
            best = rb
    return r if best is None else best


def _adamw(w, g, m, v, *, name):
    r, c = w.shape
    rb = _row_block(r)

    def body(w_ref, g_ref, m_ref, v_ref, d_ref, nm_ref, nv_ref):
        gv = g_ref[...]
        mn = ADAM_B1 * m_ref[...] + (1.0 - ADAM_B1) * gv
        vn = ADAM_B2 * v_ref[...] + (1.0 - ADAM_B2) * (gv * gv)
        m_hat = mn / ADAM_C1
        v_hat = vn / ADAM_C2
        d_ref[...] = (-ADAM_LR) * (m_hat / (jnp.sqrt(v_hat) + ADAM_EPS) + ADAM_WD * w_ref[...])
        nm_ref[...] = mn
        nv_ref[...] = vn

    blk = pl.BlockSpec((rb, c), lambda i: (i, 0))
    shp = jax.ShapeDtypeStruct((r, c), F32)
    return pl.pallas_call(
        body, name=name, grid=(r // rb,), in_specs=[blk] * 4, out_specs=(blk,) * 3, out_shape=(shp,) * 3,
    )(w, g, m, v)


def _adamw_halves(w, mine, theirs, m, v, core, *, name):
    layers, r, c = w.shape
    rh = r // 2
    rb = _row_block(rh)
    per = rh // rb

    def body(core_ref, w_ref, *refs):
        g_refs = refs[:2 * layers]
        m_ref, v_ref, g_ref, d_ref, nm_ref, nv_ref = refs[2 * layers:]
        own = pl.program_id(1) == core_ref[0]
        gv = jnp.where(own, g_refs[0][...], g_refs[layers][...])
        for l in range(1, layers):
            gv = jnp.where(pl.program_id(0) == l, jnp.where(own, g_refs[l][...], g_refs[layers + l][...]), gv)
        g_ref[...] = gv
        mn = ADAM_B1 * m_ref[...] + (1.0 - ADAM_B1) * gv
        vn = ADAM_B2 * v_ref[...] + (1.0 - ADAM_B2) * (gv * gv)
        m_hat = mn / ADAM_C1
        v_hat = vn / ADAM_C2
        d_ref[...] = (-ADAM_LR) * (m_hat / (jnp.sqrt(v_hat) + ADAM_EPS) + ADAM_WD * w_ref[...])
        nm_ref[...] = mn
        nv_ref[...] = vn

    full = pl.BlockSpec((None, rb, c), lambda l, h, i, core_ref: (l, h * per + i, 0))
    half = pl.BlockSpec((rb, c), lambda l, h, i, core_ref: (i, 0))
    shp = jax.ShapeDtypeStruct((layers, r, c), F32)
    return pl.pallas_call(
        body, name=name,
        grid_spec=pltpu.PrefetchScalarGridSpec(
            num_scalar_prefetch=1, grid=(layers, 2, per),
            in_specs=[full] + [half] * (2 * layers) + [full, full], out_specs=(full,) * 4),
        out_shape=(shp,) * 4,
    )(core, w, *mine, *theirs, m, v)


def _pair_specs(col, rb, c):
    if col:
        g_spec = pl.BlockSpec((None, rb, c), lambda t, i, sel: (sel[0], i, sel[1 + t]))
    else:
        g_spec = pl.BlockSpec((None, None, rb, c), lambda t, i, sel: (sel[1 + t], sel[0], i, 0))
    return g_spec, pl.BlockSpec((None, rb, c), lambda t, i, sel: (sel[1 + t], i, 0))


def _pair_sum(g, col, ra, sel, after, *, name):
    _, rh, c = ra.shape
    rb = _row_block(rh)

    def body(sel_ref, g_ref, ra_ref, after_ref, h16_ref):
        h16_ref[...] = (g_ref[...] + ra_ref[...]).astype(BF16)

    g_spec, ra_spec = _pair_specs(col, rb, c)
    return pl.pallas_call(
        body, name=name,
        grid_spec=pltpu.PrefetchScalarGridSpec(
            num_scalar_prefetch=1, grid=(2, rh // rb), in_specs=[g_spec, ra_spec, ANY],
            out_specs=pl.BlockSpec((None, rb, c), lambda t, i, sel: (t, i, 0))),
        out_shape=jax.ShapeDtypeStruct((2, rh, c), BF16),
    )(sel, g, ra, after)


def _first_sum(g, col, ra, r1, sel, after, *, name):
    _, rh, c = ra.shape
    rb = _row_block(rh)

    def body(sel_ref, g_ref, ra_ref, r_ref, after_ref, s_ref, s16_ref):
        tot = (g_ref[...] + ra_ref[...]) + r_ref[...].astype(F32)
        s_ref[...] = tot
        s16_ref[...] = tot.astype(BF16)

    g_spec, ra_spec = _pair_specs(col, rb, c)
    slot = pl.BlockSpec((None, rb, c), lambda t, i, sel_ref: (t, i, 0))
    return pl.pallas_call(
        body, name=name,
        grid_spec=pltpu.PrefetchScalarGridSpec(
            num_scalar_prefetch=1, grid=(2, rh // rb), in_specs=[g_spec, ra_spec, slot, ANY],
            out_specs=(slot, slot)),
        out_shape=(jax.ShapeDtypeStruct((2, rh, c), F32), jax.ShapeDtypeStruct((2, rh, c), BF16)),
    )(sel, g, ra, r1, after)


def _second_sum(s1, r2, mine, after, *, name):
    _, rh, c = s1.shape
    rb = _row_block(rh)

    def body(mine_ref, s_ref, r_ref, after_ref, t_ref):
        t_ref[...] = s_ref[...] + r_ref[...].astype(F32)

    flat = pl.BlockSpec((rb, c), lambda i, mine_ref: (i, 0))
    return pl.pallas_call(
        body, name=name,
        grid_spec=pltpu.PrefetchScalarGridSpec(
            num_scalar_prefetch=1, grid=(rh // rb,),
            in_specs=[pl.BlockSpec((None, rb, c), lambda i, mine_ref: (mine_ref[0], i, 0)), flat, ANY],
            out_specs=flat),
        out_shape=jax.ShapeDtypeStruct((rh, c), F32),
    )(mine, s1, r2, after)


def _place(shard, col, chip, dtype, *, name):
    r, c = shard.shape
    rh = r // 2
    rb = _row_block(rh)

    def body(chip_ref, s_ref, o_ref):
        o_ref[...] = s_ref[...].astype(o_ref.dtype)

    if col:
        out_spec = pl.BlockSpec((None, rb, c), lambda h, i, chip_ref: (h, i, chip_ref[0]))
        shape = (2, rh, N_CHIPS * c)
    else:
        out_spec = pl.BlockSpec((None, None, rb, c), lambda h, i, chip_ref: (chip_ref[0], h, i, 0))
        shape = (N_CHIPS, 2, rh, c)
    per = rh // rb
    return pl.pallas_call(
        body, name=name,
        grid_spec=pltpu.PrefetchScalarGridSpec(
            num_scalar_prefetch=1, grid=(2, per),
            in_specs=[pl.BlockSpec((rb, c), lambda h, i, chip_ref: (h * per + i, 0))], out_specs=out_spec),
        out_shape=jax.ShapeDtypeStruct(shape, dtype),
    )(chip, shard)


ANY = pl.BlockSpec(memory_space=pl.ANY)


def _mesh_pos():
    return lax.axis_index("x"), lax.axis_index("y"), lax.axis_index("c")


def _other_chips(x, y):
    return [(1 - x, y), (x, 1 - y), (1 - x, 1 - y)]


def _remote(src, dst, ssem, rsem, dev):
    return pltpu.make_async_remote_copy(src_ref=src, dst_ref=dst, send_sem=ssem, recv_sem=rsem,
                                        device_id=dev, device_id_type=MESH)


def _flip(a, b):
    return a + b - 2 * a * b


def _handshake(peers):
    barrier = pltpu.get_barrier_semaphore()
    for peer in peers:
        pl.semaphore_signal(barrier, inc=1, device_id=peer, device_id_type=MESH)
    pl.semaphore_wait(barrier, len(peers))


def _slab(ref, col, width, k, h):
    if not col:
        return ref.at[k, h]
    start = k * width if isinstance(k, int) else pl.multiple_of(k * width, LANES)
    return ref.at[h, :, pl.ds(start, width)]


def _all_gather(bufs, cols, *, collective_id, name):
    n = len(bufs)
    widths = [b.shape[2] // N_CHIPS if col else b.shape[3] for b, col in zip(bufs, cols)]
    outs = [jax.new_ref(b, memory_space=pltpu.MemorySpace.HBM) for b in bufs]

    def body(ssem, rsem):
        x, y, c = _mesh_pos()
        me = 2 * x + y
        sib = (x, y, 1 - c)
        n1 = (_flip(x, 1 - c), _flip(y, c))
        n2 = (_flip(x, c), _flip(y, 1 - c))
        k1 = 2 * n1[0] + n1[1]
        k2 = 2 * n2[0] + n2[1]
        kd = 2 * (1 - x) + (1 - y)
        _handshake([n1 + (c,), n2 + (c,), sib])

        def slab(a, k, h):
            return _slab(outs[a], cols[a], widths[a], k, h)

        def copy(a, j, src, dst, dev):
            return _remote(src, dst, ssem.at[a, j], rsem.at[a, j], dev)

        sends = []
        for a in range(n):
            for j, nb in ((0, n1), (1, n2)):
                own = slab(a, me, c)
                cp = copy(a, j, own, own, nb + (c,))
                cp.start()
                sends.append(cp)
        arrivals = ((0, k1, n1, 3), (1, k2, n2, 4), (2, kd, n2, 5))
        for j, k, nb, fwd in arrivals:
            for a in range(n):
                got = slab(a, k, c)
                copy(a, j, got, got, nb + (c,)).wait_recv()
                if j == 0:
                    cp = copy(a, 2, got, got, n2 + (c,))
                    cp.start()
                    sends.append(cp)
                cp = copy(a, fwd, got, got, sib)
                cp.start()
                sends.append(cp)
        for fwd, k in ((3, k2), (4, k1), (5, kd)):
            for a in range(n):
                got = slab(a, k, 1 - c)
                copy(a, fwd, got, got, sib).wait_recv()
        for cp in sends:
            cp.wait_send()

    _sequencer_call(body, (), [(n, 6), (n, 6)], collective_id, name)()
    return [ref[...] for ref in outs]


def _sequencer_call(body, out_types, sem_shapes, collective_id, name):
    return pl.kernel(
        body, name=name, out_type=out_types,
        mesh=plsc.ScalarSubcoreMesh(axis_name="sequencer", num_cores=1),
        scratch_types=[pltpu.SemaphoreType.DMA(shape) for shape in sem_shapes],
        compiler_params=pltpu.CompilerParams(collective_id=collective_id))


def _send_other_half(grads, cols, *, collective_id, name):
    n = len(grads)

    def shard_shape(g, col):
        if col:
            return (g.shape[1], g.shape[2] // N_CHIPS)
        return g.shape[2:]

    shapes = [shard_shape(g, col) for g, col in zip(grads, cols)]

    def body(*refs):
        ins, outs = refs[:n], refs[n:2 * n]
        ssem, rsem = refs[2 * n:]
        x, y, c = _mesh_pos()
        sib = (x, y, 1 - c)
        _handshake([sib])
        sends = []
        for a in range(n):
            for k in range(N_CHIPS):
                src = _slab(ins[a], cols[a], shapes[a][1], k, 1 - c)
                cp = _remote(src, outs[a].at[k], ssem.at[a, k], rsem.at[a, k], sib)
                cp.start()
                sends.append(cp)
        for cp in sends:
            cp.wait()

    out_types = [jax.ShapeDtypeStruct((N_CHIPS,) + shp, g.dtype) for g, shp in zip(grads, shapes)]
    return _sequencer_call(body, out_types, [(n, N_CHIPS), (n, N_CHIPS)], collective_id, name)(*grads)


def _send_first(sums, *, collective_id, name):
    n = len(sums)

    def body(*refs):
        ins, outs = refs[:n], refs[n:2 * n]
        ssem, rsem = refs[2 * n:]
        x, y, c = _mesh_pos()
        nb = (_flip(x, c), _flip(y, 1 - c), c)
        _handshake([nb])
        sends = []
        for a in range(n):
            for t in range(2):
                cp = _remote(ins[a].at[t], outs[a].at[t], ssem.at[a, t], rsem.at[a, t], nb)
                cp.start()
                sends.append(cp)
        for cp in sends:
            cp.wait()

    out_types = [jax.ShapeDtypeStruct(h.shape, h.dtype) for h in sums]
    return _sequencer_call(body, out_types, [(n, 2), (n, 2)], collective_id, name)(*sums)


def _send_second(sums, *, collective_id, name):
    n = len(sums)

    def body(*refs):
        ins, outs = refs[:n], refs[n:2 * n]
        ssem, rsem = refs[2 * n:]
        x, y, c = _mesh_pos()
        nb = (_flip(x, 1 - c), _flip(y, c), c)
        other = 1 - (c * y + (1 - c) * x)
        _handshake([nb])
        sends = []
        for a in range(n):
            cp = _remote(ins[a].at[other], outs[a], ssem.at[a], rsem.at[a], nb)
            cp.start()
            sends.append(cp)
        for cp in sends:
            cp.wait()

    out_types = [jax.ShapeDtypeStruct(s.shape[1:], s.dtype) for s in sums]
    return _sequencer_call(body, out_types, [(n,), (n,)], collective_id, name)(*sums)


def _swap_halves(halves, *, collective_id, name):
    n = len(halves)

    def body(*refs):
        ins, outs = refs[:n], refs[n:2 * n]
        ssem, rsem = refs[2 * n:]
        x, y, c = _mesh_pos()
        sib = (x, y, 1 - c)
        _handshake([sib])
        cps = []
        for a in range(n):
            cp = _remote(ins[a], outs[a], ssem.at[a], rsem.at[a], sib)
            cp.start()
            cps.append(cp)
        for cp in cps:
            cp.wait()

    out_types = [jax.ShapeDtypeStruct(h.shape, h.dtype) for h in halves]
    return _sequencer_call(body, out_types, [(n,), (n,)], collective_id, name)(*halves)


def _all_reduce_small(buf, *, name):
    r = buf.shape[0]
    rh = r // 2

    def body(in_ref, out_ref, x1_ref, x2_ref, ssem, rsem):
        x, y, c = _mesh_pos()
        me = 2 * x + y
        sib = (x, y, 1 - c)
        chips = _other_chips(x, y)
        cp = _remote(in_ref, x1_ref, ssem.at[0], rsem.at[0], sib)
        cp.start()
        cp.wait()
        off = pl.multiple_of(c * rh, SUBLANES)
        x2_ref[me] = in_ref[pl.ds(off, rh), :] + x1_ref[pl.ds(off, rh), :]
        sends = []
        for j, (cx, cy) in enumerate(chips):
            s = _remote(x2_ref.at[me], x2_ref.at[me], ssem.at[1 + j], rsem.at[1 + j], (cx, cy, c))
            s.start()
            sends.append(s)
        for j, (cx, cy) in enumerate(chips):
            slot = x2_ref.at[2 * cx + cy]
            _remote(slot, slot, ssem.at[1 + j], rsem.at[1 + j], (cx, cy, c)).wait_recv()
        out_ref[pl.ds(off, rh), :] = ((x2_ref[0] + x2_ref[1]) + x2_ref[2]) + x2_ref[3]
        for s in sends:
            s.wait_send()
        mine = out_ref.at[pl.ds(off, rh), :]
        s3 = _remote(mine, mine, ssem.at[4], rsem.at[4], sib)
        s3.start()
        off2 = pl.multiple_of((1 - c) * rh, SUBLANES)
        theirs = out_ref.at[pl.ds(off2, rh), :]
        _remote(theirs, theirs, ssem.at[4], rsem.at[4], sib).wait_recv()
        s3.wait_send()

    vm = pl.BlockSpec(memory_space=pltpu.VMEM)
    return pl.pallas_call(
        body, name=name, in_specs=[vm], out_specs=vm,
        out_shape=jax.ShapeDtypeStruct((r, LANES), F32),
        scratch_shapes=[pltpu.VMEM((r, LANES), F32), pltpu.VMEM((N_CHIPS, rh, LANES), F32),
                        pltpu.SemaphoreType.DMA((5,)), pltpu.SemaphoreType.DMA((5,))],
    )(buf)


PACK_ALIGN = 2 * SUBLANES * LANES


def _pack(arrays):
    parts, offs, off = [], [], 0
    for a in arrays:
        flat = a.reshape(-1).astype(F32)
        padded = -(-flat.shape[0] // PACK_ALIGN) * PACK_ALIGN
        parts.append(jnp.pad(flat, (0, padded - flat.shape[0])))
        offs.append(off)
        off += padded
    buf = jnp.concatenate(parts).reshape(-1, LANES)
    return buf, offs


def _unpack(buf, offs, shapes):
    flat = buf.reshape(-1)
    out = []
    for off, shp in zip(offs, shapes):
        size = 1
        for d in shp:
            size *= d
        out.append(flat[off:off + size].reshape(shp))
    return out


def _cols_from_shards(g4):
    _, k, ns = g4.shape
    return jnp.transpose(g4, (1, 0, 2)).reshape(k, N_CHIPS * ns)


def _cols_to_shards(w):
    k, n = w.shape
    return jnp.transpose(w.reshape(k, N_CHIPS, n // N_CHIPS), (1, 0, 2))


def _pair_blockdiag(w8):
    w = w8.reshape(4, 2, 64, 64)
    z = jnp.zeros((4, 64, 64), w8.dtype)
    top = jnp.concatenate([w[:, 0], z], axis=2)
    bot = jnp.concatenate([z, w[:, 1]], axis=2)
    return jnp.concatenate([top, bot], axis=1)


def _pair_diag_blocks(w4):
    a = w4[:, :64, :64]
    b = w4[:, 64:, 64:]
    return jnp.stack([a, b], axis=1).reshape(8, 64, 64)


def _local_step(x, target, wts, on_event=None):
    s = x.shape[0]
    g = {}

    def event(name, token):
        if on_event is not None:
            on_event(name, g, token)

    win0 = wts["w_in0"]
    wout0 = wts["w_out0"]
    win1 = wts["w_in1"]
    wout1 = wts["w_out1"]
    wup = wts["w_up"]
    wdown = wts["w_down"]
    w4, b4, w3, b3 = wts["w4"], wts["b4"], wts["w3"], wts["b3"]
    wa, wx = wts["wa"], wts["wx"]
    wat, wxt = jnp.swapaxes(wa, 1, 2), jnp.swapaxes(wx, 1, 2)
    ba, bx, lam = wts["ba"], wts["bx"], wts["lam"]
    fcw, fcb = wts["ffn_cw"], wts["ffn_cb"]
    sgu_w, sgu_wt = wts["sgu_w"], wts["sgu_wt"]
    sgu_bias, sgu_gn = wts["sgu_bias"], wts["sgu_gn"]
    bf = wts["bf"]

    lane = jnp.arange(LANES)
    seg = jnp.where((lane[:, None] // 64) == (lane[None, :] // 64), 1.0 / 64.0, 0.0).astype(BF16)
    sel = jnp.stack([jnp.broadcast_to((lane[:, None] < 64), (LANES, LANES)),
                     jnp.broadcast_to((lane[:, None] >= 64), (LANES, LANES))]).astype(BF16)
    tril = (lane[:, None] >= lane[None, :]).astype(F32)

    n0 = _norm_fwd(x, wts["g_mix0"], name="norm_mix0")
    p0 = _mm([n0], win0, nb=1280, name="mm_in0")
    ya, yb, hl = _even_core_fwd(p0, w4, b4, wa, ba, wx, bx, lam, w3, b3, name="even_fwd")
    h1, n1 = _mm([ya, yb], wout0, res=x, norm_out=wts["g_ffn"][0], name="mm_out0")

    def ffn_fwd(h, n, layer, next_gain):
        up = _mm([n], wup[layer], out_dtype=BF16, ts=1024, nb=1408, name=f"mm_up{layer}")
        act = _ffn_core_fwd(up, fcw[layer], fcb[layer], name=f"ffn_fwd{layer}")
        if next_gain is None:
            return up, act, _mm([act], wdown[layer], res=h, name=f"mm_down{layer}"), None
        hn, nn = _mm([act], wdown[layer], res=h, norm_out=next_gain, name=f"mm_down{layer}")
        return up, act, hn, nn

    up0, act0, h2, n2 = ffn_fwd(h1, n1, 0, wts["g_mix1"])

    p1 = _mm([n2], win1, name="mm_in1")
    yc = _sgu_fwd(p1, sgu_gn, sgu_w, sgu_bias, seg, name="sgu_fwd")
    cum = _fcum_fwd(p1, bf, name="fcum_fwd")
    c8 = cum[:, :8]
    cq = jnp.broadcast_to(c8[:, :, None], (s, 8, LANES)).reshape(s, 8 * LANES)
    ck = jnp.transpose(c8).reshape(8, 1, s)
    yd, lb = _fox_fwd(p1, cq, ck, name="fox_fwd")
    h3, n3 = _mm([yc, yd], wout1, res=h2, norm_out=wts["g_ffn"][1], name="mm_out1")

    up1, act1, h4, _ = ffn_fwd(h3, n3, 1, None)
    dh4, loss, g["final_norm"] = _final(h4, wts["g_final"], target, name="final")

    def ffn_bwd(dh, h, n, up, act, layer):
        dact = _mm([dh], wdown[layer], trans_w=True, out_dtype=BF16, ts=1024, nb=1408, name=f"mm_dact{layer}")
        g[f"w_down{layer}"] = _mm_tn([act], [dh], ts=1024, nb=512, name=f"mm_dwdown{layer}")
        event(f"dwdown{layer}", g[f"w_down{layer}"])
        dgate, dval, dcwg, dcwv, dcbg, dcbv = _ffn_core_bwd(dact, up, fcw[layer], fcb[layer], name=f"ffn_bwd{layer}")
        event(f"ffn_bwd{layer}", dgate)
        g[f"w_up{layer}"] = _mm_tn([n], [dgate, dval], ts=1024, nb=1408, name=f"mm_dwup{layer}")
        event(f"dwup{layer}", g[f"w_up{layer}"])
        dhn, g[f"g_ffn{layer}"] = _mm([dgate, dval], wup[layer], trans_w=True, ts=512,
                                      norm_bwd=(h, wts["g_ffn"][layer], dh), name=f"mm_dn_ffn{layer}")
        g[f"ffn_cw{layer}"] = jnp.concatenate([dcwg, dcwv], axis=1)
        g[f"ffn_cb{layer}"] = jnp.concatenate([dcbg, dcbv], axis=1)
        return dhn

    dh3 = ffn_bwd(dh4, h3, n3, up1, act1, 1)

    dy1 = _mm([dh3], wout1, trans_w=True, ts=1024, name="mm_dy1")
    g["w_out1"] = _mm_tn([yc, yd], [dh3], ts=1024, nb=512, name="mm_dwout1")
    event("dwout1", g["w_out1"])
    dzu, dzg, g["sgu_w"], g["sgu_bias"], g["sgu_gn"] = _sgu_bwd(
        p1, dy1, sgu_gn, sgu_w, sgu_wt, sgu_bias, seg, tril, name="sgu_bwd")
    delta = _fox_delta(dy1, yd, sel, name="fox_delta")
    dq, dk, dv, dck, dcq = _fox_bwd(p1, dy1, lb, delta, ck, name="fox_bwd")
    event("fox_bwd", dq)
    dcs = jnp.pad(jnp.transpose(dck.reshape(8, s)), ((0, 0), (0, LANES - 8)))
    df, g["bf"] = _fcum_bwd(dcs, dcq, p1, bf, name="fcum_bwd")
    dp1 = jnp.concatenate([dzu, dzg, dq, dk, dv, df], axis=1)
    g["w_in1"] = _mm_tn([n2], [dp1], ts=1024, nb=896, name="mm_dwin1")
    event("dwin1", g["w_in1"])
    dh2, g["g_mix1"] = _mm([dp1], win1, trans_w=True, norm_bwd=(h2, wts["g_mix1"], dh3), name="mm_dn_mix1")

    dh1 = ffn_bwd(dh2, h1, n1, up0, act0, 0)

    dy0 = _mm([dh1], wout0, trans_w=True, ts=1024, name="mm_dy0")
    g["w_out0"] = _mm_tn([ya, yb], [dh1], ts=1024, nb=512, name="mm_dwout0")
    event("dwout0", g["w_out0"])
    (*dp0, g["w4"], g["b4"], g["wa"], g["ba"], g["wx"], g["bx"], g["lam"], g["w3"], g["b3"]) = _even_core_bwd(
        dy0, p0, hl, w4, b4, wa, wat, ba, wx, wxt, bx, lam, w3, b3, name="even_bwd")
    event("even_bwd", dp0[0])
    g["w_in0"] = _mm_tn([n0], dp0, ts=1024, nb=512, name="mm_dwin0")
    event("dwin0", g["w_in0"])
    grad_x, g["g_mix0"] = _mm(dp0, win0, trans_w=True, norm_bwd=(x, wts["g_mix0"], dh1), name="mm_dn_mix0")
    return loss, grad_x, g


def _prepare_weights(nat):
    lane = jnp.arange(LANES)
    tril = (lane[:, None] >= lane[None, :]).astype(F32)
    sgu_tril = nat["sgu_w"][0] * tril
    w_in1 = nat["mix1_w_in"]
    return {
        "w_in0": nat["mix0_w_in"],
        "w_out0": nat["mix0_w_out"],
        "w_in1": jnp.pad(w_in1, ((0, 0), (0, 21 * LANES - w_in1.shape[1]))),
        "w_out1": nat["mix1_w_out"],
        "w_up": [nat["ffn_up"][l] for l in range(2)],
        "w_down": [nat["ffn_down"][l] for l in range(2)],
        "w4": nat["lru_conv_w"], "b4": nat["lru_conv_b"], "w3": nat["sconv_w"], "b3": nat["sconv_b"],
        "wa": _pair_blockdiag(nat["lru_wa"][0]).astype(BF16), "wx": _pair_blockdiag(nat["lru_wx"][0]).astype(BF16),
        "ba": nat["lru_ba"], "bx": nat["lru_bx"], "lam": nat["lru_lambda"],
        "ffn_cw": [nat["ffn_conv_w"][l] for l in range(2)],
        "ffn_cb": [nat["ffn_conv_b"][l:l + 1] for l in range(2)],
        "sgu_w": sgu_tril.astype(BF16), "sgu_wt": jnp.swapaxes(sgu_tril, 1, 2).astype(BF16),
        "sgu_bias": jnp.repeat(jnp.transpose(nat["sgu_b"][0]), 64, axis=1), "sgu_gn": nat["sgu_norm"],
        "bf": jnp.pad(nat["fox_bf"], ((0, 0), (0, LANES - 8))),
        "g_mix0": nat["mix0_norm"], "g_mix1": nat["mix1_norm"],
        "g_ffn": [nat["ffn_norm"][0:1], nat["ffn_norm"][1:2]], "g_final": nat["final_norm"].reshape(1, D_MODEL),
    }


def _natural_grads(g):
    small = {
        "mix0_norm": g["g_mix0"], "lru_conv_b": g["b4"],
        "lru_wa": _pair_diag_blocks(g["wa"])[None], "lru_ba": g["ba"],
        "lru_wx": _pair_diag_blocks(g["wx"])[None], "lru_bx": g["bx"],
        "lru_lambda": g["lam"], "sconv_b": g["b3"],
        "sgu_w": g["sgu_w"][None],
        "sgu_b": jnp.transpose(g["sgu_bias"].reshape(CHUNK, 8, 64).sum(axis=2))[None],
        "fox_bf": g["bf"][:, :8],
        "ffn_norm": jnp.concatenate([g["g_ffn0"], g["g_ffn1"]], axis=0),
        "ffn_conv_b": jnp.concatenate([g["ffn_cb0"], g["ffn_cb1"]], axis=0),
        "final_norm": g["final_norm"].reshape(D_MODEL),
        "lru_conv_w": g["w4"][None], "sconv_w": g["w3"][None],
        "ffn_conv_w": jnp.stack([g["ffn_cw0"], g["ffn_cw1"]]),
        "mix1_norm": g["g_mix1"], "sgu_norm": g["sgu_gn"],
    }
    big = {
        "mix0_w_in": g["w_in0"], "mix0_w_out": g["w_out0"],
        "mix1_w_in": g["w_in1"][:, :2568], "mix1_w_out": g["w_out1"],
        "ffn_up0": g["w_up0"], "ffn_up1": g["w_up1"],
        "ffn_down0": g["w_down0"], "ffn_down1": g["w_down1"],
    }
    return small, big


COL_SHARDED = ("mix0_w_in", "mix1_w_in", "ffn_up0", "ffn_up1")
COL_ALIGNED = ("mix0_w_in", "ffn_up0", "ffn_up1")
SMALL_SHARDED = ("lru_conv_w", "sconv_w", "ffn_conv_w", "mix1_norm", "sgu_norm")
SMALL_REPLICATED = ("mix0_norm", "lru_conv_b", "lru_wa", "lru_ba", "lru_wx", "lru_bx", "lru_lambda", "sconv_b",
                    "sgu_w", "sgu_b", "fox_bf", "ffn_norm", "ffn_conv_b", "final_norm")
WEIGHT_ORDER = ("mix0_norm", "mix0_w_in", "lru_conv_w", "lru_conv_b", "lru_wa", "lru_ba", "lru_wx", "lru_bx",
                "lru_lambda", "sconv_w", "sconv_b", "mix0_w_out", "mix1_norm", "mix1_w_in", "sgu_norm", "sgu_w",
                "sgu_b", "fox_bf", "mix1_w_out", "ffn_norm", "ffn_up", "ffn_conv_w", "ffn_conv_b", "ffn_down",
                "final_norm")


GATHER_GROUPS = (("mix0_w_in", "mix0_w_out"), ("ffn_up0",), ("ffn_down0", "mix1_w_in"),
                 ("mix1_w_out", "ffn_up1", "ffn_down1"))
CID_GATHER, CID_PAIR, CID_FIRST, CID_SECOND, CID_SWAP = 1, 2, 3, 4, 5


class _GradReducer:
    def __init__(self):
        x, y, c = _mesh_pos()
        self.send = jnp.stack([c] + [2 * (c * (1 - x) + (1 - c) * t) + (c * t + (1 - c) * (1 - y))
                                     for t in range(2)]).astype(jnp.int32)
        self.keep = jnp.stack([c] + [c * (2 * x + t) + (1 - c) * (2 * t + y) for t in range(2)]).astype(jnp.int32)
        self.mine = (c * y + (1 - c) * x).reshape(1).astype(jnp.int32)
        self.groups = {}

    @staticmethod
    def _view(name, a):
        if name in COL_ALIGNED:
            return a.reshape(2, a.shape[0] // 2, a.shape[1])
        if name in COL_SHARDED:
            a = _cols_to_shards(a)
            return a.reshape(N_CHIPS, 2, a.shape[1] // 2, a.shape[2])
        rows = a.shape[0] // (2 * N_CHIPS)
        return a.reshape(N_CHIPS, 2, rows, a.shape[1])

    def start(self, group, grads):
        names = tuple(grads)
        views = [self._view(k, grads[k]) for k in names]
        cols = [k in COL_ALIGNED for k in names]
        data = _send_other_half(views, cols, collective_id=CID_PAIR, name=f"rs_pair_{group}")
        self.groups[group] = dict(names=names, stage=0, views=views, cols=cols, data=data)

    def step(self, group, after):
        st = self.groups[group]
        names = st["names"]
        if st["stage"] == 0:
            sums = [_pair_sum(a, col, b, self.send, after, name=f"rs_pair_sum_{k}")
                    for k, a, col, b in zip(names, st["views"], st["cols"], st["data"])]
            st["from_sib"] = st["data"]
            st["data"] = _send_first(sums, collective_id=CID_FIRST, name=f"rs_first_{group}")
        elif st["stage"] == 1:
            sums = [_first_sum(a, col, b, r, self.keep, after, name=f"rs_first_sum_{k}")
                    for k, a, col, b, r in zip(names, st["views"], st["cols"], st["from_sib"], st["data"])]
            st["keep"] = [s32 for s32, _ in sums]
            st["data"] = _send_second([s16 for _, s16 in sums], collective_id=CID_SECOND, name=f"rs_second_{group}")
        else:
            st["mine"] = [_second_sum(s32, r, self.mine, after, name=f"rs_second_sum_{k}")
                          for k, s32, r in zip(names, st["keep"], st["data"])]
            st["data"] = _swap_halves(st["mine"], collective_id=CID_SWAP, name=f"rs_swap_{group}")
        st["stage"] += 1

    def result(self, group):
        st = self.groups[group]
        return {k: (a, b) for k, a, b in zip(st["names"], st["mine"], st["data"])}


def _train_step(x, target, w, m, v):
    x2 = x[0]
    t2 = target[0]
    chip = 2 * lax.axis_index("x") + lax.axis_index("y")
    core_arr = lax.axis_index("c").reshape(1).astype(jnp.int32)
    chip_arr = chip.reshape(1).astype(jnp.int32)

    big_shards = {
        "mix0_w_in": w["mix0_w_in"][0], "mix0_w_out": w["mix0_w_out"][0],
        "mix1_w_in": w["mix1_w_in"][0], "mix1_w_out": w["mix1_w_out"][0],
        "ffn_up0": w["ffn_up"][0], "ffn_up1": w["ffn_up"][1],
        "ffn_down0": w["ffn_down"][0], "ffn_down1": w["ffn_down"][1],
    }
    small_shards = [w[k] for k in SMALL_SHARDED]
    small_buf, small_offs = _pack(small_shards)
    full = {}
    small_all = None
    for gi, names in enumerate(GATHER_GROUPS):
        cols = [k in COL_ALIGNED for k in names]
        placed = [_place(big_shards[k], col, chip_arr, BF16, name=f"place_{k}") for k, col in zip(names, cols)]
        if gi == 0:
            placed.append(_place(small_buf, False, chip_arr, F32, name="place_small"))
            cols = cols + [False]
        gathered = _all_gather(placed, cols, collective_id=CID_GATHER, name=f"gather_weights{gi}")
        if gi == 0:
            small_all = gathered[-1].reshape(N_CHIPS, -1, LANES)
        for k, arr in zip(names, gathered):
            if k in COL_ALIGNED:
                full[k] = arr.reshape(arr.shape[0] * arr.shape[1], arr.shape[2])
            elif k in COL_SHARDED:
                full[k] = _cols_from_shards(arr.reshape((N_CHIPS, arr.shape[1] * arr.shape[2], arr.shape[3])))
            else:
                full[k] = arr.reshape(-1, arr.shape[3])
    per_chip = [_unpack(small_all[k], small_offs, [a.shape for a in small_shards]) for k in range(N_CHIPS)]
    lru_conv_w = jnp.concatenate([per_chip[k][0] for k in range(N_CHIPS)], axis=-1)[0]
    sconv_w = jnp.concatenate([per_chip[k][1] for k in range(N_CHIPS)], axis=-1)[0]
    ffn_conv_w = jnp.concatenate([per_chip[k][2] for k in range(N_CHIPS)], axis=-1)
    mix1_norm = jnp.concatenate([per_chip[k][3] for k in range(N_CHIPS)], axis=-1)
    sgu_norm = jnp.concatenate([per_chip[k][4] for k in range(N_CHIPS)], axis=-1)

    nat = {
        "mix0_w_in": full["mix0_w_in"], "mix0_w_out": full["mix0_w_out"],
        "mix1_w_in": full["mix1_w_in"], "mix1_w_out": full["mix1_w_out"],
        "ffn_up": [full["ffn_up0"], full["ffn_up1"]], "ffn_down": [full["ffn_down0"], full["ffn_down1"]],
        "lru_conv_w": lru_conv_w, "sconv_w": sconv_w, "ffn_conv_w": ffn_conv_w, "mix1_norm": mix1_norm,
        "sgu_norm": sgu_norm,
    }
    for k in SMALL_REPLICATED:
        nat[k] = w[k]
    wts = _prepare_weights(nat)

    reducer = _GradReducer()

    def on_event(name, g, token):
        if name == "dwup1":
            reducer.start("ffn1", {"ffn_up1": g["w_up1"], "ffn_down1": g["w_down1"]})
        elif name in ("dwout1", "fox_bwd"):
            reducer.step("ffn1", token)
        elif name == "dwin1":
            reducer.step("ffn1", token)
            reducer.start("mix1", {"mix1_w_in": g["w_in1"][:, :2568], "mix1_w_out": g["w_out1"]})
        elif name in ("dwdown0", "ffn_bwd0"):
            reducer.step("mix1", token)
        elif name == "dwup0":
            reducer.step("mix1", token)
            reducer.start("ffn0", {"ffn_up0": g["w_up0"], "ffn_down0": g["w_down0"]})
        elif name in ("dwout0", "even_bwd"):
            reducer.step("ffn0", token)
        elif name == "dwin0":
            reducer.step("ffn0", token)
            reducer.start("mix0", {"mix0_w_in": g["w_in0"], "mix0_w_out": g["w_out0"]})

    loss, grad_x, g = _local_step(x2, t2, wts, on_event)
    grads_small, _ = _natural_grads(g)

    small_names = SMALL_REPLICATED + SMALL_SHARDED
    small_list = [grads_small[k] for k in small_names] + [loss[:, :1]]
    sbuf, soffs = _pack(small_list)
    sred = _all_reduce_small(sbuf, name="reduce_small")
    small_red = _unpack(sred, soffs, [a.shape for a in small_list])
    loss_total = small_red[-1][0, 0]
    gsum = dict(zip(small_names, small_red[:-1]))
    for k in SMALL_SHARDED:
        width = w[k].shape[-1]
        gsum[k] = lax.dynamic_slice_in_dim(gsum[k], chip * width, width, axis=gsum[k].ndim - 1)

    out_g, out_d, out_m, out_v = {}, {}, {}, {}
    reduced = {}
    for group in ("ffn1", "mix1", "ffn0"):
        reduced.update(reducer.result(group))

    def update(pname, keys):
        mine = [reduced[k][0] for k in keys]
        theirs = [reduced[k][1] for k in keys]
        out_g[pname], out_d[pname], out_m[pname], out_v[pname] = _adamw_halves(
            w[pname], mine, theirs, m[pname], v[pname], core_arr, name=f"adamw_{pname}")
        return out_d[pname]

    reducer.step("mix0", update("ffn_up", ("ffn_up0", "ffn_up1")))
    small_w = [w[k] for k in small_names]
    pg, offs = _pack([gsum[k] for k in small_names])
    pw, _ = _pack(small_w)
    pm, _ = _pack([m[k] for k in small_names])
    pv, _ = _pack([v[k] for k in small_names])
    sd, sm, sv = _adamw(pw, pg, pm, pv, name="adamw_small")
    reducer.step("mix0", update("ffn_down", ("ffn_down0", "ffn_down1")))
    update("mix1_w_in", ("mix1_w_in",))
    reducer.step("mix0", update("mix1_w_out", ("mix1_w_out",)))
    reduced.update(reducer.result("mix0"))
    update("mix0_w_in", ("mix0_w_in",))
    update("mix0_w_out", ("mix0_w_out",))

    shapes = [a.shape for a in small_w]
    for k, dd, mm, vv in zip(small_names, _unpack(sd, offs, shapes), _unpack(sm, offs, shapes),
                             _unpack(sv, offs, shapes)):
        out_g[k], out_d[k], out_m[k], out_v[k] = gsum[k].reshape(w[k].shape), dd, mm, vv

    outs = [loss_total, grad_x[None]]
    for d in (out_g, out_d, out_m, out_v):
        outs.extend(d[k] for k in WEIGHT_ORDER)
    return tuple(outs)


def kernel(x, mix0_norm, mix0_w_in, lru_conv_w, lru_conv_b, lru_wa, lru_ba, lru_wx, lru_bx, lru_lambda, sconv_w, sconv_b, mix0_w_out, mix1_norm, mix1_w_in, sgu_norm, sgu_w, sgu_b, fox_bf, mix1_w_out, ffn_norm, ffn_up, ffn_conv_w, ffn_conv_b, ffn_down, final_norm, loss_target, m_mix0_norm, m_mix0_w_in, m_lru_conv_w, m_lru_conv_b, m_lru_wa, m_lru_ba, m_lru_wx, m_lru_bx, m_lru_lambda, m_sconv_w, m_sconv_b, m_mix0_w_out, m_mix1_norm, m_mix1_w_in, m_sgu_norm, m_sgu_w, m_sgu_b, m_fox_bf, m_mix1_w_out, m_ffn_norm, m_ffn_up, m_ffn_conv_w, m_ffn_conv_b, m_ffn_down, m_final_norm, v_mix0_norm, v_mix0_w_in, v_lru_conv_w, v_lru_conv_b, v_lru_wa, v_lru_ba, v_lru_wx, v_lru_bx, v_lru_lambda, v_sconv_w, v_sconv_b, v_mix0_w_out, v_mix1_norm, v_mix1_w_in, v_sgu_norm, v_sgu_w, v_sgu_b, v_fox_bf, v_mix1_w_out, v_ffn_norm, v_ffn_up, v_ffn_conv_w, v_ffn_conv_b, v_ffn_down, v_final_norm):
    w = dict(zip(WEIGHT_ORDER, (mix0_norm, mix0_w_in, lru_conv_w, lru_conv_b, lru_wa, lru_ba, lru_wx, lru_bx, lru_lambda, sconv_w, sconv_b, mix0_w_out, mix1_norm, mix1_w_in, sgu_norm, sgu_w, sgu_b, fox_bf, mix1_w_out, ffn_norm, ffn_up, ffn_conv_w, ffn_conv_b, ffn_down, final_norm)))
    m = dict(zip(WEIGHT_ORDER, (m_mix0_norm, m_mix0_w_in, m_lru_conv_w, m_lru_conv_b, m_lru_wa, m_lru_ba, m_lru_wx, m_lru_bx, m_lru_lambda, m_sconv_w, m_sconv_b, m_mix0_w_out, m_mix1_norm, m_mix1_w_in, m_sgu_norm, m_sgu_w, m_sgu_b, m_fox_bf, m_mix1_w_out, m_ffn_norm, m_ffn_up, m_ffn_conv_w, m_ffn_conv_b, m_ffn_down, m_final_norm)))
    v = dict(zip(WEIGHT_ORDER, (v_mix0_norm, v_mix0_w_in, v_lru_conv_w, v_lru_conv_b, v_lru_wa, v_lru_ba, v_lru_wx, v_lru_bx, v_lru_lambda, v_sconv_w, v_sconv_b, v_mix0_w_out, v_mix1_norm, v_mix1_w_in, v_sgu_norm, v_sgu_w, v_sgu_b, v_fox_bf, v_mix1_w_out, v_ffn_norm, v_ffn_up, v_ffn_conv_w, v_ffn_conv_b, v_ffn_down, v_final_norm)))
    return _train_step(x, loss_target, w, m, v)
```

```python
import functools

import jax
import jax.numpy as jnp
from jax import lax
from jax.experimental import pallas as pl
from jax.experimental.pallas import tpu as pltpu
from jax.experimental.pallas import tpu_sc as plsc

F32 = jnp.float32
BF16 = jnp.bfloat16
MESH = pl.DeviceIdType.MESH

D_MODEL = 1024
LANES = 128
SUBLANES = 8
N_CHIPS = 4
EPS = 1e-6
LRU_C = 8.0
D_FF = 2816
FFN_CB = 256
CHUNK = 128
NEG = -1e30

ADAM_LR = 0.001
ADAM_B1 = 0.9
ADAM_B2 = 0.999
ADAM_EPS = 1e-08
ADAM_WD = 0.01
ADAM_STEP = 10
ADAM_C1 = 1.0 - ADAM_B1 ** ADAM_STEP
ADAM_C2 = 1.0 - ADAM_B2 ** ADAM_STEP

_GELU_C = 0.7978845608028654
_GELU_A = 0.044715


def _sigmoid(x):
    return 1.0 / (1.0 + jnp.exp(-x))


def _sigmoid_tanh(x):
    return 0.5 * jnp.tanh(0.5 * x) + 0.5


def _log1p_pos(e):
    w = 1.0 + e
    return jnp.where(w == 1.0, e, jnp.log(w) * (e / (w - 1.0)))


def _softplus(x):
    return jnp.maximum(x, 0.0) + _log1p_pos(jnp.exp(-jnp.abs(x)))


def _gelu(x):
    t = jnp.tanh(_GELU_C * (x + _GELU_A * (x * x * x)))
    return 0.5 * x * (1.0 + t), t


def _gelu_grad(x, t):
    return 0.5 * (1.0 + t) + 0.5 * x * (1.0 - t * t) * (_GELU_C * (1.0 + 3.0 * _GELU_A * x * x))


def _rows(shape):
    return lax.broadcasted_iota(jnp.int32, shape, 0)


def _lanes(shape):
    return lax.broadcasted_iota(jnp.int32, shape, 1)


def _shift_down(x, halo8, j):
    if j == 0:
        return x
    r = pltpu.roll(x, j, 0)
    hr = pltpu.roll(halo8, j, 0)
    top = jnp.where(_rows(hr.shape) < j, hr, r[:SUBLANES])
    return jnp.concatenate([top, r[SUBLANES:]], axis=0)


def _shift_up(x, next8, j):
    if j == 0:
        return x
    n = x.shape[0]
    r = pltpu.roll(x, n - j, 0)
    nr = pltpu.roll(next8, SUBLANES - j, 0)
    bot = jnp.where(_rows(nr.shape) >= SUBLANES - j, nr, r[n - SUBLANES:])
    return jnp.concatenate([r[:n - SUBLANES], bot], axis=0)


def _scan_fwd(a, u):
    n = a.shape[0]
    row = _rows(a.shape)
    h = u
    k = 1
    while k < n:
        keep = row >= k
        h_sh = jnp.where(keep, pltpu.roll(h, k, 0), 0.0)
        a_sh = jnp.where(keep, pltpu.roll(a, k, 0), 1.0)
        h = a * h_sh + h
        a = a * a_sh
        k *= 2
    return h, a


def _scan_rev(b, d):
    n = b.shape[0]
    row = _rows(b.shape)
    g = d
    k = 1
    while k < n:
        keep = row < n - k
        g_sh = jnp.where(keep, pltpu.roll(g, n - k, 0), 0.0)
        b_sh = jnp.where(keep, pltpu.roll(b, n - k, 0), 1.0)
        g = b * g_sh + g
        b = b * b_sh
        k *= 2
    return g, b


def _cumsum_fwd(x):
    n = x.shape[0]
    row = _rows(x.shape)
    k = 1
    while k < n:
        x = x + jnp.where(row >= k, pltpu.roll(x, k, 0), 0.0)
        k *= 2
    return x


def _cumsum_rev(x):
    n = x.shape[0]
    row = _rows(x.shape)
    k = 1
    while k < n:
        x = x + jnp.where(row < n - k, pltpu.roll(x, n - k, 0), 0.0)
        k *= 2
    return x


def _dot(a, b):
    return lax.dot_general(a, b, (((1,), (0,)), ((), ())), preferred_element_type=F32)


def _dot_nt(a, b):
    return lax.dot_general(a, b, (((1,), (1,)), ((), ())), preferred_element_type=F32)


def _dot_tn(a, b):
    return lax.dot_general(a, b, (((0,), (0,)), ((), ())), preferred_element_type=F32)


def _dot_split(x, m_bf16):
    hi = x.astype(BF16)
    lo = (x - hi.astype(F32)).astype(BF16)
    return _dot(hi, m_bf16) + _dot(lo, m_bf16)


def _tile_rows(ts, s):
    return min(ts, s)


def _mm(a_list, w, *, trans_w=False, res=None, norm_bwd=None, norm_out=None, out_dtype=F32, ts=512, nb=None,
        name):
    s = a_list[0].shape[0]
    ks = [a.shape[1] for a in a_list]
    k = sum(ks)
    n = w.shape[0] if trans_w else w.shape[1]
    ts = _tile_rows(ts, s)
    nb = n if nb is None else nb
    na = len(a_list)
    has_res = res is not None
    fused = norm_bwd is not None
    normed = norm_out is not None
    offs = [sum(ks[:p]) for p in range(na)]

    def body(*refs):
        a_refs = refs[:na]
        w_ref = refs[na]
        acc = None
        for a_ref, off, kk in zip(a_refs, offs, ks):
            a = a_ref[...].astype(BF16)
            if trans_w:
                part = _dot_nt(a, w_ref[:, off:off + kk])
            else:
                part = _dot(a, w_ref[off:off + kk, :])
            acc = part if acc is None else acc + part
        if has_res:
            acc = acc + refs[na + 1][...]
        if normed:
            gn_ref, o_ref, n_ref = refs[-3:]
            o_ref[...] = acc.astype(out_dtype)
            r = lax.rsqrt(jnp.mean(acc * acc, axis=-1, keepdims=True) + EPS)
            n_ref[...] = ((acc * r) * gn_ref[...]).astype(BF16)
            return
        if not fused:
            refs[-1][...] = acc.astype(out_dtype)
            return
        h_ref, g_ref, dres_ref, dh_ref, dg_ref = refs[na + 1:]
        i = pl.program_id(1)
        x = h_ref[...]
        r = lax.rsqrt(jnp.mean(x * x, axis=-1, keepdims=True) + EPS)
        xhat = x * r
        part = jnp.sum(acc * xhat, axis=0, keepdims=True)

        @pl.when(i == 0)
        def _():
            dg_ref[...] = part

        @pl.when(i > 0)
        def _():
            dg_ref[...] += part

        dxh = acc * g_ref[...]
        dh_ref[...] = dres_ref[...] + r * (dxh - xhat * jnp.mean(dxh * xhat, axis=-1, keepdims=True))

    in_specs = [pl.BlockSpec((ts, kk), lambda j, i: (i, 0)) for kk in ks]
    if trans_w:
        in_specs.append(pl.BlockSpec((nb, k), lambda j, i: (j, 0)))
    else:
        in_specs.append(pl.BlockSpec((k, nb), lambda j, i: (0, j)))
    args = list(a_list) + [w]
    tile = pl.BlockSpec((ts, nb), lambda j, i: (i, j))
    if has_res:
        in_specs.append(tile)
        args.append(res)
    if fused:
        assert nb == n and not has_res
        vec = pl.BlockSpec((1, n), lambda j, i: (0, 0))
        h, g, dres = norm_bwd
        return pl.pallas_call(
            body, name=name, grid=(1, s // ts), in_specs=in_specs + [tile, vec, tile],
            out_specs=(tile, vec),
            out_shape=(jax.ShapeDtypeStruct((s, n), F32), jax.ShapeDtypeStruct((1, n), F32)),
        )(*args, h, g, dres)
    if normed:
        assert nb == n and out_dtype == F32
        vec = pl.BlockSpec((1, n), lambda j, i: (0, 0))
        return pl.pallas_call(
            body, name=name, grid=(1, s // ts), in_specs=in_specs + [vec], out_specs=(tile, tile),
            out_shape=(jax.ShapeDtypeStruct((s, n), F32), jax.ShapeDtypeStruct((s, n), BF16)),
        )(*args, norm_out)
    return pl.pallas_call(
        body, name=name, grid=(n // nb, s // ts), in_specs=in_specs, out_specs=tile,
        out_shape=jax.ShapeDtypeStruct((s, n), out_dtype),
    )(*args)


def _mm_tn(a_list, b_list, *, ts=512, nb=None, name):
    s = b_list[0].shape[0]
    ks = [a.shape[1] for a in a_list]
    k = sum(ks)
    width = b_list[0].shape[1]
    n = width * len(b_list)
    ts = _tile_rows(ts, s)
    nb = width if nb is None else nb
    per = width // nb
    na = len(a_list)
    nparts = len(b_list)

    def body(*refs):
        a_refs = refs[:na]
        b_refs = refs[na:na + nparts]
        o_ref = refs[-1]
        j = pl.program_id(0)
        i = pl.program_id(1)
        parts = [r[...].astype(BF16) for r in a_refs]
        a = parts[0] if na == 1 else jnp.concatenate(parts, axis=1)

        def accumulate(b_ref):
            upd = _dot_tn(a, b_ref[...].astype(BF16))

            @pl.when(i == 0)
            def _():
                o_ref[...] = upd

            @pl.when(i > 0)
            def _():
                o_ref[...] += upd

        if nparts == 1:
            accumulate(b_refs[0])
        else:
            for part, b_ref in enumerate(b_refs):
                pl.when(j // per == part)(functools.partial(accumulate, b_ref))

    in_specs = [pl.BlockSpec((ts, kk), lambda j, i: (i, 0)) for kk in ks]
    for part in range(nparts):
        in_specs.append(pl.BlockSpec(
            (ts, nb), lambda j, i, part=part: (i, jnp.clip(j - part * per, 0, per - 1))))
    return pl.pallas_call(
        body, name=name, grid=(n // nb, s // ts), in_specs=in_specs,
        out_specs=pl.BlockSpec((k, nb), lambda j, i: (0, j)),
        out_shape=jax.ShapeDtypeStruct((k, n), F32),
    )(*a_list, *b_list)


def _norm_fwd(h, g, *, ts=512, name):
    s, d = h.shape
    ts = _tile_rows(ts, s)

    def body(h_ref, g_ref, n_ref):
        x = h_ref[...]
        r = lax.rsqrt(jnp.mean(x * x, axis=-1, keepdims=True) + EPS)
        n_ref[...] = ((x * r) * g_ref[...]).astype(BF16)

    return pl.pallas_call(
        body, name=name, grid=(s // ts,),
        in_specs=[pl.BlockSpec((ts, d), lambda i: (i, 0)), pl.BlockSpec((1, d), lambda i: (0, 0))],
        out_specs=pl.BlockSpec((ts, d), lambda i: (i, 0)),
        out_shape=jax.ShapeDtypeStruct((s, d), BF16),
    )(h, g)


def _final(h, g, target, *, ts=512, name):
    s, d = h.shape
    ts = _tile_rows(ts, s)
    nt = s // ts

    def body(h_ref, g_ref, t_ref, dh_ref, loss_ref, dg_ref, acc_ref):
        i = pl.program_id(0)
        x = h_ref[...]
        r = lax.rsqrt(jnp.mean(x * x, axis=-1, keepdims=True) + EPS)
        xhat = x * r
        gv = g_ref[...]
        err = xhat * gv - t_ref[...]
        sq = jnp.sum(err * err, axis=0, keepdims=True)
        dy = err * (1.0 / d)
        part = jnp.sum(dy * xhat, axis=0, keepdims=True)

        @pl.when(i == 0)
        def _():
            acc_ref[...] = sq
            dg_ref[...] = part

        @pl.when(i > 0)
        def _():
            acc_ref[...] += sq
            dg_ref[...] += part

        dxh = dy * gv
        dh_ref[...] = r * (dxh - xhat * jnp.mean(dxh * xhat, axis=-1, keepdims=True))

        @pl.when(i == nt - 1)
        def _():
            tot = jnp.sum(acc_ref[...], axis=1, keepdims=True) * (0.5 / d)
            loss_ref[...] = jnp.broadcast_to(tot, (1, LANES))

    tile = pl.BlockSpec((ts, d), lambda i: (i, 0))
    vec = pl.BlockSpec((1, d), lambda i: (0, 0))
    return pl.pallas_call(
        body, name=name, grid=(nt,), in_specs=[tile, vec, tile],
        out_specs=(tile, pl.BlockSpec((1, LANES), lambda i: (0, 0)), vec),
        out_shape=(jax.ShapeDtypeStruct((s, d), F32), jax.ShapeDtypeStruct((1, LANES), F32),
                   jax.ShapeDtypeStruct((1, d), F32)),
        scratch_shapes=[pltpu.VMEM((1, d), F32)],
    )(h, g, target)


def _halo_map(ts, width_blocks):
    per = ts // SUBLANES

    def index(j, i):
        return (jnp.maximum(i * per - 1, 0), width_blocks(j))

    return index


def _even_gates(xc, wa, ba, wx, bx, sp):
    xb = xc.astype(BF16)
    r = _sigmoid(_dot(xb, wa) + ba)
    ig = _sigmoid(_dot(xb, wx) + bx)
    la = (-LRU_C) * r * sp
    a = jnp.exp(la)
    a2 = a * a
    m = jnp.sqrt(-jnp.tanh(la) * (1.0 + a2))
    return r, ig, la, a, a2, m


def _even_core_fwd(p, w4, b4, wa, ba, wx, bx, lam, w3, b3, *, ts=512, name):
    s = p.shape[0]
    ts = _tile_rows(ts, s)
    nt = s // ts
    nblk = 4

    def body(xa_ref, ga_ref, cp_ref, bp_ref, vb_ref, xah_ref, cph_ref, vbh_ref,
             w4_ref, b4_ref, wa_ref, ba_ref, wx_ref, bx_ref, lam_ref, w3_ref, b3_ref,
             ya_ref, yb_ref, hl_ref, hcar_ref):
        i = pl.program_id(1)
        first = (i > 0).astype(F32)
        xa, ga, cp, bp, vb = xa_ref[...], ga_ref[...], cp_ref[...], bp_ref[...], vb_ref[...]
        xa_h = xah_ref[...] * first
        s_h = cph_ref[...] * vbh_ref[...] * first

        xc = b4_ref[...] + w4_ref[3:4, :] * xa
        for k in range(3):
            xc = xc + w4_ref[k:k + 1, :] * _shift_down(xa, xa_h, 3 - k)
        sp = _softplus(-lam_ref[...])
        _, ig, _, a, _, m = _even_gates(xc, wa_ref[0], ba_ref[...], wx_ref[0], bx_ref[...], sp)
        u = m * (ig * xc)
        hs, acum = _scan_fwd(a, u)

        @pl.when(i == 0)
        def _():
            hcar_ref[...] = jnp.zeros_like(hcar_ref)

        hs = hs + acum * hcar_ref[0:1, :]
        hl_ref[...] = hs
        hcar_ref[0:1, :] = hl_ref[ts - 1:ts, :]
        ge, _ = _gelu(ga)
        ya_ref[...] = (hs * ge).astype(BF16)

        sv = cp * vb
        sc = b3_ref[...] + w3_ref[2:3, :] * sv
        for k in range(2):
            sc = sc + w3_ref[k:k + 1, :] * _shift_down(sv, s_h, 2 - k)
        yb_ref[...] = (bp * sc).astype(BF16)

    parts = [pl.BlockSpec((ts, LANES), lambda j, i, q=q: (i, 4 * q + j)) for q in range(5)]
    halos = [pl.BlockSpec((SUBLANES, LANES), _halo_map(ts, lambda j, q=q: 4 * q + j)) for q in (0, 2, 4)]
    vec = pl.BlockSpec((1, LANES), lambda j, i: (0, j))
    out = pl.BlockSpec((ts, LANES), lambda j, i: (i, j))
    return pl.pallas_call(
        body, name=name, grid=(nblk, nt),
        in_specs=parts + halos + [
                  pl.BlockSpec((4, LANES), lambda j, i: (0, j)), vec,
                  pl.BlockSpec((1, LANES, LANES), lambda j, i: (j, 0, 0)), vec,
                  pl.BlockSpec((1, LANES, LANES), lambda j, i: (j, 0, 0)), vec, vec,
                  pl.BlockSpec((3, LANES), lambda j, i: (0, j)), vec],
        out_specs=(out, out, out),
        out_shape=(jax.ShapeDtypeStruct((s, 4 * LANES), BF16), jax.ShapeDtypeStruct((s, 4 * LANES), BF16),
                   jax.ShapeDtypeStruct((s, 4 * LANES), F32)),
        scratch_shapes=[pltpu.VMEM((SUBLANES, LANES), F32)],
    )(*([p] * 8), w4, b4, wa, ba, wx, bx, lam, w3, b3)


def _even_core_bwd(dy, p, hl, w4, b4, wa, wat, ba, wx, wxt, bx, lam, w3, b3, *, ts=512, name):
    s = p.shape[0]
    ts = _tile_rows(ts, s)
    nt = s // ts
    nblk = 4
    per = ts // SUBLANES

    def body(dya_ref, dyb_ref, xa_ref, ga_ref, cp_ref, bp_ref, vb_ref, xah_ref, cph_ref, vbh_ref, hl_ref, hh_ref,
             w4_ref, b4_ref, wa_ref, wat_ref, ba_ref, wx_ref, wxt_ref, bx_ref, lam_ref, w3_ref, b3_ref,
             dxa_ref, dga_ref, dcp_ref, dbp_ref, dvb_ref,
             dw4_ref, db4_ref, dwa_ref, dba_ref, dwx_ref, dbx_ref, dlam_ref, dw3_ref, db3_ref,
             dxc_nx, dsc_nx, cg_ref):
        i = pl.program_id(1)
        ti = nt - 1 - i
        first = (ti > 0).astype(F32)
        xa, ga, cp, bp, vb = xa_ref[...], ga_ref[...], cp_ref[...], bp_ref[...], vb_ref[...]
        xa_h = xah_ref[...] * first
        s_h = cph_ref[...] * vbh_ref[...] * first
        h_h = hh_ref[...] * first

        @pl.when(i == 0)
        def _():
            dxc_nx[...] = jnp.zeros_like(dxc_nx)
            dsc_nx[...] = jnp.zeros_like(dsc_nx)
            cg_ref[...] = jnp.zeros_like(cg_ref)
            for ref in (dw4_ref, db4_ref, dwa_ref, dba_ref, dwx_ref, dbx_ref, dlam_ref, dw3_ref, db3_ref):
                ref[...] = jnp.zeros_like(ref)

        xa_sh = [_shift_down(xa, xa_h, 3 - k) for k in range(3)] + [xa]
        xc = b4_ref[...]
        for k in range(4):
            xc = xc + w4_ref[k:k + 1, :] * xa_sh[k]
        lamv = lam_ref[...]
        sp = _softplus(-lamv)
        r, ig, _, a, a2, m = _even_gates(xc, wa_ref[0], ba_ref[...], wx_ref[0], bx_ref[...], sp)
        sv = cp * vb
        sv_sh = [_shift_down(sv, s_h, 2 - k) for k in range(2)] + [sv]
        sc = b3_ref[...]
        for k in range(3):
            sc = sc + w3_ref[k:k + 1, :] * sv_sh[k]
        hs = hl_ref[...]
        h_prev = _shift_down(hs, h_h, 1)

        dya = dya_ref[...]
        dyb = dyb_ref[...]
        ge, gt = _gelu(ga)
        dga = dya * hs * _gelu_grad(ga, gt)
        dh = dya * ge

        ones8 = jnp.ones((SUBLANES, LANES), F32)
        b = _shift_up(a, ones8, 1)
        g, bcum = _scan_rev(b, dh)
        g = g + bcum * cg_ref[0:1, :]
        ag = a * g
        cg_ref[...] = ag[:SUBLANES]

        da = g * h_prev
        xi = ig * xc
        dm = g * xi
        dig = g * m * xc
        dxc = g * m * ig
        dla = da * a - dm * (a2 / m)
        dr = dla * ((-LRU_C) * sp)
        dlam_ref[...] += jnp.sum(dla * r, axis=0, keepdims=True) * (LRU_C * _sigmoid(-lamv))
        dra = dr * r * (1.0 - r)
        dia = dig * ig * (1.0 - ig)
        drab = dra.astype(BF16)
        diab = dia.astype(BF16)
        xcb = xc.astype(BF16)
        dxc = dxc + _dot(drab, wat_ref[0]) + _dot(diab, wxt_ref[0])
        dwa_ref[0] += _dot_tn(xcb, drab)
        dwx_ref[0] += _dot_tn(xcb, diab)
        dba_ref[...] += jnp.sum(dra, axis=0, keepdims=True)
        dbx_ref[...] += jnp.sum(dia, axis=0, keepdims=True)

        nx = dxc_nx[...]
        dxa = w4_ref[3:4, :] * dxc
        for k in range(3):
            dxa = dxa + w4_ref[k:k + 1, :] * _shift_up(dxc, nx, 3 - k)
        for k in range(4):
            dw4_ref[k:k + 1, :] += jnp.sum(dxc * xa_sh[k], axis=0, keepdims=True)
        db4_ref[...] += jnp.sum(dxc, axis=0, keepdims=True)
        dxc_nx[...] = dxc[:SUBLANES]

        dbp = dyb * sc
        dsc = dyb * bp
        nsc = dsc_nx[...]
        ds = w3_ref[2:3, :] * dsc
        for k in range(2):
            ds = ds + w3_ref[k:k + 1, :] * _shift_up(dsc, nsc, 2 - k)
        for k in range(3):
            dw3_ref[k:k + 1, :] += jnp.sum(dsc * sv_sh[k], axis=0, keepdims=True)
        db3_ref[...] += jnp.sum(dsc, axis=0, keepdims=True)
        dsc_nx[...] = dsc[:SUBLANES]

        dxa_ref[...] = dxa.astype(BF16)
        dga_ref[...] = dga.astype(BF16)
        dcp_ref[...] = (ds * vb).astype(BF16)
        dbp_ref[...] = dbp.astype(BF16)
        dvb_ref[...] = (ds * cp).astype(BF16)

    def rev(j, i):
        return (nt - 1 - i, j)

    def rev_halo(col):
        def index(j, i):
            return (jnp.maximum((nt - 1 - i) * per - 1, 0), col(j))
        return index

    parts = [pl.BlockSpec((ts, LANES), lambda j, i, q=q: (nt - 1 - i, 4 * q + j)) for q in range(5)]
    halos = [pl.BlockSpec((SUBLANES, LANES), rev_halo(lambda j, q=q: 4 * q + j)) for q in (0, 2, 4)]
    one = pl.BlockSpec((ts, LANES), rev)
    vec = pl.BlockSpec((1, LANES), lambda j, i: (0, j))
    mat = pl.BlockSpec((1, LANES, LANES), lambda j, i: (j, 0, 0))
    w4s = pl.BlockSpec((4, LANES), lambda j, i: (0, j))
    w3s = pl.BlockSpec((3, LANES), lambda j, i: (0, j))
    f = jax.ShapeDtypeStruct
    return pl.pallas_call(
        body, name=name, grid=(nblk, nt),
        in_specs=[one, pl.BlockSpec((ts, LANES), lambda j, i: (nt - 1 - i, 4 + j))] + parts + halos + [
                  one, pl.BlockSpec((SUBLANES, LANES), rev_halo(lambda j: j)),
                  w4s, vec, mat, mat, vec, mat, mat, vec, vec, w3s, vec],
        out_specs=(one,) * 5 + (w4s, vec, mat, vec, mat, vec, vec, w3s, vec),
        out_shape=(f((s, 4 * LANES), BF16),) * 5 + (
                   f((4, 4 * LANES), F32), f((1, 4 * LANES), F32),
                   f((4, LANES, LANES), F32), f((1, 4 * LANES), F32),
                   f((4, LANES, LANES), F32), f((1, 4 * LANES), F32), f((1, 4 * LANES), F32),
                   f((3, 4 * LANES), F32), f((1, 4 * LANES), F32)),
        scratch_shapes=[pltpu.VMEM((SUBLANES, LANES), F32), pltpu.VMEM((SUBLANES, LANES), F32),
                        pltpu.VMEM((SUBLANES, LANES), F32)],
    )(dy, dy, *([p] * 8), hl, hl, w4, b4, wa, wat, ba, wx, wxt, bx, lam, w3, b3)


def _ffn_conv(u_ref, uh_ref, w_ref, b_ref, first):
    u = u_ref[...].astype(F32)
    u_h = uh_ref[...].astype(F32)[SUBLANES:] * first
    u_sh = [_shift_down(u, u_h, 2 - k) for k in range(2)] + [u]
    hc = b_ref[...]
    for k in range(3):
        hc = hc + w_ref[k:k + 1, :] * u_sh[k]
    return hc, u_sh


def _ffn_specs(ts, row, halo_row):
    nblk = D_FF // FFN_CB
    specs = []
    for off in (0, nblk):
        specs.append(pl.BlockSpec((ts, FFN_CB), lambda j, i, off=off: (row(i), off + j)))
        specs.append(pl.BlockSpec((16, FFN_CB), lambda j, i, off=off: (halo_row(i), off + j)))
        specs.append(pl.BlockSpec((3, FFN_CB), lambda j, i, off=off: (0, off + j)))
        specs.append(pl.BlockSpec((1, FFN_CB), lambda j, i, off=off: (0, off + j)))
    return specs


FFN_STRIP = 4 * SUBLANES
FFN_HALO = 2 * SUBLANES


def _ffn_stage(u_ref, uh_ref, dst_ref, first):
    dst_ref[0:FFN_HALO, :] = jnp.where(first, uh_ref[...], jnp.zeros_like(uh_ref))
    dst_ref[FFN_HALO:, :] = u_ref[...]


def _ffn_strip_conv(u_ref, r, w, b):
    win = u_ref[pl.ds(r, FFN_HALO + FFN_STRIP), :].astype(F32)
    cur, before = win[FFN_HALO:], win[SUBLANES:FFN_HALO]
    sh = [_shift_down(cur, before, 2 - k) for k in range(2)] + [cur]
    return b + w[0:1] * sh[0] + w[1:2] * sh[1] + w[2:3] * sh[2], sh


def _ffn_core_fwd(up, w, b, *, ts=512, name):
    s = up.shape[0]
    ts = _tile_rows(ts, s)
    nt = s // ts
    nblk = D_FF // FFN_CB
    per = ts // 16

    def body(g_ref, gh_ref, wg_ref, bg_ref, v_ref, vh_ref, wv_ref, bv_ref, act_ref):
        first = (pl.program_id(1) > 0).astype(F32)
        gate, _ = _ffn_conv(g_ref, gh_ref, wg_ref, bg_ref, first)
        val, _ = _ffn_conv(v_ref, vh_ref, wv_ref, bv_ref, first)
        act_ref[...] = (gate * _sigmoid_tanh(gate) * val).astype(BF16)

    return pl.pallas_call(
        body, name=name, grid=(nblk, nt),
        in_specs=_ffn_specs(ts, lambda i: i, lambda i: jnp.maximum(i * per - 1, 0)),
        out_specs=pl.BlockSpec((ts, FFN_CB), lambda j, i: (i, j)),
        out_shape=jax.ShapeDtypeStruct((s, D_FF), BF16),
    )(up, up, w, b, up, up, w, b)


def _ffn_core_bwd(dact, up, w, b, *, ts=1024, name):
    s = up.shape[0]
    ts = _tile_rows(ts, s)
    nt = s // ts
    nblk = D_FF // FFN_CB
    per = ts // 16
    strip, halo = FFN_STRIP, FFN_HALO
    nstrips = ts // strip

    def fold(x):
        out = x[:SUBLANES]
        for r0 in range(SUBLANES, strip, SUBLANES):
            out = out + x[r0:r0 + SUBLANES]
        return out

    def body(da_ref, g_ref, gh_ref, wg_ref, bg_ref, v_ref, vh_ref, wv_ref, bv_ref,
             dg_ref, dv_ref, dwg_ref, dwv_ref, dbg_ref, dbv_ref, nxg_ref, nxv_ref, ug_ref, uv_ref):
        i = pl.program_id(1)
        first = nt - 1 - i > 0

        @pl.when(i == 0)
        def _():
            for ref in (nxg_ref, nxv_ref, dwg_ref, dwv_ref, dbg_ref, dbv_ref):
                ref[...] = jnp.zeros_like(ref)

        _ffn_stage(g_ref, gh_ref, ug_ref, first)
        _ffn_stage(v_ref, vh_ref, uv_ref, first)
        wg, wv, bg, bv = wg_ref[...], wv_ref[...], bg_ref[...], bv_ref[...]
        conv = _ffn_strip_conv

        def conv_t(d, nxt, w):
            out = w[2:3] * d
            for k in range(2):
                out = out + w[k:k + 1] * _shift_up(d, nxt, 2 - k)
            return out

        def step(t, carry):
            nxg, nxv, awg, awv, abg, abv = carry
            r = pl.multiple_of((nstrips - 1 - t) * strip, strip)
            gate, g_sh = conv(ug_ref, r, wg, bg)
            val, v_sh = conv(uv_ref, r, wv, bv)
            da = da_ref[pl.ds(r, strip), :].astype(F32)
            sg = _sigmoid_tanh(gate)
            dgate = da * val * (sg * (1.0 + gate * (1.0 - sg)))
            dval = da * (gate * sg)
            dg_ref[pl.ds(r, strip), :] = conv_t(dgate, nxg, wg).astype(BF16)
            dv_ref[pl.ds(r, strip), :] = conv_t(dval, nxv, wv).astype(BF16)
            awg = tuple(a + fold(dgate * sh) for a, sh in zip(awg, g_sh))
            awv = tuple(a + fold(dval * sh) for a, sh in zip(awv, v_sh))
            return dgate[:SUBLANES], dval[:SUBLANES], awg, awv, abg + fold(dgate), abv + fold(dval)

        zero = jnp.zeros((SUBLANES, FFN_CB), F32)
        init = (nxg_ref[...], nxv_ref[...], (zero,) * 3, (zero,) * 3, zero, zero)
        nxg, nxv, awg, awv, abg, abv = lax.fori_loop(0, nstrips, step, init)
        nxg_ref[...] = nxg
        nxv_ref[...] = nxv
        for k in range(3):
            dwg_ref[k:k + 1, :] += jnp.sum(awg[k], axis=0, keepdims=True)
            dwv_ref[k:k + 1, :] += jnp.sum(awv[k], axis=0, keepdims=True)
        dbg_ref[...] += jnp.sum(abg, axis=0, keepdims=True)
        dbv_ref[...] += jnp.sum(abv, axis=0, keepdims=True)

    def rev(i):
        return nt - 1 - i

    tile = pl.BlockSpec((ts, FFN_CB), lambda j, i: (rev(i), j))
    w_out = pl.BlockSpec((3, FFN_CB), lambda j, i: (0, j))
    b_out = pl.BlockSpec((1, FFN_CB), lambda j, i: (0, j))
    f = jax.ShapeDtypeStruct
    return pl.pallas_call(
        body, name=name, grid=(nblk, nt),
        in_specs=[tile] + _ffn_specs(ts, rev, lambda i: jnp.maximum(rev(i) * per - 1, 0)),
        out_specs=(tile, tile, w_out, w_out, b_out, b_out),
        out_shape=(f((s, D_FF), BF16), f((s, D_FF), BF16), f((3, D_FF), F32), f((3, D_FF), F32),
                   f((1, D_FF), F32), f((1, D_FF), F32)),
        scratch_shapes=[pltpu.VMEM((SUBLANES, FFN_CB), F32), pltpu.VMEM((SUBLANES, FFN_CB), F32),
                        pltpu.VMEM((ts + halo, FFN_CB), BF16), pltpu.VMEM((ts + halo, FFN_CB), BF16)],
    )(dact, up, up, w, b, up, up, w, b)


def _sgu_forward_block(zu, zg, gn, w_ref, bias, seg):
    u, tu = _gelu(zu)
    g, tg = _gelu(zg)
    ms = _dot_split(g * g, seg)
    rs = lax.rsqrt(ms + EPS)
    ghat = g * rs
    gv = ghat * gn
    gvb = gv.astype(BF16)
    lane = _lanes((CHUNK, LANES))
    chunks = []
    for c in range(zu.shape[0] // CHUNK):
        gc = gvb[c * CHUNK:(c + 1) * CHUNK]
        mix = jnp.where(lane < 64, _dot(w_ref[0], gc), _dot(w_ref[1], gc)) + bias
        chunks.append(mix)
    mixed = chunks[0] if len(chunks) == 1 else jnp.concatenate(chunks, axis=0)
    return u, tu, g, tg, rs, ghat, gvb, mixed


def _sgu_fwd(p1, gn, w, bias, seg, *, ts=512, name):
    s = p1.shape[0]
    ts = _tile_rows(ts, s)

    def body(zu_ref, zg_ref, gn_ref, w_ref, bias_ref, seg_ref, yc_ref):
        u, _, _, _, _, _, _, mixed = _sgu_forward_block(
            zu_ref[...], zg_ref[...], gn_ref[...], w_ref, bias_ref[...], seg_ref[...])
        yc_ref[...] = (u * mixed).astype(BF16)

    return pl.pallas_call(
        body, name=name, grid=(4, s // ts),
        in_specs=[pl.BlockSpec((ts, LANES), lambda j, i: (i, j)),
                  pl.BlockSpec((ts, LANES), lambda j, i: (i, 4 + j)),
                  pl.BlockSpec((1, LANES), lambda j, i: (0, j)),
                  pl.BlockSpec((2, CHUNK, CHUNK), lambda j, i: (j, 0, 0)),
                  pl.BlockSpec((CHUNK, LANES), lambda j, i: (0, j)),
                  pl.BlockSpec((LANES, LANES), lambda j, i: (0, 0))],
        out_specs=pl.BlockSpec((ts, LANES), lambda j, i: (i, j)),
        out_shape=jax.ShapeDtypeStruct((s, 4 * LANES), BF16),
    )(p1, p1, gn, w, bias, seg)


def _sgu_bwd(p1, dy, gn, w, wt, bias, seg, tril, *, ts=512, name):
    s = p1.shape[0]
    ts = _tile_rows(ts, s)
    nt = s // ts

    def body(zu_ref, zg_ref, dy_ref, gn_ref, w_ref, wt_ref, bias_ref, seg_ref, tril_ref,
             dzu_ref, dzg_ref, dw_ref, dbias_ref, dgn_ref):
        i = pl.program_id(1)
        zu = zu_ref[...]
        zg = zg_ref[...]
        gn_v = gn_ref[...]
        segv = seg_ref[...]
        u, tu, g, tg, rs, ghat, gvb, mixed = _sgu_forward_block(zu, zg, gn_v, w_ref, bias_ref[...], segv)
        dyv = dy_ref[...]
        du = dyv * mixed
        dmx = dyv * u

        @pl.when(i == 0)
        def _():
            dw_ref[...] = jnp.zeros_like(dw_ref)
            dbias_ref[...] = jnp.zeros_like(dbias_ref)
            dgn_ref[...] = jnp.zeros_like(dgn_ref)

        lane = _lanes((CHUNK, LANES))
        dgv_chunks = []
        dbias = jnp.zeros((CHUNK, LANES), F32)
        for c in range(ts // CHUNK):
            dmc = dmx[c * CHUNK:(c + 1) * CHUNK]
            gc = gvb[c * CHUNK:(c + 1) * CHUNK]
            dm_a = jnp.where(lane < 64, dmc, 0.0).astype(BF16)
            dm_b = jnp.where(lane >= 64, dmc, 0.0).astype(BF16)
            dw_ref[0] += _dot_nt(dm_a, gc)
            dw_ref[1] += _dot_nt(dm_b, gc)
            dgv_chunks.append(_dot(wt_ref[0], dm_a) + _dot(wt_ref[1], dm_b))
            dbias = dbias + dmc
        dbias_ref[...] += dbias
        dgv = dgv_chunks[0] if len(dgv_chunks) == 1 else jnp.concatenate(dgv_chunks, axis=0)
        dgn_ref[...] += jnp.sum(dgv * ghat, axis=0, keepdims=True)
        dgh = dgv * gn_v
        dg = rs * (dgh - ghat * _dot_split(dgh * ghat, segv))
        dzu_ref[...] = (du * _gelu_grad(zu, tu)).astype(BF16)
        dzg_ref[...] = (dg * _gelu_grad(zg, tg)).astype(BF16)

        @pl.when(i == nt - 1)
        def _():
            dw_ref[0] = dw_ref[0] * tril_ref[...]
            dw_ref[1] = dw_ref[1] * tril_ref[...]

    f = jax.ShapeDtypeStruct
    colj = pl.BlockSpec((ts, LANES), lambda j, i: (i, j))
    wsp = pl.BlockSpec((2, CHUNK, CHUNK), lambda j, i: (j, 0, 0))
    sq = pl.BlockSpec((LANES, LANES), lambda j, i: (0, 0))
    return pl.pallas_call(
        body, name=name, grid=(4, nt),
        in_specs=[colj, pl.BlockSpec((ts, LANES), lambda j, i: (i, 4 + j)), colj,
                  pl.BlockSpec((1, LANES), lambda j, i: (0, j)), wsp, wsp,
                  pl.BlockSpec((CHUNK, LANES), lambda j, i: (0, j)), sq, sq],
        out_specs=(colj, colj, wsp, pl.BlockSpec((CHUNK, LANES), lambda j, i: (0, j)),
                   pl.BlockSpec((1, LANES), lambda j, i: (0, j))),
        out_shape=(f((s, 4 * LANES), BF16), f((s, 4 * LANES), BF16), f((8, CHUNK, CHUNK), F32),
                   f((CHUNK, 4 * LANES), F32), f((1, 4 * LANES), F32)),
    )(p1, p1, dy, gn, w, wt, bias, seg, tril)


F_COL = 20


def _fcum_fwd(p1, bf, *, ts=512, name):
    s = p1.shape[0]
    ts = _tile_rows(ts, s)

    def body(f_ref, bf_ref, c_ref, car_ref):
        i = pl.program_id(0)
        z = f_ref[...] + bf_ref[...]
        logf = jnp.minimum(z, 0.0) - _log1p_pos(jnp.exp(-jnp.abs(z)))

        @pl.when(i == 0)
        def _():
            car_ref[...] = jnp.zeros_like(car_ref)

        c_ref[...] = _cumsum_fwd(logf) + car_ref[0:1, :]
        car_ref[0:1, :] = c_ref[ts - 1:ts, :]

    return pl.pallas_call(
        body, name=name, grid=(s // ts,),
        in_specs=[pl.BlockSpec((ts, LANES), lambda i: (i, F_COL)), pl.BlockSpec((1, LANES), lambda i: (0, 0))],
        out_specs=pl.BlockSpec((ts, LANES), lambda i: (i, 0)),
        out_shape=jax.ShapeDtypeStruct((s, LANES), F32),
        scratch_shapes=[pltpu.VMEM((SUBLANES, LANES), F32)],
    )(p1, bf)


def _fcum_bwd(dcs, dcq, p1, bf, *, ts=512, name):
    s = p1.shape[0]
    ts = _tile_rows(ts, s)
    nt = s // ts

    def body(dc_ref, dcq_ref, f_ref, bf_ref, df_ref, dbf_ref, car_ref):
        i = pl.program_id(0)

        @pl.when(i == 0)
        def _():
            car_ref[...] = jnp.zeros_like(car_ref)
            dbf_ref[...] = jnp.zeros_like(dbf_ref)

        dc = dc_ref[...]
        lane = _lanes((ts, LANES))
        for h in range(8):
            dc = dc + jnp.where(lane == h, dcq_ref[:, h * LANES:(h + 1) * LANES], 0.0)
        dlog = _cumsum_rev(dc) + car_ref[0:1, :]
        car_ref[...] = dlog[:SUBLANES]
        z = f_ref[...] + bf_ref[...]
        df = dlog * _sigmoid(-z)
        df_ref[...] = df.astype(BF16)
        dbf_ref[...] += jnp.sum(df, axis=0, keepdims=True)

    return pl.pallas_call(
        body, name=name, grid=(nt,),
        in_specs=[pl.BlockSpec((ts, LANES), lambda i: (nt - 1 - i, 0)),
                  pl.BlockSpec((ts, 8 * LANES), lambda i: (nt - 1 - i, 0)),
                  pl.BlockSpec((ts, LANES), lambda i: (nt - 1 - i, F_COL)),
                  pl.BlockSpec((1, LANES), lambda i: (0, 0))],
        out_specs=(pl.BlockSpec((ts, LANES), lambda i: (nt - 1 - i, 0)), pl.BlockSpec((1, LANES), lambda i: (0, 0))),
        out_shape=(jax.ShapeDtypeStruct((s, LANES), BF16), jax.ShapeDtypeStruct((1, LANES), F32)),
        scratch_shapes=[pltpu.VMEM((SUBLANES, LANES), F32)],
    )(dcs, dcq, p1, bf)


def _fox_scores(qm, kb, bias, ck, diagonal):
    sc = _dot_nt(qm, kb) + bias - ck
    if diagonal:
        sc = jnp.where(_lanes(sc.shape) <= _rows(sc.shape), sc, NEG)
    return sc


def _head_masks(shape):
    lane = _lanes(shape)
    return lane < 64, lane >= 64


def _fox_fwd(p1, cq, ck, *, tq=512, name):
    s = p1.shape[0]
    tq = _tile_rows(tq, s)
    tk = tq
    nq = s // tq

    def body(q_ref, k_ref, v_ref, cq_ref, ck_ref, o_ref, lb_ref):
        qi = pl.program_id(1)
        q = q_ref[...] * 0.125
        first, second = _head_masks((tq, LANES))
        qms = [jnp.where(sel, q, 0.0).astype(BF16) for sel in (first, second)]
        cqs = [cq_ref[:, hh * LANES:(hh + 1) * LANES] for hh in range(2)]
        biases = [jnp.tile(cqh, (1, tk // LANES)) for cqh in cqs]

        def step(kj, carry, diagonal):
            cols = pl.ds(pl.multiple_of(kj * tk, tk), tk)
            kb = k_ref[cols, :].astype(BF16)
            vb = v_ref[cols, :].astype(BF16)
            new, outs = [], []
            acc = carry[4]
            for hh in range(2):
                m_prev, l_prev = carry[2 * hh], carry[2 * hh + 1]
                sc = _fox_scores(qms[hh], kb, biases[hh], ck_ref[hh, :, cols], diagonal)
                m_new = jnp.maximum(m_prev, jnp.max(sc, axis=1, keepdims=True))
                pm = jnp.exp(sc - jnp.tile(m_new, (1, tk // LANES)))
                alpha = jnp.exp(m_prev - m_new)
                new += [m_new, alpha * l_prev + jnp.sum(pm, axis=1, keepdims=True)]
                outs.append(acc * alpha + _dot(pm.astype(BF16), vb))
            return tuple(new) + (jnp.where(first, outs[0], outs[1]),)

        zero = jnp.zeros((tq, LANES), F32)
        low = jnp.full((tq, LANES), NEG, F32)
        carry = lax.fori_loop(0, qi, lambda kj, c: step(kj, c, False), (low, zero, low, zero, zero))
        m0, l0, m1, l1, acc = step(qi, carry, True)
        o_ref[...] = (acc / jnp.where(first, l0, l1)).astype(BF16)
        lb_ref[:, 0:LANES] = cqs[0] - (m0 + jnp.log(l0))
        lb_ref[:, LANES:2 * LANES] = cqs[1] - (m1 + jnp.log(l1))

    return pl.pallas_call(
        body, name=name, grid=(4, nq),
        in_specs=[pl.BlockSpec((tq, LANES), lambda j, qi: (qi, 8 + j)),
                  pl.BlockSpec((s, LANES), lambda j, qi: (0, 12 + j)),
                  pl.BlockSpec((s, LANES), lambda j, qi: (0, 16 + j)),
                  pl.BlockSpec((tq, 2 * LANES), lambda j, qi: (qi, j)),
                  pl.BlockSpec((2, 1, s), lambda j, qi: (j, 0, 0))],
        out_specs=(pl.BlockSpec((tq, LANES), lambda j, qi: (qi, j)),
                   pl.BlockSpec((tq, 2 * LANES), lambda j, qi: (qi, j))),
        out_shape=(jax.ShapeDtypeStruct((s, 4 * LANES), BF16), jax.ShapeDtypeStruct((s, 8 * LANES), F32)),
    )(p1, p1, p1, cq, ck)


def _fox_delta(dy, o, sel, *, ts=512, name):
    s = o.shape[0]
    ts = _tile_rows(ts, s)

    def body(do_ref, o_ref, sel_ref, d_ref):
        prod = do_ref[...] * o_ref[...].astype(F32)
        d_ref[:, 0:LANES] = _dot_split(prod, sel_ref[0])
        d_ref[:, LANES:2 * LANES] = _dot_split(prod, sel_ref[1])

    return pl.pallas_call(
        body, name=name, grid=(4, s // ts),
        in_specs=[pl.BlockSpec((ts, LANES), lambda j, i: (i, 4 + j)),
                  pl.BlockSpec((ts, LANES), lambda j, i: (i, j)),
                  pl.BlockSpec((2, LANES, LANES), lambda j, i: (0, 0, 0))],
        out_specs=pl.BlockSpec((ts, 2 * LANES), lambda j, i: (i, j)),
        out_shape=jax.ShapeDtypeStruct((s, 8 * LANES), F32),
    )(dy, o, sel)


def _fox_bwd(p1, dy, lb, delta, ck, *, tq=512, name):
    s = p1.shape[0]
    tq = _tile_rows(tq, s)
    tk = tq
    nq = s // tq

    def body(q_ref, k_ref, v_ref, do_ref, lb_ref, dl_ref, ck_ref,
             dq_ref, dk_ref, dv_ref, dck_ref, dcq_ref, dqa_ref, dra_ref):
        kj = pl.program_id(1)

        @pl.when(kj == 0)
        def _():
            dqa_ref[...] = jnp.zeros_like(dqa_ref)
            dra_ref[...] = jnp.zeros_like(dra_ref)

        kf = k_ref[...]
        kb = kf.astype(BF16)
        vb = v_ref[...].astype(BF16)
        first, second = _head_masks((tk, LANES))
        kms = [jnp.where(sel, kf, 0.0).astype(BF16) for sel in (first, second)]
        cks = [ck_ref[hh] for hh in range(2)]

        def step(qi, carry, diagonal):
            dk_acc, dv_acc, dc0, dc1 = carry
            dcs = [dc0, dc1]
            rows = pl.ds(pl.multiple_of(qi * tq, tq), tq)
            q = q_ref[rows, :] * 0.125
            do = do_ref[rows, :]
            for hh, sel in enumerate((first, second)):
                qm = jnp.where(sel, q, 0.0).astype(BF16)
                dom = jnp.where(sel, do, 0.0).astype(BF16)
                bias = jnp.tile(lb_ref[rows, hh * LANES:(hh + 1) * LANES], (1, tk // LANES))
                pm = jnp.exp(_fox_scores(qm, kb, bias, cks[hh], diagonal))
                dv_acc = dv_acc + _dot_tn(pm.astype(BF16), dom)
                dp = _dot_nt(dom, vb)
                ds = pm * (dp - jnp.tile(dl_ref[rows, hh * LANES:(hh + 1) * LANES], (1, tk // LANES)))
                dsb = ds.astype(BF16)
                dk_acc = dk_acc + _dot_tn(dsb, qm)
                dcs[hh] = dcs[hh] - jnp.sum(ds, axis=0, keepdims=True)
                dqa_ref[rows, :] += _dot(dsb, kms[hh])
                dra_ref[hh, rows, :] += jnp.sum(ds, axis=1, keepdims=True)
            return dk_acc, dv_acc, dcs[0], dcs[1]

        zero = jnp.zeros((tk, LANES), F32)
        zrow = jnp.zeros((1, tk), F32)
        carry = step(kj, (zero, zero, zrow, zrow), True)
        dk_acc, dv_acc, dc0, dc1 = lax.fori_loop(kj + 1, nq, lambda qi, c: step(qi, c, False), carry)
        dk_ref[...] = dk_acc.astype(BF16)
        dv_ref[...] = dv_acc.astype(BF16)
        dck_ref[0] = dc0
        dck_ref[1] = dc1

        @pl.when(kj == nq - 1)
        def _():
            dq_ref[...] = (dqa_ref[...] * 0.125).astype(BF16)
            dcq_ref[:, 0:LANES] = dra_ref[0]
            dcq_ref[:, LANES:2 * LANES] = dra_ref[1]

    def full(width, col0):
        return pl.BlockSpec((s, width), lambda j, kj: (0, col0 + j))

    kblk = pl.BlockSpec((tk, LANES), lambda j, kj: (kj, j))
    f = jax.ShapeDtypeStruct
    return pl.pallas_call(
        body, name=name, grid=(4, nq),
        in_specs=[full(LANES, 8),
                  pl.BlockSpec((tk, LANES), lambda j, kj: (kj, 12 + j)),
                  pl.BlockSpec((tk, LANES), lambda j, kj: (kj, 16 + j)),
                  full(LANES, 4), full(2 * LANES, 0), full(2 * LANES, 0),
                  pl.BlockSpec((2, 1, tk), lambda j, kj: (j, 0, kj))],
        out_specs=(full(LANES, 0), kblk, kblk, pl.BlockSpec((2, 1, tk), lambda j, kj: (j, 0, kj)),
                   full(2 * LANES, 0)),
        out_shape=(f((s, 4 * LANES), BF16), f((s, 4 * LANES), BF16), f((s, 4 * LANES), BF16),
                   f((8, 1, s), F32), f((s, 8 * LANES), F32)),
        scratch_shapes=[pltpu.VMEM((s, LANES), F32), pltpu.VMEM((2, s, LANES), F32)],
    )(p1, p1, p1, dy, lb, delta, ck)


def _row_block(r, cap=256):
    best = None
    for rb in range(2 * SUBLANES, min(r, cap) + 1, 2 * SUBLANES):
        if r % rb == 0:
            best = rb
    return r if best is None else best


def _adamw_many(ws, gs, ms, vs, *, name):
    shapes = [a.shape for a in ws]

    def flat(a):
        return a.reshape((-1, a.shape[-1]))

    n = len(ws)
    operands = [flat(a) for group in (ws, gs, ms, vs) for a in group]

    def body(*refs):
        w_refs, g_refs, m_refs, v_refs = (refs[i * n:(i + 1) * n] for i in range(4))
        d_refs, nm_refs, nv_refs = (refs[(4 + i) * n:(5 + i) * n] for i in range(3))
        for p in range(n):
            gv = g_refs[p][...]
            mn = ADAM_B1 * m_refs[p][...] + (1.0 - ADAM_B1) * gv
            vn = ADAM_B2 * v_refs[p][...] + (1.0 - ADAM_B2) * (gv * gv)
            m_hat = mn / ADAM_C1
            v_hat = vn / ADAM_C2
            d_refs[p][...] = (-ADAM_LR) * (m_hat / (jnp.sqrt(v_hat) + ADAM_EPS) + ADAM_WD * w_refs[p][...])
            nm_refs[p][...] = mn
            nv_refs[p][...] = vn

    vm = pl.BlockSpec(memory_space=pltpu.VMEM)
    out_shape = [jax.ShapeDtypeStruct(flat(a).shape, F32) for a in ws] * 3
    outs = pl.pallas_call(
        body, name=name, in_specs=[vm] * (4 * n), out_specs=[vm] * (3 * n), out_shape=out_shape,
    )(*operands)
    return [tuple(outs[i * n + p].reshape(shapes[p]) for i in range(3)) for p in range(n)]


def _adamw_halves(w, mine, theirs, m, v, core, *, name):
    layers, r, c = w.shape
    rh = r // 2
    rb = _row_block(rh)
    per = rh // rb

    def body(core_ref, w_ref, *refs):
        g_refs = refs[:2 * layers]
        m_ref, v_ref, g_ref, d_ref, nm_ref, nv_ref = refs[2 * layers:]
        own = pl.program_id(1) == core_ref[0]
        gv = jnp.where(own, g_refs[0][...], g_refs[layers][...])
        for l in range(1, layers):
            gv = jnp.where(pl.program_id(0) == l, jnp.where(own, g_refs[l][...], g_refs[layers + l][...]), gv)
        g_ref[...] = gv
        mn = ADAM_B1 * m_ref[...] + (1.0 - ADAM_B1) * gv
        vn = ADAM_B2 * v_ref[...] + (1.0 - ADAM_B2) * (gv * gv)
        m_hat = mn / ADAM_C1
        v_hat = vn / ADAM_C2
        d_ref[...] = (-ADAM_LR) * (m_hat / (jnp.sqrt(v_hat) + ADAM_EPS) + ADAM_WD * w_ref[...])
        nm_ref[...] = mn
        nv_ref[...] = vn

    full = pl.BlockSpec((None, rb, c), lambda l, h, i, core_ref: (l, h * per + i, 0))
    half = pl.BlockSpec((rb, c), lambda l, h, i, core_ref: (i, 0))
    shp = jax.ShapeDtypeStruct((layers, r, c), F32)
    return pl.pallas_call(
        body, name=name,
        grid_spec=pltpu.PrefetchScalarGridSpec(
            num_scalar_prefetch=1, grid=(layers, 2, per),
            in_specs=[full] + [half] * (2 * layers) + [full, full], out_specs=(full,) * 4),
        out_shape=(shp,) * 4,
    )(core, w, *mine, *theirs, m, v)


def _pair_specs(col, rb, c):
    if col:
        g_spec = pl.BlockSpec((None, rb, c), lambda t, i, sel: (sel[0], i, sel[1 + t]))
    else:
        g_spec = pl.BlockSpec((None, None, rb, c), lambda t, i, sel: (sel[1 + t], sel[0], i, 0))
    return g_spec, pl.BlockSpec((None, rb, c), lambda t, i, sel: (sel[1 + t], i, 0))


def _pair_sum(g, col, ra, sel, after, *, name):
    _, rh, c = ra.shape
    rb = _row_block(rh)

    def body(sel_ref, g_ref, ra_ref, after_ref, h16_ref):
        h16_ref[...] = (g_ref[...] + ra_ref[...]).astype(BF16)

    g_spec, ra_spec = _pair_specs(col, rb, c)
    return pl.pallas_call(
        body, name=name,
        grid_spec=pltpu.PrefetchScalarGridSpec(
            num_scalar_prefetch=1, grid=(2, rh // rb), in_specs=[g_spec, ra_spec, ANY],
            out_specs=pl.BlockSpec((None, rb, c), lambda t, i, sel: (t, i, 0))),
        out_shape=jax.ShapeDtypeStruct((2, rh, c), BF16),
    )(sel, g, ra, after)


def _first_sum(g, col, ra, r1, sel, after, *, name):
    _, rh, c = ra.shape
    rb = _row_block(rh)

    def body(sel_ref, g_ref, ra_ref, r_ref, after_ref, s_ref, s16_ref):
        tot = (g_ref[...] + ra_ref[...]) + r_ref[...].astype(F32)
        s_ref[...] = tot
        s16_ref[...] = tot.astype(BF16)

    g_spec, ra_spec = _pair_specs(col, rb, c)
    slot = pl.BlockSpec((None, rb, c), lambda t, i, sel_ref: (t, i, 0))
    return pl.pallas_call(
        body, name=name,
        grid_spec=pltpu.PrefetchScalarGridSpec(
            num_scalar_prefetch=1, grid=(2, rh // rb), in_specs=[g_spec, ra_spec, slot, ANY],
            out_specs=(slot, slot)),
        out_shape=(jax.ShapeDtypeStruct((2, rh, c), F32), jax.ShapeDtypeStruct((2, rh, c), BF16)),
    )(sel, g, ra, r1, after)


def _second_sum(s1, r2, mine, after, *, name):
    _, rh, c = s1.shape
    rb = _row_block(rh)

    def body(mine_ref, s_ref, r_ref, after_ref, t_ref):
        t_ref[...] = s_ref[...] + r_ref[...].astype(F32)

    flat = pl.BlockSpec((rb, c), lambda i, mine_ref: (i, 0))
    return pl.pallas_call(
        body, name=name,
        grid_spec=pltpu.PrefetchScalarGridSpec(
            num_scalar_prefetch=1, grid=(rh // rb,),
            in_specs=[pl.BlockSpec((None, rb, c), lambda i, mine_ref: (mine_ref[0], i, 0)), flat, ANY],
            out_specs=flat),
        out_shape=jax.ShapeDtypeStruct((rh, c), F32),
    )(mine, s1, r2, after)


def _place(shard, col, chip, dtype, *, name):
    r, c = shard.shape
    rh = r // 2
    rb = _row_block(rh)

    def body(chip_ref, s_ref, o_ref):
        o_ref[...] = s_ref[...].astype(o_ref.dtype)

    if col:
        out_spec = pl.BlockSpec((None, rb, c), lambda h, i, chip_ref: (h, i, chip_ref[0]))
        shape = (2, rh, N_CHIPS * c)
    else:
        out_spec = pl.BlockSpec((None, None, rb, c), lambda h, i, chip_ref: (chip_ref[0], h, i, 0))
        shape = (N_CHIPS, 2, rh, c)
    per = rh // rb
    return pl.pallas_call(
        body, name=name,
        grid_spec=pltpu.PrefetchScalarGridSpec(
            num_scalar_prefetch=1, grid=(2, per),
            in_specs=[pl.BlockSpec((rb, c), lambda h, i, chip_ref: (h * per + i, 0))], out_specs=out_spec),
        out_shape=jax.ShapeDtypeStruct(shape, dtype),
    )(chip, shard)


ANY = pl.BlockSpec(memory_space=pl.ANY)


def _mesh_pos():
    return lax.axis_index("x"), lax.axis_index("y"), lax.axis_index("c")


def _other_chips(x, y):
    return [(1 - x, y), (x, 1 - y), (1 - x, 1 - y)]


def _remote(src, dst, ssem, rsem, dev):
    return pltpu.make_async_remote_copy(src_ref=src, dst_ref=dst, send_sem=ssem, recv_sem=rsem,
                                        device_id=dev, device_id_type=MESH)


def _flip(a, b):
    return a + b - 2 * a * b


def _handshake(peers):
    barrier = pltpu.get_barrier_semaphore()
    for peer in peers:
        pl.semaphore_signal(barrier, inc=1, device_id=peer, device_id_type=MESH)
    pl.semaphore_wait(barrier, len(peers))


def _slab(ref, col, width, k, h):
    if not col:
        return ref.at[k, h]
    start = k * width if isinstance(k, int) else pl.multiple_of(k * width, LANES)
    return ref.at[h, :, pl.ds(start, width)]


def _all_gather(bufs, cols, *, collective_id, name):
    n = len(bufs)
    widths = [b.shape[2] // N_CHIPS if col else b.shape[3] for b, col in zip(bufs, cols)]
    outs = [jax.new_ref(b, memory_space=pltpu.MemorySpace.HBM) for b in bufs]

    def body(ssem, rsem):
        x, y, c = _mesh_pos()
        me = 2 * x + y
        sib = (x, y, 1 - c)
        n1 = (_flip(x, 1 - c), _flip(y, c))
        n2 = (_flip(x, c), _flip(y, 1 - c))
        k1 = 2 * n1[0] + n1[1]
        k2 = 2 * n2[0] + n2[1]
        kd = 2 * (1 - x) + (1 - y)
        _handshake([n1 + (c,), n2 + (c,), sib])

        def slab(a, k, h):
            return _slab(outs[a], cols[a], widths[a], k, h)

        def copy(a, j, src, dst, dev):
            return _remote(src, dst, ssem.at[a, j], rsem.at[a, j], dev)

        sends = []
        for a in range(n):
            for j, nb in ((0, n1), (1, n2)):
                own = slab(a, me, c)
                cp = copy(a, j, own, own, nb + (c,))
                cp.start()
                sends.append(cp)
        arrivals = ((0, k1, n1, 3), (1, k2, n2, 4), (2, kd, n2, 5))
        for j, k, nb, fwd in arrivals:
            for a in range(n):
                got = slab(a, k, c)
                copy(a, j, got, got, nb + (c,)).wait_recv()
                if j == 0:
                    cp = copy(a, 2, got, got, n2 + (c,))
                    cp.start()
                    sends.append(cp)
                cp = copy(a, fwd, got, got, sib)
                cp.start()
                sends.append(cp)
        for fwd, k in ((3, k2), (4, k1), (5, kd)):
            for a in range(n):
                got = slab(a, k, 1 - c)
                copy(a, fwd, got, got, sib).wait_recv()
        for cp in sends:
            cp.wait_send()

    _sequencer_call(body, (), [(n, 6), (n, 6)], collective_id, name)()
    return [ref[...] for ref in outs]


def _sequencer_call(body, out_types, sem_shapes, collective_id, name):
    return pl.kernel(
        body, name=name, out_type=out_types,
        mesh=plsc.ScalarSubcoreMesh(axis_name="sequencer", num_cores=1),
        scratch_types=[pltpu.SemaphoreType.DMA(shape) for shape in sem_shapes],
        compiler_params=pltpu.CompilerParams(collective_id=collective_id))


def _send_other_half(grads, cols, *, collective_id, name):
    n = len(grads)

    def shard_shape(g, col):
        if col:
            return (g.shape[1], g.shape[2] // N_CHIPS)
        return g.shape[2:]

    shapes = [shard_shape(g, col) for g, col in zip(grads, cols)]

    def body(*refs):
        ins, outs = refs[:n], refs[n:2 * n]
        ssem, rsem = refs[2 * n:]
        x, y, c = _mesh_pos()
        sib = (x, y, 1 - c)
        _handshake([sib])
        sends = []
        for a in range(n):
            for k in range(N_CHIPS):
                src = _slab(ins[a], cols[a], shapes[a][1], k, 1 - c)
                cp = _remote(src, outs[a].at[k], ssem.at[a, k], rsem.at[a, k], sib)
                cp.start()
                sends.append(cp)
        for cp in sends:
            cp.wait()

    out_types = [jax.ShapeDtypeStruct((N_CHIPS,) + shp, g.dtype) for g, shp in zip(grads, shapes)]
    return _sequencer_call(body, out_types, [(n, N_CHIPS), (n, N_CHIPS)], collective_id, name)(*grads)


def _send_first(sums, *, collective_id, name):
    n = len(sums)

    def body(*refs):
        ins, outs = refs[:n], refs[n:2 * n]
        ssem, rsem = refs[2 * n:]
        x, y, c = _mesh_pos()
        nb = (_flip(x, c), _flip(y, 1 - c), c)
        _handshake([nb])
        sends = []
        for a in range(n):
            for t in range(2):
                cp = _remote(ins[a].at[t], outs[a].at[t], ssem.at[a, t], rsem.at[a, t], nb)
                cp.start()
                sends.append(cp)
        for cp in sends:
            cp.wait()

    out_types = [jax.ShapeDtypeStruct(h.shape, h.dtype) for h in sums]
    return _sequencer_call(body, out_types, [(n, 2), (n, 2)], collective_id, name)(*sums)


def _send_second(sums, *, collective_id, name):
    n = len(sums)

    def body(*refs):
        ins, outs = refs[:n], refs[n:2 * n]
        ssem, rsem = refs[2 * n:]
        x, y, c = _mesh_pos()
        nb = (_flip(x, 1 - c), _flip(y, c), c)
        other = 1 - (c * y + (1 - c) * x)
        _handshake([nb])
        sends = []
        for a in range(n):
            cp = _remote(ins[a].at[other], outs[a], ssem.at[a], rsem.at[a], nb)
            cp.start()
            sends.append(cp)
        for cp in sends:
            cp.wait()

    out_types = [jax.ShapeDtypeStruct(s.shape[1:], s.dtype) for s in sums]
    return _sequencer_call(body, out_types, [(n,), (n,)], collective_id, name)(*sums)


def _swap_halves(halves, *, collective_id, name):
    n = len(halves)

    def body(*refs):
        ins, outs = refs[:n], refs[n:2 * n]
        ssem, rsem = refs[2 * n:]
        x, y, c = _mesh_pos()
        sib = (x, y, 1 - c)
        _handshake([sib])
        cps = []
        for a in range(n):
            cp = _remote(ins[a], outs[a], ssem.at[a], rsem.at[a], sib)
            cp.start()
            cps.append(cp)
        for cp in cps:
            cp.wait()

    out_types = [jax.ShapeDtypeStruct(h.shape, h.dtype) for h in halves]
    return _sequencer_call(body, out_types, [(n,), (n,)], collective_id, name)(*halves)


def _all_reduce_small(buf, *, name):
    r = buf.shape[0]
    rh = r // 2

    def body(in_ref, out_ref, x1_ref, x2_ref, ssem, rsem):
        x, y, c = _mesh_pos()
        me = 2 * x + y
        sib = (x, y, 1 - c)
        chips = _other_chips(x, y)
        cp = _remote(in_ref, x1_ref, ssem.at[0], rsem.at[0], sib)
        cp.start()
        cp.wait()
        off = pl.multiple_of(c * rh, SUBLANES)
        x2_ref[me] = in_ref[pl.ds(off, rh), :] + x1_ref[pl.ds(off, rh), :]
        sends = []
        for j, (cx, cy) in enumerate(chips):
            s = _remote(x2_ref.at[me], x2_ref.at[me], ssem.at[1 + j], rsem.at[1 + j], (cx, cy, c))
            s.start()
            sends.append(s)
        for j, (cx, cy) in enumerate(chips):
            slot = x2_ref.at[2 * cx + cy]
            _remote(slot, slot, ssem.at[1 + j], rsem.at[1 + j], (cx, cy, c)).wait_recv()
        out_ref[pl.ds(off, rh), :] = ((x2_ref[0] + x2_ref[1]) + x2_ref[2]) + x2_ref[3]
        for s in sends:
            s.wait_send()
        mine = out_ref.at[pl.ds(off, rh), :]
        s3 = _remote(mine, mine, ssem.at[4], rsem.at[4], sib)
        s3.start()
        off2 = pl.multiple_of((1 - c) * rh, SUBLANES)
        theirs = out_ref.at[pl.ds(off2, rh), :]
        _remote(theirs, theirs, ssem.at[4], rsem.at[4], sib).wait_recv()
        s3.wait_send()

    vm = pl.BlockSpec(memory_space=pltpu.VMEM)
    return pl.pallas_call(
        body, name=name, in_specs=[vm], out_specs=vm,
        out_shape=jax.ShapeDtypeStruct((r, LANES), F32),
        scratch_shapes=[pltpu.VMEM((r, LANES), F32), pltpu.VMEM((N_CHIPS, rh, LANES), F32),
                        pltpu.SemaphoreType.DMA((5,)), pltpu.SemaphoreType.DMA((5,))],
    )(buf)


PACK_ALIGN = 2 * SUBLANES * LANES


def _pack(arrays):
    parts, offs, off = [], [], 0
    for a in arrays:
        flat = a.reshape(-1).astype(F32)
        padded = -(-flat.shape[0] // PACK_ALIGN) * PACK_ALIGN
        parts.append(jnp.pad(flat, (0, padded - flat.shape[0])))
        offs.append(off)
        off += padded
    buf = jnp.concatenate(parts).reshape(-1, LANES)
    return buf, offs


def _unpack(buf, offs, shapes):
    flat = buf.reshape(-1)
    out = []
    for off, shp in zip(offs, shapes):
        size = 1
        for d in shp:
            size *= d
        out.append(flat[off:off + size].reshape(shp))
    return out


def _cols_from_shards(g4):
    _, k, ns = g4.shape
    return jnp.transpose(g4, (1, 0, 2)).reshape(k, N_CHIPS * ns)


def _cols_to_shards(w):
    k, n = w.shape
    return jnp.transpose(w.reshape(k, N_CHIPS, n // N_CHIPS), (1, 0, 2))


def _pair_blockdiag(w8):
    w = w8.reshape(4, 2, 64, 64)
    z = jnp.zeros((4, 64, 64), w8.dtype)
    top = jnp.concatenate([w[:, 0], z], axis=2)
    bot = jnp.concatenate([z, w[:, 1]], axis=2)
    return jnp.concatenate([top, bot], axis=1)


def _pair_diag_blocks(w4):
    a = w4[:, :64, :64]
    b = w4[:, 64:, 64:]
    return jnp.stack([a, b], axis=1).reshape(8, 64, 64)


def _local_step(x, target, wts, on_event=None):
    s = x.shape[0]
    g = {}

    def event(name, token):
        if on_event is not None:
            on_event(name, g, token)

    win0 = wts["w_in0"]
    wout0 = wts["w_out0"]
    win1 = wts["w_in1"]
    wout1 = wts["w_out1"]
    wup = wts["w_up"]
    wdown = wts["w_down"]
    w4, b4, w3, b3 = wts["w4"], wts["b4"], wts["w3"], wts["b3"]
    wa, wx = wts["wa"], wts["wx"]
    wat, wxt = jnp.swapaxes(wa, 1, 2), jnp.swapaxes(wx, 1, 2)
    ba, bx, lam = wts["ba"], wts["bx"], wts["lam"]
    fcw, fcb = wts["ffn_cw"], wts["ffn_cb"]
    sgu_w, sgu_wt = wts["sgu_w"], wts["sgu_wt"]
    sgu_bias, sgu_gn = wts["sgu_bias"], wts["sgu_gn"]
    bf = wts["bf"]

    lane = jnp.arange(LANES)
    seg = jnp.where((lane[:, None] // 64) == (lane[None, :] // 64), 1.0 / 64.0, 0.0).astype(BF16)
    sel = jnp.stack([jnp.broadcast_to((lane[:, None] < 64), (LANES, LANES)),
                     jnp.broadcast_to((lane[:, None] >= 64), (LANES, LANES))]).astype(BF16)
    tril = (lane[:, None] >= lane[None, :]).astype(F32)

    n0 = _norm_fwd(x, wts["g_mix0"], name="norm_mix0")
    p0 = _mm([n0], win0, nb=1280, name="mm_in0")
    ya, yb, hl = _even_core_fwd(p0, w4, b4, wa, ba, wx, bx, lam, w3, b3, name="even_fwd")
    h1, n1 = _mm([ya, yb], wout0, res=x, norm_out=wts["g_ffn"][0], name="mm_out0")

    def ffn_fwd(h, n, layer, next_gain):
        up = _mm([n], wup[layer], out_dtype=BF16, ts=1024, nb=1408, name=f"mm_up{layer}")
        act = _ffn_core_fwd(up, fcw[layer], fcb[layer], name=f"ffn_fwd{layer}")
        if next_gain is None:
            return up, act, _mm([act], wdown[layer], res=h, name=f"mm_down{layer}"), None
        hn, nn = _mm([act], wdown[layer], res=h, norm_out=next_gain, name=f"mm_down{layer}")
        return up, act, hn, nn

    up0, act0, h2, n2 = ffn_fwd(h1, n1, 0, wts["g_mix1"])

    p1 = _mm([n2], win1, name="mm_in1")
    yc = _sgu_fwd(p1, sgu_gn, sgu_w, sgu_bias, seg, name="sgu_fwd")
    cum = _fcum_fwd(p1, bf, name="fcum_fwd")
    c8 = cum[:, :8]
    cq = jnp.broadcast_to(c8[:, :, None], (s, 8, LANES)).reshape(s, 8 * LANES)
    ck = jnp.transpose(c8).reshape(8, 1, s)
    yd, lb = _fox_fwd(p1, cq, ck, name="fox_fwd")
    h3, n3 = _mm([yc, yd], wout1, res=h2, norm_out=wts["g_ffn"][1], name="mm_out1")

    up1, act1, h4, _ = ffn_fwd(h3, n3, 1, None)
    dh4, loss, g["final_norm"] = _final(h4, wts["g_final"], target, name="final")

    def ffn_bwd(dh, h, n, up, act, layer):
        dact = _mm([dh], wdown[layer], trans_w=True, out_dtype=BF16, ts=1024, nb=1408, name=f"mm_dact{layer}")
        g[f"w_down{layer}"] = _mm_tn([act], [dh], ts=1024, nb=512, name=f"mm_dwdown{layer}")
        event(f"dwdown{layer}", g[f"w_down{layer}"])
        dgate, dval, dcwg, dcwv, dcbg, dcbv = _ffn_core_bwd(dact, up, fcw[layer], fcb[layer], name=f"ffn_bwd{layer}")
        event(f"ffn_bwd{layer}", dgate)
        g[f"w_up{layer}"] = _mm_tn([n], [dgate, dval], ts=1024, nb=1408, name=f"mm_dwup{layer}")
        event(f"dwup{layer}", g[f"w_up{layer}"])
        dhn, g[f"g_ffn{layer}"] = _mm([dgate, dval], wup[layer], trans_w=True, ts=512,
                                      norm_bwd=(h, wts["g_ffn"][layer], dh), name=f"mm_dn_ffn{layer}")
        g[f"ffn_cw{layer}"] = jnp.concatenate([dcwg, dcwv], axis=1)
        g[f"ffn_cb{layer}"] = jnp.concatenate([dcbg, dcbv], axis=1)
        return dhn

    dh3 = ffn_bwd(dh4, h3, n3, up1, act1, 1)

    dy1 = _mm([dh3], wout1, trans_w=True, ts=1024, name="mm_dy1")
    g["w_out1"] = _mm_tn([yc, yd], [dh3], ts=1024, nb=512, name="mm_dwout1")
    event("dwout1", g["w_out1"])
    dzu, dzg, g["sgu_w"], g["sgu_bias"], g["sgu_gn"] = _sgu_bwd(
        p1, dy1, sgu_gn, sgu_w, sgu_wt, sgu_bias, seg, tril, name="sgu_bwd")
    delta = _fox_delta(dy1, yd, sel, name="fox_delta")
    dq, dk, dv, dck, dcq = _fox_bwd(p1, dy1, lb, delta, ck, name="fox_bwd")
    event("fox_bwd", dq)
    dcs = jnp.pad(jnp.transpose(dck.reshape(8, s)), ((0, 0), (0, LANES - 8)))
    df, g["bf"] = _fcum_bwd(dcs, dcq, p1, bf, name="fcum_bwd")
    dp1 = jnp.concatenate([dzu, dzg, dq, dk, dv, df], axis=1)
    g["w_in1"] = _mm_tn([n2], [dp1], ts=1024, nb=896, name="mm_dwin1")
    event("dwin1", g["w_in1"])
    dh2, g["g_mix1"] = _mm([dp1], win1, trans_w=True, norm_bwd=(h2, wts["g_mix1"], dh3), name="mm_dn_mix1")

    dh1 = ffn_bwd(dh2, h1, n1, up0, act0, 0)

    dy0 = _mm([dh1], wout0, trans_w=True, ts=1024, name="mm_dy0")
    g["w_out0"] = _mm_tn([ya, yb], [dh1], ts=1024, nb=512, name="mm_dwout0")
    event("dwout0", g["w_out0"])
    (*dp0, g["w4"], g["b4"], g["wa"], g["ba"], g["wx"], g["bx"], g["lam"], g["w3"], g["b3"]) = _even_core_bwd(
        dy0, p0, hl, w4, b4, wa, wat, ba, wx, wxt, bx, lam, w3, b3, name="even_bwd")
    event("even_bwd", dp0[0])
    g["w_in0"] = _mm_tn([n0], dp0, ts=1024, nb=512, name="mm_dwin0")
    event("dwin0", g["w_in0"])
    grad_x, g["g_mix0"] = _mm(dp0, win0, trans_w=True, norm_bwd=(x, wts["g_mix0"], dh1), name="mm_dn_mix0")
    return loss, grad_x, g


def _prepare_weights(nat):
    lane = jnp.arange(LANES)
    tril = (lane[:, None] >= lane[None, :]).astype(F32)
    sgu_tril = nat["sgu_w"][0] * tril
    w_in1 = nat["mix1_w_in"]
    return {
        "w_in0": nat["mix0_w_in"],
        "w_out0": nat["mix0_w_out"],
        "w_in1": jnp.pad(w_in1, ((0, 0), (0, 21 * LANES - w_in1.shape[1]))),
        "w_out1": nat["mix1_w_out"],
        "w_up": [nat["ffn_up"][l] for l in range(2)],
        "w_down": [nat["ffn_down"][l] for l in range(2)],
        "w4": nat["lru_conv_w"], "b4": nat["lru_conv_b"], "w3": nat["sconv_w"], "b3": nat["sconv_b"],
        "wa": _pair_blockdiag(nat["lru_wa"][0]).astype(BF16), "wx": _pair_blockdiag(nat["lru_wx"][0]).astype(BF16),
        "ba": nat["lru_ba"], "bx": nat["lru_bx"], "lam": nat["lru_lambda"],
        "ffn_cw": [nat["ffn_conv_w"][l] for l in range(2)],
        "ffn_cb": [nat["ffn_conv_b"][l:l + 1] for l in range(2)],
        "sgu_w": sgu_tril.astype(BF16), "sgu_wt": jnp.swapaxes(sgu_tril, 1, 2).astype(BF16),
        "sgu_bias": jnp.repeat(jnp.transpose(nat["sgu_b"][0]), 64, axis=1), "sgu_gn": nat["sgu_norm"],
        "bf": jnp.pad(nat["fox_bf"], ((0, 0), (0, LANES - 8))),
        "g_mix0": nat["mix0_norm"], "g_mix1": nat["mix1_norm"],
        "g_ffn": [nat["ffn_norm"][0:1], nat["ffn_norm"][1:2]], "g_final": nat["final_norm"].reshape(1, D_MODEL),
    }


def _natural_grads(g):
    small = {
        "mix0_norm": g["g_mix0"], "lru_conv_b": g["b4"],
        "lru_wa": _pair_diag_blocks(g["wa"])[None], "lru_ba": g["ba"],
        "lru_wx": _pair_diag_blocks(g["wx"])[None], "lru_bx": g["bx"],
        "lru_lambda": g["lam"], "sconv_b": g["b3"],
        "sgu_w": g["sgu_w"][None],
        "sgu_b": jnp.transpose(g["sgu_bias"].reshape(CHUNK, 8, 64).sum(axis=2))[None],
        "fox_bf": g["bf"][:, :8],
        "ffn_norm": jnp.concatenate([g["g_ffn0"], g["g_ffn1"]], axis=0),
        "ffn_conv_b": jnp.concatenate([g["ffn_cb0"], g["ffn_cb1"]], axis=0),
        "final_norm": g["final_norm"].reshape(D_MODEL),
        "lru_conv_w": g["w4"][None], "sconv_w": g["w3"][None],
        "ffn_conv_w": jnp.stack([g["ffn_cw0"], g["ffn_cw1"]]),
        "mix1_norm": g["g_mix1"], "sgu_norm": g["sgu_gn"],
    }
    big = {
        "mix0_w_in": g["w_in0"], "mix0_w_out": g["w_out0"],
        "mix1_w_in": g["w_in1"][:, :2568], "mix1_w_out": g["w_out1"],
        "ffn_up0": g["w_up0"], "ffn_up1": g["w_up1"],
        "ffn_down0": g["w_down0"], "ffn_down1": g["w_down1"],
    }
    return small, big


COL_SHARDED = ("mix0_w_in", "mix1_w_in", "ffn_up0", "ffn_up1")
COL_ALIGNED = ("mix0_w_in", "ffn_up0", "ffn_up1")
SMALL_SHARDED = ("lru_conv_w", "sconv_w", "ffn_conv_w", "mix1_norm", "sgu_norm")
SMALL_REPLICATED = ("mix0_norm", "lru_conv_b", "lru_wa", "lru_ba", "lru_wx", "lru_bx", "lru_lambda", "sconv_b",
                    "sgu_w", "sgu_b", "fox_bf", "ffn_norm", "ffn_conv_b", "final_norm")
WEIGHT_ORDER = ("mix0_norm", "mix0_w_in", "lru_conv_w", "lru_conv_b", "lru_wa", "lru_ba", "lru_wx", "lru_bx",
                "lru_lambda", "sconv_w", "sconv_b", "mix0_w_out", "mix1_norm", "mix1_w_in", "sgu_norm", "sgu_w",
                "sgu_b", "fox_bf", "mix1_w_out", "ffn_norm", "ffn_up", "ffn_conv_w", "ffn_conv_b", "ffn_down",
                "final_norm")


GATHER_GROUPS = (("mix0_w_in", "mix0_w_out"), ("ffn_up0",), ("ffn_down0", "mix1_w_in"),
                 ("mix1_w_out", "ffn_up1", "ffn_down1"))
CID_GATHER, CID_PAIR, CID_FIRST, CID_SECOND, CID_SWAP = 1, 2, 3, 4, 5


class _GradReducer:
    def __init__(self):
        x, y, c = _mesh_pos()
        self.send = jnp.stack([c] + [2 * (c * (1 - x) + (1 - c) * t) + (c * t + (1 - c) * (1 - y))
                                     for t in range(2)]).astype(jnp.int32)
        self.keep = jnp.stack([c] + [c * (2 * x + t) + (1 - c) * (2 * t + y) for t in range(2)]).astype(jnp.int32)
        self.mine = (c * y + (1 - c) * x).reshape(1).astype(jnp.int32)
        self.groups = {}

    @staticmethod
    def _view(name, a):
        if name in COL_ALIGNED:
            return a.reshape(2, a.shape[0] // 2, a.shape[1])
        if name in COL_SHARDED:
            a = _cols_to_shards(a)
            return a.reshape(N_CHIPS, 2, a.shape[1] // 2, a.shape[2])
        rows = a.shape[0] // (2 * N_CHIPS)
        return a.reshape(N_CHIPS, 2, rows, a.shape[1])

    def start(self, group, grads):
        names = tuple(grads)
        views = [self._view(k, grads[k]) for k in names]
        cols = [k in COL_ALIGNED for k in names]
        data = _send_other_half(views, cols, collective_id=CID_PAIR, name=f"rs_pair_{group}")
        self.groups[group] = dict(names=names, stage=0, views=views, cols=cols, data=data)

    def step(self, group, after):
        st = self.groups[group]
        names = st["names"]
        if st["stage"] == 0:
            sums = [_pair_sum(a, col, b, self.send, after, name=f"rs_pair_sum_{k}")
                    for k, a, col, b in zip(names, st["views"], st["cols"], st["data"])]
            st["from_sib"] = st["data"]
            st["data"] = _send_first(sums, collective_id=CID_FIRST, name=f"rs_first_{group}")
        elif st["stage"] == 1:
            sums = [_first_sum(a, col, b, r, self.keep, after, name=f"rs_first_sum_{k}")
                    for k, a, col, b, r in zip(names, st["views"], st["cols"], st["from_sib"], st["data"])]
            st["keep"] = [s32 for s32, _ in sums]
            st["data"] = _send_second([s16 for _, s16 in sums], collective_id=CID_SECOND, name=f"rs_second_{group}")
        else:
            st["mine"] = [_second_sum(s32, r, self.mine, after, name=f"rs_second_sum_{k}")
                          for k, s32, r in zip(names, st["keep"], st["data"])]
            st["data"] = _swap_halves(st["mine"], collective_id=CID_SWAP, name=f"rs_swap_{group}")
        st["stage"] += 1

    def result(self, group):
        st = self.groups[group]
        return {k: (a, b) for k, a, b in zip(st["names"], st["mine"], st["data"])}


def _train_step(x, target, w, m, v):
    x2 = x[0]
    t2 = target[0]
    chip = 2 * lax.axis_index("x") + lax.axis_index("y")
    core_arr = lax.axis_index("c").reshape(1).astype(jnp.int32)
    chip_arr = chip.reshape(1).astype(jnp.int32)

    big_shards = {
        "mix0_w_in": w["mix0_w_in"][0], "mix0_w_out": w["mix0_w_out"][0],
        "mix1_w_in": w["mix1_w_in"][0], "mix1_w_out": w["mix1_w_out"][0],
        "ffn_up0": w["ffn_up"][0], "ffn_up1": w["ffn_up"][1],
        "ffn_down0": w["ffn_down"][0], "ffn_down1": w["ffn_down"][1],
    }
    small_shards = [w[k] for k in SMALL_SHARDED]
    small_buf, small_offs = _pack(small_shards)
    full = {}
    small_all = None
    for gi, names in enumerate(GATHER_GROUPS):
        cols = [k in COL_ALIGNED for k in names]
        placed = [_place(big_shards[k], col, chip_arr, BF16, name=f"place_{k}") for k, col in zip(names, cols)]
        if gi == 0:
            placed.append(_place(small_buf, False, chip_arr, F32, name="place_small"))
            cols = cols + [False]
        gathered = _all_gather(placed, cols, collective_id=CID_GATHER, name=f"gather_weights{gi}")
        if gi == 0:
            small_all = gathered[-1].reshape(N_CHIPS, -1, LANES)
        for k, arr in zip(names, gathered):
            if k in COL_ALIGNED:
                full[k] = arr.reshape(arr.shape[0] * arr.shape[1], arr.shape[2])
            elif k in COL_SHARDED:
                full[k] = _cols_from_shards(arr.reshape((N_CHIPS, arr.shape[1] * arr.shape[2], arr.shape[3])))
            else:
                full[k] = arr.reshape(-1, arr.shape[3])
    per_chip = [_unpack(small_all[k], small_offs, [a.shape for a in small_shards]) for k in range(N_CHIPS)]
    lru_conv_w = jnp.concatenate([per_chip[k][0] for k in range(N_CHIPS)], axis=-1)[0]
    sconv_w = jnp.concatenate([per_chip[k][1] for k in range(N_CHIPS)], axis=-1)[0]
    ffn_conv_w = jnp.concatenate([per_chip[k][2] for k in range(N_CHIPS)], axis=-1)
    mix1_norm = jnp.concatenate([per_chip[k][3] for k in range(N_CHIPS)], axis=-1)
    sgu_norm = jnp.concatenate([per_chip[k][4] for k in range(N_CHIPS)], axis=-1)

    nat = {
        "mix0_w_in": full["mix0_w_in"], "mix0_w_out": full["mix0_w_out"],
        "mix1_w_in": full["mix1_w_in"], "mix1_w_out": full["mix1_w_out"],
        "ffn_up": [full["ffn_up0"], full["ffn_up1"]], "ffn_down": [full["ffn_down0"], full["ffn_down1"]],
        "lru_conv_w": lru_conv_w, "sconv_w": sconv_w, "ffn_conv_w": ffn_conv_w, "mix1_norm": mix1_norm,
        "sgu_norm": sgu_norm,
    }
    for k in SMALL_REPLICATED:
        nat[k] = w[k]
    wts = _prepare_weights(nat)

    reducer = _GradReducer()

    def on_event(name, g, token):
        if name == "dwup1":
            reducer.start("ffn1", {"ffn_up1": g["w_up1"], "ffn_down1": g["w_down1"]})
        elif name in ("dwout1", "fox_bwd"):
            reducer.step("ffn1", token)
        elif name == "dwin1":
            reducer.step("ffn1", token)
            reducer.start("mix1", {"mix1_w_in": g["w_in1"][:, :2568], "mix1_w_out": g["w_out1"]})
        elif name in ("dwdown0", "ffn_bwd0"):
            reducer.step("mix1", token)
        elif name == "dwup0":
            reducer.step("mix1", token)
            reducer.start("ffn0", {"ffn_up0": g["w_up0"], "ffn_down0": g["w_down0"]})
        elif name in ("dwout0", "even_bwd"):
            reducer.step("ffn0", token)
        elif name == "dwin0":
            reducer.step("ffn0", token)
            reducer.start("mix0", {"mix0_w_in": g["w_in0"], "mix0_w_out": g["w_out0"]})

    loss, grad_x, g = _local_step(x2, t2, wts, on_event)
    grads_small, _ = _natural_grads(g)

    small_names = SMALL_REPLICATED + SMALL_SHARDED
    small_list = [grads_small[k] for k in small_names] + [loss[:, :1]]
    sbuf, soffs = _pack(small_list)
    sred = _all_reduce_small(sbuf, name="reduce_small")
    small_red = _unpack(sred, soffs, [a.shape for a in small_list])
    loss_total = small_red[-1][0, 0]
    gsum = dict(zip(small_names, small_red[:-1]))
    for k in SMALL_SHARDED:
        width = w[k].shape[-1]
        gsum[k] = lax.dynamic_slice_in_dim(gsum[k], chip * width, width, axis=gsum[k].ndim - 1)

    out_g, out_d, out_m, out_v = {}, {}, {}, {}
    reduced = {}
    for group in ("ffn1", "mix1", "ffn0"):
        reduced.update(reducer.result(group))

    def update(pname, keys):
        mine = [reduced[k][0] for k in keys]
        theirs = [reduced[k][1] for k in keys]
        out_g[pname], out_d[pname], out_m[pname], out_v[pname] = _adamw_halves(
            w[pname], mine, theirs, m[pname], v[pname], core_arr, name=f"adamw_{pname}")
        return out_d[pname]

    reducer.step("mix0", update("ffn_up", ("ffn_up0", "ffn_up1")))
    small_new = _adamw_many([w[k] for k in small_names], [gsum[k] for k in small_names],
                            [m[k] for k in small_names], [v[k] for k in small_names], name="adamw_small")
    reducer.step("mix0", update("ffn_down", ("ffn_down0", "ffn_down1")))
    update("mix1_w_in", ("mix1_w_in",))
    reducer.step("mix0", update("mix1_w_out", ("mix1_w_out",)))
    reduced.update(reducer.result("mix0"))
    update("mix0_w_in", ("mix0_w_in",))
    update("mix0_w_out", ("mix0_w_out",))

    for k, (dd, mm, vv) in zip(small_names, small_new):
        out_g[k], out_d[k], out_m[k], out_v[k] = gsum[k].reshape(w[k].shape), dd, mm, vv

    outs = [loss_total, grad_x[None]]
    for d in (out_g, out_d, out_m, out_v):
        outs.extend(d[k] for k in WEIGHT_ORDER)
    return tuple(outs)


def kernel(x, mix0_norm, mix0_w_in, lru_conv_w, lru_conv_b, lru_wa, lru_ba, lru_wx, lru_bx, lru_lambda, sconv_w, sconv_b, mix0_w_out, mix1_norm, mix1_w_in, sgu_norm, sgu_w, sgu_b, fox_bf, mix1_w_out, ffn_norm, ffn_up, ffn_conv_w, ffn_conv_b, ffn_down, final_norm, loss_target, m_mix0_norm, m_mix0_w_in, m_lru_conv_w, m_lru_conv_b, m_lru_wa, m_lru_ba, m_lru_wx, m_lru_bx, m_lru_lambda, m_sconv_w, m_sconv_b, m_mix0_w_out, m_mix1_norm, m_mix1_w_in, m_sgu_norm, m_sgu_w, m_sgu_b, m_fox_bf, m_mix1_w_out, m_ffn_norm, m_ffn_up, m_ffn_conv_w, m_ffn_conv_b, m_ffn_down, m_final_norm, v_mix0_norm, v_mix0_w_in, v_lru_conv_w, v_lru_conv_b, v_lru_wa, v_lru_ba, v_lru_wx, v_lru_bx, v_lru_lambda, v_sconv_w, v_sconv_b, v_mix0_w_out, v_mix1_norm, v_mix1_w_in, v_sgu_norm, v_sgu_w, v_sgu_b, v_fox_bf, v_mix1_w_out, v_ffn_norm, v_ffn_up, v_ffn_conv_w, v_ffn_conv_b, v_ffn_down, v_final_norm):
    w = dict(zip(WEIGHT_ORDER, (mix0_norm, mix0_w_in, lru_conv_w, lru_conv_b, lru_wa, lru_ba, lru_wx, lru_bx, lru_lambda, sconv_w, sconv_b, mix0_w_out, mix1_norm, mix1_w_in, sgu_norm, sgu_w, sgu_b, fox_bf, mix1_w_out, ffn_norm, ffn_up, ffn_conv_w, ffn_conv_b, ffn_down, final_norm)))
    m = dict(zip(WEIGHT_ORDER, (m_mix0_norm, m_mix0_w_in, m_lru_conv_w, m_lru_conv_b, m_lru_wa, m_lru_ba, m_lru_wx, m_lru_bx, m_lru_lambda, m_sconv_w, m_sconv_b, m_mix0_w_out, m_mix1_norm, m_mix1_w_in, m_sgu_norm, m_sgu_w, m_sgu_b, m_fox_bf, m_mix1_w_out, m_ffn_norm, m_ffn_up, m_ffn_conv_w, m_ffn_conv_b, m_ffn_down, m_final_norm)))
    v = dict(zip(WEIGHT_ORDER, (v_mix0_norm, v_mix0_w_in, v_lru_conv_w, v_lru_conv_b, v_lru_wa, v_lru_ba, v_lru_wx, v_lru_bx, v_lru_lambda, v_sconv_w, v_sconv_b, v_mix0_w_out, v_mix1_norm, v_mix1_w_in, v_sgu_norm, v_sgu_w, v_sgu_b, v_fox_bf, v_mix1_w_out, v_ffn_norm, v_ffn_up, v_ffn_conv_w, v_ffn_conv_b, v_ffn_down, v_final_norm)))
    return _train_step(x, loss_target, w, m, v)
```

```python
import functools

import jax
import jax.numpy as jnp
from jax import lax
from jax.experimental import pallas as pl
from jax.experimental.pallas import tpu as pltpu
from jax.experimental.pallas import tpu_sc as plsc

F32 = jnp.float32
BF16 = jnp.bfloat16
MESH = pl.DeviceIdType.MESH

D_MODEL = 1024
LANES = 128
SUBLANES = 8
N_CHIPS = 4
EPS = 1e-6
LRU_C = 8.0
D_FF = 2816
FFN_CB = 256
CHUNK = 128
NEG = -1e30

ADAM_LR = 0.001
ADAM_B1 = 0.9
ADAM_B2 = 0.999
ADAM_EPS = 1e-08
ADAM_WD = 0.01
ADAM_STEP = 10
ADAM_C1 = 1.0 - ADAM_B1 ** ADAM_STEP
ADAM_C2 = 1.0 - ADAM_B2 ** ADAM_STEP

_GELU_C = 0.7978845608028654
_GELU_A = 0.044715


def _sigmoid(x):
    return 1.0 / (1.0 + jnp.exp(-x))


def _sigmoid_tanh(x):
    return 0.5 * jnp.tanh(0.5 * x) + 0.5


def _log1p_pos(e):
    w = 1.0 + e
    return jnp.where(w == 1.0, e, jnp.log(w) * (e / (w - 1.0)))


def _softplus(x):
    return jnp.maximum(x, 0.0) + _log1p_pos(jnp.exp(-jnp.abs(x)))


def _gelu(x):
    t = jnp.tanh(_GELU_C * (x + _GELU_A * (x * x * x)))
    return 0.5 * x * (1.0 + t), t


def _gelu_grad(x, t):
    return 0.5 * (1.0 + t) + 0.5 * x * (1.0 - t * t) * (_GELU_C * (1.0 + 3.0 * _GELU_A * x * x))


def _rows(shape):
    return lax.broadcasted_iota(jnp.int32, shape, 0)


def _lanes(shape):
    return lax.broadcasted_iota(jnp.int32, shape, 1)


def _shift_down(x, halo8, j):
    if j == 0:
        return x
    r = pltpu.roll(x, j, 0)
    hr = pltpu.roll(halo8, j, 0)
    top = jnp.where(_rows(hr.shape) < j, hr, r[:SUBLANES])
    return jnp.concatenate([top, r[SUBLANES:]], axis=0)


def _shift_up(x, next8, j):
    if j == 0:
        return x
    n = x.shape[0]
    r = pltpu.roll(x, n - j, 0)
    nr = pltpu.roll(next8, SUBLANES - j, 0)
    bot = jnp.where(_rows(nr.shape) >= SUBLANES - j, nr, r[n - SUBLANES:])
    return jnp.concatenate([r[:n - SUBLANES], bot], axis=0)


def _scan_fwd(a, u):
    n = a.shape[0]
    row = _rows(a.shape)
    h = u
    k = 1
    while k < n:
        keep = row >= k
        h_sh = jnp.where(keep, pltpu.roll(h, k, 0), 0.0)
        a_sh = jnp.where(keep, pltpu.roll(a, k, 0), 1.0)
        h = a * h_sh + h
        a = a * a_sh
        k *= 2
    return h, a


def _scan_rev(b, d):
    n = b.shape[0]
    row = _rows(b.shape)
    g = d
    k = 1
    while k < n:
        keep = row < n - k
        g_sh = jnp.where(keep, pltpu.roll(g, n - k, 0), 0.0)
        b_sh = jnp.where(keep, pltpu.roll(b, n - k, 0), 1.0)
        g = b * g_sh + g
        b = b * b_sh
        k *= 2
    return g, b


def _cumsum_fwd(x):
    n = x.shape[0]
    row = _rows(x.shape)
    k = 1
    while k < n:
        x = x + jnp.where(row >= k, pltpu.roll(x, k, 0), 0.0)
        k *= 2
    return x


def _cumsum_rev(x):
    n = x.shape[0]
    row = _rows(x.shape)
    k = 1
    while k < n:
        x = x + jnp.where(row < n - k, pltpu.roll(x, n - k, 0), 0.0)
        k *= 2
    return x


def _dot(a, b):
    return lax.dot_general(a, b, (((1,), (0,)), ((), ())), preferred_element_type=F32)


def _dot_nt(a, b):
    return lax.dot_general(a, b, (((1,), (1,)), ((), ())), preferred_element_type=F32)


def _dot_tn(a, b):
    return lax.dot_general(a, b, (((0,), (0,)), ((), ())), preferred_element_type=F32)


def _dot_split(x, m_bf16):
    hi = x.astype(BF16)
    lo = (x - hi.astype(F32)).astype(BF16)
    return _dot(hi, m_bf16) + _dot(lo, m_bf16)


def _tile_rows(ts, s):
    return min(ts, s)


def _mm(a_list, w, *, trans_w=False, res=None, norm_bwd=None, norm_out=None, out_dtype=F32, ts=512, nb=None,
        name):
    s = a_list[0].shape[0]
    ks = [a.shape[1] for a in a_list]
    k = sum(ks)
    n = w.shape[0] if trans_w else w.shape[1]
    ts = _tile_rows(ts, s)
    nb = n if nb is None else nb
    na = len(a_list)
    has_res = res is not None
    fused = norm_bwd is not None
    normed = norm_out is not None
    offs = [sum(ks[:p]) for p in range(na)]

    def body(*refs):
        a_refs = refs[:na]
        w_ref = refs[na]
        acc = None
        for a_ref, off, kk in zip(a_refs, offs, ks):
            a = a_ref[...].astype(BF16)
            if trans_w:
                part = _dot_nt(a, w_ref[:, off:off + kk])
            else:
                part = _dot(a, w_ref[off:off + kk, :])
            acc = part if acc is None else acc + part
        if has_res:
            acc = acc + refs[na + 1][...]
        if normed:
            gn_ref, o_ref, n_ref = refs[-3:]
            o_ref[...] = acc.astype(out_dtype)
            r = lax.rsqrt(jnp.mean(acc * acc, axis=-1, keepdims=True) + EPS)
            n_ref[...] = ((acc * r) * gn_ref[...]).astype(BF16)
            return
        if not fused:
            refs[-1][...] = acc.astype(out_dtype)
            return
        h_ref, g_ref, dres_ref, dh_ref, dg_ref = refs[na + 1:]
        i = pl.program_id(1)
        x = h_ref[...]
        r = lax.rsqrt(jnp.mean(x * x, axis=-1, keepdims=True) + EPS)
        xhat = x * r
        part = jnp.sum(acc * xhat, axis=0, keepdims=True)

        @pl.when(i == 0)
        def _():
            dg_ref[...] = part

        @pl.when(i > 0)
        def _():
            dg_ref[...] += part

        dxh = acc * g_ref[...]
        dh_ref[...] = dres_ref[...] + r * (dxh - xhat * jnp.mean(dxh * xhat, axis=-1, keepdims=True))

    in_specs = [pl.BlockSpec((ts, kk), lambda j, i: (i, 0)) for kk in ks]
    if trans_w:
        in_specs.append(pl.BlockSpec((nb, k), lambda j, i: (j, 0)))
    else:
        in_specs.append(pl.BlockSpec((k, nb), lambda j, i: (0, j)))
    args = list(a_list) + [w]
    tile = pl.BlockSpec((ts, nb), lambda j, i: (i, j))
    if has_res:
        in_specs.append(tile)
        args.append(res)
    if fused:
        assert nb == n and not has_res
        vec = pl.BlockSpec((1, n), lambda j, i: (0, 0))
        h, g, dres = norm_bwd
        return pl.pallas_call(
            body, name=name, grid=(1, s // ts), in_specs=in_specs + [tile, vec, tile],
            out_specs=(tile, vec),
            out_shape=(jax.ShapeDtypeStruct((s, n), F32), jax.ShapeDtypeStruct((1, n), F32)),
        )(*args, h, g, dres)
    if normed:
        assert nb == n and out_dtype == F32
        vec = pl.BlockSpec((1, n), lambda j, i: (0, 0))
        return pl.pallas_call(
            body, name=name, grid=(1, s // ts), in_specs=in_specs + [vec], out_specs=(tile, tile),
            out_shape=(jax.ShapeDtypeStruct((s, n), F32), jax.ShapeDtypeStruct((s, n), BF16)),
        )(*args, norm_out)
    return pl.pallas_call(
        body, name=name, grid=(n // nb, s // ts), in_specs=in_specs, out_specs=tile,
        out_shape=jax.ShapeDtypeStruct((s, n), out_dtype),
    )(*args)


def _mm_tn(a_list, b_list, *, ts=512, nb=None, name):
    s = b_list[0].shape[0]
    ks = [a.shape[1] for a in a_list]
    k = sum(ks)
    width = b_list[0].shape[1]
    n = width * len(b_list)
    ts = _tile_rows(ts, s)
    nb = width if nb is None else nb
    per = width // nb
    na = len(a_list)
    nparts = len(b_list)

    def body(*refs):
        a_refs = refs[:na]
        b_refs = refs[na:na + nparts]
        o_ref = refs[-1]
        j = pl.program_id(0)
        i = pl.program_id(1)
        parts = [r[...].astype(BF16) for r in a_refs]
        a = parts[0] if na == 1 else jnp.concatenate(parts, axis=1)

        def accumulate(b_ref):
            upd = _dot_tn(a, b_ref[...].astype(BF16))

            @pl.when(i == 0)
            def _():
                o_ref[...] = upd

            @pl.when(i > 0)
            def _():
                o_ref[...] += upd

        if nparts == 1:
            accumulate(b_refs[0])
        else:
            for part, b_ref in enumerate(b_refs):
                pl.when(j // per == part)(functools.partial(accumulate, b_ref))

    in_specs = [pl.BlockSpec((ts, kk), lambda j, i: (i, 0)) for kk in ks]
    for part in range(nparts):
        in_specs.append(pl.BlockSpec(
            (ts, nb), lambda j, i, part=part: (i, jnp.clip(j - part * per, 0, per - 1))))
    return pl.pallas_call(
        body, name=name, grid=(n // nb, s // ts), in_specs=in_specs,
        out_specs=pl.BlockSpec((k, nb), lambda j, i: (0, j)),
        out_shape=jax.ShapeDtypeStruct((k, n), F32),
    )(*a_list, *b_list)


def _norm_fwd(h, g, *, ts=512, name):
    s, d = h.shape
    ts = _tile_rows(ts, s)

    def body(h_ref, g_ref, n_ref):
        x = h_ref[...]
        r = lax.rsqrt(jnp.mean(x * x, axis=-1, keepdims=True) + EPS)
        n_ref[...] = ((x * r) * g_ref[...]).astype(BF16)

    return pl.pallas_call(
        body, name=name, grid=(s // ts,),
        in_specs=[pl.BlockSpec((ts, d), lambda i: (i, 0)), pl.BlockSpec((1, d), lambda i: (0, 0))],
        out_specs=pl.BlockSpec((ts, d), lambda i: (i, 0)),
        out_shape=jax.ShapeDtypeStruct((s, d), BF16),
    )(h, g)


def _final(h, g, target, *, ts=512, name):
    s, d = h.shape
    ts = _tile_rows(ts, s)
    nt = s // ts

    def body(h_ref, g_ref, t_ref, dh_ref, loss_ref, dg_ref, acc_ref):
        i = pl.program_id(0)
        x = h_ref[...]
        r = lax.rsqrt(jnp.mean(x * x, axis=-1, keepdims=True) + EPS)
        xhat = x * r
        gv = g_ref[...]
        err = xhat * gv - t_ref[...]
        sq = jnp.sum(err * err, axis=0, keepdims=True)
        dy = err * (1.0 / d)
        part = jnp.sum(dy * xhat, axis=0, keepdims=True)

        @pl.when(i == 0)
        def _():
            acc_ref[...] = sq
            dg_ref[...] = part

        @pl.when(i > 0)
        def _():
            acc_ref[...] += sq
            dg_ref[...] += part

        dxh = dy * gv
        dh_ref[...] = r * (dxh - xhat * jnp.mean(dxh * xhat, axis=-1, keepdims=True))

        @pl.when(i == nt - 1)
        def _():
            tot = jnp.sum(acc_ref[...], axis=1, keepdims=True) * (0.5 / d)
            loss_ref[...] = jnp.broadcast_to(tot, (1, LANES))

    tile = pl.BlockSpec((ts, d), lambda i: (i, 0))
    vec = pl.BlockSpec((1, d), lambda i: (0, 0))
    return pl.pallas_call(
        body, name=name, grid=(nt,), in_specs=[tile, vec, tile],
        out_specs=(tile, pl.BlockSpec((1, LANES), lambda i: (0, 0)), vec),
        out_shape=(jax.ShapeDtypeStruct((s, d), F32), jax.ShapeDtypeStruct((1, LANES), F32),
                   jax.ShapeDtypeStruct((1, d), F32)),
        scratch_shapes=[pltpu.VMEM((1, d), F32)],
    )(h, g, target)


def _halo_map(ts, width_blocks):
    per = ts // SUBLANES

    def index(j, i):
        return (jnp.maximum(i * per - 1, 0), width_blocks(j))

    return index


def _even_gates(xc, wa, ba, wx, bx, sp):
    xb = xc.astype(BF16)
    r = _sigmoid(_dot(xb, wa) + ba)
    ig = _sigmoid(_dot(xb, wx) + bx)
    la = (-LRU_C) * r * sp
    a = jnp.exp(la)
    a2 = a * a
    m = jnp.sqrt(-jnp.tanh(la) * (1.0 + a2))
    return r, ig, la, a, a2, m


def _even_core_fwd(p, w4, b4, wa, ba, wx, bx, lam, w3, b3, *, ts=512, name):
    s = p.shape[0]
    ts = _tile_rows(ts, s)
    nt = s // ts
    nblk = 4

    def body(xa_ref, ga_ref, cp_ref, bp_ref, vb_ref, xah_ref, cph_ref, vbh_ref,
             w4_ref, b4_ref, wa_ref, ba_ref, wx_ref, bx_ref, lam_ref, w3_ref, b3_ref,
             ya_ref, yb_ref, hl_ref, hcar_ref):
        i = pl.program_id(1)
        first = (i > 0).astype(F32)
        xa, ga, cp, bp, vb = xa_ref[...], ga_ref[...], cp_ref[...], bp_ref[...], vb_ref[...]
        xa_h = xah_ref[...] * first
        s_h = cph_ref[...] * vbh_ref[...] * first

        xc = b4_ref[...] + w4_ref[3:4, :] * xa
        for k in range(3):
            xc = xc + w4_ref[k:k + 1, :] * _shift_down(xa, xa_h, 3 - k)
        sp = _softplus(-lam_ref[...])
        _, ig, _, a, _, m = _even_gates(xc, wa_ref[0], ba_ref[...], wx_ref[0], bx_ref[...], sp)
        u = m * (ig * xc)
        hs, acum = _scan_fwd(a, u)

        @pl.when(i == 0)
        def _():
            hcar_ref[...] = jnp.zeros_like(hcar_ref)

        hs = hs + acum * hcar_ref[0:1, :]
        hl_ref[...] = hs
        hcar_ref[0:1, :] = hl_ref[ts - 1:ts, :]
        ge, _ = _gelu(ga)
        ya_ref[...] = (hs * ge).astype(BF16)

        sv = cp * vb
        sc = b3_ref[...] + w3_ref[2:3, :] * sv
        for k in range(2):
            sc = sc + w3_ref[k:k + 1, :] * _shift_down(sv, s_h, 2 - k)
        yb_ref[...] = (bp * sc).astype(BF16)

    parts = [pl.BlockSpec((ts, LANES), lambda j, i, q=q: (i, 4 * q + j)) for q in range(5)]
    halos = [pl.BlockSpec((SUBLANES, LANES), _halo_map(ts, lambda j, q=q: 4 * q + j)) for q in (0, 2, 4)]
    vec = pl.BlockSpec((1, LANES), lambda j, i: (0, j))
    out = pl.BlockSpec((ts, LANES), lambda j, i: (i, j))
    return pl.pallas_call(
        body, name=name, grid=(nblk, nt),
        in_specs=parts + halos + [
                  pl.BlockSpec((4, LANES), lambda j, i: (0, j)), vec,
                  pl.BlockSpec((1, LANES, LANES), lambda j, i: (j, 0, 0)), vec,
                  pl.BlockSpec((1, LANES, LANES), lambda j, i: (j, 0, 0)), vec, vec,
                  pl.BlockSpec((3, LANES), lambda j, i: (0, j)), vec],
        out_specs=(out, out, out),
        out_shape=(jax.ShapeDtypeStruct((s, 4 * LANES), BF16), jax.ShapeDtypeStruct((s, 4 * LANES), BF16),
                   jax.ShapeDtypeStruct((s, 4 * LANES), F32)),
        scratch_shapes=[pltpu.VMEM((SUBLANES, LANES), F32)],
    )(*([p] * 8), w4, b4, wa, ba, wx, bx, lam, w3, b3)


def _even_core_bwd(dy, p, hl, w4, b4, wa, wat, ba, wx, wxt, bx, lam, w3, b3, *, ts=512, name):
    s = p.shape[0]
    ts = _tile_rows(ts, s)
    nt = s // ts
    nblk = 4
    per = ts // SUBLANES

    def body(dya_ref, dyb_ref, xa_ref, ga_ref, cp_ref, bp_ref, vb_ref, xah_ref, cph_ref, vbh_ref, hl_ref, hh_ref,
             w4_ref, b4_ref, wa_ref, wat_ref, ba_ref, wx_ref, wxt_ref, bx_ref, lam_ref, w3_ref, b3_ref,
             dxa_ref, dga_ref, dcp_ref, dbp_ref, dvb_ref,
             dw4_ref, db4_ref, dwa_ref, dba_ref, dwx_ref, dbx_ref, dlam_ref, dw3_ref, db3_ref,
             dxc_nx, dsc_nx, cg_ref):
        i = pl.program_id(1)
        ti = nt - 1 - i
        first = (ti > 0).astype(F32)
        xa, ga, cp, bp, vb = xa_ref[...], ga_ref[...], cp_ref[...], bp_ref[...], vb_ref[...]
        xa_h = xah_ref[...] * first
        s_h = cph_ref[...] * vbh_ref[...] * first
        h_h = hh_ref[...] * first

        @pl.when(i == 0)
        def _():
            dxc_nx[...] = jnp.zeros_like(dxc_nx)
            dsc_nx[...] = jnp.zeros_like(dsc_nx)
            cg_ref[...] = jnp.zeros_like(cg_ref)
            for ref in (dw4_ref, db4_ref, dwa_ref, dba_ref, dwx_ref, dbx_ref, dlam_ref, dw3_ref, db3_ref):
                ref[...] = jnp.zeros_like(ref)

        xa_sh = [_shift_down(xa, xa_h, 3 - k) for k in range(3)] + [xa]
        xc = b4_ref[...]
        for k in range(4):
            xc = xc + w4_ref[k:k + 1, :] * xa_sh[k]
        lamv = lam_ref[...]
        sp = _softplus(-lamv)
        r, ig, _, a, a2, m = _even_gates(xc, wa_ref[0], ba_ref[...], wx_ref[0], bx_ref[...], sp)
        sv = cp * vb
        sv_sh = [_shift_down(sv, s_h, 2 - k) for k in range(2)] + [sv]
        sc = b3_ref[...]
        for k in range(3):
            sc = sc + w3_ref[k:k + 1, :] * sv_sh[k]
        hs = hl_ref[...]
        h_prev = _shift_down(hs, h_h, 1)

        dya = dya_ref[...]
        dyb = dyb_ref[...]
        ge, gt = _gelu(ga)
        dga = dya * hs * _gelu_grad(ga, gt)
        dh = dya * ge

        ones8 = jnp.ones((SUBLANES, LANES), F32)
        b = _shift_up(a, ones8, 1)
        g, bcum = _scan_rev(b, dh)
        g = g + bcum * cg_ref[0:1, :]
        ag = a * g
        cg_ref[...] = ag[:SUBLANES]

        da = g * h_prev
        xi = ig * xc
        dm = g * xi
        dig = g * m * xc
        dxc = g * m * ig
        dla = da * a - dm * (a2 / m)
        dr = dla * ((-LRU_C) * sp)
        dlam_ref[...] += jnp.sum(dla * r, axis=0, keepdims=True) * (LRU_C * _sigmoid(-lamv))
        dra = dr * r * (1.0 - r)
        dia = dig * ig * (1.0 - ig)
        drab = dra.astype(BF16)
        diab = dia.astype(BF16)
        xcb = xc.astype(BF16)
        dxc = dxc + _dot(drab, wat_ref[0]) + _dot(diab, wxt_ref[0])
        dwa_ref[0] += _dot_tn(xcb, drab)
        dwx_ref[0] += _dot_tn(xcb, diab)
        dba_ref[...] += jnp.sum(dra, axis=0, keepdims=True)
        dbx_ref[...] += jnp.sum(dia, axis=0, keepdims=True)

        nx = dxc_nx[...]
        dxa = w4_ref[3:4, :] * dxc
        for k in range(3):
            dxa = dxa + w4_ref[k:k + 1, :] * _shift_up(dxc, nx, 3 - k)
        for k in range(4):
            dw4_ref[k:k + 1, :] += jnp.sum(dxc * xa_sh[k], axis=0, keepdims=True)
        db4_ref[...] += jnp.sum(dxc, axis=0, keepdims=True)
        dxc_nx[...] = dxc[:SUBLANES]

        dbp = dyb * sc
        dsc = dyb * bp
        nsc = dsc_nx[...]
        ds = w3_ref[2:3, :] * dsc
        for k in range(2):
            ds = ds + w3_ref[k:k + 1, :] * _shift_up(dsc, nsc, 2 - k)
        for k in range(3):
            dw3_ref[k:k + 1, :] += jnp.sum(dsc * sv_sh[k], axis=0, keepdims=True)
        db3_ref[...] += jnp.sum(dsc, axis=0, keepdims=True)
        dsc_nx[...] = dsc[:SUBLANES]

        dxa_ref[...] = dxa.astype(BF16)
        dga_ref[...] = dga.astype(BF16)
        dcp_ref[...] = (ds * vb).astype(BF16)
        dbp_ref[...] = dbp.astype(BF16)
        dvb_ref[...] = (ds * cp).astype(BF16)

    def rev(j, i):
        return (nt - 1 - i, j)

    def rev_halo(col):
        def index(j, i):
            return (jnp.maximum((nt - 1 - i) * per - 1, 0), col(j))
        return index

    parts = [pl.BlockSpec((ts, LANES), lambda j, i, q=q: (nt - 1 - i, 4 * q + j)) for q in range(5)]
    halos = [pl.BlockSpec((SUBLANES, LANES), rev_halo(lambda j, q=q: 4 * q + j)) for q in (0, 2, 4)]
    one = pl.BlockSpec((ts, LANES), rev)
    vec = pl.BlockSpec((1, LANES), lambda j, i: (0, j))
    mat = pl.BlockSpec((1, LANES, LANES), lambda j, i: (j, 0, 0))
    w4s = pl.BlockSpec((4, LANES), lambda j, i: (0, j))
    w3s = pl.BlockSpec((3, LANES), lambda j, i: (0, j))
    f = jax.ShapeDtypeStruct
    return pl.pallas_call(
        body, name=name, grid=(nblk, nt),
        in_specs=[one, pl.BlockSpec((ts, LANES), lambda j, i: (nt - 1 - i, 4 + j))] + parts + halos + [
                  one, pl.BlockSpec((SUBLANES, LANES), rev_halo(lambda j: j)),
                  w4s, vec, mat, mat, vec, mat, mat, vec, vec, w3s, vec],
        out_specs=(one,) * 5 + (w4s, vec, mat, vec, mat, vec, vec, w3s, vec),
        out_shape=(f((s, 4 * LANES), BF16),) * 5 + (
                   f((4, 4 * LANES), F32), f((1, 4 * LANES), F32),
                   f((4, LANES, LANES), F32), f((1, 4 * LANES), F32),
                   f((4, LANES, LANES), F32), f((1, 4 * LANES), F32), f((1, 4 * LANES), F32),
                   f((3, 4 * LANES), F32), f((1, 4 * LANES), F32)),
        scratch_shapes=[pltpu.VMEM((SUBLANES, LANES), F32), pltpu.VMEM((SUBLANES, LANES), F32),
                        pltpu.VMEM((SUBLANES, LANES), F32)],
    )(dy, dy, *([p] * 8), hl, hl, w4, b4, wa, wat, ba, wx, wxt, bx, lam, w3, b3)


def _ffn_conv(u_ref, uh_ref, w_ref, b_ref, first):
    u = u_ref[...].astype(F32)
    u_h = uh_ref[...].astype(F32)[SUBLANES:] * first
    u_sh = [_shift_down(u, u_h, 2 - k) for k in range(2)] + [u]
    hc = b_ref[...]
    for k in range(3):
        hc = hc + w_ref[k:k + 1, :] * u_sh[k]
    return hc, u_sh


def _ffn_specs(ts, row, halo_row):
    nblk = D_FF // FFN_CB
    specs = []
    for off in (0, nblk):
        specs.append(pl.BlockSpec((ts, FFN_CB), lambda j, i, off=off: (row(i), off + j)))
        specs.append(pl.BlockSpec((16, FFN_CB), lambda j, i, off=off: (halo_row(i), off + j)))
        specs.append(pl.BlockSpec((3, FFN_CB), lambda j, i, off=off: (0, off + j)))
        specs.append(pl.BlockSpec((1, FFN_CB), lambda j, i, off=off: (0, off + j)))
    return specs


FFN_STRIP = 4 * SUBLANES
FFN_HALO = 2 * SUBLANES


def _ffn_stage(u_ref, uh_ref, dst_ref, first):
    dst_ref[0:FFN_HALO, :] = jnp.where(first, uh_ref[...], jnp.zeros_like(uh_ref))
    dst_ref[FFN_HALO:, :] = u_ref[...]


def _ffn_strip_conv(u_ref, r, w, b):
    win = u_ref[pl.ds(r, FFN_HALO + FFN_STRIP), :].astype(F32)
    cur, before = win[FFN_HALO:], win[SUBLANES:FFN_HALO]
    sh = [_shift_down(cur, before, 2 - k) for k in range(2)] + [cur]
    return b + w[0:1] * sh[0] + w[1:2] * sh[1] + w[2:3] * sh[2], sh


def _ffn_core_fwd(up, w, b, *, ts=512, name):
    s = up.shape[0]
    ts = _tile_rows(ts, s)
    nt = s // ts
    nblk = D_FF // FFN_CB
    per = ts // 16

    def body(g_ref, gh_ref, wg_ref, bg_ref, v_ref, vh_ref, wv_ref, bv_ref, act_ref):
        first = (pl.program_id(1) > 0).astype(F32)
        gate, _ = _ffn_conv(g_ref, gh_ref, wg_ref, bg_ref, first)
        val, _ = _ffn_conv(v_ref, vh_ref, wv_ref, bv_ref, first)
        act_ref[...] = (gate * _sigmoid_tanh(gate) * val).astype(BF16)

    return pl.pallas_call(
        body, name=name, grid=(nblk, nt),
        in_specs=_ffn_specs(ts, lambda i: i, lambda i: jnp.maximum(i * per - 1, 0)),
        out_specs=pl.BlockSpec((ts, FFN_CB), lambda j, i: (i, j)),
        out_shape=jax.ShapeDtypeStruct((s, D_FF), BF16),
    )(up, up, w, b, up, up, w, b)


def _ffn_core_bwd(dact, up, w, b, *, ts=1024, name):
    s = up.shape[0]
    ts = _tile_rows(ts, s)
    nt = s // ts
    nblk = D_FF // FFN_CB
    per = ts // 16
    strip, halo = FFN_STRIP, FFN_HALO
    nstrips = ts // strip

    def fold(x):
        out = x[:SUBLANES]
        for r0 in range(SUBLANES, strip, SUBLANES):
            out = out + x[r0:r0 + SUBLANES]
        return out

    def body(da_ref, g_ref, gh_ref, wg_ref, bg_ref, v_ref, vh_ref, wv_ref, bv_ref,
             dg_ref, dv_ref, dwg_ref, dwv_ref, dbg_ref, dbv_ref, nxg_ref, nxv_ref, ug_ref, uv_ref):
        i = pl.program_id(1)
        first = nt - 1 - i > 0

        @pl.when(i == 0)
        def _():
            for ref in (nxg_ref, nxv_ref, dwg_ref, dwv_ref, dbg_ref, dbv_ref):
                ref[...] = jnp.zeros_like(ref)

        _ffn_stage(g_ref, gh_ref, ug_ref, first)
        _ffn_stage(v_ref, vh_ref, uv_ref, first)
        wg, wv, bg, bv = wg_ref[...], wv_ref[...], bg_ref[...], bv_ref[...]
        conv = _ffn_strip_conv

        def conv_t(d, nxt, w):
            out = w[2:3] * d
            for k in range(2):
                out = out + w[k:k + 1] * _shift_up(d, nxt, 2 - k)
            return out

        def step(t, carry):
            nxg, nxv, awg, awv, abg, abv = carry
            r = pl.multiple_of((nstrips - 1 - t) * strip, strip)
            gate, g_sh = conv(ug_ref, r, wg, bg)
            val, v_sh = conv(uv_ref, r, wv, bv)
            da = da_ref[pl.ds(r, strip), :].astype(F32)
            sg = _sigmoid_tanh(gate)
            dgate = da * val * (sg * (1.0 + gate * (1.0 - sg)))
            dval = da * (gate * sg)
            dg_ref[pl.ds(r, strip), :] = conv_t(dgate, nxg, wg).astype(BF16)
            dv_ref[pl.ds(r, strip), :] = conv_t(dval, nxv, wv).astype(BF16)
            awg = tuple(a + fold(dgate * sh) for a, sh in zip(awg, g_sh))
            awv = tuple(a + fold(dval * sh) for a, sh in zip(awv, v_sh))
            return dgate[:SUBLANES], dval[:SUBLANES], awg, awv, abg + fold(dgate), abv + fold(dval)

        zero = jnp.zeros((SUBLANES, FFN_CB), F32)
        init = (nxg_ref[...], nxv_ref[...], (zero,) * 3, (zero,) * 3, zero, zero)
        nxg, nxv, awg, awv, abg, abv = lax.fori_loop(0, nstrips, step, init)
        nxg_ref[...] = nxg
        nxv_ref[...] = nxv
        for k in range(3):
            dwg_ref[k:k + 1, :] += jnp.sum(awg[k], axis=0, keepdims=True)
            dwv_ref[k:k + 1, :] += jnp.sum(awv[k], axis=0, keepdims=True)
        dbg_ref[...] += jnp.sum(abg, axis=0, keepdims=True)
        dbv_ref[...] += jnp.sum(abv, axis=0, keepdims=True)

    def rev(i):
        return nt - 1 - i

    tile = pl.BlockSpec((ts, FFN_CB), lambda j, i: (rev(i), j))
    w_out = pl.BlockSpec((3, FFN_CB), lambda j, i: (0, j))
    b_out = pl.BlockSpec((1, FFN_CB), lambda j, i: (0, j))
    f = jax.ShapeDtypeStruct
    return pl.pallas_call(
        body, name=name, grid=(nblk, nt),
        in_specs=[tile] + _ffn_specs(ts, rev, lambda i: jnp.maximum(rev(i) * per - 1, 0)),
        out_specs=(tile, tile, w_out, w_out, b_out, b_out),
        out_shape=(f((s, D_FF), BF16), f((s, D_FF), BF16), f((3, D_FF), F32), f((3, D_FF), F32),
                   f((1, D_FF), F32), f((1, D_FF), F32)),
        scratch_shapes=[pltpu.VMEM((SUBLANES, FFN_CB), F32), pltpu.VMEM((SUBLANES, FFN_CB), F32),
                        pltpu.VMEM((ts + halo, FFN_CB), BF16), pltpu.VMEM((ts + halo, FFN_CB), BF16)],
    )(dact, up, up, w, b, up, up, w, b)


def _sgu_forward_block(zu, zg, gn, w_ref, bias, seg):
    u, tu = _gelu(zu)
    g, tg = _gelu(zg)
    ms = _dot_split(g * g, seg)
    rs = lax.rsqrt(ms + EPS)
    ghat = g * rs
    gv = ghat * gn
    gvb = gv.astype(BF16)
    lane = _lanes((CHUNK, LANES))
    chunks = []
    for c in range(zu.shape[0] // CHUNK):
        gc = gvb[c * CHUNK:(c + 1) * CHUNK]
        mix = jnp.where(lane < 64, _dot(w_ref[0], gc), _dot(w_ref[1], gc)) + bias
        chunks.append(mix)
    mixed = chunks[0] if len(chunks) == 1 else jnp.concatenate(chunks, axis=0)
    return u, tu, g, tg, rs, ghat, gvb, mixed


def _sgu_fwd(p1, gn, w, bias, seg, *, ts=512, name):
    s = p1.shape[0]
    ts = _tile_rows(ts, s)

    def body(zu_ref, zg_ref, gn_ref, w_ref, bias_ref, seg_ref, yc_ref):
        u, _, _, _, _, _, _, mixed = _sgu_forward_block(
            zu_ref[...], zg_ref[...], gn_ref[...], w_ref, bias_ref[...], seg_ref[...])
        yc_ref[...] = (u * mixed).astype(BF16)

    return pl.pallas_call(
        body, name=name, grid=(4, s // ts),
        in_specs=[pl.BlockSpec((ts, LANES), lambda j, i: (i, j)),
                  pl.BlockSpec((ts, LANES), lambda j, i: (i, 4 + j)),
                  pl.BlockSpec((1, LANES), lambda j, i: (0, j)),
                  pl.BlockSpec((2, CHUNK, CHUNK), lambda j, i: (j, 0, 0)),
                  pl.BlockSpec((CHUNK, LANES), lambda j, i: (0, j)),
                  pl.BlockSpec((LANES, LANES), lambda j, i: (0, 0))],
        out_specs=pl.BlockSpec((ts, LANES), lambda j, i: (i, j)),
        out_shape=jax.ShapeDtypeStruct((s, 4 * LANES), BF16),
    )(p1, p1, gn, w, bias, seg)


def _sgu_bwd(p1, dy, gn, w, wt, bias, seg, tril, *, ts=512, name):
    s = p1.shape[0]
    ts = _tile_rows(ts, s)
    nt = s // ts

    def body(zu_ref, zg_ref, dy_ref, gn_ref, w_ref, wt_ref, bias_ref, seg_ref, tril_ref,
             dzu_ref, dzg_ref, dw_ref, dbias_ref, dgn_ref):
        i = pl.program_id(1)
        zu = zu_ref[...]
        zg = zg_ref[...]
        gn_v = gn_ref[...]
        segv = seg_ref[...]
        u, tu, g, tg, rs, ghat, gvb, mixed = _sgu_forward_block(zu, zg, gn_v, w_ref, bias_ref[...], segv)
        dyv = dy_ref[...]
        du = dyv * mixed
        dmx = dyv * u

        @pl.when(i == 0)
        def _():
            dw_ref[...] = jnp.zeros_like(dw_ref)
            dbias_ref[...] = jnp.zeros_like(dbias_ref)
            dgn_ref[...] = jnp.zeros_like(dgn_ref)

        lane = _lanes((CHUNK, LANES))
        dgv_chunks = []
        dbias = jnp.zeros((CHUNK, LANES), F32)
        for c in range(ts // CHUNK):
            dmc = dmx[c * CHUNK:(c + 1) * CHUNK]
            gc = gvb[c * CHUNK:(c + 1) * CHUNK]
            dm_a = jnp.where(lane < 64, dmc, 0.0).astype(BF16)
            dm_b = jnp.where(lane >= 64, dmc, 0.0).astype(BF16)
            dw_ref[0] += _dot_nt(dm_a, gc)
            dw_ref[1] += _dot_nt(dm_b, gc)
            dgv_chunks.append(_dot(wt_ref[0], dm_a) + _dot(wt_ref[1], dm_b))
            dbias = dbias + dmc
        dbias_ref[...] += dbias
        dgv = dgv_chunks[0] if len(dgv_chunks) == 1 else jnp.concatenate(dgv_chunks, axis=0)
        dgn_ref[...] += jnp.sum(dgv * ghat, axis=0, keepdims=True)
        dgh = dgv * gn_v
        dg = rs * (dgh - ghat * _dot_split(dgh * ghat, segv))
        dzu_ref[...] = (du * _gelu_grad(zu, tu)).astype(BF16)
        dzg_ref[...] = (dg * _gelu_grad(zg, tg)).astype(BF16)

        @pl.when(i == nt - 1)
        def _():
            dw_ref[0] = dw_ref[0] * tril_ref[...]
            dw_ref[1] = dw_ref[1] * tril_ref[...]

    f = jax.ShapeDtypeStruct
    colj = pl.BlockSpec((ts, LANES), lambda j, i: (i, j))
    wsp = pl.BlockSpec((2, CHUNK, CHUNK), lambda j, i: (j, 0, 0))
    sq = pl.BlockSpec((LANES, LANES), lambda j, i: (0, 0))
    return pl.pallas_call(
        body, name=name, grid=(4, nt),
        in_specs=[colj, pl.BlockSpec((ts, LANES), lambda j, i: (i, 4 + j)), colj,
                  pl.BlockSpec((1, LANES), lambda j, i: (0, j)), wsp, wsp,
                  pl.BlockSpec((CHUNK, LANES), lambda j, i: (0, j)), sq, sq],
        out_specs=(colj, colj, wsp, pl.BlockSpec((CHUNK, LANES), lambda j, i: (0, j)),
                   pl.BlockSpec((1, LANES), lambda j, i: (0, j))),
        out_shape=(f((s, 4 * LANES), BF16), f((s, 4 * LANES), BF16), f((8, CHUNK, CHUNK), F32),
                   f((CHUNK, 4 * LANES), F32), f((1, 4 * LANES), F32)),
    )(p1, p1, dy, gn, w, wt, bias, seg, tril)


F_COL = 20


def _fcum_fwd(p1, bf, *, ts=512, name):
    s = p1.shape[0]
    ts = _tile_rows(ts, s)

    def body(f_ref, bf_ref, c_ref, car_ref):
        i = pl.program_id(0)
        z = f_ref[...] + bf_ref[...]
        logf = jnp.minimum(z, 0.0) - _log1p_pos(jnp.exp(-jnp.abs(z)))

        @pl.when(i == 0)
        def _():
            car_ref[...] = jnp.zeros_like(car_ref)

        c_ref[...] = _cumsum_fwd(logf) + car_ref[0:1, :]
        car_ref[0:1, :] = c_ref[ts - 1:ts, :]

    return pl.pallas_call(
        body, name=name, grid=(s // ts,),
        in_specs=[pl.BlockSpec((ts, LANES), lambda i: (i, F_COL)), pl.BlockSpec((1, LANES), lambda i: (0, 0))],
        out_specs=pl.BlockSpec((ts, LANES), lambda i: (i, 0)),
        out_shape=jax.ShapeDtypeStruct((s, LANES), F32),
        scratch_shapes=[pltpu.VMEM((SUBLANES, LANES), F32)],
    )(p1, bf)


def _fcum_bwd(dcs, dcq, p1, bf, *, ts=512, name):
    s = p1.shape[0]
    ts = _tile_rows(ts, s)
    nt = s // ts

    def body(dc_ref, dcq_ref, f_ref, bf_ref, df_ref, dbf_ref, car_ref):
        i = pl.program_id(0)

        @pl.when(i == 0)
        def _():
            car_ref[...] = jnp.zeros_like(car_ref)
            dbf_ref[...] = jnp.zeros_like(dbf_ref)

        dc = dc_ref[...]
        lane = _lanes((ts, LANES))
        for h in range(8):
            dc = dc + jnp.where(lane == h, dcq_ref[:, h * LANES:(h + 1) * LANES], 0.0)
        dlog = _cumsum_rev(dc) + car_ref[0:1, :]
        car_ref[...] = dlog[:SUBLANES]
        z = f_ref[...] + bf_ref[...]
        df = dlog * _sigmoid(-z)
        df_ref[...] = df.astype(BF16)
        dbf_ref[...] += jnp.sum(df, axis=0, keepdims=True)

    return pl.pallas_call(
        body, name=name, grid=(nt,),
        in_specs=[pl.BlockSpec((ts, LANES), lambda i: (nt - 1 - i, 0)),
                  pl.BlockSpec((ts, 8 * LANES), lambda i: (nt - 1 - i, 0)),
                  pl.BlockSpec((ts, LANES), lambda i: (nt - 1 - i, F_COL)),
                  pl.BlockSpec((1, LANES), lambda i: (0, 0))],
        out_specs=(pl.BlockSpec((ts, LANES), lambda i: (nt - 1 - i, 0)), pl.BlockSpec((1, LANES), lambda i: (0, 0))),
        out_shape=(jax.ShapeDtypeStruct((s, LANES), BF16), jax.ShapeDtypeStruct((1, LANES), F32)),
        scratch_shapes=[pltpu.VMEM((SUBLANES, LANES), F32)],
    )(dcs, dcq, p1, bf)


def _fox_scores(qm, kb, bias, ck, diagonal):
    sc = _dot_nt(qm, kb) + bias - ck
    if diagonal:
        sc = jnp.where(_lanes(sc.shape) <= _rows(sc.shape), sc, NEG)
    return sc


def _head_masks(shape):
    lane = _lanes(shape)
    return lane < 64, lane >= 64


def _fox_fwd(p1, cq, ck, *, tq=512, name):
    s = p1.shape[0]
    tq = _tile_rows(tq, s)
    tk = tq
    nq = s // tq

    def body(q_ref, k_ref, v_ref, cq_ref, ck_ref, o_ref, lb_ref):
        qi = pl.program_id(1)
        q = q_ref[...] * 0.125
        first, second = _head_masks((tq, LANES))
        qms = [jnp.where(sel, q, 0.0).astype(BF16) for sel in (first, second)]
        cqs = [cq_ref[:, hh * LANES:(hh + 1) * LANES] for hh in range(2)]
        biases = [jnp.tile(cqh, (1, tk // LANES)) for cqh in cqs]

        def step(kj, carry, diagonal):
            cols = pl.ds(pl.multiple_of(kj * tk, tk), tk)
            kb = k_ref[cols, :].astype(BF16)
            vb = v_ref[cols, :].astype(BF16)
            new, outs = [], []
            acc = carry[4]
            for hh in range(2):
                m_prev, l_prev = carry[2 * hh], carry[2 * hh + 1]
                sc = _fox_scores(qms[hh], kb, biases[hh], ck_ref[hh, :, cols], diagonal)
                m_new = jnp.maximum(m_prev, jnp.max(sc, axis=1, keepdims=True))
                pm = jnp.exp(sc - jnp.tile(m_new, (1, tk // LANES)))
                alpha = jnp.exp(m_prev - m_new)
                new += [m_new, alpha * l_prev + jnp.sum(pm, axis=1, keepdims=True)]
                outs.append(acc * alpha + _dot(pm.astype(BF16), vb))
            return tuple(new) + (jnp.where(first, outs[0], outs[1]),)

        zero = jnp.zeros((tq, LANES), F32)
        low = jnp.full((tq, LANES), NEG, F32)
        carry = lax.fori_loop(0, qi, lambda kj, c: step(kj, c, False), (low, zero, low, zero, zero))
        m0, l0, m1, l1, acc = step(qi, carry, True)
        o_ref[...] = (acc / jnp.where(first, l0, l1)).astype(BF16)
        lb_ref[:, 0:LANES] = cqs[0] - (m0 + jnp.log(l0))
        lb_ref[:, LANES:2 * LANES] = cqs[1] - (m1 + jnp.log(l1))

    return pl.pallas_call(
        body, name=name, grid=(4, nq),
        in_specs=[pl.BlockSpec((tq, LANES), lambda j, qi: (qi, 8 + j)),
                  pl.BlockSpec((s, LANES), lambda j, qi: (0, 12 + j)),
                  pl.BlockSpec((s, LANES), lambda j, qi: (0, 16 + j)),
                  pl.BlockSpec((tq, 2 * LANES), lambda j, qi: (qi, j)),
                  pl.BlockSpec((2, 1, s), lambda j, qi: (j, 0, 0))],
        out_specs=(pl.BlockSpec((tq, LANES), lambda j, qi: (qi, j)),
                   pl.BlockSpec((tq, 2 * LANES), lambda j, qi: (qi, j))),
        out_shape=(jax.ShapeDtypeStruct((s, 4 * LANES), BF16), jax.ShapeDtypeStruct((s, 8 * LANES), F32)),
    )(p1, p1, p1, cq, ck)


def _fox_delta(dy, o, sel, *, ts=512, name):
    s = o.shape[0]
    ts = _tile_rows(ts, s)

    def body(do_ref, o_ref, sel_ref, d_ref):
        prod = do_ref[...] * o_ref[...].astype(F32)
        d_ref[:, 0:LANES] = _dot_split(prod, sel_ref[0])
        d_ref[:, LANES:2 * LANES] = _dot_split(prod, sel_ref[1])

    return pl.pallas_call(
        body, name=name, grid=(4, s // ts),
        in_specs=[pl.BlockSpec((ts, LANES), lambda j, i: (i, 4 + j)),
                  pl.BlockSpec((ts, LANES), lambda j, i: (i, j)),
                  pl.BlockSpec((2, LANES, LANES), lambda j, i: (0, 0, 0))],
        out_specs=pl.BlockSpec((ts, 2 * LANES), lambda j, i: (i, j)),
        out_shape=jax.ShapeDtypeStruct((s, 8 * LANES), F32),
    )(dy, o, sel)


def _fox_bwd(p1, dy, lb, delta, ck, *, tq=512, name):
    s = p1.shape[0]
    tq = _tile_rows(tq, s)
    tk = tq
    nq = s // tq

    def body(q_ref, k_ref, v_ref, do_ref, lb_ref, dl_ref, ck_ref,
             dq_ref, dk_ref, dv_ref, dck_ref, dcq_ref, dqa_ref, dra_ref):
        kj = pl.program_id(1)

        @pl.when(kj == 0)
        def _():
            dqa_ref[...] = jnp.zeros_like(dqa_ref)
            dra_ref[...] = jnp.zeros_like(dra_ref)

        kf = k_ref[...]
        kb = kf.astype(BF16)
        vb = v_ref[...].astype(BF16)
        first, second = _head_masks((tk, LANES))
        kms = [jnp.where(sel, kf, 0.0).astype(BF16) for sel in (first, second)]
        cks = [ck_ref[hh] for hh in range(2)]

        def step(qi, carry, diagonal):
            dk_acc, dv_acc, dc0, dc1 = carry
            dcs = [dc0, dc1]
            rows = pl.ds(pl.multiple_of(qi * tq, tq), tq)
            q = q_ref[rows, :] * 0.125
            do = do_ref[rows, :]
            for hh, sel in enumerate((first, second)):
                qm = jnp.where(sel, q, 0.0).astype(BF16)
                dom = jnp.where(sel, do, 0.0).astype(BF16)
                bias = jnp.tile(lb_ref[rows, hh * LANES:(hh + 1) * LANES], (1, tk // LANES))
                pm = jnp.exp(_fox_scores(qm, kb, bias, cks[hh], diagonal))
                dv_acc = dv_acc + _dot_tn(pm.astype(BF16), dom)
                dp = _dot_nt(dom, vb)
                ds = pm * (dp - jnp.tile(dl_ref[rows, hh * LANES:(hh + 1) * LANES], (1, tk // LANES)))
                dsb = ds.astype(BF16)
                dk_acc = dk_acc + _dot_tn(dsb, qm)
                dcs[hh] = dcs[hh] - jnp.sum(ds, axis=0, keepdims=True)
                dqa_ref[rows, :] += _dot(dsb, kms[hh])
                dra_ref[hh, rows, :] += jnp.sum(ds, axis=1, keepdims=True)
            return dk_acc, dv_acc, dcs[0], dcs[1]

        zero = jnp.zeros((tk, LANES), F32)
        zrow = jnp.zeros((1, tk), F32)
        carry = step(kj, (zero, zero, zrow, zrow), True)
        dk_acc, dv_acc, dc0, dc1 = lax.fori_loop(kj + 1, nq, lambda qi, c: step(qi, c, False), carry)
        dk_ref[...] = dk_acc.astype(BF16)
        dv_ref[...] = dv_acc.astype(BF16)
        dck_ref[0] = dc0
        dck_ref[1] = dc1

        @pl.when(kj == nq - 1)
        def _():
            dq_ref[...] = (dqa_ref[...] * 0.125).astype(BF16)
            dcq_ref[:, 0:LANES] = dra_ref[0]
            dcq_ref[:, LANES:2 * LANES] = dra_ref[1]

    def full(width, col0):
        return pl.BlockSpec((s, width), lambda j, kj: (0, col0 + j))

    kblk = pl.BlockSpec((tk, LANES), lambda j, kj: (kj, j))
    f = jax.ShapeDtypeStruct
    return pl.pallas_call(
        body, name=name, grid=(4, nq),
        in_specs=[full(LANES, 8),
                  pl.BlockSpec((tk, LANES), lambda j, kj: (kj, 12 + j)),
                  pl.BlockSpec((tk, LANES), lambda j, kj: (kj, 16 + j)),
                  full(LANES, 4), full(2 * LANES, 0), full(2 * LANES, 0),
                  pl.BlockSpec((2, 1, tk), lambda j, kj: (j, 0, kj))],
        out_specs=(full(LANES, 0), kblk, kblk, pl.BlockSpec((2, 1, tk), lambda j, kj: (j, 0, kj)),
                   full(2 * LANES, 0)),
        out_shape=(f((s, 4 * LANES), BF16), f((s, 4 * LANES), BF16), f((s, 4 * LANES), BF16),
                   f((8, 1, s), F32), f((s, 8 * LANES), F32)),
        scratch_shapes=[pltpu.VMEM((s, LANES), F32), pltpu.VMEM((2, s, LANES), F32)],
    )(p1, p1, p1, dy, lb, delta, ck)


def _row_block(r, cap=256):
    best = None
    for rb in range(2 * SUBLANES, min(r, cap) + 1, 2 * SUBLANES):
        if r % rb == 0:
            best = rb
    return r if best is None else best


def _adamw_many(ws, gs, ms, vs, *, name):
    shapes = [a.shape for a in ws]

    def flat(a):
        return a.reshape((-1, a.shape[-1]))

    n = len(ws)
    operands = [flat(a) for group in (ws, gs, ms, vs) for a in group]

    def body(*refs):
        w_refs, g_refs, m_refs, v_refs = (refs[i * n:(i + 1) * n] for i in range(4))
        d_refs, nm_refs, nv_refs = (refs[(4 + i) * n:(5 + i) * n] for i in range(3))
        for p in range(n):
            gv = g_refs[p][...]
            mn = ADAM_B1 * m_refs[p][...] + (1.0 - ADAM_B1) * gv
            vn = ADAM_B2 * v_refs[p][...] + (1.0 - ADAM_B2) * (gv * gv)
            m_hat = mn / ADAM_C1
            v_hat = vn / ADAM_C2
            d_refs[p][...] = (-ADAM_LR) * (m_hat / (jnp.sqrt(v_hat) + ADAM_EPS) + ADAM_WD * w_refs[p][...])
            nm_refs[p][...] = mn
            nv_refs[p][...] = vn

    vm = pl.BlockSpec(memory_space=pltpu.VMEM)
    out_shape = [jax.ShapeDtypeStruct(flat(a).shape, F32) for a in ws] * 3
    outs = pl.pallas_call(
        body, name=name, in_specs=[vm] * (4 * n), out_specs=[vm] * (3 * n), out_shape=out_shape,
    )(*operands)
    return [tuple(outs[i * n + p].reshape(shapes[p]) for i in range(3)) for p in range(n)]


def _adamw_halves(w, mine, theirs, m, v, core, *, name):
    layers, r, c = w.shape
    rh = r // 2
    rb = _row_block(rh)
    per = rh // rb

    def body(core_ref, w_ref, *refs):
        g_refs = refs[:2 * layers]
        m_ref, v_ref, g_ref, d_ref, nm_ref, nv_ref = refs[2 * layers:]
        own = pl.program_id(1) == core_ref[0]
        gv = jnp.where(own, g_refs[0][...], g_refs[layers][...])
        for l in range(1, layers):
            gv = jnp.where(pl.program_id(0) == l, jnp.where(own, g_refs[l][...], g_refs[layers + l][...]), gv)
        g_ref[...] = gv
        mn = ADAM_B1 * m_ref[...] + (1.0 - ADAM_B1) * gv
        vn = ADAM_B2 * v_ref[...] + (1.0 - ADAM_B2) * (gv * gv)
        m_hat = mn / ADAM_C1
        v_hat = vn / ADAM_C2
        d_ref[...] = (-ADAM_LR) * (m_hat / (jnp.sqrt(v_hat) + ADAM_EPS) + ADAM_WD * w_ref[...])
        nm_ref[...] = mn
        nv_ref[...] = vn

    full = pl.BlockSpec((None, rb, c), lambda l, h, i, core_ref: (l, h * per + i, 0))
    half = pl.BlockSpec((rb, c), lambda l, h, i, core_ref: (i, 0))
    shp = jax.ShapeDtypeStruct((layers, r, c), F32)
    return pl.pallas_call(
        body, name=name,
        grid_spec=pltpu.PrefetchScalarGridSpec(
            num_scalar_prefetch=1, grid=(layers, 2, per),
            in_specs=[full] + [half] * (2 * layers) + [full, full], out_specs=(full,) * 4),
        out_shape=(shp,) * 4,
    )(core, w, *mine, *theirs, m, v)


def _pair_specs(col, rb, c):
    if col:
        g_spec = pl.BlockSpec((None, rb, c), lambda t, i, sel: (sel[0], i, sel[1 + t]))
    else:
        g_spec = pl.BlockSpec((None, None, rb, c), lambda t, i, sel: (sel[1 + t], sel[0], i, 0))
    return g_spec, pl.BlockSpec((None, rb, c), lambda t, i, sel: (sel[1 + t], i, 0))


def _pair_sum(g, col, ra, sel, after, *, name):
    _, rh, c = ra.shape
    rb = _row_block(rh)

    def body(sel_ref, g_ref, ra_ref, after_ref, h16_ref):
        h16_ref[...] = (g_ref[...] + ra_ref[...]).astype(BF16)

    g_spec, ra_spec = _pair_specs(col, rb, c)
    return pl.pallas_call(
        body, name=name,
        grid_spec=pltpu.PrefetchScalarGridSpec(
            num_scalar_prefetch=1, grid=(2, rh // rb), in_specs=[g_spec, ra_spec, ANY],
            out_specs=pl.BlockSpec((None, rb, c), lambda t, i, sel: (t, i, 0))),
        out_shape=jax.ShapeDtypeStruct((2, rh, c), BF16),
    )(sel, g, ra, after)


def _first_sum(g, col, ra, r1, sel, after, *, name):
    _, rh, c = ra.shape
    rb = _row_block(rh)

    def body(sel_ref, g_ref, ra_ref, r_ref, after_ref, s_ref, s16_ref):
        tot = (g_ref[...] + ra_ref[...]) + r_ref[...].astype(F32)
        s_ref[...] = tot
        s16_ref[...] = tot.astype(BF16)

    g_spec, ra_spec = _pair_specs(col, rb, c)
    slot = pl.BlockSpec((None, rb, c), lambda t, i, sel_ref: (t, i, 0))
    return pl.pallas_call(
        body, name=name,
        grid_spec=pltpu.PrefetchScalarGridSpec(
            num_scalar_prefetch=1, grid=(2, rh // rb), in_specs=[g_spec, ra_spec, slot, ANY],
            out_specs=(slot, slot)),
        out_shape=(jax.ShapeDtypeStruct((2, rh, c), F32), jax.ShapeDtypeStruct((2, rh, c), BF16)),
    )(sel, g, ra, r1, after)


def _second_sum(s1, r2, mine, after, *, name):
    _, rh, c = s1.shape
    rb = _row_block(rh)

    def body(mine_ref, s_ref, r_ref, after_ref, t_ref):
        t_ref[...] = s_ref[...] + r_ref[...].astype(F32)

    flat = pl.BlockSpec((rb, c), lambda i, mine_ref: (i, 0))
    return pl.pallas_call(
        body, name=name,
        grid_spec=pltpu.PrefetchScalarGridSpec(
            num_scalar_prefetch=1, grid=(rh // rb,),
            in_specs=[pl.BlockSpec((None, rb, c), lambda i, mine_ref: (mine_ref[0], i, 0)), flat, ANY],
            out_specs=flat),
        out_shape=jax.ShapeDtypeStruct((rh, c), F32),
    )(mine, s1, r2, after)


def _place(shards, layer, col, chip, dtype, *, name):
    _, r, c = shards.shape
    rh = r // 2
    rb = _row_block(rh)

    def body(chip_ref, s_ref, o_ref):
        o_ref[...] = s_ref[...].astype(o_ref.dtype)

    if col:
        out_spec = pl.BlockSpec((None, rb, c), lambda h, i, chip_ref: (h, i, chip_ref[0]))
        shape = (2, rh, N_CHIPS * c)
    else:
        out_spec = pl.BlockSpec((None, None, rb, c), lambda h, i, chip_ref: (chip_ref[0], h, i, 0))
        shape = (N_CHIPS, 2, rh, c)
    per = rh // rb
    return pl.pallas_call(
        body, name=name,
        grid_spec=pltpu.PrefetchScalarGridSpec(
            num_scalar_prefetch=1, grid=(2, per),
            in_specs=[pl.BlockSpec((None, rb, c), lambda h, i, chip_ref: (layer, h * per + i, 0))],
            out_specs=out_spec),
        out_shape=jax.ShapeDtypeStruct(shape, dtype),
    )(chip, shards)


ANY = pl.BlockSpec(memory_space=pl.ANY)


def _mesh_pos():
    return lax.axis_index("x"), lax.axis_index("y"), lax.axis_index("c")


def _other_chips(x, y):
    return [(1 - x, y), (x, 1 - y), (1 - x, 1 - y)]


def _remote(src, dst, ssem, rsem, dev):
    return pltpu.make_async_remote_copy(src_ref=src, dst_ref=dst, send_sem=ssem, recv_sem=rsem,
                                        device_id=dev, device_id_type=MESH)


def _flip(a, b):
    return a + b - 2 * a * b


def _handshake(peers):
    barrier = pltpu.get_barrier_semaphore()
    for peer in peers:
        pl.semaphore_signal(barrier, inc=1, device_id=peer, device_id_type=MESH)
    pl.semaphore_wait(barrier, len(peers))


def _slab(ref, col, width, k, h):
    if not col:
        return ref.at[k, h]
    start = k * width if isinstance(k, int) else pl.multiple_of(k * width, LANES)
    return ref.at[h, :, pl.ds(start, width)]


def _all_gather(bufs, cols, *, collective_id, name):
    n = len(bufs)
    widths = [b.shape[2] // N_CHIPS if col else b.shape[3] for b, col in zip(bufs, cols)]
    outs = [jax.new_ref(b, memory_space=pltpu.MemorySpace.HBM) for b in bufs]

    def body(ssem, rsem):
        x, y, c = _mesh_pos()
        me = 2 * x + y
        sib = (x, y, 1 - c)
        n1 = (_flip(x, 1 - c), _flip(y, c))
        n2 = (_flip(x, c), _flip(y, 1 - c))
        k1 = 2 * n1[0] + n1[1]
        k2 = 2 * n2[0] + n2[1]
        kd = 2 * (1 - x) + (1 - y)
        _handshake([n1 + (c,), n2 + (c,), sib])

        def slab(a, k, h):
            return _slab(outs[a], cols[a], widths[a], k, h)

        def copy(a, j, src, dst, dev):
            return _remote(src, dst, ssem.at[a, j], rsem.at[a, j], dev)

        sends = []
        for a in range(n):
            for j, nb in ((0, n1), (1, n2)):
                own = slab(a, me, c)
                cp = copy(a, j, own, own, nb + (c,))
                cp.start()
                sends.append(cp)
        arrivals = ((0, k1, n1, 3), (1, k2, n2, 4), (2, kd, n2, 5))
        for j, k, nb, fwd in arrivals:
            for a in range(n):
                got = slab(a, k, c)
                copy(a, j, got, got, nb + (c,)).wait_recv()
                if j == 0:
                    cp = copy(a, 2, got, got, n2 + (c,))
                    cp.start()
                    sends.append(cp)
                cp = copy(a, fwd, got, got, sib)
                cp.start()
                sends.append(cp)
        for fwd, k in ((3, k2), (4, k1), (5, kd)):
            for a in range(n):
                got = slab(a, k, 1 - c)
                copy(a, fwd, got, got, sib).wait_recv()
        for cp in sends:
            cp.wait_send()

    _sequencer_call(body, (), [(n, 6), (n, 6)], collective_id, name)()
    return [ref[...] for ref in outs]


def _sequencer_call(body, out_types, sem_shapes, collective_id, name):
    return pl.kernel(
        body, name=name, out_type=out_types,
        mesh=plsc.ScalarSubcoreMesh(axis_name="sequencer", num_cores=1),
        scratch_types=[pltpu.SemaphoreType.DMA(shape) for shape in sem_shapes],
        compiler_params=pltpu.CompilerParams(collective_id=collective_id))


def _send_other_half(grads, cols, *, collective_id, name):
    n = len(grads)

    def shard_shape(g, col):
        if col:
            return (g.shape[1], g.shape[2] // N_CHIPS)
        return g.shape[2:]

    shapes = [shard_shape(g, col) for g, col in zip(grads, cols)]

    def body(*refs):
        ins, outs = refs[:n], refs[n:2 * n]
        ssem, rsem = refs[2 * n:]
        x, y, c = _mesh_pos()
        sib = (x, y, 1 - c)
        _handshake([sib])
        sends = []
        for a in range(n):
            for k in range(N_CHIPS):
                src = _slab(ins[a], cols[a], shapes[a][1], k, 1 - c)
                cp = _remote(src, outs[a].at[k], ssem.at[a, k], rsem.at[a, k], sib)
                cp.start()
                sends.append(cp)
        for cp in sends:
            cp.wait()

    out_types = [jax.ShapeDtypeStruct((N_CHIPS,) + shp, g.dtype) for g, shp in zip(grads, shapes)]
    return _sequencer_call(body, out_types, [(n, N_CHIPS), (n, N_CHIPS)], collective_id, name)(*grads)


def _send_first(sums, *, collective_id, name):
    n = len(sums)

    def body(*refs):
        ins, outs = refs[:n], refs[n:2 * n]
        ssem, rsem = refs[2 * n:]
        x, y, c = _mesh_pos()
        nb = (_flip(x, c), _flip(y, 1 - c), c)
        _handshake([nb])
        sends = []
        for a in range(n):
            for t in range(2):
                cp = _remote(ins[a].at[t], outs[a].at[t], ssem.at[a, t], rsem.at[a, t], nb)
                cp.start()
                sends.append(cp)
        for cp in sends:
            cp.wait()

    out_types = [jax.ShapeDtypeStruct(h.shape, h.dtype) for h in sums]
    return _sequencer_call(body, out_types, [(n, 2), (n, 2)], collective_id, name)(*sums)


def _send_second(sums, *, collective_id, name):
    n = len(sums)

    def body(*refs):
        ins, outs = refs[:n], refs[n:2 * n]
        ssem, rsem = refs[2 * n:]
        x, y, c = _mesh_pos()
        nb = (_flip(x, 1 - c), _flip(y, c), c)
        other = 1 - (c * y + (1 - c) * x)
        _handshake([nb])
        sends = []
        for a in range(n):
            cp = _remote(ins[a].at[other], outs[a], ssem.at[a], rsem.at[a], nb)
            cp.start()
            sends.append(cp)
        for cp in sends:
            cp.wait()

    out_types = [jax.ShapeDtypeStruct(s.shape[1:], s.dtype) for s in sums]
    return _sequencer_call(body, out_types, [(n,), (n,)], collective_id, name)(*sums)


def _swap_halves(halves, *, collective_id, name):
    n = len(halves)

    def body(*refs):
        ins, outs = refs[:n], refs[n:2 * n]
        ssem, rsem = refs[2 * n:]
        x, y, c = _mesh_pos()
        sib = (x, y, 1 - c)
        _handshake([sib])
        cps = []
        for a in range(n):
            cp = _remote(ins[a], outs[a], ssem.at[a], rsem.at[a], sib)
            cp.start()
            cps.append(cp)
        for cp in cps:
            cp.wait()

    out_types = [jax.ShapeDtypeStruct(h.shape, h.dtype) for h in halves]
    return _sequencer_call(body, out_types, [(n,), (n,)], collective_id, name)(*halves)


def _all_reduce_small(buf, *, name):
    r = buf.shape[0]
    rh = r // 2

    def body(in_ref, out_ref, x1_ref, x2_ref, ssem, rsem):
        x, y, c = _mesh_pos()
        me = 2 * x + y
        sib = (x, y, 1 - c)
        chips = _other_chips(x, y)
        cp = _remote(in_ref, x1_ref, ssem.at[0], rsem.at[0], sib)
        cp.start()
        cp.wait()
        off = pl.multiple_of(c * rh, SUBLANES)
        x2_ref[me] = in_ref[pl.ds(off, rh), :] + x1_ref[pl.ds(off, rh), :]
        sends = []
        for j, (cx, cy) in enumerate(chips):
            s = _remote(x2_ref.at[me], x2_ref.at[me], ssem.at[1 + j], rsem.at[1 + j], (cx, cy, c))
            s.start()
            sends.append(s)
        for j, (cx, cy) in enumerate(chips):
            slot = x2_ref.at[2 * cx + cy]
            _remote(slot, slot, ssem.at[1 + j], rsem.at[1 + j], (cx, cy, c)).wait_recv()
        out_ref[pl.ds(off, rh), :] = ((x2_ref[0] + x2_ref[1]) + x2_ref[2]) + x2_ref[3]
        for s in sends:
            s.wait_send()
        mine = out_ref.at[pl.ds(off, rh), :]
        s3 = _remote(mine, mine, ssem.at[4], rsem.at[4], sib)
        s3.start()
        off2 = pl.multiple_of((1 - c) * rh, SUBLANES)
        theirs = out_ref.at[pl.ds(off2, rh), :]
        _remote(theirs, theirs, ssem.at[4], rsem.at[4], sib).wait_recv()
        s3.wait_send()

    vm = pl.BlockSpec(memory_space=pltpu.VMEM)
    return pl.pallas_call(
        body, name=name, in_specs=[vm], out_specs=vm,
        out_shape=jax.ShapeDtypeStruct((r, LANES), F32),
        scratch_shapes=[pltpu.VMEM((r, LANES), F32), pltpu.VMEM((N_CHIPS, rh, LANES), F32),
                        pltpu.SemaphoreType.DMA((5,)), pltpu.SemaphoreType.DMA((5,))],
    )(buf)


PACK_ALIGN = 2 * SUBLANES * LANES


def _pack(arrays):
    parts, offs, off = [], [], 0
    for a in arrays:
        flat = a.reshape(-1).astype(F32)
        padded = -(-flat.shape[0] // PACK_ALIGN) * PACK_ALIGN
        parts.append(jnp.pad(flat, (0, padded - flat.shape[0])))
        offs.append(off)
        off += padded
    buf = jnp.concatenate(parts).reshape(-1, LANES)
    return buf, offs


def _unpack(buf, offs, shapes):
    flat = buf.reshape(-1)
    out = []
    for off, shp in zip(offs, shapes):
        size = 1
        for d in shp:
            size *= d
        out.append(flat[off:off + size].reshape(shp))
    return out


def _cols_from_shards(g4):
    _, k, ns = g4.shape
    return jnp.transpose(g4, (1, 0, 2)).reshape(k, N_CHIPS * ns)


def _cols_to_shards(w):
    k, n = w.shape
    return jnp.transpose(w.reshape(k, N_CHIPS, n // N_CHIPS), (1, 0, 2))


def _pair_blockdiag(w8):
    w = w8.reshape(4, 2, 64, 64)
    z = jnp.zeros((4, 64, 64), w8.dtype)
    top = jnp.concatenate([w[:, 0], z], axis=2)
    bot = jnp.concatenate([z, w[:, 1]], axis=2)
    return jnp.concatenate([top, bot], axis=1)


def _pair_diag_blocks(w4):
    a = w4[:, :64, :64]
    b = w4[:, 64:, 64:]
    return jnp.stack([a, b], axis=1).reshape(8, 64, 64)


def _local_step(x, target, wts, on_event=None):
    s = x.shape[0]
    g = {}

    def event(name, token):
        if on_event is not None:
            on_event(name, g, token)

    win0 = wts["w_in0"]
    wout0 = wts["w_out0"]
    win1 = wts["w_in1"]
    wout1 = wts["w_out1"]
    wup = wts["w_up"]
    wdown = wts["w_down"]
    w4, b4, w3, b3 = wts["w4"], wts["b4"], wts["w3"], wts["b3"]
    wa, wx = wts["wa"], wts["wx"]
    wat, wxt = jnp.swapaxes(wa, 1, 2), jnp.swapaxes(wx, 1, 2)
    ba, bx, lam = wts["ba"], wts["bx"], wts["lam"]
    fcw, fcb = wts["ffn_cw"], wts["ffn_cb"]
    sgu_w, sgu_wt = wts["sgu_w"], wts["sgu_wt"]
    sgu_bias, sgu_gn = wts["sgu_bias"], wts["sgu_gn"]
    bf = wts["bf"]

    lane = jnp.arange(LANES)
    seg = jnp.where((lane[:, None] // 64) == (lane[None, :] // 64), 1.0 / 64.0, 0.0).astype(BF16)
    sel = jnp.stack([jnp.broadcast_to((lane[:, None] < 64), (LANES, LANES)),
                     jnp.broadcast_to((lane[:, None] >= 64), (LANES, LANES))]).astype(BF16)
    tril = (lane[:, None] >= lane[None, :]).astype(F32)

    n0 = _norm_fwd(x, wts["g_mix0"], name="norm_mix0")
    p0 = _mm([n0], win0, nb=1280, name="mm_in0")
    ya, yb, hl = _even_core_fwd(p0, w4, b4, wa, ba, wx, bx, lam, w3, b3, name="even_fwd")
    h1, n1 = _mm([ya, yb], wout0, res=x, norm_out=wts["g_ffn"][0], name="mm_out0")

    def ffn_fwd(h, n, layer, next_gain):
        up = _mm([n], wup[layer], out_dtype=BF16, ts=1024, nb=1408, name=f"mm_up{layer}")
        act = _ffn_core_fwd(up, fcw[layer], fcb[layer], name=f"ffn_fwd{layer}")
        if next_gain is None:
            return up, act, _mm([act], wdown[layer], res=h, name=f"mm_down{layer}"), None
        hn, nn = _mm([act], wdown[layer], res=h, norm_out=next_gain, name=f"mm_down{layer}")
        return up, act, hn, nn

    up0, act0, h2, n2 = ffn_fwd(h1, n1, 0, wts["g_mix1"])

    p1 = _mm([n2], win1, name="mm_in1")
    yc = _sgu_fwd(p1, sgu_gn, sgu_w, sgu_bias, seg, name="sgu_fwd")
    cum = _fcum_fwd(p1, bf, name="fcum_fwd")
    c8 = cum[:, :8]
    cq = jnp.broadcast_to(c8[:, :, None], (s, 8, LANES)).reshape(s, 8 * LANES)
    ck = jnp.transpose(c8).reshape(8, 1, s)
    yd, lb = _fox_fwd(p1, cq, ck, name="fox_fwd")
    h3, n3 = _mm([yc, yd], wout1, res=h2, norm_out=wts["g_ffn"][1], name="mm_out1")

    up1, act1, h4, _ = ffn_fwd(h3, n3, 1, None)
    dh4, loss, g["final_norm"] = _final(h4, wts["g_final"], target, name="final")

    def ffn_bwd(dh, h, n, up, act, layer):
        dact = _mm([dh], wdown[layer], trans_w=True, out_dtype=BF16, ts=1024, nb=1408, name=f"mm_dact{layer}")
        g[f"w_down{layer}"] = _mm_tn([act], [dh], ts=1024, nb=512, name=f"mm_dwdown{layer}")
        event(f"dwdown{layer}", g[f"w_down{layer}"])
        dgate, dval, dcwg, dcwv, dcbg, dcbv = _ffn_core_bwd(dact, up, fcw[layer], fcb[layer], name=f"ffn_bwd{layer}")
        event(f"ffn_bwd{layer}", dgate)
        g[f"w_up{layer}"] = _mm_tn([n], [dgate, dval], ts=1024, nb=1408, name=f"mm_dwup{layer}")
        event(f"dwup{layer}", g[f"w_up{layer}"])
        dhn, g[f"g_ffn{layer}"] = _mm([dgate, dval], wup[layer], trans_w=True, ts=512,
                                      norm_bwd=(h, wts["g_ffn"][layer], dh), name=f"mm_dn_ffn{layer}")
        g[f"ffn_cw{layer}"] = jnp.concatenate([dcwg, dcwv], axis=1)
        g[f"ffn_cb{layer}"] = jnp.concatenate([dcbg, dcbv], axis=1)
        return dhn

    dh3 = ffn_bwd(dh4, h3, n3, up1, act1, 1)

    dy1 = _mm([dh3], wout1, trans_w=True, ts=1024, name="mm_dy1")
    g["w_out1"] = _mm_tn([yc, yd], [dh3], ts=1024, nb=512, name="mm_dwout1")
    event("dwout1", g["w_out1"])
    dzu, dzg, g["sgu_w"], g["sgu_bias"], g["sgu_gn"] = _sgu_bwd(
        p1, dy1, sgu_gn, sgu_w, sgu_wt, sgu_bias, seg, tril, name="sgu_bwd")
    delta = _fox_delta(dy1, yd, sel, name="fox_delta")
    dq, dk, dv, dck, dcq = _fox_bwd(p1, dy1, lb, delta, ck, name="fox_bwd")
    event("fox_bwd", dq)
    dcs = jnp.pad(jnp.transpose(dck.reshape(8, s)), ((0, 0), (0, LANES - 8)))
    df, g["bf"] = _fcum_bwd(dcs, dcq, p1, bf, name="fcum_bwd")
    dp1 = jnp.concatenate([dzu, dzg, dq, dk, dv, df], axis=1)
    g["w_in1"] = _mm_tn([n2], [dp1], ts=1024, nb=896, name="mm_dwin1")
    event("dwin1", g["w_in1"])
    dh2, g["g_mix1"] = _mm([dp1], win1, trans_w=True, norm_bwd=(h2, wts["g_mix1"], dh3), name="mm_dn_mix1")

    dh1 = ffn_bwd(dh2, h1, n1, up0, act0, 0)

    dy0 = _mm([dh1], wout0, trans_w=True, ts=1024, name="mm_dy0")
    g["w_out0"] = _mm_tn([ya, yb], [dh1], ts=1024, nb=512, name="mm_dwout0")
    event("dwout0", g["w_out0"])
    (*dp0, g["w4"], g["b4"], g["wa"], g["ba"], g["wx"], g["bx"], g["lam"], g["w3"], g["b3"]) = _even_core_bwd(
        dy0, p0, hl, w4, b4, wa, wat, ba, wx, wxt, bx, lam, w3, b3, name="even_bwd")
    event("even_bwd", dp0[0])
    g["w_in0"] = _mm_tn([n0], dp0, ts=1024, nb=512, name="mm_dwin0")
    event("dwin0", g["w_in0"])
    grad_x, g["g_mix0"] = _mm(dp0, win0, trans_w=True, norm_bwd=(x, wts["g_mix0"], dh1), name="mm_dn_mix0")
    return loss, grad_x, g


def _prepare_weights(nat):
    lane = jnp.arange(LANES)
    tril = (lane[:, None] >= lane[None, :]).astype(F32)
    sgu_tril = nat["sgu_w"][0] * tril
    w_in1 = nat["mix1_w_in"]
    return {
        "w_in0": nat["mix0_w_in"],
        "w_out0": nat["mix0_w_out"],
        "w_in1": jnp.pad(w_in1, ((0, 0), (0, 21 * LANES - w_in1.shape[1]))),
        "w_out1": nat["mix1_w_out"],
        "w_up": [nat["ffn_up"][l] for l in range(2)],
        "w_down": [nat["ffn_down"][l] for l in range(2)],
        "w4": nat["lru_conv_w"], "b4": nat["lru_conv_b"], "w3": nat["sconv_w"], "b3": nat["sconv_b"],
        "wa": _pair_blockdiag(nat["lru_wa"][0]).astype(BF16), "wx": _pair_blockdiag(nat["lru_wx"][0]).astype(BF16),
        "ba": nat["lru_ba"], "bx": nat["lru_bx"], "lam": nat["lru_lambda"],
        "ffn_cw": [nat["ffn_conv_w"][l] for l in range(2)],
        "ffn_cb": [nat["ffn_conv_b"][l:l + 1] for l in range(2)],
        "sgu_w": sgu_tril.astype(BF16), "sgu_wt": jnp.swapaxes(sgu_tril, 1, 2).astype(BF16),
        "sgu_bias": jnp.repeat(jnp.transpose(nat["sgu_b"][0]), 64, axis=1), "sgu_gn": nat["sgu_norm"],
        "bf": jnp.pad(nat["fox_bf"], ((0, 0), (0, LANES - 8))),
        "g_mix0": nat["mix0_norm"], "g_mix1": nat["mix1_norm"],
        "g_ffn": [nat["ffn_norm"][0:1], nat["ffn_norm"][1:2]], "g_final": nat["final_norm"].reshape(1, D_MODEL),
    }


def _natural_grads(g):
    small = {
        "mix0_norm": g["g_mix0"], "lru_conv_b": g["b4"],
        "lru_wa": _pair_diag_blocks(g["wa"])[None], "lru_ba": g["ba"],
        "lru_wx": _pair_diag_blocks(g["wx"])[None], "lru_bx": g["bx"],
        "lru_lambda": g["lam"], "sconv_b": g["b3"],
        "sgu_w": g["sgu_w"][None],
        "sgu_b": jnp.transpose(g["sgu_bias"].reshape(CHUNK, 8, 64).sum(axis=2))[None],
        "fox_bf": g["bf"][:, :8],
        "ffn_norm": jnp.concatenate([g["g_ffn0"], g["g_ffn1"]], axis=0),
        "ffn_conv_b": jnp.concatenate([g["ffn_cb0"], g["ffn_cb1"]], axis=0),
        "final_norm": g["final_norm"].reshape(D_MODEL),
        "lru_conv_w": g["w4"][None], "sconv_w": g["w3"][None],
        "ffn_conv_w": jnp.stack([g["ffn_cw0"], g["ffn_cw1"]]),
        "mix1_norm": g["g_mix1"], "sgu_norm": g["sgu_gn"],
    }
    big = {
        "mix0_w_in": g["w_in0"], "mix0_w_out": g["w_out0"],
        "mix1_w_in": g["w_in1"][:, :2568], "mix1_w_out": g["w_out1"],
        "ffn_up0": g["w_up0"], "ffn_up1": g["w_up1"],
        "ffn_down0": g["w_down0"], "ffn_down1": g["w_down1"],
    }
    return small, big


COL_SHARDED = ("mix0_w_in", "mix1_w_in", "ffn_up0", "ffn_up1")
COL_ALIGNED = ("mix0_w_in", "ffn_up0", "ffn_up1")
SMALL_SHARDED = ("lru_conv_w", "sconv_w", "ffn_conv_w", "mix1_norm", "sgu_norm")
SMALL_REPLICATED = ("mix0_norm", "lru_conv_b", "lru_wa", "lru_ba", "lru_wx", "lru_bx", "lru_lambda", "sconv_b",
                    "sgu_w", "sgu_b", "fox_bf", "ffn_norm", "ffn_conv_b", "final_norm")
WEIGHT_ORDER = ("mix0_norm", "mix0_w_in", "lru_conv_w", "lru_conv_b", "lru_wa", "lru_ba", "lru_wx", "lru_bx",
                "lru_lambda", "sconv_w", "sconv_b", "mix0_w_out", "mix1_norm", "mix1_w_in", "sgu_norm", "sgu_w",
                "sgu_b", "fox_bf", "mix1_w_out", "ffn_norm", "ffn_up", "ffn_conv_w", "ffn_conv_b", "ffn_down",
                "final_norm")


GATHER_GROUPS = (("mix0_w_in", "mix0_w_out"), ("ffn_up0",), ("ffn_down0", "mix1_w_in"),
                 ("mix1_w_out", "ffn_up1", "ffn_down1"))
CID_GATHER, CID_PAIR, CID_FIRST, CID_SECOND, CID_SWAP = 1, 2, 3, 4, 5


class _GradReducer:
    def __init__(self):
        x, y, c = _mesh_pos()
        self.send = jnp.stack([c] + [2 * (c * (1 - x) + (1 - c) * t) + (c * t + (1 - c) * (1 - y))
                                     for t in range(2)]).astype(jnp.int32)
        self.keep = jnp.stack([c] + [c * (2 * x + t) + (1 - c) * (2 * t + y) for t in range(2)]).astype(jnp.int32)
        self.mine = (c * y + (1 - c) * x).reshape(1).astype(jnp.int32)
        self.groups = {}

    @staticmethod
    def _view(name, a):
        if name in COL_ALIGNED:
            return a.reshape(2, a.shape[0] // 2, a.shape[1])
        if name in COL_SHARDED:
            a = _cols_to_shards(a)
            return a.reshape(N_CHIPS, 2, a.shape[1] // 2, a.shape[2])
        rows = a.shape[0] // (2 * N_CHIPS)
        return a.reshape(N_CHIPS, 2, rows, a.shape[1])

    def start(self, group, grads):
        names = tuple(grads)
        views = [self._view(k, grads[k]) for k in names]
        cols = [k in COL_ALIGNED for k in names]
        data = _send_other_half(views, cols, collective_id=CID_PAIR, name=f"rs_pair_{group}")
        self.groups[group] = dict(names=names, stage=0, views=views, cols=cols, data=data)

    def step(self, group, after):
        st = self.groups[group]
        names = st["names"]
        if st["stage"] == 0:
            sums = [_pair_sum(a, col, b, self.send, after, name=f"rs_pair_sum_{k}")
                    for k, a, col, b in zip(names, st["views"], st["cols"], st["data"])]
            st["from_sib"] = st["data"]
            st["data"] = _send_first(sums, collective_id=CID_FIRST, name=f"rs_first_{group}")
        elif st["stage"] == 1:
            sums = [_first_sum(a, col, b, r, self.keep, after, name=f"rs_first_sum_{k}")
                    for k, a, col, b, r in zip(names, st["views"], st["cols"], st["from_sib"], st["data"])]
            st["keep"] = [s32 for s32, _ in sums]
            st["data"] = _send_second([s16 for _, s16 in sums], collective_id=CID_SECOND, name=f"rs_second_{group}")
        else:
            st["mine"] = [_second_sum(s32, r, self.mine, after, name=f"rs_second_sum_{k}")
                          for k, s32, r in zip(names, st["keep"], st["data"])]
            st["data"] = _swap_halves(st["mine"], collective_id=CID_SWAP, name=f"rs_swap_{group}")
        st["stage"] += 1

    def result(self, group):
        st = self.groups[group]
        return {k: (a, b) for k, a, b in zip(st["names"], st["mine"], st["data"])}


def _train_step(x, target, w, m, v):
    x2 = x[0]
    t2 = target[0]
    chip = 2 * lax.axis_index("x") + lax.axis_index("y")
    core_arr = lax.axis_index("c").reshape(1).astype(jnp.int32)
    chip_arr = chip.reshape(1).astype(jnp.int32)

    big_shards = {
        "mix0_w_in": (w["mix0_w_in"], 0), "mix0_w_out": (w["mix0_w_out"], 0),
        "mix1_w_in": (w["mix1_w_in"], 0), "mix1_w_out": (w["mix1_w_out"], 0),
        "ffn_up0": (w["ffn_up"], 0), "ffn_up1": (w["ffn_up"], 1),
        "ffn_down0": (w["ffn_down"], 0), "ffn_down1": (w["ffn_down"], 1),
    }
    small_shards = [w[k] for k in SMALL_SHARDED]
    small_buf, small_offs = _pack(small_shards)
    full = {}
    small_all = None
    for gi, names in enumerate(GATHER_GROUPS):
        cols = [k in COL_ALIGNED for k in names]
        placed = [_place(*big_shards[k], col, chip_arr, BF16, name=f"place_{k}") for k, col in zip(names, cols)]
        if gi == 0:
            placed.append(_place(small_buf[None], 0, False, chip_arr, F32, name="place_small"))
            cols = cols + [False]
        gathered = _all_gather(placed, cols, collective_id=CID_GATHER, name=f"gather_weights{gi}")
        if gi == 0:
            small_all = gathered[-1].reshape(N_CHIPS, -1, LANES)
        for k, arr in zip(names, gathered):
            if k in COL_ALIGNED:
                full[k] = arr.reshape(arr.shape[0] * arr.shape[1], arr.shape[2])
            elif k in COL_SHARDED:
                full[k] = _cols_from_shards(arr.reshape((N_CHIPS, arr.shape[1] * arr.shape[2], arr.shape[3])))
            else:
                full[k] = arr.reshape(-1, arr.shape[3])
    per_chip = [_unpack(small_all[k], small_offs, [a.shape for a in small_shards]) for k in range(N_CHIPS)]
    lru_conv_w = jnp.concatenate([per_chip[k][0] for k in range(N_CHIPS)], axis=-1)[0]
    sconv_w = jnp.concatenate([per_chip[k][1] for k in range(N_CHIPS)], axis=-1)[0]
    ffn_conv_w = jnp.concatenate([per_chip[k][2] for k in range(N_CHIPS)], axis=-1)
    mix1_norm = jnp.concatenate([per_chip[k][3] for k in range(N_CHIPS)], axis=-1)
    sgu_norm = jnp.concatenate([per_chip[k][4] for k in range(N_CHIPS)], axis=-1)

    nat = {
        "mix0_w_in": full["mix0_w_in"], "mix0_w_out": full["mix0_w_out"],
        "mix1_w_in": full["mix1_w_in"], "mix1_w_out": full["mix1_w_out"],
        "ffn_up": [full["ffn_up0"], full["ffn_up1"]], "ffn_down": [full["ffn_down0"], full["ffn_down1"]],
        "lru_conv_w": lru_conv_w, "sconv_w": sconv_w, "ffn_conv_w": ffn_conv_w, "mix1_norm": mix1_norm,
        "sgu_norm": sgu_norm,
    }
    for k in SMALL_REPLICATED:
        nat[k] = w[k]
    wts = _prepare_weights(nat)

    reducer = _GradReducer()

    def on_event(name, g, token):
        if name == "dwup1":
            reducer.start("ffn1", {"ffn_up1": g["w_up1"], "ffn_down1": g["w_down1"]})
        elif name in ("dwout1", "fox_bwd"):
            reducer.step("ffn1", token)
        elif name == "dwin1":
            reducer.step("ffn1", token)
            reducer.start("mix1", {"mix1_w_in": g["w_in1"][:, :2568], "mix1_w_out": g["w_out1"]})
        elif name in ("dwdown0", "ffn_bwd0"):
            reducer.step("mix1", token)
        elif name == "dwup0":
            reducer.step("mix1", token)
            reducer.start("ffn0", {"ffn_up0": g["w_up0"], "ffn_down0": g["w_down0"]})
        elif name in ("dwout0", "even_bwd"):
            reducer.step("ffn0", token)
        elif name == "dwin0":
            reducer.step("ffn0", token)
            reducer.start("mix0", {"mix0_w_in": g["w_in0"], "mix0_w_out": g["w_out0"]})

    loss, grad_x, g = _local_step(x2, t2, wts, on_event)
    grads_small, _ = _natural_grads(g)

    small_names = SMALL_REPLICATED + SMALL_SHARDED
    small_list = [grads_small[k] for k in small_names] + [loss[:, :1]]
    sbuf, soffs = _pack(small_list)
    sred = _all_reduce_small(sbuf, name="reduce_small")
    small_red = _unpack(sred, soffs, [a.shape for a in small_list])
    loss_total = small_red[-1][0, 0]
    gsum = dict(zip(small_names, small_red[:-1]))
    for k in SMALL_SHARDED:
        width = w[k].shape[-1]
        gsum[k] = lax.dynamic_slice_in_dim(gsum[k], chip * width, width, axis=gsum[k].ndim - 1)

    out_g, out_d, out_m, out_v = {}, {}, {}, {}
    reduced = {}
    for group in ("ffn1", "mix1", "ffn0"):
        reduced.update(reducer.result(group))

    def update(pname, keys):
        mine = [reduced[k][0] for k in keys]
        theirs = [reduced[k][1] for k in keys]
        out_g[pname], out_d[pname], out_m[pname], out_v[pname] = _adamw_halves(
            w[pname], mine, theirs, m[pname], v[pname], core_arr, name=f"adamw_{pname}")
        return out_d[pname]

    reducer.step("mix0", update("ffn_up", ("ffn_up0", "ffn_up1")))
    small_new = _adamw_many([w[k] for k in small_names], [gsum[k] for k in small_names],
                            [m[k] for k in small_names], [v[k] for k in small_names], name="adamw_small")
    reducer.step("mix0", update("ffn_down", ("ffn_down0", "ffn_down1")))
    update("mix1_w_in", ("mix1_w_in",))
    reducer.step("mix0", update("mix1_w_out", ("mix1_w_out",)))
    reduced.update(reducer.result("mix0"))
    update("mix0_w_in", ("mix0_w_in",))
    update("mix0_w_out", ("mix0_w_out",))

    for k, (dd, mm, vv) in zip(small_names, small_new):
        out_g[k], out_d[k], out_m[k], out_v[k] = gsum[k].reshape(w[k].shape), dd, mm, vv

    outs = [loss_total, grad_x[None]]
    for d in (out_g, out_d, out_m, out_v):
        outs.extend(d[k] for k in WEIGHT_ORDER)
    return tuple(outs)


def kernel(x, mix0_norm, mix0_w_in, lru_conv_w, lru_conv_b, lru_wa, lru_ba, lru_wx, lru_bx, lru_lambda, sconv_w, sconv_b, mix0_w_out, mix1_norm, mix1_w_in, sgu_norm, sgu_w, sgu_b, fox_bf, mix1_w_out, ffn_norm, ffn_up, ffn_conv_w, ffn_conv_b, ffn_down, final_norm, loss_target, m_mix0_norm, m_mix0_w_in, m_lru_conv_w, m_lru_conv_b, m_lru_wa, m_lru_ba, m_lru_wx, m_lru_bx, m_lru_lambda, m_sconv_w, m_sconv_b, m_mix0_w_out, m_mix1_norm, m_mix1_w_in, m_sgu_norm, m_sgu_w, m_sgu_b, m_fox_bf, m_mix1_w_out, m_ffn_norm, m_ffn_up, m_ffn_conv_w, m_ffn_conv_b, m_ffn_down, m_final_norm, v_mix0_norm, v_mix0_w_in, v_lru_conv_w, v_lru_conv_b, v_lru_wa, v_lru_ba, v_lru_wx, v_lru_bx, v_lru_lambda, v_sconv_w, v_sconv_b, v_mix0_w_out, v_mix1_norm, v_mix1_w_in, v_sgu_norm, v_sgu_w, v_sgu_b, v_fox_bf, v_mix1_w_out, v_ffn_norm, v_ffn_up, v_ffn_conv_w, v_ffn_conv_b, v_ffn_down, v_final_norm):
    w = dict(zip(WEIGHT_ORDER, (mix0_norm, mix0_w_in, lru_conv_w, lru_conv_b, lru_wa, lru_ba, lru_wx, lru_bx, lru_lambda, sconv_w, sconv_b, mix0_w_out, mix1_norm, mix1_w_in, sgu_norm, sgu_w, sgu_b, fox_bf, mix1_w_out, ffn_norm, ffn_up, ffn_conv_w, ffn_conv_b, ffn_down, final_norm)))
    m = dict(zip(WEIGHT_ORDER, (m_mix0_norm, m_mix0_w_in, m_lru_conv_w, m_lru_conv_b, m_lru_wa, m_lru_ba, m_lru_wx, m_lru_bx, m_lru_lambda, m_sconv_w, m_sconv_b, m_mix0_w_out, m_mix1_norm, m_mix1_w_in, m_sgu_norm, m_sgu_w, m_sgu_b, m_fox_bf, m_mix1_w_out, m_ffn_norm, m_ffn_up, m_ffn_conv_w, m_ffn_conv_b, m_ffn_down, m_final_norm)))
    v = dict(zip(WEIGHT_ORDER, (v_mix0_norm, v_mix0_w_in, v_lru_conv_w, v_lru_conv_b, v_lru_wa, v_lru_ba, v_lru_wx, v_lru_bx, v_lru_lambda, v_sconv_w, v_sconv_b, v_mix0_w_out, v_mix1_norm, v_mix1_w_in, v_sgu_norm, v_sgu_w, v_sgu_b, v_fox_bf, v_mix1_w_out, v_ffn_norm, v_ffn_up, v_ffn_conv_w, v_ffn_conv_b, v_ffn_down, v_final_norm)))
    return _train_step(x, loss_target, w, m, v)
```

```python
import functools

import jax
import jax.numpy as jnp
from jax import lax
from jax.experimental import pallas as pl
from jax.experimental.pallas import tpu as pltpu
from jax.experimental.pallas import tpu_sc as plsc

F32 = jnp.float32
BF16 = jnp.bfloat16
MESH = pl.DeviceIdType.MESH

D_MODEL = 1024
LANES = 128
SUBLANES = 8
N_CHIPS = 4
EPS = 1e-6
LRU_C = 8.0
D_FF = 2816
FFN_CB = 256
CHUNK = 128
NEG = -1e30

ADAM_LR = 0.001
ADAM_B1 = 0.9
ADAM_B2 = 0.999
ADAM_EPS = 1e-08
ADAM_WD = 0.01
ADAM_STEP = 10
ADAM_C1 = 1.0 - ADAM_B1 ** ADAM_STEP
ADAM_C2 = 1.0 - ADAM_B2 ** ADAM_STEP

_GELU_C = 0.7978845608028654
_GELU_A = 0.044715


def _sigmoid(x):
    return 1.0 / (1.0 + jnp.exp(-x))


def _sigmoid_tanh(x):
    return 0.5 * jnp.tanh(0.5 * x) + 0.5


def _log1p_pos(e):
    w = 1.0 + e
    return jnp.where(w == 1.0, e, jnp.log(w) * (e / (w - 1.0)))


def _softplus(x):
    return jnp.maximum(x, 0.0) + _log1p_pos(jnp.exp(-jnp.abs(x)))


def _gelu(x):
    t = jnp.tanh(_GELU_C * (x + _GELU_A * (x * x * x)))
    return 0.5 * x * (1.0 + t), t


def _gelu_grad(x, t):
    return 0.5 * (1.0 + t) + 0.5 * x * (1.0 - t * t) * (_GELU_C * (1.0 + 3.0 * _GELU_A * x * x))


def _rows(shape):
    return lax.broadcasted_iota(jnp.int32, shape, 0)


def _lanes(shape):
    return lax.broadcasted_iota(jnp.int32, shape, 1)


def _shift_down(x, halo8, j):
    if j == 0:
        return x
    r = pltpu.roll(x, j, 0)
    hr = pltpu.roll(halo8, j, 0)
    top = jnp.where(_rows(hr.shape) < j, hr, r[:SUBLANES])
    return jnp.concatenate([top, r[SUBLANES:]], axis=0)


def _shift_up(x, next8, j):
    if j == 0:
        return x
    n = x.shape[0]
    r = pltpu.roll(x, n - j, 0)
    nr = pltpu.roll(next8, SUBLANES - j, 0)
    bot = jnp.where(_rows(nr.shape) >= SUBLANES - j, nr, r[n - SUBLANES:])
    return jnp.concatenate([r[:n - SUBLANES], bot], axis=0)


def _scan_fwd(a, u):
    n = a.shape[0]
    row = _rows(a.shape)
    h = u
    k = 1
    while k < n:
        keep = row >= k
        h_sh = jnp.where(keep, pltpu.roll(h, k, 0), 0.0)
        a_sh = jnp.where(keep, pltpu.roll(a, k, 0), 1.0)
        h = a * h_sh + h
        a = a * a_sh
        k *= 2
    return h, a


def _scan_rev(b, d):
    n = b.shape[0]
    row = _rows(b.shape)
    g = d
    k = 1
    while k < n:
        keep = row < n - k
        g_sh = jnp.where(keep, pltpu.roll(g, n - k, 0), 0.0)
        b_sh = jnp.where(keep, pltpu.roll(b, n - k, 0), 1.0)
        g = b * g_sh + g
        b = b * b_sh
        k *= 2
    return g, b


def _cumsum_fwd(x):
    n = x.shape[0]
    row = _rows(x.shape)
    k = 1
    while k < n:
        x = x + jnp.where(row >= k, pltpu.roll(x, k, 0), 0.0)
        k *= 2
    return x


def _cumsum_rev(x):
    n = x.shape[0]
    row = _rows(x.shape)
    k = 1
    while k < n:
        x = x + jnp.where(row < n - k, pltpu.roll(x, n - k, 0), 0.0)
        k *= 2
    return x


def _dot(a, b):
    return lax.dot_general(a, b, (((1,), (0,)), ((), ())), preferred_element_type=F32)


def _dot_nt(a, b):
    return lax.dot_general(a, b, (((1,), (1,)), ((), ())), preferred_element_type=F32)


def _dot_tn(a, b):
    return lax.dot_general(a, b, (((0,), (0,)), ((), ())), preferred_element_type=F32)


def _dot_split(x, m_bf16):
    hi = x.astype(BF16)
    lo = (x - hi.astype(F32)).astype(BF16)
    return _dot(hi, m_bf16) + _dot(lo, m_bf16)


def _tile_rows(ts, s):
    return min(ts, s)


def _mm(a_list, w, *, trans_w=False, res=None, norm_bwd=None, norm_out=None, out_dtype=F32, ts=512, nb=None,
        name):
    s = a_list[0].shape[0]
    ks = [a.shape[1] for a in a_list]
    k = sum(ks)
    n = w.shape[0] if trans_w else w.shape[1]
    ts = _tile_rows(ts, s)
    nb = n if nb is None else nb
    na = len(a_list)
    has_res = res is not None
    fused = norm_bwd is not None
    normed = norm_out is not None
    offs = [sum(ks[:p]) for p in range(na)]

    def body(*refs):
        a_refs = refs[:na]
        w_ref = refs[na]
        acc = None
        for a_ref, off, kk in zip(a_refs, offs, ks):
            a = a_ref[...].astype(BF16)
            if trans_w:
                part = _dot_nt(a, w_ref[:, off:off + kk])
            else:
                part = _dot(a, w_ref[off:off + kk, :])
            acc = part if acc is None else acc + part
        if has_res:
            acc = acc + refs[na + 1][...]
        if normed:
            gn_ref, o_ref, n_ref = refs[-3:]
            o_ref[...] = acc.astype(out_dtype)
            r = lax.rsqrt(jnp.mean(acc * acc, axis=-1, keepdims=True) + EPS)
            n_ref[...] = ((acc * r) * gn_ref[...]).astype(BF16)
            return
        if not fused:
            refs[-1][...] = acc.astype(out_dtype)
            return
        h_ref, g_ref, dres_ref, dh_ref, dg_ref = refs[na + 1:]
        i = pl.program_id(1)
        x = h_ref[...]
        r = lax.rsqrt(jnp.mean(x * x, axis=-1, keepdims=True) + EPS)
        xhat = x * r
        part = jnp.sum(acc * xhat, axis=0, keepdims=True)

        @pl.when(i == 0)
        def _():
            dg_ref[...] = part

        @pl.when(i > 0)
        def _():
            dg_ref[...] += part

        dxh = acc * g_ref[...]
        dh_ref[...] = dres_ref[...] + r * (dxh - xhat * jnp.mean(dxh * xhat, axis=-1, keepdims=True))

    in_specs = [pl.BlockSpec((ts, kk), lambda j, i: (i, 0)) for kk in ks]
    if trans_w:
        in_specs.append(pl.BlockSpec((nb, k), lambda j, i: (j, 0)))
    else:
        in_specs.append(pl.BlockSpec((k, nb), lambda j, i: (0, j)))
    args = list(a_list) + [w]
    tile = pl.BlockSpec((ts, nb), lambda j, i: (i, j))
    if has_res:
        in_specs.append(tile)
        args.append(res)
    if fused:
        assert nb == n and not has_res
        vec = pl.BlockSpec((1, n), lambda j, i: (0, 0))
        h, g, dres = norm_bwd
        return pl.pallas_call(
            body, name=name, grid=(1, s // ts), in_specs=in_specs + [tile, vec, tile],
            out_specs=(tile, vec),
            out_shape=(jax.ShapeDtypeStruct((s, n), F32), jax.ShapeDtypeStruct((1, n), F32)),
        )(*args, h, g, dres)
    if normed:
        assert nb == n and out_dtype == F32
        vec = pl.BlockSpec((1, n), lambda j, i: (0, 0))
        return pl.pallas_call(
            body, name=name, grid=(1, s // ts), in_specs=in_specs + [vec], out_specs=(tile, tile),
            out_shape=(jax.ShapeDtypeStruct((s, n), F32), jax.ShapeDtypeStruct((s, n), BF16)),
        )(*args, norm_out)
    return pl.pallas_call(
        body, name=name, grid=(n // nb, s // ts), in_specs=in_specs, out_specs=tile,
        out_shape=jax.ShapeDtypeStruct((s, n), out_dtype),
    )(*args)


def _mm_tn(a_list, b_list, *, ts=512, nb=None, name):
    s = b_list[0].shape[0]
    ks = [a.shape[1] for a in a_list]
    k = sum(ks)
    width = b_list[0].shape[1]
    n = width * len(b_list)
    ts = _tile_rows(ts, s)
    nb = width if nb is None else nb
    per = width // nb
    na = len(a_list)
    nparts = len(b_list)

    def body(*refs):
        a_refs = refs[:na]
        b_refs = refs[na:na + nparts]
        o_ref = refs[-1]
        j = pl.program_id(0)
        i = pl.program_id(1)
        parts = [r[...].astype(BF16) for r in a_refs]
        a = parts[0] if na == 1 else jnp.concatenate(parts, axis=1)

        def accumulate(b_ref):
            upd = _dot_tn(a, b_ref[...].astype(BF16))

            @pl.when(i == 0)
            def _():
                o_ref[...] = upd

            @pl.when(i > 0)
            def _():
                o_ref[...] += upd

        if nparts == 1:
            accumulate(b_refs[0])
        else:
            for part, b_ref in enumerate(b_refs):
                pl.when(j // per == part)(functools.partial(accumulate, b_ref))

    in_specs = [pl.BlockSpec((ts, kk), lambda j, i: (i, 0)) for kk in ks]
    for part in range(nparts):
        in_specs.append(pl.BlockSpec(
            (ts, nb), lambda j, i, part=part: (i, jnp.clip(j - part * per, 0, per - 1))))
    return pl.pallas_call(
        body, name=name, grid=(n // nb, s // ts), in_specs=in_specs,
        out_specs=pl.BlockSpec((k, nb), lambda j, i: (0, j)),
        out_shape=jax.ShapeDtypeStruct((k, n), F32),
    )(*a_list, *b_list)


def _norm_fwd(h, g, *, ts=512, name):
    s, d = h.shape
    ts = _tile_rows(ts, s)

    def body(h_ref, g_ref, n_ref):
        x = h_ref[...]
        r = lax.rsqrt(jnp.mean(x * x, axis=-1, keepdims=True) + EPS)
        n_ref[...] = ((x * r) * g_ref[...]).astype(BF16)

    return pl.pallas_call(
        body, name=name, grid=(s // ts,),
        in_specs=[pl.BlockSpec((ts, d), lambda i: (i, 0)), pl.BlockSpec((1, d), lambda i: (0, 0))],
        out_specs=pl.BlockSpec((ts, d), lambda i: (i, 0)),
        out_shape=jax.ShapeDtypeStruct((s, d), BF16),
    )(h, g)


def _final(h, g, target, *, ts=512, name):
    s, d = h.shape
    ts = _tile_rows(ts, s)
    nt = s // ts

    def body(h_ref, g_ref, t_ref, dh_ref, loss_ref, dg_ref, acc_ref):
        i = pl.program_id(0)
        x = h_ref[...]
        r = lax.rsqrt(jnp.mean(x * x, axis=-1, keepdims=True) + EPS)
        xhat = x * r
        gv = g_ref[...]
        err = xhat * gv - t_ref[...]
        sq = jnp.sum(err * err, axis=0, keepdims=True)
        dy = err * (1.0 / d)
        part = jnp.sum(dy * xhat, axis=0, keepdims=True)

        @pl.when(i == 0)
        def _():
            acc_ref[...] = sq
            dg_ref[...] = part

        @pl.when(i > 0)
        def _():
            acc_ref[...] += sq
            dg_ref[...] += part

        dxh = dy * gv
        dh_ref[...] = r * (dxh - xhat * jnp.mean(dxh * xhat, axis=-1, keepdims=True))

        @pl.when(i == nt - 1)
        def _():
            tot = jnp.sum(acc_ref[...], axis=1, keepdims=True) * (0.5 / d)
            loss_ref[...] = jnp.broadcast_to(tot, (1, LANES))

    tile = pl.BlockSpec((ts, d), lambda i: (i, 0))
    vec = pl.BlockSpec((1, d), lambda i: (0, 0))
    return pl.pallas_call(
        body, name=name, grid=(nt,), in_specs=[tile, vec, tile],
        out_specs=(tile, pl.BlockSpec((1, LANES), lambda i: (0, 0)), vec),
        out_shape=(jax.ShapeDtypeStruct((s, d), F32), jax.ShapeDtypeStruct((1, LANES), F32),
                   jax.ShapeDtypeStruct((1, d), F32)),
        scratch_shapes=[pltpu.VMEM((1, d), F32)],
    )(h, g, target)


def _halo_map(ts, width_blocks):
    per = ts // SUBLANES

    def index(j, i):
        return (jnp.maximum(i * per - 1, 0), width_blocks(j))

    return index


def _even_gates(xc, wa, ba, wx, bx, sp):
    xb = xc.astype(BF16)
    r = _sigmoid(_dot(xb, wa) + ba)
    ig = _sigmoid(_dot(xb, wx) + bx)
    la = (-LRU_C) * r * sp
    a = jnp.exp(la)
    a2 = a * a
    m = jnp.sqrt(-jnp.tanh(la) * (1.0 + a2))
    return r, ig, la, a, a2, m


def _even_core_fwd(p, w4, b4, wa, ba, wx, bx, lam, w3, b3, *, ts=512, name):
    s = p.shape[0]
    ts = _tile_rows(ts, s)
    nt = s // ts
    nblk = 4

    def body(xa_ref, ga_ref, cp_ref, bp_ref, vb_ref, xah_ref, cph_ref, vbh_ref,
             w4_ref, b4_ref, wa_ref, ba_ref, wx_ref, bx_ref, lam_ref, w3_ref, b3_ref,
             ya_ref, yb_ref, hl_ref, hcar_ref):
        i = pl.program_id(1)
        first = (i > 0).astype(F32)
        xa, ga, cp, bp, vb = xa_ref[...], ga_ref[...], cp_ref[...], bp_ref[...], vb_ref[...]
        xa_h = xah_ref[...] * first
        s_h = cph_ref[...] * vbh_ref[...] * first

        xc = b4_ref[...] + w4_ref[3:4, :] * xa
        for k in range(3):
            xc = xc + w4_ref[k:k + 1, :] * _shift_down(xa, xa_h, 3 - k)
        sp = _softplus(-lam_ref[...])
        _, ig, _, a, _, m = _even_gates(xc, wa_ref[0], ba_ref[...], wx_ref[0], bx_ref[...], sp)
        u = m * (ig * xc)
        hs, acum = _scan_fwd(a, u)

        @pl.when(i == 0)
        def _():
            hcar_ref[...] = jnp.zeros_like(hcar_ref)

        hs = hs + acum * hcar_ref[0:1, :]
        hl_ref[...] = hs
        hcar_ref[0:1, :] = hl_ref[ts - 1:ts, :]
        ge, _ = _gelu(ga)
        ya_ref[...] = (hs * ge).astype(BF16)

        sv = cp * vb
        sc = b3_ref[...] + w3_ref[2:3, :] * sv
        for k in range(2):
            sc = sc + w3_ref[k:k + 1, :] * _shift_down(sv, s_h, 2 - k)
        yb_ref[...] = (bp * sc).astype(BF16)

    parts = [pl.BlockSpec((ts, LANES), lambda j, i, q=q: (i, 4 * q + j)) for q in range(5)]
    halos = [pl.BlockSpec((SUBLANES, LANES), _halo_map(ts, lambda j, q=q: 4 * q + j)) for q in (0, 2, 4)]
    vec = pl.BlockSpec((1, LANES), lambda j, i: (0, j))
    out = pl.BlockSpec((ts, LANES), lambda j, i: (i, j))
    return pl.pallas_call(
        body, name=name, grid=(nblk, nt),
        in_specs=parts + halos + [
                  pl.BlockSpec((4, LANES), lambda j, i: (0, j)), vec,
                  pl.BlockSpec((1, LANES, LANES), lambda j, i: (j, 0, 0)), vec,
                  pl.BlockSpec((1, LANES, LANES), lambda j, i: (j, 0, 0)), vec, vec,
                  pl.BlockSpec((3, LANES), lambda j, i: (0, j)), vec],
        out_specs=(out, out, out),
        out_shape=(jax.ShapeDtypeStruct((s, 4 * LANES), BF16), jax.ShapeDtypeStruct((s, 4 * LANES), BF16),
                   jax.ShapeDtypeStruct((s, 4 * LANES), F32)),
        scratch_shapes=[pltpu.VMEM((SUBLANES, LANES), F32)],
    )(*([p] * 8), w4, b4, wa, ba, wx, bx, lam, w3, b3)


def _even_core_bwd(dy, p, hl, w4, b4, wa, wat, ba, wx, wxt, bx, lam, w3, b3, *, ts=512, name):
    s = p.shape[0]
    ts = _tile_rows(ts, s)
    nt = s // ts
    nblk = 4
    per = ts // SUBLANES

    def body(dya_ref, dyb_ref, xa_ref, ga_ref, cp_ref, bp_ref, vb_ref, xah_ref, cph_ref, vbh_ref, hl_ref, hh_ref,
             w4_ref, b4_ref, wa_ref, wat_ref, ba_ref, wx_ref, wxt_ref, bx_ref, lam_ref, w3_ref, b3_ref,
             dxa_ref, dga_ref, dcp_ref, dbp_ref, dvb_ref,
             dw4_ref, db4_ref, dwa_ref, dba_ref, dwx_ref, dbx_ref, dlam_ref, dw3_ref, db3_ref,
             dxc_nx, dsc_nx, cg_ref):
        i = pl.program_id(1)
        ti = nt - 1 - i
        first = (ti > 0).astype(F32)
        xa, ga, cp, bp, vb = xa_ref[...], ga_ref[...], cp_ref[...], bp_ref[...], vb_ref[...]
        xa_h = xah_ref[...] * first
        s_h = cph_ref[...] * vbh_ref[...] * first
        h_h = hh_ref[...] * first

        @pl.when(i == 0)
        def _():
            dxc_nx[...] = jnp.zeros_like(dxc_nx)
            dsc_nx[...] = jnp.zeros_like(dsc_nx)
            cg_ref[...] = jnp.zeros_like(cg_ref)
            for ref in (dw4_ref, db4_ref, dwa_ref, dba_ref, dwx_ref, dbx_ref, dlam_ref, dw3_ref, db3_ref):
                ref[...] = jnp.zeros_like(ref)

        xa_sh = [_shift_down(xa, xa_h, 3 - k) for k in range(3)] + [xa]
        xc = b4_ref[...]
        for k in range(4):
            xc = xc + w4_ref[k:k + 1, :] * xa_sh[k]
        lamv = lam_ref[...]
        sp = _softplus(-lamv)
        r, ig, _, a, a2, m = _even_gates(xc, wa_ref[0], ba_ref[...], wx_ref[0], bx_ref[...], sp)
        sv = cp * vb
        sv_sh = [_shift_down(sv, s_h, 2 - k) for k in range(2)] + [sv]
        sc = b3_ref[...]
        for k in range(3):
            sc = sc + w3_ref[k:k + 1, :] * sv_sh[k]
        hs = hl_ref[...]
        h_prev = _shift_down(hs, h_h, 1)

        dya = dya_ref[...]
        dyb = dyb_ref[...]
        ge, gt = _gelu(ga)
        dga = dya * hs * _gelu_grad(ga, gt)
        dh = dya * ge

        ones8 = jnp.ones((SUBLANES, LANES), F32)
        b = _shift_up(a, ones8, 1)
        g, bcum = _scan_rev(b, dh)
        g = g + bcum * cg_ref[0:1, :]
        ag = a * g
        cg_ref[...] = ag[:SUBLANES]

        da = g * h_prev
        xi = ig * xc
        dm = g * xi
        dig = g * m * xc
        dxc = g * m * ig
        dla = da * a - dm * (a2 / m)
        dr = dla * ((-LRU_C) * sp)
        dlam_ref[...] += jnp.sum(dla * r, axis=0, keepdims=True) * (LRU_C * _sigmoid(-lamv))
        dra = dr * r * (1.0 - r)
        dia = dig * ig * (1.0 - ig)
        drab = dra.astype(BF16)
        diab = dia.astype(BF16)
        xcb = xc.astype(BF16)
        dxc = dxc + _dot(drab, wat_ref[0]) + _dot(diab, wxt_ref[0])
        dwa_ref[0] += _dot_tn(xcb, drab)
        dwx_ref[0] += _dot_tn(xcb, diab)
        dba_ref[...] += jnp.sum(dra, axis=0, keepdims=True)
        dbx_ref[...] += jnp.sum(dia, axis=0, keepdims=True)

        nx = dxc_nx[...]
        dxa = w4_ref[3:4, :] * dxc
        for k in range(3):
            dxa = dxa + w4_ref[k:k + 1, :] * _shift_up(dxc, nx, 3 - k)
        for k in range(4):
            dw4_ref[k:k + 1, :] += jnp.sum(dxc * xa_sh[k], axis=0, keepdims=True)
        db4_ref[...] += jnp.sum(dxc, axis=0, keepdims=True)
        dxc_nx[...] = dxc[:SUBLANES]

        dbp = dyb * sc
        dsc = dyb * bp
        nsc = dsc_nx[...]
        ds = w3_ref[2:3, :] * dsc
        for k in range(2):
            ds = ds + w3_ref[k:k + 1, :] * _shift_up(dsc, nsc, 2 - k)
        for k in range(3):
            dw3_ref[k:k + 1, :] += jnp.sum(dsc * sv_sh[k], axis=0, keepdims=True)
        db3_ref[...] += jnp.sum(dsc, axis=0, keepdims=True)
        dsc_nx[...] = dsc[:SUBLANES]

        dxa_ref[...] = dxa.astype(BF16)
        dga_ref[...] = dga.astype(BF16)
        dcp_ref[...] = (ds * vb).astype(BF16)
        dbp_ref[...] = dbp.astype(BF16)
        dvb_ref[...] = (ds * cp).astype(BF16)

    def rev(j, i):
        return (nt - 1 - i, j)

    def rev_halo(col):
        def index(j, i):
            return (jnp.maximum((nt - 1 - i) * per - 1, 0), col(j))
        return index

    parts = [pl.BlockSpec((ts, LANES), lambda j, i, q=q: (nt - 1 - i, 4 * q + j)) for q in range(5)]
    halos = [pl.BlockSpec((SUBLANES, LANES), rev_halo(lambda j, q=q: 4 * q + j)) for q in (0, 2, 4)]
    one = pl.BlockSpec((ts, LANES), rev)
    vec = pl.BlockSpec((1, LANES), lambda j, i: (0, j))
    mat = pl.BlockSpec((1, LANES, LANES), lambda j, i: (j, 0, 0))
    w4s = pl.BlockSpec((4, LANES), lambda j, i: (0, j))
    w3s = pl.BlockSpec((3, LANES), lambda j, i: (0, j))
    f = jax.ShapeDtypeStruct
    return pl.pallas_call(
        body, name=name, grid=(nblk, nt),
        in_specs=[one, pl.BlockSpec((ts, LANES), lambda j, i: (nt - 1 - i, 4 + j))] + parts + halos + [
                  one, pl.BlockSpec((SUBLANES, LANES), rev_halo(lambda j: j)),
                  w4s, vec, mat, mat, vec, mat, mat, vec, vec, w3s, vec],
        out_specs=(one,) * 5 + (w4s, vec, mat, vec, mat, vec, vec, w3s, vec),
        out_shape=(f((s, 4 * LANES), BF16),) * 5 + (
                   f((4, 4 * LANES), F32), f((1, 4 * LANES), F32),
                   f((4, LANES, LANES), F32), f((1, 4 * LANES), F32),
                   f((4, LANES, LANES), F32), f((1, 4 * LANES), F32), f((1, 4 * LANES), F32),
                   f((3, 4 * LANES), F32), f((1, 4 * LANES), F32)),
        scratch_shapes=[pltpu.VMEM((SUBLANES, LANES), F32), pltpu.VMEM((SUBLANES, LANES), F32),
                        pltpu.VMEM((SUBLANES, LANES), F32)],
    )(dy, dy, *([p] * 8), hl, hl, w4, b4, wa, wat, ba, wx, wxt, bx, lam, w3, b3)


def _ffn_conv(u_ref, uh_ref, w_ref, b_ref, first):
    u = u_ref[...].astype(F32)
    u_h = uh_ref[...].astype(F32)[SUBLANES:] * first
    u_sh = [_shift_down(u, u_h, 2 - k) for k in range(2)] + [u]
    hc = b_ref[...]
    for k in range(3):
        hc = hc + w_ref[k:k + 1, :] * u_sh[k]
    return hc, u_sh


def _ffn_specs(ts, row, halo_row):
    nblk = D_FF // FFN_CB
    specs = []
    for off in (0, nblk):
        specs.append(pl.BlockSpec((ts, FFN_CB), lambda j, i, off=off: (row(i), off + j)))
        specs.append(pl.BlockSpec((16, FFN_CB), lambda j, i, off=off: (halo_row(i), off + j)))
        specs.append(pl.BlockSpec((3, FFN_CB), lambda j, i, off=off: (0, off + j)))
        specs.append(pl.BlockSpec((1, FFN_CB), lambda j, i, off=off: (0, off + j)))
    return specs


FFN_STRIP = 4 * SUBLANES
FFN_HALO = 2 * SUBLANES


def _ffn_stage(u_ref, uh_ref, dst_ref, first):
    dst_ref[0:FFN_HALO, :] = jnp.where(first, uh_ref[...], jnp.zeros_like(uh_ref))
    dst_ref[FFN_HALO:, :] = u_ref[...]


def _ffn_strip_conv(u_ref, r, w, b):
    win = u_ref[pl.ds(r, FFN_HALO + FFN_STRIP), :].astype(F32)
    cur, before = win[FFN_HALO:], win[SUBLANES:FFN_HALO]
    sh = [_shift_down(cur, before, 2 - k) for k in range(2)] + [cur]
    return b + w[0:1] * sh[0] + w[1:2] * sh[1] + w[2:3] * sh[2], sh


def _ffn_core_fwd(up, w, b, *, ts=512, name):
    s = up.shape[0]
    ts = _tile_rows(ts, s)
    nt = s // ts
    nblk = D_FF // FFN_CB
    per = ts // 16

    def body(g_ref, gh_ref, wg_ref, bg_ref, v_ref, vh_ref, wv_ref, bv_ref, act_ref):
        first = (pl.program_id(1) > 0).astype(F32)
        gate, _ = _ffn_conv(g_ref, gh_ref, wg_ref, bg_ref, first)
        val, _ = _ffn_conv(v_ref, vh_ref, wv_ref, bv_ref, first)
        act_ref[...] = (gate * _sigmoid_tanh(gate) * val).astype(BF16)

    return pl.pallas_call(
        body, name=name, grid=(nblk, nt),
        in_specs=_ffn_specs(ts, lambda i: i, lambda i: jnp.maximum(i * per - 1, 0)),
        out_specs=pl.BlockSpec((ts, FFN_CB), lambda j, i: (i, j)),
        out_shape=jax.ShapeDtypeStruct((s, D_FF), BF16),
    )(up, up, w, b, up, up, w, b)


def _ffn_core_bwd(dact, up, w, b, *, ts=1024, name):
    s = up.shape[0]
    ts = _tile_rows(ts, s)
    nt = s // ts
    nblk = D_FF // FFN_CB
    per = ts // 16
    strip, halo = FFN_STRIP, FFN_HALO
    nstrips = ts // strip

    def fold(x):
        out = x[:SUBLANES]
        for r0 in range(SUBLANES, strip, SUBLANES):
            out = out + x[r0:r0 + SUBLANES]
        return out

    def body(da_ref, g_ref, gh_ref, wg_ref, bg_ref, v_ref, vh_ref, wv_ref, bv_ref,
             dg_ref, dv_ref, dwg_ref, dwv_ref, dbg_ref, dbv_ref, nxg_ref, nxv_ref, ug_ref, uv_ref):
        i = pl.program_id(1)
        first = nt - 1 - i > 0

        @pl.when(i == 0)
        def _():
            for ref in (nxg_ref, nxv_ref, dwg_ref, dwv_ref, dbg_ref, dbv_ref):
                ref[...] = jnp.zeros_like(ref)

        _ffn_stage(g_ref, gh_ref, ug_ref, first)
        _ffn_stage(v_ref, vh_ref, uv_ref, first)
        wg, wv, bg, bv = wg_ref[...], wv_ref[...], bg_ref[...], bv_ref[...]
        conv = _ffn_strip_conv

        def conv_t(d, nxt, w):
            out = w[2:3] * d
            for k in range(2):
                out = out + w[k:k + 1] * _shift_up(d, nxt, 2 - k)
            return out

        def step(t, carry):
            nxg, nxv, awg, awv, abg, abv = carry
            r = pl.multiple_of((nstrips - 1 - t) * strip, strip)
            gate, g_sh = conv(ug_ref, r, wg, bg)
            val, v_sh = conv(uv_ref, r, wv, bv)
            da = da_ref[pl.ds(r, strip), :].astype(F32)
            sg = _sigmoid_tanh(gate)
            dgate = da * val * (sg * (1.0 + gate * (1.0 - sg)))
            dval = da * (gate * sg)
            dg_ref[pl.ds(r, strip), :] = conv_t(dgate, nxg, wg).astype(BF16)
            dv_ref[pl.ds(r, strip), :] = conv_t(dval, nxv, wv).astype(BF16)
            awg = tuple(a + fold(dgate * sh) for a, sh in zip(awg, g_sh))
            awv = tuple(a + fold(dval * sh) for a, sh in zip(awv, v_sh))
            return dgate[:SUBLANES], dval[:SUBLANES], awg, awv, abg + fold(dgate), abv + fold(dval)

        zero = jnp.zeros((SUBLANES, FFN_CB), F32)
        init = (nxg_ref[...], nxv_ref[...], (zero,) * 3, (zero,) * 3, zero, zero)
        nxg, nxv, awg, awv, abg, abv = lax.fori_loop(0, nstrips, step, init)
        nxg_ref[...] = nxg
        nxv_ref[...] = nxv
        for k in range(3):
            dwg_ref[k:k + 1, :] += jnp.sum(awg[k], axis=0, keepdims=True)
            dwv_ref[k:k + 1, :] += jnp.sum(awv[k], axis=0, keepdims=True)
        dbg_ref[...] += jnp.sum(abg, axis=0, keepdims=True)
        dbv_ref[...] += jnp.sum(abv, axis=0, keepdims=True)

    def rev(i):
        return nt - 1 - i

    tile = pl.BlockSpec((ts, FFN_CB), lambda j, i: (rev(i), j))
    w_out = pl.BlockSpec((3, FFN_CB), lambda j, i: (0, j))
    b_out = pl.BlockSpec((1, FFN_CB), lambda j, i: (0, j))
    f = jax.ShapeDtypeStruct
    return pl.pallas_call(
        body, name=name, grid=(nblk, nt),
        in_specs=[tile] + _ffn_specs(ts, rev, lambda i: jnp.maximum(rev(i) * per - 1, 0)),
        out_specs=(tile, tile, w_out, w_out, b_out, b_out),
        out_shape=(f((s, D_FF), BF16), f((s, D_FF), BF16), f((3, D_FF), F32), f((3, D_FF), F32),
                   f((1, D_FF), F32), f((1, D_FF), F32)),
        scratch_shapes=[pltpu.VMEM((SUBLANES, FFN_CB), F32), pltpu.VMEM((SUBLANES, FFN_CB), F32),
                        pltpu.VMEM((ts + halo, FFN_CB), BF16), pltpu.VMEM((ts + halo, FFN_CB), BF16)],
    )(dact, up, up, w, b, up, up, w, b)


def _sgu_forward_block(zu, zg, gn, w_ref, bias, seg):
    u, tu = _gelu(zu)
    g, tg = _gelu(zg)
    ms = _dot_split(g * g, seg)
    rs = lax.rsqrt(ms + EPS)
    ghat = g * rs
    gv = ghat * gn
    gvb = gv.astype(BF16)
    lane = _lanes((CHUNK, LANES))
    chunks = []
    for c in range(zu.shape[0] // CHUNK):
        gc = gvb[c * CHUNK:(c + 1) * CHUNK]
        mix = jnp.where(lane < 64, _dot(w_ref[0], gc), _dot(w_ref[1], gc)) + bias
        chunks.append(mix)
    mixed = chunks[0] if len(chunks) == 1 else jnp.concatenate(chunks, axis=0)
    return u, tu, g, tg, rs, ghat, gvb, mixed


def _sgu_fwd(p1, gn, w, bias, seg, *, ts=512, name):
    s = p1.shape[0]
    ts = _tile_rows(ts, s)

    def body(zu_ref, zg_ref, gn_ref, w_ref, bias_ref, seg_ref, yc_ref):
        u, _, _, _, _, _, _, mixed = _sgu_forward_block(
            zu_ref[...], zg_ref[...], gn_ref[...], w_ref, bias_ref[...], seg_ref[...])
        yc_ref[...] = (u * mixed).astype(BF16)

    return pl.pallas_call(
        body, name=name, grid=(4, s // ts),
        in_specs=[pl.BlockSpec((ts, LANES), lambda j, i: (i, j)),
                  pl.BlockSpec((ts, LANES), lambda j, i: (i, 4 + j)),
                  pl.BlockSpec((1, LANES), lambda j, i: (0, j)),
                  pl.BlockSpec((2, CHUNK, CHUNK), lambda j, i: (j, 0, 0)),
                  pl.BlockSpec((CHUNK, LANES), lambda j, i: (0, j)),
                  pl.BlockSpec((LANES, LANES), lambda j, i: (0, 0))],
        out_specs=pl.BlockSpec((ts, LANES), lambda j, i: (i, j)),
        out_shape=jax.ShapeDtypeStruct((s, 4 * LANES), BF16),
    )(p1, p1, gn, w, bias, seg)


def _sgu_bwd(p1, dy, gn, w, wt, bias, seg, tril, *, ts=512, name):
    s = p1.shape[0]
    ts = _tile_rows(ts, s)
    nt = s // ts

    def body(zu_ref, zg_ref, dy_ref, gn_ref, w_ref, wt_ref, bias_ref, seg_ref, tril_ref,
             dzu_ref, dzg_ref, dw_ref, dbias_ref, dgn_ref):
        i = pl.program_id(1)
        zu = zu_ref[...]
        zg = zg_ref[...]
        gn_v = gn_ref[...]
        segv = seg_ref[...]
        u, tu, g, tg, rs, ghat, gvb, mixed = _sgu_forward_block(zu, zg, gn_v, w_ref, bias_ref[...], segv)
        dyv = dy_ref[...]
        du = dyv * mixed
        dmx = dyv * u

        @pl.when(i == 0)
        def _():
            dw_ref[...] = jnp.zeros_like(dw_ref)
            dbias_ref[...] = jnp.zeros_like(dbias_ref)
            dgn_ref[...] = jnp.zeros_like(dgn_ref)

        lane = _lanes((CHUNK, LANES))
        dgv_chunks = []
        dbias = jnp.zeros((CHUNK, LANES), F32)
        for c in range(ts // CHUNK):
            dmc = dmx[c * CHUNK:(c + 1) * CHUNK]
            gc = gvb[c * CHUNK:(c + 1) * CHUNK]
            dm_a = jnp.where(lane < 64, dmc, 0.0).astype(BF16)
            dm_b = jnp.where(lane >= 64, dmc, 0.0).astype(BF16)
            dw_ref[0] += _dot_nt(dm_a, gc)
            dw_ref[1] += _dot_nt(dm_b, gc)
            dgv_chunks.append(_dot(wt_ref[0], dm_a) + _dot(wt_ref[1], dm_b))
            dbias = dbias + dmc
        dbias_ref[...] += dbias
        dgv = dgv_chunks[0] if len(dgv_chunks) == 1 else jnp.concatenate(dgv_chunks, axis=0)
        dgn_ref[...] += jnp.sum(dgv * ghat, axis=0, keepdims=True)
        dgh = dgv * gn_v
        dg = rs * (dgh - ghat * _dot_split(dgh * ghat, segv))
        dzu_ref[...] = (du * _gelu_grad(zu, tu)).astype(BF16)
        dzg_ref[...] = (dg * _gelu_grad(zg, tg)).astype(BF16)

        @pl.when(i == nt - 1)
        def _():
            dw_ref[0] = dw_ref[0] * tril_ref[...]
            dw_ref[1] = dw_ref[1] * tril_ref[...]

    f = jax.ShapeDtypeStruct
    colj = pl.BlockSpec((ts, LANES), lambda j, i: (i, j))
    wsp = pl.BlockSpec((2, CHUNK, CHUNK), lambda j, i: (j, 0, 0))
    sq = pl.BlockSpec((LANES, LANES), lambda j, i: (0, 0))
    return pl.pallas_call(
        body, name=name, grid=(4, nt),
        in_specs=[colj, pl.BlockSpec((ts, LANES), lambda j, i: (i, 4 + j)), colj,
                  pl.BlockSpec((1, LANES), lambda j, i: (0, j)), wsp, wsp,
                  pl.BlockSpec((CHUNK, LANES), lambda j, i: (0, j)), sq, sq],
        out_specs=(colj, colj, wsp, pl.BlockSpec((CHUNK, LANES), lambda j, i: (0, j)),
                   pl.BlockSpec((1, LANES), lambda j, i: (0, j))),
        out_shape=(f((s, 4 * LANES), BF16), f((s, 4 * LANES), BF16), f((8, CHUNK, CHUNK), F32),
                   f((CHUNK, 4 * LANES), F32), f((1, 4 * LANES), F32)),
    )(p1, p1, dy, gn, w, wt, bias, seg, tril)


F_COL = 20


def _fcum_fwd(p1, bf, *, ts=512, name):
    s = p1.shape[0]
    ts = _tile_rows(ts, s)

    def body(f_ref, bf_ref, c_ref, car_ref):
        i = pl.program_id(0)
        z = f_ref[...] + bf_ref[...]
        logf = jnp.minimum(z, 0.0) - _log1p_pos(jnp.exp(-jnp.abs(z)))

        @pl.when(i == 0)
        def _():
            car_ref[...] = jnp.zeros_like(car_ref)

        c_ref[...] = _cumsum_fwd(logf) + car_ref[0:1, :]
        car_ref[0:1, :] = c_ref[ts - 1:ts, :]

    return pl.pallas_call(
        body, name=name, grid=(s // ts,),
        in_specs=[pl.BlockSpec((ts, LANES), lambda i: (i, F_COL)), pl.BlockSpec((1, LANES), lambda i: (0, 0))],
        out_specs=pl.BlockSpec((ts, LANES), lambda i: (i, 0)),
        out_shape=jax.ShapeDtypeStruct((s, LANES), F32),
        scratch_shapes=[pltpu.VMEM((SUBLANES, LANES), F32)],
    )(p1, bf)


def _fcum_bwd(dcs, dcq, p1, bf, *, ts=512, name):
    s = p1.shape[0]
    ts = _tile_rows(ts, s)
    nt = s // ts

    def body(dc_ref, dcq_ref, f_ref, bf_ref, df_ref, dbf_ref, car_ref):
        i = pl.program_id(0)

        @pl.when(i == 0)
        def _():
            car_ref[...] = jnp.zeros_like(car_ref)
            dbf_ref[...] = jnp.zeros_like(dbf_ref)

        dc = dc_ref[...]
        lane = _lanes((ts, LANES))
        for h in range(8):
            dc = dc + jnp.where(lane == h, dcq_ref[:, h * LANES:(h + 1) * LANES], 0.0)
        dlog = _cumsum_rev(dc) + car_ref[0:1, :]
        car_ref[...] = dlog[:SUBLANES]
        z = f_ref[...] + bf_ref[...]
        df = dlog * _sigmoid(-z)
        df_ref[...] = df.astype(BF16)
        dbf_ref[...] += jnp.sum(df, axis=0, keepdims=True)

    return pl.pallas_call(
        body, name=name, grid=(nt,),
        in_specs=[pl.BlockSpec((ts, LANES), lambda i: (nt - 1 - i, 0)),
                  pl.BlockSpec((ts, 8 * LANES), lambda i: (nt - 1 - i, 0)),
                  pl.BlockSpec((ts, LANES), lambda i: (nt - 1 - i, F_COL)),
                  pl.BlockSpec((1, LANES), lambda i: (0, 0))],
        out_specs=(pl.BlockSpec((ts, LANES), lambda i: (nt - 1 - i, 0)), pl.BlockSpec((1, LANES), lambda i: (0, 0))),
        out_shape=(jax.ShapeDtypeStruct((s, LANES), BF16), jax.ShapeDtypeStruct((1, LANES), F32)),
        scratch_shapes=[pltpu.VMEM((SUBLANES, LANES), F32)],
    )(dcs, dcq, p1, bf)


def _fox_scores(qm, kb, bias, ck, diagonal):
    sc = _dot_nt(qm, kb) + bias - ck
    if diagonal:
        sc = jnp.where(_lanes(sc.shape) <= _rows(sc.shape), sc, NEG)
    return sc


def _head_masks(shape):
    lane = _lanes(shape)
    return lane < 64, lane >= 64


def _fox_fwd(p1, cq, ck, *, tq=512, name):
    s = p1.shape[0]
    tq = _tile_rows(tq, s)
    tk = tq
    nq = s // tq

    def body(q_ref, k_ref, v_ref, cq_ref, ck_ref, o_ref, lb_ref):
        qi = pl.program_id(1)
        q = q_ref[...] * 0.125
        first, second = _head_masks((tq, LANES))
        qms = [jnp.where(sel, q, 0.0).astype(BF16) for sel in (first, second)]
        cqs = [cq_ref[:, hh * LANES:(hh + 1) * LANES] for hh in range(2)]
        biases = [jnp.tile(cqh, (1, tk // LANES)) for cqh in cqs]

        def step(kj, carry, diagonal):
            cols = pl.ds(pl.multiple_of(kj * tk, tk), tk)
            kb = k_ref[cols, :].astype(BF16)
            vb = v_ref[cols, :].astype(BF16)
            new, outs = [], []
            acc = carry[4]
            for hh in range(2):
                m_prev, l_prev = carry[2 * hh], carry[2 * hh + 1]
                sc = _fox_scores(qms[hh], kb, biases[hh], ck_ref[hh, :, cols], diagonal)
                m_new = jnp.maximum(m_prev, jnp.max(sc, axis=1, keepdims=True))
                pm = jnp.exp(sc - jnp.tile(m_new, (1, tk // LANES)))
                alpha = jnp.exp(m_prev - m_new)
                new += [m_new, alpha * l_prev + jnp.sum(pm, axis=1, keepdims=True)]
                outs.append(acc * alpha + _dot(pm.astype(BF16), vb))
            return tuple(new) + (jnp.where(first, outs[0], outs[1]),)

        zero = jnp.zeros((tq, LANES), F32)
        low = jnp.full((tq, LANES), NEG, F32)
        carry = lax.fori_loop(0, qi, lambda kj, c: step(kj, c, False), (low, zero, low, zero, zero))
        m0, l0, m1, l1, acc = step(qi, carry, True)
        o_ref[...] = (acc / jnp.where(first, l0, l1)).astype(BF16)
        lb_ref[:, 0:LANES] = cqs[0] - (m0 + jnp.log(l0))
        lb_ref[:, LANES:2 * LANES] = cqs[1] - (m1 + jnp.log(l1))

    return pl.pallas_call(
        body, name=name, grid=(4, nq),
        in_specs=[pl.BlockSpec((tq, LANES), lambda j, qi: (qi, 8 + j)),
                  pl.BlockSpec((s, LANES), lambda j, qi: (0, 12 + j)),
                  pl.BlockSpec((s, LANES), lambda j, qi: (0, 16 + j)),
                  pl.BlockSpec((tq, 2 * LANES), lambda j, qi: (qi, j)),
                  pl.BlockSpec((2, 1, s), lambda j, qi: (j, 0, 0))],
        out_specs=(pl.BlockSpec((tq, LANES), lambda j, qi: (qi, j)),
                   pl.BlockSpec((tq, 2 * LANES), lambda j, qi: (qi, j))),
        out_shape=(jax.ShapeDtypeStruct((s, 4 * LANES), BF16), jax.ShapeDtypeStruct((s, 8 * LANES), F32)),
    )(p1, p1, p1, cq, ck)


def _fox_delta(dy, o, sel, *, ts=512, name):
    s = o.shape[0]
    ts = _tile_rows(ts, s)

    def body(do_ref, o_ref, sel_ref, d_ref):
        prod = do_ref[...] * o_ref[...].astype(F32)
        d_ref[:, 0:LANES] = _dot_split(prod, sel_ref[0])
        d_ref[:, LANES:2 * LANES] = _dot_split(prod, sel_ref[1])

    return pl.pallas_call(
        body, name=name, grid=(4, s // ts),
        in_specs=[pl.BlockSpec((ts, LANES), lambda j, i: (i, 4 + j)),
                  pl.BlockSpec((ts, LANES), lambda j, i: (i, j)),
                  pl.BlockSpec((2, LANES, LANES), lambda j, i: (0, 0, 0))],
        out_specs=pl.BlockSpec((ts, 2 * LANES), lambda j, i: (i, j)),
        out_shape=jax.ShapeDtypeStruct((s, 8 * LANES), F32),
    )(dy, o, sel)


def _fox_bwd(p1, dy, lb, delta, ck, *, tq=512, name):
    s = p1.shape[0]
    tq = _tile_rows(tq, s)
    tk = tq
    nq = s // tq

    def body(q_ref, k_ref, v_ref, do_ref, lb_ref, dl_ref, ck_ref,
             dq_ref, dk_ref, dv_ref, dck_ref, dcq_ref, dqa_ref, dra_ref):
        kj = pl.program_id(1)

        @pl.when(kj == 0)
        def _():
            dqa_ref[...] = jnp.zeros_like(dqa_ref)
            dra_ref[...] = jnp.zeros_like(dra_ref)

        kf = k_ref[...]
        kb = kf.astype(BF16)
        vb = v_ref[...].astype(BF16)
        first, second = _head_masks((tk, LANES))
        kms = [jnp.where(sel, kf, 0.0).astype(BF16) for sel in (first, second)]
        cks = [ck_ref[hh] for hh in range(2)]

        def step(qi, carry, diagonal):
            dk_acc, dv_acc, dc0, dc1 = carry
            dcs = [dc0, dc1]
            rows = pl.ds(pl.multiple_of(qi * tq, tq), tq)
            q = q_ref[rows, :] * 0.125
            do = do_ref[rows, :]
            for hh, sel in enumerate((first, second)):
                qm = jnp.where(sel, q, 0.0).astype(BF16)
                dom = jnp.where(sel, do, 0.0).astype(BF16)
                bias = jnp.tile(lb_ref[rows, hh * LANES:(hh + 1) * LANES], (1, tk // LANES))
                pm = jnp.exp(_fox_scores(qm, kb, bias, cks[hh], diagonal))
                dv_acc = dv_acc + _dot_tn(pm.astype(BF16), dom)
                dp = _dot_nt(dom, vb)
                ds = pm * (dp - jnp.tile(dl_ref[rows, hh * LANES:(hh + 1) * LANES], (1, tk // LANES)))
                dsb = ds.astype(BF16)
                dk_acc = dk_acc + _dot_tn(dsb, qm)
                dcs[hh] = dcs[hh] - jnp.sum(ds, axis=0, keepdims=True)
                dqa_ref[rows, :] += _dot(dsb, kms[hh])
                dra_ref[hh, rows, :] += jnp.sum(ds, axis=1, keepdims=True)
            return dk_acc, dv_acc, dcs[0], dcs[1]

        zero = jnp.zeros((tk, LANES), F32)
        zrow = jnp.zeros((1, tk), F32)
        carry = step(kj, (zero, zero, zrow, zrow), True)
        dk_acc, dv_acc, dc0, dc1 = lax.fori_loop(kj + 1, nq, lambda qi, c: step(qi, c, False), carry)
        dk_ref[...] = dk_acc.astype(BF16)
        dv_ref[...] = dv_acc.astype(BF16)
        dck_ref[0] = dc0
        dck_ref[1] = dc1

        @pl.when(kj == nq - 1)
        def _():
            dq_ref[...] = (dqa_ref[...] * 0.125).astype(BF16)
            dcq_ref[:, 0:LANES] = dra_ref[0]
            dcq_ref[:, LANES:2 * LANES] = dra_ref[1]

    def full(width, col0):
        return pl.BlockSpec((s, width), lambda j, kj: (0, col0 + j))

    kblk = pl.BlockSpec((tk, LANES), lambda j, kj: (kj, j))
    f = jax.ShapeDtypeStruct
    return pl.pallas_call(
        body, name=name, grid=(4, nq),
        in_specs=[full(LANES, 8),
                  pl.BlockSpec((tk, LANES), lambda j, kj: (kj, 12 + j)),
                  pl.BlockSpec((tk, LANES), lambda j, kj: (kj, 16 + j)),
                  full(LANES, 4), full(2 * LANES, 0), full(2 * LANES, 0),
                  pl.BlockSpec((2, 1, tk), lambda j, kj: (j, 0, kj))],
        out_specs=(full(LANES, 0), kblk, kblk, pl.BlockSpec((2, 1, tk), lambda j, kj: (j, 0, kj)),
                   full(2 * LANES, 0)),
        out_shape=(f((s, 4 * LANES), BF16), f((s, 4 * LANES), BF16), f((s, 4 * LANES), BF16),
                   f((8, 1, s), F32), f((s, 8 * LANES), F32)),
        scratch_shapes=[pltpu.VMEM((s, LANES), F32), pltpu.VMEM((2, s, LANES), F32)],
    )(p1, p1, p1, dy, lb, delta, ck)


def _row_block(r, cap=256):
    best = None
    for rb in range(2 * SUBLANES, min(r, cap) + 1, 2 * SUBLANES):
        if r % rb == 0:
            best = rb
    return r if best is None else best


def _adamw_many(ws, gs, ms, vs, *, name):
    shapes = [a.shape for a in ws]

    def flat(a):
        return a.reshape((-1, a.shape[-1]))

    n = len(ws)
    operands = [flat(a) for group in (ws, gs, ms, vs) for a in group]

    def body(*refs):
        w_refs, g_refs, m_refs, v_refs = (refs[i * n:(i + 1) * n] for i in range(4))
        d_refs, nm_refs, nv_refs = (refs[(4 + i) * n:(5 + i) * n] for i in range(3))
        for p in range(n):
            gv = g_refs[p][...]
            mn = ADAM_B1 * m_refs[p][...] + (1.0 - ADAM_B1) * gv
            vn = ADAM_B2 * v_refs[p][...] + (1.0 - ADAM_B2) * (gv * gv)
            m_hat = mn / ADAM_C1
            v_hat = vn / ADAM_C2
            d_refs[p][...] = (-ADAM_LR) * (m_hat / (jnp.sqrt(v_hat) + ADAM_EPS) + ADAM_WD * w_refs[p][...])
            nm_refs[p][...] = mn
            nv_refs[p][...] = vn

    vm = pl.BlockSpec(memory_space=pltpu.VMEM)
    out_shape = [jax.ShapeDtypeStruct(flat(a).shape, F32) for a in ws] * 3
    outs = pl.pallas_call(
        body, name=name, in_specs=[vm] * (4 * n), out_specs=[vm] * (3 * n), out_shape=out_shape,
    )(*operands)
    return [tuple(outs[i * n + p].reshape(shapes[p]) for i in range(3)) for p in range(n)]


def _adamw_halves(w, mine, theirs, m, v, core, *, name):
    layers, r, c = w.shape
    rh = r // 2
    rb = _row_block(rh)
    per = rh // rb

    def body(core_ref, w_ref, *refs):
        g_refs = refs[:2 * layers]
        m_ref, v_ref, g_ref, d_ref, nm_ref, nv_ref = refs[2 * layers:]
        own = pl.program_id(1) == core_ref[0]
        gv = jnp.where(own, g_refs[0][...], g_refs[layers][...])
        for l in range(1, layers):
            gv = jnp.where(pl.program_id(0) == l, jnp.where(own, g_refs[l][...], g_refs[layers + l][...]), gv)
        g_ref[...] = gv
        mn = ADAM_B1 * m_ref[...] + (1.0 - ADAM_B1) * gv
        vn = ADAM_B2 * v_ref[...] + (1.0 - ADAM_B2) * (gv * gv)
        m_hat = mn / ADAM_C1
        v_hat = vn / ADAM_C2
        d_ref[...] = (-ADAM_LR) * (m_hat / (jnp.sqrt(v_hat) + ADAM_EPS) + ADAM_WD * w_ref[...])
        nm_ref[...] = mn
        nv_ref[...] = vn

    full = pl.BlockSpec((None, rb, c), lambda l, h, i, core_ref: (l, h * per + i, 0))
    half = pl.BlockSpec((rb, c), lambda l, h, i, core_ref: (i, 0))
    shp = jax.ShapeDtypeStruct((layers, r, c), F32)
    return pl.pallas_call(
        body, name=name,
        grid_spec=pltpu.PrefetchScalarGridSpec(
            num_scalar_prefetch=1, grid=(layers, 2, per),
            in_specs=[full] + [half] * (2 * layers) + [full, full], out_specs=(full,) * 4),
        out_shape=(shp,) * 4,
    )(core, w, *mine, *theirs, m, v)


def _pair_specs(col, rb, c):
    if col:
        g_spec = pl.BlockSpec((None, rb, c), lambda t, i, sel: (sel[0], i, sel[1 + t]))
    else:
        g_spec = pl.BlockSpec((None, None, rb, c), lambda t, i, sel: (sel[1 + t], sel[0], i, 0))
    return g_spec, pl.BlockSpec((None, rb, c), lambda t, i, sel: (sel[1 + t], i, 0))


def _pair_sum(g, col, ra, sel, after, *, name):
    _, rh, c = ra.shape
    rb = _row_block(rh)

    def body(sel_ref, g_ref, ra_ref, after_ref, h16_ref):
        h16_ref[...] = (g_ref[...] + ra_ref[...]).astype(BF16)

    g_spec, ra_spec = _pair_specs(col, rb, c)
    return pl.pallas_call(
        body, name=name,
        grid_spec=pltpu.PrefetchScalarGridSpec(
            num_scalar_prefetch=1, grid=(2, rh // rb), in_specs=[g_spec, ra_spec, ANY],
            out_specs=pl.BlockSpec((None, rb, c), lambda t, i, sel: (t, i, 0))),
        out_shape=jax.ShapeDtypeStruct((2, rh, c), BF16),
    )(sel, g, ra, after)


def _first_sum(g, col, ra, r1, sel, after, *, name):
    _, rh, c = ra.shape
    rb = _row_block(rh)

    def body(sel_ref, g_ref, ra_ref, r_ref, after_ref, s_ref, s16_ref):
        tot = (g_ref[...] + ra_ref[...]) + r_ref[...].astype(F32)
        s_ref[...] = tot
        s16_ref[...] = tot.astype(BF16)

    g_spec, ra_spec = _pair_specs(col, rb, c)
    slot = pl.BlockSpec((None, rb, c), lambda t, i, sel_ref: (t, i, 0))
    return pl.pallas_call(
        body, name=name,
        grid_spec=pltpu.PrefetchScalarGridSpec(
            num_scalar_prefetch=1, grid=(2, rh // rb), in_specs=[g_spec, ra_spec, slot, ANY],
            out_specs=(slot, slot)),
        out_shape=(jax.ShapeDtypeStruct((2, rh, c), F32), jax.ShapeDtypeStruct((2, rh, c), BF16)),
    )(sel, g, ra, r1, after)


def _second_sum(s1, r2, mine, after, *, name):
    _, rh, c = s1.shape
    rb = _row_block(rh)

    def body(mine_ref, s_ref, r_ref, after_ref, t_ref):
        t_ref[...] = s_ref[...] + r_ref[...].astype(F32)

    flat = pl.BlockSpec((rb, c), lambda i, mine_ref: (i, 0))
    return pl.pallas_call(
        body, name=name,
        grid_spec=pltpu.PrefetchScalarGridSpec(
            num_scalar_prefetch=1, grid=(rh // rb,),
            in_specs=[pl.BlockSpec((None, rb, c), lambda i, mine_ref: (mine_ref[0], i, 0)), flat, ANY],
            out_specs=flat),
        out_shape=jax.ShapeDtypeStruct((rh, c), F32),
    )(mine, s1, r2, after)


def _place(shards, layer, col, chip, dtype, *, name):
    _, r, c = shards.shape
    rh = r // 2
    rb = _row_block(rh)

    def body(chip_ref, s_ref, o_ref):
        o_ref[...] = s_ref[...].astype(o_ref.dtype)

    if col:
        out_spec = pl.BlockSpec((None, rb, c), lambda h, i, chip_ref: (h, i, chip_ref[0]))
        shape = (2, rh, N_CHIPS * c)
    else:
        out_spec = pl.BlockSpec((None, None, rb, c), lambda h, i, chip_ref: (chip_ref[0], h, i, 0))
        shape = (N_CHIPS, 2, rh, c)
    per = rh // rb
    return pl.pallas_call(
        body, name=name,
        grid_spec=pltpu.PrefetchScalarGridSpec(
            num_scalar_prefetch=1, grid=(2, per),
            in_specs=[pl.BlockSpec((None, rb, c), lambda h, i, chip_ref: (layer, h * per + i, 0))],
            out_specs=out_spec),
        out_shape=jax.ShapeDtypeStruct(shape, dtype),
    )(chip, shards)


ANY = pl.BlockSpec(memory_space=pl.ANY)


def _mesh_pos():
    return lax.axis_index("x"), lax.axis_index("y"), lax.axis_index("c")


def _other_chips(x, y):
    return [(1 - x, y), (x, 1 - y), (1 - x, 1 - y)]


def _remote(src, dst, ssem, rsem, dev):
    return pltpu.make_async_remote_copy(src_ref=src, dst_ref=dst, send_sem=ssem, recv_sem=rsem,
                                        device_id=dev, device_id_type=MESH)


def _flip(a, b):
    return a + b - 2 * a * b


def _handshake(peers):
    barrier = pltpu.get_barrier_semaphore()
    for peer in peers:
        pl.semaphore_signal(barrier, inc=1, device_id=peer, device_id_type=MESH)
    pl.semaphore_wait(barrier, len(peers))


def _slab(ref, col, width, k, h):
    if not col:
        return ref.at[k, h]
    start = k * width if isinstance(k, int) else pl.multiple_of(k * width, LANES)
    return ref.at[h, :, pl.ds(start, width)]


def _all_gather(bufs, cols, *, collective_id, name):
    n = len(bufs)
    widths = [b.shape[2] // N_CHIPS if col else b.shape[3] for b, col in zip(bufs, cols)]
    outs = [jax.new_ref(b, memory_space=pltpu.MemorySpace.HBM) for b in bufs]

    def body(ssem, rsem):
        x, y, c = _mesh_pos()
        me = 2 * x + y
        sib = (x, y, 1 - c)
        n1 = (_flip(x, 1 - c), _flip(y, c))
        n2 = (_flip(x, c), _flip(y, 1 - c))
        k1 = 2 * n1[0] + n1[1]
        k2 = 2 * n2[0] + n2[1]
        kd = 2 * (1 - x) + (1 - y)
        _handshake([n1 + (c,), n2 + (c,), sib])

        def slab(a, k, h):
            return _slab(outs[a], cols[a], widths[a], k, h)

        def copy(a, j, src, dst, dev):
            return _remote(src, dst, ssem.at[a, j], rsem.at[a, j], dev)

        sends = []
        for a in range(n):
            for j, nb in ((0, n1), (1, n2)):
                own = slab(a, me, c)
                cp = copy(a, j, own, own, nb + (c,))
                cp.start()
                sends.append(cp)
        arrivals = ((0, k1, n1, 3), (1, k2, n2, 4), (2, kd, n2, 5))
        for j, k, nb, fwd in arrivals:
            for a in range(n):
                got = slab(a, k, c)
                copy(a, j, got, got, nb + (c,)).wait_recv()
                if j == 0:
                    cp = copy(a, 2, got, got, n2 + (c,))
                    cp.start()
                    sends.append(cp)
                cp = copy(a, fwd, got, got, sib)
                cp.start()
                sends.append(cp)
        for fwd, k in ((3, k2), (4, k1), (5, kd)):
            for a in range(n):
                got = slab(a, k, 1 - c)
                copy(a, fwd, got, got, sib).wait_recv()
        for cp in sends:
            cp.wait_send()

    _sequencer_call(body, (), [(n, 6), (n, 6)], collective_id, name)()
    return [ref[...] for ref in outs]


def _sequencer_call(body, out_types, sem_shapes, collective_id, name):
    return pl.kernel(
        body, name=name, out_type=out_types,
        mesh=plsc.ScalarSubcoreMesh(axis_name="sequencer", num_cores=1),
        scratch_types=[pltpu.SemaphoreType.DMA(shape) for shape in sem_shapes],
        compiler_params=pltpu.CompilerParams(collective_id=collective_id))


def _send_other_half(grads, cols, *, collective_id, name):
    n = len(grads)

    def shard_shape(g, col):
        if col:
            return (g.shape[1], g.shape[2] // N_CHIPS)
        return g.shape[2:]

    shapes = [shard_shape(g, col) for g, col in zip(grads, cols)]

    def body(*refs):
        ins, outs = refs[:n], refs[n:2 * n]
        ssem, rsem = refs[2 * n:]
        x, y, c = _mesh_pos()
        sib = (x, y, 1 - c)
        _handshake([sib])
        sends = []
        for a in range(n):
            for k in range(N_CHIPS):
                src = _slab(ins[a], cols[a], shapes[a][1], k, 1 - c)
                cp = _remote(src, outs[a].at[k], ssem.at[a, k], rsem.at[a, k], sib)
                cp.start()
                sends.append(cp)
        for cp in sends:
            cp.wait()

    out_types = [jax.ShapeDtypeStruct((N_CHIPS,) + shp, g.dtype) for g, shp in zip(grads, shapes)]
    return _sequencer_call(body, out_types, [(n, N_CHIPS), (n, N_CHIPS)], collective_id, name)(*grads)


def _send_first(sums, *, collective_id, name):
    n = len(sums)

    def body(*refs):
        ins, outs = refs[:n], refs[n:2 * n]
        ssem, rsem = refs[2 * n:]
        x, y, c = _mesh_pos()
        nb = (_flip(x, c), _flip(y, 1 - c), c)
        _handshake([nb])
        sends = []
        for a in range(n):
            for t in range(2):
                cp = _remote(ins[a].at[t], outs[a].at[t], ssem.at[a, t], rsem.at[a, t], nb)
                cp.start()
                sends.append(cp)
        for cp in sends:
            cp.wait()

    out_types = [jax.ShapeDtypeStruct(h.shape, h.dtype) for h in sums]
    return _sequencer_call(body, out_types, [(n, 2), (n, 2)], collective_id, name)(*sums)


def _send_second(sums, *, collective_id, name):
    n = len(sums)

    def body(*refs):
        ins, outs = refs[:n], refs[n:2 * n]
        ssem, rsem = refs[2 * n:]
        x, y, c = _mesh_pos()
        nb = (_flip(x, 1 - c), _flip(y, c), c)
        other = 1 - (c * y + (1 - c) * x)
        _handshake([nb])
        sends = []
        for a in range(n):
            cp = _remote(ins[a].at[other], outs[a], ssem.at[a], rsem.at[a], nb)
            cp.start()
            sends.append(cp)
        for cp in sends:
            cp.wait()

    out_types = [jax.ShapeDtypeStruct(s.shape[1:], s.dtype) for s in sums]
    return _sequencer_call(body, out_types, [(n,), (n,)], collective_id, name)(*sums)


def _swap_halves(halves, *, collective_id, name):
    n = len(halves)

    def body(*refs):
        ins, outs = refs[:n], refs[n:2 * n]
        ssem, rsem = refs[2 * n:]
        x, y, c = _mesh_pos()
        sib = (x, y, 1 - c)
        _handshake([sib])
        cps = []
        for a in range(n):
            cp = _remote(ins[a], outs[a], ssem.at[a], rsem.at[a], sib)
            cp.start()
            cps.append(cp)
        for cp in cps:
            cp.wait()

    out_types = [jax.ShapeDtypeStruct(h.shape, h.dtype) for h in halves]
    return _sequencer_call(body, out_types, [(n,), (n,)], collective_id, name)(*halves)


def _all_reduce_small(buf, *, name):
    r = buf.shape[0]
    rh = r // 2

    def body(in_ref, out_ref, x1_ref, x2_ref, ssem, rsem):
        x, y, c = _mesh_pos()
        me = 2 * x + y
        sib = (x, y, 1 - c)
        chips = _other_chips(x, y)
        cp = _remote(in_ref, x1_ref, ssem.at[0], rsem.at[0], sib)
        cp.start()
        cp.wait()
        off = pl.multiple_of(c * rh, SUBLANES)
        x2_ref[me] = in_ref[pl.ds(off, rh), :] + x1_ref[pl.ds(off, rh), :]
        sends = []
        for j, (cx, cy) in enumerate(chips):
            s = _remote(x2_ref.at[me], x2_ref.at[me], ssem.at[1 + j], rsem.at[1 + j], (cx, cy, c))
            s.start()
            sends.append(s)
        for j, (cx, cy) in enumerate(chips):
            slot = x2_ref.at[2 * cx + cy]
            _remote(slot, slot, ssem.at[1 + j], rsem.at[1 + j], (cx, cy, c)).wait_recv()
        out_ref[pl.ds(off, rh), :] = ((x2_ref[0] + x2_ref[1]) + x2_ref[2]) + x2_ref[3]
        for s in sends:
            s.wait_send()
        mine = out_ref.at[pl.ds(off, rh), :]
        s3 = _remote(mine, mine, ssem.at[4], rsem.at[4], sib)
        s3.start()
        off2 = pl.multiple_of((1 - c) * rh, SUBLANES)
        theirs = out_ref.at[pl.ds(off2, rh), :]
        _remote(theirs, theirs, ssem.at[4], rsem.at[4], sib).wait_recv()
        s3.wait_send()

    vm = pl.BlockSpec(memory_space=pltpu.VMEM)
    return pl.pallas_call(
        body, name=name, in_specs=[vm], out_specs=vm,
        out_shape=jax.ShapeDtypeStruct((r, LANES), F32),
        scratch_shapes=[pltpu.VMEM((r, LANES), F32), pltpu.VMEM((N_CHIPS, rh, LANES), F32),
                        pltpu.SemaphoreType.DMA((5,)), pltpu.SemaphoreType.DMA((5,))],
    )(buf)


PACK_ALIGN = 2 * SUBLANES * LANES


def _pack(arrays):
    parts, offs, off = [], [], 0
    for a in arrays:
        flat = a.reshape(-1).astype(F32)
        padded = -(-flat.shape[0] // PACK_ALIGN) * PACK_ALIGN
        parts.append(jnp.pad(flat, (0, padded - flat.shape[0])))
        offs.append(off)
        off += padded
    buf = jnp.concatenate(parts).reshape(-1, LANES)
    return buf, offs


def _unpack(buf, offs, shapes):
    flat = buf.reshape(-1)
    out = []
    for off, shp in zip(offs, shapes):
        size = 1
        for d in shp:
            size *= d
        out.append(flat[off:off + size].reshape(shp))
    return out


def _cols_from_shards(g4):
    _, k, ns = g4.shape
    return jnp.transpose(g4, (1, 0, 2)).reshape(k, N_CHIPS * ns)


def _cols_to_shards(w):
    k, n = w.shape
    return jnp.transpose(w.reshape(k, N_CHIPS, n // N_CHIPS), (1, 0, 2))


def _pair_blockdiag(w8):
    w = w8.reshape(4, 2, 64, 64)
    z = jnp.zeros((4, 64, 64), w8.dtype)
    top = jnp.concatenate([w[:, 0], z], axis=2)
    bot = jnp.concatenate([z, w[:, 1]], axis=2)
    return jnp.concatenate([top, bot], axis=1)


def _pair_diag_blocks(w4):
    a = w4[:, :64, :64]
    b = w4[:, 64:, 64:]
    return jnp.stack([a, b], axis=1).reshape(8, 64, 64)


def _local_step(x, target, wts, on_event=None):
    s = x.shape[0]
    g = {}

    def event(name, token):
        if on_event is not None:
            on_event(name, g, token)

    win0 = wts["w_in0"]
    wout0 = wts["w_out0"]
    wout1 = wts["w_out1"]
    wup = wts["w_up"]
    wdown = wts["w_down"]
    w4, b4, w3, b3 = wts["w4"], wts["b4"], wts["w3"], wts["b3"]
    wa, wx = wts["wa"], wts["wx"]
    wat, wxt = jnp.swapaxes(wa, 1, 2), jnp.swapaxes(wx, 1, 2)
    ba, bx, lam = wts["ba"], wts["bx"], wts["lam"]
    fcw, fcb = wts["ffn_cw"], wts["ffn_cb"]
    sgu_w, sgu_wt = wts["sgu_w"], wts["sgu_wt"]
    sgu_bias, sgu_gn = wts["sgu_bias"], wts["sgu_gn"]
    bf = wts["bf"]

    lane = jnp.arange(LANES)
    seg = jnp.where((lane[:, None] // 64) == (lane[None, :] // 64), 1.0 / 64.0, 0.0).astype(BF16)
    sel = jnp.stack([jnp.broadcast_to((lane[:, None] < 64), (LANES, LANES)),
                     jnp.broadcast_to((lane[:, None] >= 64), (LANES, LANES))]).astype(BF16)
    tril = (lane[:, None] >= lane[None, :]).astype(F32)

    n0 = _norm_fwd(x, wts["g_mix0"], name="norm_mix0")
    p0 = _mm([n0], win0, nb=1280, name="mm_in0")
    ya, yb, hl = _even_core_fwd(p0, w4, b4, wa, ba, wx, bx, lam, w3, b3, name="even_fwd")
    h1, n1 = _mm([ya, yb], wout0, res=x, norm_out=wts["g_ffn"][0], name="mm_out0")

    def ffn_fwd(h, n, layer, next_gain):
        up = _mm([n], wup[layer], out_dtype=BF16, ts=1024, nb=1408, name=f"mm_up{layer}")
        act = _ffn_core_fwd(up, fcw[layer], fcb[layer], name=f"ffn_fwd{layer}")
        if next_gain is None:
            return up, act, _mm([act], wdown[layer], res=h, name=f"mm_down{layer}"), None
        hn, nn = _mm([act], wdown[layer], res=h, norm_out=next_gain, name=f"mm_down{layer}")
        return up, act, hn, nn

    up0, act0, h2, n2 = ffn_fwd(h1, n1, 0, wts["g_mix1"])

    win1 = wts["w_in1"](n2)
    p1 = _mm([n2], win1, name="mm_in1")
    yc = _sgu_fwd(p1, sgu_gn, sgu_w, sgu_bias, seg, name="sgu_fwd")
    cum = _fcum_fwd(p1, bf, name="fcum_fwd")
    c8 = cum[:, :8]
    cq = jnp.broadcast_to(c8[:, :, None], (s, 8, LANES)).reshape(s, 8 * LANES)
    ck = jnp.transpose(c8).reshape(8, 1, s)
    yd, lb = _fox_fwd(p1, cq, ck, name="fox_fwd")
    h3, n3 = _mm([yc, yd], wout1, res=h2, norm_out=wts["g_ffn"][1], name="mm_out1")

    up1, act1, h4, _ = ffn_fwd(h3, n3, 1, None)
    dh4, loss, g["final_norm"] = _final(h4, wts["g_final"], target, name="final")

    def ffn_bwd(dh, h, n, up, act, layer):
        dact = _mm([dh], wdown[layer], trans_w=True, out_dtype=BF16, ts=1024, nb=1408, name=f"mm_dact{layer}")
        g[f"w_down{layer}"] = _mm_tn([act], [dh], ts=1024, nb=512, name=f"mm_dwdown{layer}")
        event(f"dwdown{layer}", g[f"w_down{layer}"])
        dgate, dval, dcwg, dcwv, dcbg, dcbv = _ffn_core_bwd(dact, up, fcw[layer], fcb[layer], name=f"ffn_bwd{layer}")
        event(f"ffn_bwd{layer}", dgate)
        g[f"w_up{layer}"] = _mm_tn([n], [dgate, dval], ts=1024, nb=1408, name=f"mm_dwup{layer}")
        event(f"dwup{layer}", g[f"w_up{layer}"])
        dhn, g[f"g_ffn{layer}"] = _mm([dgate, dval], wup[layer], trans_w=True, ts=512,
                                      norm_bwd=(h, wts["g_ffn"][layer], dh), name=f"mm_dn_ffn{layer}")
        g[f"ffn_cw{layer}"] = jnp.concatenate([dcwg, dcwv], axis=1)
        g[f"ffn_cb{layer}"] = jnp.concatenate([dcbg, dcbv], axis=1)
        return dhn

    dh3 = ffn_bwd(dh4, h3, n3, up1, act1, 1)

    dy1 = _mm([dh3], wout1, trans_w=True, ts=1024, name="mm_dy1")
    g["w_out1"] = _mm_tn([yc, yd], [dh3], ts=1024, nb=512, name="mm_dwout1")
    event("dwout1", g["w_out1"])
    dzu, dzg, g["sgu_w"], g["sgu_bias"], g["sgu_gn"] = _sgu_bwd(
        p1, dy1, sgu_gn, sgu_w, sgu_wt, sgu_bias, seg, tril, name="sgu_bwd")
    delta = _fox_delta(dy1, yd, sel, name="fox_delta")
    dq, dk, dv, dck, dcq = _fox_bwd(p1, dy1, lb, delta, ck, name="fox_bwd")
    event("fox_bwd", dq)
    dcs = jnp.pad(jnp.transpose(dck.reshape(8, s)), ((0, 0), (0, LANES - 8)))
    df, g["bf"] = _fcum_bwd(dcs, dcq, p1, bf, name="fcum_bwd")
    dp1 = jnp.concatenate([dzu, dzg, dq, dk, dv, df], axis=1)
    g["w_in1"] = _mm_tn([n2], [dp1], ts=1024, nb=896, name="mm_dwin1")
    event("dwin1", g["w_in1"])
    dh2, g["g_mix1"] = _mm([dp1], win1, trans_w=True, norm_bwd=(h2, wts["g_mix1"], dh3), name="mm_dn_mix1")

    dh1 = ffn_bwd(dh2, h1, n1, up0, act0, 0)

    dy0 = _mm([dh1], wout0, trans_w=True, ts=1024, name="mm_dy0")
    g["w_out0"] = _mm_tn([ya, yb], [dh1], ts=1024, nb=512, name="mm_dwout0")
    event("dwout0", g["w_out0"])
    (*dp0, g["w4"], g["b4"], g["wa"], g["ba"], g["wx"], g["bx"], g["lam"], g["w3"], g["b3"]) = _even_core_bwd(
        dy0, p0, hl, w4, b4, wa, wat, ba, wx, wxt, bx, lam, w3, b3, name="even_bwd")
    event("even_bwd", dp0[0])
    g["w_in0"] = _mm_tn([n0], dp0, ts=1024, nb=512, name="mm_dwin0")
    event("dwin0", g["w_in0"])
    grad_x, g["g_mix0"] = _mm(dp0, win0, trans_w=True, norm_bwd=(x, wts["g_mix0"], dh1), name="mm_dn_mix0")
    return loss, grad_x, g


def _late_cols_from_shards(stacked, token):
    stacked, _ = lax.optimization_barrier((stacked, token))
    return _cols_from_shards(stacked.reshape((N_CHIPS, stacked.shape[1] * stacked.shape[2], stacked.shape[3])))


def _prepare_weights(nat):
    lane = jnp.arange(LANES)
    tril = (lane[:, None] >= lane[None, :]).astype(F32)
    sgu_tril = nat["sgu_w"][0] * tril

    def w_in1(token):
        full = nat["mix1_w_in"](token) if callable(nat["mix1_w_in"]) else nat["mix1_w_in"]
        return jnp.pad(full, ((0, 0), (0, 21 * LANES - full.shape[1])))

    return {
        "w_in0": nat["mix0_w_in"],
        "w_out0": nat["mix0_w_out"],
        "w_in1": w_in1,
        "w_out1": nat["mix1_w_out"],
        "w_up": [nat["ffn_up"][l] for l in range(2)],
        "w_down": [nat["ffn_down"][l] for l in range(2)],
        "w4": nat["lru_conv_w"], "b4": nat["lru_conv_b"], "w3": nat["sconv_w"], "b3": nat["sconv_b"],
        "wa": _pair_blockdiag(nat["lru_wa"][0]).astype(BF16), "wx": _pair_blockdiag(nat["lru_wx"][0]).astype(BF16),
        "ba": nat["lru_ba"], "bx": nat["lru_bx"], "lam": nat["lru_lambda"],
        "ffn_cw": [nat["ffn_conv_w"][l] for l in range(2)],
        "ffn_cb": [nat["ffn_conv_b"][l:l + 1] for l in range(2)],
        "sgu_w": sgu_tril.astype(BF16), "sgu_wt": jnp.swapaxes(sgu_tril, 1, 2).astype(BF16),
        "sgu_bias": jnp.repeat(jnp.transpose(nat["sgu_b"][0]), 64, axis=1), "sgu_gn": nat["sgu_norm"],
        "bf": jnp.pad(nat["fox_bf"], ((0, 0), (0, LANES - 8))),
        "g_mix0": nat["mix0_norm"], "g_mix1": nat["mix1_norm"],
        "g_ffn": [nat["ffn_norm"][0:1], nat["ffn_norm"][1:2]], "g_final": nat["final_norm"].reshape(1, D_MODEL),
    }


def _natural_grads(g):
    small = {
        "mix0_norm": g["g_mix0"], "lru_conv_b": g["b4"],
        "lru_wa": _pair_diag_blocks(g["wa"])[None], "lru_ba": g["ba"],
        "lru_wx": _pair_diag_blocks(g["wx"])[None], "lru_bx": g["bx"],
        "lru_lambda": g["lam"], "sconv_b": g["b3"],
        "sgu_w": g["sgu_w"][None],
        "sgu_b": jnp.transpose(g["sgu_bias"].reshape(CHUNK, 8, 64).sum(axis=2))[None],
        "fox_bf": g["bf"][:, :8],
        "ffn_norm": jnp.concatenate([g["g_ffn0"], g["g_ffn1"]], axis=0),
        "ffn_conv_b": jnp.concatenate([g["ffn_cb0"], g["ffn_cb1"]], axis=0),
        "final_norm": g["final_norm"].reshape(D_MODEL),
        "lru_conv_w": g["w4"][None], "sconv_w": g["w3"][None],
        "ffn_conv_w": jnp.stack([g["ffn_cw0"], g["ffn_cw1"]]),
        "mix1_norm": g["g_mix1"], "sgu_norm": g["sgu_gn"],
    }
    big = {
        "mix0_w_in": g["w_in0"], "mix0_w_out": g["w_out0"],
        "mix1_w_in": g["w_in1"][:, :2568], "mix1_w_out": g["w_out1"],
        "ffn_up0": g["w_up0"], "ffn_up1": g["w_up1"],
        "ffn_down0": g["w_down0"], "ffn_down1": g["w_down1"],
    }
    return small, big


COL_SHARDED = ("mix0_w_in", "mix1_w_in", "ffn_up0", "ffn_up1")
COL_ALIGNED = ("mix0_w_in", "ffn_up0", "ffn_up1")
SMALL_SHARDED = ("lru_conv_w", "sconv_w", "ffn_conv_w", "mix1_norm", "sgu_norm")
SMALL_REPLICATED = ("mix0_norm", "lru_conv_b", "lru_wa", "lru_ba", "lru_wx", "lru_bx", "lru_lambda", "sconv_b",
                    "sgu_w", "sgu_b", "fox_bf", "ffn_norm", "ffn_conv_b", "final_norm")
WEIGHT_ORDER = ("mix0_norm", "mix0_w_in", "lru_conv_w", "lru_conv_b", "lru_wa", "lru_ba", "lru_wx", "lru_bx",
                "lru_lambda", "sconv_w", "sconv_b", "mix0_w_out", "mix1_norm", "mix1_w_in", "sgu_norm", "sgu_w",
                "sgu_b", "fox_bf", "mix1_w_out", "ffn_norm", "ffn_up", "ffn_conv_w", "ffn_conv_b", "ffn_down",
                "final_norm")


GATHER_GROUPS = (("mix0_w_in", "mix0_w_out"), ("ffn_up0",), ("ffn_down0", "mix1_w_in"),
                 ("mix1_w_out", "ffn_up1", "ffn_down1"))
CID_GATHER, CID_PAIR, CID_FIRST, CID_SECOND, CID_SWAP = 1, 2, 3, 4, 5


class _GradReducer:
    def __init__(self):
        x, y, c = _mesh_pos()
        self.send = jnp.stack([c] + [2 * (c * (1 - x) + (1 - c) * t) + (c * t + (1 - c) * (1 - y))
                                     for t in range(2)]).astype(jnp.int32)
        self.keep = jnp.stack([c] + [c * (2 * x + t) + (1 - c) * (2 * t + y) for t in range(2)]).astype(jnp.int32)
        self.mine = (c * y + (1 - c) * x).reshape(1).astype(jnp.int32)
        self.groups = {}

    @staticmethod
    def _view(name, a):
        if name in COL_ALIGNED:
            return a.reshape(2, a.shape[0] // 2, a.shape[1])
        if name in COL_SHARDED:
            a = _cols_to_shards(a)
            return a.reshape(N_CHIPS, 2, a.shape[1] // 2, a.shape[2])
        rows = a.shape[0] // (2 * N_CHIPS)
        return a.reshape(N_CHIPS, 2, rows, a.shape[1])

    def start(self, group, grads):
        names = tuple(grads)
        views = [self._view(k, grads[k]) for k in names]
        cols = [k in COL_ALIGNED for k in names]
        data = _send_other_half(views, cols, collective_id=CID_PAIR, name=f"rs_pair_{group}")
        self.groups[group] = dict(names=names, stage=0, views=views, cols=cols, data=data)

    def step(self, group, after):
        st = self.groups[group]
        names = st["names"]
        if st["stage"] == 0:
            sums = [_pair_sum(a, col, b, self.send, after, name=f"rs_pair_sum_{k}")
                    for k, a, col, b in zip(names, st["views"], st["cols"], st["data"])]
            st["from_sib"] = st["data"]
            st["data"] = _send_first(sums, collective_id=CID_FIRST, name=f"rs_first_{group}")
        elif st["stage"] == 1:
            sums = [_first_sum(a, col, b, r, self.keep, after, name=f"rs_first_sum_{k}")
                    for k, a, col, b, r in zip(names, st["views"], st["cols"], st["from_sib"], st["data"])]
            st["keep"] = [s32 for s32, _ in sums]
            st["data"] = _send_second([s16 for _, s16 in sums], collective_id=CID_SECOND, name=f"rs_second_{group}")
        else:
            st["mine"] = [_second_sum(s32, r, self.mine, after, name=f"rs_second_sum_{k}")
                          for k, s32, r in zip(names, st["keep"], st["data"])]
            st["data"] = _swap_halves(st["mine"], collective_id=CID_SWAP, name=f"rs_swap_{group}")
        st["stage"] += 1

    def result(self, group):
        st = self.groups[group]
        return {k: (a, b) for k, a, b in zip(st["names"], st["mine"], st["data"])}


def _train_step(x, target, w, m, v):
    x2 = x[0]
    t2 = target[0]
    chip = 2 * lax.axis_index("x") + lax.axis_index("y")
    core_arr = lax.axis_index("c").reshape(1).astype(jnp.int32)
    chip_arr = chip.reshape(1).astype(jnp.int32)

    big_shards = {
        "mix0_w_in": (w["mix0_w_in"], 0), "mix0_w_out": (w["mix0_w_out"], 0),
        "mix1_w_in": (w["mix1_w_in"], 0), "mix1_w_out": (w["mix1_w_out"], 0),
        "ffn_up0": (w["ffn_up"], 0), "ffn_up1": (w["ffn_up"], 1),
        "ffn_down0": (w["ffn_down"], 0), "ffn_down1": (w["ffn_down"], 1),
    }
    small_shards = [w[k] for k in SMALL_SHARDED]
    small_buf, small_offs = _pack(small_shards)
    full = {}
    small_all = None
    for gi, names in enumerate(GATHER_GROUPS):
        cols = [k in COL_ALIGNED for k in names]
        placed = [_place(*big_shards[k], col, chip_arr, BF16, name=f"place_{k}") for k, col in zip(names, cols)]
        if gi == 0:
            placed.append(_place(small_buf[None], 0, False, chip_arr, F32, name="place_small"))
            cols = cols + [False]
        gathered = _all_gather(placed, cols, collective_id=CID_GATHER, name=f"gather_weights{gi}")
        if gi == 0:
            small_all = gathered[-1].reshape(N_CHIPS, -1, LANES)
        for k, arr in zip(names, gathered):
            if k in COL_ALIGNED:
                full[k] = arr.reshape(arr.shape[0] * arr.shape[1], arr.shape[2])
            elif k in COL_SHARDED:
                full[k] = functools.partial(_late_cols_from_shards, arr)
            else:
                full[k] = arr.reshape(-1, arr.shape[3])
    per_chip = [_unpack(small_all[k], small_offs, [a.shape for a in small_shards]) for k in range(N_CHIPS)]
    lru_conv_w = jnp.concatenate([per_chip[k][0] for k in range(N_CHIPS)], axis=-1)[0]
    sconv_w = jnp.concatenate([per_chip[k][1] for k in range(N_CHIPS)], axis=-1)[0]
    ffn_conv_w = jnp.concatenate([per_chip[k][2] for k in range(N_CHIPS)], axis=-1)
    mix1_norm = jnp.concatenate([per_chip[k][3] for k in range(N_CHIPS)], axis=-1)
    sgu_norm = jnp.concatenate([per_chip[k][4] for k in range(N_CHIPS)], axis=-1)

    nat = {
        "mix0_w_in": full["mix0_w_in"], "mix0_w_out": full["mix0_w_out"],
        "mix1_w_in": full["mix1_w_in"], "mix1_w_out": full["mix1_w_out"],
        "ffn_up": [full["ffn_up0"], full["ffn_up1"]], "ffn_down": [full["ffn_down0"], full["ffn_down1"]],
        "lru_conv_w": lru_conv_w, "sconv_w": sconv_w, "ffn_conv_w": ffn_conv_w, "mix1_norm": mix1_norm,
        "sgu_norm": sgu_norm,
    }
    for k in SMALL_REPLICATED:
        nat[k] = w[k]
    wts = _prepare_weights(nat)

    reducer = _GradReducer()

    def on_event(name, g, token):
        if name == "dwup1":
            reducer.start("ffn1", {"ffn_up1": g["w_up1"], "ffn_down1": g["w_down1"]})
        elif name in ("dwout1", "fox_bwd"):
            reducer.step("ffn1", token)
        elif name == "dwin1":
            reducer.step("ffn1", token)
            reducer.start("mix1", {"mix1_w_in": g["w_in1"][:, :2568], "mix1_w_out": g["w_out1"]})
        elif name in ("dwdown0", "ffn_bwd0"):
            reducer.step("mix1", token)
        elif name == "dwup0":
            reducer.step("mix1", token)
            reducer.start("ffn0", {"ffn_up0": g["w_up0"], "ffn_down0": g["w_down0"]})
        elif name in ("dwout0", "even_bwd"):
            reducer.step("ffn0", token)
        elif name == "dwin0":
            reducer.step("ffn0", token)
            reducer.start("mix0", {"mix0_w_in": g["w_in0"], "mix0_w_out": g["w_out0"]})

    loss, grad_x, g = _local_step(x2, t2, wts, on_event)
    grads_small, _ = _natural_grads(g)

    small_names = SMALL_REPLICATED + SMALL_SHARDED
    small_list = [grads_small[k] for k in small_names] + [loss[:, :1]]
    sbuf, soffs = _pack(small_list)
    sred = _all_reduce_small(sbuf, name="reduce_small")
    small_red = _unpack(sred, soffs, [a.shape for a in small_list])
    loss_total = small_red[-1][0, 0]
    gsum = dict(zip(small_names, small_red[:-1]))
    for k in SMALL_SHARDED:
        width = w[k].shape[-1]
        gsum[k] = lax.dynamic_slice_in_dim(gsum[k], chip * width, width, axis=gsum[k].ndim - 1)

    out_g, out_d, out_m, out_v = {}, {}, {}, {}
    reduced = {}
    for group in ("ffn1", "mix1", "ffn0"):
        reduced.update(reducer.result(group))

    def update(pname, keys):
        mine = [reduced[k][0] for k in keys]
        theirs = [reduced[k][1] for k in keys]
        out_g[pname], out_d[pname], out_m[pname], out_v[pname] = _adamw_halves(
            w[pname], mine, theirs, m[pname], v[pname], core_arr, name=f"adamw_{pname}")
        return out_d[pname]

    reducer.step("mix0", update("ffn_up", ("ffn_up0", "ffn_up1")))
    small_new = _adamw_many([w[k] for k in small_names], [gsum[k] for k in small_names],
                            [m[k] for k in small_names], [v[k] for k in small_names], name="adamw_small")
    reducer.step("mix0", update("ffn_down", ("ffn_down0", "ffn_down1")))
    update("mix1_w_in", ("mix1_w_in",))
    reducer.step("mix0", update("mix1_w_out", ("mix1_w_out",)))
    reduced.update(reducer.result("mix0"))
    update("mix0_w_in", ("mix0_w_in",))
    update("mix0_w_out", ("mix0_w_out",))

    for k, (dd, mm, vv) in zip(small_names, small_new):
        out_g[k], out_d[k], out_m[k], out_v[k] = gsum[k].reshape(w[k].shape), dd, mm, vv

    outs = [loss_total, grad_x[None]]
    for d in (out_g, out_d, out_m, out_v):
        outs.extend(d[k] for k in WEIGHT_ORDER)
    return tuple(outs)


def kernel(x, mix0_norm, mix0_w_in, lru_conv_w, lru_conv_b, lru_wa, lru_ba, lru_wx, lru_bx, lru_lambda, sconv_w, sconv_b, mix0_w_out, mix1_norm, mix1_w_in, sgu_norm, sgu_w, sgu_b, fox_bf, mix1_w_out, ffn_norm, ffn_up, ffn_conv_w, ffn_conv_b, ffn_down, final_norm, loss_target, m_mix0_norm, m_mix0_w_in, m_lru_conv_w, m_lru_conv_b, m_lru_wa, m_lru_ba, m_lru_wx, m_lru_bx, m_lru_lambda, m_sconv_w, m_sconv_b, m_mix0_w_out, m_mix1_norm, m_mix1_w_in, m_sgu_norm, m_sgu_w, m_sgu_b, m_fox_bf, m_mix1_w_out, m_ffn_norm, m_ffn_up, m_ffn_conv_w, m_ffn_conv_b, m_ffn_down, m_final_norm, v_mix0_norm, v_mix0_w_in, v_lru_conv_w, v_lru_conv_b, v_lru_wa, v_lru_ba, v_lru_wx, v_lru_bx, v_lru_lambda, v_sconv_w, v_sconv_b, v_mix0_w_out, v_mix1_norm, v_mix1_w_in, v_sgu_norm, v_sgu_w, v_sgu_b, v_fox_bf, v_mix1_w_out, v_ffn_norm, v_ffn_up, v_ffn_conv_w, v_ffn_conv_b, v_ffn_down, v_final_norm):
    w = dict(zip(WEIGHT_ORDER, (mix0_norm, mix0_w_in, lru_conv_w, lru_conv_b, lru_wa, lru_ba, lru_wx, lru_bx, lru_lambda, sconv_w, sconv_b, mix0_w_out, mix1_norm, mix1_w_in, sgu_norm, sgu_w, sgu_b, fox_bf, mix1_w_out, ffn_norm, ffn_up, ffn_conv_w, ffn_conv_b, ffn_down, final_norm)))
    m = dict(zip(WEIGHT_ORDER, (m_mix0_norm, m_mix0_w_in, m_lru_conv_w, m_lru_conv_b, m_lru_wa, m_lru_ba, m_lru_wx, m_lru_bx, m_lru_lambda, m_sconv_w, m_sconv_b, m_mix0_w_out, m_mix1_norm, m_mix1_w_in, m_sgu_norm, m_sgu_w, m_sgu_b, m_fox_bf, m_mix1_w_out, m_ffn_norm, m_ffn_up, m_ffn_conv_w, m_ffn_conv_b, m_ffn_down, m_final_norm)))
    v = dict(zip(WEIGHT_ORDER, (v_mix0_norm, v_mix0_w_in, v_lru_conv_w, v_lru_conv_b, v_lru_wa, v_lru_ba, v_lru_wx, v_lru_bx, v_lru_lambda, v_sconv_w, v_sconv_b, v_mix0_w_out, v_mix1_norm, v_mix1_w_in, v_sgu_norm, v_sgu_w, v_sgu_b, v_fox_bf, v_mix1_w_out, v_ffn_norm, v_ffn_up, v_ffn_conv_w, v_ffn_conv_b, v_ffn_down, v_final_norm)))
    return _train_step(x, loss_target, w, m, v)
```

```python
import functools

import jax
import jax.numpy as jnp
from jax import lax
from jax.experimental import pallas as pl
from jax.experimental.pallas import tpu as pltpu
from jax.experimental.pallas import tpu_sc as plsc

F32 = jnp.float32
BF16 = jnp.bfloat16
MESH = pl.DeviceIdType.MESH

D_MODEL = 1024
LANES = 128
SUBLANES = 8
N_CHIPS = 4
EPS = 1e-6
LRU_C = 8.0
D_FF = 2816
FFN_CB = 256
CHUNK = 128
NEG = -1e30

ADAM_LR = 0.001
ADAM_B1 = 0.9
ADAM_B2 = 0.999
ADAM_EPS = 1e-08
ADAM_WD = 0.01
ADAM_STEP = 10
ADAM_C1 = 1.0 - ADAM_B1 ** ADAM_STEP
ADAM_C2 = 1.0 - ADAM_B2 ** ADAM_STEP

_GELU_C = 0.7978845608028654
_GELU_A = 0.044715


def _sigmoid(x):
    return 1.0 / (1.0 + jnp.exp(-x))


def _sigmoid_tanh(x):
    return 0.5 * jnp.tanh(0.5 * x) + 0.5


def _log1p_pos(e):
    w = 1.0 + e
    return jnp.where(w == 1.0, e, jnp.log(w) * (e / (w - 1.0)))


def _softplus(x):
    return jnp.maximum(x, 0.0) + _log1p_pos(jnp.exp(-jnp.abs(x)))


def _gelu(x):
    t = jnp.tanh(_GELU_C * (x + _GELU_A * (x * x * x)))
    return 0.5 * x * (1.0 + t), t


def _gelu_grad(x, t):
    return 0.5 * (1.0 + t) + 0.5 * x * (1.0 - t * t) * (_GELU_C * (1.0 + 3.0 * _GELU_A * x * x))


def _rows(shape):
    return lax.broadcasted_iota(jnp.int32, shape, 0)


def _lanes(shape):
    return lax.broadcasted_iota(jnp.int32, shape, 1)


def _shift_down(x, halo8, j):
    if j == 0:
        return x
    r = pltpu.roll(x, j, 0)
    hr = pltpu.roll(halo8, j, 0)
    top = jnp.where(_rows(hr.shape) < j, hr, r[:SUBLANES])
    return jnp.concatenate([top, r[SUBLANES:]], axis=0)


def _shift_up(x, next8, j):
    if j == 0:
        return x
    n = x.shape[0]
    r = pltpu.roll(x, n - j, 0)
    nr = pltpu.roll(next8, SUBLANES - j, 0)
    bot = jnp.where(_rows(nr.shape) >= SUBLANES - j, nr, r[n - SUBLANES:])
    return jnp.concatenate([r[:n - SUBLANES], bot], axis=0)


def _scan_fwd(a, u):
    n = a.shape[0]
    row = _rows(a.shape)
    h = u
    k = 1
    while k < n:
        keep = row >= k
        h_sh = jnp.where(keep, pltpu.roll(h, k, 0), 0.0)
        a_sh = jnp.where(keep, pltpu.roll(a, k, 0), 1.0)
        h = a * h_sh + h
        a = a * a_sh
        k *= 2
    return h, a


def _scan_rev(b, d):
    n = b.shape[0]
    row = _rows(b.shape)
    g = d
    k = 1
    while k < n:
        keep = row < n - k
        g_sh = jnp.where(keep, pltpu.roll(g, n - k, 0), 0.0)
        b_sh = jnp.where(keep, pltpu.roll(b, n - k, 0), 1.0)
        g = b * g_sh + g
        b = b * b_sh
        k *= 2
    return g, b


def _cumsum_fwd(x):
    n = x.shape[0]
    row = _rows(x.shape)
    k = 1
    while k < n:
        x = x + jnp.where(row >= k, pltpu.roll(x, k, 0), 0.0)
        k *= 2
    return x


def _cumsum_rev(x):
    n = x.shape[0]
    row = _rows(x.shape)
    k = 1
    while k < n:
        x = x + jnp.where(row < n - k, pltpu.roll(x, n - k, 0), 0.0)
        k *= 2
    return x


def _dot(a, b):
    return lax.dot_general(a, b, (((1,), (0,)), ((), ())), preferred_element_type=F32)


def _dot_nt(a, b):
    return lax.dot_general(a, b, (((1,), (1,)), ((), ())), preferred_element_type=F32)


def _dot_tn(a, b):
    return lax.dot_general(a, b, (((0,), (0,)), ((), ())), preferred_element_type=F32)


def _dot_split(x, m_bf16):
    hi = x.astype(BF16)
    lo = (x - hi.astype(F32)).astype(BF16)
    return _dot(hi, m_bf16) + _dot(lo, m_bf16)


def _tile_rows(ts, s):
    return min(ts, s)


def _mm(a_list, w, *, trans_w=False, res=None, norm_bwd=None, norm_out=None, out_dtype=F32, ts=512, nb=None,
        name):
    s = a_list[0].shape[0]
    ks = [a.shape[1] for a in a_list]
    k = sum(ks)
    n = w.shape[0] if trans_w else w.shape[1]
    ts = _tile_rows(ts, s)
    nb = n if nb is None else nb
    na = len(a_list)
    has_res = res is not None
    fused = norm_bwd is not None
    normed = norm_out is not None
    offs = [sum(ks[:p]) for p in range(na)]

    def body(*refs):
        a_refs = refs[:na]
        w_ref = refs[na]
        acc = None
        for a_ref, off, kk in zip(a_refs, offs, ks):
            a = a_ref[...].astype(BF16)
            if trans_w:
                part = _dot_nt(a, w_ref[:, off:off + kk])
            else:
                part = _dot(a, w_ref[off:off + kk, :])
            acc = part if acc is None else acc + part
        if has_res:
            acc = acc + refs[na + 1][...]
        if normed:
            gn_ref, o_ref, n_ref = refs[-3:]
            o_ref[...] = acc.astype(out_dtype)
            r = lax.rsqrt(jnp.mean(acc * acc, axis=-1, keepdims=True) + EPS)
            n_ref[...] = ((acc * r) * gn_ref[...]).astype(BF16)
            return
        if not fused:
            refs[-1][...] = acc.astype(out_dtype)
            return
        h_ref, g_ref, dres_ref, dh_ref, dg_ref = refs[na + 1:]
        i = pl.program_id(1)
        x = h_ref[...]
        r = lax.rsqrt(jnp.mean(x * x, axis=-1, keepdims=True) + EPS)
        xhat = x * r
        part = jnp.sum(acc * xhat, axis=0, keepdims=True)

        @pl.when(i == 0)
        def _():
            dg_ref[...] = part

        @pl.when(i > 0)
        def _():
            dg_ref[...] += part

        dxh = acc * g_ref[...]
        dh_ref[...] = dres_ref[...] + r * (dxh - xhat * jnp.mean(dxh * xhat, axis=-1, keepdims=True))

    in_specs = [pl.BlockSpec((ts, kk), lambda j, i: (i, 0)) for kk in ks]
    if trans_w:
        in_specs.append(pl.BlockSpec((nb, k), lambda j, i: (j, 0)))
    else:
        in_specs.append(pl.BlockSpec((k, nb), lambda j, i: (0, j)))
    args = list(a_list) + [w]
    tile = pl.BlockSpec((ts, nb), lambda j, i: (i, j))
    if has_res:
        in_specs.append(tile)
        args.append(res)
    if fused:
        assert nb == n and not has_res
        vec = pl.BlockSpec((1, n), lambda j, i: (0, 0))
        h, g, dres = norm_bwd
        return pl.pallas_call(
            body, name=name, grid=(1, s // ts), in_specs=in_specs + [tile, vec, tile],
            out_specs=(tile, vec),
            out_shape=(jax.ShapeDtypeStruct((s, n), F32), jax.ShapeDtypeStruct((1, n), F32)),
        )(*args, h, g, dres)
    if normed:
        assert nb == n and out_dtype == F32
        vec = pl.BlockSpec((1, n), lambda j, i: (0, 0))
        return pl.pallas_call(
            body, name=name, grid=(1, s // ts), in_specs=in_specs + [vec], out_specs=(tile, tile),
            out_shape=(jax.ShapeDtypeStruct((s, n), F32), jax.ShapeDtypeStruct((s, n), BF16)),
        )(*args, norm_out)
    return pl.pallas_call(
        body, name=name, grid=(n // nb, s // ts), in_specs=in_specs, out_specs=tile,
        out_shape=jax.ShapeDtypeStruct((s, n), out_dtype),
    )(*args)


def _mm_tn(a_list, b_list, *, ts=512, nb=None, name):
    s = b_list[0].shape[0]
    ks = [a.shape[1] for a in a_list]
    k = sum(ks)
    width = b_list[0].shape[1]
    n = width * len(b_list)
    ts = _tile_rows(ts, s)
    nb = width if nb is None else nb
    per = width // nb
    na = len(a_list)
    nparts = len(b_list)

    def body(*refs):
        a_refs = refs[:na]
        b_refs = refs[na:na + nparts]
        o_ref = refs[-1]
        j = pl.program_id(0)
        i = pl.program_id(1)
        parts = [r[...].astype(BF16) for r in a_refs]
        a = parts[0] if na == 1 else jnp.concatenate(parts, axis=1)

        def accumulate(b_ref):
            upd = _dot_tn(a, b_ref[...].astype(BF16))

            @pl.when(i == 0)
            def _():
                o_ref[...] = upd

            @pl.when(i > 0)
            def _():
                o_ref[...] += upd

        if nparts == 1:
            accumulate(b_refs[0])
        else:
            for part, b_ref in enumerate(b_refs):
                pl.when(j // per == part)(functools.partial(accumulate, b_ref))

    in_specs = [pl.BlockSpec((ts, kk), lambda j, i: (i, 0)) for kk in ks]
    for part in range(nparts):
        in_specs.append(pl.BlockSpec(
            (ts, nb), lambda j, i, part=part: (i, jnp.clip(j - part * per, 0, per - 1))))
    return pl.pallas_call(
        body, name=name, grid=(n // nb, s // ts), in_specs=in_specs,
        out_specs=pl.BlockSpec((k, nb), lambda j, i: (0, j)),
        out_shape=jax.ShapeDtypeStruct((k, n), F32),
    )(*a_list, *b_list)


def _norm_fwd(h, g, *, ts=512, name):
    s, d = h.shape
    ts = _tile_rows(ts, s)

    def body(h_ref, g_ref, n_ref):
        x = h_ref[...]
        r = lax.rsqrt(jnp.mean(x * x, axis=-1, keepdims=True) + EPS)
        n_ref[...] = ((x * r) * g_ref[...]).astype(BF16)

    return pl.pallas_call(
        body, name=name, grid=(s // ts,),
        in_specs=[pl.BlockSpec((ts, d), lambda i: (i, 0)), pl.BlockSpec((1, d), lambda i: (0, 0))],
        out_specs=pl.BlockSpec((ts, d), lambda i: (i, 0)),
        out_shape=jax.ShapeDtypeStruct((s, d), BF16),
    )(h, g)


def _final(h, g, target, *, ts=512, name):
    s, d = h.shape
    ts = _tile_rows(ts, s)
    nt = s // ts

    def body(h_ref, g_ref, t_ref, dh_ref, loss_ref, dg_ref, acc_ref):
        i = pl.program_id(0)
        x = h_ref[...]
        r = lax.rsqrt(jnp.mean(x * x, axis=-1, keepdims=True) + EPS)
        xhat = x * r
        gv = g_ref[...]
        err = xhat * gv - t_ref[...]
        sq = jnp.sum(err * err, axis=0, keepdims=True)
        dy = err * (1.0 / d)
        part = jnp.sum(dy * xhat, axis=0, keepdims=True)

        @pl.when(i == 0)
        def _():
            acc_ref[...] = sq
            dg_ref[...] = part

        @pl.when(i > 0)
        def _():
            acc_ref[...] += sq
            dg_ref[...] += part

        dxh = dy * gv
        dh_ref[...] = r * (dxh - xhat * jnp.mean(dxh * xhat, axis=-1, keepdims=True))

        @pl.when(i == nt - 1)
        def _():
            tot = jnp.sum(acc_ref[...], axis=1, keepdims=True) * (0.5 / d)
            loss_ref[...] = jnp.broadcast_to(tot, (1, LANES))

    tile = pl.BlockSpec((ts, d), lambda i: (i, 0))
    vec = pl.BlockSpec((1, d), lambda i: (0, 0))
    return pl.pallas_call(
        body, name=name, grid=(nt,), in_specs=[tile, vec, tile],
        out_specs=(tile, pl.BlockSpec((1, LANES), lambda i: (0, 0)), vec),
        out_shape=(jax.ShapeDtypeStruct((s, d), F32), jax.ShapeDtypeStruct((1, LANES), F32),
                   jax.ShapeDtypeStruct((1, d), F32)),
        scratch_shapes=[pltpu.VMEM((1, d), F32)],
    )(h, g, target)


def _halo_map(ts, width_blocks):
    per = ts // SUBLANES

    def index(j, i):
        return (jnp.maximum(i * per - 1, 0), width_blocks(j))

    return index


def _even_gates(xc, wa, ba, wx, bx, sp):
    xb = xc.astype(BF16)
    r = _sigmoid(_dot(xb, wa) + ba)
    ig = _sigmoid(_dot(xb, wx) + bx)
    la = (-LRU_C) * r * sp
    a = jnp.exp(la)
    a2 = a * a
    m = jnp.sqrt(-jnp.tanh(la) * (1.0 + a2))
    return r, ig, la, a, a2, m


def _even_core_fwd(p, w4, b4, wa, ba, wx, bx, lam, w3, b3, *, ts=512, name):
    s = p.shape[0]
    ts = _tile_rows(ts, s)
    nt = s // ts
    nblk = 4

    def body(xa_ref, ga_ref, cp_ref, bp_ref, vb_ref, xah_ref, cph_ref, vbh_ref,
             w4_ref, b4_ref, wa_ref, ba_ref, wx_ref, bx_ref, lam_ref, w3_ref, b3_ref,
             ya_ref, yb_ref, hl_ref, hcar_ref):
        i = pl.program_id(1)
        first = (i > 0).astype(F32)
        xa, ga, cp, bp, vb = xa_ref[...], ga_ref[...], cp_ref[...], bp_ref[...], vb_ref[...]
        xa_h = xah_ref[...] * first
        s_h = cph_ref[...] * vbh_ref[...] * first

        xc = b4_ref[...] + w4_ref[3:4, :] * xa
        for k in range(3):
            xc = xc + w4_ref[k:k + 1, :] * _shift_down(xa, xa_h, 3 - k)
        sp = _softplus(-lam_ref[...])
        _, ig, _, a, _, m = _even_gates(xc, wa_ref[0], ba_ref[...], wx_ref[0], bx_ref[...], sp)
        u = m * (ig * xc)
        hs, acum = _scan_fwd(a, u)

        @pl.when(i == 0)
        def _():
            hcar_ref[...] = jnp.zeros_like(hcar_ref)

        hs = hs + acum * hcar_ref[0:1, :]
        hl_ref[...] = hs
        hcar_ref[0:1, :] = hl_ref[ts - 1:ts, :]
        ge, _ = _gelu(ga)
        ya_ref[...] = (hs * ge).astype(BF16)

        sv = cp * vb
        sc = b3_ref[...] + w3_ref[2:3, :] * sv
        for k in range(2):
            sc = sc + w3_ref[k:k + 1, :] * _shift_down(sv, s_h, 2 - k)
        yb_ref[...] = (bp * sc).astype(BF16)

    parts = [pl.BlockSpec((ts, LANES), lambda j, i, q=q: (i, 4 * q + j)) for q in range(5)]
    halos = [pl.BlockSpec((SUBLANES, LANES), _halo_map(ts, lambda j, q=q: 4 * q + j)) for q in (0, 2, 4)]
    vec = pl.BlockSpec((1, LANES), lambda j, i: (0, j))
    out = pl.BlockSpec((ts, LANES), lambda j, i: (i, j))
    return pl.pallas_call(
        body, name=name, grid=(nblk, nt),
        in_specs=parts + halos + [
                  pl.BlockSpec((4, LANES), lambda j, i: (0, j)), vec,
                  pl.BlockSpec((1, LANES, LANES), lambda j, i: (j, 0, 0)), vec,
                  pl.BlockSpec((1, LANES, LANES), lambda j, i: (j, 0, 0)), vec, vec,
                  pl.BlockSpec((3, LANES), lambda j, i: (0, j)), vec],
        out_specs=(out, out, out),
        out_shape=(jax.ShapeDtypeStruct((s, 4 * LANES), BF16), jax.ShapeDtypeStruct((s, 4 * LANES), BF16),
                   jax.ShapeDtypeStruct((s, 4 * LANES), F32)),
        scratch_shapes=[pltpu.VMEM((SUBLANES, LANES), F32)],
    )(*([p] * 8), w4, b4, wa, ba, wx, bx, lam, w3, b3)


def _even_core_bwd(dy, p, hl, w4, b4, wa, wat, ba, wx, wxt, bx, lam, w3, b3, *, ts=512, name):
    s = p.shape[0]
    ts = _tile_rows(ts, s)
    nt = s // ts
    nblk = 4
    per = ts // SUBLANES

    def body(dya_ref, dyb_ref, xa_ref, ga_ref, cp_ref, bp_ref, vb_ref, xah_ref, cph_ref, vbh_ref, hl_ref, hh_ref,
             w4_ref, b4_ref, wa_ref, wat_ref, ba_ref, wx_ref, wxt_ref, bx_ref, lam_ref, w3_ref, b3_ref,
             dxa_ref, dga_ref, dcp_ref, dbp_ref, dvb_ref,
             dw4_ref, db4_ref, dwa_ref, dba_ref, dwx_ref, dbx_ref, dlam_ref, dw3_ref, db3_ref,
             dxc_nx, dsc_nx, cg_ref):
        i = pl.program_id(1)
        ti = nt - 1 - i
        first = (ti > 0).astype(F32)
        xa, ga, cp, bp, vb = xa_ref[...], ga_ref[...], cp_ref[...], bp_ref[...], vb_ref[...]
        xa_h = xah_ref[...] * first
        s_h = cph_ref[...] * vbh_ref[...] * first
        h_h = hh_ref[...] * first

        @pl.when(i == 0)
        def _():
            dxc_nx[...] = jnp.zeros_like(dxc_nx)
            dsc_nx[...] = jnp.zeros_like(dsc_nx)
            cg_ref[...] = jnp.zeros_like(cg_ref)
            for ref in (dw4_ref, db4_ref, dwa_ref, dba_ref, dwx_ref, dbx_ref, dlam_ref, dw3_ref, db3_ref):
                ref[...] = jnp.zeros_like(ref)

        xa_sh = [_shift_down(xa, xa_h, 3 - k) for k in range(3)] + [xa]
        xc = b4_ref[...]
        for k in range(4):
            xc = xc + w4_ref[k:k + 1, :] * xa_sh[k]
        lamv = lam_ref[...]
        sp = _softplus(-lamv)
        r, ig, _, a, a2, m = _even_gates(xc, wa_ref[0], ba_ref[...], wx_ref[0], bx_ref[...], sp)
        sv = cp * vb
        sv_sh = [_shift_down(sv, s_h, 2 - k) for k in range(2)] + [sv]
        sc = b3_ref[...]
        for k in range(3):
            sc = sc + w3_ref[k:k + 1, :] * sv_sh[k]
        hs = hl_ref[...]
        h_prev = _shift_down(hs, h_h, 1)

        dya = dya_ref[...]
        dyb = dyb_ref[...]
        ge, gt = _gelu(ga)
        dga = dya * hs * _gelu_grad(ga, gt)
        dh = dya * ge

        ones8 = jnp.ones((SUBLANES, LANES), F32)
        b = _shift_up(a, ones8, 1)
        g, bcum = _scan_rev(b, dh)
        g = g + bcum * cg_ref[0:1, :]
        ag = a * g
        cg_ref[...] = ag[:SUBLANES]

        da = g * h_prev
        xi = ig * xc
        dm = g * xi
        dig = g * m * xc
        dxc = g * m * ig
        dla = da * a - dm * (a2 / m)
        dr = dla * ((-LRU_C) * sp)
        dlam_ref[...] += jnp.sum(dla * r, axis=0, keepdims=True) * (LRU_C * _sigmoid(-lamv))
        dra = dr * r * (1.0 - r)
        dia = dig * ig * (1.0 - ig)
        drab = dra.astype(BF16)
        diab = dia.astype(BF16)
        xcb = xc.astype(BF16)
        dxc = dxc + _dot(drab, wat_ref[0]) + _dot(diab, wxt_ref[0])
        dwa_ref[0] += _dot_tn(xcb, drab)
        dwx_ref[0] += _dot_tn(xcb, diab)
        dba_ref[...] += jnp.sum(dra, axis=0, keepdims=True)
        dbx_ref[...] += jnp.sum(dia, axis=0, keepdims=True)

        nx = dxc_nx[...]
        dxa = w4_ref[3:4, :] * dxc
        for k in range(3):
            dxa = dxa + w4_ref[k:k + 1, :] * _shift_up(dxc, nx, 3 - k)
        for k in range(4):
            dw4_ref[k:k + 1, :] += jnp.sum(dxc * xa_sh[k], axis=0, keepdims=True)
        db4_ref[...] += jnp.sum(dxc, axis=0, keepdims=True)
        dxc_nx[...] = dxc[:SUBLANES]

        dbp = dyb * sc
        dsc = dyb * bp
        nsc = dsc_nx[...]
        ds = w3_ref[2:3, :] * dsc
        for k in range(2):
            ds = ds + w3_ref[k:k + 1, :] * _shift_up(dsc, nsc, 2 - k)
        for k in range(3):
            dw3_ref[k:k + 1, :] += jnp.sum(dsc * sv_sh[k], axis=0, keepdims=True)
        db3_ref[...] += jnp.sum(dsc, axis=0, keepdims=True)
        dsc_nx[...] = dsc[:SUBLANES]

        dxa_ref[...] = dxa.astype(BF16)
        dga_ref[...] = dga.astype(BF16)
        dcp_ref[...] = (ds * vb).astype(BF16)
        dbp_ref[...] = dbp.astype(BF16)
        dvb_ref[...] = (ds * cp).astype(BF16)

    def rev(j, i):
        return (nt - 1 - i, j)

    def rev_halo(col):
        def index(j, i):
            return (jnp.maximum((nt - 1 - i) * per - 1, 0), col(j))
        return index

    parts = [pl.BlockSpec((ts, LANES), lambda j, i, q=q: (nt - 1 - i, 4 * q + j)) for q in range(5)]
    halos = [pl.BlockSpec((SUBLANES, LANES), rev_halo(lambda j, q=q: 4 * q + j)) for q in (0, 2, 4)]
    one = pl.BlockSpec((ts, LANES), rev)
    vec = pl.BlockSpec((1, LANES), lambda j, i: (0, j))
    mat = pl.BlockSpec((1, LANES, LANES), lambda j, i: (j, 0, 0))
    w4s = pl.BlockSpec((4, LANES), lambda j, i: (0, j))
    w3s = pl.BlockSpec((3, LANES), lambda j, i: (0, j))
    f = jax.ShapeDtypeStruct
    return pl.pallas_call(
        body, name=name, grid=(nblk, nt),
        in_specs=[one, pl.BlockSpec((ts, LANES), lambda j, i: (nt - 1 - i, 4 + j))] + parts + halos + [
                  one, pl.BlockSpec((SUBLANES, LANES), rev_halo(lambda j: j)),
                  w4s, vec, mat, mat, vec, mat, mat, vec, vec, w3s, vec],
        out_specs=(one,) * 5 + (w4s, vec, mat, vec, mat, vec, vec, w3s, vec),
        out_shape=(f((s, 4 * LANES), BF16),) * 5 + (
                   f((4, 4 * LANES), F32), f((1, 4 * LANES), F32),
                   f((4, LANES, LANES), F32), f((1, 4 * LANES), F32),
                   f((4, LANES, LANES), F32), f((1, 4 * LANES), F32), f((1, 4 * LANES), F32),
                   f((3, 4 * LANES), F32), f((1, 4 * LANES), F32)),
        scratch_shapes=[pltpu.VMEM((SUBLANES, LANES), F32), pltpu.VMEM((SUBLANES, LANES), F32),
                        pltpu.VMEM((SUBLANES, LANES), F32)],
    )(dy, dy, *([p] * 8), hl, hl, w4, b4, wa, wat, ba, wx, wxt, bx, lam, w3, b3)


def _ffn_conv(u_ref, uh_ref, w_ref, b_ref, first):
    u = u_ref[...].astype(F32)
    u_h = uh_ref[...].astype(F32)[SUBLANES:] * first
    u_sh = [_shift_down(u, u_h, 2 - k) for k in range(2)] + [u]
    hc = b_ref[...]
    for k in range(3):
        hc = hc + w_ref[k:k + 1, :] * u_sh[k]
    return hc, u_sh


def _ffn_specs(ts, row, halo_row):
    nblk = D_FF // FFN_CB
    specs = []
    for off in (0, nblk):
        specs.append(pl.BlockSpec((ts, FFN_CB), lambda j, i, off=off: (row(i), off + j)))
        specs.append(pl.BlockSpec((16, FFN_CB), lambda j, i, off=off: (halo_row(i), off + j)))
        specs.append(pl.BlockSpec((3, FFN_CB), lambda j, i, off=off: (0, off + j)))
        specs.append(pl.BlockSpec((1, FFN_CB), lambda j, i, off=off: (0, off + j)))
    return specs


FFN_STRIP = 4 * SUBLANES
FFN_HALO = 2 * SUBLANES


def _ffn_stage(u_ref, uh_ref, dst_ref, first):
    dst_ref[0:FFN_HALO, :] = jnp.where(first, uh_ref[...], jnp.zeros_like(uh_ref))
    dst_ref[FFN_HALO:, :] = u_ref[...]


def _ffn_strip_conv(u_ref, r, w, b):
    win = u_ref[pl.ds(r, FFN_HALO + FFN_STRIP), :].astype(F32)
    cur, before = win[FFN_HALO:], win[SUBLANES:FFN_HALO]
    sh = [_shift_down(cur, before, 2 - k) for k in range(2)] + [cur]
    return b + w[0:1] * sh[0] + w[1:2] * sh[1] + w[2:3] * sh[2], sh


def _ffn_core_fwd(up, w, b, *, ts=512, name):
    s = up.shape[0]
    ts = _tile_rows(ts, s)
    nt = s // ts
    nblk = D_FF // FFN_CB
    per = ts // 16

    def body(g_ref, gh_ref, wg_ref, bg_ref, v_ref, vh_ref, wv_ref, bv_ref, act_ref):
        first = (pl.program_id(1) > 0).astype(F32)
        gate, _ = _ffn_conv(g_ref, gh_ref, wg_ref, bg_ref, first)
        val, _ = _ffn_conv(v_ref, vh_ref, wv_ref, bv_ref, first)
        act_ref[...] = (gate * _sigmoid_tanh(gate) * val).astype(BF16)

    return pl.pallas_call(
        body, name=name, grid=(nblk, nt),
        in_specs=_ffn_specs(ts, lambda i: i, lambda i: jnp.maximum(i * per - 1, 0)),
        out_specs=pl.BlockSpec((ts, FFN_CB), lambda j, i: (i, j)),
        out_shape=jax.ShapeDtypeStruct((s, D_FF), BF16),
    )(up, up, w, b, up, up, w, b)


def _ffn_core_bwd(dact, up, w, b, *, ts=1024, name):
    s = up.shape[0]
    ts = _tile_rows(ts, s)
    nt = s // ts
    nblk = D_FF // FFN_CB
    per = ts // 16
    strip, halo = FFN_STRIP, FFN_HALO
    nstrips = ts // strip

    def fold(x):
        out = x[:SUBLANES]
        for r0 in range(SUBLANES, strip, SUBLANES):
            out = out + x[r0:r0 + SUBLANES]
        return out

    def body(da_ref, g_ref, gh_ref, wg_ref, bg_ref, v_ref, vh_ref, wv_ref, bv_ref,
             dg_ref, dv_ref, dwg_ref, dwv_ref, dbg_ref, dbv_ref, nxg_ref, nxv_ref, ug_ref, uv_ref):
        i = pl.program_id(1)
        first = nt - 1 - i > 0

        @pl.when(i == 0)
        def _():
            for ref in (nxg_ref, nxv_ref, dwg_ref, dwv_ref, dbg_ref, dbv_ref):
                ref[...] = jnp.zeros_like(ref)

        _ffn_stage(g_ref, gh_ref, ug_ref, first)
        _ffn_stage(v_ref, vh_ref, uv_ref, first)
        wg, wv, bg, bv = wg_ref[...], wv_ref[...], bg_ref[...], bv_ref[...]
        conv = _ffn_strip_conv

        def conv_t(d, nxt, w):
            out = w[2:3] * d
            for k in range(2):
                out = out + w[k:k + 1] * _shift_up(d, nxt, 2 - k)
            return out

        def step(t, carry):
            nxg, nxv, awg, awv, abg, abv = carry
            r = pl.multiple_of((nstrips - 1 - t) * strip, strip)
            gate, g_sh = conv(ug_ref, r, wg, bg)
            val, v_sh = conv(uv_ref, r, wv, bv)
            da = da_ref[pl.ds(r, strip), :].astype(F32)
            sg = _sigmoid_tanh(gate)
            dgate = da * val * (sg * (1.0 + gate * (1.0 - sg)))
            dval = da * (gate * sg)
            dg_ref[pl.ds(r, strip), :] = conv_t(dgate, nxg, wg).astype(BF16)
            dv_ref[pl.ds(r, strip), :] = conv_t(dval, nxv, wv).astype(BF16)
            awg = tuple(a + fold(dgate * sh) for a, sh in zip(awg, g_sh))
            awv = tuple(a + fold(dval * sh) for a, sh in zip(awv, v_sh))
            return dgate[:SUBLANES], dval[:SUBLANES], awg, awv, abg + fold(dgate), abv + fold(dval)

        zero = jnp.zeros((SUBLANES, FFN_CB), F32)
        init = (nxg_ref[...], nxv_ref[...], (zero,) * 3, (zero,) * 3, zero, zero)
        nxg, nxv, awg, awv, abg, abv = lax.fori_loop(0, nstrips, step, init)
        nxg_ref[...] = nxg
        nxv_ref[...] = nxv
        for k in range(3):
            dwg_ref[k:k + 1, :] += jnp.sum(awg[k], axis=0, keepdims=True)
            dwv_ref[k:k + 1, :] += jnp.sum(awv[k], axis=0, keepdims=True)
        dbg_ref[...] += jnp.sum(abg, axis=0, keepdims=True)
        dbv_ref[...] += jnp.sum(abv, axis=0, keepdims=True)

    def rev(i):
        return nt - 1 - i

    tile = pl.BlockSpec((ts, FFN_CB), lambda j, i: (rev(i), j))
    w_out = pl.BlockSpec((3, FFN_CB), lambda j, i: (0, j))
    b_out = pl.BlockSpec((1, FFN_CB), lambda j, i: (0, j))
    f = jax.ShapeDtypeStruct
    return pl.pallas_call(
        body, name=name, grid=(nblk, nt),
        in_specs=[tile] + _ffn_specs(ts, rev, lambda i: jnp.maximum(rev(i) * per - 1, 0)),
        out_specs=(tile, tile, w_out, w_out, b_out, b_out),
        out_shape=(f((s, D_FF), BF16), f((s, D_FF), BF16), f((3, D_FF), F32), f((3, D_FF), F32),
                   f((1, D_FF), F32), f((1, D_FF), F32)),
        scratch_shapes=[pltpu.VMEM((SUBLANES, FFN_CB), F32), pltpu.VMEM((SUBLANES, FFN_CB), F32),
                        pltpu.VMEM((ts + halo, FFN_CB), BF16), pltpu.VMEM((ts + halo, FFN_CB), BF16)],
    )(dact, up, up, w, b, up, up, w, b)


def _sgu_forward_block(zu, zg, gn, w_ref, bias, seg):
    u, tu = _gelu(zu)
    g, tg = _gelu(zg)
    ms = _dot_split(g * g, seg)
    rs = lax.rsqrt(ms + EPS)
    ghat = g * rs
    gv = ghat * gn
    gvb = gv.astype(BF16)
    lane = _lanes((CHUNK, LANES))
    chunks = []
    for c in range(zu.shape[0] // CHUNK):
        gc = gvb[c * CHUNK:(c + 1) * CHUNK]
        mix = jnp.where(lane < 64, _dot(w_ref[0], gc), _dot(w_ref[1], gc)) + bias
        chunks.append(mix)
    mixed = chunks[0] if len(chunks) == 1 else jnp.concatenate(chunks, axis=0)
    return u, tu, g, tg, rs, ghat, gvb, mixed


def _sgu_fwd(p1, gn, w, bias, seg, *, ts=512, name):
    s = p1.shape[0]
    ts = _tile_rows(ts, s)

    def body(zu_ref, zg_ref, gn_ref, w_ref, bias_ref, seg_ref, yc_ref):
        u, _, _, _, _, _, _, mixed = _sgu_forward_block(
            zu_ref[...], zg_ref[...], gn_ref[...], w_ref, bias_ref[...], seg_ref[...])
        yc_ref[...] = (u * mixed).astype(BF16)

    return pl.pallas_call(
        body, name=name, grid=(4, s // ts),
        in_specs=[pl.BlockSpec((ts, LANES), lambda j, i: (i, j)),
                  pl.BlockSpec((ts, LANES), lambda j, i: (i, 4 + j)),
                  pl.BlockSpec((1, LANES), lambda j, i: (0, j)),
                  pl.BlockSpec((2, CHUNK, CHUNK), lambda j, i: (j, 0, 0)),
                  pl.BlockSpec((CHUNK, LANES), lambda j, i: (0, j)),
                  pl.BlockSpec((LANES, LANES), lambda j, i: (0, 0))],
        out_specs=pl.BlockSpec((ts, LANES), lambda j, i: (i, j)),
        out_shape=jax.ShapeDtypeStruct((s, 4 * LANES), BF16),
    )(p1, p1, gn, w, bias, seg)


def _sgu_bwd(p1, dy, gn, w, wt, bias, seg, tril, *, ts=512, name):
    s = p1.shape[0]
    ts = _tile_rows(ts, s)
    nt = s // ts

    def body(zu_ref, zg_ref, dy_ref, gn_ref, w_ref, wt_ref, bias_ref, seg_ref, tril_ref,
             dzu_ref, dzg_ref, dw_ref, dbias_ref, dgn_ref):
        i = pl.program_id(1)
        zu = zu_ref[...]
        zg = zg_ref[...]
        gn_v = gn_ref[...]
        segv = seg_ref[...]
        u, tu, g, tg, rs, ghat, gvb, mixed = _sgu_forward_block(zu, zg, gn_v, w_ref, bias_ref[...], segv)
        dyv = dy_ref[...]
        du = dyv * mixed
        dmx = dyv * u

        @pl.when(i == 0)
        def _():
            dw_ref[...] = jnp.zeros_like(dw_ref)
            dbias_ref[...] = jnp.zeros_like(dbias_ref)
            dgn_ref[...] = jnp.zeros_like(dgn_ref)

        lane = _lanes((CHUNK, LANES))
        dgv_chunks = []
        dbias = jnp.zeros((CHUNK, LANES), F32)
        for c in range(ts // CHUNK):
            dmc = dmx[c * CHUNK:(c + 1) * CHUNK]
            gc = gvb[c * CHUNK:(c + 1) * CHUNK]
            dm_a = jnp.where(lane < 64, dmc, 0.0).astype(BF16)
            dm_b = jnp.where(lane >= 64, dmc, 0.0).astype(BF16)
            dw_ref[0] += _dot_nt(dm_a, gc)
            dw_ref[1] += _dot_nt(dm_b, gc)
            dgv_chunks.append(_dot(wt_ref[0], dm_a) + _dot(wt_ref[1], dm_b))
            dbias = dbias + dmc
        dbias_ref[...] += dbias
        dgv = dgv_chunks[0] if len(dgv_chunks) == 1 else jnp.concatenate(dgv_chunks, axis=0)
        dgn_ref[...] += jnp.sum(dgv * ghat, axis=0, keepdims=True)
        dgh = dgv * gn_v
        dg = rs * (dgh - ghat * _dot_split(dgh * ghat, segv))
        dzu_ref[...] = (du * _gelu_grad(zu, tu)).astype(BF16)
        dzg_ref[...] = (dg * _gelu_grad(zg, tg)).astype(BF16)

        @pl.when(i == nt - 1)
        def _():
            dw_ref[0] = dw_ref[0] * tril_ref[...]
            dw_ref[1] = dw_ref[1] * tril_ref[...]

    f = jax.ShapeDtypeStruct
    colj = pl.BlockSpec((ts, LANES), lambda j, i: (i, j))
    wsp = pl.BlockSpec((2, CHUNK, CHUNK), lambda j, i: (j, 0, 0))
    sq = pl.BlockSpec((LANES, LANES), lambda j, i: (0, 0))
    return pl.pallas_call(
        body, name=name, grid=(4, nt),
        in_specs=[colj, pl.BlockSpec((ts, LANES), lambda j, i: (i, 4 + j)), colj,
                  pl.BlockSpec((1, LANES), lambda j, i: (0, j)), wsp, wsp,
                  pl.BlockSpec((CHUNK, LANES), lambda j, i: (0, j)), sq, sq],
        out_specs=(colj, colj, wsp, pl.BlockSpec((CHUNK, LANES), lambda j, i: (0, j)),
                   pl.BlockSpec((1, LANES), lambda j, i: (0, j))),
        out_shape=(f((s, 4 * LANES), BF16), f((s, 4 * LANES), BF16), f((8, CHUNK, CHUNK), F32),
                   f((CHUNK, 4 * LANES), F32), f((1, 4 * LANES), F32)),
    )(p1, p1, dy, gn, w, wt, bias, seg, tril)


F_COL = 20


def _fcum_fwd(p1, bf, *, ts=512, name):
    s = p1.shape[0]
    ts = _tile_rows(ts, s)

    def body(f_ref, bf_ref, c_ref, car_ref):
        i = pl.program_id(0)
        z = f_ref[...] + bf_ref[...]
        logf = jnp.minimum(z, 0.0) - _log1p_pos(jnp.exp(-jnp.abs(z)))

        @pl.when(i == 0)
        def _():
            car_ref[...] = jnp.zeros_like(car_ref)

        c_ref[...] = _cumsum_fwd(logf) + car_ref[0:1, :]
        car_ref[0:1, :] = c_ref[ts - 1:ts, :]

    return pl.pallas_call(
        body, name=name, grid=(s // ts,),
        in_specs=[pl.BlockSpec((ts, LANES), lambda i: (i, F_COL)), pl.BlockSpec((1, LANES), lambda i: (0, 0))],
        out_specs=pl.BlockSpec((ts, LANES), lambda i: (i, 0)),
        out_shape=jax.ShapeDtypeStruct((s, LANES), F32),
        scratch_shapes=[pltpu.VMEM((SUBLANES, LANES), F32)],
    )(p1, bf)


def _fcum_bwd(dcs, dcq, p1, bf, *, ts=512, name):
    s = p1.shape[0]
    ts = _tile_rows(ts, s)
    nt = s // ts

    def body(dc_ref, dcq_ref, f_ref, bf_ref, df_ref, dbf_ref, car_ref):
        i = pl.program_id(0)

        @pl.when(i == 0)
        def _():
            car_ref[...] = jnp.zeros_like(car_ref)
            dbf_ref[...] = jnp.zeros_like(dbf_ref)

        dc = dc_ref[...]
        lane = _lanes((ts, LANES))
        for h in range(8):
            dc = dc + jnp.where(lane == h, dcq_ref[:, h * LANES:(h + 1) * LANES], 0.0)
        dlog = _cumsum_rev(dc) + car_ref[0:1, :]
        car_ref[...] = dlog[:SUBLANES]
        z = f_ref[...] + bf_ref[...]
        df = dlog * _sigmoid(-z)
        df_ref[...] = df.astype(BF16)
        dbf_ref[...] += jnp.sum(df, axis=0, keepdims=True)

    return pl.pallas_call(
        body, name=name, grid=(nt,),
        in_specs=[pl.BlockSpec((ts, LANES), lambda i: (nt - 1 - i, 0)),
                  pl.BlockSpec((ts, 8 * LANES), lambda i: (nt - 1 - i, 0)),
                  pl.BlockSpec((ts, LANES), lambda i: (nt - 1 - i, F_COL)),
                  pl.BlockSpec((1, LANES), lambda i: (0, 0))],
        out_specs=(pl.BlockSpec((ts, LANES), lambda i: (nt - 1 - i, 0)), pl.BlockSpec((1, LANES), lambda i: (0, 0))),
        out_shape=(jax.ShapeDtypeStruct((s, LANES), BF16), jax.ShapeDtypeStruct((1, LANES), F32)),
        scratch_shapes=[pltpu.VMEM((SUBLANES, LANES), F32)],
    )(dcs, dcq, p1, bf)


def _fox_scores(qm, kb, bias, ck, diagonal):
    sc = _dot_nt(qm, kb) + bias - ck
    if diagonal:
        sc = jnp.where(_lanes(sc.shape) <= _rows(sc.shape), sc, NEG)
    return sc


def _head_masks(shape):
    lane = _lanes(shape)
    return lane < 64, lane >= 64


def _split3(x):
    hi = x.astype(BF16).astype(F32)
    rest = x - hi
    mid = rest.astype(BF16).astype(F32)
    return hi, mid, (rest - mid).astype(BF16).astype(F32)


def _with_extra_lanes(base, head, values):
    lane = _lanes(base.shape)
    sel = _head_masks(base.shape)[head]
    off = 64 * (1 - head)
    out = jnp.where(sel, base, 0.0)
    for n, val in enumerate(values):
        out = jnp.where(lane == off + n, val, out)
    return out.astype(BF16)


def _fox_fwd(p1, cq, *, tq=512, name):
    s = p1.shape[0]
    tq = _tile_rows(tq, s)
    tk = tq
    nq = s // tq

    def body(q_ref, k_ref, v_ref, cq_ref, cqf_ref, o_ref, lb_ref, kx_ref, sc_ref, pm_ref):
        qi = pl.program_id(1)
        first, second = _head_masks((tq, LANES))

        @pl.when(qi == 0)
        def _():
            for r0 in range(0, s, tq):
                kf = k_ref[r0:r0 + tq, :]
                for hh in range(2):
                    hi, mid, lo = _split3(cqf_ref[r0:r0 + tq, hh * LANES:(hh + 1) * LANES])
                    kx_ref[hh, r0:r0 + tq, :] = _with_extra_lanes(kf, hh, [1.0, 1.0, 1.0, -hi, -mid, -lo])

        q = q_ref[...] * 0.125
        cqs = [cq_ref[:, hh * LANES:(hh + 1) * LANES] for hh in range(2)]
        qxs = [_with_extra_lanes(q, hh, list(_split3(cqs[hh])) + [1.0, 1.0, 1.0]) for hh in range(2)]

        def step(kj, carry, diagonal):
            cols = pl.ds(pl.multiple_of(kj * tk, tk), tk)
            vb = v_ref[cols, :].astype(BF16)
            new, outs = [], []
            acc = carry[4]
            row = _rows((tq, LANES))
            lane = _lanes((tq, LANES))
            for hh in range(2):
                m_prev, l_prev = carry[2 * hh], carry[2 * hh + 1]
                sc_ref[hh] = _dot_nt(qxs[hh], kx_ref[hh, cols, :])

                def chunk(n, hh=hh):
                    sc = sc_ref[hh, :, n * LANES:(n + 1) * LANES]
                    return jnp.where(lane + n * LANES <= row, sc, NEG) if diagonal else sc

                top = m_prev
                for n in range(tk // LANES):
                    top = jnp.maximum(top, chunk(n))
                m_new = jnp.max(top, axis=1, keepdims=True) + jnp.zeros_like(top)
                total = jnp.zeros((tq, LANES), F32)
                for n in range(tk // LANES):
                    pm = jnp.exp(chunk(n) - m_new)
                    total = total + pm
                    pm_ref[hh, :, n * LANES:(n + 1) * LANES] = pm.astype(BF16)
                alpha = jnp.exp(m_prev - m_new)
                new += [m_new, alpha * l_prev + jnp.sum(total, axis=1, keepdims=True)]
                outs.append(acc * alpha + _dot(pm_ref[hh], vb))
            return tuple(new) + (jnp.where(first, outs[0], outs[1]),)

        zero = jnp.zeros((tq, LANES), F32)
        low = jnp.full((tq, LANES), NEG, F32)
        carry = lax.fori_loop(0, qi, lambda kj, c: step(kj, c, False), (low, zero, low, zero, zero))
        m0, l0, m1, l1, acc = step(qi, carry, True)
        o_ref[...] = (acc / jnp.where(first, l0, l1)).astype(BF16)
        lb_ref[:, 0:LANES] = cqs[0] - (m0 + jnp.log(l0))
        lb_ref[:, LANES:2 * LANES] = cqs[1] - (m1 + jnp.log(l1))

    return pl.pallas_call(
        body, name=name, grid=(4, nq),
        in_specs=[pl.BlockSpec((tq, LANES), lambda j, qi: (qi, 8 + j)),
                  pl.BlockSpec((s, LANES), lambda j, qi: (0, 12 + j)),
                  pl.BlockSpec((s, LANES), lambda j, qi: (0, 16 + j)),
                  pl.BlockSpec((tq, 2 * LANES), lambda j, qi: (qi, j)),
                  pl.BlockSpec((s, 2 * LANES), lambda j, qi: (0, j))],
        out_specs=(pl.BlockSpec((tq, LANES), lambda j, qi: (qi, j)),
                   pl.BlockSpec((tq, 2 * LANES), lambda j, qi: (qi, j))),
        out_shape=(jax.ShapeDtypeStruct((s, 4 * LANES), BF16), jax.ShapeDtypeStruct((s, 8 * LANES), F32)),
        scratch_shapes=[pltpu.VMEM((2, s, LANES), BF16), pltpu.VMEM((2, tq, tk), F32),
                        pltpu.VMEM((2, tq, tk), BF16)],
    )(p1, p1, p1, cq, cq)


def _fox_delta(dy, o, sel, *, ts=512, name):
    s = o.shape[0]
    ts = _tile_rows(ts, s)

    def body(do_ref, o_ref, sel_ref, d_ref):
        prod = do_ref[...] * o_ref[...].astype(F32)
        d_ref[:, 0:LANES] = _dot_split(prod, sel_ref[0])
        d_ref[:, LANES:2 * LANES] = _dot_split(prod, sel_ref[1])

    return pl.pallas_call(
        body, name=name, grid=(4, s // ts),
        in_specs=[pl.BlockSpec((ts, LANES), lambda j, i: (i, 4 + j)),
                  pl.BlockSpec((ts, LANES), lambda j, i: (i, j)),
                  pl.BlockSpec((2, LANES, LANES), lambda j, i: (0, 0, 0))],
        out_specs=pl.BlockSpec((ts, 2 * LANES), lambda j, i: (i, j)),
        out_shape=jax.ShapeDtypeStruct((s, 8 * LANES), F32),
    )(dy, o, sel)


def _fox_bwd(p1, dy, lb, delta, ck, *, tq=512, name):
    s = p1.shape[0]
    tq = _tile_rows(tq, s)
    tk = tq
    nq = s // tq

    def body(q_ref, k_ref, v_ref, do_ref, lb_ref, dl_ref, ck_ref,
             dq_ref, dk_ref, dv_ref, dck_ref, dcq_ref, dqa_ref, dra_ref):
        kj = pl.program_id(1)

        @pl.when(kj == 0)
        def _():
            dqa_ref[...] = jnp.zeros_like(dqa_ref)
            dra_ref[...] = jnp.zeros_like(dra_ref)

        kf = k_ref[...]
        kb = kf.astype(BF16)
        vb = v_ref[...].astype(BF16)
        first, second = _head_masks((tk, LANES))
        kms = [jnp.where(sel, kf, 0.0).astype(BF16) for sel in (first, second)]
        cks = [ck_ref[hh] for hh in range(2)]

        def step(qi, carry, diagonal):
            dk_acc, dv_acc, dc0, dc1 = carry
            dcs = [dc0, dc1]
            rows = pl.ds(pl.multiple_of(qi * tq, tq), tq)
            q = q_ref[rows, :] * 0.125
            do = do_ref[rows, :]
            for hh, sel in enumerate((first, second)):
                qm = jnp.where(sel, q, 0.0).astype(BF16)
                dom = jnp.where(sel, do, 0.0).astype(BF16)
                bias = jnp.tile(lb_ref[rows, hh * LANES:(hh + 1) * LANES], (1, tk // LANES))
                pm = jnp.exp(_fox_scores(qm, kb, bias, cks[hh], diagonal))
                dv_acc = dv_acc + _dot_tn(pm.astype(BF16), dom)
                dp = _dot_nt(dom, vb)
                ds = pm * (dp - jnp.tile(dl_ref[rows, hh * LANES:(hh + 1) * LANES], (1, tk // LANES)))
                dsb = ds.astype(BF16)
                dk_acc = dk_acc + _dot_tn(dsb, qm)
                dcs[hh] = dcs[hh] - jnp.sum(ds, axis=0, keepdims=True)
                dqa_ref[rows, :] += _dot(dsb, kms[hh])
                dra_ref[hh, rows, :] += jnp.sum(ds, axis=1, keepdims=True)
            return dk_acc, dv_acc, dcs[0], dcs[1]

        zero = jnp.zeros((tk, LANES), F32)
        zrow = jnp.zeros((1, tk), F32)
        carry = step(kj, (zero, zero, zrow, zrow), True)
        dk_acc, dv_acc, dc0, dc1 = lax.fori_loop(kj + 1, nq, lambda qi, c: step(qi, c, False), carry)
        dk_ref[...] = dk_acc.astype(BF16)
        dv_ref[...] = dv_acc.astype(BF16)
        dck_ref[0] = dc0
        dck_ref[1] = dc1

        @pl.when(kj == nq - 1)
        def _():
            dq_ref[...] = (dqa_ref[...] * 0.125).astype(BF16)
            dcq_ref[:, 0:LANES] = dra_ref[0]
            dcq_ref[:, LANES:2 * LANES] = dra_ref[1]

    def full(width, col0):
        return pl.BlockSpec((s, width), lambda j, kj: (0, col0 + j))

    kblk = pl.BlockSpec((tk, LANES), lambda j, kj: (kj, j))
    f = jax.ShapeDtypeStruct
    return pl.pallas_call(
        body, name=name, grid=(4, nq),
        in_specs=[full(LANES, 8),
                  pl.BlockSpec((tk, LANES), lambda j, kj: (kj, 12 + j)),
                  pl.BlockSpec((tk, LANES), lambda j, kj: (kj, 16 + j)),
                  full(LANES, 4), full(2 * LANES, 0), full(2 * LANES, 0),
                  pl.BlockSpec((2, 1, tk), lambda j, kj: (j, 0, kj))],
        out_specs=(full(LANES, 0), kblk, kblk, pl.BlockSpec((2, 1, tk), lambda j, kj: (j, 0, kj)),
                   full(2 * LANES, 0)),
        out_shape=(f((s, 4 * LANES), BF16), f((s, 4 * LANES), BF16), f((s, 4 * LANES), BF16),
                   f((8, 1, s), F32), f((s, 8 * LANES), F32)),
        scratch_shapes=[pltpu.VMEM((s, LANES), F32), pltpu.VMEM((2, s, LANES), F32)],
    )(p1, p1, p1, dy, lb, delta, ck)


def _row_block(r, cap=256):
    best = None
    for rb in range(2 * SUBLANES, min(r, cap) + 1, 2 * SUBLANES):
        if r % rb == 0:
            best = rb
    return r if best is None else best


def _adamw_many(ws, gs, ms, vs, *, name):
    shapes = [a.shape for a in ws]

    def flat(a):
        return a.reshape((-1, a.shape[-1]))

    n = len(ws)
    operands = [flat(a) for group in (ws, gs, ms, vs) for a in group]

    def body(*refs):
        w_refs, g_refs, m_refs, v_refs = (refs[i * n:(i + 1) * n] for i in range(4))
        d_refs, nm_refs, nv_refs = (refs[(4 + i) * n:(5 + i) * n] for i in range(3))
        for p in range(n):
            gv = g_refs[p][...]
            mn = ADAM_B1 * m_refs[p][...] + (1.0 - ADAM_B1) * gv
            vn = ADAM_B2 * v_refs[p][...] + (1.0 - ADAM_B2) * (gv * gv)
            m_hat = mn / ADAM_C1
            v_hat = vn / ADAM_C2
            d_refs[p][...] = (-ADAM_LR) * (m_hat / (jnp.sqrt(v_hat) + ADAM_EPS) + ADAM_WD * w_refs[p][...])
            nm_refs[p][...] = mn
            nv_refs[p][...] = vn

    vm = pl.BlockSpec(memory_space=pltpu.VMEM)
    out_shape = [jax.ShapeDtypeStruct(flat(a).shape, F32) for a in ws] * 3
    outs = pl.pallas_call(
        body, name=name, in_specs=[vm] * (4 * n), out_specs=[vm] * (3 * n), out_shape=out_shape,
    )(*operands)
    return [tuple(outs[i * n + p].reshape(shapes[p]) for i in range(3)) for p in range(n)]


def _adamw_halves(w, mine, theirs, m, v, core, *, name):
    layers, r, c = w.shape
    rh = r // 2
    rb = _row_block(rh)
    per = rh // rb

    def body(core_ref, w_ref, *refs):
        g_refs = refs[:2 * layers]
        m_ref, v_ref, g_ref, d_ref, nm_ref, nv_ref = refs[2 * layers:]
        own = pl.program_id(1) == core_ref[0]
        gv = jnp.where(own, g_refs[0][...], g_refs[layers][...])
        for l in range(1, layers):
            gv = jnp.where(pl.program_id(0) == l, jnp.where(own, g_refs[l][...], g_refs[layers + l][...]), gv)
        g_ref[...] = gv
        mn = ADAM_B1 * m_ref[...] + (1.0 - ADAM_B1) * gv
        vn = ADAM_B2 * v_ref[...] + (1.0 - ADAM_B2) * (gv * gv)
        m_hat = mn / ADAM_C1
        v_hat = vn / ADAM_C2
        d_ref[...] = (-ADAM_LR) * (m_hat / (jnp.sqrt(v_hat) + ADAM_EPS) + ADAM_WD * w_ref[...])
        nm_ref[...] = mn
        nv_ref[...] = vn

    full = pl.BlockSpec((None, rb, c), lambda l, h, i, core_ref: (l, h * per + i, 0))
    half = pl.BlockSpec((rb, c), lambda l, h, i, core_ref: (i, 0))
    shp = jax.ShapeDtypeStruct((layers, r, c), F32)
    return pl.pallas_call(
        body, name=name,
        grid_spec=pltpu.PrefetchScalarGridSpec(
            num_scalar_prefetch=1, grid=(layers, 2, per),
            in_specs=[full] + [half] * (2 * layers) + [full, full], out_specs=(full,) * 4),
        out_shape=(shp,) * 4,
    )(core, w, *mine, *theirs, m, v)


def _pair_specs(col, rb, c):
    if col:
        g_spec = pl.BlockSpec((None, rb, c), lambda t, i, sel: (sel[0], i, sel[1 + t]))
    else:
        g_spec = pl.BlockSpec((None, None, rb, c), lambda t, i, sel: (sel[1 + t], sel[0], i, 0))
    return g_spec, pl.BlockSpec((None, rb, c), lambda t, i, sel: (sel[1 + t], i, 0))


def _pair_sum(g, col, ra, sel, after, *, name):
    _, rh, c = ra.shape
    rb = _row_block(rh)

    def body(sel_ref, g_ref, ra_ref, after_ref, h16_ref):
        h16_ref[...] = (g_ref[...] + ra_ref[...]).astype(BF16)

    g_spec, ra_spec = _pair_specs(col, rb, c)
    return pl.pallas_call(
        body, name=name,
        grid_spec=pltpu.PrefetchScalarGridSpec(
            num_scalar_prefetch=1, grid=(2, rh // rb), in_specs=[g_spec, ra_spec, ANY],
            out_specs=pl.BlockSpec((None, rb, c), lambda t, i, sel: (t, i, 0))),
        out_shape=jax.ShapeDtypeStruct((2, rh, c), BF16),
    )(sel, g, ra, after)


def _first_sum(g, col, ra, r1, sel, after, *, name):
    _, rh, c = ra.shape
    rb = _row_block(rh)

    def body(sel_ref, g_ref, ra_ref, r_ref, after_ref, s_ref, s16_ref):
        tot = (g_ref[...] + ra_ref[...]) + r_ref[...].astype(F32)
        s_ref[...] = tot
        s16_ref[...] = tot.astype(BF16)

    g_spec, ra_spec = _pair_specs(col, rb, c)
    slot = pl.BlockSpec((None, rb, c), lambda t, i, sel_ref: (t, i, 0))
    return pl.pallas_call(
        body, name=name,
        grid_spec=pltpu.PrefetchScalarGridSpec(
            num_scalar_prefetch=1, grid=(2, rh // rb), in_specs=[g_spec, ra_spec, slot, ANY],
            out_specs=(slot, slot)),
        out_shape=(jax.ShapeDtypeStruct((2, rh, c), F32), jax.ShapeDtypeStruct((2, rh, c), BF16)),
    )(sel, g, ra, r1, after)


def _second_sum(s1, r2, mine, after, *, name):
    _, rh, c = s1.shape
    rb = _row_block(rh)

    def body(mine_ref, s_ref, r_ref, after_ref, t_ref):
        t_ref[...] = s_ref[...] + r_ref[...].astype(F32)

    flat = pl.BlockSpec((rb, c), lambda i, mine_ref: (i, 0))
    return pl.pallas_call(
        body, name=name,
        grid_spec=pltpu.PrefetchScalarGridSpec(
            num_scalar_prefetch=1, grid=(rh // rb,),
            in_specs=[pl.BlockSpec((None, rb, c), lambda i, mine_ref: (mine_ref[0], i, 0)), flat, ANY],
            out_specs=flat),
        out_shape=jax.ShapeDtypeStruct((rh, c), F32),
    )(mine, s1, r2, after)


def _place(shards, layer, col, chip, dtype, *, name):
    _, r, c = shards.shape
    rh = r // 2
    rb = _row_block(rh)

    def body(chip_ref, s_ref, o_ref):
        o_ref[...] = s_ref[...].astype(o_ref.dtype)

    if col:
        out_spec = pl.BlockSpec((None, rb, c), lambda h, i, chip_ref: (h, i, chip_ref[0]))
        shape = (2, rh, N_CHIPS * c)
    else:
        out_spec = pl.BlockSpec((None, None, rb, c), lambda h, i, chip_ref: (chip_ref[0], h, i, 0))
        shape = (N_CHIPS, 2, rh, c)
    per = rh // rb
    return pl.pallas_call(
        body, name=name,
        grid_spec=pltpu.PrefetchScalarGridSpec(
            num_scalar_prefetch=1, grid=(2, per),
            in_specs=[pl.BlockSpec((None, rb, c), lambda h, i, chip_ref: (layer, h * per + i, 0))],
            out_specs=out_spec),
        out_shape=jax.ShapeDtypeStruct(shape, dtype),
    )(chip, shards)


ANY = pl.BlockSpec(memory_space=pl.ANY)


def _mesh_pos():
    return lax.axis_index("x"), lax.axis_index("y"), lax.axis_index("c")


def _other_chips(x, y):
    return [(1 - x, y), (x, 1 - y), (1 - x, 1 - y)]


def _remote(src, dst, ssem, rsem, dev):
    return pltpu.make_async_remote_copy(src_ref=src, dst_ref=dst, send_sem=ssem, recv_sem=rsem,
                                        device_id=dev, device_id_type=MESH)


def _flip(a, b):
    return a + b - 2 * a * b


def _handshake(peers):
    barrier = pltpu.get_barrier_semaphore()
    for peer in peers:
        pl.semaphore_signal(barrier, inc=1, device_id=peer, device_id_type=MESH)
    pl.semaphore_wait(barrier, len(peers))


def _slab(ref, col, width, k, h):
    if not col:
        return ref.at[k, h]
    start = k * width if isinstance(k, int) else pl.multiple_of(k * width, LANES)
    return ref.at[h, :, pl.ds(start, width)]


def _all_gather(bufs, cols, *, collective_id, name):
    n = len(bufs)
    widths = [b.shape[2] // N_CHIPS if col else b.shape[3] for b, col in zip(bufs, cols)]
    outs = [jax.new_ref(b, memory_space=pltpu.MemorySpace.HBM) for b in bufs]

    def body(ssem, rsem):
        x, y, c = _mesh_pos()
        me = 2 * x + y
        sib = (x, y, 1 - c)
        n1 = (_flip(x, 1 - c), _flip(y, c))
        n2 = (_flip(x, c), _flip(y, 1 - c))
        k1 = 2 * n1[0] + n1[1]
        k2 = 2 * n2[0] + n2[1]
        kd = 2 * (1 - x) + (1 - y)
        _handshake([n1 + (c,), n2 + (c,), sib])

        def slab(a, k, h):
            return _slab(outs[a], cols[a], widths[a], k, h)

        def copy(a, j, src, dst, dev):
            return _remote(src, dst, ssem.at[a, j], rsem.at[a, j], dev)

        sends = []
        for a in range(n):
            for j, nb in ((0, n1), (1, n2)):
                own = slab(a, me, c)
                cp = copy(a, j, own, own, nb + (c,))
                cp.start()
                sends.append(cp)
        arrivals = ((0, k1, n1, 3), (1, k2, n2, 4), (2, kd, n2, 5))
        for j, k, nb, fwd in arrivals:
            for a in range(n):
                got = slab(a, k, c)
                copy(a, j, got, got, nb + (c,)).wait_recv()
                if j == 0:
                    cp = copy(a, 2, got, got, n2 + (c,))
                    cp.start()
                    sends.append(cp)
                cp = copy(a, fwd, got, got, sib)
                cp.start()
                sends.append(cp)
        for fwd, k in ((3, k2), (4, k1), (5, kd)):
            for a in range(n):
                got = slab(a, k, 1 - c)
                copy(a, fwd, got, got, sib).wait_recv()
        for cp in sends:
            cp.wait_send()

    _sequencer_call(body, (), [(n, 6), (n, 6)], collective_id, name)()
    return [ref[...] for ref in outs]


def _sequencer_call(body, out_types, sem_shapes, collective_id, name):
    return pl.kernel(
        body, name=name, out_type=out_types,
        mesh=plsc.ScalarSubcoreMesh(axis_name="sequencer", num_cores=1),
        scratch_types=[pltpu.SemaphoreType.DMA(shape) for shape in sem_shapes],
        compiler_params=pltpu.CompilerParams(collective_id=collective_id))


def _send_other_half(grads, cols, *, collective_id, name):
    n = len(grads)

    def shard_shape(g, col):
        if col:
            return (g.shape[1], g.shape[2] // N_CHIPS)
        return g.shape[2:]

    shapes = [shard_shape(g, col) for g, col in zip(grads, cols)]

    def body(*refs):
        ins, outs = refs[:n], refs[n:2 * n]
        ssem, rsem = refs[2 * n:]
        x, y, c = _mesh_pos()
        sib = (x, y, 1 - c)
        _handshake([sib])
        sends = []
        for a in range(n):
            for k in range(N_CHIPS):
                src = _slab(ins[a], cols[a], shapes[a][1], k, 1 - c)
                cp = _remote(src, outs[a].at[k], ssem.at[a, k], rsem.at[a, k], sib)
                cp.start()
                sends.append(cp)
        for cp in sends:
            cp.wait()

    out_types = [jax.ShapeDtypeStruct((N_CHIPS,) + shp, g.dtype) for g, shp in zip(grads, shapes)]
    return _sequencer_call(body, out_types, [(n, N_CHIPS), (n, N_CHIPS)], collective_id, name)(*grads)


def _send_first(sums, *, collective_id, name):
    n = len(sums)

    def body(*refs):
        ins, outs = refs[:n], refs[n:2 * n]
        ssem, rsem = refs[2 * n:]
        x, y, c = _mesh_pos()
        nb = (_flip(x, c), _flip(y, 1 - c), c)
        _handshake([nb])
        sends = []
        for a in range(n):
            for t in range(2):
                cp = _remote(ins[a].at[t], outs[a].at[t], ssem.at[a, t], rsem.at[a, t], nb)
                cp.start()
                sends.append(cp)
        for cp in sends:
            cp.wait()

    out_types = [jax.ShapeDtypeStruct(h.shape, h.dtype) for h in sums]
    return _sequencer_call(body, out_types, [(n, 2), (n, 2)], collective_id, name)(*sums)


def _send_second(sums, *, collective_id, name):
    n = len(sums)

    def body(*refs):
        ins, outs = refs[:n], refs[n:2 * n]
        ssem, rsem = refs[2 * n:]
        x, y, c = _mesh_pos()
        nb = (_flip(x, 1 - c), _flip(y, c), c)
        other = 1 - (c * y + (1 - c) * x)
        _handshake([nb])
        sends = []
        for a in range(n):
            cp = _remote(ins[a].at[other], outs[a], ssem.at[a], rsem.at[a], nb)
            cp.start()
            sends.append(cp)
        for cp in sends:
            cp.wait()

    out_types = [jax.ShapeDtypeStruct(s.shape[1:], s.dtype) for s in sums]
    return _sequencer_call(body, out_types, [(n,), (n,)], collective_id, name)(*sums)


def _swap_halves(halves, *, collective_id, name):
    n = len(halves)

    def body(*refs):
        ins, outs = refs[:n], refs[n:2 * n]
        ssem, rsem = refs[2 * n:]
        x, y, c = _mesh_pos()
        sib = (x, y, 1 - c)
        _handshake([sib])
        cps = []
        for a in range(n):
            cp = _remote(ins[a], outs[a], ssem.at[a], rsem.at[a], sib)
            cp.start()
            cps.append(cp)
        for cp in cps:
            cp.wait()

    out_types = [jax.ShapeDtypeStruct(h.shape, h.dtype) for h in halves]
    return _sequencer_call(body, out_types, [(n,), (n,)], collective_id, name)(*halves)


def _all_reduce_small(buf, *, name):
    r = buf.shape[0]
    rh = r // 2

    def body(in_ref, out_ref, x1_ref, x2_ref, ssem, rsem):
        x, y, c = _mesh_pos()
        me = 2 * x + y
        sib = (x, y, 1 - c)
        chips = _other_chips(x, y)
        cp = _remote(in_ref, x1_ref, ssem.at[0], rsem.at[0], sib)
        cp.start()
        cp.wait()
        off = pl.multiple_of(c * rh, SUBLANES)
        x2_ref[me] = in_ref[pl.ds(off, rh), :] + x1_ref[pl.ds(off, rh), :]
        sends = []
        for j, (cx, cy) in enumerate(chips):
            s = _remote(x2_ref.at[me], x2_ref.at[me], ssem.at[1 + j], rsem.at[1 + j], (cx, cy, c))
            s.start()
            sends.append(s)
        for j, (cx, cy) in enumerate(chips):
            slot = x2_ref.at[2 * cx + cy]
            _remote(slot, slot, ssem.at[1 + j], rsem.at[1 + j], (cx, cy, c)).wait_recv()
        out_ref[pl.ds(off, rh), :] = ((x2_ref[0] + x2_ref[1]) + x2_ref[2]) + x2_ref[3]
        for s in sends:
            s.wait_send()
        mine = out_ref.at[pl.ds(off, rh), :]
        s3 = _remote(mine, mine, ssem.at[4], rsem.at[4], sib)
        s3.start()
        off2 = pl.multiple_of((1 - c) * rh, SUBLANES)
        theirs = out_ref.at[pl.ds(off2, rh), :]
        _remote(theirs, theirs, ssem.at[4], rsem.at[4], sib).wait_recv()
        s3.wait_send()

    vm = pl.BlockSpec(memory_space=pltpu.VMEM)
    return pl.pallas_call(
        body, name=name, in_specs=[vm], out_specs=vm,
        out_shape=jax.ShapeDtypeStruct((r, LANES), F32),
        scratch_shapes=[pltpu.VMEM((r, LANES), F32), pltpu.VMEM((N_CHIPS, rh, LANES), F32),
                        pltpu.SemaphoreType.DMA((5,)), pltpu.SemaphoreType.DMA((5,))],
    )(buf)


PACK_ALIGN = 2 * SUBLANES * LANES


def _pack(arrays):
    parts, offs, off = [], [], 0
    for a in arrays:
        flat = a.reshape(-1).astype(F32)
        padded = -(-flat.shape[0] // PACK_ALIGN) * PACK_ALIGN
        parts.append(jnp.pad(flat, (0, padded - flat.shape[0])))
        offs.append(off)
        off += padded
    buf = jnp.concatenate(parts).reshape(-1, LANES)
    return buf, offs


def _unpack(buf, offs, shapes):
    flat = buf.reshape(-1)
    out = []
    for off, shp in zip(offs, shapes):
        size = 1
        for d in shp:
            size *= d
        out.append(flat[off:off + size].reshape(shp))
    return out


def _cols_from_shards(g4):
    _, k, ns = g4.shape
    return jnp.transpose(g4, (1, 0, 2)).reshape(k, N_CHIPS * ns)


def _cols_to_shards(w):
    k, n = w.shape
    return jnp.transpose(w.reshape(k, N_CHIPS, n // N_CHIPS), (1, 0, 2))


def _pair_blockdiag(w8):
    w = w8.reshape(4, 2, 64, 64)
    z = jnp.zeros((4, 64, 64), w8.dtype)
    top = jnp.concatenate([w[:, 0], z], axis=2)
    bot = jnp.concatenate([z, w[:, 1]], axis=2)
    return jnp.concatenate([top, bot], axis=1)


def _pair_diag_blocks(w4):
    a = w4[:, :64, :64]
    b = w4[:, 64:, 64:]
    return jnp.stack([a, b], axis=1).reshape(8, 64, 64)


def _local_step(x, target, wts, on_event=None):
    s = x.shape[0]
    g = {}

    def event(name, token):
        if on_event is not None:
            on_event(name, g, token)

    win0 = wts["w_in0"]
    wout0 = wts["w_out0"]
    wout1 = wts["w_out1"]
    wup = wts["w_up"]
    wdown = wts["w_down"]
    w4, b4, w3, b3 = wts["w4"], wts["b4"], wts["w3"], wts["b3"]
    wa, wx = wts["wa"], wts["wx"]
    wat, wxt = jnp.swapaxes(wa, 1, 2), jnp.swapaxes(wx, 1, 2)
    ba, bx, lam = wts["ba"], wts["bx"], wts["lam"]
    fcw, fcb = wts["ffn_cw"], wts["ffn_cb"]
    sgu_w, sgu_wt = wts["sgu_w"], wts["sgu_wt"]
    sgu_bias, sgu_gn = wts["sgu_bias"], wts["sgu_gn"]
    bf = wts["bf"]

    lane = jnp.arange(LANES)
    seg = jnp.where((lane[:, None] // 64) == (lane[None, :] // 64), 1.0 / 64.0, 0.0).astype(BF16)
    sel = jnp.stack([jnp.broadcast_to((lane[:, None] < 64), (LANES, LANES)),
                     jnp.broadcast_to((lane[:, None] >= 64), (LANES, LANES))]).astype(BF16)
    tril = (lane[:, None] >= lane[None, :]).astype(F32)

    n0 = _norm_fwd(x, wts["g_mix0"], name="norm_mix0")
    p0 = _mm([n0], win0, nb=1280, name="mm_in0")
    ya, yb, hl = _even_core_fwd(p0, w4, b4, wa, ba, wx, bx, lam, w3, b3, name="even_fwd")
    h1, n1 = _mm([ya, yb], wout0, res=x, norm_out=wts["g_ffn"][0], name="mm_out0")

    def ffn_fwd(h, n, layer, next_gain):
        up = _mm([n], wup[layer], out_dtype=BF16, ts=1024, nb=1408, name=f"mm_up{layer}")
        act = _ffn_core_fwd(up, fcw[layer], fcb[layer], name=f"ffn_fwd{layer}")
        if next_gain is None:
            return up, act, _mm([act], wdown[layer], res=h, name=f"mm_down{layer}"), None
        hn, nn = _mm([act], wdown[layer], res=h, norm_out=next_gain, name=f"mm_down{layer}")
        return up, act, hn, nn

    up0, act0, h2, n2 = ffn_fwd(h1, n1, 0, wts["g_mix1"])

    win1 = wts["w_in1"](n2)
    p1 = _mm([n2], win1, name="mm_in1")
    yc = _sgu_fwd(p1, sgu_gn, sgu_w, sgu_bias, seg, name="sgu_fwd")
    cum = _fcum_fwd(p1, bf, name="fcum_fwd")
    c8 = cum[:, :8]
    cq = jnp.broadcast_to(c8[:, :, None], (s, 8, LANES)).reshape(s, 8 * LANES)
    ck = jnp.transpose(c8).reshape(8, 1, s)
    yd, lb = _fox_fwd(p1, cq, name="fox_fwd")
    h3, n3 = _mm([yc, yd], wout1, res=h2, norm_out=wts["g_ffn"][1], name="mm_out1")

    up1, act1, h4, _ = ffn_fwd(h3, n3, 1, None)
    dh4, loss, g["final_norm"] = _final(h4, wts["g_final"], target, name="final")

    def ffn_bwd(dh, h, n, up, act, layer):
        dact = _mm([dh], wdown[layer], trans_w=True, out_dtype=BF16, ts=1024, nb=1408, name=f"mm_dact{layer}")
        g[f"w_down{layer}"] = _mm_tn([act], [dh], ts=1024, nb=512, name=f"mm_dwdown{layer}")
        event(f"dwdown{layer}", g[f"w_down{layer}"])
        dgate, dval, dcwg, dcwv, dcbg, dcbv = _ffn_core_bwd(dact, up, fcw[layer], fcb[layer], name=f"ffn_bwd{layer}")
        event(f"ffn_bwd{layer}", dgate)
        g[f"w_up{layer}"] = _mm_tn([n], [dgate, dval], ts=1024, nb=1408, name=f"mm_dwup{layer}")
        event(f"dwup{layer}", g[f"w_up{layer}"])
        dhn, g[f"g_ffn{layer}"] = _mm([dgate, dval], wup[layer], trans_w=True, ts=512,
                                      norm_bwd=(h, wts["g_ffn"][layer], dh), name=f"mm_dn_ffn{layer}")
        g[f"ffn_cw{layer}"] = jnp.concatenate([dcwg, dcwv], axis=1)
        g[f"ffn_cb{layer}"] = jnp.concatenate([dcbg, dcbv], axis=1)
        return dhn

    dh3 = ffn_bwd(dh4, h3, n3, up1, act1, 1)

    dy1 = _mm([dh3], wout1, trans_w=True, ts=1024, name="mm_dy1")
    g["w_out1"] = _mm_tn([yc, yd], [dh3], ts=1024, nb=512, name="mm_dwout1")
    event("dwout1", g["w_out1"])
    dzu, dzg, g["sgu_w"], g["sgu_bias"], g["sgu_gn"] = _sgu_bwd(
        p1, dy1, sgu_gn, sgu_w, sgu_wt, sgu_bias, seg, tril, name="sgu_bwd")
    delta = _fox_delta(dy1, yd, sel, name="fox_delta")
    dq, dk, dv, dck, dcq = _fox_bwd(p1, dy1, lb, delta, ck, name="fox_bwd")
    event("fox_bwd", dq)
    dcs = jnp.pad(jnp.transpose(dck.reshape(8, s)), ((0, 0), (0, LANES - 8)))
    df, g["bf"] = _fcum_bwd(dcs, dcq, p1, bf, name="fcum_bwd")
    dp1 = jnp.concatenate([dzu, dzg, dq, dk, dv, df], axis=1)
    g["w_in1"] = _mm_tn([n2], [dp1], ts=1024, nb=896, name="mm_dwin1")
    event("dwin1", g["w_in1"])
    dh2, g["g_mix1"] = _mm([dp1], win1, trans_w=True, norm_bwd=(h2, wts["g_mix1"], dh3), name="mm_dn_mix1")

    dh1 = ffn_bwd(dh2, h1, n1, up0, act0, 0)

    dy0 = _mm([dh1], wout0, trans_w=True, ts=1024, name="mm_dy0")
    g["w_out0"] = _mm_tn([ya, yb], [dh1], ts=1024, nb=512, name="mm_dwout0")
    event("dwout0", g["w_out0"])
    (*dp0, g["w4"], g["b4"], g["wa"], g["ba"], g["wx"], g["bx"], g["lam"], g["w3"], g["b3"]) = _even_core_bwd(
        dy0, p0, hl, w4, b4, wa, wat, ba, wx, wxt, bx, lam, w3, b3, name="even_bwd")
    event("even_bwd", dp0[0])
    g["w_in0"] = _mm_tn([n0], dp0, ts=1024, nb=512, name="mm_dwin0")
    event("dwin0", g["w_in0"])
    grad_x, g["g_mix0"] = _mm(dp0, win0, trans_w=True, norm_bwd=(x, wts["g_mix0"], dh1), name="mm_dn_mix0")
    return loss, grad_x, g


def _late_cols_from_shards(stacked, token):
    stacked, _ = lax.optimization_barrier((stacked, token))
    return _cols_from_shards(stacked.reshape((N_CHIPS, stacked.shape[1] * stacked.shape[2], stacked.shape[3])))


def _prepare_weights(nat):
    lane = jnp.arange(LANES)
    tril = (lane[:, None] >= lane[None, :]).astype(F32)
    sgu_tril = nat["sgu_w"][0] * tril

    def w_in1(token):
        full = nat["mix1_w_in"](token) if callable(nat["mix1_w_in"]) else nat["mix1_w_in"]
        return jnp.pad(full, ((0, 0), (0, 21 * LANES - full.shape[1])))

    return {
        "w_in0": nat["mix0_w_in"],
        "w_out0": nat["mix0_w_out"],
        "w_in1": w_in1,
        "w_out1": nat["mix1_w_out"],
        "w_up": [nat["ffn_up"][l] for l in range(2)],
        "w_down": [nat["ffn_down"][l] for l in range(2)],
        "w4": nat["lru_conv_w"], "b4": nat["lru_conv_b"], "w3": nat["sconv_w"], "b3": nat["sconv_b"],
        "wa": _pair_blockdiag(nat["lru_wa"][0]).astype(BF16), "wx": _pair_blockdiag(nat["lru_wx"][0]).astype(BF16),
        "ba": nat["lru_ba"], "bx": nat["lru_bx"], "lam": nat["lru_lambda"],
        "ffn_cw": [nat["ffn_conv_w"][l] for l in range(2)],
        "ffn_cb": [nat["ffn_conv_b"][l:l + 1] for l in range(2)],
        "sgu_w": sgu_tril.astype(BF16), "sgu_wt": jnp.swapaxes(sgu_tril, 1, 2).astype(BF16),
        "sgu_bias": jnp.repeat(jnp.transpose(nat["sgu_b"][0]), 64, axis=1), "sgu_gn": nat["sgu_norm"],
        "bf": jnp.pad(nat["fox_bf"], ((0, 0), (0, LANES - 8))),
        "g_mix0": nat["mix0_norm"], "g_mix1": nat["mix1_norm"],
        "g_ffn": [nat["ffn_norm"][0:1], nat["ffn_norm"][1:2]], "g_final": nat["final_norm"].reshape(1, D_MODEL),
    }


def _natural_grads(g):
    small = {
        "mix0_norm": g["g_mix0"], "lru_conv_b": g["b4"],
        "lru_wa": _pair_diag_blocks(g["wa"])[None], "lru_ba": g["ba"],
        "lru_wx": _pair_diag_blocks(g["wx"])[None], "lru_bx": g["bx"],
        "lru_lambda": g["lam"], "sconv_b": g["b3"],
        "sgu_w": g["sgu_w"][None],
        "sgu_b": jnp.transpose(g["sgu_bias"].reshape(CHUNK, 8, 64).sum(axis=2))[None],
        "fox_bf": g["bf"][:, :8],
        "ffn_norm": jnp.concatenate([g["g_ffn0"], g["g_ffn1"]], axis=0),
        "ffn_conv_b": jnp.concatenate([g["ffn_cb0"], g["ffn_cb1"]], axis=0),
        "final_norm": g["final_norm"].reshape(D_MODEL),
        "lru_conv_w": g["w4"][None], "sconv_w": g["w3"][None],
        "ffn_conv_w": jnp.stack([g["ffn_cw0"], g["ffn_cw1"]]),
        "mix1_norm": g["g_mix1"], "sgu_norm": g["sgu_gn"],
    }
    big = {
        "mix0_w_in": g["w_in0"], "mix0_w_out": g["w_out0"],
        "mix1_w_in": g["w_in1"][:, :2568], "mix1_w_out": g["w_out1"],
        "ffn_up0": g["w_up0"], "ffn_up1": g["w_up1"],
        "ffn_down0": g["w_down0"], "ffn_down1": g["w_down1"],
    }
    return small, big


COL_SHARDED = ("mix0_w_in", "mix1_w_in", "ffn_up0", "ffn_up1")
COL_ALIGNED = ("mix0_w_in", "ffn_up0", "ffn_up1")
SMALL_SHARDED = ("lru_conv_w", "sconv_w", "ffn_conv_w", "mix1_norm", "sgu_norm")
SMALL_REPLICATED = ("mix0_norm", "lru_conv_b", "lru_wa", "lru_ba", "lru_wx", "lru_bx", "lru_lambda", "sconv_b",
                    "sgu_w", "sgu_b", "fox_bf", "ffn_norm", "ffn_conv_b", "final_norm")
WEIGHT_ORDER = ("mix0_norm", "mix0_w_in", "lru_conv_w", "lru_conv_b", "lru_wa", "lru_ba", "lru_wx", "lru_bx",
                "lru_lambda", "sconv_w", "sconv_b", "mix0_w_out", "mix1_norm", "mix1_w_in", "sgu_norm", "sgu_w",
                "sgu_b", "fox_bf", "mix1_w_out", "ffn_norm", "ffn_up", "ffn_conv_w", "ffn_conv_b", "ffn_down",
                "final_norm")


GATHER_GROUPS = (("mix0_w_in", "mix0_w_out"), ("ffn_up0",), ("ffn_down0", "mix1_w_in"),
                 ("mix1_w_out", "ffn_up1", "ffn_down1"))
CID_GATHER, CID_PAIR, CID_FIRST, CID_SECOND, CID_SWAP = 1, 2, 3, 4, 5


class _GradReducer:
    def __init__(self):
        x, y, c = _mesh_pos()
        self.send = jnp.stack([c] + [2 * (c * (1 - x) + (1 - c) * t) + (c * t + (1 - c) * (1 - y))
                                     for t in range(2)]).astype(jnp.int32)
        self.keep = jnp.stack([c] + [c * (2 * x + t) + (1 - c) * (2 * t + y) for t in range(2)]).astype(jnp.int32)
        self.mine = (c * y + (1 - c) * x).reshape(1).astype(jnp.int32)
        self.groups = {}

    @staticmethod
    def _view(name, a):
        if name in COL_ALIGNED:
            return a.reshape(2, a.shape[0] // 2, a.shape[1])
        if name in COL_SHARDED:
            a = _cols_to_shards(a)
            return a.reshape(N_CHIPS, 2, a.shape[1] // 2, a.shape[2])
        rows = a.shape[0] // (2 * N_CHIPS)
        return a.reshape(N_CHIPS, 2, rows, a.shape[1])

    def start(self, group, grads):
        names = tuple(grads)
        views = [self._view(k, grads[k]) for k in names]
        cols = [k in COL_ALIGNED for k in names]
        data = _send_other_half(views, cols, collective_id=CID_PAIR, name=f"rs_pair_{group}")
        self.groups[group] = dict(names=names, stage=0, views=views, cols=cols, data=data)

    def step(self, group, after):
        st = self.groups[group]
        names = st["names"]
        if st["stage"] == 0:
            sums = [_pair_sum(a, col, b, self.send, after, name=f"rs_pair_sum_{k}")
                    for k, a, col, b in zip(names, st["views"], st["cols"], st["data"])]
            st["from_sib"] = st["data"]
            st["data"] = _send_first(sums, collective_id=CID_FIRST, name=f"rs_first_{group}")
        elif st["stage"] == 1:
            sums = [_first_sum(a, col, b, r, self.keep, after, name=f"rs_first_sum_{k}")
                    for k, a, col, b, r in zip(names, st["views"], st["cols"], st["from_sib"], st["data"])]
            st["keep"] = [s32 for s32, _ in sums]
            st["data"] = _send_second([s16 for _, s16 in sums], collective_id=CID_SECOND, name=f"rs_second_{group}")
        else:
            st["mine"] = [_second_sum(s32, r, self.mine, after, name=f"rs_second_sum_{k}")
                          for k, s32, r in zip(names, st["keep"], st["data"])]
            st["data"] = _swap_halves(st["mine"], collective_id=CID_SWAP, name=f"rs_swap_{group}")
        st["stage"] += 1

    def result(self, group):
        st = self.groups[group]
        return {k: (a, b) for k, a, b in zip(st["names"], st["mine"], st["data"])}


def _train_step(x, target, w, m, v):
    x2 = x[0]
    t2 = target[0]
    chip = 2 * lax.axis_index("x") + lax.axis_index("y")
    core_arr = lax.axis_index("c").reshape(1).astype(jnp.int32)
    chip_arr = chip.reshape(1).astype(jnp.int32)

    big_shards = {
        "mix0_w_in": (w["mix0_w_in"], 0), "mix0_w_out": (w["mix0_w_out"], 0),
        "mix1_w_in": (w["mix1_w_in"], 0), "mix1_w_out": (w["mix1_w_out"], 0),
        "ffn_up0": (w["ffn_up"], 0), "ffn_up1": (w["ffn_up"], 1),
        "ffn_down0": (w["ffn_down"], 0), "ffn_down1": (w["ffn_down"], 1),
    }
    small_shards = [w[k] for k in SMALL_SHARDED]
    small_buf, small_offs = _pack(small_shards)
    full = {}
    small_all = None
    for gi, names in enumerate(GATHER_GROUPS):
        cols = [k in COL_ALIGNED for k in names]
        placed = [_place(*big_shards[k], col, chip_arr, BF16, name=f"place_{k}") for k, col in zip(names, cols)]
        if gi == 0:
            placed.append(_place(small_buf[None], 0, False, chip_arr, F32, name="place_small"))
            cols = cols + [False]
        gathered = _all_gather(placed, cols, collective_id=CID_GATHER, name=f"gather_weights{gi}")
        if gi == 0:
            small_all = gathered[-1].reshape(N_CHIPS, -1, LANES)
        for k, arr in zip(names, gathered):
            if k in COL_ALIGNED:
                full[k] = arr.reshape(arr.shape[0] * arr.shape[1], arr.shape[2])
            elif k in COL_SHARDED:
                full[k] = functools.partial(_late_cols_from_shards, arr)
            else:
                full[k] = arr.reshape(-1, arr.shape[3])
    per_chip = [_unpack(small_all[k], small_offs, [a.shape for a in small_shards]) for k in range(N_CHIPS)]
    lru_conv_w = jnp.concatenate([per_chip[k][0] for k in range(N_CHIPS)], axis=-1)[0]
    sconv_w = jnp.concatenate([per_chip[k][1] for k in range(N_CHIPS)], axis=-1)[0]
    ffn_conv_w = jnp.concatenate([per_chip[k][2] for k in range(N_CHIPS)], axis=-1)
    mix1_norm = jnp.concatenate([per_chip[k][3] for k in range(N_CHIPS)], axis=-1)
    sgu_norm = jnp.concatenate([per_chip[k][4] for k in range(N_CHIPS)], axis=-1)

    nat = {
        "mix0_w_in": full["mix0_w_in"], "mix0_w_out": full["mix0_w_out"],
        "mix1_w_in": full["mix1_w_in"], "mix1_w_out": full["mix1_w_out"],
        "ffn_up": [full["ffn_up0"], full["ffn_up1"]], "ffn_down": [full["ffn_down0"], full["ffn_down1"]],
        "lru_conv_w": lru_conv_w, "sconv_w": sconv_w, "ffn_conv_w": ffn_conv_w, "mix1_norm": mix1_norm,
        "sgu_norm": sgu_norm,
    }
    for k in SMALL_REPLICATED:
        nat[k] = w[k]
    wts = _prepare_weights(nat)

    reducer = _GradReducer()

    def on_event(name, g, token):
        if name == "dwup1":
            reducer.start("ffn1", {"ffn_up1": g["w_up1"], "ffn_down1": g["w_down1"]})
        elif name in ("dwout1", "fox_bwd"):
            reducer.step("ffn1", token)
        elif name == "dwin1":
            reducer.step("ffn1", token)
            reducer.start("mix1", {"mix1_w_in": g["w_in1"][:, :2568], "mix1_w_out": g["w_out1"]})
        elif name in ("dwdown0", "ffn_bwd0"):
            reducer.step("mix1", token)
        elif name == "dwup0":
            reducer.step("mix1", token)
            reducer.start("ffn0", {"ffn_up0": g["w_up0"], "ffn_down0": g["w_down0"]})
        elif name in ("dwout0", "even_bwd"):
            reducer.step("ffn0", token)
        elif name == "dwin0":
            reducer.step("ffn0", token)
            reducer.start("mix0", {"mix0_w_in": g["w_in0"], "mix0_w_out": g["w_out0"]})

    loss, grad_x, g = _local_step(x2, t2, wts, on_event)
    grads_small, _ = _natural_grads(g)

    small_names = SMALL_REPLICATED + SMALL_SHARDED
    small_list = [grads_small[k] for k in small_names] + [loss[:, :1]]
    sbuf, soffs = _pack(small_list)
    sred = _all_reduce_small(sbuf, name="reduce_small")
    small_red = _unpack(sred, soffs, [a.shape for a in small_list])
    loss_total = small_red[-1][0, 0]
    gsum = dict(zip(small_names, small_red[:-1]))
    for k in SMALL_SHARDED:
        width = w[k].shape[-1]
        gsum[k] = lax.dynamic_slice_in_dim(gsum[k], chip * width, width, axis=gsum[k].ndim - 1)

    out_g, out_d, out_m, out_v = {}, {}, {}, {}
    reduced = {}
    for group in ("ffn1", "mix1", "ffn0"):
        reduced.update(reducer.result(group))

    def update(pname, keys):
        mine = [reduced[k][0] for k in keys]
        theirs = [reduced[k][1] for k in keys]
        out_g[pname], out_d[pname], out_m[pname], out_v[pname] = _adamw_halves(
            w[pname], mine, theirs, m[pname], v[pname], core_arr, name=f"adamw_{pname}")
        return out_d[pname]

    reducer.step("mix0", update("ffn_up", ("ffn_up0", "ffn_up1")))
    small_new = _adamw_many([w[k] for k in small_names], [gsum[k] for k in small_names],
                            [m[k] for k in small_names], [v[k] for k in small_names], name="adamw_small")
    reducer.step("mix0", update("ffn_down", ("ffn_down0", "ffn_down1")))
    update("mix1_w_in", ("mix1_w_in",))
    reducer.step("mix0", update("mix1_w_out", ("mix1_w_out",)))
    reduced.update(reducer.result("mix0"))
    update("mix0_w_in", ("mix0_w_in",))
    update("mix0_w_out", ("mix0_w_out",))

    for k, (dd, mm, vv) in zip(small_names, small_new):
        out_g[k], out_d[k], out_m[k], out_v[k] = gsum[k].reshape(w[k].shape), dd, mm, vv

    outs = [loss_total, grad_x[None]]
    for d in (out_g, out_d, out_m, out_v):
        outs.extend(d[k] for k in WEIGHT_ORDER)
    return tuple(outs)


def kernel(x, mix0_norm, mix0_w_in, lru_conv_w, lru_conv_b, lru_wa, lru_ba, lru_wx, lru_bx, lru_lambda, sconv_w, sconv_b, mix0_w_out, mix1_norm, mix1_w_in, sgu_norm, sgu_w, sgu_b, fox_bf, mix1_w_out, ffn_norm, ffn_up, ffn_conv_w, ffn_conv_b, ffn_down, final_norm, loss_target, m_mix0_norm, m_mix0_w_in, m_lru_conv_w, m_lru_conv_b, m_lru_wa, m_lru_ba, m_lru_wx, m_lru_bx, m_lru_lambda, m_sconv_w, m_sconv_b, m_mix0_w_out, m_mix1_norm, m_mix1_w_in, m_sgu_norm, m_sgu_w, m_sgu_b, m_fox_bf, m_mix1_w_out, m_ffn_norm, m_ffn_up, m_ffn_conv_w, m_ffn_conv_b, m_ffn_down, m_final_norm, v_mix0_norm, v_mix0_w_in, v_lru_conv_w, v_lru_conv_b, v_lru_wa, v_lru_ba, v_lru_wx, v_lru_bx, v_lru_lambda, v_sconv_w, v_sconv_b, v_mix0_w_out, v_mix1_norm, v_mix1_w_in, v_sgu_norm, v_sgu_w, v_sgu_b, v_fox_bf, v_mix1_w_out, v_ffn_norm, v_ffn_up, v_ffn_conv_w, v_ffn_conv_b, v_ffn_down, v_final_norm):
    w = dict(zip(WEIGHT_ORDER, (mix0_norm, mix0_w_in, lru_conv_w, lru_conv_b, lru_wa, lru_ba, lru_wx, lru_bx, lru_lambda, sconv_w, sconv_b, mix0_w_out, mix1_norm, mix1_w_in, sgu_norm, sgu_w, sgu_b, fox_bf, mix1_w_out, ffn_norm, ffn_up, ffn_conv_w, ffn_conv_b, ffn_down, final_norm)))
    m = dict(zip(WEIGHT_ORDER, (m_mix0_norm, m_mix0_w_in, m_lru_conv_w, m_lru_conv_b, m_lru_wa, m_lru_ba, m_lru_wx, m_lru_bx, m_lru_lambda, m_sconv_w, m_sconv_b, m_mix0_w_out, m_mix1_norm, m_mix1_w_in, m_sgu_norm, m_sgu_w, m_sgu_b, m_fox_bf, m_mix1_w_out, m_ffn_norm, m_ffn_up, m_ffn_conv_w, m_ffn_conv_b, m_ffn_down, m_final_norm)))
    v = dict(zip(WEIGHT_ORDER, (v_mix0_norm, v_mix0_w_in, v_lru_conv_w, v_lru_conv_b, v_lru_wa, v_lru_ba, v_lru_wx, v_lru_bx, v_lru_lambda, v_sconv_w, v_sconv_b, v_mix0_w_out, v_mix1_norm, v_mix1_w_in, v_sgu_norm, v_sgu_w, v_sgu_b, v_fox_bf, v_mix1_w_out, v_ffn_norm, v_ffn_up, v_ffn_conv_w, v_ffn_conv_b, v_ffn_down, v_final_norm)))
    return _train_step(x, loss_target, w, m, v)
```
